```python
import jax, jax.numpy as jnp
from jax import lax
import numpy as np

D_MODEL = 2048
BATCH = 8
SEQ = 4096
DEPTH = 4

N_EVEN = (DEPTH + 1) // 2
N_ODD = DEPTH // 2
EPS = 1e-6
D_FF = 4 * D_MODEL
GROUP_DIM = 128
D_A = D_MODEL // 2
D_B = D_MODEL - D_A
N_A_GROUPS = D_A // GROUP_DIM
CHUNK = 128
CONV_WIDTH = 31
D_AB_IN = 2 * D_A + 2 * D_B
N_HEADS = 16
Q_RANK = 512
KV_RANK = 512
NOPE_DIM = 128
ROPE_DIM = 64
V_DIM = 128
QK_DIM = NOPE_DIM + ROPE_DIM
D_MLA_IN = Q_RANK + KV_RANK + ROPE_DIM
ROPE_THETA = 10000.0
Q_BLOCK = 128
ATTN_SCALE = QK_DIM ** -0.5

kernel_name = "hybrid_sgu_conv_mla_adaln_trunk"


def _rms(x, g):
    xf = x.astype(jnp.float32)
    y = xf * lax.rsqrt(jnp.mean(xf * xf, axis=-1, keepdims=True) + EPS)
    return (y * g.astype(jnp.float32)).astype(x.dtype)


def _layernorm(x, g, b):
    xf = x.astype(jnp.float32)
    mu = jnp.mean(xf, axis=-1, keepdims=True)
    var = jnp.mean(jnp.square(xf - mu), axis=-1, keepdims=True)
    y = (xf - mu) * lax.rsqrt(var + EPS)
    return (y * g.astype(jnp.float32) + b.astype(jnp.float32)).astype(x.dtype)


def _modulate(h, shift, scale):
    return h * (1 + scale[:, None, :]) + shift[:, None, :]


def _spatial_gating(u, v, v_norm_g, w_s, b_s):
    bsz, s, _ = u.shape
    shp = (bsz, s // CHUNK, CHUNK, N_A_GROUPS, GROUP_DIM)
    v = _rms(v.reshape(shp), v_norm_g)
    mask = jnp.tril(jnp.ones((CHUNK, CHUNK), dtype=w_s.dtype))
    mixed = jnp.einsum('gts,bnsgd->bntgd', w_s * mask, v) + b_s.T[None, None, :, :, None]
    return (u.reshape(shp) * mixed).reshape(bsz, s, D_A)


def _conformer_conv(a, g, conv_w, conv_b, ln_g, ln_b):
    y = a * jax.nn.sigmoid(g)
    y = lax.conv_general_dilated(y, conv_w[:, None, :], window_strides=(1,),
                                 padding=[(CONV_WIDTH - 1, 0)],
                                 dimension_numbers=('NWC', 'WIO', 'NWC'),
                                 feature_group_count=D_B) + conv_b
    return jax.nn.silu(_layernorm(y, ln_g, ln_b))


def _even_mixer(h, w_in, sgu_norm_g, sgu_w, sgu_b, conv_w, conv_b, ln_g, ln_b, w_out):
    proj = h @ w_in
    u, v, a, g = jnp.split(proj, [D_A, 2 * D_A, 2 * D_A + D_B], axis=-1)
    out_a = _spatial_gating(jax.nn.gelu(u), jax.nn.gelu(v), sgu_norm_g, sgu_w, sgu_b)
    out_b = _conformer_conv(a, g, conv_w, conv_b, ln_g, ln_b)
    return jnp.concatenate([out_a, out_b], axis=-1) @ w_out


def _rope_tables(s):
    pos = jnp.arange(s, dtype=jnp.float32)
    inv = ROPE_THETA ** (-jnp.arange(0, ROPE_DIM, 2, dtype=jnp.float32) / ROPE_DIM)
    ang = pos[:, None] * inv[None, :]
    return jnp.cos(ang), jnp.sin(ang)


def _apply_rope(x, cos, sin):
    xf = x.astype(jnp.float32)
    x1, x2 = jnp.split(xf, 2, axis=-1)
    return jnp.concatenate([x1 * cos - x2 * sin, x1 * sin + x2 * cos], axis=-1).astype(x.dtype)


def _segment_head_norm(t, g):
    return jnp.concatenate([_rms(t[..., :NOPE_DIM], g[:NOPE_DIM]),
                            _rms(t[..., NOPE_DIM:], g[NOPE_DIM:])], axis=-1)


def _block_causal_attention(q_nope, q_rope, k_nope, k_rope, v):
    s = q_nope.shape[1]
    outs = []
    for i in range(s // Q_BLOCK):
        q0, q1 = i * Q_BLOCK, (i + 1) * Q_BLOCK
        sc = (jnp.einsum('bqhd,bkhd->bhqk', q_nope[:, q0:q1], k_nope[:, :q1])
              + jnp.einsum('bqhr,bkr->bhqk', q_rope[:, q0:q1], k_rope[:, :q1]))
        sc = sc.astype(jnp.float32) * ATTN_SCALE
        qi = q0 + jnp.arange(Q_BLOCK)
        ki = jnp.arange(q1)
        sc = jnp.where(ki[None, :] <= qi[:, None], sc, -jnp.inf)
        p = jax.nn.softmax(sc, axis=-1).astype(v.dtype)
        outs.append(jnp.einsum('bhqk,bkhd->bqhd', p, v[:, :q1]))
    return jnp.concatenate(outs, axis=1)


def _mla_mixer(h, w_in, q_norm_g, kv_norm_g, w_uq, w_ukv, q_head_g, k_head_g, w_out):
    bsz, s, _ = h.shape
    proj = h @ w_in
    c_q, c_kv, k_rope = jnp.split(proj, [Q_RANK, Q_RANK + KV_RANK], axis=-1)
    q = (_rms(c_q, q_norm_g) @ w_uq).reshape(bsz, s, N_HEADS, QK_DIM)
    kv = (_rms(c_kv, kv_norm_g) @ w_ukv).reshape(bsz, s, N_HEADS, NOPE_DIM + V_DIM)
    k_nope, v = jnp.split(kv, [NOPE_DIM], axis=-1)
    q = _segment_head_norm(q, q_head_g)
    k_nope = _rms(k_nope, k_head_g[:NOPE_DIM])
    k_rope = _rms(k_rope, k_head_g[NOPE_DIM:])
    cos, sin = _rope_tables(s)
    q_rope = _apply_rope(q[..., NOPE_DIM:], cos[:, None, :], sin[:, None, :])
    k_rope = _apply_rope(k_rope, cos, sin)
    o = _block_causal_attention(q[..., :NOPE_DIM], q_rope, k_nope, k_rope, v)
    return o.reshape(bsz, s, N_HEADS * V_DIM) @ w_out


def _fwd_setup_inputs(seed: int = 0) -> dict:
    key = jax.random.key(seed)
    ks = jax.random.split(key, 32)
    f = jnp.float32
    nrm = lambda k, shp, sc: jax.random.normal(k, shp, f) * sc
    d = D_MODEL
    return {
        "x": nrm(ks[0], (BATCH, SEQ, d), 1.0),
        "c": nrm(ks[1], (BATCH, d), 1.0),
        "norm1_g": 1.0 + nrm(ks[2], (DEPTH, d), 0.02),
        "norm2_g": 1.0 + nrm(ks[3], (DEPTH, d), 0.02),
        "ada_w": nrm(ks[4], (DEPTH, d, 6 * d), 0.5 * d ** -0.5),
        "ada_b": nrm(ks[5], (DEPTH, 6 * d), 0.01),
        "mlp_w1": nrm(ks[6], (DEPTH, d, D_FF), d ** -0.5),
        "mlp_w2": nrm(ks[7], (DEPTH, D_FF, d), D_FF ** -0.5),
        "ab_w_in": nrm(ks[8], (N_EVEN, d, D_AB_IN), d ** -0.5),
        "sgu_norm_g": 1.0 + nrm(ks[9], (N_EVEN, N_A_GROUPS, GROUP_DIM), 0.02),
        "sgu_w": nrm(ks[10], (N_EVEN, N_A_GROUPS, CHUNK, CHUNK), CHUNK ** -0.5),
        "sgu_b": 1.0 + nrm(ks[11], (N_EVEN, N_A_GROUPS, CHUNK), 0.02),
        "conv_w": nrm(ks[12], (N_EVEN, CONV_WIDTH, D_B), CONV_WIDTH ** -0.5),
        "conv_b": nrm(ks[13], (N_EVEN, D_B), 0.01),
        "conv_ln_g": 1.0 + nrm(ks[14], (N_EVEN, D_B), 0.02),
        "conv_ln_b": nrm(ks[15], (N_EVEN, D_B), 0.01),
        "ab_w_out": nrm(ks[16], (N_EVEN, D_A + D_B, d), (D_A + D_B) ** -0.5),
        "mla_w_in": nrm(ks[17], (N_ODD, d, D_MLA_IN), d ** -0.5),
        "mla_q_norm_g": 1.0 + nrm(ks[18], (N_ODD, Q_RANK), 0.02),
        "mla_kv_norm_g": 1.0 + nrm(ks[19], (N_ODD, KV_RANK), 0.02),
        "mla_w_uq": nrm(ks[20], (N_ODD, Q_RANK, N_HEADS * QK_DIM), Q_RANK ** -0.5),
        "mla_w_ukv": nrm(ks[21], (N_ODD, KV_RANK, N_HEADS * (NOPE_DIM + V_DIM)), KV_RANK ** -0.5),
        "mla_q_head_g": 1.0 + nrm(ks[22], (N_ODD, QK_DIM), 0.02),
        "mla_k_head_g": 1.0 + nrm(ks[23], (N_ODD, QK_DIM), 0.02),
        "mla_w_out": nrm(ks[24], (N_ODD, N_HEADS * V_DIM, d), (N_HEADS * V_DIM) ** -0.5),
    }


def _fwd_reference(x, c, norm1_g, norm2_g, ada_w, ada_b, mlp_w1, mlp_w2,
              ab_w_in, sgu_norm_g, sgu_w, sgu_b, conv_w, conv_b, conv_ln_g, conv_ln_b, ab_w_out,
              mla_w_in, mla_q_norm_g, mla_kv_norm_g, mla_w_uq, mla_w_ukv,
              mla_q_head_g, mla_k_head_g, mla_w_out):
    c_act = jax.nn.silu(c.astype(x.dtype))
    for l in range(DEPTH):
        mod = c_act @ ada_w[l] + ada_b[l]
        shift1, scale1, gate1, shift2, scale2, gate2 = jnp.split(mod, 6, axis=-1)
        h = _modulate(_rms(x, norm1_g[l]), shift1, scale1)
        if l % 2 == 0:
            e = l // 2
            mix = _even_mixer(h, ab_w_in[e], sgu_norm_g[e], sgu_w[e], sgu_b[e], conv_w[e],
                              conv_b[e], conv_ln_g[e], conv_ln_b[e], ab_w_out[e])
        else:
            o = l // 2
            mix = _mla_mixer(h, mla_w_in[o], mla_q_norm_g[o], mla_kv_norm_g[o], mla_w_uq[o],
                             mla_w_ukv[o], mla_q_head_g[o], mla_k_head_g[o], mla_w_out[o])
        x = x + gate1[:, None, :] * mix
        h = _modulate(_rms(x, norm2_g[l]), shift2, scale2)
        x = x + gate2[:, None, :] * (jnp.square(jax.nn.relu(h @ mlp_w1[l])) @ mlp_w2[l])
    return x


import jax as _jax
import jax.numpy as _jnp

TWIN_FORMAT = 'train_step'
FWD_PARAMS = ['x', 'c', 'norm1_g', 'norm2_g', 'ada_w', 'ada_b', 'mlp_w1', 'mlp_w2', 'ab_w_in', 'sgu_norm_g', 'sgu_w', 'sgu_b', 'conv_w', 'conv_b', 'conv_ln_g', 'conv_ln_b', 'ab_w_out', 'mla_w_in', 'mla_q_norm_g', 'mla_kv_norm_g', 'mla_w_uq', 'mla_w_ukv', 'mla_q_head_g', 'mla_k_head_g', 'mla_w_out']
TWIN_WEIGHTS = ['norm1_g', 'norm2_g', 'ada_w', 'ada_b', 'mlp_w1', 'mlp_w2', 'ab_w_in', 'sgu_norm_g', 'sgu_w', 'sgu_b', 'conv_w', 'conv_b', 'conv_ln_g', 'conv_ln_b', 'ab_w_out', 'mla_w_in', 'mla_q_norm_g', 'mla_kv_norm_g', 'mla_w_uq', 'mla_w_ukv', 'mla_q_head_g', 'mla_k_head_g', 'mla_w_out']
TWIN_DIFF_INPUT = 'x'
TWIN_INPUTS = ['x', 'c', 'norm1_g', 'norm2_g', 'ada_w', 'ada_b', 'mlp_w1', 'mlp_w2', 'ab_w_in', 'sgu_norm_g', 'sgu_w', 'sgu_b', 'conv_w', 'conv_b', 'conv_ln_g', 'conv_ln_b', 'ab_w_out', 'mla_w_in', 'mla_q_norm_g', 'mla_kv_norm_g', 'mla_w_uq', 'mla_w_ukv', 'mla_q_head_g', 'mla_k_head_g', 'mla_w_out', 'loss_target', 'm_norm1_g', 'm_norm2_g', 'm_ada_w', 'm_ada_b', 'm_mlp_w1', 'm_mlp_w2', 'm_ab_w_in', 'm_sgu_norm_g', 'm_sgu_w', 'm_sgu_b', 'm_conv_w', 'm_conv_b', 'm_conv_ln_g', 'm_conv_ln_b', 'm_ab_w_out', 'm_mla_w_in', 'm_mla_q_norm_g', 'm_mla_kv_norm_g', 'm_mla_w_uq', 'm_mla_w_ukv', 'm_mla_q_head_g', 'm_mla_k_head_g', 'm_mla_w_out', 'v_norm1_g', 'v_norm2_g', 'v_ada_w', 'v_ada_b', 'v_mlp_w1', 'v_mlp_w2', 'v_ab_w_in', 'v_sgu_norm_g', 'v_sgu_w', 'v_sgu_b', 'v_conv_w', 'v_conv_b', 'v_conv_ln_g', 'v_conv_ln_b', 'v_ab_w_out', 'v_mla_w_in', 'v_mla_q_norm_g', 'v_mla_kv_norm_g', 'v_mla_w_uq', 'v_mla_w_ukv', 'v_mla_q_head_g', 'v_mla_k_head_g', 'v_mla_w_out']
TWIN_OUTPUTS = ['loss', 'grad_x', 'grad_norm1_g', 'grad_norm2_g', 'grad_ada_w', 'grad_ada_b', 'grad_mlp_w1', 'grad_mlp_w2', 'grad_ab_w_in', 'grad_sgu_norm_g', 'grad_sgu_w', 'grad_sgu_b', 'grad_conv_w', 'grad_conv_b', 'grad_conv_ln_g', 'grad_conv_ln_b', 'grad_ab_w_out', 'grad_mla_w_in', 'grad_mla_q_norm_g', 'grad_mla_kv_norm_g', 'grad_mla_w_uq', 'grad_mla_w_ukv', 'grad_mla_q_head_g', 'grad_mla_k_head_g', 'grad_mla_w_out', 'delta_norm1_g', 'delta_norm2_g', 'delta_ada_w', 'delta_ada_b', 'delta_mlp_w1', 'delta_mlp_w2', 'delta_ab_w_in', 'delta_sgu_norm_g', 'delta_sgu_w', 'delta_sgu_b', 'delta_conv_w', 'delta_conv_b', 'delta_conv_ln_g', 'delta_conv_ln_b', 'delta_ab_w_out', 'delta_mla_w_in', 'delta_mla_q_norm_g', 'delta_mla_kv_norm_g', 'delta_mla_w_uq', 'delta_mla_w_ukv', 'delta_mla_q_head_g', 'delta_mla_k_head_g', 'delta_mla_w_out', 'new_m_norm1_g', 'new_m_norm2_g', 'new_m_ada_w', 'new_m_ada_b', 'new_m_mlp_w1', 'new_m_mlp_w2', 'new_m_ab_w_in', 'new_m_sgu_norm_g', 'new_m_sgu_w', 'new_m_sgu_b', 'new_m_conv_w', 'new_m_conv_b', 'new_m_conv_ln_g', 'new_m_conv_ln_b', 'new_m_ab_w_out', 'new_m_mla_w_in', 'new_m_mla_q_norm_g', 'new_m_mla_kv_norm_g', 'new_m_mla_w_uq', 'new_m_mla_w_ukv', 'new_m_mla_q_head_g', 'new_m_mla_k_head_g', 'new_m_mla_w_out', 'new_v_norm1_g', 'new_v_norm2_g', 'new_v_ada_w', 'new_v_ada_b', 'new_v_mlp_w1', 'new_v_mlp_w2', 'new_v_ab_w_in', 'new_v_sgu_norm_g', 'new_v_sgu_w', 'new_v_sgu_b', 'new_v_conv_w', 'new_v_conv_b', 'new_v_conv_ln_g', 'new_v_conv_ln_b', 'new_v_ab_w_out', 'new_v_mla_w_in', 'new_v_mla_q_norm_g', 'new_v_mla_kv_norm_g', 'new_v_mla_w_uq', 'new_v_mla_w_ukv', 'new_v_mla_q_head_g', 'new_v_mla_k_head_g', 'new_v_mla_w_out']
TWIN_LEAF_KINDS = {'loss': 'loss', 'grad_x': 'grad_x', 'grad_norm1_g': 'grad_w', 'grad_norm2_g': 'grad_w', 'grad_ada_w': 'grad_w', 'grad_ada_b': 'grad_w', 'grad_mlp_w1': 'grad_w', 'grad_mlp_w2': 'grad_w', 'grad_ab_w_in': 'grad_w', 'grad_sgu_norm_g': 'grad_w', 'grad_sgu_w': 'grad_w', 'grad_sgu_b': 'grad_w', 'grad_conv_w': 'grad_w', 'grad_conv_b': 'grad_w', 'grad_conv_ln_g': 'grad_w', 'grad_conv_ln_b': 'grad_w', 'grad_ab_w_out': 'grad_w', 'grad_mla_w_in': 'grad_w', 'grad_mla_q_norm_g': 'grad_w', 'grad_mla_kv_norm_g': 'grad_w', 'grad_mla_w_uq': 'grad_w', 'grad_mla_w_ukv': 'grad_w', 'grad_mla_q_head_g': 'grad_w', 'grad_mla_k_head_g': 'grad_w', 'grad_mla_w_out': 'grad_w', 'delta_norm1_g': 'delta_w', 'delta_norm2_g': 'delta_w', 'delta_ada_w': 'delta_w', 'delta_ada_b': 'delta_w', 'delta_mlp_w1': 'delta_w', 'delta_mlp_w2': 'delta_w', 'delta_ab_w_in': 'delta_w', 'delta_sgu_norm_g': 'delta_w', 'delta_sgu_w': 'delta_w', 'delta_sgu_b': 'delta_w', 'delta_conv_w': 'delta_w', 'delta_conv_b': 'delta_w', 'delta_conv_ln_g': 'delta_w', 'delta_conv_ln_b': 'delta_w', 'delta_ab_w_out': 'delta_w', 'delta_mla_w_in': 'delta_w', 'delta_mla_q_norm_g': 'delta_w', 'delta_mla_kv_norm_g': 'delta_w', 'delta_mla_w_uq': 'delta_w', 'delta_mla_w_ukv': 'delta_w', 'delta_mla_q_head_g': 'delta_w', 'delta_mla_k_head_g': 'delta_w', 'delta_mla_w_out': 'delta_w', 'new_m_norm1_g': 'new_m', 'new_m_norm2_g': 'new_m', 'new_m_ada_w': 'new_m', 'new_m_ada_b': 'new_m', 'new_m_mlp_w1': 'new_m', 'new_m_mlp_w2': 'new_m', 'new_m_ab_w_in': 'new_m', 'new_m_sgu_norm_g': 'new_m', 'new_m_sgu_w': 'new_m', 'new_m_sgu_b': 'new_m', 'new_m_conv_w': 'new_m', 'new_m_conv_b': 'new_m', 'new_m_conv_ln_g': 'new_m', 'new_m_conv_ln_b': 'new_m', 'new_m_ab_w_out': 'new_m', 'new_m_mla_w_in': 'new_m', 'new_m_mla_q_norm_g': 'new_m', 'new_m_mla_kv_norm_g': 'new_m', 'new_m_mla_w_uq': 'new_m', 'new_m_mla_w_ukv': 'new_m', 'new_m_mla_q_head_g': 'new_m', 'new_m_mla_k_head_g': 'new_m', 'new_m_mla_w_out': 'new_m', 'new_v_norm1_g': 'new_v', 'new_v_norm2_g': 'new_v', 'new_v_ada_w': 'new_v', 'new_v_ada_b': 'new_v', 'new_v_mlp_w1': 'new_v', 'new_v_mlp_w2': 'new_v', 'new_v_ab_w_in': 'new_v', 'new_v_sgu_norm_g': 'new_v', 'new_v_sgu_w': 'new_v', 'new_v_sgu_b': 'new_v', 'new_v_conv_w': 'new_v', 'new_v_conv_b': 'new_v', 'new_v_conv_ln_g': 'new_v', 'new_v_conv_ln_b': 'new_v', 'new_v_ab_w_out': 'new_v', 'new_v_mla_w_in': 'new_v', 'new_v_mla_q_norm_g': 'new_v', 'new_v_mla_kv_norm_g': 'new_v', 'new_v_mla_w_uq': 'new_v', 'new_v_mla_w_ukv': 'new_v', 'new_v_mla_q_head_g': 'new_v', 'new_v_mla_k_head_g': 'new_v', 'new_v_mla_w_out': 'new_v'}


def _forward(args):
    return _fwd_reference(*[args[k] for k in FWD_PARAMS])


def _output_shape():
    def fwd():
        inp = _fwd_setup_inputs(0)
        return _fwd_reference(*[inp[k] for k in FWD_PARAMS])
    out = _jax.eval_shape(fwd)
    return out.shape, out.dtype

N_MICROBATCH = 1
ADAM_LR = 0.001
ADAM_B1 = 0.9
ADAM_B2 = 0.999
ADAM_EPS = 1e-08
ADAM_WD = 0.01
ADAM_STEP = 10
PER_EXAMPLE_BATCH_AXIS = {'x': 0, 'c': 0, 'loss_target': 0}
SHARED_INPUTS = []
_WEIGHT_DTYPES = {'norm1_g': _jnp.float32, 'norm2_g': _jnp.float32, 'ada_w': _jnp.float32, 'ada_b': _jnp.float32, 'mlp_w1': _jnp.float32, 'mlp_w2': _jnp.float32, 'ab_w_in': _jnp.float32, 'sgu_norm_g': _jnp.float32, 'sgu_w': _jnp.float32, 'sgu_b': _jnp.float32, 'conv_w': _jnp.float32, 'conv_b': _jnp.float32, 'conv_ln_g': _jnp.float32, 'conv_ln_b': _jnp.float32, 'ab_w_out': _jnp.float32, 'mla_w_in': _jnp.float32, 'mla_q_norm_g': _jnp.float32, 'mla_kv_norm_g': _jnp.float32, 'mla_w_uq': _jnp.float32, 'mla_w_ukv': _jnp.float32, 'mla_q_head_g': _jnp.float32, 'mla_k_head_g': _jnp.float32, 'mla_w_out': _jnp.float32}
MOMENT_SCALE = {'norm1_g': 4.607332e-01, 'norm2_g': 5.748698e+00, 'ada_w': 1.346717e+00, 'ada_b': 3.324342e+00, 'mlp_w1': 2.341491e-01, 'mlp_w2': 8.762179e-01, 'ab_w_in': 1.430109e-01, 'sgu_norm_g': 3.859083e-01, 'sgu_w': 2.544000e-01, 'sgu_b': 7.825616e-01, 'conv_w': 1.767398e-01, 'conv_b': 9.978042e-01, 'conv_ln_g': 8.017992e-01, 'conv_ln_b': 7.480025e-01, 'ab_w_out': 3.650007e-01, 'mla_w_in': 5.132305e-01, 'mla_q_norm_g': 2.257079e-02, 'mla_kv_norm_g': 1.094675e+00, 'mla_w_uq': 9.167823e-03, 'mla_w_ukv': 2.537665e-01, 'mla_q_head_g': 8.302211e-02, 'mla_k_head_g': 8.284937e-02, 'mla_w_out': 3.548073e-01}


def _to_microbatches(a, axis):
    t = _jnp.moveaxis(a, axis, 0)
    t = t.reshape((N_MICROBATCH, t.shape[0] // N_MICROBATCH) + t.shape[1:])
    return _jnp.moveaxis(t, 1, axis + 1)


def setup_inputs(seed: int = 0) -> dict:
    inp = _fwd_setup_inputs(seed)
    key = _jax.random.fold_in(_jax.random.key(seed), 7919)
    shape, _ = _output_shape()
    out = dict(inp)
    out["loss_target"] = _jax.random.normal(_jax.random.fold_in(key, 0), shape, _jnp.float32)
    for i, name in enumerate(TWIN_WEIGHTS):
        w = inp[name].astype(_jnp.float32)
        if MOMENT_SCALE is None:
            s = _jnp.sqrt(_jnp.mean(_jnp.square(w)) + 1e-30)
        else:
            s = MOMENT_SCALE[name]
        km, kv = _jax.random.split(_jax.random.fold_in(key, i + 1))
        out[name] = w
        out["m_" + name] = s * _jax.random.normal(km, w.shape, _jnp.float32)
        out["v_" + name] = (s * s) * _jax.random.uniform(kv, w.shape, _jnp.float32, 0.5, 1.5)
    if N_MICROBATCH > 1:
        for name, axis in PER_EXAMPLE_BATCH_AXIS.items():
            out[name] = _to_microbatches(out[name], axis)
    return {'x': out['x'], 'c': out['c'], 'norm1_g': out['norm1_g'], 'norm2_g': out['norm2_g'], 'ada_w': out['ada_w'], 'ada_b': out['ada_b'], 'mlp_w1': out['mlp_w1'], 'mlp_w2': out['mlp_w2'], 'ab_w_in': out['ab_w_in'], 'sgu_norm_g': out['sgu_norm_g'], 'sgu_w': out['sgu_w'], 'sgu_b': out['sgu_b'], 'conv_w': out['conv_w'], 'conv_b': out['conv_b'], 'conv_ln_g': out['conv_ln_g'], 'conv_ln_b': out['conv_ln_b'], 'ab_w_out': out['ab_w_out'], 'mla_w_in': out['mla_w_in'], 'mla_q_norm_g': out['mla_q_norm_g'], 'mla_kv_norm_g': out['mla_kv_norm_g'], 'mla_w_uq': out['mla_w_uq'], 'mla_w_ukv': out['mla_w_ukv'], 'mla_q_head_g': out['mla_q_head_g'], 'mla_k_head_g': out['mla_k_head_g'], 'mla_w_out': out['mla_w_out'], 'loss_target': out['loss_target'], 'm_norm1_g': out['m_norm1_g'], 'm_norm2_g': out['m_norm2_g'], 'm_ada_w': out['m_ada_w'], 'm_ada_b': out['m_ada_b'], 'm_mlp_w1': out['m_mlp_w1'], 'm_mlp_w2': out['m_mlp_w2'], 'm_ab_w_in': out['m_ab_w_in'], 'm_sgu_norm_g': out['m_sgu_norm_g'], 'm_sgu_w': out['m_sgu_w'], 'm_sgu_b': out['m_sgu_b'], 'm_conv_w': out['m_conv_w'], 'm_conv_b': out['m_conv_b'], 'm_conv_ln_g': out['m_conv_ln_g'], 'm_conv_ln_b': out['m_conv_ln_b'], 'm_ab_w_out': out['m_ab_w_out'], 'm_mla_w_in': out['m_mla_w_in'], 'm_mla_q_norm_g': out['m_mla_q_norm_g'], 'm_mla_kv_norm_g': out['m_mla_kv_norm_g'], 'm_mla_w_uq': out['m_mla_w_uq'], 'm_mla_w_ukv': out['m_mla_w_ukv'], 'm_mla_q_head_g': out['m_mla_q_head_g'], 'm_mla_k_head_g': out['m_mla_k_head_g'], 'm_mla_w_out': out['m_mla_w_out'], 'v_norm1_g': out['v_norm1_g'], 'v_norm2_g': out['v_norm2_g'], 'v_ada_w': out['v_ada_w'], 'v_ada_b': out['v_ada_b'], 'v_mlp_w1': out['v_mlp_w1'], 'v_mlp_w2': out['v_mlp_w2'], 'v_ab_w_in': out['v_ab_w_in'], 'v_sgu_norm_g': out['v_sgu_norm_g'], 'v_sgu_w': out['v_sgu_w'], 'v_sgu_b': out['v_sgu_b'], 'v_conv_w': out['v_conv_w'], 'v_conv_b': out['v_conv_b'], 'v_conv_ln_g': out['v_conv_ln_g'], 'v_conv_ln_b': out['v_conv_ln_b'], 'v_ab_w_out': out['v_ab_w_out'], 'v_mla_w_in': out['v_mla_w_in'], 'v_mla_q_norm_g': out['v_mla_q_norm_g'], 'v_mla_kv_norm_g': out['v_mla_kv_norm_g'], 'v_mla_w_uq': out['v_mla_w_uq'], 'v_mla_w_ukv': out['v_mla_w_ukv'], 'v_mla_q_head_g': out['v_mla_q_head_g'], 'v_mla_k_head_g': out['v_mla_k_head_g'], 'v_mla_w_out': out['v_mla_w_out']}


def _loss(weights, diff, rest, loss_target):
    with _jax.named_scope("forward"):
        args = {**rest, TWIN_DIFF_INPUT: diff, **{k: w.astype(_WEIGHT_DTYPES[k]) for k, w in weights.items()}}
        y = _forward(args)
    with _jax.named_scope("loss_head"):
        err = _jnp.square(y.astype(_jnp.float32) - loss_target)
        return 0.5 * _jnp.sum(_jnp.mean(err, axis=-1)) if err.ndim else 0.5 * err


def _adamw(w, g, m, v):
    m = ADAM_B1 * m + (1.0 - ADAM_B1) * g
    v = ADAM_B2 * v + (1.0 - ADAM_B2) * _jnp.square(g)
    m_hat = m / (1.0 - ADAM_B1 ** ADAM_STEP)
    v_hat = v / (1.0 - ADAM_B2 ** ADAM_STEP)
    delta = -ADAM_LR * (m_hat / (_jnp.sqrt(v_hat) + ADAM_EPS) + ADAM_WD * w)
    return delta, m, v


def reference(x, c, norm1_g, norm2_g, ada_w, ada_b, mlp_w1, mlp_w2, ab_w_in, sgu_norm_g, sgu_w, sgu_b, conv_w, conv_b, conv_ln_g, conv_ln_b, ab_w_out, mla_w_in, mla_q_norm_g, mla_kv_norm_g, mla_w_uq, mla_w_ukv, mla_q_head_g, mla_k_head_g, mla_w_out, loss_target, m_norm1_g, m_norm2_g, m_ada_w, m_ada_b, m_mlp_w1, m_mlp_w2, m_ab_w_in, m_sgu_norm_g, m_sgu_w, m_sgu_b, m_conv_w, m_conv_b, m_conv_ln_g, m_conv_ln_b, m_ab_w_out, m_mla_w_in, m_mla_q_norm_g, m_mla_kv_norm_g, m_mla_w_uq, m_mla_w_ukv, m_mla_q_head_g, m_mla_k_head_g, m_mla_w_out, v_norm1_g, v_norm2_g, v_ada_w, v_ada_b, v_mlp_w1, v_mlp_w2, v_ab_w_in, v_sgu_norm_g, v_sgu_w, v_sgu_b, v_conv_w, v_conv_b, v_conv_ln_g, v_conv_ln_b, v_ab_w_out, v_mla_w_in, v_mla_q_norm_g, v_mla_kv_norm_g, v_mla_w_uq, v_mla_w_ukv, v_mla_q_head_g, v_mla_k_head_g, v_mla_w_out):
    given = dict(x=x, c=c, norm1_g=norm1_g, norm2_g=norm2_g, ada_w=ada_w, ada_b=ada_b, mlp_w1=mlp_w1, mlp_w2=mlp_w2, ab_w_in=ab_w_in, sgu_norm_g=sgu_norm_g, sgu_w=sgu_w, sgu_b=sgu_b, conv_w=conv_w, conv_b=conv_b, conv_ln_g=conv_ln_g, conv_ln_b=conv_ln_b, ab_w_out=ab_w_out, mla_w_in=mla_w_in, mla_q_norm_g=mla_q_norm_g, mla_kv_norm_g=mla_kv_norm_g, mla_w_uq=mla_w_uq, mla_w_ukv=mla_w_ukv, mla_q_head_g=mla_q_head_g, mla_k_head_g=mla_k_head_g, mla_w_out=mla_w_out, loss_target=loss_target, m_norm1_g=m_norm1_g, m_norm2_g=m_norm2_g, m_ada_w=m_ada_w, m_ada_b=m_ada_b, m_mlp_w1=m_mlp_w1, m_mlp_w2=m_mlp_w2, m_ab_w_in=m_ab_w_in, m_sgu_norm_g=m_sgu_norm_g, m_sgu_w=m_sgu_w, m_sgu_b=m_sgu_b, m_conv_w=m_conv_w, m_conv_b=m_conv_b, m_conv_ln_g=m_conv_ln_g, m_conv_ln_b=m_conv_ln_b, m_ab_w_out=m_ab_w_out, m_mla_w_in=m_mla_w_in, m_mla_q_norm_g=m_mla_q_norm_g, m_mla_kv_norm_g=m_mla_kv_norm_g, m_mla_w_uq=m_mla_w_uq, m_mla_w_ukv=m_mla_w_ukv, m_mla_q_head_g=m_mla_q_head_g, m_mla_k_head_g=m_mla_k_head_g, m_mla_w_out=m_mla_w_out, v_norm1_g=v_norm1_g, v_norm2_g=v_norm2_g, v_ada_w=v_ada_w, v_ada_b=v_ada_b, v_mlp_w1=v_mlp_w1, v_mlp_w2=v_mlp_w2, v_ab_w_in=v_ab_w_in, v_sgu_norm_g=v_sgu_norm_g, v_sgu_w=v_sgu_w, v_sgu_b=v_sgu_b, v_conv_w=v_conv_w, v_conv_b=v_conv_b, v_conv_ln_g=v_conv_ln_g, v_conv_ln_b=v_conv_ln_b, v_ab_w_out=v_ab_w_out, v_mla_w_in=v_mla_w_in, v_mla_q_norm_g=v_mla_q_norm_g, v_mla_kv_norm_g=v_mla_kv_norm_g, v_mla_w_uq=v_mla_w_uq, v_mla_w_ukv=v_mla_w_ukv, v_mla_q_head_g=v_mla_q_head_g, v_mla_k_head_g=v_mla_k_head_g, v_mla_w_out=v_mla_w_out)
    weights = {n: given[n] for n in TWIN_WEIGHTS}
    shared = {n: given[n] for n in SHARED_INPUTS}
    per_example = {n: given[n] for n in ['x', 'c']}
    grad_fn = _jax.value_and_grad(_loss, argnums=(0, 1))

    def one_microbatch(ex, loss_target):
        ex = dict(ex)
        diff = ex.pop(TWIN_DIFF_INPUT)
        return grad_fn(weights, diff, {**shared, **ex}, loss_target)

    if N_MICROBATCH == 1:
        loss, (grad_w, grad_x) = one_microbatch(per_example, given["loss_target"])
    else:
        def body(carry, xs):
            loss_sum, grad_sum = carry
            l_k, (gw_k, gx_k) = one_microbatch(xs[0], xs[1])
            with _jax.named_scope("update"):
                return (loss_sum + l_k, _jax.tree.map(_jnp.add, grad_sum, gw_k)), gx_k

        init = (_jnp.zeros((), _jnp.float32), _jax.tree.map(_jnp.zeros_like, weights))
        (loss, grad_w), grad_x = _jax.lax.scan(body, init, (per_example, given["loss_target"]))
    with _jax.named_scope("update"):
        delta_w, new_m, new_v = {}, {}, {}
        for n in TWIN_WEIGHTS:
            delta_w[n], new_m[n], new_v[n] = _adamw(weights[n], grad_w[n], given["m_" + n], given["v_" + n])
    return (loss, grad_x, *[grad_w[n] for n in TWIN_WEIGHTS], *[delta_w[n] for n in TWIN_WEIGHTS],
            *[new_m[n] for n in TWIN_WEIGHTS], *[new_v[n] for n in TWIN_WEIGHTS])
```

```python
import functools

import jax
import jax.numpy as jnp
from jax import lax
from jax.experimental import pallas as pl
from jax.experimental.pallas import tpu as pltpu

F32 = jnp.float32
BF16 = jnp.bfloat16
EPS = 1e-6
NDEV = 8
LANES = 128
CHUNK = 128
GROUP = 128
CONV_W = 31
CONV_PAD = 32
NOPE, ROPE, VDIM = 128, 64, 128
QK = NOPE + ROPE
ROPE_THETA = 10000.0
VMEM_LIMIT = 56 * 1024 * 1024
ADAM_LR, ADAM_B1, ADAM_B2, ADAM_EPS, ADAM_WD, ADAM_STEP = 0.001, 0.9, 0.999, 1e-08, 0.01, 10
MESH = pl.DeviceIdType.MESH
NEG = -1e30


def _pcall(body, **kw):
    return pl.pallas_call(body, **kw)


def _params(sem=None):
    return pltpu.CompilerParams(dimension_semantics=sem, vmem_limit_bytes=VMEM_LIMIT)


def _tile(dim, target, align=LANES):
    if dim <= target:
        return dim
    t = (target // align) * align
    while t >= align:
        if dim % t == 0:
            return t
        t -= align
    return dim


def _rstd(x):
    return lax.rsqrt(jnp.mean(x * x, axis=-1, keepdims=True) + EPS)


def _sigmoid(x):
    return 1.0 / (1.0 + jnp.exp(-x))


_GC = 0.7978845608028654


def _gelu(x):
    return 0.5 * x * (1.0 + jnp.tanh(_GC * (x + 0.044715 * x * x * x)))


def _gelu_grad(x):
    t = jnp.tanh(_GC * (x + 0.044715 * x * x * x))
    return 0.5 * (1.0 + t) + 0.5 * x * (1.0 - t * t) * _GC * (1.0 + 3 * 0.044715 * x * x)


def _colsum(x):
    return jnp.sum(x, axis=0, keepdims=True)


def _rms_bwd(dy, xhat, rstd, g):
    dxh = dy * g
    dx = rstd * (dxh - xhat * jnp.mean(dxh * xhat, axis=-1, keepdims=True))
    return dx, _colsum(dy * xhat)


def _swap_halves(x):
    h = x.shape[-1] // 2
    return jnp.concatenate([x[:, h:], x[:, :h]], axis=1)


def _rope(x, cos2, sinm):
    return x * cos2 + _swap_halves(x) * sinm


def _unrope(dy, cos2, sinm):
    return dy * cos2 + _swap_halves(dy * sinm)


_DIMS = {"nn": (((1,), (0,)), ((), ())), "nt": (((1,), (1,)), ((), ())), "tn": (((0,), (0,)), ((), ()))}


def mm(a, b, mode, *, name, out_dtypes=(F32,), epi=None, extras=(), rowvecs=(), b_blocked=False, out_blocked=0,
       tm=1024, tn=1024, tk=512, precision=None, cast=BF16):
    if mode == "tn":
        K, M = a.shape
    else:
        M, K = a.shape
    if b_blocked:
        J, Rb, Cb = b.shape
        N = Rb if mode == "nt" else J * Cb
    else:
        N = b.shape[0] if mode == "nt" else b.shape[1]
    tm = _tile(M, tm)
    if mode == "nn" and b_blocked:
        tn = _tile(Cb, tn)
    elif out_blocked:
        tn = _tile(N // out_blocked, tn)
    else:
        tn = _tile(N, tn)
    tk = _tile(Cb, tk) if (mode == "nt" and b_blocked) else _tile(K, tk)
    nk = K // tk
    grid = (M // tm, N // tn, nk)

    if mode == "tn":
        a_spec = pl.BlockSpec((tk, tm), lambda i, j, k: (k, i))
    else:
        a_spec = pl.BlockSpec((tm, tk), lambda i, j, k: (i, k))
    if mode == "nn":
        if b_blocked:
            nper = Cb // tn
            b_spec = pl.BlockSpec((None, tk, tn), lambda i, j, k: (j // nper, k, j % nper))
        else:
            b_spec = pl.BlockSpec((tk, tn), lambda i, j, k: (k, j))
    elif mode == "nt":
        if b_blocked:
            kper = Cb // tk
            b_spec = pl.BlockSpec((None, tn, tk), lambda i, j, k: (k // kper, j, k % kper))
        else:
            b_spec = pl.BlockSpec((tn, tk), lambda i, j, k: (j, k))
    else:
        b_spec = pl.BlockSpec((tk, tn), lambda i, j, k: (k, j))
    if out_blocked:
        oper = (N // out_blocked) // tn
        o_spec = pl.BlockSpec((None, tm, tn), lambda i, j, k: (j // oper, i, j % oper))
        o_shape = (out_blocked, M, N // out_blocked)
    else:
        o_spec = pl.BlockSpec((tm, tn), lambda i, j, k: (i, j))
        o_shape = (M, N)
    e_spec = pl.BlockSpec((tm, tn), lambda i, j, k: (i, j))
    r_spec = pl.BlockSpec((1, tn), lambda i, j, k: (0, j))
    ne, nr, no = len(extras), len(rowvecs), len(out_dtypes)
    dims = _DIMS[mode]

    def body(a_ref, b_ref, *rest):
        ex = rest[:ne]
        rv = rest[ne:ne + nr]
        outs = rest[ne + nr:ne + nr + no]
        acc = rest[ne + nr + no]
        k = pl.program_id(2)

        @pl.when(k == 0)
        def _():
            acc[...] = jnp.zeros_like(acc)

        av, bv = a_ref[...], b_ref[...]
        if cast is not None:
            av, bv = av.astype(cast), bv.astype(cast)
        acc[...] += lax.dot_general(av, bv, dims, preferred_element_type=F32, precision=precision)

        @pl.when(k == nk - 1)
        def _():
            r = acc[...]
            vals = (r,) if epi is None else epi(r, *[e[...] for e in ex], *[v[...] for v in rv])
            for o, val in zip(outs, vals):
                o[...] = val.astype(o.dtype)

    res = _pcall(
        body, name=name, grid=grid,
        in_specs=[a_spec, b_spec] + [e_spec] * ne + [r_spec] * nr,
        out_specs=[o_spec] * no,
        out_shape=[jax.ShapeDtypeStruct(o_shape, dt) for dt in out_dtypes],
        scratch_shapes=[pltpu.VMEM((tm, tn), F32)],
        compiler_params=_params(("parallel", "parallel", "arbitrary")),
    )(a, b, *extras, *rowvecs)
    return res[0] if no == 1 else res


def rowwise(fn, tiled, consts=(), outs=(), reds=(), *, ts, name):
    specs = []
    arrs = []
    rows = None
    for t in tiled:
        a, w, cb = t if isinstance(t, tuple) else (t, None, 0)
        arrs.append(a)
        rows = a.shape[-2] if rows is None else rows
        if a.ndim == 2:
            specs.append(pl.BlockSpec((ts, a.shape[1] if w is None else w), lambda i, cb=cb: (i, cb)))
        else:
            specs.append(pl.BlockSpec((a.shape[0], ts, a.shape[2]), lambda i: (0, i, 0)))
    for a in consts:
        specs.append(pl.BlockSpec(a.shape, lambda i, n=a.ndim: (0,) * n))
    o_specs, o_shapes = [], []
    for shp, dt in outs:
        if len(shp) == 2:
            o_specs.append(pl.BlockSpec((ts, shp[1]), lambda i: (i, 0)))
        else:
            o_specs.append(pl.BlockSpec((shp[0], ts, shp[2]), lambda i: (0, i, 0)))
        o_shapes.append(jax.ShapeDtypeStruct(shp, dt))
    for shp in reds:
        o_specs.append(pl.BlockSpec(shp, lambda i, n=len(shp): (0,) * n))
        o_shapes.append(jax.ShapeDtypeStruct(shp, F32))
    nt, nc, no = len(arrs), len(consts), len(outs)

    def body(*refs):
        i = pl.program_id(0)
        red_refs = refs[nt + nc + no:]
        vals = fn(refs[:nt], refs[nt:nt + nc], refs[nt + nc:nt + nc + no])
        if red_refs:
            @pl.when(i == 0)
            def _():
                for r in red_refs:
                    r[...] = jnp.zeros_like(r)
            for r, v in zip(red_refs, vals):
                r[...] += v

    return _pcall(body, name=name, grid=(rows // ts,), in_specs=specs, out_specs=o_specs, out_shape=o_shapes,
                  compiler_params=_params(("arbitrary",)))(*arrs, *consts)


def cast_bf16(w, name):
    L, R, C = w.shape
    w2 = w.reshape(L * R, C)

    def fn(t, c, o):
        o[0][...] = t[0][...].astype(BF16)
        return []

    return rowwise(fn, [w2], outs=[((L * R, C), BF16)], ts=_tile(L * R, 512, 16), name=name)[0].reshape(L, R, C)


def prenorm(x, g, scale, shift, name):
    S, D = x.shape

    def fn(t, c, o):
        xv = t[0][...]
        r = _rstd(xv)
        o[0][...] = ((xv * r * c[0][...]) * (1.0 + c[1][...]) + c[2][...]).astype(BF16)
        o[1][...] = r
        return []

    return rowwise(fn, [x], [g, scale, shift], [((S, D), BF16), ((S, 1), F32)], ts=_tile(S, 128, 16), name=name)


def gate_bwd(dx, y, gate, name):
    S, D = dx.shape

    def fn(t, c, o):
        d = t[0][...]
        o[0][...] = (d * c[0][...]).astype(BF16)
        return [_colsum(d * t[1][...].astype(F32))]

    return rowwise(fn, [dx, y], [gate], [((S, D), BF16)], [(1, D)], ts=_tile(S, 128, 16), name=name)


def norm_bwd(dh, x, rstd, dres, g, scale, name):
    S, D = x.shape

    def fn(t, c, o):
        d = t[0][...]
        r = t[2][...]
        xh = t[1][...] * r
        gv = c[0][...]
        dr = d * (1.0 + c[1][...])
        dx, dg = _rms_bwd(dr, xh, r, gv)
        o[0][...] = t[3][...] + dx
        return [_colsum(d * (xh * gv)), _colsum(d), dg]

    return rowwise(fn, [dh, x, rstd, dres], [g, scale], [((S, D), F32)], [(1, D)] * 3, ts=_tile(S, 128, 8), name=name)


def loss_grad(y, tgt, name):
    S, D = y.shape

    def fn(t, c, o):
        e = t[0][...] - t[1][...]
        o[0][...] = e * (1.0 / D)
        return [_colsum(e * e)]

    return rowwise(fn, [y, tgt], outs=[((S, D), F32)], reds=[(1, D)], ts=_tile(S, 128, 8), name=name)


def _tril_mask():
    r = lax.broadcasted_iota(jnp.int32, (CHUNK, CHUNK), 0)
    c = lax.broadcasted_iota(jnp.int32, (CHUNK, CHUNK), 1)
    return c <= r


def sgu_fwd(proj, ng, w, bcol, DA, name):
    S = proj.shape[0]
    G = DA // GROUP
    tr = _tile(S, 2 * CHUNK)

    def body(u_ref, v_ref, ng_ref, w_ref, b_ref, o_ref):
        mask = _tril_mask()
        for g in range(G):
            cols = slice(g * GROUP, (g + 1) * GROUP)
            wm = jnp.where(mask, w_ref[g], 0.0).astype(BF16)
            for ci in range(tr // CHUNK):
                rows = slice(ci * CHUNK, (ci + 1) * CHUNK)
                gv = _gelu(v_ref[rows, cols])
                vn = gv * _rstd(gv) * ng_ref[:, cols]
                mixed = jnp.dot(wm, vn.astype(BF16), preferred_element_type=F32) + b_ref[g]
                o_ref[rows, cols] = (_gelu(u_ref[rows, cols]) * mixed).astype(o_ref.dtype)

    return _pcall(
        body, name=name, grid=(S // tr,),
        in_specs=[pl.BlockSpec((tr, DA), lambda i: (i, 0)), pl.BlockSpec((tr, DA), lambda i: (i, 1)),
                  pl.BlockSpec((1, DA), lambda i: (0, 0)), pl.BlockSpec((G, CHUNK, CHUNK), lambda i: (0, 0, 0)),
                  pl.BlockSpec((G, CHUNK, 1), lambda i: (0, 0, 0))],
        out_specs=pl.BlockSpec((tr, DA), lambda i: (i, 0)),
        out_shape=jax.ShapeDtypeStruct((S, DA), BF16),
        compiler_params=_params(("parallel",)),
    )(proj, proj, ng, w, bcol)


def sgu_bwd(proj, dcat, ng, w, bcol, DA, name):
    S = proj.shape[0]
    G = DA // GROUP
    tr = _tile(S, 2 * CHUNK)
    nsteps = S // tr

    def body(u_ref, v_ref, d_ref, ng_ref, w_ref, b_ref, duv_ref, dw_ref, db_ref, dng_ref, dbacc):
        i = pl.program_id(0)

        @pl.when(i == 0)
        def _():
            dw_ref[...] = jnp.zeros_like(dw_ref)
            dng_ref[...] = jnp.zeros_like(dng_ref)
            dbacc[...] = jnp.zeros_like(dbacc)

        mask = _tril_mask()
        for g in range(G):
            cols = slice(g * GROUP, (g + 1) * GROUP)
            wm = jnp.where(mask, w_ref[g], 0.0).astype(BF16)
            ngg = ng_ref[:, cols]
            for ci in range(tr // CHUNK):
                rows = slice(ci * CHUNK, (ci + 1) * CHUNK)
                u, v, d = u_ref[rows, cols], v_ref[rows, cols], d_ref[rows, cols]
                gv = _gelu(v)
                rs = _rstd(gv)
                vhat = gv * rs
                vn = (vhat * ngg).astype(BF16)
                mixed = jnp.dot(wm, vn, preferred_element_type=F32) + b_ref[g]
                dmixed = d * _gelu(u)
                dmb = dmixed.astype(BF16)
                duv_ref[rows, cols] = (d * mixed * _gelu_grad(u)).astype(duv_ref.dtype)
                dwg = lax.dot_general(dmb, vn, _DIMS["nt"], preferred_element_type=F32)
                dw_ref[g] += jnp.where(mask, dwg, 0.0)
                dbacc[g] += dmixed
                dvn = lax.dot_general(wm, dmb, _DIMS["tn"], preferred_element_type=F32)
                dgv, dngg = _rms_bwd(dvn, vhat, rs, ngg)
                dng_ref[:, cols] += dngg
                duv_ref[rows, DA + g * GROUP:DA + (g + 1) * GROUP] = (dgv * _gelu_grad(v)).astype(duv_ref.dtype)

        @pl.when(i == nsteps - 1)
        def _():
            for g in range(G):
                db_ref[g] = jnp.sum(dbacc[g], axis=-1, keepdims=True)

    return _pcall(
        body, name=name, grid=(nsteps,),
        in_specs=[pl.BlockSpec((tr, DA), lambda i: (i, 0)), pl.BlockSpec((tr, DA), lambda i: (i, 1)),
                  pl.BlockSpec((tr, DA), lambda i: (i, 0)),
                  pl.BlockSpec((1, DA), lambda i: (0, 0)), pl.BlockSpec((G, CHUNK, CHUNK), lambda i: (0, 0, 0)),
                  pl.BlockSpec((G, CHUNK, 1), lambda i: (0, 0, 0))],
        out_specs=[pl.BlockSpec((tr, 2 * DA), lambda i: (i, 0)), pl.BlockSpec((G, CHUNK, CHUNK), lambda i: (0, 0, 0)),
                   pl.BlockSpec((G, CHUNK, 1), lambda i: (0, 0, 0)), pl.BlockSpec((1, DA), lambda i: (0, 0))],
        out_shape=[jax.ShapeDtypeStruct((S, 2 * DA), BF16), jax.ShapeDtypeStruct((G, CHUNK, CHUNK), F32),
                   jax.ShapeDtypeStruct((G, CHUNK, 1), F32), jax.ShapeDtypeStruct((1, DA), F32)],
        scratch_shapes=[pltpu.VMEM((G, CHUNK, CHUNK), F32)],
        compiler_params=_params(("arbitrary",)),
    )(proj, proj, dcat, ng, w, bcol)


def _conv_tile(S):
    return _tile(S, 256, 8)


def conv_fwd(proj, wk, bias, DA, DB, name):
    S = proj.shape[0]
    nb = DB // LANES
    a0, g0 = 2 * DA // LANES, (2 * DA + DB) // LANES
    T = _conv_tile(S)
    off = CONV_PAD - (CONV_W - 1)

    def body(a_ref, g_ref, w_ref, b_ref, o_ref, ypad):
        ypad[0:CONV_PAD, :] = jnp.zeros((CONV_PAD, LANES), F32)

        def fill(t, cr):
            r = pl.multiple_of(t * T, T)
            ypad[pl.ds(CONV_PAD + r, T), :] = a_ref[pl.ds(r, T), :] * _sigmoid(g_ref[pl.ds(r, T), :])
            return cr

        lax.fori_loop(0, S // T, fill, 0)

        def step(t, cr):
            r = pl.multiple_of(t * T, T)
            acc = jnp.zeros((T, LANES), F32) + b_ref[...]
            for k in range(CONV_W):
                acc = acc + w_ref[k:k + 1, :] * ypad[pl.ds(r + (k + off), T), :]
            o_ref[pl.ds(r, T), :] = acc
            return cr

        lax.fori_loop(0, S // T, step, 0)

    return _pcall(
        body, name=name, grid=(nb,),
        in_specs=[pl.BlockSpec((S, LANES), lambda j: (0, a0 + j)), pl.BlockSpec((S, LANES), lambda j: (0, g0 + j)),
                  pl.BlockSpec((CONV_PAD, LANES), lambda j: (0, j)), pl.BlockSpec((1, LANES), lambda j: (0, j))],
        out_specs=pl.BlockSpec((S, LANES), lambda j: (0, j)),
        out_shape=jax.ShapeDtypeStruct((S, DB), F32),
        scratch_shapes=[pltpu.VMEM((S + CONV_PAD, LANES), F32)],
        compiler_params=_params(("parallel",)),
    )(proj, proj, wk, bias)


def conv_bwd(proj, dy1, wk, DA, DB, name):
    S = proj.shape[0]
    nb = DB // LANES
    a0, g0 = 2 * DA // LANES, (2 * DA + DB) // LANES
    T = _conv_tile(S)
    off = CONV_PAD - (CONV_W - 1)

    def body(a_ref, g_ref, d_ref, w_ref, da_ref, dg_ref, dw_ref, ypad, dpad, wacc):
        ypad[0:CONV_PAD, :] = jnp.zeros((CONV_PAD, LANES), F32)
        dpad[S:S + CONV_PAD, :] = jnp.zeros((CONV_PAD, LANES), F32)
        wacc[...] = jnp.zeros_like(wacc)

        def fill(t, cr):
            r = pl.multiple_of(t * T, T)
            ypad[pl.ds(CONV_PAD + r, T), :] = a_ref[pl.ds(r, T), :] * _sigmoid(g_ref[pl.ds(r, T), :])
            dpad[pl.ds(r, T), :] = d_ref[pl.ds(r, T), :]
            return cr

        lax.fori_loop(0, S // T, fill, 0)

        def step(t, cr):
            r = pl.multiple_of(t * T, T)
            dt = dpad[pl.ds(r, T), :]
            dy0 = jnp.zeros((T, LANES), F32)
            for k in range(CONV_W):
                prod = dt * ypad[pl.ds(r + (k + off), T), :]
                wacc[k] += jnp.sum(prod.reshape(T // 8, 8, LANES), axis=0)
                dy0 = dy0 + w_ref[k:k + 1, :] * dpad[pl.ds(r + (CONV_W - 1 - k), T), :]
            av, gv = a_ref[pl.ds(r, T), :], g_ref[pl.ds(r, T), :]
            sg = _sigmoid(gv)
            da_ref[pl.ds(r, T), :] = (dy0 * sg).astype(da_ref.dtype)
            dg_ref[pl.ds(r, T), :] = (dy0 * av * sg * (1.0 - sg)).astype(dg_ref.dtype)
            return cr

        lax.fori_loop(0, S // T, step, 0)
        for k in range(CONV_W):
            dw_ref[k:k + 1, :] = jnp.sum(wacc[k], axis=0, keepdims=True)
        dw_ref[CONV_W:CONV_PAD, :] = jnp.zeros((CONV_PAD - CONV_W, LANES), F32)

    return _pcall(
        body, name=name, grid=(nb,),
        in_specs=[pl.BlockSpec((S, LANES), lambda j: (0, a0 + j)), pl.BlockSpec((S, LANES), lambda j: (0, g0 + j)),
                  pl.BlockSpec((S, LANES), lambda j: (0, j)), pl.BlockSpec((CONV_PAD, LANES), lambda j: (0, j))],
        out_specs=[pl.BlockSpec((S, LANES), lambda j: (0, j)), pl.BlockSpec((S, LANES), lambda j: (0, j)),
                   pl.BlockSpec((CONV_PAD, LANES), lambda j: (0, j))],
        out_shape=[jax.ShapeDtypeStruct((S, DB), BF16), jax.ShapeDtypeStruct((S, DB), BF16),
                   jax.ShapeDtypeStruct((CONV_PAD, DB), F32)],
        scratch_shapes=[pltpu.VMEM((S + CONV_PAD, LANES), F32), pltpu.VMEM((S + CONV_PAD, LANES), F32),
                        pltpu.VMEM((CONV_PAD, 8, LANES), F32)],
        compiler_params=_params(("parallel",)),
    )(proj, proj, dy1, wk)


def _ln_stats(y):
    mu = jnp.mean(y, axis=-1, keepdims=True)
    yc = y - mu
    rs = lax.rsqrt(jnp.mean(yc * yc, axis=-1, keepdims=True) + EPS)
    return yc * rs, rs


def ln_silu(y1, lg, lb, name):
    S, DB = y1.shape

    def fn(t, c, o):
        yh, _ = _ln_stats(t[0][...])
        ln = yh * c[0][...] + c[1][...]
        o[0][...] = (ln * _sigmoid(ln)).astype(BF16)
        return []

    return rowwise(fn, [y1], [lg, lb], [((S, DB), BF16)], ts=_tile(S, 128, 16), name=name)[0]


def ln_silu_bwd(dcat, y1, lg, lb, name):
    S, DB = y1.shape
    cb = (dcat.shape[1] - DB) // DB

    def fn(t, c, o):
        yh, rs = _ln_stats(t[1][...])
        gv = c[0][...]
        ln = yh * gv + c[1][...]
        sg = _sigmoid(ln)
        dln = t[0][...] * (sg * (1.0 + ln * (1.0 - sg)))
        dyh = dln * gv
        dy = rs * (dyh - jnp.mean(dyh, axis=-1, keepdims=True) - yh * jnp.mean(dyh * yh, axis=-1, keepdims=True))
        o[0][...] = dy
        return [_colsum(dln * yh), _colsum(dln), _colsum(dy)]

    return rowwise(fn, [(dcat, DB, cb), y1], [lg, lb], [((S, DB), F32)], [(1, DB)] * 3, ts=_tile(S, 128, 8), name=name)


def mla_norms(proj, cos2, sinm, qg, kvg, kgr, R, name):
    S = proj.shape[0]

    def fn(t, c, o):
        cq = t[0][:, 0:R]
        ckv = t[0][:, R:2 * R]
        kr = t[0][:, 2 * R:2 * R + ROPE]
        o[0][...] = (cq * _rstd(cq) * c[0][...]).astype(BF16)
        o[1][...] = (ckv * _rstd(ckv) * c[1][...]).astype(BF16)
        o[2][...] = _rope(kr * _rstd(kr) * c[2][...], t[1][...], t[2][...])
        return []

    return rowwise(fn, [proj, cos2, sinm], [qg, kvg, kgr], [((S, R), BF16), ((S, R), BF16), ((S, ROPE), F32)],
                   ts=_tile(S, 128, 16), name=name)


def mla_heads(q, kv, kr, cos2, sinm, qg, kg, H, name):
    S = q.shape[0]

    def fn(t, c, o):
        cs, sn = t[3][...], t[4][...]
        krv = t[2][...]
        qgn, qgr, kgn = c[0][:, 0:NOPE], c[0][:, NOPE:QK], c[1][:, 0:NOPE]
        for h in range(H):
            qn = t[0][:, QK * h:QK * h + NOPE]
            qr = t[0][:, QK * h + NOPE:QK * (h + 1)]
            o[0][h, :, 0:NOPE] = (qn * _rstd(qn) * qgn).astype(BF16)
            o[0][h, :, NOPE:QK] = _rope(qr * _rstd(qr) * qgr, cs, sn).astype(BF16)
            kn = t[1][:, (NOPE + VDIM) * h:(NOPE + VDIM) * h + NOPE]
            o[1][h, :, 0:NOPE] = (kn * _rstd(kn) * kgn).astype(BF16)
            o[1][h, :, NOPE:QK] = krv.astype(BF16)
            o[2][h] = t[1][:, (NOPE + VDIM) * h + NOPE:(NOPE + VDIM) * (h + 1)].astype(BF16)
        return []

    return rowwise(fn, [q, kv, kr, cos2, sinm], [qg, kg],
                   [((H, S, QK), BF16), ((H, S, QK), BF16), ((H, S, VDIM), BF16)], ts=_tile(S, 128, 16), name=name)


def mla_heads_bwd(dQ, dK, dV, q, kv, cos2, sinm, qg, kg, H, name):
    S = q.shape[0]
    KV = NOPE + VDIM

    def fn(t, c, o):
        cs, sn = t[5][...], t[6][...]
        qgn, qgr, kgn = c[0][:, 0:NOPE], c[0][:, NOPE:QK], c[1][:, 0:NOPE]
        a_qn = jnp.zeros((1, NOPE), F32)
        a_qr = jnp.zeros((1, ROPE), F32)
        a_kn = jnp.zeros((1, NOPE), F32)
        dkr = jnp.zeros((t[0].shape[1], ROPE), F32)
        for h in range(H):
            qn = t[3][:, QK * h:QK * h + NOPE]
            rs = _rstd(qn)
            dx, dg = _rms_bwd(t[0][h, :, 0:NOPE], qn * rs, rs, qgn)
            o[0][:, QK * h:QK * h + NOPE] = dx.astype(BF16)
            a_qn = a_qn + dg
            qr = t[3][:, QK * h + NOPE:QK * (h + 1)]
            rs = _rstd(qr)
            dx, dg = _rms_bwd(_unrope(t[0][h, :, NOPE:QK], cs, sn), qr * rs, rs, qgr)
            o[0][:, QK * h + NOPE:QK * (h + 1)] = dx.astype(BF16)
            a_qr = a_qr + dg
            kn = t[4][:, KV * h:KV * h + NOPE]
            rs = _rstd(kn)
            dx, dg = _rms_bwd(t[1][h, :, 0:NOPE], kn * rs, rs, kgn)
            o[1][:, KV * h:KV * h + NOPE] = dx.astype(BF16)
            a_kn = a_kn + dg
            o[1][:, KV * h + NOPE:KV * (h + 1)] = t[2][h].astype(BF16)
            dkr = dkr + t[1][h, :, NOPE:QK]
        o[2][...] = _unrope(dkr, cs, sn)
        return [a_qn, a_qr, a_kn]

    return rowwise(fn, [dQ, dK, dV, q, kv, cos2, sinm], [qg, kg],
                   [((S, H * QK), BF16), ((S, H * KV), BF16), ((S, ROPE), F32)],
                   [(1, NOPE), (1, ROPE), (1, NOPE)], ts=_tile(S, 128, 16), name=name)


def mla_norms_bwd(dqn, dkvn, dkr, proj, qg, kvg, kgr, R, name):
    S = proj.shape[0]

    def fn(t, c, o):
        reds = []
        for idx, (lo, hi) in enumerate(((0, R), (R, 2 * R), (2 * R, 2 * R + ROPE))):
            xv = t[3][:, lo:hi]
            rs = _rstd(xv)
            dx, dg = _rms_bwd(t[idx][...], xv * rs, rs, c[idx][...])
            o[0][:, lo:hi] = dx.astype(BF16)
            reds.append(dg)
        return reds

    return rowwise(fn, [dqn, dkvn, dkr, proj], [qg, kvg, kgr], [((S, 2 * R + ROPE), BF16)],
                   [(1, R), (1, R), (1, ROPE)], ts=_tile(S, 128, 16), name=name)


def _causal_mask(qi, ki, tq, tk):
    r = qi * tq + lax.broadcasted_iota(jnp.int32, (tq, tk), 0)
    c = ki * tk + lax.broadcasted_iota(jnp.int32, (tq, tk), 1)
    return c <= r


def attn_fwd(Q, K, V, name):
    H, S, _ = Q.shape
    t = _tile(S, 512)
    n = S // t
    scale = QK ** -0.5

    def body(q_ref, k_ref, v_ref, o_ref, lse_ref, m_s, l_s, acc):
        qi, ki = pl.program_id(1), pl.program_id(2)

        @pl.when(ki == 0)
        def _():
            m_s[...] = jnp.full_like(m_s, NEG)
            l_s[...] = jnp.zeros_like(l_s)
            acc[...] = jnp.zeros_like(acc)

        @pl.when(ki <= qi)
        def _():
            s = lax.dot_general(q_ref[...], k_ref[...], _DIMS["nt"], preferred_element_type=F32) * scale
            s = jnp.where(_causal_mask(qi, ki, t, t), s, NEG)
            m_new = jnp.maximum(m_s[...], jnp.max(s, axis=-1, keepdims=True))
            alpha = jnp.exp(m_s[...] - m_new)
            p = jnp.exp(s - m_new)
            l_s[...] = alpha * l_s[...] + jnp.sum(p, axis=-1, keepdims=True)
            acc[...] = alpha * acc[...] + jnp.dot(p.astype(BF16), v_ref[...], preferred_element_type=F32)
            m_s[...] = m_new

        @pl.when(ki == n - 1)
        def _():
            o_ref[...] = (acc[...] / l_s[...]).astype(o_ref.dtype)
            lse_ref[...] = m_s[...] + jnp.log(l_s[...])

    return _pcall(
        body, name=name, grid=(H, n, n),
        in_specs=[pl.BlockSpec((None, t, QK), lambda h, qi, ki: (h, qi, 0)),
                  pl.BlockSpec((None, t, QK), lambda h, qi, ki: (h, jnp.minimum(ki, qi), 0)),
                  pl.BlockSpec((None, t, VDIM), lambda h, qi, ki: (h, jnp.minimum(ki, qi), 0))],
        out_specs=[pl.BlockSpec((t, VDIM), lambda h, qi, ki: (qi, h)),
                   pl.BlockSpec((None, t, 1), lambda h, qi, ki: (h, qi, 0))],
        out_shape=[jax.ShapeDtypeStruct((S, H * VDIM), BF16), jax.ShapeDtypeStruct((H, S, 1), F32)],
        scratch_shapes=[pltpu.VMEM((t, 1), F32), pltpu.VMEM((t, 1), F32), pltpu.VMEM((t, VDIM), F32)],
        compiler_params=_params(("parallel", "parallel", "arbitrary")),
    )(Q, K, V)


def attn_bwd(Q, K, V, dO, O, lse, name):
    H, S, _ = Q.shape
    t = _tile(S, 512)
    n = S // t
    scale = QK ** -0.5

    def body(q_ref, k_ref, v_ref, do_ref, o_ref, lse_ref, dq_ref, dk_ref, dv_ref):
        ki, qi = pl.program_id(1), pl.program_id(2)

        @pl.when((ki == 0) & (qi == 0))
        def _():
            dq_ref[...] = jnp.zeros_like(dq_ref)

        @pl.when(qi == 0)
        def _():
            dk_ref[...] = jnp.zeros_like(dk_ref)
            dv_ref[...] = jnp.zeros_like(dv_ref)

        @pl.when(qi >= ki)
        def _():
            q, k, v, do = q_ref[...], k_ref[...], v_ref[...], do_ref[...]
            s = lax.dot_general(q, k, _DIMS["nt"], preferred_element_type=F32) * scale
            s = jnp.where(_causal_mask(qi, ki, t, t), s, NEG)
            p = jnp.exp(s - lse_ref[...])
            delta = jnp.sum(do.astype(F32) * o_ref[...].astype(F32), axis=-1, keepdims=True)
            dv_ref[...] += lax.dot_general(p.astype(BF16), do, _DIMS["tn"], preferred_element_type=F32)
            dp = lax.dot_general(do, v, _DIMS["nt"], preferred_element_type=F32)
            ds = (p * (dp - delta) * scale).astype(BF16)
            dk_ref[...] += lax.dot_general(ds, q, _DIMS["tn"], preferred_element_type=F32)
            r = pl.multiple_of(qi * t, t)
            dq_ref[pl.ds(r, t), :] += jnp.dot(ds, k, preferred_element_type=F32)

    qmap = lambda h, ki, qi: (h, jnp.maximum(qi, ki), 0)
    return _pcall(
        body, name=name, grid=(H, n, n),
        in_specs=[pl.BlockSpec((None, t, QK), qmap),
                  pl.BlockSpec((None, t, QK), lambda h, ki, qi: (h, ki, 0)),
                  pl.BlockSpec((None, t, VDIM), lambda h, ki, qi: (h, ki, 0)),
                  pl.BlockSpec((t, VDIM), lambda h, ki, qi: (jnp.maximum(qi, ki), h)),
                  pl.BlockSpec((t, VDIM), lambda h, ki, qi: (jnp.maximum(qi, ki), h)),
                  pl.BlockSpec((None, t, 1), qmap)],
        out_specs=[pl.BlockSpec((None, S, QK), lambda h, ki, qi: (h, 0, 0)),
                   pl.BlockSpec((None, t, QK), lambda h, ki, qi: (h, ki, 0)),
                   pl.BlockSpec((None, t, VDIM), lambda h, ki, qi: (h, ki, 0))],
        out_shape=[jax.ShapeDtypeStruct((H, S, QK), F32), jax.ShapeDtypeStruct((H, S, QK), F32),
                   jax.ShapeDtypeStruct((H, S, VDIM), F32)],
        compiler_params=_params(("parallel", "arbitrary", "arbitrary")),
    )(Q, K, V, dO, O, lse)


def adamw(w, m, v, g, *, row0, name):
    P, rows, C = g.shape
    tr = _tile(rows, max(16, 131072 // C), 16)
    off = row0 // tr
    assert row0 % tr == 0
    bc1 = 1.0 - ADAM_B1 ** ADAM_STEP
    bc2 = 1.0 - ADAM_B2 ** ADAM_STEP

    def body(w_ref, m_ref, v_ref, g_ref, go_ref, d_ref, mo_ref, vo_ref):
        gs = g_ref[0].astype(F32)
        for p in range(1, P):
            gs = gs + g_ref[p].astype(F32)
        wv = w_ref[...]
        mn = ADAM_B1 * m_ref[...] + (1.0 - ADAM_B1) * gs
        vn = ADAM_B2 * v_ref[...] + (1.0 - ADAM_B2) * (gs * gs)
        go_ref[...] = gs
        mo_ref[...] = mn
        vo_ref[...] = vn
        d_ref[...] = -ADAM_LR * ((mn / bc1) / (jnp.sqrt(vn / bc2) + ADAM_EPS) + ADAM_WD * wv)

    wspec = pl.BlockSpec((tr, C), lambda i: (i + off, 0))
    ospec = pl.BlockSpec((tr, C), lambda i: (i, 0))
    return _pcall(
        body, name=name, grid=(rows // tr,),
        in_specs=[wspec, wspec, wspec, pl.BlockSpec((P, tr, C), lambda i: (0, i, 0))],
        out_specs=[ospec] * 4, out_shape=[jax.ShapeDtypeStruct((rows, C), F32)] * 4,
        compiler_params=_params(("parallel",)),
    )(w, m, v, g)


def _coords():
    return lax.axis_index("x"), lax.axis_index("y"), lax.axis_index("c")


def _me():
    x, y, c = _coords()
    return 4 * x + 2 * y + c


_ANY = pl.BlockSpec(memory_space=pl.ANY)


def all_gather(items, name):
    n = len(items)
    blks = [a.shape if idx is None else a.shape[1:] for a, idx in items]

    def body(*refs):
        ins, outs = refs[:n], refs[n:2 * n]
        send, recv, lsem = refs[2 * n:]
        x, y, c = _coords()
        me, sib = (x, y, c), (x, y, 1 - c)
        chips = [(1 - x, y), (x, 1 - y), (1 - x, 1 - y)]

        def src(i):
            return ins[i] if items[i][1] is None else ins[i].at[items[i][1]]

        def slot(i, p):
            return outs[i].at[4 * p[0] + 2 * p[1] + p[2]]

        def cp(i, k, block, to, s=None):
            return pltpu.make_async_remote_copy(
                src_ref=slot(i, block) if s is None else s, dst_ref=slot(i, block),
                send_sem=send.at[7 * i + k], recv_sem=recv.at[7 * i + k], device_id=to, device_id_type=MESH)

        mine = [pltpu.make_async_copy(src(i), slot(i, me), lsem.at[i]) for i in range(n)]
        for m_ in mine:
            m_.start()
        first = []
        for i in range(n):
            first.append(cp(i, 0, me, sib, src(i)))
            first += [cp(i, 1 + j, me, (*chip, c), src(i)) for j, chip in enumerate(chips)]
        for f in first:
            f.start()
        passed = []
        for j, chip in enumerate(chips):
            for i in range(n):
                cp(i, 1 + j, (*chip, c), me).wait_recv()
                p_ = cp(i, 4 + j, (*chip, c), sib)
                p_.start()
                passed.append(p_)
        for i in range(n):
            cp(i, 0, sib, me).wait_recv()
            for j, chip in enumerate(chips):
                cp(i, 4 + j, (*chip, 1 - c), me).wait_recv()
        for f in first + passed:
            f.wait_send()
        for m_ in mine:
            m_.wait()

    res = _pcall(
        body, name=name, in_specs=[_ANY] * n, out_specs=[_ANY] * n,
        out_shape=[jax.ShapeDtypeStruct((NDEV,) + tuple(b), a.dtype) for b, (a, _) in zip(blks, items)],
        scratch_shapes=[pltpu.SemaphoreType.DMA((7 * n,)), pltpu.SemaphoreType.DMA((7 * n,)),
                        pltpu.SemaphoreType.DMA((n,))],
    )(*[a for a, _ in items])
    return list(res)


def reduce_scatter_land(grads, name):
    n = len(grads)

    def body(*refs):
        ins, outs = refs[:n], refs[n:2 * n]
        send, recv, lsem = refs[2 * n:]
        x, y, c = _coords()
        me_i = 4 * x + 2 * y + c
        mine = [pltpu.make_async_copy(ins[i].at[me_i], outs[i].at[me_i], lsem.at[i]) for i in range(n)]
        for m_ in mine:
            m_.start()
        sends, recvs = [], []
        for i in range(n):
            for r in range(1, NDEV):
                px = jnp.bitwise_xor(x, (r >> 2) & 1)
                py = jnp.bitwise_xor(y, (r >> 1) & 1)
                pc = jnp.bitwise_xor(c, r & 1)
                p_i = 4 * px + 2 * py + pc
                k = 7 * i + r - 1
                s_ = pltpu.make_async_remote_copy(
                    src_ref=ins[i].at[p_i], dst_ref=outs[i].at[me_i], send_sem=send.at[k], recv_sem=recv.at[k],
                    device_id=(px, py, pc), device_id_type=MESH)
                s_.start()
                sends.append(s_)
                recvs.append(pltpu.make_async_remote_copy(
                    src_ref=ins[i].at[p_i], dst_ref=outs[i].at[p_i], send_sem=send.at[k], recv_sem=recv.at[k],
                    device_id=(px, py, pc), device_id_type=MESH))
        for r_ in recvs:
            r_.wait_recv()
        for s_ in sends:
            s_.wait_send()
        for m_ in mine:
            m_.wait()

    res = _pcall(
        body, name=name, in_specs=[_ANY] * n, out_specs=[_ANY] * n,
        out_shape=[jax.ShapeDtypeStruct(g.shape, g.dtype) for g in grads],
        scratch_shapes=[pltpu.SemaphoreType.DMA((7 * n,)), pltpu.SemaphoreType.DMA((7 * n,)),
                        pltpu.SemaphoreType.DMA((n,))],
    )(*grads)
    return list(res)


_PACK_ALIGN = 8 * LANES


def _pack(arrs):
    parts = []
    for a in arrs:
        f = a.reshape(-1).astype(F32)
        pad = (-f.shape[0]) % _PACK_ALIGN
        parts.append(jnp.pad(f, (0, pad)) if pad else f)
    return jnp.concatenate(parts).reshape(-1, LANES)


def _unpack(p, shapes, lead=()):
    nl = len(lead)
    flat = p.reshape(lead + (-1,))
    out, off = [], 0
    for shp in shapes:
        n = 1
        for d in shp:
            n *= d
        out.append(lax.slice_in_dim(flat, off, off + n, axis=nl).reshape(lead + tuple(shp)))
        off += n + ((-n) % _PACK_ALIGN)
    return out


def _shard_cols(a, me, width, axis):
    return lax.dynamic_slice_in_dim(a, me * width, width, axis=axis)


def kernel(x, c, norm1_g, norm2_g, ada_w, ada_b, mlp_w1, mlp_w2, ab_w_in, sgu_norm_g, sgu_w, sgu_b, conv_w, conv_b, conv_ln_g, conv_ln_b, ab_w_out, mla_w_in, mla_q_norm_g, mla_kv_norm_g, mla_w_uq, mla_w_ukv, mla_q_head_g, mla_k_head_g, mla_w_out, loss_target, m_norm1_g, m_norm2_g, m_ada_w, m_ada_b, m_mlp_w1, m_mlp_w2, m_ab_w_in, m_sgu_norm_g, m_sgu_w, m_sgu_b, m_conv_w, m_conv_b, m_conv_ln_g, m_conv_ln_b, m_ab_w_out, m_mla_w_in, m_mla_q_norm_g, m_mla_kv_norm_g, m_mla_w_uq, m_mla_w_ukv, m_mla_q_head_g, m_mla_k_head_g, m_mla_w_out, v_norm1_g, v_norm2_g, v_ada_w, v_ada_b, v_mlp_w1, v_mlp_w2, v_ab_w_in, v_sgu_norm_g, v_sgu_w, v_sgu_b, v_conv_w, v_conv_b, v_conv_ln_g, v_conv_ln_b, v_ab_w_out, v_mla_w_in, v_mla_q_norm_g, v_mla_kv_norm_g, v_mla_w_uq, v_mla_w_ukv, v_mla_q_head_g, v_mla_k_head_g, v_mla_w_out):
    W = dict(norm1_g=norm1_g, norm2_g=norm2_g, ada_w=ada_w, ada_b=ada_b, mlp_w1=mlp_w1, mlp_w2=mlp_w2, ab_w_in=ab_w_in,
             sgu_norm_g=sgu_norm_g, sgu_w=sgu_w, sgu_b=sgu_b, conv_w=conv_w, conv_b=conv_b, conv_ln_g=conv_ln_g,
             conv_ln_b=conv_ln_b, ab_w_out=ab_w_out, mla_w_in=mla_w_in, mla_q_norm_g=mla_q_norm_g,
             mla_kv_norm_g=mla_kv_norm_g, mla_w_uq=mla_w_uq, mla_w_ukv=mla_w_ukv, mla_q_head_g=mla_q_head_g,
             mla_k_head_g=mla_k_head_g, mla_w_out=mla_w_out)
    M = dict(norm1_g=m_norm1_g, norm2_g=m_norm2_g, ada_w=m_ada_w, ada_b=m_ada_b, mlp_w1=m_mlp_w1, mlp_w2=m_mlp_w2,
             ab_w_in=m_ab_w_in, sgu_norm_g=m_sgu_norm_g, sgu_w=m_sgu_w, sgu_b=m_sgu_b, conv_w=m_conv_w, conv_b=m_conv_b,
             conv_ln_g=m_conv_ln_g, conv_ln_b=m_conv_ln_b, ab_w_out=m_ab_w_out, mla_w_in=m_mla_w_in,
             mla_q_norm_g=m_mla_q_norm_g, mla_kv_norm_g=m_mla_kv_norm_g, mla_w_uq=m_mla_w_uq, mla_w_ukv=m_mla_w_ukv,
             mla_q_head_g=m_mla_q_head_g, mla_k_head_g=m_mla_k_head_g, mla_w_out=m_mla_w_out)
    V = dict(norm1_g=v_norm1_g, norm2_g=v_norm2_g, ada_w=v_ada_w, ada_b=v_ada_b, mlp_w1=v_mlp_w1, mlp_w2=v_mlp_w2,
             ab_w_in=v_ab_w_in, sgu_norm_g=v_sgu_norm_g, sgu_w=v_sgu_w, sgu_b=v_sgu_b, conv_w=v_conv_w, conv_b=v_conv_b,
             conv_ln_g=v_conv_ln_g, conv_ln_b=v_conv_ln_b, ab_w_out=v_ab_w_out, mla_w_in=v_mla_w_in,
             mla_q_norm_g=v_mla_q_norm_g, mla_kv_norm_g=v_mla_kv_norm_g, mla_w_uq=v_mla_w_uq, mla_w_ukv=v_mla_w_ukv,
             mla_q_head_g=v_mla_q_head_g, mla_k_head_g=v_mla_k_head_g, mla_w_out=v_mla_w_out)
    ORDER = list(W)

    S, D = x.shape[1], x.shape[2]
    L, NE, NO = norm1_g.shape[0], ab_w_in.shape[0], mla_w_in.shape[0]
    DA = D // 2
    DB = D - DA
    G = DA // GROUP
    R = NDEV * mla_q_norm_g.shape[1]
    H = NDEV * mla_w_uq.shape[2] // QK
    AW = ada_w.shape[2]
    CB = conv_w.shape[2]
    me = _me()
    xs, tgt = x[0], loss_target[0]

    BIG_EVEN = ("mlp_w1", "mlp_w2", "ab_w_in", "ab_w_out")
    BIG_ODD = ("mlp_w1", "mlp_w2", "mla_w_in", "mla_w_uq", "mla_w_ukv", "mla_w_out")
    BIG = ("mlp_w1", "mlp_w2", "ab_w_in", "ab_w_out", "mla_w_in", "mla_w_uq", "mla_w_ukv", "mla_w_out")
    COL_SHARDED = ("mlp_w1", "ab_w_in", "mla_w_uq", "mla_w_ukv")
    wb = {k: cast_bf16(W[k], "cast_" + k) for k in BIG}

    small_in = [c, mla_q_norm_g, mla_kv_norm_g, conv_w]
    sg = all_gather([(_pack(small_in), None)], "gather_small")[0]
    c_all, qng_all, kvng_all, cw_all = _unpack(sg, [a.shape for a in small_in], (NDEV,))
    c_all = c_all.reshape(NDEV, D)
    qng_full = jnp.transpose(qng_all, (1, 0, 2)).reshape(NO, 1, R)
    kvng_full = jnp.transpose(kvng_all, (1, 0, 2)).reshape(NO, 1, R)
    cw_full = jnp.transpose(cw_all, (1, 2, 0, 3)).reshape(NE, CONV_W, DB)
    cw_pad = jnp.pad(cw_full, ((0, 0), (0, CONV_PAD - CONV_W), (0, 0)))

    def gathered(l):
        names = BIG_EVEN if l % 2 == 0 else BIG_ODD
        idx = {k: (l if k in ("mlp_w1", "mlp_w2") else l // 2) for k in names}
        res = all_gather([(wb[k], idx[k]) for k in names], "gather_w_l%d" % l)
        out = {}
        for k, g_ in zip(names, res):
            out[k] = g_ if k in COL_SHARDED else g_.reshape(NDEV * g_.shape[1], g_.shape[2])
        return out

    wg = [gathered(l) for l in range(L)]

    def silu_fn(t, c_, o):
        v_ = t[0][...]
        o[0][...] = v_ * _sigmoid(v_)
        return []

    c_act = rowwise(silu_fn, [c_all], outs=[((NDEV, D), F32)], ts=NDEV, name="silu_c")[0]
    bias_cols = _shard_cols(ada_b, me, AW, 1).reshape(1, L * AW)
    mod_cols = mm(c_act, ada_w, "nn", name="ada_fwd", b_blocked=True, rowvecs=[bias_cols],
                  epi=lambda acc, b_: (acc + b_,), tm=NDEV, tn=768)
    mod_all = all_gather([(mod_cols, None)], "gather_mod")[0]
    mod = lax.dynamic_index_in_dim(mod_all, me, axis=1, keepdims=False)
    mod = jnp.transpose(mod.reshape(NDEV, L, AW), (1, 0, 2)).reshape(L, 6, 1, D)

    pos = jnp.arange(S, dtype=F32)
    inv = ROPE_THETA ** (-jnp.arange(0, ROPE, 2, dtype=F32) / ROPE)
    ang = pos[:, None] * inv[None, :]
    cos2 = jnp.concatenate([jnp.cos(ang), jnp.cos(ang)], axis=1)
    sinm = jnp.concatenate([-jnp.sin(ang), jnp.sin(ang)], axis=1)

    residual = lambda acc, xr, gt: (acc, xr + gt * acc)

    saved = []
    xc = xs
    for l in range(L):
        sh1, sc1, g1, sh2, sc2, g2 = [mod[l, k] for k in range(6)]
        wl = wg[l]
        tag = "_l%d" % l
        sv = dict(x0=xc)
        h, sv["rstd1"] = prenorm(xc, norm1_g[l][None], sc1, sh1, "prenorm1" + tag)
        sv["h"] = h
        if l % 2 == 0:
            e = l // 2
            ng = sgu_norm_g[e].reshape(1, DA)
            bcol = sgu_b[e][:, :, None]
            proj = mm(h, wl["ab_w_in"], "nn", name="ab_in" + tag, b_blocked=True)
            out_a = sgu_fwd(proj, ng, sgu_w[e], bcol, DA, "sgu_fwd" + tag)
            y1 = conv_fwd(proj, cw_pad[e], conv_b[e][None], DA, DB, "conv_fwd" + tag)
            out_b = ln_silu(y1, conv_ln_g[e][None], conv_ln_b[e][None], "ln_silu" + tag)
            cat = jnp.concatenate([out_a, out_b], axis=1)
            sv.update(proj=proj, y1=y1, cat=cat)
            mixb, x1 = mm(cat, wl["ab_w_out"], "nn", name="ab_out" + tag, out_dtypes=(BF16, F32), epi=residual,
                          extras=[xc], rowvecs=[g1])
        else:
            o_ = l // 2
            proj = mm(h, wl["mla_w_in"], "nn", name="mla_in" + tag)
            kgr = mla_k_head_g[o_][None, NOPE:QK]
            qn, kvn, kr = mla_norms(proj, cos2, sinm, qng_full[o_], kvng_full[o_], kgr, R, "mla_norms" + tag)
            q = mm(qn, wl["mla_w_uq"], "nn", name="mla_uq" + tag, b_blocked=True)
            kv = mm(kvn, wl["mla_w_ukv"], "nn", name="mla_ukv" + tag, b_blocked=True)
            Qh, Kh, Vh = mla_heads(q, kv, kr, cos2, sinm, mla_q_head_g[o_][None], mla_k_head_g[o_][None], H,
                                   "mla_heads" + tag)
            att, lse = attn_fwd(Qh, Kh, Vh, "attn_fwd" + tag)
            sv.update(proj=proj, qn=qn, kvn=kvn, q=q, kv=kv, Qh=Qh, Kh=Kh, Vh=Vh, att=att, lse=lse)
            mixb, x1 = mm(att, wl["mla_w_out"], "nn", name="mla_out" + tag, out_dtypes=(BF16, F32), epi=residual,
                          extras=[xc], rowvecs=[g1])
        sv.update(mixb=mixb, x1=x1)
        h2, sv["rstd2"] = prenorm(x1, norm2_g[l][None], sc2, sh2, "prenorm2" + tag)
        z, act = mm(h2, wl["mlp_w1"], "nn", name="mlp_up" + tag, b_blocked=True, out_dtypes=(BF16, BF16),
                    epi=lambda acc: (acc, jnp.square(jnp.maximum(acc, 0.0))))
        yb, xc = mm(act, wl["mlp_w2"], "nn", name="mlp_down" + tag, out_dtypes=(BF16, F32), epi=residual,
                    extras=[x1], rowvecs=[g2])
        sv.update(h2=h2, z=z, act=act, yb=yb)
        saved.append(sv)

    dx, loss_cols = loss_grad(xc, tgt, "loss")
    loss = lax.psum(0.5 / D * jnp.sum(loss_cols), ("x", "y", "c"))

    big_out = {k: [None] * W[k].shape[0] for k in BIG}
    sm = {k: [None] * W[k].shape[0] for k in ("norm1_g", "norm2_g", "sgu_norm_g", "sgu_w", "sgu_b", "conv_b", "conv_ln_g",
                                               "conv_ln_b", "mla_q_head_g", "mla_k_head_g", "mla_q_norm_g",
                                               "mla_kv_norm_g", "conv_w")}
    dmod = [None] * L
    flat2 = {k: W[k].reshape(-1, W[k].shape[2]) for k in BIG}
    flat2m = {k: M[k].reshape(-1, W[k].shape[2]) for k in BIG}
    flat2v = {k: V[k].reshape(-1, W[k].shape[2]) for k in BIG}

    for l in reversed(range(L)):
        sh1, sc1, g1, sh2, sc2, g2 = [mod[l, k] for k in range(6)]
        wl, sv = wg[l], saved[l]
        tag = "_l%d" % l
        gr = {}
        dy, dgate2 = gate_bwd(dx, sv["yb"], g2, "gate2_bwd" + tag)
        dz = mm(dy, wl["mlp_w2"], "nt", name="mlp_down_dx" + tag, out_dtypes=(BF16,), extras=[sv["z"]],
                epi=lambda acc, z_: (acc * (2.0 * jnp.maximum(z_.astype(F32), 0.0)),))
        gr["mlp_w2"] = mm(sv["act"], dy, "tn", name="mlp_down_dw" + tag, out_dtypes=(BF16,))
        dh2 = mm(dz, wl["mlp_w1"], "nt", name="mlp_up_dx" + tag, b_blocked=True)
        gr["mlp_w1"] = mm(sv["h2"], dz, "tn", name="mlp_up_dw" + tag, out_dtypes=(BF16,), out_blocked=NDEV)
        dx1, dsc2, dsh2, sm["norm2_g"][l] = norm_bwd(dh2, sv["x1"], sv["rstd2"], dx, norm2_g[l][None], sc2,
                                                     "norm2_bwd" + tag)
        dmix, dgate1 = gate_bwd(dx1, sv["mixb"], g1, "gate1_bwd" + tag)
        if l % 2 == 0:
            e = l // 2
            ng = sgu_norm_g[e].reshape(1, DA)
            bcol = sgu_b[e][:, :, None]
            dcat = mm(dmix, wl["ab_w_out"], "nt", name="ab_out_dx" + tag)
            gr["ab_w_out"] = mm(sv["cat"], dmix, "tn", name="ab_out_dw" + tag, out_dtypes=(BF16,))
            dy1, sm["conv_ln_g"][e], sm["conv_ln_b"][e], sm["conv_b"][e] = ln_silu_bwd(
                dcat, sv["y1"], conv_ln_g[e][None], conv_ln_b[e][None], "ln_silu_bwd" + tag)
            da, dg_, dwc = conv_bwd(sv["proj"], dy1, cw_pad[e], DA, DB, "conv_bwd" + tag)
            sm["conv_w"][e] = dwc[:CONV_W]
            duv, dsw, dsb, dsng = sgu_bwd(sv["proj"], dcat, ng, sgu_w[e], bcol, DA, "sgu_bwd" + tag)
            sm["sgu_w"][e], sm["sgu_b"][e], sm["sgu_norm_g"][e] = dsw, dsb, dsng
            dproj = jnp.concatenate([duv, da, dg_], axis=1)
            dh = mm(dproj, wl["ab_w_in"], "nt", name="ab_in_dx" + tag, b_blocked=True)
            gr["ab_w_in"] = mm(sv["h"], dproj, "tn", name="ab_in_dw" + tag, out_dtypes=(BF16,), out_blocked=NDEV)
        else:
            o_ = l // 2
            kgr = mla_k_head_g[o_][None, NOPE:QK]
            dO = mm(dmix, wl["mla_w_out"], "nt", name="mla_out_dx" + tag, out_dtypes=(BF16,))
            gr["mla_w_out"] = mm(sv["att"], dmix, "tn", name="mla_out_dw" + tag, out_dtypes=(BF16,))
            dQ, dK, dV = attn_bwd(sv["Qh"], sv["Kh"], sv["Vh"], dO, sv["att"], sv["lse"], "attn_bwd" + tag)
            dq_pre, dkv_pre, dkr, dqgn, dqgr, dkgn = mla_heads_bwd(
                dQ, dK, dV, sv["q"], sv["kv"], cos2, sinm, mla_q_head_g[o_][None], mla_k_head_g[o_][None], H,
                "mla_heads_bwd" + tag)
            dqn = mm(dq_pre, wl["mla_w_uq"], "nt", name="mla_uq_dx" + tag, b_blocked=True)
            gr["mla_w_uq"] = mm(sv["qn"], dq_pre, "tn", name="mla_uq_dw" + tag, out_dtypes=(BF16,), out_blocked=NDEV)
            dkvn = mm(dkv_pre, wl["mla_w_ukv"], "nt", name="mla_ukv_dx" + tag, b_blocked=True)
            gr["mla_w_ukv"] = mm(sv["kvn"], dkv_pre, "tn", name="mla_ukv_dw" + tag, out_dtypes=(BF16,),
                                 out_blocked=NDEV)
            dproj, sm["mla_q_norm_g"][o_], sm["mla_kv_norm_g"][o_], dkgr = mla_norms_bwd(
                dqn, dkvn, dkr, sv["proj"], qng_full[o_], kvng_full[o_], kgr, R, "mla_norms_bwd" + tag)
            sm["mla_q_head_g"][o_] = jnp.concatenate([dqgn, dqgr], axis=1)
            sm["mla_k_head_g"][o_] = jnp.concatenate([dkgn, dkgr], axis=1)
            dh = mm(dproj, wl["mla_w_in"], "nt", name="mla_in_dx" + tag)
            gr["mla_w_in"] = mm(sv["h"], dproj, "tn", name="mla_in_dw" + tag, out_dtypes=(BF16,))
        dx, dsc1, dsh1, sm["norm1_g"][l] = norm_bwd(dh, sv["x0"], sv["rstd1"], dx1, norm1_g[l][None], sc1,
                                                    "norm1_bwd" + tag)
        dmod[l] = jnp.concatenate([dsh1, dsc1, dgate1, dsh2, dsc2, dgate2], axis=1)

        names = list(gr)
        blocks = [gr[k] if k in COL_SHARDED else gr[k].reshape(NDEV, gr[k].shape[0] // NDEV, gr[k].shape[1])
                  for k in names]
        landed = reduce_scatter_land(blocks, "scatter_g" + tag)
        for k, land in zip(names, landed):
            li = l if k in ("mlp_w1", "mlp_w2") else l // 2
            rows = W[k].shape[1]
            big_out[k][li] = adamw(flat2[k], flat2m[k], flat2v[k], land, row0=li * rows, name="adamw_%s%s" % (k, tag))

    rep = ("norm1_g", "norm2_g", "sgu_norm_g", "sgu_w", "sgu_b", "conv_b", "conv_ln_g", "conv_ln_b", "mla_q_head_g",
           "mla_k_head_g")
    part_full = {"mla_q_norm_g": (NO, R), "mla_kv_norm_g": (NO, R), "conv_w": (NE, CONV_W, DB)}
    parts = [jnp.stack(sm[k]).reshape(W[k].shape) for k in rep]
    parts += [jnp.stack(sm[k]).reshape(part_full[k]) for k in part_full]
    parts.append(jnp.stack(dmod).reshape(L, 6 * D))
    shapes = [p.shape for p in parts]
    gp = all_gather([(_pack(parts), None)], "gather_smallgrads")[0]
    rows_p = gp.shape[1]
    dmod_all = _unpack(gp, shapes, (NDEV,))[-1]
    dmod_cols = _shard_cols(dmod_all, me, AW, 2).reshape(NDEV, L * AW)

    def sum_fn(t, c_, o):
        acc = t[0][0]
        for s_ in range(1, NDEV):
            acc = acc + t[0][s_]
        o[0][...] = acc
        return []

    gsummed = rowwise(sum_fn, [gp], outs=[((rows_p, LANES), F32)], ts=_tile(rows_p, 256, 8), name="sum_smallgrads")[0]
    gsum = dict(zip(list(rep) + list(part_full) + ["ada_b"], _unpack(gsummed, shapes)))
    gsum["mla_q_norm_g"] = _shard_cols(gsum["mla_q_norm_g"], me, R // NDEV, 1)
    gsum["mla_kv_norm_g"] = _shard_cols(gsum["mla_kv_norm_g"], me, R // NDEV, 1)
    gsum["conv_w"] = _shard_cols(gsum["conv_w"], me, CB, 2)
    small = list(rep) + list(part_full) + ["ada_b"]
    sm_shapes = [W[k].shape for k in small]
    sres = adamw(_pack([W[k] for k in small]), _pack([M[k] for k in small]), _pack([V[k] for k in small]),
                 _pack([gsum[k] for k in small])[None], row0=0, name="adamw_small")
    small_out = {k: vals for k, vals in zip(small, zip(*[_unpack(r_, sm_shapes) for r_ in sres]))}

    g_ada = mm(c_act, dmod_cols, "tn", name="ada_dw", out_blocked=L, tm=1024, tn=768, tk=NDEV, cast=None,
               precision=lax.Precision.HIGHEST)
    ada_out = adamw(ada_w.reshape(L * D, AW), m_ada_w.reshape(L * D, AW), v_ada_w.reshape(L * D, AW),
                    g_ada.reshape(1, L * D, AW), row0=0, name="adamw_ada_w")
    ada_out = [a.reshape(L, D, AW) for a in ada_out]

    def result(k, which):
        if k == "ada_w":
            return ada_out[which]
        if k in BIG:
            return jnp.stack([r_[which] for r_ in big_out[k]]).reshape(W[k].shape)
        return small_out[k][which]

    outs = [loss, dx[None]]
    for which in range(4):
        outs += [result(k, which) for k in ORDER]
    return tuple(outs)
```

```python
import functools

import jax
import jax.numpy as jnp
from jax import lax
from jax.experimental import pallas as pl
from jax.experimental.pallas import tpu as pltpu

F32 = jnp.float32
BF16 = jnp.bfloat16
EPS = 1e-6
NDEV = 8
LANES = 128
CHUNK = 128
GROUP = 128
CONV_W = 31
CONV_PAD = 32
NOPE, ROPE, VDIM = 128, 64, 128
QK = NOPE + ROPE
ROPE_THETA = 10000.0
VMEM_LIMIT = 56 * 1024 * 1024
ADAM_LR, ADAM_B1, ADAM_B2, ADAM_EPS, ADAM_WD, ADAM_STEP = 0.001, 0.9, 0.999, 1e-08, 0.01, 10
MESH = pl.DeviceIdType.MESH
NEG = -1e30


def _pcall(body, **kw):
    return pl.pallas_call(body, **kw)


def _params(sem=None):
    return pltpu.CompilerParams(dimension_semantics=sem, vmem_limit_bytes=VMEM_LIMIT)


def _tile(dim, target, align=LANES):
    if dim <= target:
        return dim
    t = (target // align) * align
    while t >= align:
        if dim % t == 0:
            return t
        t -= align
    return dim


def _rstd(x):
    return lax.rsqrt(jnp.mean(x * x, axis=-1, keepdims=True) + EPS)


def _sigmoid(x):
    return 1.0 / (1.0 + jnp.exp(-x))


_GC = 0.7978845608028654


def _gelu(x):
    return 0.5 * x * (1.0 + jnp.tanh(_GC * (x + 0.044715 * x * x * x)))


def _gelu_grad(x):
    t = jnp.tanh(_GC * (x + 0.044715 * x * x * x))
    return 0.5 * (1.0 + t) + 0.5 * x * (1.0 - t * t) * _GC * (1.0 + 3 * 0.044715 * x * x)


def _colsum(x):
    return jnp.sum(x, axis=0, keepdims=True)


def _rms_bwd(dy, xhat, rstd, g):
    dxh = dy * g
    dx = rstd * (dxh - xhat * jnp.mean(dxh * xhat, axis=-1, keepdims=True))
    return dx, _colsum(dy * xhat)


def _swap_halves(x):
    h = x.shape[-1] // 2
    return jnp.concatenate([x[:, h:], x[:, :h]], axis=1)


def _rope(x, cos2, sinm):
    return x * cos2 + _swap_halves(x) * sinm


def _unrope(dy, cos2, sinm):
    return dy * cos2 + _swap_halves(dy * sinm)


_DIMS = {"nn": (((1,), (0,)), ((), ())), "nt": (((1,), (1,)), ((), ())), "tn": (((0,), (0,)), ((), ()))}


def mm(a, b, mode, *, name, out_dtypes=(F32,), epi=None, extras=(), rowvecs=(), b_blocked=False, out_blocked=0,
       tm=1024, tn=1024, tk=512, precision=None, cast=BF16):
    if mode == "tn":
        K, M = a.shape
    else:
        M, K = a.shape
    if b_blocked:
        J, Rb, Cb = b.shape
        N = Rb if mode == "nt" else J * Cb
    else:
        N = b.shape[0] if mode == "nt" else b.shape[1]
    tm = _tile(M, tm)
    if mode == "nn" and b_blocked:
        tn = _tile(Cb, tn)
    elif out_blocked:
        tn = _tile(N // out_blocked, tn)
    else:
        tn = _tile(N, tn)
    tk = _tile(Cb, tk) if (mode == "nt" and b_blocked) else _tile(K, tk)
    nk = K // tk
    grid = (M // tm, N // tn, nk)

    if mode == "tn":
        a_spec = pl.BlockSpec((tk, tm), lambda i, j, k: (k, i))
    else:
        a_spec = pl.BlockSpec((tm, tk), lambda i, j, k: (i, k))
    if mode == "nn":
        if b_blocked:
            nper = Cb // tn
            b_spec = pl.BlockSpec((None, tk, tn), lambda i, j, k: (j // nper, k, j % nper))
        else:
            b_spec = pl.BlockSpec((tk, tn), lambda i, j, k: (k, j))
    elif mode == "nt":
        if b_blocked:
            kper = Cb // tk
            b_spec = pl.BlockSpec((None, tn, tk), lambda i, j, k: (k // kper, j, k % kper))
        else:
            b_spec = pl.BlockSpec((tn, tk), lambda i, j, k: (j, k))
    else:
        b_spec = pl.BlockSpec((tk, tn), lambda i, j, k: (k, j))
    if out_blocked:
        oper = (N // out_blocked) // tn
        o_spec = pl.BlockSpec((None, tm, tn), lambda i, j, k: (j // oper, i, j % oper))
        o_shape = (out_blocked, M, N // out_blocked)
    else:
        o_spec = pl.BlockSpec((tm, tn), lambda i, j, k: (i, j))
        o_shape = (M, N)
    e_spec = pl.BlockSpec((tm, tn), lambda i, j, k: (i, j))
    r_spec = pl.BlockSpec((1, tn), lambda i, j, k: (0, j))
    ne, nr, no = len(extras), len(rowvecs), len(out_dtypes)
    dims = _DIMS[mode]

    def body(a_ref, b_ref, *rest):
        ex = rest[:ne]
        rv = rest[ne:ne + nr]
        outs = rest[ne + nr:ne + nr + no]
        acc = rest[ne + nr + no]
        k = pl.program_id(2)

        @pl.when(k == 0)
        def _():
            acc[...] = jnp.zeros_like(acc)

        av, bv = a_ref[...], b_ref[...]
        if cast is not None:
            av, bv = av.astype(cast), bv.astype(cast)
        acc[...] += lax.dot_general(av, bv, dims, preferred_element_type=F32, precision=precision)

        @pl.when(k == nk - 1)
        def _():
            r = acc[...]
            vals = (r,) if epi is None else epi(r, *[e[...] for e in ex], *[v[...] for v in rv])
            for o, val in zip(outs, vals):
                o[...] = val.astype(o.dtype)

    res = _pcall(
        body, name=name, grid=grid,
        in_specs=[a_spec, b_spec] + [e_spec] * ne + [r_spec] * nr,
        out_specs=[o_spec] * no,
        out_shape=[jax.ShapeDtypeStruct(o_shape, dt) for dt in out_dtypes],
        scratch_shapes=[pltpu.VMEM((tm, tn), F32)],
        compiler_params=_params(("parallel", "parallel", "arbitrary")),
    )(a, b, *extras, *rowvecs)
    return res[0] if no == 1 else res


def rowwise(fn, tiled, consts=(), outs=(), reds=(), *, ts, name):
    specs = []
    arrs = []
    rows = None
    for t in tiled:
        a, w, cb = t if isinstance(t, tuple) else (t, None, 0)
        arrs.append(a)
        rows = a.shape[-2] if rows is None else rows
        if a.ndim == 2:
            specs.append(pl.BlockSpec((ts, a.shape[1] if w is None else w), lambda i, cb=cb: (i, cb)))
        else:
            specs.append(pl.BlockSpec((a.shape[0], ts, a.shape[2]), lambda i: (0, i, 0)))
    for a in consts:
        specs.append(pl.BlockSpec(a.shape, lambda i, n=a.ndim: (0,) * n))
    o_specs, o_shapes = [], []
    for shp, dt in outs:
        if len(shp) == 2:
            o_specs.append(pl.BlockSpec((ts, shp[1]), lambda i: (i, 0)))
        else:
            o_specs.append(pl.BlockSpec((shp[0], ts, shp[2]), lambda i: (0, i, 0)))
        o_shapes.append(jax.ShapeDtypeStruct(shp, dt))
    for shp in reds:
        o_specs.append(pl.BlockSpec(shp, lambda i, n=len(shp): (0,) * n))
        o_shapes.append(jax.ShapeDtypeStruct(shp, F32))
    nt, nc, no = len(arrs), len(consts), len(outs)

    def body(*refs):
        i = pl.program_id(0)
        red_refs = refs[nt + nc + no:]
        vals = fn(refs[:nt], refs[nt:nt + nc], refs[nt + nc:nt + nc + no])
        if red_refs:
            @pl.when(i == 0)
            def _():
                for r in red_refs:
                    r[...] = jnp.zeros_like(r)
            for r, v in zip(red_refs, vals):
                r[...] += v

    return _pcall(body, name=name, grid=(rows // ts,), in_specs=specs, out_specs=o_specs, out_shape=o_shapes,
                  compiler_params=_params(("arbitrary",)))(*arrs, *consts)


def cast_bf16(w, l, name):
    _, R, C = w.shape
    tr = _tile(R, 512, 16)

    def body(w_ref, o_ref):
        o_ref[...] = w_ref[...].astype(BF16)

    return _pcall(body, name=name, grid=(R // tr,), in_specs=[pl.BlockSpec((None, tr, C), lambda i: (l, i, 0))],
                  out_specs=pl.BlockSpec((tr, C), lambda i: (i, 0)), out_shape=jax.ShapeDtypeStruct((R, C), BF16),
                  compiler_params=_params(("parallel",)))(w)


def prenorm(x, g, scale, shift, name):
    S, D = x.shape

    def fn(t, c, o):
        xv = t[0][...]
        r = _rstd(xv)
        o[0][...] = ((xv * r * c[0][...]) * (1.0 + c[1][...]) + c[2][...]).astype(BF16)
        o[1][...] = r
        return []

    return rowwise(fn, [x], [g, scale, shift], [((S, D), BF16), ((S, 1), F32)], ts=_tile(S, 128, 16), name=name)


def gate_bwd(dx, y, gate, name):
    S, D = dx.shape

    def fn(t, c, o):
        d = t[0][...]
        o[0][...] = (d * c[0][...]).astype(BF16)
        return [_colsum(d * t[1][...].astype(F32))]

    return rowwise(fn, [dx, y], [gate], [((S, D), BF16)], [(1, D)], ts=_tile(S, 128, 16), name=name)


def norm_bwd(dh, x, rstd, dres, g, scale, name):
    S, D = x.shape

    def fn(t, c, o):
        d = t[0][...]
        r = t[2][...]
        xh = t[1][...] * r
        gv = c[0][...]
        dr = d * (1.0 + c[1][...])
        dx, dg = _rms_bwd(dr, xh, r, gv)
        o[0][...] = t[3][...] + dx
        return [_colsum(d * (xh * gv)), _colsum(d), dg]

    return rowwise(fn, [dh, x, rstd, dres], [g, scale], [((S, D), F32)], [(1, D)] * 3, ts=_tile(S, 128, 8), name=name)


def loss_grad(y, tgt, name):
    S, D = y.shape

    def fn(t, c, o):
        e = t[0][...] - t[1][...]
        o[0][...] = e * (1.0 / D)
        return [_colsum(e * e)]

    return rowwise(fn, [y, tgt], outs=[((S, D), F32)], reds=[(1, D)], ts=_tile(S, 128, 8), name=name)


def _tril_mask():
    r = lax.broadcasted_iota(jnp.int32, (CHUNK, CHUNK), 0)
    c = lax.broadcasted_iota(jnp.int32, (CHUNK, CHUNK), 1)
    return c <= r


def sgu_fwd(proj, ng, w, bcol, DA, name):
    S = proj.shape[0]
    G = DA // GROUP
    tr = _tile(S, 2 * CHUNK)

    def body(u_ref, v_ref, ng_ref, w_ref, b_ref, o_ref):
        mask = _tril_mask()
        for g in range(G):
            cols = slice(g * GROUP, (g + 1) * GROUP)
            wm = jnp.where(mask, w_ref[g], 0.0).astype(BF16)
            for ci in range(tr // CHUNK):
                rows = slice(ci * CHUNK, (ci + 1) * CHUNK)
                gv = _gelu(v_ref[rows, cols])
                vn = gv * _rstd(gv) * ng_ref[:, cols]
                mixed = jnp.dot(wm, vn.astype(BF16), preferred_element_type=F32) + b_ref[g]
                o_ref[rows, cols] = (_gelu(u_ref[rows, cols]) * mixed).astype(o_ref.dtype)

    return _pcall(
        body, name=name, grid=(S // tr,),
        in_specs=[pl.BlockSpec((tr, DA), lambda i: (i, 0)), pl.BlockSpec((tr, DA), lambda i: (i, 1)),
                  pl.BlockSpec((1, DA), lambda i: (0, 0)), pl.BlockSpec((G, CHUNK, CHUNK), lambda i: (0, 0, 0)),
                  pl.BlockSpec((G, CHUNK, 1), lambda i: (0, 0, 0))],
        out_specs=pl.BlockSpec((tr, DA), lambda i: (i, 0)),
        out_shape=jax.ShapeDtypeStruct((S, DA), BF16),
        compiler_params=_params(("parallel",)),
    )(proj, proj, ng, w, bcol)


def sgu_bwd(proj, dcat, ng, w, bcol, DA, name):
    S = proj.shape[0]
    G = DA // GROUP
    tr = _tile(S, 2 * CHUNK)
    nsteps = S // tr

    def body(u_ref, v_ref, d_ref, ng_ref, w_ref, b_ref, duv_ref, dw_ref, db_ref, dng_ref, dbacc):
        i = pl.program_id(0)

        @pl.when(i == 0)
        def _():
            dw_ref[...] = jnp.zeros_like(dw_ref)
            dng_ref[...] = jnp.zeros_like(dng_ref)
            dbacc[...] = jnp.zeros_like(dbacc)

        mask = _tril_mask()
        for g in range(G):
            cols = slice(g * GROUP, (g + 1) * GROUP)
            wm = jnp.where(mask, w_ref[g], 0.0).astype(BF16)
            ngg = ng_ref[:, cols]
            for ci in range(tr // CHUNK):
                rows = slice(ci * CHUNK, (ci + 1) * CHUNK)
                u, v, d = u_ref[rows, cols], v_ref[rows, cols], d_ref[rows, cols]
                gv = _gelu(v)
                rs = _rstd(gv)
                vhat = gv * rs
                vn = (vhat * ngg).astype(BF16)
                mixed = jnp.dot(wm, vn, preferred_element_type=F32) + b_ref[g]
                dmixed = d * _gelu(u)
                dmb = dmixed.astype(BF16)
                duv_ref[rows, cols] = (d * mixed * _gelu_grad(u)).astype(duv_ref.dtype)
                dwg = lax.dot_general(dmb, vn, _DIMS["nt"], preferred_element_type=F32)
                dw_ref[g] += jnp.where(mask, dwg, 0.0)
                dbacc[g] += dmixed
                dvn = lax.dot_general(wm, dmb, _DIMS["tn"], preferred_element_type=F32)
                dgv, dngg = _rms_bwd(dvn, vhat, rs, ngg)
                dng_ref[:, cols] += dngg
                duv_ref[rows, DA + g * GROUP:DA + (g + 1) * GROUP] = (dgv * _gelu_grad(v)).astype(duv_ref.dtype)

        @pl.when(i == nsteps - 1)
        def _():
            for g in range(G):
                db_ref[g] = jnp.sum(dbacc[g], axis=-1, keepdims=True)

    return _pcall(
        body, name=name, grid=(nsteps,),
        in_specs=[pl.BlockSpec((tr, DA), lambda i: (i, 0)), pl.BlockSpec((tr, DA), lambda i: (i, 1)),
                  pl.BlockSpec((tr, DA), lambda i: (i, 0)),
                  pl.BlockSpec((1, DA), lambda i: (0, 0)), pl.BlockSpec((G, CHUNK, CHUNK), lambda i: (0, 0, 0)),
                  pl.BlockSpec((G, CHUNK, 1), lambda i: (0, 0, 0))],
        out_specs=[pl.BlockSpec((tr, 2 * DA), lambda i: (i, 0)), pl.BlockSpec((G, CHUNK, CHUNK), lambda i: (0, 0, 0)),
                   pl.BlockSpec((G, CHUNK, 1), lambda i: (0, 0, 0)), pl.BlockSpec((1, DA), lambda i: (0, 0))],
        out_shape=[jax.ShapeDtypeStruct((S, 2 * DA), BF16), jax.ShapeDtypeStruct((G, CHUNK, CHUNK), F32),
                   jax.ShapeDtypeStruct((G, CHUNK, 1), F32), jax.ShapeDtypeStruct((1, DA), F32)],
        scratch_shapes=[pltpu.VMEM((G, CHUNK, CHUNK), F32)],
        compiler_params=_params(("arbitrary",)),
    )(proj, proj, dcat, ng, w, bcol)


def _conv_tile(S):
    return _tile(S, 256, 8)


def conv_fwd(proj, wk, bias, DA, DB, name):
    S = proj.shape[0]
    nb = DB // LANES
    a0, g0 = 2 * DA // LANES, (2 * DA + DB) // LANES
    T = _conv_tile(S)
    off = CONV_PAD - (CONV_W - 1)

    def body(a_ref, g_ref, w_ref, b_ref, o_ref, ypad):
        ypad[0:CONV_PAD, :] = jnp.zeros((CONV_PAD, LANES), F32)

        def fill(t, cr):
            r = pl.multiple_of(t * T, T)
            ypad[pl.ds(CONV_PAD + r, T), :] = a_ref[pl.ds(r, T), :] * _sigmoid(g_ref[pl.ds(r, T), :])
            return cr

        lax.fori_loop(0, S // T, fill, 0)

        def step(t, cr):
            r = pl.multiple_of(t * T, T)
            acc = jnp.zeros((T, LANES), F32) + b_ref[...]
            for k in range(CONV_W):
                acc = acc + w_ref[k:k + 1, :] * ypad[pl.ds(r + (k + off), T), :]
            o_ref[pl.ds(r, T), :] = acc
            return cr

        lax.fori_loop(0, S // T, step, 0)

    return _pcall(
        body, name=name, grid=(nb,),
        in_specs=[pl.BlockSpec((S, LANES), lambda j: (0, a0 + j)), pl.BlockSpec((S, LANES), lambda j: (0, g0 + j)),
                  pl.BlockSpec((CONV_PAD, LANES), lambda j: (0, j)), pl.BlockSpec((1, LANES), lambda j: (0, j))],
        out_specs=pl.BlockSpec((S, LANES), lambda j: (0, j)),
        out_shape=jax.ShapeDtypeStruct((S, DB), F32),
        scratch_shapes=[pltpu.VMEM((S + CONV_PAD, LANES), F32)],
        compiler_params=_params(("parallel",)),
    )(proj, proj, wk, bias)


def conv_bwd(proj, dy1, wk, DA, DB, name):
    S = proj.shape[0]
    nb = DB // LANES
    a0, g0 = 2 * DA // LANES, (2 * DA + DB) // LANES
    T = _conv_tile(S)
    off = CONV_PAD - (CONV_W - 1)

    def body(a_ref, g_ref, d_ref, w_ref, da_ref, dg_ref, dw_ref, ypad, dpad, wacc):
        ypad[0:CONV_PAD, :] = jnp.zeros((CONV_PAD, LANES), F32)
        dpad[S:S + CONV_PAD, :] = jnp.zeros((CONV_PAD, LANES), F32)
        wacc[...] = jnp.zeros_like(wacc)

        def fill(t, cr):
            r = pl.multiple_of(t * T, T)
            ypad[pl.ds(CONV_PAD + r, T), :] = a_ref[pl.ds(r, T), :] * _sigmoid(g_ref[pl.ds(r, T), :])
            dpad[pl.ds(r, T), :] = d_ref[pl.ds(r, T), :]
            return cr

        lax.fori_loop(0, S // T, fill, 0)

        def step(t, cr):
            r = pl.multiple_of(t * T, T)
            dt = dpad[pl.ds(r, T), :]
            dy0 = jnp.zeros((T, LANES), F32)
            for k in range(CONV_W):
                prod = dt * ypad[pl.ds(r + (k + off), T), :]
                wacc[k] += jnp.sum(prod.reshape(T // 8, 8, LANES), axis=0)
                dy0 = dy0 + w_ref[k:k + 1, :] * dpad[pl.ds(r + (CONV_W - 1 - k), T), :]
            av, gv = a_ref[pl.ds(r, T), :], g_ref[pl.ds(r, T), :]
            sg = _sigmoid(gv)
            da_ref[pl.ds(r, T), :] = (dy0 * sg).astype(da_ref.dtype)
            dg_ref[pl.ds(r, T), :] = (dy0 * av * sg * (1.0 - sg)).astype(dg_ref.dtype)
            return cr

        lax.fori_loop(0, S // T, step, 0)
        for k in range(CONV_W):
            dw_ref[k:k + 1, :] = jnp.sum(wacc[k], axis=0, keepdims=True)
        dw_ref[CONV_W:CONV_PAD, :] = jnp.zeros((CONV_PAD - CONV_W, LANES), F32)

    return _pcall(
        body, name=name, grid=(nb,),
        in_specs=[pl.BlockSpec((S, LANES), lambda j: (0, a0 + j)), pl.BlockSpec((S, LANES), lambda j: (0, g0 + j)),
                  pl.BlockSpec((S, LANES), lambda j: (0, j)), pl.BlockSpec((CONV_PAD, LANES), lambda j: (0, j))],
        out_specs=[pl.BlockSpec((S, LANES), lambda j: (0, j)), pl.BlockSpec((S, LANES), lambda j: (0, j)),
                   pl.BlockSpec((CONV_PAD, LANES), lambda j: (0, j))],
        out_shape=[jax.ShapeDtypeStruct((S, DB), BF16), jax.ShapeDtypeStruct((S, DB), BF16),
                   jax.ShapeDtypeStruct((CONV_PAD, DB), F32)],
        scratch_shapes=[pltpu.VMEM((S + CONV_PAD, LANES), F32), pltpu.VMEM((S + CONV_PAD, LANES), F32),
                        pltpu.VMEM((CONV_PAD, 8, LANES), F32)],
        compiler_params=_params(("parallel",)),
    )(proj, proj, dy1, wk)


def _ln_stats(y):
    mu = jnp.mean(y, axis=-1, keepdims=True)
    yc = y - mu
    rs = lax.rsqrt(jnp.mean(yc * yc, axis=-1, keepdims=True) + EPS)
    return yc * rs, rs


def ln_silu(y1, lg, lb, name):
    S, DB = y1.shape

    def fn(t, c, o):
        yh, _ = _ln_stats(t[0][...])
        ln = yh * c[0][...] + c[1][...]
        o[0][...] = (ln * _sigmoid(ln)).astype(BF16)
        return []

    return rowwise(fn, [y1], [lg, lb], [((S, DB), BF16)], ts=_tile(S, 128, 16), name=name)[0]


def ln_silu_bwd(dcat, y1, lg, lb, name):
    S, DB = y1.shape
    cb = (dcat.shape[1] - DB) // DB

    def fn(t, c, o):
        yh, rs = _ln_stats(t[1][...])
        gv = c[0][...]
        ln = yh * gv + c[1][...]
        sg = _sigmoid(ln)
        dln = t[0][...] * (sg * (1.0 + ln * (1.0 - sg)))
        dyh = dln * gv
        dy = rs * (dyh - jnp.mean(dyh, axis=-1, keepdims=True) - yh * jnp.mean(dyh * yh, axis=-1, keepdims=True))
        o[0][...] = dy
        return [_colsum(dln * yh), _colsum(dln), _colsum(dy)]

    return rowwise(fn, [(dcat, DB, cb), y1], [lg, lb], [((S, DB), F32)], [(1, DB)] * 3, ts=_tile(S, 128, 8), name=name)


def mla_norms(proj, cos2, sinm, qg, kvg, kgr, R, name):
    S = proj.shape[0]

    def fn(t, c, o):
        cq = t[0][:, 0:R]
        ckv = t[0][:, R:2 * R]
        kr = t[0][:, 2 * R:2 * R + ROPE]
        o[0][...] = (cq * _rstd(cq) * c[0][...]).astype(BF16)
        o[1][...] = (ckv * _rstd(ckv) * c[1][...]).astype(BF16)
        o[2][...] = _rope(kr * _rstd(kr) * c[2][...], t[1][...], t[2][...])
        return []

    return rowwise(fn, [proj, cos2, sinm], [qg, kvg, kgr], [((S, R), BF16), ((S, R), BF16), ((S, ROPE), F32)],
                   ts=_tile(S, 128, 16), name=name)


def mla_heads(q, kv, kr, cos2, sinm, qg, kg, H, name):
    S = q.shape[0]

    def fn(t, c, o):
        cs, sn = t[3][...], t[4][...]
        krv = t[2][...]
        qgn, qgr, kgn = c[0][:, 0:NOPE], c[0][:, NOPE:QK], c[1][:, 0:NOPE]
        for h in range(H):
            qn = t[0][:, QK * h:QK * h + NOPE]
            qr = t[0][:, QK * h + NOPE:QK * (h + 1)]
            o[0][h, :, 0:NOPE] = (qn * _rstd(qn) * qgn).astype(BF16)
            o[0][h, :, NOPE:QK] = _rope(qr * _rstd(qr) * qgr, cs, sn).astype(BF16)
            kn = t[1][:, (NOPE + VDIM) * h:(NOPE + VDIM) * h + NOPE]
            o[1][h, :, 0:NOPE] = (kn * _rstd(kn) * kgn).astype(BF16)
            o[1][h, :, NOPE:QK] = krv.astype(BF16)
            o[2][h] = t[1][:, (NOPE + VDIM) * h + NOPE:(NOPE + VDIM) * (h + 1)].astype(BF16)
        return []

    return rowwise(fn, [q, kv, kr, cos2, sinm], [qg, kg],
                   [((H, S, QK), BF16), ((H, S, QK), BF16), ((H, S, VDIM), BF16)], ts=_tile(S, 128, 16), name=name)


def mla_heads_bwd(dQ, dK, dV, q, kv, cos2, sinm, qg, kg, H, name):
    S = q.shape[0]
    KV = NOPE + VDIM

    def fn(t, c, o):
        cs, sn = t[5][...], t[6][...]
        qgn, qgr, kgn = c[0][:, 0:NOPE], c[0][:, NOPE:QK], c[1][:, 0:NOPE]
        a_qn = jnp.zeros((1, NOPE), F32)
        a_qr = jnp.zeros((1, ROPE), F32)
        a_kn = jnp.zeros((1, NOPE), F32)
        dkr = jnp.zeros((t[0].shape[1], ROPE), F32)
        for h in range(H):
            qn = t[3][:, QK * h:QK * h + NOPE]
            rs = _rstd(qn)
            dx, dg = _rms_bwd(t[0][h, :, 0:NOPE], qn * rs, rs, qgn)
            o[0][:, QK * h:QK * h + NOPE] = dx.astype(BF16)
            a_qn = a_qn + dg
            qr = t[3][:, QK * h + NOPE:QK * (h + 1)]
            rs = _rstd(qr)
            dx, dg = _rms_bwd(_unrope(t[0][h, :, NOPE:QK], cs, sn), qr * rs, rs, qgr)
            o[0][:, QK * h + NOPE:QK * (h + 1)] = dx.astype(BF16)
            a_qr = a_qr + dg
            kn = t[4][:, KV * h:KV * h + NOPE]
            rs = _rstd(kn)
            dx, dg = _rms_bwd(t[1][h, :, 0:NOPE], kn * rs, rs, kgn)
            o[1][:, KV * h:KV * h + NOPE] = dx.astype(BF16)
            a_kn = a_kn + dg
            o[1][:, KV * h + NOPE:KV * (h + 1)] = t[2][h].astype(BF16)
            dkr = dkr + t[1][h, :, NOPE:QK]
        o[2][...] = _unrope(dkr, cs, sn)
        return [a_qn, a_qr, a_kn]

    return rowwise(fn, [dQ, dK, dV, q, kv, cos2, sinm], [qg, kg],
                   [((S, H * QK), BF16), ((S, H * KV), BF16), ((S, ROPE), F32)],
                   [(1, NOPE), (1, ROPE), (1, NOPE)], ts=_tile(S, 128, 16), name=name)


def mla_norms_bwd(dqn, dkvn, dkr, proj, qg, kvg, kgr, R, name):
    S = proj.shape[0]

    def fn(t, c, o):
        reds = []
        for idx, (lo, hi) in enumerate(((0, R), (R, 2 * R), (2 * R, 2 * R + ROPE))):
            xv = t[3][:, lo:hi]
            rs = _rstd(xv)
            dx, dg = _rms_bwd(t[idx][...], xv * rs, rs, c[idx][...])
            o[0][:, lo:hi] = dx.astype(BF16)
            reds.append(dg)
        return reds

    return rowwise(fn, [dqn, dkvn, dkr, proj], [qg, kvg, kgr], [((S, 2 * R + ROPE), BF16)],
                   [(1, R), (1, R), (1, ROPE)], ts=_tile(S, 128, 16), name=name)


def _causal_mask(qi, ki, tq, tk):
    r = qi * tq + lax.broadcasted_iota(jnp.int32, (tq, tk), 0)
    c = ki * tk + lax.broadcasted_iota(jnp.int32, (tq, tk), 1)
    return c <= r


def attn_fwd(Q, K, V, name):
    H, S, _ = Q.shape
    t = _tile(S, 512)
    n = S // t
    scale = QK ** -0.5

    def body(q_ref, k_ref, v_ref, o_ref, lse_ref, m_s, l_s, acc):
        qi, ki = pl.program_id(1), pl.program_id(2)

        @pl.when(ki == 0)
        def _():
            m_s[...] = jnp.full_like(m_s, NEG)
            l_s[...] = jnp.zeros_like(l_s)
            acc[...] = jnp.zeros_like(acc)

        @pl.when(ki <= qi)
        def _():
            s = lax.dot_general(q_ref[...], k_ref[...], _DIMS["nt"], preferred_element_type=F32) * scale
            s = jnp.where(_causal_mask(qi, ki, t, t), s, NEG)
            m_new = jnp.maximum(m_s[...], jnp.max(s, axis=-1, keepdims=True))
            alpha = jnp.exp(m_s[...] - m_new)
            p = jnp.exp(s - m_new)
            l_s[...] = alpha * l_s[...] + jnp.sum(p, axis=-1, keepdims=True)
            acc[...] = alpha * acc[...] + jnp.dot(p.astype(BF16), v_ref[...], preferred_element_type=F32)
            m_s[...] = m_new

        @pl.when(ki == n - 1)
        def _():
            o_ref[...] = (acc[...] / l_s[...]).astype(o_ref.dtype)
            lse_ref[...] = m_s[...] + jnp.log(l_s[...])

    return _pcall(
        body, name=name, grid=(H, n, n),
        in_specs=[pl.BlockSpec((None, t, QK), lambda h, qi, ki: (h, qi, 0)),
                  pl.BlockSpec((None, t, QK), lambda h, qi, ki: (h, jnp.minimum(ki, qi), 0)),
                  pl.BlockSpec((None, t, VDIM), lambda h, qi, ki: (h, jnp.minimum(ki, qi), 0))],
        out_specs=[pl.BlockSpec((t, VDIM), lambda h, qi, ki: (qi, h)),
                   pl.BlockSpec((None, t, 1), lambda h, qi, ki: (h, qi, 0))],
        out_shape=[jax.ShapeDtypeStruct((S, H * VDIM), BF16), jax.ShapeDtypeStruct((H, S, 1), F32)],
        scratch_shapes=[pltpu.VMEM((t, 1), F32), pltpu.VMEM((t, 1), F32), pltpu.VMEM((t, VDIM), F32)],
        compiler_params=_params(("parallel", "parallel", "arbitrary")),
    )(Q, K, V)


def attn_bwd(Q, K, V, dO, O, lse, name):
    H, S, _ = Q.shape
    t = _tile(S, 512)
    n = S // t
    scale = QK ** -0.5

    def body(q_ref, k_ref, v_ref, do_ref, o_ref, lse_ref, dq_ref, dk_ref, dv_ref):
        ki, qi = pl.program_id(1), pl.program_id(2)

        @pl.when((ki == 0) & (qi == 0))
        def _():
            dq_ref[...] = jnp.zeros_like(dq_ref)

        @pl.when(qi == 0)
        def _():
            dk_ref[...] = jnp.zeros_like(dk_ref)
            dv_ref[...] = jnp.zeros_like(dv_ref)

        @pl.when(qi >= ki)
        def _():
            q, k, v, do = q_ref[...], k_ref[...], v_ref[...], do_ref[...]
            s = lax.dot_general(q, k, _DIMS["nt"], preferred_element_type=F32) * scale
            s = jnp.where(_causal_mask(qi, ki, t, t), s, NEG)
            p = jnp.exp(s - lse_ref[...])
            delta = jnp.sum(do.astype(F32) * o_ref[...].astype(F32), axis=-1, keepdims=True)
            dv_ref[...] += lax.dot_general(p.astype(BF16), do, _DIMS["tn"], preferred_element_type=F32)
            dp = lax.dot_general(do, v, _DIMS["nt"], preferred_element_type=F32)
            ds = (p * (dp - delta) * scale).astype(BF16)
            dk_ref[...] += lax.dot_general(ds, q, _DIMS["tn"], preferred_element_type=F32)
            r = pl.multiple_of(qi * t, t)
            dq_ref[pl.ds(r, t), :] += jnp.dot(ds, k, preferred_element_type=F32)

    qmap = lambda h, ki, qi: (h, jnp.maximum(qi, ki), 0)
    return _pcall(
        body, name=name, grid=(H, n, n),
        in_specs=[pl.BlockSpec((None, t, QK), qmap),
                  pl.BlockSpec((None, t, QK), lambda h, ki, qi: (h, ki, 0)),
                  pl.BlockSpec((None, t, VDIM), lambda h, ki, qi: (h, ki, 0)),
                  pl.BlockSpec((t, VDIM), lambda h, ki, qi: (jnp.maximum(qi, ki), h)),
                  pl.BlockSpec((t, VDIM), lambda h, ki, qi: (jnp.maximum(qi, ki), h)),
                  pl.BlockSpec((None, t, 1), qmap)],
        out_specs=[pl.BlockSpec((None, S, QK), lambda h, ki, qi: (h, 0, 0)),
                   pl.BlockSpec((None, t, QK), lambda h, ki, qi: (h, ki, 0)),
                   pl.BlockSpec((None, t, VDIM), lambda h, ki, qi: (h, ki, 0))],
        out_shape=[jax.ShapeDtypeStruct((H, S, QK), F32), jax.ShapeDtypeStruct((H, S, QK), F32),
                   jax.ShapeDtypeStruct((H, S, VDIM), F32)],
        compiler_params=_params(("parallel", "arbitrary", "arbitrary")),
    )(Q, K, V, dO, O, lse)


def adamw(w, m, v, g, *, row0, name):
    P, rows, C = g.shape
    tr = _tile(rows, max(16, 131072 // C), 16)
    off = row0 // tr
    assert row0 % tr == 0
    bc1 = 1.0 - ADAM_B1 ** ADAM_STEP
    bc2 = 1.0 - ADAM_B2 ** ADAM_STEP

    def body(w_ref, m_ref, v_ref, g_ref, go_ref, d_ref, mo_ref, vo_ref):
        gs = g_ref[0].astype(F32)
        for p in range(1, P):
            gs = gs + g_ref[p].astype(F32)
        wv = w_ref[...]
        mn = ADAM_B1 * m_ref[...] + (1.0 - ADAM_B1) * gs
        vn = ADAM_B2 * v_ref[...] + (1.0 - ADAM_B2) * (gs * gs)
        go_ref[...] = gs
        mo_ref[...] = mn
        vo_ref[...] = vn
        d_ref[...] = -ADAM_LR * ((mn / bc1) / (jnp.sqrt(vn / bc2) + ADAM_EPS) + ADAM_WD * wv)

    wspec = pl.BlockSpec((tr, C), lambda i: (i + off, 0))
    ospec = pl.BlockSpec((tr, C), lambda i: (i, 0))
    return _pcall(
        body, name=name, grid=(rows // tr,),
        in_specs=[wspec, wspec, wspec, pl.BlockSpec((P, tr, C), lambda i: (0, i, 0))],
        out_specs=[ospec] * 4, out_shape=[jax.ShapeDtypeStruct((rows, C), F32)] * 4,
        compiler_params=_params(("parallel",)),
    )(w, m, v, g)


def _coords():
    return lax.axis_index("x"), lax.axis_index("y"), lax.axis_index("c")


def _me():
    x, y, c = _coords()
    return 4 * x + 2 * y + c


_ANY = pl.BlockSpec(memory_space=pl.ANY)


def all_gather(items, name):
    n = len(items)
    blks = [a.shape if idx is None else a.shape[1:] for a, idx in items]

    def body(*refs):
        ins, outs = refs[:n], refs[n:2 * n]
        send, recv, lsem = refs[2 * n:]
        x, y, c = _coords()
        me, sib = (x, y, c), (x, y, 1 - c)
        chips = [(1 - x, y), (x, 1 - y), (1 - x, 1 - y)]

        def src(i):
            return ins[i] if items[i][1] is None else ins[i].at[items[i][1]]

        def slot(i, p):
            return outs[i].at[4 * p[0] + 2 * p[1] + p[2]]

        def cp(i, k, block, to, s=None):
            return pltpu.make_async_remote_copy(
                src_ref=slot(i, block) if s is None else s, dst_ref=slot(i, block),
                send_sem=send.at[7 * i + k], recv_sem=recv.at[7 * i + k], device_id=to, device_id_type=MESH)

        mine = [pltpu.make_async_copy(src(i), slot(i, me), lsem.at[i]) for i in range(n)]
        for m_ in mine:
            m_.start()
        first = []
        for i in range(n):
            first.append(cp(i, 0, me, sib, src(i)))
            first += [cp(i, 1 + j, me, (*chip, c), src(i)) for j, chip in enumerate(chips)]
        for f in first:
            f.start()
        passed = []
        for j, chip in enumerate(chips):
            for i in range(n):
                cp(i, 1 + j, (*chip, c), me).wait_recv()
                p_ = cp(i, 4 + j, (*chip, c), sib)
                p_.start()
                passed.append(p_)
        for i in range(n):
            cp(i, 0, sib, me).wait_recv()
            for j, chip in enumerate(chips):
                cp(i, 4 + j, (*chip, 1 - c), me).wait_recv()
        for f in first + passed:
            f.wait_send()
        for m_ in mine:
            m_.wait()

    res = _pcall(
        body, name=name, in_specs=[_ANY] * n, out_specs=[_ANY] * n,
        out_shape=[jax.ShapeDtypeStruct((NDEV,) + tuple(b), a.dtype) for b, (a, _) in zip(blks, items)],
        scratch_shapes=[pltpu.SemaphoreType.DMA((7 * n,)), pltpu.SemaphoreType.DMA((7 * n,)),
                        pltpu.SemaphoreType.DMA((n,))],
    )(*[a for a, _ in items])
    return list(res)


_HBM = pl.BlockSpec(memory_space=pltpu.HBM)
_SEM = pl.BlockSpec(memory_space=pltpu.SEMAPHORE)
_EFFECT = pltpu.SideEffectType.DATAFLOW_SIDE_EFFECTING


def _xchg_copy(src_ref, land_ref, send, recv, r, scatter, at_peer):
    x, y, c = _coords()
    px = jnp.bitwise_xor(x, (r >> 2) & 1)
    py = jnp.bitwise_xor(y, (r >> 1) & 1)
    pc = jnp.bitwise_xor(c, r & 1)
    p_i = 4 * px + 2 * py + pc
    me_i = 4 * x + 2 * y + c
    return pltpu.make_async_remote_copy(
        src_ref=src_ref.at[p_i] if scatter else src_ref, dst_ref=land_ref.at[p_i if at_peer else me_i],
        send_sem=send.at[r - 1], recv_sem=recv.at[r - 1], device_id=(px, py, pc), device_id_type=MESH)


def xchg_start(srcs, scatter, name):
    n = len(srcs)
    me = _me()
    lands = []
    for s in srcs:
        blk = s.shape[1:] if scatter else s.shape
        own = lax.dynamic_index_in_dim(s, me, 0, keepdims=True) if scatter else s[None]
        lands.append(lax.dynamic_update_index_in_dim(lax.empty((NDEV,) + tuple(blk), s.dtype), own, me, 0))

    def body(*refs):
        s_in, l_in = refs[:n], refs[n:2 * n]
        outs = refs[2 * n:]
        sends, recvs = outs[:n], outs[n:2 * n]
        token = outs[4 * n]
        for i in range(n):
            for r in range(1, NDEV):
                _xchg_copy(s_in[i], l_in[i], sends[i], recvs[i], r, scatter, False).start()
        token[...] = jnp.zeros_like(token)

    hbm = lambda a: pltpu.HBM(a.shape, a.dtype)
    res = _pcall(
        body, name=name,
        out_shape=tuple([pltpu.SemaphoreType.DMA((NDEV - 1,))] * (2 * n) + [hbm(a) for a in srcs] + [hbm(a) for a in lands]
                        + [jax.ShapeDtypeStruct((8, LANES), F32)]),
        in_specs=[_HBM] * (2 * n),
        out_specs=tuple([_SEM] * (2 * n) + [_HBM] * (2 * n) + [pl.BlockSpec(memory_space=pltpu.VMEM)]),
        input_output_aliases={i: 2 * n + i for i in range(2 * n)},
        compiler_params=pltpu.CompilerParams(has_side_effects=_EFFECT),
    )(*[pltpu.with_memory_space_constraint(a, pltpu.HBM) for a in list(srcs) + lands])
    handles = [(res[i], res[n + i], res[2 * n + i], res[3 * n + i]) for i in range(n)]
    return handles, res[4 * n]


def xchg_wait(handles, after, scatter, name):
    n = len(handles)

    def body(*refs):
        s_in, l_in = refs[:n], refs[n:2 * n]
        sends, recvs = refs[2 * n:3 * n], refs[3 * n:4 * n]
        for i in range(n):
            for r in range(1, NDEV):
                cp = _xchg_copy(s_in[i], l_in[i], sends[i], recvs[i], r, scatter, True)
                cp.wait_send()
                cp.wait_recv()

    srcs = [h[2] for h in handles]
    lands = [h[3] for h in handles]
    hbm = lambda a: pltpu.HBM(a.shape, a.dtype)
    res = _pcall(
        body, name=name,
        out_shape=tuple([hbm(a) for a in srcs] + [hbm(a) for a in lands]),
        in_specs=[_HBM] * (2 * n) + [_SEM] * (2 * n) + [_ANY],
        out_specs=tuple([_HBM] * (2 * n)),
        input_output_aliases={i: i for i in range(2 * n)},
        compiler_params=pltpu.CompilerParams(has_side_effects=_EFFECT),
    )(*srcs, *lands, *[h[0] for h in handles], *[h[1] for h in handles], after)
    return list(res[n:])


_PACK_ALIGN = 8 * LANES


def _pack(arrs):
    parts = []
    for a in arrs:
        f = a.reshape(-1).astype(F32)
        pad = (-f.shape[0]) % _PACK_ALIGN
        parts.append(jnp.pad(f, (0, pad)) if pad else f)
    return jnp.concatenate(parts).reshape(-1, LANES)


def _packed_rows(shape):
    n = 1
    for d in shape:
        n *= d
    return (n + _PACK_ALIGN - 1) // _PACK_ALIGN * 8


def _unpack(p, shapes, lead=()):
    nl = len(lead)
    out, r0 = [], 0
    for shp in shapes:
        n = 1
        for d in shp:
            n *= d
        nr = _packed_rows(shp)
        flat = lax.slice_in_dim(p, r0, r0 + nr, axis=nl).reshape(lead + (nr * LANES,))
        out.append(lax.slice_in_dim(flat, 0, n, axis=nl).reshape(lead + tuple(shp)))
        r0 += nr
    return out


def _shard_cols(a, me, width, axis):
    return lax.dynamic_slice_in_dim(a, me * width, width, axis=axis)


def kernel(x, c, norm1_g, norm2_g, ada_w, ada_b, mlp_w1, mlp_w2, ab_w_in, sgu_norm_g, sgu_w, sgu_b, conv_w, conv_b, conv_ln_g, conv_ln_b, ab_w_out, mla_w_in, mla_q_norm_g, mla_kv_norm_g, mla_w_uq, mla_w_ukv, mla_q_head_g, mla_k_head_g, mla_w_out, loss_target, m_norm1_g, m_norm2_g, m_ada_w, m_ada_b, m_mlp_w1, m_mlp_w2, m_ab_w_in, m_sgu_norm_g, m_sgu_w, m_sgu_b, m_conv_w, m_conv_b, m_conv_ln_g, m_conv_ln_b, m_ab_w_out, m_mla_w_in, m_mla_q_norm_g, m_mla_kv_norm_g, m_mla_w_uq, m_mla_w_ukv, m_mla_q_head_g, m_mla_k_head_g, m_mla_w_out, v_norm1_g, v_norm2_g, v_ada_w, v_ada_b, v_mlp_w1, v_mlp_w2, v_ab_w_in, v_sgu_norm_g, v_sgu_w, v_sgu_b, v_conv_w, v_conv_b, v_conv_ln_g, v_conv_ln_b, v_ab_w_out, v_mla_w_in, v_mla_q_norm_g, v_mla_kv_norm_g, v_mla_w_uq, v_mla_w_ukv, v_mla_q_head_g, v_mla_k_head_g, v_mla_w_out):
    W = dict(norm1_g=norm1_g, norm2_g=norm2_g, ada_w=ada_w, ada_b=ada_b, mlp_w1=mlp_w1, mlp_w2=mlp_w2, ab_w_in=ab_w_in,
             sgu_norm_g=sgu_norm_g, sgu_w=sgu_w, sgu_b=sgu_b, conv_w=conv_w, conv_b=conv_b, conv_ln_g=conv_ln_g,
             conv_ln_b=conv_ln_b, ab_w_out=ab_w_out, mla_w_in=mla_w_in, mla_q_norm_g=mla_q_norm_g,
             mla_kv_norm_g=mla_kv_norm_g, mla_w_uq=mla_w_uq, mla_w_ukv=mla_w_ukv, mla_q_head_g=mla_q_head_g,
             mla_k_head_g=mla_k_head_g, mla_w_out=mla_w_out)
    M = dict(norm1_g=m_norm1_g, norm2_g=m_norm2_g, ada_w=m_ada_w, ada_b=m_ada_b, mlp_w1=m_mlp_w1, mlp_w2=m_mlp_w2,
             ab_w_in=m_ab_w_in, sgu_norm_g=m_sgu_norm_g, sgu_w=m_sgu_w, sgu_b=m_sgu_b, conv_w=m_conv_w, conv_b=m_conv_b,
             conv_ln_g=m_conv_ln_g, conv_ln_b=m_conv_ln_b, ab_w_out=m_ab_w_out, mla_w_in=m_mla_w_in,
             mla_q_norm_g=m_mla_q_norm_g, mla_kv_norm_g=m_mla_kv_norm_g, mla_w_uq=m_mla_w_uq, mla_w_ukv=m_mla_w_ukv,
             mla_q_head_g=m_mla_q_head_g, mla_k_head_g=m_mla_k_head_g, mla_w_out=m_mla_w_out)
    V = dict(norm1_g=v_norm1_g, norm2_g=v_norm2_g, ada_w=v_ada_w, ada_b=v_ada_b, mlp_w1=v_mlp_w1, mlp_w2=v_mlp_w2,
             ab_w_in=v_ab_w_in, sgu_norm_g=v_sgu_norm_g, sgu_w=v_sgu_w, sgu_b=v_sgu_b, conv_w=v_conv_w, conv_b=v_conv_b,
             conv_ln_g=v_conv_ln_g, conv_ln_b=v_conv_ln_b, ab_w_out=v_ab_w_out, mla_w_in=v_mla_w_in,
             mla_q_norm_g=v_mla_q_norm_g, mla_kv_norm_g=v_mla_kv_norm_g, mla_w_uq=v_mla_w_uq, mla_w_ukv=v_mla_w_ukv,
             mla_q_head_g=v_mla_q_head_g, mla_k_head_g=v_mla_k_head_g, mla_w_out=v_mla_w_out)
    ORDER = list(W)

    S, D = x.shape[1], x.shape[2]
    L, NE, NO = norm1_g.shape[0], ab_w_in.shape[0], mla_w_in.shape[0]
    DA = D // 2
    DB = D - DA
    G = DA // GROUP
    R = NDEV * mla_q_norm_g.shape[1]
    H = NDEV * mla_w_uq.shape[2] // QK
    AW = ada_w.shape[2]
    CB = conv_w.shape[2]
    me = _me()
    xs, tgt = x[0], loss_target[0]

    BIG_EVEN = ("mlp_w1", "mlp_w2", "ab_w_in", "ab_w_out")
    BIG_ODD = ("mlp_w1", "mlp_w2", "mla_w_in", "mla_w_uq", "mla_w_ukv", "mla_w_out")
    BIG = ("mlp_w1", "mlp_w2", "ab_w_in", "ab_w_out", "mla_w_in", "mla_w_uq", "mla_w_ukv", "mla_w_out")
    COL_SHARDED = ("mlp_w1", "ab_w_in", "mla_w_uq", "mla_w_ukv")
    MLP_W = ("mlp_w1", "mlp_w2")
    mixer_w = lambda l: ("ab_w_in", "ab_w_out") if l % 2 == 0 else ("mla_w_in", "mla_w_uq", "mla_w_ukv", "mla_w_out")
    widx = lambda k, l: l if k in MLP_W else l // 2

    g_handles = []
    tok_sum = jnp.zeros((1, 1), F32)
    for l in range(L):
        names = mixer_w(l) + MLP_W
        srcs = [cast_bf16(W[k], widx(k, l), "cast_%s_l%d" % (k, l)) for k in names]
        handles, tok = xchg_start(srcs, False, "gather_start_l%d" % l)
        g_handles.append(dict(zip(names, handles)))
        tok_sum = tok_sum + tok[0:1, 0:1]
    c = c + tok_sum

    def wait_weights(l, names, after, what):
        lands = xchg_wait([g_handles[l][k] for k in names], after, False, "gather_wait_%s_l%d" % (what, l))
        return {k: (ld if k in COL_SHARDED else ld.reshape(NDEV * ld.shape[1], ld.shape[2])) for k, ld in zip(names, lands)}

    small_in = [c, mla_q_norm_g, mla_kv_norm_g, conv_w]
    sg = all_gather([(_pack(small_in), None)], "gather_small")[0]
    c_all, qng_all, kvng_all, cw_all = _unpack(sg, [a.shape for a in small_in], (NDEV,))
    c_all = c_all.reshape(NDEV, D)
    qng_full = jnp.transpose(qng_all, (1, 0, 2)).reshape(NO, 1, R)
    kvng_full = jnp.transpose(kvng_all, (1, 0, 2)).reshape(NO, 1, R)
    cw_full = jnp.transpose(cw_all, (1, 2, 0, 3)).reshape(NE, CONV_W, DB)
    cw_pad = jnp.pad(cw_full, ((0, 0), (0, CONV_PAD - CONV_W), (0, 0)))

    def silu_fn(t, c_, o):
        v_ = t[0][...]
        o[0][...] = v_ * _sigmoid(v_)
        return []

    c_act = rowwise(silu_fn, [c_all], outs=[((NDEV, D), F32)], ts=NDEV, name="silu_c")[0]
    bias_cols = _shard_cols(ada_b, me, AW, 1).reshape(1, L * AW)
    mod_cols = mm(c_act, ada_w, "nn", name="ada_fwd", b_blocked=True, rowvecs=[bias_cols],
                  epi=lambda acc, b_: (acc + b_,), tm=NDEV, tn=768)
    mod_all = all_gather([(mod_cols, None)], "gather_mod")[0]
    mod = lax.dynamic_index_in_dim(mod_all, me, axis=1, keepdims=False)
    mod = jnp.transpose(mod.reshape(NDEV, L, AW), (1, 0, 2)).reshape(L, 6, 1, D)

    pos = jnp.arange(S, dtype=F32)
    inv = ROPE_THETA ** (-jnp.arange(0, ROPE, 2, dtype=F32) / ROPE)
    ang = pos[:, None] * inv[None, :]
    cos2 = jnp.concatenate([jnp.cos(ang), jnp.cos(ang)], axis=1)
    sinm = jnp.concatenate([-jnp.sin(ang), jnp.sin(ang)], axis=1)

    residual = lambda acc, xr, gt: (acc, xr + gt * acc)

    saved = []
    xc = xs
    for l in range(L):
        sh1, sc1, g1, sh2, sc2, g2 = [mod[l, k] for k in range(6)]
        tag = "_l%d" % l
        sv = dict(x0=xc)
        h, sv["rstd1"] = prenorm(xc, norm1_g[l][None], sc1, sh1, "prenorm1" + tag)
        sv["h"] = h
        wl = wait_weights(l, mixer_w(l), h, "mix")
        if l % 2 == 0:
            e = l // 2
            ng = sgu_norm_g[e].reshape(1, DA)
            bcol = sgu_b[e][:, :, None]
            proj = mm(h, wl["ab_w_in"], "nn", name="ab_in" + tag, b_blocked=True)
            out_a = sgu_fwd(proj, ng, sgu_w[e], bcol, DA, "sgu_fwd" + tag)
            y1 = conv_fwd(proj, cw_pad[e], conv_b[e][None], DA, DB, "conv_fwd" + tag)
            out_b = ln_silu(y1, conv_ln_g[e][None], conv_ln_b[e][None], "ln_silu" + tag)
            cat = jnp.concatenate([out_a, out_b], axis=1)
            sv.update(proj=proj, y1=y1, cat=cat)
            mixb, x1 = mm(cat, wl["ab_w_out"], "nn", name="ab_out" + tag, out_dtypes=(BF16, F32), epi=residual,
                          extras=[xc], rowvecs=[g1])
        else:
            o_ = l // 2
            proj = mm(h, wl["mla_w_in"], "nn", name="mla_in" + tag)
            kgr = mla_k_head_g[o_][None, NOPE:QK]
            qn, kvn, kr = mla_norms(proj, cos2, sinm, qng_full[o_], kvng_full[o_], kgr, R, "mla_norms" + tag)
            q = mm(qn, wl["mla_w_uq"], "nn", name="mla_uq" + tag, b_blocked=True)
            kv = mm(kvn, wl["mla_w_ukv"], "nn", name="mla_ukv" + tag, b_blocked=True)
            Qh, Kh, Vh = mla_heads(q, kv, kr, cos2, sinm, mla_q_head_g[o_][None], mla_k_head_g[o_][None], H,
                                   "mla_heads" + tag)
            att, lse = attn_fwd(Qh, Kh, Vh, "attn_fwd" + tag)
            sv.update(proj=proj, qn=qn, kvn=kvn, q=q, kv=kv, Qh=Qh, Kh=Kh, Vh=Vh, att=att, lse=lse)
            mixb, x1 = mm(att, wl["mla_w_out"], "nn", name="mla_out" + tag, out_dtypes=(BF16, F32), epi=residual,
                          extras=[xc], rowvecs=[g1])
        sv.update(mixb=mixb, x1=x1)
        h2, sv["rstd2"] = prenorm(x1, norm2_g[l][None], sc2, sh2, "prenorm2" + tag)
        wl.update(wait_weights(l, MLP_W, h2, "mlp"))
        sv["w"] = wl
        z, act = mm(h2, wl["mlp_w1"], "nn", name="mlp_up" + tag, b_blocked=True, out_dtypes=(BF16, BF16),
                    epi=lambda acc: (acc, jnp.square(jnp.maximum(acc, 0.0))))
        yb, xc = mm(act, wl["mlp_w2"], "nn", name="mlp_down" + tag, out_dtypes=(BF16, F32), epi=residual,
                    extras=[x1], rowvecs=[g2])
        sv.update(h2=h2, z=z, act=act, yb=yb)
        saved.append(sv)

    dx, loss_cols = loss_grad(xc, tgt, "loss")
    loss = lax.psum(0.5 / D * jnp.sum(loss_cols), ("x", "y", "c"))

    big_out = {k: [None] * W[k].shape[0] for k in BIG}
    sm = {k: [None] * W[k].shape[0] for k in ("norm1_g", "norm2_g", "sgu_norm_g", "sgu_w", "sgu_b", "conv_b", "conv_ln_g",
                                               "conv_ln_b", "mla_q_head_g", "mla_k_head_g", "mla_q_norm_g",
                                               "mla_kv_norm_g", "conv_w")}
    dmod = [None] * L
    flat2 = {k: W[k].reshape(-1, W[k].shape[2]) for k in BIG}
    flat2m = {k: M[k].reshape(-1, W[k].shape[2]) for k in BIG}
    flat2v = {k: V[k].reshape(-1, W[k].shape[2]) for k in BIG}

    def scatter_start(gr, what, tag):
        names = list(gr)
        blocks = [gr[k] if k in COL_SHARDED else gr[k].reshape(NDEV, gr[k].shape[0] // NDEV, gr[k].shape[1])
                  for k in names]
        handles, tok = xchg_start(blocks, True, "scatter_start_%s%s" % (what, tag))
        return (names, handles, what, tag), tok[0:1, 0:1]

    def scatter_finish(pending, after, l):
        names, handles, what, tag = pending
        landed = xchg_wait(handles, after, True, "scatter_wait_%s%s" % (what, tag))
        for k, land in zip(names, landed):
            li = widx(k, l)
            big_out[k][li] = adamw(flat2[k], flat2m[k], flat2v[k], land, row0=li * W[k].shape[1],
                                   name="adamw_%s%s" % (k, tag))

    pend_mix, tok_mix = None, None
    for l in reversed(range(L)):
        sh1, sc1, g1, sh2, sc2, g2 = [mod[l, k] for k in range(6)]
        if tok_mix is not None:
            g2 = g2 + tok_mix
        sv = saved[l]
        wl = sv["w"]
        tag = "_l%d" % l
        gr = {}
        dy, dgate2 = gate_bwd(dx, sv["yb"], g2, "gate2_bwd" + tag)
        dz = mm(dy, wl["mlp_w2"], "nt", name="mlp_down_dx" + tag, out_dtypes=(BF16,), extras=[sv["z"]],
                epi=lambda acc, z_: (acc * (2.0 * jnp.maximum(z_.astype(F32), 0.0)),))
        gr["mlp_w2"] = mm(sv["act"], dy, "tn", name="mlp_down_dw" + tag, out_dtypes=(BF16,))
        dh2 = mm(dz, wl["mlp_w1"], "nt", name="mlp_up_dx" + tag, b_blocked=True)
        gr["mlp_w1"] = mm(sv["h2"], dz, "tn", name="mlp_up_dw" + tag, out_dtypes=(BF16,), out_blocked=NDEV)
        dx1, dsc2, dsh2, sm["norm2_g"][l] = norm_bwd(dh2, sv["x1"], sv["rstd2"], dx, norm2_g[l][None], sc2,
                                                     "norm2_bwd" + tag)
        pend_mlp, tok_mlp = scatter_start(gr, "mlp", tag)
        gr = {}
        if pend_mix is not None:
            scatter_finish(pend_mix, dx1, l + 1)
        dmix, dgate1 = gate_bwd(dx1, sv["mixb"], g1 + tok_mlp, "gate1_bwd" + tag)
        if l % 2 == 0:
            e = l // 2
            ng = sgu_norm_g[e].reshape(1, DA)
            bcol = sgu_b[e][:, :, None]
            dcat = mm(dmix, wl["ab_w_out"], "nt", name="ab_out_dx" + tag)
            gr["ab_w_out"] = mm(sv["cat"], dmix, "tn", name="ab_out_dw" + tag, out_dtypes=(BF16,))
            dy1, sm["conv_ln_g"][e], sm["conv_ln_b"][e], sm["conv_b"][e] = ln_silu_bwd(
                dcat, sv["y1"], conv_ln_g[e][None], conv_ln_b[e][None], "ln_silu_bwd" + tag)
            da, dg_, dwc = conv_bwd(sv["proj"], dy1, cw_pad[e], DA, DB, "conv_bwd" + tag)
            sm["conv_w"][e] = dwc[:CONV_W]
            duv, dsw, dsb, dsng = sgu_bwd(sv["proj"], dcat, ng, sgu_w[e], bcol, DA, "sgu_bwd" + tag)
            sm["sgu_w"][e], sm["sgu_b"][e], sm["sgu_norm_g"][e] = dsw, dsb, dsng
            dproj = jnp.concatenate([duv, da, dg_], axis=1)
            dh = mm(dproj, wl["ab_w_in"], "nt", name="ab_in_dx" + tag, b_blocked=True)
            gr["ab_w_in"] = mm(sv["h"], dproj, "tn", name="ab_in_dw" + tag, out_dtypes=(BF16,), out_blocked=NDEV)
        else:
            o_ = l // 2
            kgr = mla_k_head_g[o_][None, NOPE:QK]
            dO = mm(dmix, wl["mla_w_out"], "nt", name="mla_out_dx" + tag, out_dtypes=(BF16,))
            gr["mla_w_out"] = mm(sv["att"], dmix, "tn", name="mla_out_dw" + tag, out_dtypes=(BF16,))
            dQ, dK, dV = attn_bwd(sv["Qh"], sv["Kh"], sv["Vh"], dO, sv["att"], sv["lse"], "attn_bwd" + tag)
            dq_pre, dkv_pre, dkr, dqgn, dqgr, dkgn = mla_heads_bwd(
                dQ, dK, dV, sv["q"], sv["kv"], cos2, sinm, mla_q_head_g[o_][None], mla_k_head_g[o_][None], H,
                "mla_heads_bwd" + tag)
            dqn = mm(dq_pre, wl["mla_w_uq"], "nt", name="mla_uq_dx" + tag, b_blocked=True)
            gr["mla_w_uq"] = mm(sv["qn"], dq_pre, "tn", name="mla_uq_dw" + tag, out_dtypes=(BF16,), out_blocked=NDEV)
            dkvn = mm(dkv_pre, wl["mla_w_ukv"], "nt", name="mla_ukv_dx" + tag, b_blocked=True)
            gr["mla_w_ukv"] = mm(sv["kvn"], dkv_pre, "tn", name="mla_ukv_dw" + tag, out_dtypes=(BF16,),
                                 out_blocked=NDEV)
            dproj, sm["mla_q_norm_g"][o_], sm["mla_kv_norm_g"][o_], dkgr = mla_norms_bwd(
                dqn, dkvn, dkr, sv["proj"], qng_full[o_], kvng_full[o_], kgr, R, "mla_norms_bwd" + tag)
            sm["mla_q_head_g"][o_] = jnp.concatenate([dqgn, dqgr], axis=1)
            sm["mla_k_head_g"][o_] = jnp.concatenate([dkgn, dkgr], axis=1)
            dh = mm(dproj, wl["mla_w_in"], "nt", name="mla_in_dx" + tag)
            gr["mla_w_in"] = mm(sv["h"], dproj, "tn", name="mla_in_dw" + tag, out_dtypes=(BF16,))
        dx, dsc1, dsh1, sm["norm1_g"][l] = norm_bwd(dh, sv["x0"], sv["rstd1"], dx1, norm1_g[l][None], sc1,
                                                    "norm1_bwd" + tag)
        dmod[l] = jnp.concatenate([dsh1, dsc1, dgate1, dsh2, dsc2, dgate2], axis=1)

        pend_mix, tok_mix = scatter_start(gr, "mix", tag)
        scatter_finish(pend_mlp, dx, l)
    scatter_finish(pend_mix, dx, 0)

    rep = ("norm1_g", "norm2_g", "sgu_norm_g", "sgu_w", "sgu_b", "conv_b", "conv_ln_g", "conv_ln_b", "mla_q_head_g",
           "mla_k_head_g")
    part_full = {"mla_q_norm_g": (NO, R), "mla_kv_norm_g": (NO, R), "conv_w": (NE, CONV_W, DB)}
    parts = [jnp.stack(sm[k]).reshape(W[k].shape) for k in rep]
    parts += [jnp.stack(sm[k]).reshape(part_full[k]) for k in part_full]
    parts.append(jnp.stack(dmod).reshape(L, 6 * D))
    shapes = [p.shape for p in parts]
    gp = all_gather([(_pack(parts), None)], "gather_smallgrads")[0]
    rows_p = gp.shape[1]
    r0 = sum(_packed_rows(s_) for s_ in shapes[:-1])
    per = 6 * D // LANES
    dm = lax.slice_in_dim(gp, r0, r0 + L * per, axis=1).reshape(NDEV, L, per, LANES)
    dmod_cols = lax.dynamic_slice_in_dim(dm, me * (AW // LANES), AW // LANES, axis=2).reshape(NDEV, L * AW)

    def sum_fn(t, c_, o):
        acc = t[0][0]
        for s_ in range(1, NDEV):
            acc = acc + t[0][s_]
        o[0][...] = acc
        return []

    gsummed = rowwise(sum_fn, [gp], outs=[((rows_p, LANES), F32)], ts=_tile(rows_p, 256, 8), name="sum_smallgrads")[0]
    gsum = dict(zip(list(rep) + list(part_full) + ["ada_b"], _unpack(gsummed, shapes)))
    gsum["mla_q_norm_g"] = _shard_cols(gsum["mla_q_norm_g"], me, R // NDEV, 1)
    gsum["mla_kv_norm_g"] = _shard_cols(gsum["mla_kv_norm_g"], me, R // NDEV, 1)
    gsum["conv_w"] = _shard_cols(gsum["conv_w"], me, CB, 2)
    small = list(rep) + list(part_full) + ["ada_b"]
    sm_shapes = [W[k].shape for k in small]
    sres = adamw(_pack([W[k] for k in small]), _pack([M[k] for k in small]), _pack([V[k] for k in small]),
                 _pack([gsum[k] for k in small])[None], row0=0, name="adamw_small")
    small_out = {k: vals for k, vals in zip(small, zip(*[_unpack(r_, sm_shapes) for r_ in sres]))}

    g_ada = mm(c_act, dmod_cols, "tn", name="ada_dw", out_blocked=L, tm=1024, tn=768, tk=NDEV, cast=None,
               precision=lax.Precision.HIGHEST)
    ada_out = adamw(ada_w.reshape(L * D, AW), m_ada_w.reshape(L * D, AW), v_ada_w.reshape(L * D, AW),
                    g_ada.reshape(1, L * D, AW), row0=0, name="adamw_ada_w")
    ada_out = [a.reshape(L, D, AW) for a in ada_out]

    def result(k, which):
        if k == "ada_w":
            return ada_out[which]
        if k in BIG:
            return jnp.stack([r_[which] for r_ in big_out[k]]).reshape(W[k].shape)
        return small_out[k][which]

    outs = [loss, dx[None]]
    for which in range(4):
        outs += [result(k, which) for k in ORDER]
    return tuple(outs)
```

```python
import functools

import jax
import jax.numpy as jnp
from jax import lax
from jax.experimental import pallas as pl
from jax.experimental.pallas import tpu as pltpu

F32 = jnp.float32
BF16 = jnp.bfloat16
EPS = 1e-6
NDEV = 8
LANES = 128
CHUNK = 128
GROUP = 128
CONV_W = 31
CONV_PAD = 32
NOPE, ROPE, VDIM = 128, 64, 128
QK = NOPE + ROPE
ROPE_THETA = 10000.0
VMEM_LIMIT = 56 * 1024 * 1024
ADAM_LR, ADAM_B1, ADAM_B2, ADAM_EPS, ADAM_WD, ADAM_STEP = 0.001, 0.9, 0.999, 1e-08, 0.01, 10
MESH = pl.DeviceIdType.MESH
NEG = -1e30
ATTN_STRIP = 64


def _pcall(body, **kw):
    return pl.pallas_call(body, **kw)


def _params(sem=None):
    return pltpu.CompilerParams(dimension_semantics=sem, vmem_limit_bytes=VMEM_LIMIT)


def _tile(dim, target, align=LANES):
    if dim <= target:
        return dim
    t = (target // align) * align
    while t >= align:
        if dim % t == 0:
            return t
        t -= align
    return dim


def _rstd(x):
    return lax.rsqrt(jnp.mean(x * x, axis=-1, keepdims=True) + EPS)


def _sigmoid(x):
    return 1.0 / (1.0 + jnp.exp(-x))


_GC = 0.7978845608028654


def _gelu(x):
    return 0.5 * x * (1.0 + jnp.tanh(_GC * (x + 0.044715 * x * x * x)))


def _gelu_grad(x):
    t = jnp.tanh(_GC * (x + 0.044715 * x * x * x))
    return 0.5 * (1.0 + t) + 0.5 * x * (1.0 - t * t) * _GC * (1.0 + 3 * 0.044715 * x * x)


def _colsum(x):
    return jnp.sum(x, axis=0, keepdims=True)


def _rms_bwd(dy, xhat, rstd, g):
    dxh = dy * g
    dx = rstd * (dxh - xhat * jnp.mean(dxh * xhat, axis=-1, keepdims=True))
    return dx, _colsum(dy * xhat)


def _swap_halves(x):
    h = x.shape[-1] // 2
    return jnp.concatenate([x[:, h:], x[:, :h]], axis=1)


def _rope(x, cos2, sinm):
    return x * cos2 + _swap_halves(x) * sinm


def _unrope(dy, cos2, sinm):
    return dy * cos2 + _swap_halves(dy * sinm)


_DIMS = {"nn": (((1,), (0,)), ((), ())), "nt": (((1,), (1,)), ((), ())), "tn": (((0,), (0,)), ((), ()))}


def mm(a, b, mode, *, name, out_dtypes=(F32,), epi=None, extras=(), rowvecs=(), b_blocked=False, out_blocked=0,
       tm=1024, tn=1024, tk=2048, precision=None, cast=BF16):
    if mode == "tn":
        K, M = a.shape
    else:
        M, K = a.shape
    if b_blocked:
        J, Rb, Cb = b.shape
        N = Rb if mode == "nt" else J * Cb
    else:
        N = b.shape[0] if mode == "nt" else b.shape[1]
    tm = _tile(M, tm)
    if mode == "nn" and b_blocked:
        tn = _tile(Cb, tn)
    elif out_blocked:
        tn = _tile(N // out_blocked, tn)
    else:
        tn = _tile(N, tn)
    kb = 1
    if mode == "nt" and b_blocked:
        if Cb >= tk:
            tk = _tile(Cb, tk)
        else:
            kb = max(d for d in range(1, J + 1) if J % d == 0 and d * Cb <= tk)
            tk = kb * Cb
    else:
        tk = _tile(K, tk)
    nk = K // tk
    grid = (M // tm, N // tn, nk)

    if mode == "tn":
        a_spec = pl.BlockSpec((tk, tm), lambda i, j, k: (k, i))
    else:
        a_spec = pl.BlockSpec((tm, tk), lambda i, j, k: (i, k))
    if mode == "nn":
        if b_blocked:
            nper = Cb // tn
            b_spec = pl.BlockSpec((None, tk, tn), lambda i, j, k: (j // nper, k, j % nper))
        else:
            b_spec = pl.BlockSpec((tk, tn), lambda i, j, k: (k, j))
    elif mode == "nt":
        if b_blocked:
            if kb > 1:
                b_spec = pl.BlockSpec((kb, tn, Cb), lambda i, j, k: (k, j, 0))
            else:
                kper = Cb // tk
                b_spec = pl.BlockSpec((None, tn, tk), lambda i, j, k: (k // kper, j, k % kper))
        else:
            b_spec = pl.BlockSpec((tn, tk), lambda i, j, k: (j, k))
    else:
        b_spec = pl.BlockSpec((tk, tn), lambda i, j, k: (k, j))
    if out_blocked:
        oper = (N // out_blocked) // tn
        o_spec = pl.BlockSpec((None, tm, tn), lambda i, j, k: (j // oper, i, j % oper))
        o_shape = (out_blocked, M, N // out_blocked)
    else:
        o_spec = pl.BlockSpec((tm, tn), lambda i, j, k: (i, j))
        o_shape = (M, N)
    e_spec = pl.BlockSpec((tm, tn), lambda i, j, k: (i, j))
    r_spec = pl.BlockSpec((1, tn), lambda i, j, k: (0, j))
    ne, nr, no = len(extras), len(rowvecs), len(out_dtypes)
    dims = _DIMS[mode]

    def body(a_ref, b_ref, *rest):
        ex = rest[:ne]
        rv = rest[ne:ne + nr]
        outs = rest[ne + nr:ne + nr + no]

        def product():
            if kb > 1:
                r = None
                for q in range(kb):
                    av, bv = a_ref[:, q * Cb:(q + 1) * Cb], b_ref[q]
                    if cast is not None:
                        av, bv = av.astype(cast), bv.astype(cast)
                    d = lax.dot_general(av, bv, dims, preferred_element_type=F32, precision=precision)
                    r = d if r is None else r + d
                return r
            av, bv = a_ref[...], b_ref[...]
            if cast is not None:
                av, bv = av.astype(cast), bv.astype(cast)
            return lax.dot_general(av, bv, dims, preferred_element_type=F32, precision=precision)

        def finish(r):
            vals = (r,) if epi is None else epi(r, *[e[...] for e in ex], *[v[...] for v in rv])
            for o, val in zip(outs, vals):
                o[...] = val.astype(o.dtype)

        if nk == 1:
            finish(product())
            return
        acc = rest[ne + nr + no]
        k = pl.program_id(2)

        @pl.when(k == 0)
        def _():
            acc[...] = product()

        @pl.when((k > 0) & (k < nk - 1))
        def _():
            acc[...] += product()

        @pl.when(k == nk - 1)
        def _():
            finish(acc[...] + product())

    res = _pcall(
        body, name=name, grid=grid,
        in_specs=[a_spec, b_spec] + [e_spec] * ne + [r_spec] * nr,
        out_specs=[o_spec] * no,
        out_shape=[jax.ShapeDtypeStruct(o_shape, dt) for dt in out_dtypes],
        scratch_shapes=[] if nk == 1 else [pltpu.VMEM((tm, tn), F32)],
        compiler_params=_params(("parallel", "parallel", "arbitrary")),
    )(a, b, *extras, *rowvecs)
    return res[0] if no == 1 else res


def rowwise(fn, tiled, consts=(), outs=(), reds=(), *, ts, name):
    specs = []
    arrs = []
    rows = None
    for t in tiled:
        a, w, cb = t if isinstance(t, tuple) else (t, None, 0)
        arrs.append(a)
        rows = a.shape[-2] if rows is None else rows
        if a.ndim == 2:
            specs.append(pl.BlockSpec((ts, a.shape[1] if w is None else w), lambda i, cb=cb: (i, cb)))
        else:
            specs.append(pl.BlockSpec((a.shape[0], ts, a.shape[2]), lambda i: (0, i, 0)))
    for a in consts:
        specs.append(pl.BlockSpec(a.shape, lambda i, n=a.ndim: (0,) * n))
    o_specs, o_shapes = [], []
    for shp, dt in outs:
        if len(shp) == 2:
            o_specs.append(pl.BlockSpec((ts, shp[1]), lambda i: (i, 0)))
        else:
            o_specs.append(pl.BlockSpec((shp[0], ts, shp[2]), lambda i: (0, i, 0)))
        o_shapes.append(jax.ShapeDtypeStruct(shp, dt))
    for shp in reds:
        o_specs.append(pl.BlockSpec(shp, lambda i, n=len(shp): (0,) * n))
        o_shapes.append(jax.ShapeDtypeStruct(shp, F32))
    nt, nc, no = len(arrs), len(consts), len(outs)

    def body(*refs):
        i = pl.program_id(0)
        red_refs = refs[nt + nc + no:]
        vals = fn(refs[:nt], refs[nt:nt + nc], refs[nt + nc:nt + nc + no])
        if red_refs:
            @pl.when(i == 0)
            def _():
                for r in red_refs:
                    r[...] = jnp.zeros_like(r)
            for r, v in zip(red_refs, vals):
                r[...] += v

    return _pcall(body, name=name, grid=(rows // ts,), in_specs=specs, out_specs=o_specs, out_shape=o_shapes,
                  compiler_params=_params(("arbitrary",)))(*arrs, *consts)


def cast_bf16(w, l, name):
    _, R, C = w.shape
    tr = _tile(R, 512, 16)

    def body(w_ref, o_ref):
        o_ref[...] = w_ref[...].astype(BF16)

    return _pcall(body, name=name, grid=(R // tr,), in_specs=[pl.BlockSpec((None, tr, C), lambda i: (l, i, 0))],
                  out_specs=pl.BlockSpec((tr, C), lambda i: (i, 0)), out_shape=jax.ShapeDtypeStruct((R, C), BF16),
                  compiler_params=_params(("parallel",)))(w)


def prenorm(x, g, scale, shift, name):
    S, D = x.shape

    def fn(t, c, o):
        xv = t[0][...]
        r = _rstd(xv)
        o[0][...] = ((xv * r * c[0][...]) * (1.0 + c[1][...]) + c[2][...]).astype(BF16)
        o[1][...] = r
        return []

    return rowwise(fn, [x], [g, scale, shift], [((S, D), BF16), ((S, 1), F32)], ts=_tile(S, 128, 16), name=name)


def gate_bwd(dx, y, gate, name):
    S, D = dx.shape

    def fn(t, c, o):
        d = t[0][...]
        o[0][...] = (d * c[0][...]).astype(BF16)
        return [_colsum(d * t[1][...].astype(F32))]

    return rowwise(fn, [dx, y], [gate], [((S, D), BF16)], [(1, D)], ts=_tile(S, 128, 16), name=name)


def norm_bwd(dh, x, rstd, dres, g, scale, name):
    S, D = x.shape

    def fn(t, c, o):
        d = t[0][...]
        r = t[2][...]
        xh = t[1][...] * r
        gv = c[0][...]
        dr = d * (1.0 + c[1][...])
        dx, dg = _rms_bwd(dr, xh, r, gv)
        o[0][...] = t[3][...] + dx
        return [_colsum(d * (xh * gv)), _colsum(d), dg]

    return rowwise(fn, [dh, x, rstd, dres], [g, scale], [((S, D), F32)], [(1, D)] * 3, ts=_tile(S, 128, 8), name=name)


def loss_grad(y, tgt, name):
    S, D = y.shape

    def fn(t, c, o):
        e = t[0][...] - t[1][...]
        o[0][...] = e * (1.0 / D)
        return [_colsum(e * e)]

    return rowwise(fn, [y, tgt], outs=[((S, D), F32)], reds=[(1, D)], ts=_tile(S, 128, 8), name=name)


def _tril_mask():
    r = lax.broadcasted_iota(jnp.int32, (CHUNK, CHUNK), 0)
    c = lax.broadcasted_iota(jnp.int32, (CHUNK, CHUNK), 1)
    return c <= r


def sgu_fwd(proj, ng, w, bcol, DA, name):
    S = proj.shape[0]
    G = DA // GROUP
    tr = _tile(S, 2 * CHUNK)

    def body(u_ref, v_ref, ng_ref, w_ref, b_ref, o_ref):
        mask = _tril_mask()
        for g in range(G):
            cols = slice(g * GROUP, (g + 1) * GROUP)
            wm = jnp.where(mask, w_ref[g], 0.0).astype(BF16)
            for ci in range(tr // CHUNK):
                rows = slice(ci * CHUNK, (ci + 1) * CHUNK)
                gv = _gelu(v_ref[rows, cols])
                vn = gv * _rstd(gv) * ng_ref[:, cols]
                mixed = jnp.dot(wm, vn.astype(BF16), preferred_element_type=F32) + b_ref[g]
                o_ref[rows, cols] = (_gelu(u_ref[rows, cols]) * mixed).astype(o_ref.dtype)

    return _pcall(
        body, name=name, grid=(S // tr,),
        in_specs=[pl.BlockSpec((tr, DA), lambda i: (i, 0)), pl.BlockSpec((tr, DA), lambda i: (i, 1)),
                  pl.BlockSpec((1, DA), lambda i: (0, 0)), pl.BlockSpec((G, CHUNK, CHUNK), lambda i: (0, 0, 0)),
                  pl.BlockSpec((G, CHUNK, 1), lambda i: (0, 0, 0))],
        out_specs=pl.BlockSpec((tr, DA), lambda i: (i, 0)),
        out_shape=jax.ShapeDtypeStruct((S, DA), BF16),
        compiler_params=_params(("parallel",)),
    )(proj, proj, ng, w, bcol)


def sgu_bwd(proj, dcat, ng, w, bcol, DA, name):
    S = proj.shape[0]
    G = DA // GROUP
    tr = _tile(S, 2 * CHUNK)
    nsteps = S // tr

    def body(u_ref, v_ref, d_ref, ng_ref, w_ref, b_ref, duv_ref, dw_ref, db_ref, dng_ref, dbacc):
        i = pl.program_id(0)

        @pl.when(i == 0)
        def _():
            dw_ref[...] = jnp.zeros_like(dw_ref)
            dng_ref[...] = jnp.zeros_like(dng_ref)
            dbacc[...] = jnp.zeros_like(dbacc)

        mask = _tril_mask()
        for g in range(G):
            cols = slice(g * GROUP, (g + 1) * GROUP)
            wm = jnp.where(mask, w_ref[g], 0.0).astype(BF16)
            ngg = ng_ref[:, cols]
            for ci in range(tr // CHUNK):
                rows = slice(ci * CHUNK, (ci + 1) * CHUNK)
                u, v, d = u_ref[rows, cols], v_ref[rows, cols], d_ref[rows, cols]
                gv = _gelu(v)
                rs = _rstd(gv)
                vhat = gv * rs
                vn = (vhat * ngg).astype(BF16)
                mixed = jnp.dot(wm, vn, preferred_element_type=F32) + b_ref[g]
                dmixed = d * _gelu(u)
                dmb = dmixed.astype(BF16)
                duv_ref[rows, cols] = (d * mixed * _gelu_grad(u)).astype(duv_ref.dtype)
                dwg = lax.dot_general(dmb, vn, _DIMS["nt"], preferred_element_type=F32)
                dw_ref[g] += jnp.where(mask, dwg, 0.0)
                dbacc[g] += dmixed
                dvn = lax.dot_general(wm, dmb, _DIMS["tn"], preferred_element_type=F32)
                dgv, dngg = _rms_bwd(dvn, vhat, rs, ngg)
                dng_ref[:, cols] += dngg
                duv_ref[rows, DA + g * GROUP:DA + (g + 1) * GROUP] = (dgv * _gelu_grad(v)).astype(duv_ref.dtype)

        @pl.when(i == nsteps - 1)
        def _():
            for g in range(G):
                db_ref[g] = jnp.sum(dbacc[g], axis=-1, keepdims=True)

    return _pcall(
        body, name=name, grid=(nsteps,),
        in_specs=[pl.BlockSpec((tr, DA), lambda i: (i, 0)), pl.BlockSpec((tr, DA), lambda i: (i, 1)),
                  pl.BlockSpec((tr, DA), lambda i: (i, 0)),
                  pl.BlockSpec((1, DA), lambda i: (0, 0)), pl.BlockSpec((G, CHUNK, CHUNK), lambda i: (0, 0, 0)),
                  pl.BlockSpec((G, CHUNK, 1), lambda i: (0, 0, 0))],
        out_specs=[pl.BlockSpec((tr, 2 * DA), lambda i: (i, 0)), pl.BlockSpec((G, CHUNK, CHUNK), lambda i: (0, 0, 0)),
                   pl.BlockSpec((G, CHUNK, 1), lambda i: (0, 0, 0)), pl.BlockSpec((1, DA), lambda i: (0, 0))],
        out_shape=[jax.ShapeDtypeStruct((S, 2 * DA), BF16), jax.ShapeDtypeStruct((G, CHUNK, CHUNK), F32),
                   jax.ShapeDtypeStruct((G, CHUNK, 1), F32), jax.ShapeDtypeStruct((1, DA), F32)],
        scratch_shapes=[pltpu.VMEM((G, CHUNK, CHUNK), F32)],
        compiler_params=_params(("arbitrary",)),
    )(proj, proj, dcat, ng, w, bcol)


def _conv_tile(S):
    return _tile(S, 256, 8)


def conv_fwd(proj, wk, bias, DA, DB, name):
    S = proj.shape[0]
    nb = DB // LANES
    a0, g0 = 2 * DA // LANES, (2 * DA + DB) // LANES
    T = _conv_tile(S)
    off = CONV_PAD - (CONV_W - 1)

    def body(a_ref, g_ref, w_ref, b_ref, o_ref, ypad):
        ypad[0:CONV_PAD, :] = jnp.zeros((CONV_PAD, LANES), F32)

        def fill(t, cr):
            r = pl.multiple_of(t * T, T)
            ypad[pl.ds(CONV_PAD + r, T), :] = a_ref[pl.ds(r, T), :] * _sigmoid(g_ref[pl.ds(r, T), :])
            return cr

        lax.fori_loop(0, S // T, fill, 0)

        def step(t, cr):
            r = pl.multiple_of(t * T, T)
            acc = jnp.zeros((T, LANES), F32) + b_ref[...]
            for k in range(CONV_W):
                acc = acc + w_ref[k:k + 1, :] * ypad[pl.ds(r + (k + off), T), :]
            o_ref[pl.ds(r, T), :] = acc
            return cr

        lax.fori_loop(0, S // T, step, 0)

    return _pcall(
        body, name=name, grid=(nb,),
        in_specs=[pl.BlockSpec((S, LANES), lambda j: (0, a0 + j)), pl.BlockSpec((S, LANES), lambda j: (0, g0 + j)),
                  pl.BlockSpec((CONV_PAD, LANES), lambda j: (0, j)), pl.BlockSpec((1, LANES), lambda j: (0, j))],
        out_specs=pl.BlockSpec((S, LANES), lambda j: (0, j)),
        out_shape=jax.ShapeDtypeStruct((S, DB), F32),
        scratch_shapes=[pltpu.VMEM((S + CONV_PAD, LANES), F32)],
        compiler_params=_params(("parallel",)),
    )(proj, proj, wk, bias)


def conv_bwd(proj, dy1, wk, DA, DB, name):
    S = proj.shape[0]
    nb = DB // LANES
    a0, g0 = 2 * DA // LANES, (2 * DA + DB) // LANES
    T = _conv_tile(S)
    off = CONV_PAD - (CONV_W - 1)

    def body(a_ref, g_ref, d_ref, w_ref, da_ref, dg_ref, dw_ref, ypad, dpad, wacc):
        ypad[0:CONV_PAD, :] = jnp.zeros((CONV_PAD, LANES), F32)
        dpad[S:S + CONV_PAD, :] = jnp.zeros((CONV_PAD, LANES), F32)
        wacc[...] = jnp.zeros_like(wacc)

        def fill(t, cr):
            r = pl.multiple_of(t * T, T)
            ypad[pl.ds(CONV_PAD + r, T), :] = a_ref[pl.ds(r, T), :] * _sigmoid(g_ref[pl.ds(r, T), :])
            dpad[pl.ds(r, T), :] = d_ref[pl.ds(r, T), :]
            return cr

        lax.fori_loop(0, S // T, fill, 0)

        def step(t, cr):
            r = pl.multiple_of(t * T, T)
            dt = dpad[pl.ds(r, T), :]
            dy0 = jnp.zeros((T, LANES), F32)
            for k in range(CONV_W):
                prod = dt * ypad[pl.ds(r + (k + off), T), :]
                wacc[k] += jnp.sum(prod.reshape(T // 8, 8, LANES), axis=0)
                dy0 = dy0 + w_ref[k:k + 1, :] * dpad[pl.ds(r + (CONV_W - 1 - k), T), :]
            av, gv = a_ref[pl.ds(r, T), :], g_ref[pl.ds(r, T), :]
            sg = _sigmoid(gv)
            da_ref[pl.ds(r, T), :] = (dy0 * sg).astype(da_ref.dtype)
            dg_ref[pl.ds(r, T), :] = (dy0 * av * sg * (1.0 - sg)).astype(dg_ref.dtype)
            return cr

        lax.fori_loop(0, S // T, step, 0)
        for k in range(CONV_W):
            dw_ref[k:k + 1, :] = jnp.sum(wacc[k], axis=0, keepdims=True)
        dw_ref[CONV_W:CONV_PAD, :] = jnp.zeros((CONV_PAD - CONV_W, LANES), F32)

    return _pcall(
        body, name=name, grid=(nb,),
        in_specs=[pl.BlockSpec((S, LANES), lambda j: (0, a0 + j)), pl.BlockSpec((S, LANES), lambda j: (0, g0 + j)),
                  pl.BlockSpec((S, LANES), lambda j: (0, j)), pl.BlockSpec((CONV_PAD, LANES), lambda j: (0, j))],
        out_specs=[pl.BlockSpec((S, LANES), lambda j: (0, j)), pl.BlockSpec((S, LANES), lambda j: (0, j)),
                   pl.BlockSpec((CONV_PAD, LANES), lambda j: (0, j))],
        out_shape=[jax.ShapeDtypeStruct((S, DB), BF16), jax.ShapeDtypeStruct((S, DB), BF16),
                   jax.ShapeDtypeStruct((CONV_PAD, DB), F32)],
        scratch_shapes=[pltpu.VMEM((S + CONV_PAD, LANES), F32), pltpu.VMEM((S + CONV_PAD, LANES), F32),
                        pltpu.VMEM((CONV_PAD, 8, LANES), F32)],
        compiler_params=_params(("parallel",)),
    )(proj, proj, dy1, wk)


def _ln_stats(y):
    mu = jnp.mean(y, axis=-1, keepdims=True)
    yc = y - mu
    rs = lax.rsqrt(jnp.mean(yc * yc, axis=-1, keepdims=True) + EPS)
    return yc * rs, rs


def ln_silu(y1, lg, lb, name):
    S, DB = y1.shape

    def fn(t, c, o):
        yh, _ = _ln_stats(t[0][...])
        ln = yh * c[0][...] + c[1][...]
        o[0][...] = (ln * _sigmoid(ln)).astype(BF16)
        return []

    return rowwise(fn, [y1], [lg, lb], [((S, DB), BF16)], ts=_tile(S, 128, 16), name=name)[0]


def ln_silu_bwd(dcat, y1, lg, lb, name):
    S, DB = y1.shape
    cb = (dcat.shape[1] - DB) // DB

    def fn(t, c, o):
        yh, rs = _ln_stats(t[1][...])
        gv = c[0][...]
        ln = yh * gv + c[1][...]
        sg = _sigmoid(ln)
        dln = t[0][...] * (sg * (1.0 + ln * (1.0 - sg)))
        dyh = dln * gv
        dy = rs * (dyh - jnp.mean(dyh, axis=-1, keepdims=True) - yh * jnp.mean(dyh * yh, axis=-1, keepdims=True))
        o[0][...] = dy
        return [_colsum(dln * yh), _colsum(dln), _colsum(dy)]

    return rowwise(fn, [(dcat, DB, cb), y1], [lg, lb], [((S, DB), F32)], [(1, DB)] * 3, ts=_tile(S, 128, 8), name=name)


def mla_norms(proj, cos2, sinm, qg, kvg, kgr, R, name):
    S = proj.shape[0]

    def fn(t, c, o):
        cq = t[0][:, 0:R]
        ckv = t[0][:, R:2 * R]
        kr = t[0][:, 2 * R:2 * R + ROPE]
        o[0][...] = (cq * _rstd(cq) * c[0][...]).astype(BF16)
        o[1][...] = (ckv * _rstd(ckv) * c[1][...]).astype(BF16)
        o[2][...] = _rope(kr * _rstd(kr) * c[2][...], t[1][...], t[2][...])
        return []

    return rowwise(fn, [proj, cos2, sinm], [qg, kvg, kgr], [((S, R), BF16), ((S, R), BF16), ((S, ROPE), F32)],
                   ts=_tile(S, 128, 16), name=name)


def mla_heads(q, kv, kr, cos2, sinm, qg, kg, H, name):
    S = q.shape[0]

    def fn(t, c, o):
        cs, sn = t[3][...], t[4][...]
        krv = t[2][...]
        qgn, qgr, kgn = c[0][:, 0:NOPE], c[0][:, NOPE:QK], c[1][:, 0:NOPE]
        for h in range(H):
            qn = t[0][:, QK * h:QK * h + NOPE]
            qr = t[0][:, QK * h + NOPE:QK * (h + 1)]
            o[0][h, :, 0:NOPE] = (qn * _rstd(qn) * qgn).astype(BF16)
            o[0][h, :, NOPE:QK] = _rope(qr * _rstd(qr) * qgr, cs, sn).astype(BF16)
            kn = t[1][:, (NOPE + VDIM) * h:(NOPE + VDIM) * h + NOPE]
            o[1][h, :, 0:NOPE] = (kn * _rstd(kn) * kgn).astype(BF16)
            o[1][h, :, NOPE:QK] = krv.astype(BF16)
            o[2][h] = t[1][:, (NOPE + VDIM) * h + NOPE:(NOPE + VDIM) * (h + 1)].astype(BF16)
        return []

    return rowwise(fn, [q, kv, kr, cos2, sinm], [qg, kg],
                   [((H, S, QK), BF16), ((H, S, QK), BF16), ((H, S, VDIM), BF16)], ts=_tile(S, 128, 16), name=name)


def mla_heads_bwd(dQ, dK, dV, q, kv, cos2, sinm, qg, kg, H, name):
    S = q.shape[0]
    KV = NOPE + VDIM

    def fn(t, c, o):
        cs, sn = t[5][...], t[6][...]
        qgn, qgr, kgn = c[0][:, 0:NOPE], c[0][:, NOPE:QK], c[1][:, 0:NOPE]
        a_qn = jnp.zeros((1, NOPE), F32)
        a_qr = jnp.zeros((1, ROPE), F32)
        a_kn = jnp.zeros((1, NOPE), F32)
        dkr = jnp.zeros((t[0].shape[1], ROPE), F32)
        for h in range(H):
            qn = t[3][:, QK * h:QK * h + NOPE]
            rs = _rstd(qn)
            dx, dg = _rms_bwd(t[0][h, :, 0:NOPE], qn * rs, rs, qgn)
            o[0][:, QK * h:QK * h + NOPE] = dx.astype(BF16)
            a_qn = a_qn + dg
            qr = t[3][:, QK * h + NOPE:QK * (h + 1)]
            rs = _rstd(qr)
            dx, dg = _rms_bwd(_unrope(t[0][h, :, NOPE:QK], cs, sn), qr * rs, rs, qgr)
            o[0][:, QK * h + NOPE:QK * (h + 1)] = dx.astype(BF16)
            a_qr = a_qr + dg
            kn = t[4][:, KV * h:KV * h + NOPE]
            rs = _rstd(kn)
            dx, dg = _rms_bwd(t[1][h, :, 0:NOPE], kn * rs, rs, kgn)
            o[1][:, KV * h:KV * h + NOPE] = dx.astype(BF16)
            a_kn = a_kn + dg
            o[1][:, KV * h + NOPE:KV * (h + 1)] = t[2][h].astype(BF16)
            dkr = dkr + t[1][h, :, NOPE:QK]
        o[2][...] = _unrope(dkr, cs, sn)
        return [a_qn, a_qr, a_kn]

    return rowwise(fn, [dQ, dK, dV, q, kv, cos2, sinm], [qg, kg],
                   [((S, H * QK), BF16), ((S, H * KV), BF16), ((S, ROPE), F32)],
                   [(1, NOPE), (1, ROPE), (1, NOPE)], ts=_tile(S, 128, 16), name=name)


def mla_norms_bwd(dqn, dkvn, dkr, proj, qg, kvg, kgr, R, name):
    S = proj.shape[0]

    def fn(t, c, o):
        reds = []
        for idx, (lo, hi) in enumerate(((0, R), (R, 2 * R), (2 * R, 2 * R + ROPE))):
            xv = t[3][:, lo:hi]
            rs = _rstd(xv)
            dx, dg = _rms_bwd(t[idx][...], xv * rs, rs, c[idx][...])
            o[0][:, lo:hi] = dx.astype(BF16)
            reds.append(dg)
        return reds

    return rowwise(fn, [dqn, dkvn, dkr, proj], [qg, kvg, kgr], [((S, 2 * R + ROPE), BF16)],
                   [(1, R), (1, R), (1, ROPE)], ts=_tile(S, 128, 16), name=name)


def _tri_rows(p, n):
    qi = 0
    for j in range(1, n):
        qi = qi + (p >= j * (j + 1) // 2).astype(jnp.int32)
    return qi, p - (qi * (qi + 1)) // 2


def _tri_cols(p, n):
    ki = 0
    for j in range(1, n):
        ki = ki + (p >= j * n - j * (j - 1) // 2).astype(jnp.int32)
    return ki, ki + p - (ki * n - (ki * (ki - 1)) // 2)


def attn_fwd(Q, K, V, name):
    H, S, _ = Q.shape
    t = _tile(S, 512)
    n = S // t
    scale = QK ** -0.5

    rs = _tile(t, ATTN_STRIP, 8)

    def body(q_ref, k_ref, v_ref, o_ref, lse_ref, m_s, l_s, acc, s_scr, p_scr):
        qi, ki = _tri_rows(pl.program_id(1), n)

        @pl.when(ki == 0)
        def _():
            m_s[...] = jnp.full_like(m_s, NEG)
            l_s[...] = jnp.zeros_like(l_s)
            acc[...] = jnp.zeros_like(acc)

        def block(diagonal):
            s_scr[...] = lax.dot_general(q_ref[...], k_ref[...], _DIMS["nt"], preferred_element_type=F32)

            def strip(i, cr):
                r = slice(i * rs, (i + 1) * rs)
                s = s_scr[r, :] * scale
                if diagonal:
                    row = i * rs + lax.broadcasted_iota(jnp.int32, (rs, t), 0)
                    s = jnp.where(lax.broadcasted_iota(jnp.int32, (rs, t), 1) <= row, s, NEG)
                m_old = m_s[r, :]
                m_new = jnp.maximum(m_old, jnp.max(s, axis=-1, keepdims=True))
                alpha = jnp.exp(m_old - m_new)
                p = jnp.exp(s - m_new)
                l_s[r, :] = alpha * l_s[r, :] + jnp.sum(p, axis=-1, keepdims=True)
                m_s[r, :] = m_new
                acc[r, :] = alpha * acc[r, :]
                p_scr[r, :] = p.astype(BF16)
                return cr

            for i in range(t // rs):
                strip(i, 0)
            acc[...] += jnp.dot(p_scr[...], v_ref[...], preferred_element_type=F32)

        @pl.when(ki < qi)
        def _():
            block(False)

        @pl.when(ki == qi)
        def _():
            block(True)

        @pl.when(ki == qi)
        def _():
            o_ref[...] = (acc[...] / l_s[...]).astype(o_ref.dtype)
            lse_ref[...] = m_s[...] + jnp.log(l_s[...])

    return _pcall(
        body, name=name, grid=(H, n * (n + 1) // 2),
        in_specs=[pl.BlockSpec((None, t, QK), lambda h, p: (h, _tri_rows(p, n)[0], 0)),
                  pl.BlockSpec((None, t, QK), lambda h, p: (h, _tri_rows(p, n)[1], 0)),
                  pl.BlockSpec((None, t, VDIM), lambda h, p: (h, _tri_rows(p, n)[1], 0))],
        out_specs=[pl.BlockSpec((t, VDIM), lambda h, p: (_tri_rows(p, n)[0], h)),
                   pl.BlockSpec((None, t, 1), lambda h, p: (h, _tri_rows(p, n)[0], 0))],
        out_shape=[jax.ShapeDtypeStruct((S, H * VDIM), BF16), jax.ShapeDtypeStruct((H, S, 1), F32)],
        scratch_shapes=[pltpu.VMEM((t, 1), F32), pltpu.VMEM((t, 1), F32), pltpu.VMEM((t, VDIM), F32),
                        pltpu.VMEM((t, t), F32), pltpu.VMEM((t, t), BF16)],
        compiler_params=_params(("parallel", "arbitrary")),
    )(Q, K, V)


def attn_bwd(Q, K, V, dO, O, lse, name):
    H, S, _ = Q.shape
    t = _tile(S, 512)
    n = S // t
    scale = QK ** -0.5

    rs = _tile(t, ATTN_STRIP, 8)

    def body(q_ref, k_ref, v_ref, do_ref, o_ref, lse_ref, dq_ref, dk_ref, dv_ref, s_scr, dp_scr, p_scr, ds_scr):
        ki, qi = _tri_cols(pl.program_id(1), n)

        @pl.when(pl.program_id(1) == 0)
        def _():
            dq_ref[...] = jnp.zeros_like(dq_ref)

        @pl.when(qi == ki)
        def _():
            dk_ref[...] = jnp.zeros_like(dk_ref)
            dv_ref[...] = jnp.zeros_like(dv_ref)

        def block(diagonal):
            s_scr[...] = lax.dot_general(q_ref[...], k_ref[...], _DIMS["nt"], preferred_element_type=F32)
            dp_scr[...] = lax.dot_general(do_ref[...], v_ref[...], _DIMS["nt"], preferred_element_type=F32)

            def strip(i, cr):
                r = slice(i * rs, (i + 1) * rs)
                s = s_scr[r, :] * scale
                if diagonal:
                    row = i * rs + lax.broadcasted_iota(jnp.int32, (rs, t), 0)
                    s = jnp.where(lax.broadcasted_iota(jnp.int32, (rs, t), 1) <= row, s, NEG)
                p = jnp.exp(s - lse_ref[r, :])
                delta = jnp.sum(do_ref[r, :].astype(F32) * o_ref[r, :].astype(F32), axis=-1, keepdims=True)
                p_scr[r, :] = p.astype(BF16)
                ds_scr[r, :] = (p * (dp_scr[r, :] - delta) * scale).astype(BF16)
                return cr

            for i in range(t // rs):
                strip(i, 0)
            ds = ds_scr[...]
            dv_ref[...] += lax.dot_general(p_scr[...], do_ref[...], _DIMS["tn"], preferred_element_type=F32)
            dk_ref[...] += lax.dot_general(ds, q_ref[...], _DIMS["tn"], preferred_element_type=F32)
            rq = pl.multiple_of(qi * t, t)
            dq_ref[pl.ds(rq, t), :] += jnp.dot(ds, k_ref[...], preferred_element_type=F32)

        @pl.when(qi > ki)
        def _():
            block(False)

        @pl.when(qi == ki)
        def _():
            block(True)

    qmap = lambda h, p: (h, _tri_cols(p, n)[1], 0)
    kmap = lambda h, p: (h, _tri_cols(p, n)[0], 0)
    return _pcall(
        body, name=name, grid=(H, n * (n + 1) // 2),
        in_specs=[pl.BlockSpec((None, t, QK), qmap),
                  pl.BlockSpec((None, t, QK), kmap),
                  pl.BlockSpec((None, t, VDIM), kmap),
                  pl.BlockSpec((t, VDIM), lambda h, p: (_tri_cols(p, n)[1], h)),
                  pl.BlockSpec((t, VDIM), lambda h, p: (_tri_cols(p, n)[1], h)),
                  pl.BlockSpec((None, t, 1), qmap)],
        out_specs=[pl.BlockSpec((None, S, QK), lambda h, p: (h, 0, 0)),
                   pl.BlockSpec((None, t, QK), kmap),
                   pl.BlockSpec((None, t, VDIM), kmap)],
        out_shape=[jax.ShapeDtypeStruct((H, S, QK), F32), jax.ShapeDtypeStruct((H, S, QK), F32),
                   jax.ShapeDtypeStruct((H, S, VDIM), F32)],
        scratch_shapes=[pltpu.VMEM((t, t), F32), pltpu.VMEM((t, t), F32), pltpu.VMEM((t, t), BF16),
                        pltpu.VMEM((t, t), BF16)],
        compiler_params=_params(("parallel", "arbitrary")),
    )(Q, K, V, dO, O, lse)


def adamw(w, m, v, g, *, row0, name):
    P, rows, C = g.shape
    tr = _tile(rows, max(16, 131072 // C), 16)
    off = row0 // tr
    assert row0 % tr == 0
    bc1 = 1.0 - ADAM_B1 ** ADAM_STEP
    bc2 = 1.0 - ADAM_B2 ** ADAM_STEP

    def body(w_ref, m_ref, v_ref, g_ref, go_ref, d_ref, mo_ref, vo_ref):
        gs = g_ref[0].astype(F32)
        for p in range(1, P):
            gs = gs + g_ref[p].astype(F32)
        wv = w_ref[...]
        mn = ADAM_B1 * m_ref[...] + (1.0 - ADAM_B1) * gs
        vn = ADAM_B2 * v_ref[...] + (1.0 - ADAM_B2) * (gs * gs)
        go_ref[...] = gs
        mo_ref[...] = mn
        vo_ref[...] = vn
        d_ref[...] = -ADAM_LR * ((mn / bc1) / (jnp.sqrt(vn / bc2) + ADAM_EPS) + ADAM_WD * wv)

    wspec = pl.BlockSpec((tr, C), lambda i: (i + off, 0))
    ospec = pl.BlockSpec((tr, C), lambda i: (i, 0))
    return _pcall(
        body, name=name, grid=(rows // tr,),
        in_specs=[wspec, wspec, wspec, pl.BlockSpec((P, tr, C), lambda i: (0, i, 0))],
        out_specs=[ospec] * 4, out_shape=[jax.ShapeDtypeStruct((rows, C), F32)] * 4,
        compiler_params=_params(("parallel",)),
    )(w, m, v, g)


def _coords():
    return lax.axis_index("x"), lax.axis_index("y"), lax.axis_index("c")


def _me():
    x, y, c = _coords()
    return 4 * x + 2 * y + c


_ANY = pl.BlockSpec(memory_space=pl.ANY)


def all_gather(items, name):
    n = len(items)
    blks = [a.shape if idx is None else a.shape[1:] for a, idx in items]

    def body(*refs):
        ins, outs = refs[:n], refs[n:2 * n]
        send, recv, lsem = refs[2 * n:]
        x, y, c = _coords()
        me, sib = (x, y, c), (x, y, 1 - c)
        chips = [(1 - x, y), (x, 1 - y), (1 - x, 1 - y)]

        def src(i):
            return ins[i] if items[i][1] is None else ins[i].at[items[i][1]]

        def slot(i, p):
            return outs[i].at[4 * p[0] + 2 * p[1] + p[2]]

        def cp(i, k, block, to, s=None):
            return pltpu.make_async_remote_copy(
                src_ref=slot(i, block) if s is None else s, dst_ref=slot(i, block),
                send_sem=send.at[7 * i + k], recv_sem=recv.at[7 * i + k], device_id=to, device_id_type=MESH)

        mine = [pltpu.make_async_copy(src(i), slot(i, me), lsem.at[i]) for i in range(n)]
        for m_ in mine:
            m_.start()
        first = []
        for i in range(n):
            first.append(cp(i, 0, me, sib, src(i)))
            first += [cp(i, 1 + j, me, (*chip, c), src(i)) for j, chip in enumerate(chips)]
        for f in first:
            f.start()
        passed = []
        for j, chip in enumerate(chips):
            for i in range(n):
                cp(i, 1 + j, (*chip, c), me).wait_recv()
                p_ = cp(i, 4 + j, (*chip, c), sib)
                p_.start()
                passed.append(p_)
        for i in range(n):
            cp(i, 0, sib, me).wait_recv()
            for j, chip in enumerate(chips):
                cp(i, 4 + j, (*chip, 1 - c), me).wait_recv()
        for f in first + passed:
            f.wait_send()
        for m_ in mine:
            m_.wait()

    res = _pcall(
        body, name=name, in_specs=[_ANY] * n, out_specs=[_ANY] * n,
        out_shape=[jax.ShapeDtypeStruct((NDEV,) + tuple(b), a.dtype) for b, (a, _) in zip(blks, items)],
        scratch_shapes=[pltpu.SemaphoreType.DMA((7 * n,)), pltpu.SemaphoreType.DMA((7 * n,)),
                        pltpu.SemaphoreType.DMA((n,))],
    )(*[a for a, _ in items])
    return list(res)


_HBM = pl.BlockSpec(memory_space=pltpu.HBM)
_SEM = pl.BlockSpec(memory_space=pltpu.SEMAPHORE)
_EFFECT = pltpu.SideEffectType.DATAFLOW_SIDE_EFFECTING


def _xchg_copy(src_ref, land_ref, send, recv, r, scatter, at_peer):
    x, y, c = _coords()
    px = jnp.bitwise_xor(x, (r >> 2) & 1)
    py = jnp.bitwise_xor(y, (r >> 1) & 1)
    pc = jnp.bitwise_xor(c, r & 1)
    p_i = 4 * px + 2 * py + pc
    me_i = 4 * x + 2 * y + c
    return pltpu.make_async_remote_copy(
        src_ref=src_ref.at[p_i] if scatter else src_ref, dst_ref=land_ref.at[p_i if at_peer else me_i],
        send_sem=send.at[r - 1], recv_sem=recv.at[r - 1], device_id=(px, py, pc), device_id_type=MESH)


def xchg_start(srcs, scatter, name, after=None):
    n = len(srcs)
    me = _me()
    lands = []
    for s in srcs:
        blk = s.shape[1:] if scatter else s.shape
        own = lax.dynamic_index_in_dim(s, me, 0, keepdims=True) if scatter else s[None]
        lands.append(lax.dynamic_update_index_in_dim(lax.empty((NDEV,) + tuple(blk), s.dtype), own, me, 0))

    def body(*refs):
        s_in, l_in = refs[:n], refs[n:2 * n]
        outs = refs[2 * n + (after is not None):]
        sends, recvs = outs[:n], outs[n:2 * n]
        token = outs[4 * n]
        for i in range(n):
            for r in range(1, NDEV):
                _xchg_copy(s_in[i], l_in[i], sends[i], recvs[i], r, scatter, False).start()
        token[...] = jnp.zeros_like(token)

    hbm = lambda a: pltpu.HBM(a.shape, a.dtype)
    res = _pcall(
        body, name=name,
        out_shape=tuple([pltpu.SemaphoreType.DMA((NDEV - 1,))] * (2 * n) + [hbm(a) for a in srcs] + [hbm(a) for a in lands]
                        + [jax.ShapeDtypeStruct((8, LANES), F32)]),
        in_specs=[_HBM] * (2 * n) + ([] if after is None else [_ANY]),
        out_specs=tuple([_SEM] * (2 * n) + [_HBM] * (2 * n) + [pl.BlockSpec(memory_space=pltpu.VMEM)]),
        input_output_aliases={i: 2 * n + i for i in range(2 * n)},
        compiler_params=pltpu.CompilerParams(has_side_effects=_EFFECT),
    )(*[pltpu.with_memory_space_constraint(a, pltpu.HBM) for a in list(srcs) + lands], *([] if after is None else [after]))
    handles = [(res[i], res[n + i], res[2 * n + i], res[3 * n + i]) for i in range(n)]
    return handles, res[4 * n]


def xchg_wait(handles, after, scatter, name):
    n = len(handles)

    def body(*refs):
        s_in, l_in = refs[:n], refs[n:2 * n]
        sends, recvs = refs[2 * n:3 * n], refs[3 * n:4 * n]
        for i in range(n):
            for r in range(1, NDEV):
                cp = _xchg_copy(s_in[i], l_in[i], sends[i], recvs[i], r, scatter, True)
                cp.wait_send()
                cp.wait_recv()

    srcs = [h[2] for h in handles]
    lands = [h[3] for h in handles]
    hbm = lambda a: pltpu.HBM(a.shape, a.dtype)
    res = _pcall(
        body, name=name,
        out_shape=tuple([hbm(a) for a in srcs] + [hbm(a) for a in lands]),
        in_specs=[_HBM] * (2 * n) + [_SEM] * (2 * n) + [_ANY],
        out_specs=tuple([_HBM] * (2 * n)),
        input_output_aliases={i: i for i in range(2 * n)},
        compiler_params=pltpu.CompilerParams(has_side_effects=_EFFECT),
    )(*srcs, *lands, *[h[0] for h in handles], *[h[1] for h in handles], after)
    return list(res[n:])


_PACK_ALIGN = 8 * LANES


def _pack(arrs):
    parts = []
    for a in arrs:
        f = a.reshape(-1).astype(F32)
        pad = (-f.shape[0]) % _PACK_ALIGN
        parts.append(jnp.pad(f, (0, pad)) if pad else f)
    return jnp.concatenate(parts).reshape(-1, LANES)


def _packed_rows(shape):
    n = 1
    for d in shape:
        n *= d
    return (n + _PACK_ALIGN - 1) // _PACK_ALIGN * 8


def _unpack(p, shapes, lead=()):
    nl = len(lead)
    out, r0 = [], 0
    for shp in shapes:
        n = 1
        for d in shp:
            n *= d
        nr = _packed_rows(shp)
        flat = lax.slice_in_dim(p, r0, r0 + nr, axis=nl).reshape(lead + (nr * LANES,))
        out.append(lax.slice_in_dim(flat, 0, n, axis=nl).reshape(lead + tuple(shp)))
        r0 += nr
    return out


def _shard_cols(a, me, width, axis):
    return lax.dynamic_slice_in_dim(a, me * width, width, axis=axis)


def kernel(x, c, norm1_g, norm2_g, ada_w, ada_b, mlp_w1, mlp_w2, ab_w_in, sgu_norm_g, sgu_w, sgu_b, conv_w, conv_b, conv_ln_g, conv_ln_b, ab_w_out, mla_w_in, mla_q_norm_g, mla_kv_norm_g, mla_w_uq, mla_w_ukv, mla_q_head_g, mla_k_head_g, mla_w_out, loss_target, m_norm1_g, m_norm2_g, m_ada_w, m_ada_b, m_mlp_w1, m_mlp_w2, m_ab_w_in, m_sgu_norm_g, m_sgu_w, m_sgu_b, m_conv_w, m_conv_b, m_conv_ln_g, m_conv_ln_b, m_ab_w_out, m_mla_w_in, m_mla_q_norm_g, m_mla_kv_norm_g, m_mla_w_uq, m_mla_w_ukv, m_mla_q_head_g, m_mla_k_head_g, m_mla_w_out, v_norm1_g, v_norm2_g, v_ada_w, v_ada_b, v_mlp_w1, v_mlp_w2, v_ab_w_in, v_sgu_norm_g, v_sgu_w, v_sgu_b, v_conv_w, v_conv_b, v_conv_ln_g, v_conv_ln_b, v_ab_w_out, v_mla_w_in, v_mla_q_norm_g, v_mla_kv_norm_g, v_mla_w_uq, v_mla_w_ukv, v_mla_q_head_g, v_mla_k_head_g, v_mla_w_out):
    W = dict(norm1_g=norm1_g, norm2_g=norm2_g, ada_w=ada_w, ada_b=ada_b, mlp_w1=mlp_w1, mlp_w2=mlp_w2, ab_w_in=ab_w_in,
             sgu_norm_g=sgu_norm_g, sgu_w=sgu_w, sgu_b=sgu_b, conv_w=conv_w, conv_b=conv_b, conv_ln_g=conv_ln_g,
             conv_ln_b=conv_ln_b, ab_w_out=ab_w_out, mla_w_in=mla_w_in, mla_q_norm_g=mla_q_norm_g,
             mla_kv_norm_g=mla_kv_norm_g, mla_w_uq=mla_w_uq, mla_w_ukv=mla_w_ukv, mla_q_head_g=mla_q_head_g,
             mla_k_head_g=mla_k_head_g, mla_w_out=mla_w_out)
    M = dict(norm1_g=m_norm1_g, norm2_g=m_norm2_g, ada_w=m_ada_w, ada_b=m_ada_b, mlp_w1=m_mlp_w1, mlp_w2=m_mlp_w2,
             ab_w_in=m_ab_w_in, sgu_norm_g=m_sgu_norm_g, sgu_w=m_sgu_w, sgu_b=m_sgu_b, conv_w=m_conv_w, conv_b=m_conv_b,
             conv_ln_g=m_conv_ln_g, conv_ln_b=m_conv_ln_b, ab_w_out=m_ab_w_out, mla_w_in=m_mla_w_in,
             mla_q_norm_g=m_mla_q_norm_g, mla_kv_norm_g=m_mla_kv_norm_g, mla_w_uq=m_mla_w_uq, mla_w_ukv=m_mla_w_ukv,
             mla_q_head_g=m_mla_q_head_g, mla_k_head_g=m_mla_k_head_g, mla_w_out=m_mla_w_out)
    V = dict(norm1_g=v_norm1_g, norm2_g=v_norm2_g, ada_w=v_ada_w, ada_b=v_ada_b, mlp_w1=v_mlp_w1, mlp_w2=v_mlp_w2,
             ab_w_in=v_ab_w_in, sgu_norm_g=v_sgu_norm_g, sgu_w=v_sgu_w, sgu_b=v_sgu_b, conv_w=v_conv_w, conv_b=v_conv_b,
             conv_ln_g=v_conv_ln_g, conv_ln_b=v_conv_ln_b, ab_w_out=v_ab_w_out, mla_w_in=v_mla_w_in,
             mla_q_norm_g=v_mla_q_norm_g, mla_kv_norm_g=v_mla_kv_norm_g, mla_w_uq=v_mla_w_uq, mla_w_ukv=v_mla_w_ukv,
             mla_q_head_g=v_mla_q_head_g, mla_k_head_g=v_mla_k_head_g, mla_w_out=v_mla_w_out)
    ORDER = list(W)

    S, D = x.shape[1], x.shape[2]
    L, NE, NO = norm1_g.shape[0], ab_w_in.shape[0], mla_w_in.shape[0]
    DA = D // 2
    DB = D - DA
    G = DA // GROUP
    R = NDEV * mla_q_norm_g.shape[1]
    H = NDEV * mla_w_uq.shape[2] // QK
    AW = ada_w.shape[2]
    CB = conv_w.shape[2]
    me = _me()
    xs, tgt = x[0], loss_target[0]

    BIG_EVEN = ("mlp_w1", "mlp_w2", "ab_w_in", "ab_w_out")
    BIG_ODD = ("mlp_w1", "mlp_w2", "mla_w_in", "mla_w_uq", "mla_w_ukv", "mla_w_out")
    BIG = ("mlp_w1", "mlp_w2", "ab_w_in", "ab_w_out", "mla_w_in", "mla_w_uq", "mla_w_ukv", "mla_w_out")
    COL_SHARDED = ("mlp_w1", "ab_w_in", "mla_w_uq", "mla_w_ukv")
    MLP_W = ("mlp_w1", "mlp_w2")
    mixer_w = lambda l: ("ab_w_in", "ab_w_out") if l % 2 == 0 else ("mla_w_in", "mla_w_uq", "mla_w_ukv", "mla_w_out")
    widx = lambda k, l: l if k in MLP_W else l // 2

    small_in = [c, mla_q_norm_g, mla_kv_norm_g, conv_w]
    sg = all_gather([(_pack(small_in), None)], "gather_small")[0]
    c_all, qng_all, kvng_all, cw_all = _unpack(sg, [a.shape for a in small_in], (NDEV,))
    c_all = c_all.reshape(NDEV, D)
    qng_full = jnp.transpose(qng_all, (1, 0, 2)).reshape(NO, 1, R)
    kvng_full = jnp.transpose(kvng_all, (1, 0, 2)).reshape(NO, 1, R)
    cw_full = jnp.transpose(cw_all, (1, 2, 0, 3)).reshape(NE, CONV_W, DB)
    cw_pad = jnp.pad(cw_full, ((0, 0), (0, CONV_PAD - CONV_W), (0, 0)))

    def silu_fn(t, c_, o):
        v_ = t[0][...]
        o[0][...] = v_ * _sigmoid(v_)
        return []

    c_act = rowwise(silu_fn, [c_all], outs=[((NDEV, D), F32)], ts=NDEV, name="silu_c")[0]
    bias_cols = _shard_cols(ada_b, me, AW, 1).reshape(1, L * AW)
    mod_cols = mm(c_act, ada_w, "nn", name="ada_fwd", b_blocked=True, rowvecs=[bias_cols],
                  epi=lambda acc, b_: (acc + b_,), tm=NDEV, tn=768)
    mod_all = all_gather([(mod_cols, None)], "gather_mod")[0]
    mod = lax.dynamic_index_in_dim(mod_all, me, axis=1, keepdims=False)
    mod = jnp.transpose(mod.reshape(NDEV, L, AW), (1, 0, 2)).reshape(L, 6, 1, D)

    g_handles = []
    tok_sum = jnp.zeros((1, 1), F32)
    for l in range(L):
        names = mixer_w(l) + MLP_W
        srcs = [cast_bf16(W[k], widx(k, l), "cast_%s_l%d" % (k, l)) for k in names]
        handles, tok = xchg_start(srcs, False, "gather_start_l%d" % l, after=mod_all)
        g_handles.append(dict(zip(names, handles)))
        tok_sum = tok_sum + tok[0:1, 0:1]
    mod = mod + tok_sum

    def wait_weights(l, names, after, what):
        lands = xchg_wait([g_handles[l][k] for k in names], after, False, "gather_wait_%s_l%d" % (what, l))
        return {k: (ld if k in COL_SHARDED else ld.reshape(NDEV * ld.shape[1], ld.shape[2])) for k, ld in zip(names, lands)}

    pos = jnp.arange(S, dtype=F32)
    inv = ROPE_THETA ** (-jnp.arange(0, ROPE, 2, dtype=F32) / ROPE)
    ang = pos[:, None] * inv[None, :]
    cos2 = jnp.concatenate([jnp.cos(ang), jnp.cos(ang)], axis=1)
    sinm = jnp.concatenate([-jnp.sin(ang), jnp.sin(ang)], axis=1)

    residual = lambda acc, xr, gt: (acc, xr + gt * acc)

    saved = []
    xc = xs
    for l in range(L):
        sh1, sc1, g1, sh2, sc2, g2 = [mod[l, k] for k in range(6)]
        tag = "_l%d" % l
        sv = dict(x0=xc)
        h, sv["rstd1"] = prenorm(xc, norm1_g[l][None], sc1, sh1, "prenorm1" + tag)
        sv["h"] = h
        wl = wait_weights(l, mixer_w(l), h, "mix")
        if l % 2 == 0:
            e = l // 2
            ng = sgu_norm_g[e].reshape(1, DA)
            bcol = sgu_b[e][:, :, None]
            proj = mm(h, wl["ab_w_in"], "nn", name="ab_in" + tag, b_blocked=True)
            out_a = sgu_fwd(proj, ng, sgu_w[e], bcol, DA, "sgu_fwd" + tag)
            y1 = conv_fwd(proj, cw_pad[e], conv_b[e][None], DA, DB, "conv_fwd" + tag)
            out_b = ln_silu(y1, conv_ln_g[e][None], conv_ln_b[e][None], "ln_silu" + tag)
            cat = jnp.concatenate([out_a, out_b], axis=1)
            sv.update(proj=proj, y1=y1, cat=cat)
            mixb, x1 = mm(cat, wl["ab_w_out"], "nn", name="ab_out" + tag, out_dtypes=(BF16, F32), epi=residual,
                          extras=[xc], rowvecs=[g1])
        else:
            o_ = l // 2
            proj = mm(h, wl["mla_w_in"], "nn", name="mla_in" + tag)
            kgr = mla_k_head_g[o_][None, NOPE:QK]
            qn, kvn, kr = mla_norms(proj, cos2, sinm, qng_full[o_], kvng_full[o_], kgr, R, "mla_norms" + tag)
            q = mm(qn, wl["mla_w_uq"], "nn", name="mla_uq" + tag, b_blocked=True)
            kv = mm(kvn, wl["mla_w_ukv"], "nn", name="mla_ukv" + tag, b_blocked=True)
            Qh, Kh, Vh = mla_heads(q, kv, kr, cos2, sinm, mla_q_head_g[o_][None], mla_k_head_g[o_][None], H,
                                   "mla_heads" + tag)
            att, lse = attn_fwd(Qh, Kh, Vh, "attn_fwd" + tag)
            sv.update(proj=proj, qn=qn, kvn=kvn, q=q, kv=kv, Qh=Qh, Kh=Kh, Vh=Vh, att=att, lse=lse)
            mixb, x1 = mm(att, wl["mla_w_out"], "nn", name="mla_out" + tag, out_dtypes=(BF16, F32), epi=residual,
                          extras=[xc], rowvecs=[g1])
        sv.update(mixb=mixb, x1=x1)
        h2, sv["rstd2"] = prenorm(x1, norm2_g[l][None], sc2, sh2, "prenorm2" + tag)
        wl.update(wait_weights(l, MLP_W, h2, "mlp"))
        sv["w"] = wl
        z, act = mm(h2, wl["mlp_w1"], "nn", name="mlp_up" + tag, b_blocked=True, out_dtypes=(BF16, BF16),
                    epi=lambda acc: (acc, jnp.square(jnp.maximum(acc, 0.0))))
        yb, xc = mm(act, wl["mlp_w2"], "nn", name="mlp_down" + tag, out_dtypes=(BF16, F32), epi=residual,
                    extras=[x1], rowvecs=[g2])
        sv.update(h2=h2, z=z, act=act, yb=yb)
        saved.append(sv)

    dx, loss_cols = loss_grad(xc, tgt, "loss")
    loss = lax.psum(0.5 / D * jnp.sum(loss_cols), ("x", "y", "c"))

    big_out = {k: [None] * W[k].shape[0] for k in BIG}
    sm = {k: [None] * W[k].shape[0] for k in ("norm1_g", "norm2_g", "sgu_norm_g", "sgu_w", "sgu_b", "conv_b", "conv_ln_g",
                                               "conv_ln_b", "mla_q_head_g", "mla_k_head_g", "mla_q_norm_g",
                                               "mla_kv_norm_g", "conv_w")}
    dmod = [None] * L
    flat2 = {k: W[k].reshape(-1, W[k].shape[2]) for k in BIG}
    flat2m = {k: M[k].reshape(-1, W[k].shape[2]) for k in BIG}
    flat2v = {k: V[k].reshape(-1, W[k].shape[2]) for k in BIG}

    def scatter_start(gr, what, tag):
        names = list(gr)
        blocks = [gr[k] if k in COL_SHARDED else gr[k].reshape(NDEV, gr[k].shape[0] // NDEV, gr[k].shape[1])
                  for k in names]
        handles, tok = xchg_start(blocks, True, "scatter_start_%s%s" % (what, tag))
        return (names, handles, what, tag), tok[0:1, 0:1]

    def scatter_finish(pending, after, l):
        names, handles, what, tag = pending
        landed = xchg_wait(handles, after, True, "scatter_wait_%s%s" % (what, tag))
        for k, land in zip(names, landed):
            li = widx(k, l)
            big_out[k][li] = adamw(flat2[k], flat2m[k], flat2v[k], land, row0=li * W[k].shape[1],
                                   name="adamw_%s%s" % (k, tag))

    pend_mix, tok_mix = None, None
    for l in reversed(range(L)):
        sh1, sc1, g1, sh2, sc2, g2 = [mod[l, k] for k in range(6)]
        if tok_mix is not None:
            g2 = g2 + tok_mix
        sv = saved[l]
        wl = sv["w"]
        tag = "_l%d" % l
        gr = {}
        dy, dgate2 = gate_bwd(dx, sv["yb"], g2, "gate2_bwd" + tag)
        dz = mm(dy, wl["mlp_w2"], "nt", name="mlp_down_dx" + tag, out_dtypes=(BF16,), extras=[sv["z"]],
                epi=lambda acc, z_: (acc * (2.0 * jnp.maximum(z_.astype(F32), 0.0)),))
        gr["mlp_w2"] = mm(sv["act"], dy, "tn", name="mlp_down_dw" + tag, out_dtypes=(BF16,))
        dh2 = mm(dz, wl["mlp_w1"], "nt", name="mlp_up_dx" + tag, b_blocked=True)
        gr["mlp_w1"] = mm(sv["h2"], dz, "tn", name="mlp_up_dw" + tag, out_dtypes=(BF16,), out_blocked=NDEV)
        dx1, dsc2, dsh2, sm["norm2_g"][l] = norm_bwd(dh2, sv["x1"], sv["rstd2"], dx, norm2_g[l][None], sc2,
                                                     "norm2_bwd" + tag)
        pend_mlp, tok_mlp = scatter_start(gr, "mlp", tag)
        gr = {}
        if pend_mix is not None:
            scatter_finish(pend_mix, dx1, l + 1)
        dmix, dgate1 = gate_bwd(dx1, sv["mixb"], g1 + tok_mlp, "gate1_bwd" + tag)
        if l % 2 == 0:
            e = l // 2
            ng = sgu_norm_g[e].reshape(1, DA)
            bcol = sgu_b[e][:, :, None]
            dcat = mm(dmix, wl["ab_w_out"], "nt", name="ab_out_dx" + tag)
            gr["ab_w_out"] = mm(sv["cat"], dmix, "tn", name="ab_out_dw" + tag, out_dtypes=(BF16,))
            dy1, sm["conv_ln_g"][e], sm["conv_ln_b"][e], sm["conv_b"][e] = ln_silu_bwd(
                dcat, sv["y1"], conv_ln_g[e][None], conv_ln_b[e][None], "ln_silu_bwd" + tag)
            da, dg_, dwc = conv_bwd(sv["proj"], dy1, cw_pad[e], DA, DB, "conv_bwd" + tag)
            sm["conv_w"][e] = dwc[:CONV_W]
            duv, dsw, dsb, dsng = sgu_bwd(sv["proj"], dcat, ng, sgu_w[e], bcol, DA, "sgu_bwd" + tag)
            sm["sgu_w"][e], sm["sgu_b"][e], sm["sgu_norm_g"][e] = dsw, dsb, dsng
            dproj = jnp.concatenate([duv, da, dg_], axis=1)
            dh = mm(dproj, wl["ab_w_in"], "nt", name="ab_in_dx" + tag, b_blocked=True)
            gr["ab_w_in"] = mm(sv["h"], dproj, "tn", name="ab_in_dw" + tag, out_dtypes=(BF16,), out_blocked=NDEV)
        else:
            o_ = l // 2
            kgr = mla_k_head_g[o_][None, NOPE:QK]
            dO = mm(dmix, wl["mla_w_out"], "nt", name="mla_out_dx" + tag, out_dtypes=(BF16,))
            gr["mla_w_out"] = mm(sv["att"], dmix, "tn", name="mla_out_dw" + tag, out_dtypes=(BF16,))
            dQ, dK, dV = attn_bwd(sv["Qh"], sv["Kh"], sv["Vh"], dO, sv["att"], sv["lse"], "attn_bwd" + tag)
            dq_pre, dkv_pre, dkr, dqgn, dqgr, dkgn = mla_heads_bwd(
                dQ, dK, dV, sv["q"], sv["kv"], cos2, sinm, mla_q_head_g[o_][None], mla_k_head_g[o_][None], H,
                "mla_heads_bwd" + tag)
            dqn = mm(dq_pre, wl["mla_w_uq"], "nt", name="mla_uq_dx" + tag, b_blocked=True)
            gr["mla_w_uq"] = mm(sv["qn"], dq_pre, "tn", name="mla_uq_dw" + tag, out_dtypes=(BF16,), out_blocked=NDEV)
            dkvn = mm(dkv_pre, wl["mla_w_ukv"], "nt", name="mla_ukv_dx" + tag, b_blocked=True)
            gr["mla_w_ukv"] = mm(sv["kvn"], dkv_pre, "tn", name="mla_ukv_dw" + tag, out_dtypes=(BF16,),
                                 out_blocked=NDEV)
            dproj, sm["mla_q_norm_g"][o_], sm["mla_kv_norm_g"][o_], dkgr = mla_norms_bwd(
                dqn, dkvn, dkr, sv["proj"], qng_full[o_], kvng_full[o_], kgr, R, "mla_norms_bwd" + tag)
            sm["mla_q_head_g"][o_] = jnp.concatenate([dqgn, dqgr], axis=1)
            sm["mla_k_head_g"][o_] = jnp.concatenate([dkgn, dkgr], axis=1)
            dh = mm(dproj, wl["mla_w_in"], "nt", name="mla_in_dx" + tag)
            gr["mla_w_in"] = mm(sv["h"], dproj, "tn", name="mla_in_dw" + tag, out_dtypes=(BF16,))
        dx, dsc1, dsh1, sm["norm1_g"][l] = norm_bwd(dh, sv["x0"], sv["rstd1"], dx1, norm1_g[l][None], sc1,
                                                    "norm1_bwd" + tag)
        dmod[l] = jnp.concatenate([dsh1, dsc1, dgate1, dsh2, dsc2, dgate2], axis=1)

        pend_mix, tok_mix = scatter_start(gr, "mix", tag)
        scatter_finish(pend_mlp, dx, l)
    scatter_finish(pend_mix, dx, 0)

    rep = ("norm1_g", "norm2_g", "sgu_norm_g", "sgu_w", "sgu_b", "conv_b", "conv_ln_g", "conv_ln_b", "mla_q_head_g",
           "mla_k_head_g")
    part_full = {"mla_q_norm_g": (NO, R), "mla_kv_norm_g": (NO, R), "conv_w": (NE, CONV_W, DB)}
    parts = [jnp.stack(sm[k]).reshape(W[k].shape) for k in rep]
    parts += [jnp.stack(sm[k]).reshape(part_full[k]) for k in part_full]
    parts.append(jnp.stack(dmod).reshape(L, 6 * D))
    shapes = [p.shape for p in parts]
    gp = all_gather([(_pack(parts), None)], "gather_smallgrads")[0]
    rows_p = gp.shape[1]
    r0 = sum(_packed_rows(s_) for s_ in shapes[:-1])
    per = 6 * D // LANES
    dm = lax.slice_in_dim(gp, r0, r0 + L * per, axis=1).reshape(NDEV, L, per, LANES)
    dmod_cols = lax.dynamic_slice_in_dim(dm, me * (AW // LANES), AW // LANES, axis=2).reshape(NDEV, L * AW)

    def sum_fn(t, c_, o):
        acc = t[0][0]
        for s_ in range(1, NDEV):
            acc = acc + t[0][s_]
        o[0][...] = acc
        return []

    gsummed = rowwise(sum_fn, [gp], outs=[((rows_p, LANES), F32)], ts=_tile(rows_p, 256, 8), name="sum_smallgrads")[0]
    gsum = dict(zip(list(rep) + list(part_full) + ["ada_b"], _unpack(gsummed, shapes)))
    gsum["mla_q_norm_g"] = _shard_cols(gsum["mla_q_norm_g"], me, R // NDEV, 1)
    gsum["mla_kv_norm_g"] = _shard_cols(gsum["mla_kv_norm_g"], me, R // NDEV, 1)
    gsum["conv_w"] = _shard_cols(gsum["conv_w"], me, CB, 2)
    small = list(rep) + list(part_full) + ["ada_b"]
    sm_shapes = [W[k].shape for k in small]
    sres = adamw(_pack([W[k] for k in small]), _pack([M[k] for k in small]), _pack([V[k] for k in small]),
                 _pack([gsum[k] for k in small])[None], row0=0, name="adamw_small")
    small_out = {k: vals for k, vals in zip(small, zip(*[_unpack(r_, sm_shapes) for r_ in sres]))}

    g_ada = mm(c_act, dmod_cols, "tn", name="ada_dw", out_blocked=L, tm=1024, tn=768, tk=NDEV, cast=None,
               precision=lax.Precision.HIGHEST)
    ada_out = adamw(ada_w.reshape(L * D, AW), m_ada_w.reshape(L * D, AW), v_ada_w.reshape(L * D, AW),
                    g_ada.reshape(1, L * D, AW), row0=0, name="adamw_ada_w")
    ada_out = [a.reshape(L, D, AW) for a in ada_out]

    def result(k, which):
        if k == "ada_w":
            return ada_out[which]
        if k in BIG:
            return jnp.stack([r_[which] for r_ in big_out[k]]).reshape(W[k].shape)
        return small_out[k][which]

    outs = [loss, dx[None]]
    for which in range(4):
        outs += [result(k, which) for k in ORDER]
    return tuple(outs)
```

```python
import functools

import jax
import jax.numpy as jnp
from jax import lax
from jax.experimental import pallas as pl
from jax.experimental.pallas import tpu as pltpu

F32 = jnp.float32
BF16 = jnp.bfloat16
EPS = 1e-6
NDEV = 8
LANES = 128
CHUNK = 128
GROUP = 128
CONV_W = 31
CONV_PAD = 32
NOPE, ROPE, VDIM = 128, 64, 128
QK = NOPE + ROPE
ROPE_THETA = 10000.0
VMEM_LIMIT = 56 * 1024 * 1024
ADAM_LR, ADAM_B1, ADAM_B2, ADAM_EPS, ADAM_WD, ADAM_STEP = 0.001, 0.9, 0.999, 1e-08, 0.01, 10
MESH = pl.DeviceIdType.MESH
NEG = -1e30
ATTN_BLOCK = 1024
ATTN_STRIP = 64


def _pcall(body, **kw):
    return pl.pallas_call(body, **kw)


def _params(sem=None):
    return pltpu.CompilerParams(dimension_semantics=sem, vmem_limit_bytes=VMEM_LIMIT)


def _tile(dim, target, align=LANES):
    if dim <= target:
        return dim
    t = (target // align) * align
    while t >= align:
        if dim % t == 0:
            return t
        t -= align
    return dim


def _rstd(x):
    return lax.rsqrt(jnp.mean(x * x, axis=-1, keepdims=True) + EPS)


def _sigmoid(x):
    return 1.0 / (1.0 + jnp.exp(-x))


_GC = 0.7978845608028654


def _gelu(x):
    return 0.5 * x * (1.0 + jnp.tanh(_GC * (x + 0.044715 * x * x * x)))


def _gelu_grad(x):
    t = jnp.tanh(_GC * (x + 0.044715 * x * x * x))
    return 0.5 * (1.0 + t) + 0.5 * x * (1.0 - t * t) * _GC * (1.0 + 3 * 0.044715 * x * x)


def _colsum(x):
    return jnp.sum(x, axis=0, keepdims=True)


def _rms_bwd(dy, xhat, rstd, g):
    dxh = dy * g
    dx = rstd * (dxh - xhat * jnp.mean(dxh * xhat, axis=-1, keepdims=True))
    return dx, _colsum(dy * xhat)


def _swap_halves(x):
    h = x.shape[-1] // 2
    return jnp.concatenate([x[:, h:], x[:, :h]], axis=1)


def _rope(x, cos2, sinm):
    return x * cos2 + _swap_halves(x) * sinm


def _unrope(dy, cos2, sinm):
    return dy * cos2 + _swap_halves(dy * sinm)


_DIMS = {"nn": (((1,), (0,)), ((), ())), "nt": (((1,), (1,)), ((), ())), "tn": (((0,), (0,)), ((), ()))}


def mm(a, b, mode, *, name, out_dtypes=(F32,), epi=None, extras=(), rowvecs=(), b_blocked=False, out_blocked=0,
       tm=1024, tn=1024, tk=2048, precision=None, cast=BF16):
    if mode == "tn":
        K, M = a.shape
    else:
        M, K = a.shape
    if b_blocked:
        J, Rb, Cb = b.shape
        N = Rb if mode == "nt" else J * Cb
    else:
        N = b.shape[0] if mode == "nt" else b.shape[1]
    tm = _tile(M, tm)
    if mode == "nn" and b_blocked:
        tn = _tile(Cb, tn)
    elif out_blocked:
        tn = _tile(N // out_blocked, tn)
    else:
        tn = _tile(N, tn)
    kb = 1
    if mode == "nt" and b_blocked:
        if Cb >= tk:
            tk = _tile(Cb, tk)
        else:
            kb = max(d for d in range(1, J + 1) if J % d == 0 and d * Cb <= tk)
            tk = kb * Cb
    else:
        tk = _tile(K, tk)
    nk = K // tk
    grid = (M // tm, N // tn, nk)

    if mode == "tn":
        a_spec = pl.BlockSpec((tk, tm), lambda i, j, k: (k, i))
    else:
        a_spec = pl.BlockSpec((tm, tk), lambda i, j, k: (i, k))
    if mode == "nn":
        if b_blocked:
            nper = Cb // tn
            b_spec = pl.BlockSpec((None, tk, tn), lambda i, j, k: (j // nper, k, j % nper))
        else:
            b_spec = pl.BlockSpec((tk, tn), lambda i, j, k: (k, j))
    elif mode == "nt":
        if b_blocked:
            if kb > 1:
                b_spec = pl.BlockSpec((kb, tn, Cb), lambda i, j, k: (k, j, 0))
            else:
                kper = Cb // tk
                b_spec = pl.BlockSpec((None, tn, tk), lambda i, j, k: (k // kper, j, k % kper))
        else:
            b_spec = pl.BlockSpec((tn, tk), lambda i, j, k: (j, k))
    else:
        b_spec = pl.BlockSpec((tk, tn), lambda i, j, k: (k, j))
    if out_blocked:
        oper = (N // out_blocked) // tn
        o_spec = pl.BlockSpec((None, tm, tn), lambda i, j, k: (j // oper, i, j % oper))
        o_shape = (out_blocked, M, N // out_blocked)
    else:
        o_spec = pl.BlockSpec((tm, tn), lambda i, j, k: (i, j))
        o_shape = (M, N)
    e_spec = pl.BlockSpec((tm, tn), lambda i, j, k: (i, j))
    r_spec = pl.BlockSpec((1, tn), lambda i, j, k: (0, j))
    ne, nr, no = len(extras), len(rowvecs), len(out_dtypes)
    dims = _DIMS[mode]

    def body(a_ref, b_ref, *rest):
        ex = rest[:ne]
        rv = rest[ne:ne + nr]
        outs = rest[ne + nr:ne + nr + no]

        def product():
            if kb > 1:
                r = None
                for q in range(kb):
                    av, bv = a_ref[:, q * Cb:(q + 1) * Cb], b_ref[q]
                    if cast is not None:
                        av, bv = av.astype(cast), bv.astype(cast)
                    d = lax.dot_general(av, bv, dims, preferred_element_type=F32, precision=precision)
                    r = d if r is None else r + d
                return r
            av, bv = a_ref[...], b_ref[...]
            if cast is not None:
                av, bv = av.astype(cast), bv.astype(cast)
            return lax.dot_general(av, bv, dims, preferred_element_type=F32, precision=precision)

        def finish(r):
            vals = (r,) if epi is None else epi(r, *[e[...] for e in ex], *[v[...] for v in rv])
            for o, val in zip(outs, vals):
                o[...] = val.astype(o.dtype)

        if nk == 1:
            finish(product())
            return
        acc = rest[ne + nr + no]
        k = pl.program_id(2)

        @pl.when(k == 0)
        def _():
            acc[...] = product()

        @pl.when((k > 0) & (k < nk - 1))
        def _():
            acc[...] += product()

        @pl.when(k == nk - 1)
        def _():
            finish(acc[...] + product())

    res = _pcall(
        body, name=name, grid=grid,
        in_specs=[a_spec, b_spec] + [e_spec] * ne + [r_spec] * nr,
        out_specs=[o_spec] * no,
        out_shape=[jax.ShapeDtypeStruct(o_shape, dt) for dt in out_dtypes],
        scratch_shapes=[] if nk == 1 else [pltpu.VMEM((tm, tn), F32)],
        compiler_params=_params(("parallel", "parallel", "arbitrary")),
    )(a, b, *extras, *rowvecs)
    return res[0] if no == 1 else res


def rowwise(fn, tiled, consts=(), outs=(), reds=(), *, ts, name):
    specs = []
    arrs = []
    rows = None
    for t in tiled:
        a, w, cb = t if isinstance(t, tuple) else (t, None, 0)
        arrs.append(a)
        rows = a.shape[-2] if rows is None else rows
        if a.ndim == 2:
            specs.append(pl.BlockSpec((ts, a.shape[1] if w is None else w), lambda i, cb=cb: (i, cb)))
        else:
            specs.append(pl.BlockSpec((a.shape[0], ts, a.shape[2]), lambda i: (0, i, 0)))
    for a in consts:
        specs.append(pl.BlockSpec(a.shape, lambda i, n=a.ndim: (0,) * n))
    o_specs, o_shapes = [], []
    for shp, dt in outs:
        if len(shp) == 2:
            o_specs.append(pl.BlockSpec((ts, shp[1]), lambda i: (i, 0)))
        else:
            o_specs.append(pl.BlockSpec((shp[0], ts, shp[2]), lambda i: (0, i, 0)))
        o_shapes.append(jax.ShapeDtypeStruct(shp, dt))
    for shp in reds:
        o_specs.append(pl.BlockSpec(shp, lambda i, n=len(shp): (0,) * n))
        o_shapes.append(jax.ShapeDtypeStruct(shp, F32))
    nt, nc, no = len(arrs), len(consts), len(outs)

    def body(*refs):
        i = pl.program_id(0)
        red_refs = refs[nt + nc + no:]
        vals = fn(refs[:nt], refs[nt:nt + nc], refs[nt + nc:nt + nc + no])
        if red_refs:
            @pl.when(i == 0)
            def _():
                for r in red_refs:
                    r[...] = jnp.zeros_like(r)
            for r, v in zip(red_refs, vals):
                r[...] += v

    return _pcall(body, name=name, grid=(rows // ts,), in_specs=specs, out_specs=o_specs, out_shape=o_shapes,
                  compiler_params=_params(("arbitrary",)))(*arrs, *consts)


def cast_bf16(w, l, name):
    _, R, C = w.shape
    tr = _tile(R, 512, 16)

    def body(w_ref, o_ref):
        o_ref[...] = w_ref[...].astype(BF16)

    return _pcall(body, name=name, grid=(R // tr,), in_specs=[pl.BlockSpec((None, tr, C), lambda i: (l, i, 0))],
                  out_specs=pl.BlockSpec((tr, C), lambda i: (i, 0)), out_shape=jax.ShapeDtypeStruct((R, C), BF16),
                  compiler_params=_params(("parallel",)))(w)


def prenorm(x, g, scale, shift, name):
    S, D = x.shape

    def fn(t, c, o):
        xv = t[0][...]
        r = _rstd(xv)
        o[0][...] = ((xv * r * c[0][...]) * (1.0 + c[1][...]) + c[2][...]).astype(BF16)
        o[1][...] = r
        return []

    return rowwise(fn, [x], [g, scale, shift], [((S, D), BF16), ((S, 1), F32)], ts=_tile(S, 128, 16), name=name)


def gate_bwd(dx, y, gate, name):
    S, D = dx.shape

    def fn(t, c, o):
        d = t[0][...]
        o[0][...] = (d * c[0][...]).astype(BF16)
        return [_colsum(d * t[1][...].astype(F32))]

    return rowwise(fn, [dx, y], [gate], [((S, D), BF16)], [(1, D)], ts=_tile(S, 128, 16), name=name)


def norm_bwd(dh, x, rstd, dres, g, scale, name):
    S, D = x.shape

    def fn(t, c, o):
        d = t[0][...]
        r = t[2][...]
        xh = t[1][...] * r
        gv = c[0][...]
        dr = d * (1.0 + c[1][...])
        dx, dg = _rms_bwd(dr, xh, r, gv)
        o[0][...] = t[3][...] + dx
        return [_colsum(d * (xh * gv)), _colsum(d), dg]

    return rowwise(fn, [dh, x, rstd, dres], [g, scale], [((S, D), F32)], [(1, D)] * 3, ts=_tile(S, 128, 8), name=name)


def loss_grad(y, tgt, name):
    S, D = y.shape

    def fn(t, c, o):
        e = t[0][...] - t[1][...]
        o[0][...] = e * (1.0 / D)
        return [_colsum(e * e)]

    return rowwise(fn, [y, tgt], outs=[((S, D), F32)], reds=[(1, D)], ts=_tile(S, 128, 8), name=name)


def _tril_mask():
    r = lax.broadcasted_iota(jnp.int32, (CHUNK, CHUNK), 0)
    c = lax.broadcasted_iota(jnp.int32, (CHUNK, CHUNK), 1)
    return c <= r


def sgu_fwd(proj, ng, w, bcol, DA, name):
    S = proj.shape[0]
    G = DA // GROUP
    tr = _tile(S, 2 * CHUNK)

    def body(u_ref, v_ref, ng_ref, w_ref, b_ref, o_ref):
        mask = _tril_mask()
        for g in range(G):
            cols = slice(g * GROUP, (g + 1) * GROUP)
            wm = jnp.where(mask, w_ref[g], 0.0).astype(BF16)
            for ci in range(tr // CHUNK):
                rows = slice(ci * CHUNK, (ci + 1) * CHUNK)
                gv = _gelu(v_ref[rows, cols])
                vn = gv * _rstd(gv) * ng_ref[:, cols]
                mixed = jnp.dot(wm, vn.astype(BF16), preferred_element_type=F32) + b_ref[g]
                o_ref[rows, cols] = (_gelu(u_ref[rows, cols]) * mixed).astype(o_ref.dtype)

    return _pcall(
        body, name=name, grid=(S // tr,),
        in_specs=[pl.BlockSpec((tr, DA), lambda i: (i, 0)), pl.BlockSpec((tr, DA), lambda i: (i, 1)),
                  pl.BlockSpec((1, DA), lambda i: (0, 0)), pl.BlockSpec((G, CHUNK, CHUNK), lambda i: (0, 0, 0)),
                  pl.BlockSpec((G, CHUNK, 1), lambda i: (0, 0, 0))],
        out_specs=pl.BlockSpec((tr, DA), lambda i: (i, 0)),
        out_shape=jax.ShapeDtypeStruct((S, DA), BF16),
        compiler_params=_params(("parallel",)),
    )(proj, proj, ng, w, bcol)


def sgu_bwd(proj, dcat, ng, w, bcol, DA, name):
    S = proj.shape[0]
    G = DA // GROUP
    tr = _tile(S, 2 * CHUNK)
    nsteps = S // tr

    def body(u_ref, v_ref, d_ref, ng_ref, w_ref, b_ref, duv_ref, dw_ref, db_ref, dng_ref, dbacc):
        i = pl.program_id(0)

        @pl.when(i == 0)
        def _():
            dw_ref[...] = jnp.zeros_like(dw_ref)
            dng_ref[...] = jnp.zeros_like(dng_ref)
            dbacc[...] = jnp.zeros_like(dbacc)

        mask = _tril_mask()
        for g in range(G):
            cols = slice(g * GROUP, (g + 1) * GROUP)
            wm = jnp.where(mask, w_ref[g], 0.0).astype(BF16)
            ngg = ng_ref[:, cols]
            for ci in range(tr // CHUNK):
                rows = slice(ci * CHUNK, (ci + 1) * CHUNK)
                u, v, d = u_ref[rows, cols], v_ref[rows, cols], d_ref[rows, cols]
                gv = _gelu(v)
                rs = _rstd(gv)
                vhat = gv * rs
                vn = (vhat * ngg).astype(BF16)
                mixed = jnp.dot(wm, vn, preferred_element_type=F32) + b_ref[g]
                dmixed = d * _gelu(u)
                dmb = dmixed.astype(BF16)
                duv_ref[rows, cols] = (d * mixed * _gelu_grad(u)).astype(duv_ref.dtype)
                dwg = lax.dot_general(dmb, vn, _DIMS["nt"], preferred_element_type=F32)
                dw_ref[g] += jnp.where(mask, dwg, 0.0)
                dbacc[g] += dmixed
                dvn = lax.dot_general(wm, dmb, _DIMS["tn"], preferred_element_type=F32)
                dgv, dngg = _rms_bwd(dvn, vhat, rs, ngg)
                dng_ref[:, cols] += dngg
                duv_ref[rows, DA + g * GROUP:DA + (g + 1) * GROUP] = (dgv * _gelu_grad(v)).astype(duv_ref.dtype)

        @pl.when(i == nsteps - 1)
        def _():
            for g in range(G):
                db_ref[g] = jnp.sum(dbacc[g], axis=-1, keepdims=True)

    return _pcall(
        body, name=name, grid=(nsteps,),
        in_specs=[pl.BlockSpec((tr, DA), lambda i: (i, 0)), pl.BlockSpec((tr, DA), lambda i: (i, 1)),
                  pl.BlockSpec((tr, DA), lambda i: (i, 0)),
                  pl.BlockSpec((1, DA), lambda i: (0, 0)), pl.BlockSpec((G, CHUNK, CHUNK), lambda i: (0, 0, 0)),
                  pl.BlockSpec((G, CHUNK, 1), lambda i: (0, 0, 0))],
        out_specs=[pl.BlockSpec((tr, 2 * DA), lambda i: (i, 0)), pl.BlockSpec((G, CHUNK, CHUNK), lambda i: (0, 0, 0)),
                   pl.BlockSpec((G, CHUNK, 1), lambda i: (0, 0, 0)), pl.BlockSpec((1, DA), lambda i: (0, 0))],
        out_shape=[jax.ShapeDtypeStruct((S, 2 * DA), BF16), jax.ShapeDtypeStruct((G, CHUNK, CHUNK), F32),
                   jax.ShapeDtypeStruct((G, CHUNK, 1), F32), jax.ShapeDtypeStruct((1, DA), F32)],
        scratch_shapes=[pltpu.VMEM((G, CHUNK, CHUNK), F32)],
        compiler_params=_params(("arbitrary",)),
    )(proj, proj, dcat, ng, w, bcol)


def _conv_tile(S):
    return _tile(S, 256, 8)


def conv_fwd(proj, wk, bias, DA, DB, name):
    S = proj.shape[0]
    nb = DB // LANES
    a0, g0 = 2 * DA // LANES, (2 * DA + DB) // LANES
    T = _conv_tile(S)
    off = CONV_PAD - (CONV_W - 1)

    def body(a_ref, g_ref, w_ref, b_ref, o_ref, ypad):
        ypad[0:CONV_PAD, :] = jnp.zeros((CONV_PAD, LANES), F32)

        def fill(t, cr):
            r = pl.multiple_of(t * T, T)
            ypad[pl.ds(CONV_PAD + r, T), :] = a_ref[pl.ds(r, T), :] * _sigmoid(g_ref[pl.ds(r, T), :])
            return cr

        lax.fori_loop(0, S // T, fill, 0)

        def step(t, cr):
            r = pl.multiple_of(t * T, T)
            acc = jnp.zeros((T, LANES), F32) + b_ref[...]
            for k in range(CONV_W):
                acc = acc + w_ref[k:k + 1, :] * ypad[pl.ds(r + (k + off), T), :]
            o_ref[pl.ds(r, T), :] = acc
            return cr

        lax.fori_loop(0, S // T, step, 0)

    return _pcall(
        body, name=name, grid=(nb,),
        in_specs=[pl.BlockSpec((S, LANES), lambda j: (0, a0 + j)), pl.BlockSpec((S, LANES), lambda j: (0, g0 + j)),
                  pl.BlockSpec((CONV_PAD, LANES), lambda j: (0, j)), pl.BlockSpec((1, LANES), lambda j: (0, j))],
        out_specs=pl.BlockSpec((S, LANES), lambda j: (0, j)),
        out_shape=jax.ShapeDtypeStruct((S, DB), F32),
        scratch_shapes=[pltpu.VMEM((S + CONV_PAD, LANES), F32)],
        compiler_params=_params(("parallel",)),
    )(proj, proj, wk, bias)


def conv_bwd(proj, dy1, wk, DA, DB, name):
    S = proj.shape[0]
    nb = DB // LANES
    a0, g0 = 2 * DA // LANES, (2 * DA + DB) // LANES
    T = _conv_tile(S)
    off = CONV_PAD - (CONV_W - 1)

    def body(a_ref, g_ref, d_ref, w_ref, da_ref, dg_ref, dw_ref, ypad, dpad, wacc):
        ypad[0:CONV_PAD, :] = jnp.zeros((CONV_PAD, LANES), F32)
        dpad[S:S + CONV_PAD, :] = jnp.zeros((CONV_PAD, LANES), F32)
        wacc[...] = jnp.zeros_like(wacc)

        def fill(t, cr):
            r = pl.multiple_of(t * T, T)
            ypad[pl.ds(CONV_PAD + r, T), :] = a_ref[pl.ds(r, T), :] * _sigmoid(g_ref[pl.ds(r, T), :])
            dpad[pl.ds(r, T), :] = d_ref[pl.ds(r, T), :]
            return cr

        lax.fori_loop(0, S // T, fill, 0)

        def step(t, cr):
            r = pl.multiple_of(t * T, T)
            dt = dpad[pl.ds(r, T), :]
            dy0 = jnp.zeros((T, LANES), F32)
            for k in range(CONV_W):
                prod = dt * ypad[pl.ds(r + (k + off), T), :]
                wacc[k] += jnp.sum(prod.reshape(T // 8, 8, LANES), axis=0)
                dy0 = dy0 + w_ref[k:k + 1, :] * dpad[pl.ds(r + (CONV_W - 1 - k), T), :]
            av, gv = a_ref[pl.ds(r, T), :], g_ref[pl.ds(r, T), :]
            sg = _sigmoid(gv)
            da_ref[pl.ds(r, T), :] = (dy0 * sg).astype(da_ref.dtype)
            dg_ref[pl.ds(r, T), :] = (dy0 * av * sg * (1.0 - sg)).astype(dg_ref.dtype)
            return cr

        lax.fori_loop(0, S // T, step, 0)
        for k in range(CONV_W):
            dw_ref[k:k + 1, :] = jnp.sum(wacc[k], axis=0, keepdims=True)
        dw_ref[CONV_W:CONV_PAD, :] = jnp.zeros((CONV_PAD - CONV_W, LANES), F32)

    return _pcall(
        body, name=name, grid=(nb,),
        in_specs=[pl.BlockSpec((S, LANES), lambda j: (0, a0 + j)), pl.BlockSpec((S, LANES), lambda j: (0, g0 + j)),
                  pl.BlockSpec((S, LANES), lambda j: (0, j)), pl.BlockSpec((CONV_PAD, LANES), lambda j: (0, j))],
        out_specs=[pl.BlockSpec((S, LANES), lambda j: (0, j)), pl.BlockSpec((S, LANES), lambda j: (0, j)),
                   pl.BlockSpec((CONV_PAD, LANES), lambda j: (0, j))],
        out_shape=[jax.ShapeDtypeStruct((S, DB), BF16), jax.ShapeDtypeStruct((S, DB), BF16),
                   jax.ShapeDtypeStruct((CONV_PAD, DB), F32)],
        scratch_shapes=[pltpu.VMEM((S + CONV_PAD, LANES), F32), pltpu.VMEM((S + CONV_PAD, LANES), F32),
                        pltpu.VMEM((CONV_PAD, 8, LANES), F32)],
        compiler_params=_params(("parallel",)),
    )(proj, proj, dy1, wk)


def _ln_stats(y):
    mu = jnp.mean(y, axis=-1, keepdims=True)
    yc = y - mu
    rs = lax.rsqrt(jnp.mean(yc * yc, axis=-1, keepdims=True) + EPS)
    return yc * rs, rs


def ln_silu(y1, lg, lb, name):
    S, DB = y1.shape

    def fn(t, c, o):
        yh, _ = _ln_stats(t[0][...])
        ln = yh * c[0][...] + c[1][...]
        o[0][...] = (ln * _sigmoid(ln)).astype(BF16)
        return []

    return rowwise(fn, [y1], [lg, lb], [((S, DB), BF16)], ts=_tile(S, 128, 16), name=name)[0]


def ln_silu_bwd(dcat, y1, lg, lb, name):
    S, DB = y1.shape
    cb = (dcat.shape[1] - DB) // DB

    def fn(t, c, o):
        yh, rs = _ln_stats(t[1][...])
        gv = c[0][...]
        ln = yh * gv + c[1][...]
        sg = _sigmoid(ln)
        dln = t[0][...] * (sg * (1.0 + ln * (1.0 - sg)))
        dyh = dln * gv
        dy = rs * (dyh - jnp.mean(dyh, axis=-1, keepdims=True) - yh * jnp.mean(dyh * yh, axis=-1, keepdims=True))
        o[0][...] = dy
        return [_colsum(dln * yh), _colsum(dln), _colsum(dy)]

    return rowwise(fn, [(dcat, DB, cb), y1], [lg, lb], [((S, DB), F32)], [(1, DB)] * 3, ts=_tile(S, 128, 8), name=name)


def mla_norms(proj, cos2, sinm, qg, kvg, kgr, R, name):
    S = proj.shape[0]

    def fn(t, c, o):
        cq = t[0][:, 0:R]
        ckv = t[0][:, R:2 * R]
        kr = t[0][:, 2 * R:2 * R + ROPE]
        o[0][...] = (cq * _rstd(cq) * c[0][...]).astype(BF16)
        o[1][...] = (ckv * _rstd(ckv) * c[1][...]).astype(BF16)
        o[2][...] = _rope(kr * _rstd(kr) * c[2][...], t[1][...], t[2][...])
        return []

    return rowwise(fn, [proj, cos2, sinm], [qg, kvg, kgr], [((S, R), BF16), ((S, R), BF16), ((S, ROPE), F32)],
                   ts=_tile(S, 128, 16), name=name)


def mla_heads(q, kv, kr, cos2, sinm, qg, kg, H, name):
    S = q.shape[0]

    def fn(t, c, o):
        cs, sn = t[3][...], t[4][...]
        krv = t[2][...]
        qgn, qgr, kgn = c[0][:, 0:NOPE], c[0][:, NOPE:QK], c[1][:, 0:NOPE]
        for h in range(H):
            qn = t[0][:, QK * h:QK * h + NOPE]
            qr = t[0][:, QK * h + NOPE:QK * (h + 1)]
            o[0][h, :, 0:NOPE] = (qn * _rstd(qn) * qgn).astype(BF16)
            o[0][h, :, NOPE:QK] = _rope(qr * _rstd(qr) * qgr, cs, sn).astype(BF16)
            kn = t[1][:, (NOPE + VDIM) * h:(NOPE + VDIM) * h + NOPE]
            o[1][h, :, 0:NOPE] = (kn * _rstd(kn) * kgn).astype(BF16)
            o[1][h, :, NOPE:QK] = krv.astype(BF16)
            o[2][h] = t[1][:, (NOPE + VDIM) * h + NOPE:(NOPE + VDIM) * (h + 1)].astype(BF16)
        return []

    return rowwise(fn, [q, kv, kr, cos2, sinm], [qg, kg],
                   [((H, S, QK), BF16), ((H, S, QK), BF16), ((H, S, VDIM), BF16)], ts=_tile(S, 128, 16), name=name)


def mla_heads_bwd(dQ, dK, dV, q, kv, cos2, sinm, qg, kg, H, name):
    S = q.shape[0]
    KV = NOPE + VDIM

    def fn(t, c, o):
        cs, sn = t[5][...], t[6][...]
        qgn, qgr, kgn = c[0][:, 0:NOPE], c[0][:, NOPE:QK], c[1][:, 0:NOPE]
        a_qn = jnp.zeros((1, NOPE), F32)
        a_qr = jnp.zeros((1, ROPE), F32)
        a_kn = jnp.zeros((1, NOPE), F32)
        dkr = jnp.zeros((t[0].shape[1], ROPE), F32)
        for h in range(H):
            qn = t[3][:, QK * h:QK * h + NOPE]
            rs = _rstd(qn)
            dx, dg = _rms_bwd(t[0][h, :, 0:NOPE], qn * rs, rs, qgn)
            o[0][:, QK * h:QK * h + NOPE] = dx.astype(BF16)
            a_qn = a_qn + dg
            qr = t[3][:, QK * h + NOPE:QK * (h + 1)]
            rs = _rstd(qr)
            dx, dg = _rms_bwd(_unrope(t[0][h, :, NOPE:QK], cs, sn), qr * rs, rs, qgr)
            o[0][:, QK * h + NOPE:QK * (h + 1)] = dx.astype(BF16)
            a_qr = a_qr + dg
            kn = t[4][:, KV * h:KV * h + NOPE]
            rs = _rstd(kn)
            dx, dg = _rms_bwd(t[1][h, :, 0:NOPE], kn * rs, rs, kgn)
            o[1][:, KV * h:KV * h + NOPE] = dx.astype(BF16)
            a_kn = a_kn + dg
            o[1][:, KV * h + NOPE:KV * (h + 1)] = t[2][h].astype(BF16)
            dkr = dkr + t[1][h, :, NOPE:QK]
        o[2][...] = _unrope(dkr, cs, sn)
        return [a_qn, a_qr, a_kn]

    return rowwise(fn, [dQ, dK, dV, q, kv, cos2, sinm], [qg, kg],
                   [((S, H * QK), BF16), ((S, H * KV), BF16), ((S, ROPE), F32)],
                   [(1, NOPE), (1, ROPE), (1, NOPE)], ts=_tile(S, 128, 16), name=name)


def mla_norms_bwd(dqn, dkvn, dkr, proj, qg, kvg, kgr, R, name):
    S = proj.shape[0]

    def fn(t, c, o):
        reds = []
        for idx, (lo, hi) in enumerate(((0, R), (R, 2 * R), (2 * R, 2 * R + ROPE))):
            xv = t[3][:, lo:hi]
            rs = _rstd(xv)
            dx, dg = _rms_bwd(t[idx][...], xv * rs, rs, c[idx][...])
            o[0][:, lo:hi] = dx.astype(BF16)
            reds.append(dg)
        return reds

    return rowwise(fn, [dqn, dkvn, dkr, proj], [qg, kvg, kgr], [((S, 2 * R + ROPE), BF16)],
                   [(1, R), (1, R), (1, ROPE)], ts=_tile(S, 128, 16), name=name)


def _tri_rows(p, n):
    qi = 0
    for j in range(1, n):
        qi = qi + (p >= j * (j + 1) // 2).astype(jnp.int32)
    return qi, p - (qi * (qi + 1)) // 2


def _tri_cols(p, n):
    ki = 0
    for j in range(1, n):
        ki = ki + (p >= j * n - j * (j - 1) // 2).astype(jnp.int32)
    return ki, ki + p - (ki * n - (ki * (ki - 1)) // 2)


def attn_fwd(Q, K, V, name):
    H, S, _ = Q.shape
    t = _tile(S, ATTN_BLOCK)
    n = S // t
    scale = QK ** -0.5

    rs = _tile(t, ATTN_STRIP, 8)

    def body(q_ref, k_ref, v_ref, o_ref, lse_ref, m_s, l_s, acc, s_scr, p_scr):
        qi, ki = _tri_rows(pl.program_id(1), n)

        @pl.when(ki == 0)
        def _():
            m_s[...] = jnp.full_like(m_s, NEG)
            l_s[...] = jnp.zeros_like(l_s)
            acc[...] = jnp.zeros_like(acc)

        def block(diagonal):
            s_scr[...] = lax.dot_general(q_ref[...], k_ref[...], _DIMS["nt"], preferred_element_type=F32)

            def strip(i, cr):
                r = slice(i * rs, (i + 1) * rs)
                s = s_scr[r, :] * scale
                if diagonal:
                    row = i * rs + lax.broadcasted_iota(jnp.int32, (rs, t), 0)
                    s = jnp.where(lax.broadcasted_iota(jnp.int32, (rs, t), 1) <= row, s, NEG)
                m_old = m_s[r, :]
                m_new = jnp.maximum(m_old, jnp.max(s, axis=-1, keepdims=True))
                alpha = jnp.exp(m_old - m_new)
                p = jnp.exp(s - m_new)
                l_s[r, :] = alpha * l_s[r, :] + jnp.sum(p, axis=-1, keepdims=True)
                m_s[r, :] = m_new
                acc[r, :] = alpha * acc[r, :]
                p_scr[r, :] = p.astype(BF16)
                return cr

            for i in range(t // rs):
                strip(i, 0)
            acc[...] += jnp.dot(p_scr[...], v_ref[...], preferred_element_type=F32)

        @pl.when(ki < qi)
        def _():
            block(False)

        @pl.when(ki == qi)
        def _():
            block(True)

        @pl.when(ki == qi)
        def _():
            o_ref[...] = (acc[...] / l_s[...]).astype(o_ref.dtype)
            lse_ref[...] = m_s[...] + jnp.log(l_s[...])

    return _pcall(
        body, name=name, grid=(H, n * (n + 1) // 2),
        in_specs=[pl.BlockSpec((None, t, QK), lambda h, p: (h, _tri_rows(p, n)[0], 0)),
                  pl.BlockSpec((None, t, QK), lambda h, p: (h, _tri_rows(p, n)[1], 0)),
                  pl.BlockSpec((None, t, VDIM), lambda h, p: (h, _tri_rows(p, n)[1], 0))],
        out_specs=[pl.BlockSpec((t, VDIM), lambda h, p: (_tri_rows(p, n)[0], h)),
                   pl.BlockSpec((None, t, 1), lambda h, p: (h, _tri_rows(p, n)[0], 0))],
        out_shape=[jax.ShapeDtypeStruct((S, H * VDIM), BF16), jax.ShapeDtypeStruct((H, S, 1), F32)],
        scratch_shapes=[pltpu.VMEM((t, 1), F32), pltpu.VMEM((t, 1), F32), pltpu.VMEM((t, VDIM), F32),
                        pltpu.VMEM((t, t), F32), pltpu.VMEM((t, t), BF16)],
        compiler_params=_params(("parallel", "arbitrary")),
    )(Q, K, V)


def attn_bwd(Q, K, V, dO, O, lse, name):
    H, S, _ = Q.shape
    t = _tile(S, ATTN_BLOCK)
    n = S // t
    scale = QK ** -0.5

    rs = _tile(t, ATTN_STRIP, 8)

    def body(q_ref, k_ref, v_ref, do_ref, o_ref, lse_ref, dq_ref, dk_ref, dv_ref, s_scr, dp_scr, p_scr, ds_scr):
        ki, qi = _tri_cols(pl.program_id(1), n)

        @pl.when(pl.program_id(1) == 0)
        def _():
            dq_ref[...] = jnp.zeros_like(dq_ref)

        @pl.when(qi == ki)
        def _():
            dk_ref[...] = jnp.zeros_like(dk_ref)
            dv_ref[...] = jnp.zeros_like(dv_ref)

        def block(diagonal):
            s_scr[...] = lax.dot_general(q_ref[...], k_ref[...], _DIMS["nt"], preferred_element_type=F32)
            dp_scr[...] = lax.dot_general(do_ref[...], v_ref[...], _DIMS["nt"], preferred_element_type=F32)

            def strip(i, cr):
                r = slice(i * rs, (i + 1) * rs)
                s = s_scr[r, :] * scale
                if diagonal:
                    row = i * rs + lax.broadcasted_iota(jnp.int32, (rs, t), 0)
                    s = jnp.where(lax.broadcasted_iota(jnp.int32, (rs, t), 1) <= row, s, NEG)
                p = jnp.exp(s - lse_ref[r, :])
                delta = jnp.sum(do_ref[r, :].astype(F32) * o_ref[r, :].astype(F32), axis=-1, keepdims=True)
                p_scr[r, :] = p.astype(BF16)
                ds_scr[r, :] = (p * (dp_scr[r, :] - delta) * scale).astype(BF16)
                return cr

            for i in range(t // rs):
                strip(i, 0)
            ds = ds_scr[...]
            dv_ref[...] += lax.dot_general(p_scr[...], do_ref[...], _DIMS["tn"], preferred_element_type=F32)
            dk_ref[...] += lax.dot_general(ds, q_ref[...], _DIMS["tn"], preferred_element_type=F32)
            rq = pl.multiple_of(qi * t, t)
            dq_ref[pl.ds(rq, t), :] += jnp.dot(ds, k_ref[...], preferred_element_type=F32)

        @pl.when(qi > ki)
        def _():
            block(False)

        @pl.when(qi == ki)
        def _():
            block(True)

    qmap = lambda h, p: (h, _tri_cols(p, n)[1], 0)
    kmap = lambda h, p: (h, _tri_cols(p, n)[0], 0)
    return _pcall(
        body, name=name, grid=(H, n * (n + 1) // 2),
        in_specs=[pl.BlockSpec((None, t, QK), qmap),
                  pl.BlockSpec((None, t, QK), kmap),
                  pl.BlockSpec((None, t, VDIM), kmap),
                  pl.BlockSpec((t, VDIM), lambda h, p: (_tri_cols(p, n)[1], h)),
                  pl.BlockSpec((t, VDIM), lambda h, p: (_tri_cols(p, n)[1], h)),
                  pl.BlockSpec((None, t, 1), qmap)],
        out_specs=[pl.BlockSpec((None, S, QK), lambda h, p: (h, 0, 0)),
                   pl.BlockSpec((None, t, QK), kmap),
                   pl.BlockSpec((None, t, VDIM), kmap)],
        out_shape=[jax.ShapeDtypeStruct((H, S, QK), F32), jax.ShapeDtypeStruct((H, S, QK), F32),
                   jax.ShapeDtypeStruct((H, S, VDIM), F32)],
        scratch_shapes=[pltpu.VMEM((t, t), F32), pltpu.VMEM((t, t), F32), pltpu.VMEM((t, t), BF16),
                        pltpu.VMEM((t, t), BF16)],
        compiler_params=_params(("parallel", "arbitrary")),
    )(Q, K, V, dO, O, lse)


def adamw(w, m, v, g, *, row0, name):
    P, rows, C = g.shape
    tr = _tile(rows, max(16, 131072 // C), 16)
    off = row0 // tr
    assert row0 % tr == 0
    bc1 = 1.0 - ADAM_B1 ** ADAM_STEP
    bc2 = 1.0 - ADAM_B2 ** ADAM_STEP

    def body(w_ref, m_ref, v_ref, g_ref, go_ref, d_ref, mo_ref, vo_ref):
        gs = g_ref[0].astype(F32)
        for p in range(1, P):
            gs = gs + g_ref[p].astype(F32)
        wv = w_ref[...]
        mn = ADAM_B1 * m_ref[...] + (1.0 - ADAM_B1) * gs
        vn = ADAM_B2 * v_ref[...] + (1.0 - ADAM_B2) * (gs * gs)
        go_ref[...] = gs
        mo_ref[...] = mn
        vo_ref[...] = vn
        d_ref[...] = -ADAM_LR * ((mn / bc1) / (jnp.sqrt(vn / bc2) + ADAM_EPS) + ADAM_WD * wv)

    wspec = pl.BlockSpec((tr, C), lambda i: (i + off, 0))
    ospec = pl.BlockSpec((tr, C), lambda i: (i, 0))
    return _pcall(
        body, name=name, grid=(rows // tr,),
        in_specs=[wspec, wspec, wspec, pl.BlockSpec((P, tr, C), lambda i: (0, i, 0))],
        out_specs=[ospec] * 4, out_shape=[jax.ShapeDtypeStruct((rows, C), F32)] * 4,
        compiler_params=_params(("parallel",)),
    )(w, m, v, g)


def _coords():
    return lax.axis_index("x"), lax.axis_index("y"), lax.axis_index("c")


def _me():
    x, y, c = _coords()
    return 4 * x + 2 * y + c


_ANY = pl.BlockSpec(memory_space=pl.ANY)


def all_gather(items, name):
    n = len(items)
    blks = [a.shape if idx is None else a.shape[1:] for a, idx in items]

    def body(*refs):
        ins, outs = refs[:n], refs[n:2 * n]
        send, recv, lsem = refs[2 * n:]
        x, y, c = _coords()
        me, sib = (x, y, c), (x, y, 1 - c)
        chips = [(1 - x, y), (x, 1 - y), (1 - x, 1 - y)]

        def src(i):
            return ins[i] if items[i][1] is None else ins[i].at[items[i][1]]

        def slot(i, p):
            return outs[i].at[4 * p[0] + 2 * p[1] + p[2]]

        def cp(i, k, block, to, s=None):
            return pltpu.make_async_remote_copy(
                src_ref=slot(i, block) if s is None else s, dst_ref=slot(i, block),
                send_sem=send.at[7 * i + k], recv_sem=recv.at[7 * i + k], device_id=to, device_id_type=MESH)

        mine = [pltpu.make_async_copy(src(i), slot(i, me), lsem.at[i]) for i in range(n)]
        for m_ in mine:
            m_.start()
        first = []
        for i in range(n):
            first.append(cp(i, 0, me, sib, src(i)))
            first += [cp(i, 1 + j, me, (*chip, c), src(i)) for j, chip in enumerate(chips)]
        for f in first:
            f.start()
        passed = []
        for j, chip in enumerate(chips):
            for i in range(n):
                cp(i, 1 + j, (*chip, c), me).wait_recv()
                p_ = cp(i, 4 + j, (*chip, c), sib)
                p_.start()
                passed.append(p_)
        for i in range(n):
            cp(i, 0, sib, me).wait_recv()
            for j, chip in enumerate(chips):
                cp(i, 4 + j, (*chip, 1 - c), me).wait_recv()
        for f in first + passed:
            f.wait_send()
        for m_ in mine:
            m_.wait()

    res = _pcall(
        body, name=name, in_specs=[_ANY] * n, out_specs=[_ANY] * n,
        out_shape=[jax.ShapeDtypeStruct((NDEV,) + tuple(b), a.dtype) for b, (a, _) in zip(blks, items)],
        scratch_shapes=[pltpu.SemaphoreType.DMA((7 * n,)), pltpu.SemaphoreType.DMA((7 * n,)),
                        pltpu.SemaphoreType.DMA((n,))],
    )(*[a for a, _ in items])
    return list(res)


_HBM = pl.BlockSpec(memory_space=pltpu.HBM)
_SEM = pl.BlockSpec(memory_space=pltpu.SEMAPHORE)
_EFFECT = pltpu.SideEffectType.DATAFLOW_SIDE_EFFECTING


def _xchg_copy(src_ref, land_ref, send, recv, r, scatter, at_peer):
    x, y, c = _coords()
    px = jnp.bitwise_xor(x, (r >> 2) & 1)
    py = jnp.bitwise_xor(y, (r >> 1) & 1)
    pc = jnp.bitwise_xor(c, r & 1)
    p_i = 4 * px + 2 * py + pc
    me_i = 4 * x + 2 * y + c
    return pltpu.make_async_remote_copy(
        src_ref=src_ref.at[p_i] if scatter else src_ref, dst_ref=land_ref.at[p_i if at_peer else me_i],
        send_sem=send.at[r - 1], recv_sem=recv.at[r - 1], device_id=(px, py, pc), device_id_type=MESH)


def xchg_start(srcs, scatter, name, after=None):
    n = len(srcs)
    me = _me()
    lands = []
    for s in srcs:
        blk = s.shape[1:] if scatter else s.shape
        own = lax.dynamic_index_in_dim(s, me, 0, keepdims=True) if scatter else s[None]
        lands.append(lax.dynamic_update_index_in_dim(lax.empty((NDEV,) + tuple(blk), s.dtype), own, me, 0))

    def body(*refs):
        s_in, l_in = refs[:n], refs[n:2 * n]
        outs = refs[2 * n + (after is not None):]
        sends, recvs = outs[:n], outs[n:2 * n]
        token = outs[4 * n]
        for i in range(n):
            for r in range(1, NDEV):
                _xchg_copy(s_in[i], l_in[i], sends[i], recvs[i], r, scatter, False).start()
        token[...] = jnp.zeros_like(token)

    hbm = lambda a: pltpu.HBM(a.shape, a.dtype)
    res = _pcall(
        body, name=name,
        out_shape=tuple([pltpu.SemaphoreType.DMA((NDEV - 1,))] * (2 * n) + [hbm(a) for a in srcs] + [hbm(a) for a in lands]
                        + [jax.ShapeDtypeStruct((8, LANES), F32)]),
        in_specs=[_HBM] * (2 * n) + ([] if after is None else [_ANY]),
        out_specs=tuple([_SEM] * (2 * n) + [_HBM] * (2 * n) + [pl.BlockSpec(memory_space=pltpu.VMEM)]),
        input_output_aliases={i: 2 * n + i for i in range(2 * n)},
        compiler_params=pltpu.CompilerParams(has_side_effects=_EFFECT),
    )(*[pltpu.with_memory_space_constraint(a, pltpu.HBM) for a in list(srcs) + lands], *([] if after is None else [after]))
    handles = [(res[i], res[n + i], res[2 * n + i], res[3 * n + i]) for i in range(n)]
    return handles, res[4 * n]


def xchg_wait(handles, after, scatter, name):
    n = len(handles)

    def body(*refs):
        s_in, l_in = refs[:n], refs[n:2 * n]
        sends, recvs = refs[2 * n:3 * n], refs[3 * n:4 * n]
        for i in range(n):
            for r in range(1, NDEV):
                cp = _xchg_copy(s_in[i], l_in[i], sends[i], recvs[i], r, scatter, True)
                cp.wait_send()
                cp.wait_recv()

    srcs = [h[2] for h in handles]
    lands = [h[3] for h in handles]
    hbm = lambda a: pltpu.HBM(a.shape, a.dtype)
    res = _pcall(
        body, name=name,
        out_shape=tuple([hbm(a) for a in srcs] + [hbm(a) for a in lands]),
        in_specs=[_HBM] * (2 * n) + [_SEM] * (2 * n) + [_ANY],
        out_specs=tuple([_HBM] * (2 * n)),
        input_output_aliases={i: i for i in range(2 * n)},
        compiler_params=pltpu.CompilerParams(has_side_effects=_EFFECT),
    )(*srcs, *lands, *[h[0] for h in handles], *[h[1] for h in handles], after)
    return list(res[n:])


_PACK_ALIGN = 8 * LANES


def _pack(arrs, aligned=False):
    parts = []
    for a in arrs:
        f = a.reshape(-1).astype(F32)
        pad = (-f.shape[0]) % _PACK_ALIGN if aligned else 0
        parts.append(jnp.pad(f, (0, pad)) if pad else f)
    flat = jnp.concatenate(parts)
    pad = (-flat.shape[0]) % _PACK_ALIGN
    return (jnp.pad(flat, (0, pad)) if pad else flat).reshape(-1, LANES)


def _unpack(p, shapes, lead=(), aligned=False):
    nl = len(lead)
    flat = p.reshape(lead + (-1,))
    out, off = [], 0
    for shp in shapes:
        n = 1
        for d in shp:
            n *= d
        out.append(lax.slice_in_dim(flat, off, off + n, axis=nl).reshape(lead + tuple(shp)))
        off += n + ((-n) % _PACK_ALIGN if aligned else 0)
    return out


def _shard_cols(a, me, width, axis):
    return lax.dynamic_slice_in_dim(a, me * width, width, axis=axis)


def kernel(x, c, norm1_g, norm2_g, ada_w, ada_b, mlp_w1, mlp_w2, ab_w_in, sgu_norm_g, sgu_w, sgu_b, conv_w, conv_b, conv_ln_g, conv_ln_b, ab_w_out, mla_w_in, mla_q_norm_g, mla_kv_norm_g, mla_w_uq, mla_w_ukv, mla_q_head_g, mla_k_head_g, mla_w_out, loss_target, m_norm1_g, m_norm2_g, m_ada_w, m_ada_b, m_mlp_w1, m_mlp_w2, m_ab_w_in, m_sgu_norm_g, m_sgu_w, m_sgu_b, m_conv_w, m_conv_b, m_conv_ln_g, m_conv_ln_b, m_ab_w_out, m_mla_w_in, m_mla_q_norm_g, m_mla_kv_norm_g, m_mla_w_uq, m_mla_w_ukv, m_mla_q_head_g, m_mla_k_head_g, m_mla_w_out, v_norm1_g, v_norm2_g, v_ada_w, v_ada_b, v_mlp_w1, v_mlp_w2, v_ab_w_in, v_sgu_norm_g, v_sgu_w, v_sgu_b, v_conv_w, v_conv_b, v_conv_ln_g, v_conv_ln_b, v_ab_w_out, v_mla_w_in, v_mla_q_norm_g, v_mla_kv_norm_g, v_mla_w_uq, v_mla_w_ukv, v_mla_q_head_g, v_mla_k_head_g, v_mla_w_out):
    W = dict(norm1_g=norm1_g, norm2_g=norm2_g, ada_w=ada_w, ada_b=ada_b, mlp_w1=mlp_w1, mlp_w2=mlp_w2, ab_w_in=ab_w_in,
             sgu_norm_g=sgu_norm_g, sgu_w=sgu_w, sgu_b=sgu_b, conv_w=conv_w, conv_b=conv_b, conv_ln_g=conv_ln_g,
             conv_ln_b=conv_ln_b, ab_w_out=ab_w_out, mla_w_in=mla_w_in, mla_q_norm_g=mla_q_norm_g,
             mla_kv_norm_g=mla_kv_norm_g, mla_w_uq=mla_w_uq, mla_w_ukv=mla_w_ukv, mla_q_head_g=mla_q_head_g,
             mla_k_head_g=mla_k_head_g, mla_w_out=mla_w_out)
    M = dict(norm1_g=m_norm1_g, norm2_g=m_norm2_g, ada_w=m_ada_w, ada_b=m_ada_b, mlp_w1=m_mlp_w1, mlp_w2=m_mlp_w2,
             ab_w_in=m_ab_w_in, sgu_norm_g=m_sgu_norm_g, sgu_w=m_sgu_w, sgu_b=m_sgu_b, conv_w=m_conv_w, conv_b=m_conv_b,
             conv_ln_g=m_conv_ln_g, conv_ln_b=m_conv_ln_b, ab_w_out=m_ab_w_out, mla_w_in=m_mla_w_in,
             mla_q_norm_g=m_mla_q_norm_g, mla_kv_norm_g=m_mla_kv_norm_g, mla_w_uq=m_mla_w_uq, mla_w_ukv=m_mla_w_ukv,
             mla_q_head_g=m_mla_q_head_g, mla_k_head_g=m_mla_k_head_g, mla_w_out=m_mla_w_out)
    V = dict(norm1_g=v_norm1_g, norm2_g=v_norm2_g, ada_w=v_ada_w, ada_b=v_ada_b, mlp_w1=v_mlp_w1, mlp_w2=v_mlp_w2,
             ab_w_in=v_ab_w_in, sgu_norm_g=v_sgu_norm_g, sgu_w=v_sgu_w, sgu_b=v_sgu_b, conv_w=v_conv_w, conv_b=v_conv_b,
             conv_ln_g=v_conv_ln_g, conv_ln_b=v_conv_ln_b, ab_w_out=v_ab_w_out, mla_w_in=v_mla_w_in,
             mla_q_norm_g=v_mla_q_norm_g, mla_kv_norm_g=v_mla_kv_norm_g, mla_w_uq=v_mla_w_uq, mla_w_ukv=v_mla_w_ukv,
             mla_q_head_g=v_mla_q_head_g, mla_k_head_g=v_mla_k_head_g, mla_w_out=v_mla_w_out)
    ORDER = list(W)

    S, D = x.shape[1], x.shape[2]
    L, NE, NO = norm1_g.shape[0], ab_w_in.shape[0], mla_w_in.shape[0]
    DA = D // 2
    DB = D - DA
    G = DA // GROUP
    R = NDEV * mla_q_norm_g.shape[1]
    H = NDEV * mla_w_uq.shape[2] // QK
    AW = ada_w.shape[2]
    CB = conv_w.shape[2]
    me = _me()
    xs, tgt = x[0], loss_target[0]

    BIG_EVEN = ("mlp_w1", "mlp_w2", "ab_w_in", "ab_w_out")
    BIG_ODD = ("mlp_w1", "mlp_w2", "mla_w_in", "mla_w_uq", "mla_w_ukv", "mla_w_out")
    BIG = ("mlp_w1", "mlp_w2", "ab_w_in", "ab_w_out", "mla_w_in", "mla_w_uq", "mla_w_ukv", "mla_w_out")
    COL_SHARDED = ("mlp_w1", "ab_w_in", "mla_w_uq", "mla_w_ukv")
    MLP_W = ("mlp_w1", "mlp_w2")
    mixer_w = lambda l: ("ab_w_in", "ab_w_out") if l % 2 == 0 else ("mla_w_in", "mla_w_uq", "mla_w_ukv", "mla_w_out")
    widx = lambda k, l: l if k in MLP_W else l // 2

    small_in = [c, mla_q_norm_g, mla_kv_norm_g, conv_w]
    sg = all_gather([(_pack(small_in, True), None)], "gather_small")[0]
    c_all, qng_all, kvng_all, cw_all = _unpack(sg, [a.shape for a in small_in], (NDEV,), True)
    c_all = c_all.reshape(NDEV, D)
    qng_full = jnp.transpose(qng_all, (1, 0, 2)).reshape(NO, 1, R)
    kvng_full = jnp.transpose(kvng_all, (1, 0, 2)).reshape(NO, 1, R)
    cw_full = jnp.transpose(cw_all, (1, 2, 0, 3)).reshape(NE, CONV_W, DB)
    cw_pad = jnp.pad(cw_full, ((0, 0), (0, CONV_PAD - CONV_W), (0, 0)))

    def silu_fn(t, c_, o):
        v_ = t[0][...]
        o[0][...] = v_ * _sigmoid(v_)
        return []

    c_act = rowwise(silu_fn, [c_all], outs=[((NDEV, D), F32)], ts=NDEV, name="silu_c")[0]
    bias_cols = _shard_cols(ada_b, me, AW, 1).reshape(1, L * AW)
    mod_cols = mm(c_act, ada_w, "nn", name="ada_fwd", b_blocked=True, rowvecs=[bias_cols],
                  epi=lambda acc, b_: (acc + b_,), tm=NDEV, tn=768)
    mod_all = all_gather([(mod_cols, None)], "gather_mod")[0]
    mod = lax.dynamic_index_in_dim(mod_all, me, axis=1, keepdims=False)
    mod = jnp.transpose(mod.reshape(NDEV, L, AW), (1, 0, 2)).reshape(L, 6, 1, D)

    g_handles = []
    tok_sum = jnp.zeros((1, 1), F32)
    for l in range(L):
        names = mixer_w(l) + MLP_W
        srcs = [cast_bf16(W[k], widx(k, l), "cast_%s_l%d" % (k, l)) for k in names]
        handles, tok = xchg_start(srcs, False, "gather_start_l%d" % l, after=mod_all)
        g_handles.append(dict(zip(names, handles)))
        tok_sum = tok_sum + tok[0:1, 0:1]
    mod = mod + tok_sum

    def wait_weights(l, names, after, what):
        lands = xchg_wait([g_handles[l][k] for k in names], after, False, "gather_wait_%s_l%d" % (what, l))
        return {k: (ld if k in COL_SHARDED else ld.reshape(NDEV * ld.shape[1], ld.shape[2])) for k, ld in zip(names, lands)}

    pos = jnp.arange(S, dtype=F32)
    inv = ROPE_THETA ** (-jnp.arange(0, ROPE, 2, dtype=F32) / ROPE)
    ang = pos[:, None] * inv[None, :]
    cos2 = jnp.concatenate([jnp.cos(ang), jnp.cos(ang)], axis=1)
    sinm = jnp.concatenate([-jnp.sin(ang), jnp.sin(ang)], axis=1)

    residual = lambda acc, xr, gt: (acc, xr + gt * acc)

    saved = []
    xc = xs
    for l in range(L):
        sh1, sc1, g1, sh2, sc2, g2 = [mod[l, k] for k in range(6)]
        tag = "_l%d" % l
        sv = dict(x0=xc)
        h, sv["rstd1"] = prenorm(xc, norm1_g[l][None], sc1, sh1, "prenorm1" + tag)
        sv["h"] = h
        wl = wait_weights(l, mixer_w(l), h, "mix")
        if l % 2 == 0:
            e = l // 2
            ng = sgu_norm_g[e].reshape(1, DA)
            bcol = sgu_b[e][:, :, None]
            proj = mm(h, wl["ab_w_in"], "nn", name="ab_in" + tag, b_blocked=True)
            out_a = sgu_fwd(proj, ng, sgu_w[e], bcol, DA, "sgu_fwd" + tag)
            y1 = conv_fwd(proj, cw_pad[e], conv_b[e][None], DA, DB, "conv_fwd" + tag)
            out_b = ln_silu(y1, conv_ln_g[e][None], conv_ln_b[e][None], "ln_silu" + tag)
            cat = jnp.concatenate([out_a, out_b], axis=1)
            sv.update(proj=proj, y1=y1, cat=cat)
            mixb, x1 = mm(cat, wl["ab_w_out"], "nn", name="ab_out" + tag, out_dtypes=(BF16, F32), epi=residual,
                          extras=[xc], rowvecs=[g1])
        else:
            o_ = l // 2
            proj = mm(h, wl["mla_w_in"], "nn", name="mla_in" + tag)
            kgr = mla_k_head_g[o_][None, NOPE:QK]
            qn, kvn, kr = mla_norms(proj, cos2, sinm, qng_full[o_], kvng_full[o_], kgr, R, "mla_norms" + tag)
            q = mm(qn, wl["mla_w_uq"], "nn", name="mla_uq" + tag, b_blocked=True)
            kv = mm(kvn, wl["mla_w_ukv"], "nn", name="mla_ukv" + tag, b_blocked=True)
            Qh, Kh, Vh = mla_heads(q, kv, kr, cos2, sinm, mla_q_head_g[o_][None], mla_k_head_g[o_][None], H,
                                   "mla_heads" + tag)
            att, lse = attn_fwd(Qh, Kh, Vh, "attn_fwd" + tag)
            sv.update(proj=proj, qn=qn, kvn=kvn, q=q, kv=kv, Qh=Qh, Kh=Kh, Vh=Vh, att=att, lse=lse)
            mixb, x1 = mm(att, wl["mla_w_out"], "nn", name="mla_out" + tag, out_dtypes=(BF16, F32), epi=residual,
                          extras=[xc], rowvecs=[g1])
        sv.update(mixb=mixb, x1=x1)
        h2, sv["rstd2"] = prenorm(x1, norm2_g[l][None], sc2, sh2, "prenorm2" + tag)
        wl.update(wait_weights(l, MLP_W, h2, "mlp"))
        sv["w"] = wl
        z, act = mm(h2, wl["mlp_w1"], "nn", name="mlp_up" + tag, b_blocked=True, out_dtypes=(BF16, BF16),
                    epi=lambda acc: (acc, jnp.square(jnp.maximum(acc, 0.0))))
        yb, xc = mm(act, wl["mlp_w2"], "nn", name="mlp_down" + tag, out_dtypes=(BF16, F32), epi=residual,
                    extras=[x1], rowvecs=[g2])
        sv.update(h2=h2, z=z, act=act, yb=yb)
        saved.append(sv)

    dx, loss_cols = loss_grad(xc, tgt, "loss")
    loss = lax.psum(0.5 / D * jnp.sum(loss_cols), ("x", "y", "c"))

    big_out = {k: [None] * W[k].shape[0] for k in BIG}
    sm = {k: [None] * W[k].shape[0] for k in ("norm1_g", "norm2_g", "sgu_norm_g", "sgu_w", "sgu_b", "conv_b", "conv_ln_g",
                                               "conv_ln_b", "mla_q_head_g", "mla_k_head_g", "mla_q_norm_g",
                                               "mla_kv_norm_g", "conv_w")}
    dmod = [None] * L
    flat2 = {k: W[k].reshape(-1, W[k].shape[2]) for k in BIG}
    flat2m = {k: M[k].reshape(-1, W[k].shape[2]) for k in BIG}
    flat2v = {k: V[k].reshape(-1, W[k].shape[2]) for k in BIG}

    def scatter_start(gr, what, tag, after=None):
        names = list(gr)
        blocks = [gr[k] if k in COL_SHARDED else gr[k].reshape(NDEV, gr[k].shape[0] // NDEV, gr[k].shape[1])
                  for k in names]
        handles, tok = xchg_start(blocks, True, "scatter_start_%s%s" % (what, tag), after=after)
        return (names, handles, what, tag), tok[0:1, 0:1]

    rep = ("norm1_g", "norm2_g", "sgu_norm_g", "sgu_w", "sgu_b", "conv_b", "conv_ln_g", "conv_ln_b", "mla_q_head_g",
           "mla_k_head_g")
    part_full = {"mla_q_norm_g": (NO, R), "mla_kv_norm_g": (NO, R), "conv_w": (NE, CONV_W, DB)}
    small = ["ada_b"] + list(rep) + list(part_full)
    shapes = [(L, 6 * D)] + [W[k].shape for k in rep] + list(part_full.values())

    def small_gather():
        parts = [jnp.stack(dmod).reshape(L, 6 * D)] + [jnp.stack(sm[k]).reshape(s_) for k, s_ in zip(small[1:], shapes[1:])]
        return all_gather([(_pack(parts), None)], "gather_smallgrads")[0]

    def scatter_finish(pending, after, l):
        names, handles, what, tag = pending
        landed = xchg_wait(handles, after, True, "scatter_wait_%s%s" % (what, tag))
        for k, land in zip(names, landed):
            li = widx(k, l)
            big_out[k][li] = adamw(flat2[k], flat2m[k], flat2v[k], land, row0=li * W[k].shape[1],
                                   name="adamw_%s%s" % (k, tag))

    pend_mix, tok_mix = None, None
    for l in reversed(range(L)):
        sh1, sc1, g1, sh2, sc2, g2 = [mod[l, k] for k in range(6)]
        if tok_mix is not None:
            g2 = g2 + tok_mix
        sv = saved[l]
        wl = sv["w"]
        tag = "_l%d" % l
        gr = {}
        dy, dgate2 = gate_bwd(dx, sv["yb"], g2, "gate2_bwd" + tag)
        dz = mm(dy, wl["mlp_w2"], "nt", name="mlp_down_dx" + tag, out_dtypes=(BF16,), extras=[sv["z"]],
                epi=lambda acc, z_: (acc * (2.0 * jnp.maximum(z_.astype(F32), 0.0)),))
        gr["mlp_w2"] = mm(sv["act"], dy, "tn", name="mlp_down_dw" + tag, out_dtypes=(BF16,))
        dh2 = mm(dz, wl["mlp_w1"], "nt", name="mlp_up_dx" + tag, b_blocked=True)
        gr["mlp_w1"] = mm(sv["h2"], dz, "tn", name="mlp_up_dw" + tag, out_dtypes=(BF16,), out_blocked=NDEV)
        dx1, dsc2, dsh2, sm["norm2_g"][l] = norm_bwd(dh2, sv["x1"], sv["rstd2"], dx, norm2_g[l][None], sc2,
                                                     "norm2_bwd" + tag)
        pend_mlp, tok_mlp = scatter_start(gr, "mlp", tag)
        gr = {}
        if pend_mix is not None:
            scatter_finish(pend_mix, dx1, l + 1)
        dmix, dgate1 = gate_bwd(dx1, sv["mixb"], g1 + tok_mlp, "gate1_bwd" + tag)
        if l % 2 == 0:
            e = l // 2
            ng = sgu_norm_g[e].reshape(1, DA)
            bcol = sgu_b[e][:, :, None]
            dcat = mm(dmix, wl["ab_w_out"], "nt", name="ab_out_dx" + tag)
            gr["ab_w_out"] = mm(sv["cat"], dmix, "tn", name="ab_out_dw" + tag, out_dtypes=(BF16,))
            dy1, sm["conv_ln_g"][e], sm["conv_ln_b"][e], sm["conv_b"][e] = ln_silu_bwd(
                dcat, sv["y1"], conv_ln_g[e][None], conv_ln_b[e][None], "ln_silu_bwd" + tag)
            da, dg_, dwc = conv_bwd(sv["proj"], dy1, cw_pad[e], DA, DB, "conv_bwd" + tag)
            sm["conv_w"][e] = dwc[:CONV_W]
            duv, dsw, dsb, dsng = sgu_bwd(sv["proj"], dcat, ng, sgu_w[e], bcol, DA, "sgu_bwd" + tag)
            sm["sgu_w"][e], sm["sgu_b"][e], sm["sgu_norm_g"][e] = dsw, dsb, dsng
            dproj = jnp.concatenate([duv, da, dg_], axis=1)
            dh = mm(dproj, wl["ab_w_in"], "nt", name="ab_in_dx" + tag, b_blocked=True)
            gr["ab_w_in"] = mm(sv["h"], dproj, "tn", name="ab_in_dw" + tag, out_dtypes=(BF16,), out_blocked=NDEV)
        else:
            o_ = l // 2
            kgr = mla_k_head_g[o_][None, NOPE:QK]
            dO = mm(dmix, wl["mla_w_out"], "nt", name="mla_out_dx" + tag, out_dtypes=(BF16,))
            gr["mla_w_out"] = mm(sv["att"], dmix, "tn", name="mla_out_dw" + tag, out_dtypes=(BF16,))
            dQ, dK, dV = attn_bwd(sv["Qh"], sv["Kh"], sv["Vh"], dO, sv["att"], sv["lse"], "attn_bwd" + tag)
            dq_pre, dkv_pre, dkr, dqgn, dqgr, dkgn = mla_heads_bwd(
                dQ, dK, dV, sv["q"], sv["kv"], cos2, sinm, mla_q_head_g[o_][None], mla_k_head_g[o_][None], H,
                "mla_heads_bwd" + tag)
            dqn = mm(dq_pre, wl["mla_w_uq"], "nt", name="mla_uq_dx" + tag, b_blocked=True)
            gr["mla_w_uq"] = mm(sv["qn"], dq_pre, "tn", name="mla_uq_dw" + tag, out_dtypes=(BF16,), out_blocked=NDEV)
            dkvn = mm(dkv_pre, wl["mla_w_ukv"], "nt", name="mla_ukv_dx" + tag, b_blocked=True)
            gr["mla_w_ukv"] = mm(sv["kvn"], dkv_pre, "tn", name="mla_ukv_dw" + tag, out_dtypes=(BF16,),
                                 out_blocked=NDEV)
            dproj, sm["mla_q_norm_g"][o_], sm["mla_kv_norm_g"][o_], dkgr = mla_norms_bwd(
                dqn, dkvn, dkr, sv["proj"], qng_full[o_], kvng_full[o_], kgr, R, "mla_norms_bwd" + tag)
            sm["mla_q_head_g"][o_] = jnp.concatenate([dqgn, dqgr], axis=1)
            sm["mla_k_head_g"][o_] = jnp.concatenate([dkgn, dkgr], axis=1)
            dh = mm(dproj, wl["mla_w_in"], "nt", name="mla_in_dx" + tag)
            gr["mla_w_in"] = mm(sv["h"], dproj, "tn", name="mla_in_dw" + tag, out_dtypes=(BF16,))
        dx, dsc1, dsh1, sm["norm1_g"][l] = norm_bwd(dh, sv["x0"], sv["rstd1"], dx1, norm1_g[l][None], sc1,
                                                    "norm1_bwd" + tag)
        dmod[l] = jnp.concatenate([dsh1, dsc1, dgate1, dsh2, dsc2, dgate2], axis=1)

        gp = small_gather() if l == 0 else None
        pend_mix, tok_mix = scatter_start(gr, "mix", tag, after=gp)
        scatter_finish(pend_mlp, dx, l)

    rows_p = gp.shape[1]
    per = 6 * D // LANES
    dm = lax.slice_in_dim(gp, 0, L * per, axis=1).reshape(NDEV, L, per, LANES)
    dmod_cols = lax.dynamic_slice_in_dim(dm, me * (AW // LANES), AW // LANES, axis=2).reshape(NDEV, L * AW) + tok_mix

    def sum_fn(t, c_, o):
        acc = t[0][0]
        for s_ in range(1, NDEV):
            acc = acc + t[0][s_]
        o[0][...] = acc
        return []

    gsummed = rowwise(sum_fn, [gp], outs=[((rows_p, LANES), F32)], ts=_tile(rows_p, 256, 8), name="sum_smallgrads")[0]
    gsum = dict(zip(small, _unpack(gsummed, shapes)))
    gsum["mla_q_norm_g"] = _shard_cols(gsum["mla_q_norm_g"], me, R // NDEV, 1)
    gsum["mla_kv_norm_g"] = _shard_cols(gsum["mla_kv_norm_g"], me, R // NDEV, 1)
    gsum["conv_w"] = _shard_cols(gsum["conv_w"], me, CB, 2)
    sm_shapes = [W[k].shape for k in small]
    sres = adamw(_pack([W[k] for k in small]), _pack([M[k] for k in small]), _pack([V[k] for k in small]),
                 (_pack([gsum[k] for k in small]) + tok_mix)[None], row0=0, name="adamw_small")
    small_out = {k: vals for k, vals in zip(small, zip(*[_unpack(r_, sm_shapes) for r_ in sres]))}

    g_ada = mm(c_act, dmod_cols, "tn", name="ada_dw", out_blocked=L, tm=1024, tn=768, tk=NDEV, cast=None,
               precision=lax.Precision.HIGHEST)
    ada_out = adamw(ada_w.reshape(L * D, AW), m_ada_w.reshape(L * D, AW), v_ada_w.reshape(L * D, AW),
                    g_ada.reshape(1, L * D, AW), row0=0, name="adamw_ada_w")
    scatter_finish(pend_mix, ada_out[1], 0)
    ada_out = [a.reshape(L, D, AW) for a in ada_out]

    def result(k, which):
        if k == "ada_w":
            return ada_out[which]
        if k in BIG:
            return jnp.stack([r_[which] for r_ in big_out[k]]).reshape(W[k].shape)
        return small_out[k][which]

    outs = [loss, dx[None]]
    for which in range(4):
        outs += [result(k, which) for k in ORDER]
    return tuple(outs)
```

```python
import functools

import jax
import jax.numpy as jnp
from jax import lax
from jax.experimental import pallas as pl
from jax.experimental.pallas import tpu as pltpu

F32 = jnp.float32
BF16 = jnp.bfloat16
EPS = 1e-6
NDEV = 8
LANES = 128
CHUNK = 128
GROUP = 128
CONV_W = 31
CONV_PAD = 32
NOPE, ROPE, VDIM = 128, 64, 128
QK = NOPE + ROPE
ROPE_THETA = 10000.0
VMEM_LIMIT = 56 * 1024 * 1024
ADAM_LR, ADAM_B1, ADAM_B2, ADAM_EPS, ADAM_WD, ADAM_STEP = 0.001, 0.9, 0.999, 1e-08, 0.01, 10
MESH = pl.DeviceIdType.MESH
NEG = -1e30
ATTN_BLOCK = 1024
ATTN_STRIP = 64


def _pcall(body, **kw):
    return pl.pallas_call(body, **kw)


def _params(sem=None):
    return pltpu.CompilerParams(dimension_semantics=sem, vmem_limit_bytes=VMEM_LIMIT)


def _tile(dim, target, align=LANES):
    if dim <= target:
        return dim
    t = (target // align) * align
    while t >= align:
        if dim % t == 0:
            return t
        t -= align
    return dim


def _rstd(x):
    return lax.rsqrt(jnp.mean(x * x, axis=-1, keepdims=True) + EPS)


def _sigmoid(x):
    return 1.0 / (1.0 + jnp.exp(-x))


_GC = 0.7978845608028654


def _gelu(x):
    return 0.5 * x * (1.0 + jnp.tanh(_GC * (x + 0.044715 * x * x * x)))


def _gelu_grad(x):
    t = jnp.tanh(_GC * (x + 0.044715 * x * x * x))
    return 0.5 * (1.0 + t) + 0.5 * x * (1.0 - t * t) * _GC * (1.0 + 3 * 0.044715 * x * x)


def _colsum(x):
    return jnp.sum(x, axis=0, keepdims=True)


def _rms_bwd(dy, xhat, rstd, g):
    dxh = dy * g
    dx = rstd * (dxh - xhat * jnp.mean(dxh * xhat, axis=-1, keepdims=True))
    return dx, _colsum(dy * xhat)


def _swap_halves(x):
    h = x.shape[-1] // 2
    return jnp.concatenate([x[:, h:], x[:, :h]], axis=1)


def _rope(x, cos2, sinm):
    return x * cos2 + _swap_halves(x) * sinm


def _unrope(dy, cos2, sinm):
    return dy * cos2 + _swap_halves(dy * sinm)


_DIMS = {"nn": (((1,), (0,)), ((), ())), "nt": (((1,), (1,)), ((), ())), "tn": (((0,), (0,)), ((), ()))}


def mm(a, b, mode, *, name, out_dtypes=(F32,), epi=None, extras=(), rowvecs=(), b_blocked=False, out_blocked=0,
       tm=1024, tn=1024, tk=2048, precision=None, cast=BF16):
    if mode == "tn":
        K, M = a.shape
    else:
        M, K = a.shape
    if b_blocked:
        J, Rb, Cb = b.shape
        N = Rb if mode == "nt" else J * Cb
    else:
        N = b.shape[0] if mode == "nt" else b.shape[1]
    tm = _tile(M, tm)
    if mode == "nn" and b_blocked:
        tn = _tile(Cb, tn)
    elif out_blocked:
        tn = _tile(N // out_blocked, tn)
    else:
        tn = _tile(N, tn)
    kb = 1
    if mode == "nt" and b_blocked:
        if Cb >= tk:
            tk = _tile(Cb, tk)
        else:
            kb = max(d for d in range(1, J + 1) if J % d == 0 and d * Cb <= tk)
            tk = kb * Cb
    else:
        tk = _tile(K, tk)
    nk = K // tk
    grid = (M // tm, N // tn, nk)

    if mode == "tn":
        a_spec = pl.BlockSpec((tk, tm), lambda i, j, k: (k, i))
    else:
        a_spec = pl.BlockSpec((tm, tk), lambda i, j, k: (i, k))
    if mode == "nn":
        if b_blocked:
            nper = Cb // tn
            b_spec = pl.BlockSpec((None, tk, tn), lambda i, j, k: (j // nper, k, j % nper))
        else:
            b_spec = pl.BlockSpec((tk, tn), lambda i, j, k: (k, j))
    elif mode == "nt":
        if b_blocked:
            if kb > 1:
                b_spec = pl.BlockSpec((kb, tn, Cb), lambda i, j, k: (k, j, 0))
            else:
                kper = Cb // tk
                b_spec = pl.BlockSpec((None, tn, tk), lambda i, j, k: (k // kper, j, k % kper))
        else:
            b_spec = pl.BlockSpec((tn, tk), lambda i, j, k: (j, k))
    else:
        b_spec = pl.BlockSpec((tk, tn), lambda i, j, k: (k, j))
    if out_blocked:
        oper = (N // out_blocked) // tn
        o_spec = pl.BlockSpec((None, tm, tn), lambda i, j, k: (j // oper, i, j % oper))
        o_shape = (out_blocked, M, N // out_blocked)
    else:
        o_spec = pl.BlockSpec((tm, tn), lambda i, j, k: (i, j))
        o_shape = (M, N)
    e_spec = pl.BlockSpec((tm, tn), lambda i, j, k: (i, j))
    r_spec = pl.BlockSpec((1, tn), lambda i, j, k: (0, j))
    ne, nr, no = len(extras), len(rowvecs), len(out_dtypes)
    dims = _DIMS[mode]

    def body(a_ref, b_ref, *rest):
        ex = rest[:ne]
        rv = rest[ne:ne + nr]
        outs = rest[ne + nr:ne + nr + no]

        def product():
            if kb > 1:
                r = None
                for q in range(kb):
                    av, bv = a_ref[:, q * Cb:(q + 1) * Cb], b_ref[q]
                    if cast is not None:
                        av, bv = av.astype(cast), bv.astype(cast)
                    d = lax.dot_general(av, bv, dims, preferred_element_type=F32, precision=precision)
                    r = d if r is None else r + d
                return r
            av, bv = a_ref[...], b_ref[...]
            if cast is not None:
                av, bv = av.astype(cast), bv.astype(cast)
            return lax.dot_general(av, bv, dims, preferred_element_type=F32, precision=precision)

        def finish(r):
            vals = (r,) if epi is None else epi(r, *[e[...] for e in ex], *[v[...] for v in rv])
            for o, val in zip(outs, vals):
                o[...] = val.astype(o.dtype)

        if nk == 1:
            finish(product())
            return
        acc = rest[ne + nr + no]
        k = pl.program_id(2)

        @pl.when(k == 0)
        def _():
            acc[...] = product()

        @pl.when((k > 0) & (k < nk - 1))
        def _():
            acc[...] += product()

        @pl.when(k == nk - 1)
        def _():
            finish(acc[...] + product())

    res = _pcall(
        body, name=name, grid=grid,
        in_specs=[a_spec, b_spec] + [e_spec] * ne + [r_spec] * nr,
        out_specs=[o_spec] * no,
        out_shape=[jax.ShapeDtypeStruct(o_shape, dt) for dt in out_dtypes],
        scratch_shapes=[] if nk == 1 else [pltpu.VMEM((tm, tn), F32)],
        compiler_params=_params(("parallel", "parallel", "arbitrary")),
    )(a, b, *extras, *rowvecs)
    return res[0] if no == 1 else res


def rowwise(fn, tiled, consts=(), outs=(), reds=(), *, ts, name):
    specs = []
    arrs = []
    rows = None
    for t in tiled:
        a, w, cb = t if isinstance(t, tuple) else (t, None, 0)
        arrs.append(a)
        rows = a.shape[-2] if rows is None else rows
        if a.ndim == 2:
            specs.append(pl.BlockSpec((ts, a.shape[1] if w is None else w), lambda i, cb=cb: (i, cb)))
        else:
            specs.append(pl.BlockSpec((a.shape[0], ts, a.shape[2]), lambda i: (0, i, 0)))
    for a in consts:
        specs.append(pl.BlockSpec(a.shape, lambda i, n=a.ndim: (0,) * n))
    o_specs, o_shapes = [], []
    for shp, dt in outs:
        if len(shp) == 2:
            o_specs.append(pl.BlockSpec((ts, shp[1]), lambda i: (i, 0)))
        else:
            o_specs.append(pl.BlockSpec((shp[0], ts, shp[2]), lambda i: (0, i, 0)))
        o_shapes.append(jax.ShapeDtypeStruct(shp, dt))
    for shp in reds:
        o_specs.append(pl.BlockSpec(shp, lambda i, n=len(shp): (0,) * n))
        o_shapes.append(jax.ShapeDtypeStruct(shp, F32))
    nt, nc, no = len(arrs), len(consts), len(outs)

    def body(*refs):
        i = pl.program_id(0)
        red_refs = refs[nt + nc + no:]
        vals = fn(refs[:nt], refs[nt:nt + nc], refs[nt + nc:nt + nc + no])
        if red_refs:
            @pl.when(i == 0)
            def _():
                for r in red_refs:
                    r[...] = jnp.zeros_like(r)
            for r, v in zip(red_refs, vals):
                r[...] += v

    return _pcall(body, name=name, grid=(rows // ts,), in_specs=specs, out_specs=o_specs, out_shape=o_shapes,
                  compiler_params=_params(("arbitrary",)))(*arrs, *consts)


def cast_bf16(w, l, name):
    _, R, C = w.shape
    tr = _tile(R, 512, 16)

    def body(w_ref, o_ref):
        o_ref[...] = w_ref[...].astype(BF16)

    return _pcall(body, name=name, grid=(R // tr,), in_specs=[pl.BlockSpec((None, tr, C), lambda i: (l, i, 0))],
                  out_specs=pl.BlockSpec((tr, C), lambda i: (i, 0)), out_shape=jax.ShapeDtypeStruct((R, C), BF16),
                  compiler_params=_params(("parallel",)))(w)


def prenorm(x, g, scale, shift, name):
    S, D = x.shape

    def fn(t, c, o):
        xv = t[0][...]
        r = _rstd(xv)
        o[0][...] = ((xv * r * c[0][...]) * (1.0 + c[1][...]) + c[2][...]).astype(BF16)
        o[1][...] = r
        return []

    return rowwise(fn, [x], [g, scale, shift], [((S, D), BF16), ((S, 1), F32)], ts=_tile(S, 128, 16), name=name)


def gate_bwd(dx, y, gate, name):
    S, D = dx.shape

    def fn(t, c, o):
        d = t[0][...]
        o[0][...] = (d * c[0][...]).astype(BF16)
        return [_colsum(d * t[1][...].astype(F32))]

    return rowwise(fn, [dx, y], [gate], [((S, D), BF16)], [(1, D)], ts=_tile(S, 128, 16), name=name)


def norm_bwd(dh, x, rstd, dres, g, scale, name):
    S, D = x.shape

    def fn(t, c, o):
        d = t[0][...]
        r = t[2][...]
        xh = t[1][...] * r
        gv = c[0][...]
        dr = d * (1.0 + c[1][...])
        dx, dg = _rms_bwd(dr, xh, r, gv)
        o[0][...] = t[3][...] + dx
        return [_colsum(d * (xh * gv)), _colsum(d), dg]

    return rowwise(fn, [dh, x, rstd, dres], [g, scale], [((S, D), F32)], [(1, D)] * 3, ts=_tile(S, 128, 8), name=name)


def loss_grad(y, tgt, name):
    S, D = y.shape

    def fn(t, c, o):
        e = t[0][...] - t[1][...]
        o[0][...] = e * (1.0 / D)
        return [_colsum(e * e)]

    return rowwise(fn, [y, tgt], outs=[((S, D), F32)], reds=[(1, D)], ts=_tile(S, 128, 8), name=name)


def _tril_mask():
    r = lax.broadcasted_iota(jnp.int32, (CHUNK, CHUNK), 0)
    c = lax.broadcasted_iota(jnp.int32, (CHUNK, CHUNK), 1)
    return c <= r


def sgu_fwd(proj, ng, w, bcol, DA, name):
    S = proj.shape[0]
    G = DA // GROUP
    tr = _tile(S, 2 * CHUNK)

    def body(u_ref, v_ref, ng_ref, w_ref, b_ref, o_ref):
        mask = _tril_mask()
        for g in range(G):
            cols = slice(g * GROUP, (g + 1) * GROUP)
            wm = jnp.where(mask, w_ref[g], 0.0).astype(BF16)
            for ci in range(tr // CHUNK):
                rows = slice(ci * CHUNK, (ci + 1) * CHUNK)
                gv = _gelu(v_ref[rows, cols])
                vn = gv * _rstd(gv) * ng_ref[:, cols]
                mixed = jnp.dot(wm, vn.astype(BF16), preferred_element_type=F32) + b_ref[g]
                o_ref[rows, cols] = (_gelu(u_ref[rows, cols]) * mixed).astype(o_ref.dtype)

    return _pcall(
        body, name=name, grid=(S // tr,),
        in_specs=[pl.BlockSpec((tr, DA), lambda i: (i, 0)), pl.BlockSpec((tr, DA), lambda i: (i, 1)),
                  pl.BlockSpec((1, DA), lambda i: (0, 0)), pl.BlockSpec((G, CHUNK, CHUNK), lambda i: (0, 0, 0)),
                  pl.BlockSpec((G, CHUNK, 1), lambda i: (0, 0, 0))],
        out_specs=pl.BlockSpec((tr, DA), lambda i: (i, 0)),
        out_shape=jax.ShapeDtypeStruct((S, DA), BF16),
        compiler_params=_params(("parallel",)),
    )(proj, proj, ng, w, bcol)


def sgu_bwd(proj, dcat, ng, w, bcol, DA, name):
    S = proj.shape[0]
    G = DA // GROUP
    tr = _tile(S, 2 * CHUNK)
    nsteps = S // tr

    def body(u_ref, v_ref, d_ref, ng_ref, w_ref, b_ref, duv_ref, dw_ref, db_ref, dng_ref, dbacc):
        i = pl.program_id(0)

        @pl.when(i == 0)
        def _():
            dw_ref[...] = jnp.zeros_like(dw_ref)
            dng_ref[...] = jnp.zeros_like(dng_ref)
            dbacc[...] = jnp.zeros_like(dbacc)

        mask = _tril_mask()
        for g in range(G):
            cols = slice(g * GROUP, (g + 1) * GROUP)
            wm = jnp.where(mask, w_ref[g], 0.0).astype(BF16)
            ngg = ng_ref[:, cols]
            for ci in range(tr // CHUNK):
                rows = slice(ci * CHUNK, (ci + 1) * CHUNK)
                u, v, d = u_ref[rows, cols], v_ref[rows, cols], d_ref[rows, cols]
                gv = _gelu(v)
                rs = _rstd(gv)
                vhat = gv * rs
                vn = (vhat * ngg).astype(BF16)
                mixed = jnp.dot(wm, vn, preferred_element_type=F32) + b_ref[g]
                dmixed = d * _gelu(u)
                dmb = dmixed.astype(BF16)
                duv_ref[rows, cols] = (d * mixed * _gelu_grad(u)).astype(duv_ref.dtype)
                dwg = lax.dot_general(dmb, vn, _DIMS["nt"], preferred_element_type=F32)
                dw_ref[g] += jnp.where(mask, dwg, 0.0)
                dbacc[g] += dmixed
                dvn = lax.dot_general(wm, dmb, _DIMS["tn"], preferred_element_type=F32)
                dgv, dngg = _rms_bwd(dvn, vhat, rs, ngg)
                dng_ref[:, cols] += dngg
                duv_ref[rows, DA + g * GROUP:DA + (g + 1) * GROUP] = (dgv * _gelu_grad(v)).astype(duv_ref.dtype)

        @pl.when(i == nsteps - 1)
        def _():
            for g in range(G):
                db_ref[g] = jnp.sum(dbacc[g], axis=-1, keepdims=True)

    return _pcall(
        body, name=name, grid=(nsteps,),
        in_specs=[pl.BlockSpec((tr, DA), lambda i: (i, 0)), pl.BlockSpec((tr, DA), lambda i: (i, 1)),
                  pl.BlockSpec((tr, DA), lambda i: (i, 0)),
                  pl.BlockSpec((1, DA), lambda i: (0, 0)), pl.BlockSpec((G, CHUNK, CHUNK), lambda i: (0, 0, 0)),
                  pl.BlockSpec((G, CHUNK, 1), lambda i: (0, 0, 0))],
        out_specs=[pl.BlockSpec((tr, 2 * DA), lambda i: (i, 0)), pl.BlockSpec((G, CHUNK, CHUNK), lambda i: (0, 0, 0)),
                   pl.BlockSpec((G, CHUNK, 1), lambda i: (0, 0, 0)), pl.BlockSpec((1, DA), lambda i: (0, 0))],
        out_shape=[jax.ShapeDtypeStruct((S, 2 * DA), BF16), jax.ShapeDtypeStruct((G, CHUNK, CHUNK), F32),
                   jax.ShapeDtypeStruct((G, CHUNK, 1), F32), jax.ShapeDtypeStruct((1, DA), F32)],
        scratch_shapes=[pltpu.VMEM((G, CHUNK, CHUNK), F32)],
        compiler_params=_params(("arbitrary",)),
    )(proj, proj, dcat, ng, w, bcol)


def _conv_tile(S):
    return _tile(S, 256, 8)


def conv_fwd(proj, wk, bias, DA, DB, name):
    S = proj.shape[0]
    nb = DB // LANES
    a0, g0 = 2 * DA // LANES, (2 * DA + DB) // LANES
    T = _conv_tile(S)
    off = CONV_PAD - (CONV_W - 1)

    def body(a_ref, g_ref, w_ref, b_ref, o_ref, ypad):
        ypad[0:CONV_PAD, :] = jnp.zeros((CONV_PAD, LANES), F32)

        def fill(t, cr):
            r = pl.multiple_of(t * T, T)
            ypad[pl.ds(CONV_PAD + r, T), :] = a_ref[pl.ds(r, T), :] * _sigmoid(g_ref[pl.ds(r, T), :])
            return cr

        lax.fori_loop(0, S // T, fill, 0)

        def step(t, cr):
            r = pl.multiple_of(t * T, T)
            acc = jnp.zeros((T, LANES), F32) + b_ref[...]
            for k in range(CONV_W):
                acc = acc + w_ref[k:k + 1, :] * ypad[pl.ds(r + (k + off), T), :]
            o_ref[pl.ds(r, T), :] = acc
            return cr

        lax.fori_loop(0, S // T, step, 0)

    return _pcall(
        body, name=name, grid=(nb,),
        in_specs=[pl.BlockSpec((S, LANES), lambda j: (0, a0 + j)), pl.BlockSpec((S, LANES), lambda j: (0, g0 + j)),
                  pl.BlockSpec((CONV_PAD, LANES), lambda j: (0, j)), pl.BlockSpec((1, LANES), lambda j: (0, j))],
        out_specs=pl.BlockSpec((S, LANES), lambda j: (0, j)),
        out_shape=jax.ShapeDtypeStruct((S, DB), F32),
        scratch_shapes=[pltpu.VMEM((S + CONV_PAD, LANES), F32)],
        compiler_params=_params(("parallel",)),
    )(proj, proj, wk, bias)


def conv_bwd(proj, dy1, wk, DA, DB, name):
    S = proj.shape[0]
    nb = DB // LANES
    a0, g0 = 2 * DA // LANES, (2 * DA + DB) // LANES
    T = _conv_tile(S)
    off = CONV_PAD - (CONV_W - 1)

    def body(a_ref, g_ref, d_ref, w_ref, da_ref, dg_ref, dw_ref, ypad, dpad, wacc):
        ypad[0:CONV_PAD, :] = jnp.zeros((CONV_PAD, LANES), F32)
        dpad[S:S + CONV_PAD, :] = jnp.zeros((CONV_PAD, LANES), F32)
        wacc[...] = jnp.zeros_like(wacc)

        def fill(t, cr):
            r = pl.multiple_of(t * T, T)
            ypad[pl.ds(CONV_PAD + r, T), :] = a_ref[pl.ds(r, T), :] * _sigmoid(g_ref[pl.ds(r, T), :])
            dpad[pl.ds(r, T), :] = d_ref[pl.ds(r, T), :]
            return cr

        lax.fori_loop(0, S // T, fill, 0)

        def step(t, cr):
            r = pl.multiple_of(t * T, T)
            dt = dpad[pl.ds(r, T), :]
            dy0 = jnp.zeros((T, LANES), F32)
            for k in range(CONV_W):
                prod = dt * ypad[pl.ds(r + (k + off), T), :]
                wacc[k] += jnp.sum(prod.reshape(T // 8, 8, LANES), axis=0)
                dy0 = dy0 + w_ref[k:k + 1, :] * dpad[pl.ds(r + (CONV_W - 1 - k), T), :]
            av, gv = a_ref[pl.ds(r, T), :], g_ref[pl.ds(r, T), :]
            sg = _sigmoid(gv)
            da_ref[pl.ds(r, T), :] = (dy0 * sg).astype(da_ref.dtype)
            dg_ref[pl.ds(r, T), :] = (dy0 * av * sg * (1.0 - sg)).astype(dg_ref.dtype)
            return cr

        lax.fori_loop(0, S // T, step, 0)
        for k in range(CONV_W):
            dw_ref[k:k + 1, :] = jnp.sum(wacc[k], axis=0, keepdims=True)
        dw_ref[CONV_W:CONV_PAD, :] = jnp.zeros((CONV_PAD - CONV_W, LANES), F32)

    return _pcall(
        body, name=name, grid=(nb,),
        in_specs=[pl.BlockSpec((S, LANES), lambda j: (0, a0 + j)), pl.BlockSpec((S, LANES), lambda j: (0, g0 + j)),
                  pl.BlockSpec((S, LANES), lambda j: (0, j)), pl.BlockSpec((CONV_PAD, LANES), lambda j: (0, j))],
        out_specs=[pl.BlockSpec((S, LANES), lambda j: (0, j)), pl.BlockSpec((S, LANES), lambda j: (0, j)),
                   pl.BlockSpec((CONV_PAD, LANES), lambda j: (0, j))],
        out_shape=[jax.ShapeDtypeStruct((S, DB), BF16), jax.ShapeDtypeStruct((S, DB), BF16),
                   jax.ShapeDtypeStruct((CONV_PAD, DB), F32)],
        scratch_shapes=[pltpu.VMEM((S + CONV_PAD, LANES), F32), pltpu.VMEM((S + CONV_PAD, LANES), F32),
                        pltpu.VMEM((CONV_PAD, 8, LANES), F32)],
        compiler_params=_params(("parallel",)),
    )(proj, proj, dy1, wk)


def _ln_stats(y):
    mu = jnp.mean(y, axis=-1, keepdims=True)
    yc = y - mu
    rs = lax.rsqrt(jnp.mean(yc * yc, axis=-1, keepdims=True) + EPS)
    return yc * rs, rs


def ln_silu(y1, lg, lb, name):
    S, DB = y1.shape

    def fn(t, c, o):
        yh, _ = _ln_stats(t[0][...])
        ln = yh * c[0][...] + c[1][...]
        o[0][...] = (ln * _sigmoid(ln)).astype(BF16)
        return []

    return rowwise(fn, [y1], [lg, lb], [((S, DB), BF16)], ts=_tile(S, 128, 16), name=name)[0]


def ln_silu_bwd(dcat, y1, lg, lb, name):
    S, DB = y1.shape
    cb = (dcat.shape[1] - DB) // DB

    def fn(t, c, o):
        yh, rs = _ln_stats(t[1][...])
        gv = c[0][...]
        ln = yh * gv + c[1][...]
        sg = _sigmoid(ln)
        dln = t[0][...] * (sg * (1.0 + ln * (1.0 - sg)))
        dyh = dln * gv
        dy = rs * (dyh - jnp.mean(dyh, axis=-1, keepdims=True) - yh * jnp.mean(dyh * yh, axis=-1, keepdims=True))
        o[0][...] = dy
        return [_colsum(dln * yh), _colsum(dln), _colsum(dy)]

    return rowwise(fn, [(dcat, DB, cb), y1], [lg, lb], [((S, DB), F32)], [(1, DB)] * 3, ts=_tile(S, 128, 8), name=name)


def mla_norms(proj, cos2, sinm, qg, kvg, kgr, R, name):
    S = proj.shape[0]

    def fn(t, c, o):
        cq = t[0][:, 0:R]
        ckv = t[0][:, R:2 * R]
        kr = t[0][:, 2 * R:2 * R + ROPE]
        o[0][...] = (cq * _rstd(cq) * c[0][...]).astype(BF16)
        o[1][...] = (ckv * _rstd(ckv) * c[1][...]).astype(BF16)
        o[2][...] = _rope(kr * _rstd(kr) * c[2][...], t[1][...], t[2][...])
        return []

    return rowwise(fn, [proj, cos2, sinm], [qg, kvg, kgr], [((S, R), BF16), ((S, R), BF16), ((S, ROPE), F32)],
                   ts=_tile(S, 128, 16), name=name)


def mla_heads(q, kv, kr, cos2, sinm, qg, kg, H, name):
    S = q.shape[0]

    def fn(t, c, o):
        cs, sn = t[3][...], t[4][...]
        krv = t[2][...]
        qgn, qgr, kgn = c[0][:, 0:NOPE], c[0][:, NOPE:QK], c[1][:, 0:NOPE]
        for h in range(H):
            qn = t[0][:, QK * h:QK * h + NOPE]
            qr = t[0][:, QK * h + NOPE:QK * (h + 1)]
            o[0][h, :, 0:NOPE] = (qn * _rstd(qn) * qgn).astype(BF16)
            o[0][h, :, NOPE:QK] = _rope(qr * _rstd(qr) * qgr, cs, sn).astype(BF16)
            kn = t[1][:, (NOPE + VDIM) * h:(NOPE + VDIM) * h + NOPE]
            o[1][h, :, 0:NOPE] = (kn * _rstd(kn) * kgn).astype(BF16)
            o[1][h, :, NOPE:QK] = krv.astype(BF16)
            o[2][h] = t[1][:, (NOPE + VDIM) * h + NOPE:(NOPE + VDIM) * (h + 1)].astype(BF16)
        return []

    return rowwise(fn, [q, kv, kr, cos2, sinm], [qg, kg],
                   [((H, S, QK), BF16), ((H, S, QK), BF16), ((H, S, VDIM), BF16)], ts=_tile(S, 128, 16), name=name)


def mla_heads_bwd(dQ, dK, dV, q, kv, cos2, sinm, qg, kg, H, name):
    S = q.shape[0]
    KV = NOPE + VDIM

    def fn(t, c, o):
        cs, sn = t[5][...], t[6][...]
        qgn, qgr, kgn = c[0][:, 0:NOPE], c[0][:, NOPE:QK], c[1][:, 0:NOPE]
        a_qn = jnp.zeros((1, NOPE), F32)
        a_qr = jnp.zeros((1, ROPE), F32)
        a_kn = jnp.zeros((1, NOPE), F32)
        dkr = jnp.zeros((t[0].shape[1], ROPE), F32)
        for h in range(H):
            qn = t[3][:, QK * h:QK * h + NOPE]
            rs = _rstd(qn)
            dx, dg = _rms_bwd(t[0][h, :, 0:NOPE], qn * rs, rs, qgn)
            o[0][:, QK * h:QK * h + NOPE] = dx.astype(BF16)
            a_qn = a_qn + dg
            qr = t[3][:, QK * h + NOPE:QK * (h + 1)]
            rs = _rstd(qr)
            dx, dg = _rms_bwd(_unrope(t[0][h, :, NOPE:QK], cs, sn), qr * rs, rs, qgr)
            o[0][:, QK * h + NOPE:QK * (h + 1)] = dx.astype(BF16)
            a_qr = a_qr + dg
            kn = t[4][:, KV * h:KV * h + NOPE]
            rs = _rstd(kn)
            dx, dg = _rms_bwd(t[1][h, :, 0:NOPE], kn * rs, rs, kgn)
            o[1][:, KV * h:KV * h + NOPE] = dx.astype(BF16)
            a_kn = a_kn + dg
            o[1][:, KV * h + NOPE:KV * (h + 1)] = t[2][h].astype(BF16)
            dkr = dkr + t[1][h, :, NOPE:QK]
        o[2][...] = _unrope(dkr, cs, sn)
        return [a_qn, a_qr, a_kn]

    return rowwise(fn, [dQ, dK, dV, q, kv, cos2, sinm], [qg, kg],
                   [((S, H * QK), BF16), ((S, H * KV), BF16), ((S, ROPE), F32)],
                   [(1, NOPE), (1, ROPE), (1, NOPE)], ts=_tile(S, 128, 16), name=name)


def mla_norms_bwd(dqn, dkvn, dkr, proj, qg, kvg, kgr, R, name):
    S = proj.shape[0]

    def fn(t, c, o):
        reds = []
        for idx, (lo, hi) in enumerate(((0, R), (R, 2 * R), (2 * R, 2 * R + ROPE))):
            xv = t[3][:, lo:hi]
            rs = _rstd(xv)
            dx, dg = _rms_bwd(t[idx][...], xv * rs, rs, c[idx][...])
            o[0][:, lo:hi] = dx.astype(BF16)
            reds.append(dg)
        return reds

    return rowwise(fn, [dqn, dkvn, dkr, proj], [qg, kvg, kgr], [((S, 2 * R + ROPE), BF16)],
                   [(1, R), (1, R), (1, ROPE)], ts=_tile(S, 128, 16), name=name)


def _tri_rows(p, n):
    qi = 0
    for j in range(1, n):
        qi = qi + (p >= j * (j + 1) // 2).astype(jnp.int32)
    return qi, p - (qi * (qi + 1)) // 2


def _tri_cols(p, n):
    ki = 0
    for j in range(1, n):
        ki = ki + (p >= j * n - j * (j - 1) // 2).astype(jnp.int32)
    return ki, ki + p - (ki * n - (ki * (ki - 1)) // 2)


def attn_fwd(Q, K, V, name):
    H, S, _ = Q.shape
    t = _tile(S, ATTN_BLOCK)
    n = S // t
    scale = QK ** -0.5

    rs = _tile(t, ATTN_STRIP, 8)

    def body(q_ref, k_ref, v_ref, o_ref, lse_ref, m_s, l_s, acc, s_scr, p_scr):
        qi, ki = _tri_rows(pl.program_id(1), n)

        @pl.when(ki == 0)
        def _():
            m_s[...] = jnp.full_like(m_s, NEG)
            l_s[...] = jnp.zeros_like(l_s)
            acc[...] = jnp.zeros_like(acc)

        def block(diagonal):
            s_scr[...] = lax.dot_general(q_ref[...], k_ref[...], _DIMS["nt"], preferred_element_type=F32)

            def strip(i, cr):
                r = slice(i * rs, (i + 1) * rs)
                s = s_scr[r, :] * scale
                if diagonal:
                    row = i * rs + lax.broadcasted_iota(jnp.int32, (rs, t), 0)
                    s = jnp.where(lax.broadcasted_iota(jnp.int32, (rs, t), 1) <= row, s, NEG)
                m_old = m_s[r, :]
                m_new = jnp.maximum(m_old, jnp.max(s, axis=-1, keepdims=True))
                alpha = jnp.exp(m_old - m_new)
                p = jnp.exp(s - m_new)
                l_s[r, :] = alpha * l_s[r, :] + jnp.sum(p, axis=-1, keepdims=True)
                m_s[r, :] = m_new
                acc[r, :] = alpha * acc[r, :]
                p_scr[r, :] = p.astype(BF16)
                return cr

            for i in range(t // rs):
                strip(i, 0)
            acc[...] += jnp.dot(p_scr[...], v_ref[...], preferred_element_type=F32)

        @pl.when(ki < qi)
        def _():
            block(False)

        @pl.when(ki == qi)
        def _():
            block(True)

        @pl.when(ki == qi)
        def _():
            o_ref[...] = (acc[...] / l_s[...]).astype(o_ref.dtype)
            lse_ref[...] = m_s[...] + jnp.log(l_s[...])

    return _pcall(
        body, name=name, grid=(H, n * (n + 1) // 2),
        in_specs=[pl.BlockSpec((None, t, QK), lambda h, p: (h, _tri_rows(p, n)[0], 0)),
                  pl.BlockSpec((None, t, QK), lambda h, p: (h, _tri_rows(p, n)[1], 0)),
                  pl.BlockSpec((None, t, VDIM), lambda h, p: (h, _tri_rows(p, n)[1], 0))],
        out_specs=[pl.BlockSpec((t, VDIM), lambda h, p: (_tri_rows(p, n)[0], h)),
                   pl.BlockSpec((None, t, 1), lambda h, p: (h, _tri_rows(p, n)[0], 0))],
        out_shape=[jax.ShapeDtypeStruct((S, H * VDIM), BF16), jax.ShapeDtypeStruct((H, S, 1), F32)],
        scratch_shapes=[pltpu.VMEM((t, 1), F32), pltpu.VMEM((t, 1), F32), pltpu.VMEM((t, VDIM), F32),
                        pltpu.VMEM((t, t), F32), pltpu.VMEM((t, t), BF16)],
        compiler_params=_params(("parallel", "arbitrary")),
    )(Q, K, V)


def attn_bwd(Q, K, V, dO, O, lse, name):
    H, S, _ = Q.shape
    t = _tile(S, ATTN_BLOCK)
    n = S // t
    scale = QK ** -0.5

    rs = _tile(t, ATTN_STRIP, 8)

    def body(q_ref, k_ref, v_ref, do_ref, o_ref, lse_ref, dq_ref, dk_ref, dv_ref, s_scr, dp_scr, p_scr, ds_scr):
        ki, qi = _tri_cols(pl.program_id(1), n)

        @pl.when(pl.program_id(1) == 0)
        def _():
            dq_ref[...] = jnp.zeros_like(dq_ref)

        @pl.when(qi == ki)
        def _():
            dk_ref[...] = jnp.zeros_like(dk_ref)
            dv_ref[...] = jnp.zeros_like(dv_ref)

        def block(diagonal):
            s_scr[...] = lax.dot_general(q_ref[...], k_ref[...], _DIMS["nt"], preferred_element_type=F32)
            dp_scr[...] = lax.dot_general(do_ref[...], v_ref[...], _DIMS["nt"], preferred_element_type=F32)

            def strip(i, cr):
                r = slice(i * rs, (i + 1) * rs)
                s = s_scr[r, :] * scale
                if diagonal:
                    row = i * rs + lax.broadcasted_iota(jnp.int32, (rs, t), 0)
                    s = jnp.where(lax.broadcasted_iota(jnp.int32, (rs, t), 1) <= row, s, NEG)
                p = jnp.exp(s - lse_ref[r, :])
                delta = jnp.sum(do_ref[r, :].astype(F32) * o_ref[r, :].astype(F32), axis=-1, keepdims=True)
                p_scr[r, :] = p.astype(BF16)
                ds_scr[r, :] = (p * (dp_scr[r, :] - delta) * scale).astype(BF16)
                return cr

            for i in range(t // rs):
                strip(i, 0)
            ds = ds_scr[...]
            dv_ref[...] += lax.dot_general(p_scr[...], do_ref[...], _DIMS["tn"], preferred_element_type=F32)
            dk_ref[...] += lax.dot_general(ds, q_ref[...], _DIMS["tn"], preferred_element_type=F32)
            rq = pl.multiple_of(qi * t, t)
            dq_ref[pl.ds(rq, t), :] += jnp.dot(ds, k_ref[...], preferred_element_type=F32)

        @pl.when(qi > ki)
        def _():
            block(False)

        @pl.when(qi == ki)
        def _():
            block(True)

    qmap = lambda h, p: (h, _tri_cols(p, n)[1], 0)
    kmap = lambda h, p: (h, _tri_cols(p, n)[0], 0)
    return _pcall(
        body, name=name, grid=(H, n * (n + 1) // 2),
        in_specs=[pl.BlockSpec((None, t, QK), qmap),
                  pl.BlockSpec((None, t, QK), kmap),
                  pl.BlockSpec((None, t, VDIM), kmap),
                  pl.BlockSpec((t, VDIM), lambda h, p: (_tri_cols(p, n)[1], h)),
                  pl.BlockSpec((t, VDIM), lambda h, p: (_tri_cols(p, n)[1], h)),
                  pl.BlockSpec((None, t, 1), qmap)],
        out_specs=[pl.BlockSpec((None, S, QK), lambda h, p: (h, 0, 0)),
                   pl.BlockSpec((None, t, QK), kmap),
                   pl.BlockSpec((None, t, VDIM), kmap)],
        out_shape=[jax.ShapeDtypeStruct((H, S, QK), F32), jax.ShapeDtypeStruct((H, S, QK), F32),
                   jax.ShapeDtypeStruct((H, S, VDIM), F32)],
        scratch_shapes=[pltpu.VMEM((t, t), F32), pltpu.VMEM((t, t), F32), pltpu.VMEM((t, t), BF16),
                        pltpu.VMEM((t, t), BF16)],
        compiler_params=_params(("parallel", "arbitrary")),
    )(Q, K, V, dO, O, lse)


def adamw(w, m, v, g, *, row0, name):
    P, rows, C = g.shape
    tr = _tile(rows, max(16, 131072 // C), 16)
    off = row0 // tr
    assert row0 % tr == 0
    bc1 = 1.0 - ADAM_B1 ** ADAM_STEP
    bc2 = 1.0 - ADAM_B2 ** ADAM_STEP

    def body(w_ref, m_ref, v_ref, g_ref, go_ref, d_ref, mo_ref, vo_ref):
        gs = g_ref[0].astype(F32)
        for p in range(1, P):
            gs = gs + g_ref[p].astype(F32)
        wv = w_ref[...]
        mn = ADAM_B1 * m_ref[...] + (1.0 - ADAM_B1) * gs
        vn = ADAM_B2 * v_ref[...] + (1.0 - ADAM_B2) * (gs * gs)
        go_ref[...] = gs
        mo_ref[...] = mn
        vo_ref[...] = vn
        d_ref[...] = -ADAM_LR * ((mn / bc1) / (jnp.sqrt(vn / bc2) + ADAM_EPS) + ADAM_WD * wv)

    wspec = pl.BlockSpec((tr, C), lambda i: (i + off, 0))
    ospec = pl.BlockSpec((tr, C), lambda i: (i, 0))
    return _pcall(
        body, name=name, grid=(rows // tr,),
        in_specs=[wspec, wspec, wspec, pl.BlockSpec((P, tr, C), lambda i: (0, i, 0))],
        out_specs=[ospec] * 4, out_shape=[jax.ShapeDtypeStruct((rows, C), F32)] * 4,
        compiler_params=_params(("parallel",)),
    )(w, m, v, g)


def _coords():
    return lax.axis_index("x"), lax.axis_index("y"), lax.axis_index("c")


def _me():
    x, y, c = _coords()
    return 4 * x + 2 * y + c


_ANY = pl.BlockSpec(memory_space=pl.ANY)


def all_gather(items, name):
    n = len(items)
    blks = [a.shape if idx is None else a.shape[1:] for a, idx in items]

    def body(*refs):
        ins, outs = refs[:n], refs[n:2 * n]
        send, recv, lsem = refs[2 * n:]
        x, y, c = _coords()
        me, sib = (x, y, c), (x, y, 1 - c)
        chips = [(1 - x, y), (x, 1 - y), (1 - x, 1 - y)]

        def src(i):
            return ins[i] if items[i][1] is None else ins[i].at[items[i][1]]

        def slot(i, p):
            return outs[i].at[4 * p[0] + 2 * p[1] + p[2]]

        def cp(i, k, block, to, s=None):
            return pltpu.make_async_remote_copy(
                src_ref=slot(i, block) if s is None else s, dst_ref=slot(i, block),
                send_sem=send.at[7 * i + k], recv_sem=recv.at[7 * i + k], device_id=to, device_id_type=MESH)

        mine = [pltpu.make_async_copy(src(i), slot(i, me), lsem.at[i]) for i in range(n)]
        for m_ in mine:
            m_.start()
        first = []
        for i in range(n):
            first.append(cp(i, 0, me, sib, src(i)))
            first += [cp(i, 1 + j, me, (*chip, c), src(i)) for j, chip in enumerate(chips)]
        for f in first:
            f.start()
        passed = []
        for j, chip in enumerate(chips):
            for i in range(n):
                cp(i, 1 + j, (*chip, c), me).wait_recv()
                p_ = cp(i, 4 + j, (*chip, c), sib)
                p_.start()
                passed.append(p_)
        for i in range(n):
            cp(i, 0, sib, me).wait_recv()
            for j, chip in enumerate(chips):
                cp(i, 4 + j, (*chip, 1 - c), me).wait_recv()
        for f in first + passed:
            f.wait_send()
        for m_ in mine:
            m_.wait()

    res = _pcall(
        body, name=name, in_specs=[_ANY] * n, out_specs=[_ANY] * n,
        out_shape=[jax.ShapeDtypeStruct((NDEV,) + tuple(b), a.dtype) for b, (a, _) in zip(blks, items)],
        scratch_shapes=[pltpu.SemaphoreType.DMA((7 * n,)), pltpu.SemaphoreType.DMA((7 * n,)),
                        pltpu.SemaphoreType.DMA((n,))],
    )(*[a for a, _ in items])
    return list(res)


_HBM = pl.BlockSpec(memory_space=pltpu.HBM)
_SEM = pl.BlockSpec(memory_space=pltpu.SEMAPHORE)
_EFFECT = pltpu.SideEffectType.DATAFLOW_SIDE_EFFECTING


def _xchg_copy(src_ref, land_ref, send, recv, r, scatter, at_peer):
    x, y, c = _coords()
    px = jnp.bitwise_xor(x, (r >> 2) & 1)
    py = jnp.bitwise_xor(y, (r >> 1) & 1)
    pc = jnp.bitwise_xor(c, r & 1)
    p_i = 4 * px + 2 * py + pc
    me_i = 4 * x + 2 * y + c
    return pltpu.make_async_remote_copy(
        src_ref=src_ref.at[p_i] if scatter else src_ref, dst_ref=land_ref.at[p_i if at_peer else me_i],
        send_sem=send.at[r - 1], recv_sem=recv.at[r - 1], device_id=(px, py, pc), device_id_type=MESH)


def _phase(body, name, bufs, sems_in=(), new_sems=(), after=None, token=False):
    nb, ns, nn = len(bufs), len(sems_in), len(new_sems)

    def wrapped(*refs):
        outs = refs[nb + ns + (after is not None):]
        body(refs[:nb], refs[nb:nb + ns], outs[:nn])
        if token:
            outs[nn + nb][...] = jnp.zeros_like(outs[nn + nb])

    res = _pcall(
        wrapped, name=name,
        out_shape=tuple([pltpu.SemaphoreType.DMA((k,)) for k in new_sems] + [pltpu.HBM(a.shape, a.dtype) for a in bufs]
                        + ([jax.ShapeDtypeStruct((8, LANES), F32)] if token else [])),
        in_specs=[_HBM] * nb + [_SEM] * ns + ([] if after is None else [_ANY]),
        out_specs=tuple([_SEM] * nn + [_HBM] * nb + ([pl.BlockSpec(memory_space=pltpu.VMEM)] if token else [])),
        input_output_aliases={i: nn + i for i in range(nb)},
        compiler_params=pltpu.CompilerParams(has_side_effects=_EFFECT),
    )(*[pltpu.with_memory_space_constraint(a, pltpu.HBM) for a in bufs], *sems_in, *([] if after is None else [after]))
    return list(res[nn:nn + nb]), list(res[:nn]), (res[nn + nb][0:1, 0:1] if token else None)


def scatter_start(srcs, name, after=None):
    n = len(srcs)
    lands = [lax.empty(s.shape, s.dtype) for s in srcs]

    def body(b, taken, new):
        me_i = _me()
        for i in range(n):
            pltpu.make_async_copy(b[i].at[me_i], b[n + i].at[me_i], new[3 * i + 2].at[0]).start()
            for r in range(1, NDEV):
                _xchg_copy(b[i], b[n + i], new[3 * i], new[3 * i + 1], r, True, False).start()

    bufs, sems, tok = _phase(body, name, list(srcs) + lands, new_sems=[NDEV - 1, NDEV - 1, 1] * n, after=after, token=True)
    return (bufs, sems), tok


def scatter_wait(handle, after, name):
    bufs, sems = handle
    n = len(bufs) // 2

    def body(b, taken, new):
        me_i = _me()
        for i in range(n):
            pltpu.make_async_copy(b[i].at[me_i], b[n + i].at[me_i], taken[3 * i + 2].at[0]).wait()
            for r in range(1, NDEV):
                cp = _xchg_copy(b[i], b[n + i], taken[3 * i], taken[3 * i + 1], r, True, True)
                cp.wait_send()
                cp.wait_recv()

    return _phase(body, name, bufs, sems_in=sems, after=after)[0][n:]


def _gather_peers():
    x, y, c = _coords()
    return (x, y, c), (x, y, 1 - c), [(1 - x, y), (x, 1 - y), (1 - x, 1 - y)]


def _row(p):
    return 4 * p[0] + 2 * p[1] + p[2]


def _gcopy(src_ref, land_ref, send, recv, k, block, to):
    return pltpu.make_async_remote_copy(
        src_ref=land_ref.at[_row(block)] if src_ref is None else src_ref, dst_ref=land_ref.at[_row(block)],
        send_sem=send.at[k], recv_sem=recv.at[k], device_id=to, device_id_type=MESH)


def gather_start(srcs, name, after=None):
    n = len(srcs)
    lands = [lax.empty((NDEV,) + s.shape, s.dtype) for s in srcs]

    def body(b, taken, new):
        me, sib, chips = _gather_peers()
        for i in range(n):
            send, recv = new[3 * i], new[3 * i + 1]
            pltpu.make_async_copy(b[i], b[n + i].at[_row(me)], new[3 * i + 2].at[0]).start()
            for j, chip in enumerate(chips):
                _gcopy(b[i], b[n + i], send, recv, 1 + j, me, (*chip, me[2])).start()
            _gcopy(b[i], b[n + i], send, recv, 0, me, sib).start()

    bufs, sems, tok = _phase(body, name, list(srcs) + lands, new_sems=[4, 4, 1] * n, after=after, token=True)
    return (bufs, sems), tok


def gather_mid(handle, after, name):
    bufs, sems = handle
    n = len(bufs) // 2

    def body(b, taken, new):
        me, sib, chips = _gather_peers()
        for j, chip in enumerate(chips):
            for i in range(n):
                _gcopy(b[i], b[n + i], taken[3 * i], taken[3 * i + 1], 1 + j, (*chip, me[2]), me).wait_recv()
                _gcopy(None, b[n + i], new[2 * i], new[2 * i + 1], j, (*chip, me[2]), sib).start()
        for i in range(n):
            send, recv = taken[3 * i], taken[3 * i + 1]
            _gcopy(b[i], b[n + i], send, recv, 0, sib, me).wait_recv()
            for k in range(4):
                _gcopy(b[i], b[n + i], send, recv, k, me, sib).wait_send()
            pltpu.make_async_copy(b[i], b[n + i].at[_row(me)], taken[3 * i + 2].at[0]).wait()

    bufs, new, tok = _phase(body, name, bufs, sems_in=sems, new_sems=[3, 3] * n, after=after, token=True)
    return (bufs, new), tok


def gather_wait(handle, after, name):
    bufs, sems = handle
    n = len(bufs) // 2

    def body(b, taken, new):
        me, sib, chips = _gather_peers()
        for i in range(n):
            for j, chip in enumerate(chips):
                _gcopy(None, b[n + i], taken[2 * i], taken[2 * i + 1], j, (*chip, me[2]), sib).wait_send()
                _gcopy(None, b[n + i], taken[2 * i], taken[2 * i + 1], j, (*chip, 1 - me[2]), me).wait_recv()

    return _phase(body, name, bufs, sems_in=sems, after=after)[0][n:]


_PACK_ALIGN = 8 * LANES


def _pack(arrs, aligned=False):
    parts = []
    for a in arrs:
        f = a.reshape(-1).astype(F32)
        pad = (-f.shape[0]) % _PACK_ALIGN if aligned else 0
        parts.append(jnp.pad(f, (0, pad)) if pad else f)
    flat = jnp.concatenate(parts)
    pad = (-flat.shape[0]) % _PACK_ALIGN
    return (jnp.pad(flat, (0, pad)) if pad else flat).reshape(-1, LANES)


def _unpack(p, shapes, lead=(), aligned=False):
    nl = len(lead)
    flat = p.reshape(lead + (-1,))
    out, off = [], 0
    for shp in shapes:
        n = 1
        for d in shp:
            n *= d
        out.append(lax.slice_in_dim(flat, off, off + n, axis=nl).reshape(lead + tuple(shp)))
        off += n + ((-n) % _PACK_ALIGN if aligned else 0)
    return out


def _shard_cols(a, me, width, axis):
    return lax.dynamic_slice_in_dim(a, me * width, width, axis=axis)


def kernel(x, c, norm1_g, norm2_g, ada_w, ada_b, mlp_w1, mlp_w2, ab_w_in, sgu_norm_g, sgu_w, sgu_b, conv_w, conv_b, conv_ln_g, conv_ln_b, ab_w_out, mla_w_in, mla_q_norm_g, mla_kv_norm_g, mla_w_uq, mla_w_ukv, mla_q_head_g, mla_k_head_g, mla_w_out, loss_target, m_norm1_g, m_norm2_g, m_ada_w, m_ada_b, m_mlp_w1, m_mlp_w2, m_ab_w_in, m_sgu_norm_g, m_sgu_w, m_sgu_b, m_conv_w, m_conv_b, m_conv_ln_g, m_conv_ln_b, m_ab_w_out, m_mla_w_in, m_mla_q_norm_g, m_mla_kv_norm_g, m_mla_w_uq, m_mla_w_ukv, m_mla_q_head_g, m_mla_k_head_g, m_mla_w_out, v_norm1_g, v_norm2_g, v_ada_w, v_ada_b, v_mlp_w1, v_mlp_w2, v_ab_w_in, v_sgu_norm_g, v_sgu_w, v_sgu_b, v_conv_w, v_conv_b, v_conv_ln_g, v_conv_ln_b, v_ab_w_out, v_mla_w_in, v_mla_q_norm_g, v_mla_kv_norm_g, v_mla_w_uq, v_mla_w_ukv, v_mla_q_head_g, v_mla_k_head_g, v_mla_w_out):
    W = dict(norm1_g=norm1_g, norm2_g=norm2_g, ada_w=ada_w, ada_b=ada_b, mlp_w1=mlp_w1, mlp_w2=mlp_w2, ab_w_in=ab_w_in,
             sgu_norm_g=sgu_norm_g, sgu_w=sgu_w, sgu_b=sgu_b, conv_w=conv_w, conv_b=conv_b, conv_ln_g=conv_ln_g,
             conv_ln_b=conv_ln_b, ab_w_out=ab_w_out, mla_w_in=mla_w_in, mla_q_norm_g=mla_q_norm_g,
             mla_kv_norm_g=mla_kv_norm_g, mla_w_uq=mla_w_uq, mla_w_ukv=mla_w_ukv, mla_q_head_g=mla_q_head_g,
             mla_k_head_g=mla_k_head_g, mla_w_out=mla_w_out)
    M = dict(norm1_g=m_norm1_g, norm2_g=m_norm2_g, ada_w=m_ada_w, ada_b=m_ada_b, mlp_w1=m_mlp_w1, mlp_w2=m_mlp_w2,
             ab_w_in=m_ab_w_in, sgu_norm_g=m_sgu_norm_g, sgu_w=m_sgu_w, sgu_b=m_sgu_b, conv_w=m_conv_w, conv_b=m_conv_b,
             conv_ln_g=m_conv_ln_g, conv_ln_b=m_conv_ln_b, ab_w_out=m_ab_w_out, mla_w_in=m_mla_w_in,
             mla_q_norm_g=m_mla_q_norm_g, mla_kv_norm_g=m_mla_kv_norm_g, mla_w_uq=m_mla_w_uq, mla_w_ukv=m_mla_w_ukv,
             mla_q_head_g=m_mla_q_head_g, mla_k_head_g=m_mla_k_head_g, mla_w_out=m_mla_w_out)
    V = dict(norm1_g=v_norm1_g, norm2_g=v_norm2_g, ada_w=v_ada_w, ada_b=v_ada_b, mlp_w1=v_mlp_w1, mlp_w2=v_mlp_w2,
             ab_w_in=v_ab_w_in, sgu_norm_g=v_sgu_norm_g, sgu_w=v_sgu_w, sgu_b=v_sgu_b, conv_w=v_conv_w, conv_b=v_conv_b,
             conv_ln_g=v_conv_ln_g, conv_ln_b=v_conv_ln_b, ab_w_out=v_ab_w_out, mla_w_in=v_mla_w_in,
             mla_q_norm_g=v_mla_q_norm_g, mla_kv_norm_g=v_mla_kv_norm_g, mla_w_uq=v_mla_w_uq, mla_w_ukv=v_mla_w_ukv,
             mla_q_head_g=v_mla_q_head_g, mla_k_head_g=v_mla_k_head_g, mla_w_out=v_mla_w_out)
    ORDER = list(W)

    S, D = x.shape[1], x.shape[2]
    L, NE, NO = norm1_g.shape[0], ab_w_in.shape[0], mla_w_in.shape[0]
    DA = D // 2
    DB = D - DA
    G = DA // GROUP
    R = NDEV * mla_q_norm_g.shape[1]
    H = NDEV * mla_w_uq.shape[2] // QK
    AW = ada_w.shape[2]
    CB = conv_w.shape[2]
    me = _me()
    xs, tgt = x[0], loss_target[0]

    BIG_EVEN = ("mlp_w1", "mlp_w2", "ab_w_in", "ab_w_out")
    BIG_ODD = ("mlp_w1", "mlp_w2", "mla_w_in", "mla_w_uq", "mla_w_ukv", "mla_w_out")
    BIG = ("mlp_w1", "mlp_w2", "ab_w_in", "ab_w_out", "mla_w_in", "mla_w_uq", "mla_w_ukv", "mla_w_out")
    COL_SHARDED = ("mlp_w1", "ab_w_in", "mla_w_uq", "mla_w_ukv")
    MLP_W = ("mlp_w1", "mlp_w2")
    mixer_w = lambda l: ("ab_w_in", "ab_w_out") if l % 2 == 0 else ("mla_w_in", "mla_w_uq", "mla_w_ukv", "mla_w_out")
    widx = lambda k, l: l if k in MLP_W else l // 2

    small_in = [c, mla_q_norm_g, mla_kv_norm_g, conv_w]
    sg = all_gather([(_pack(small_in, True), None)], "gather_small")[0]
    c_all, qng_all, kvng_all, cw_all = _unpack(sg, [a.shape for a in small_in], (NDEV,), True)
    c_all = c_all.reshape(NDEV, D)
    qng_full = jnp.transpose(qng_all, (1, 0, 2)).reshape(NO, 1, R)
    kvng_full = jnp.transpose(kvng_all, (1, 0, 2)).reshape(NO, 1, R)
    cw_full = jnp.transpose(cw_all, (1, 2, 0, 3)).reshape(NE, CONV_W, DB)
    cw_pad = jnp.pad(cw_full, ((0, 0), (0, CONV_PAD - CONV_W), (0, 0)))

    def silu_fn(t, c_, o):
        v_ = t[0][...]
        o[0][...] = v_ * _sigmoid(v_)
        return []

    c_act = rowwise(silu_fn, [c_all], outs=[((NDEV, D), F32)], ts=NDEV, name="silu_c")[0]
    bias_cols = _shard_cols(ada_b, me, AW, 1).reshape(1, L * AW)
    mod_cols = mm(c_act, ada_w, "nn", name="ada_fwd", b_blocked=True, rowvecs=[bias_cols],
                  epi=lambda acc, b_: (acc + b_,), tm=NDEV, tn=768)
    mod_all = all_gather([(mod_cols, None)], "gather_mod")[0]
    mod = lax.dynamic_index_in_dim(mod_all, me, axis=1, keepdims=False)
    mod = jnp.transpose(mod.reshape(NDEV, L, AW), (1, 0, 2)).reshape(L, 6, 1, D)

    wnames = lambda l, what: mixer_w(l) if what == "mix" else MLP_W
    g_first, g_second = {}, {}
    tok_sum = jnp.zeros((1, 1), F32)
    for l in range(L):
        for what in ("mix", "mlp"):
            srcs = [cast_bf16(W[k], widx(k, l), "cast_%s_l%d" % (k, l)) for k in wnames(l, what)]
            g_first[l, what], tok = gather_start(srcs, "gather_start_%s_l%d" % (what, l), after=mod_all)
            tok_sum = tok_sum + tok
    mod = mod + tok_sum

    def pass_on(l, what, after):
        g_second[l, what], tok = gather_mid(g_first[l, what], after, "gather_mid_%s_l%d" % (what, l))
        return tok

    def wait_weights(l, what, after):
        lands = gather_wait(g_second[l, what], after, "gather_wait_%s_l%d" % (what, l))
        return {k: (ld if k in COL_SHARDED else ld.reshape(NDEV * ld.shape[1], ld.shape[2]))
                for k, ld in zip(wnames(l, what), lands)}

    mod = mod + pass_on(0, "mix", mod)

    pos = jnp.arange(S, dtype=F32)
    inv = ROPE_THETA ** (-jnp.arange(0, ROPE, 2, dtype=F32) / ROPE)
    ang = pos[:, None] * inv[None, :]
    cos2 = jnp.concatenate([jnp.cos(ang), jnp.cos(ang)], axis=1)
    sinm = jnp.concatenate([-jnp.sin(ang), jnp.sin(ang)], axis=1)

    residual = lambda acc, xr, gt: (acc, xr + gt * acc)

    saved = []
    xc = xs
    for l in range(L):
        sh1, sc1, g1, sh2, sc2, g2 = [mod[l, k] for k in range(6)]
        tag = "_l%d" % l
        sv = dict(x0=xc)
        h, sv["rstd1"] = prenorm(xc, norm1_g[l][None], sc1, sh1, "prenorm1" + tag)
        sv["h"] = h
        wl = wait_weights(l, "mix", h)
        if l % 2 == 0:
            e = l // 2
            ng = sgu_norm_g[e].reshape(1, DA)
            bcol = sgu_b[e][:, :, None]
            proj = mm(h, wl["ab_w_in"], "nn", name="ab_in" + tag, b_blocked=True)
            out_a = sgu_fwd(proj, ng, sgu_w[e], bcol, DA, "sgu_fwd" + tag)
            y1 = conv_fwd(proj, cw_pad[e], conv_b[e][None], DA, DB, "conv_fwd" + tag)
            out_b = ln_silu(y1, conv_ln_g[e][None], conv_ln_b[e][None], "ln_silu" + tag)
            cat = jnp.concatenate([out_a, out_b], axis=1)
            sv.update(proj=proj, y1=y1, cat=cat)
            mixb, x1 = mm(cat, wl["ab_w_out"], "nn", name="ab_out" + tag, out_dtypes=(BF16, F32), epi=residual,
                          extras=[xc], rowvecs=[g1])
        else:
            o_ = l // 2
            proj = mm(h, wl["mla_w_in"], "nn", name="mla_in" + tag)
            kgr = mla_k_head_g[o_][None, NOPE:QK]
            qn, kvn, kr = mla_norms(proj, cos2, sinm, qng_full[o_], kvng_full[o_], kgr, R, "mla_norms" + tag)
            q = mm(qn, wl["mla_w_uq"], "nn", name="mla_uq" + tag, b_blocked=True)
            kv = mm(kvn, wl["mla_w_ukv"], "nn", name="mla_ukv" + tag, b_blocked=True)
            Qh, Kh, Vh = mla_heads(q, kv, kr, cos2, sinm, mla_q_head_g[o_][None], mla_k_head_g[o_][None], H,
                                   "mla_heads" + tag)
            att, lse = attn_fwd(Qh, Kh, Vh, "attn_fwd" + tag)
            sv.update(proj=proj, qn=qn, kvn=kvn, q=q, kv=kv, Qh=Qh, Kh=Kh, Vh=Vh, att=att, lse=lse)
            mixb, x1 = mm(att, wl["mla_w_out"], "nn", name="mla_out" + tag, out_dtypes=(BF16, F32), epi=residual,
                          extras=[xc], rowvecs=[g1])
        sv.update(mixb=mixb, x1=x1)
        tok = pass_on(0, "mlp", x1) if l == 0 else 0.0
        if l + 1 < L:
            tok = tok + pass_on(l + 1, "mix", x1) + pass_on(l + 1, "mlp", x1)
        h2, sv["rstd2"] = prenorm(x1, norm2_g[l][None], sc2 + tok, sh2, "prenorm2" + tag)
        wl.update(wait_weights(l, "mlp", h2))
        sv["w"] = wl
        z, act = mm(h2, wl["mlp_w1"], "nn", name="mlp_up" + tag, b_blocked=True, out_dtypes=(BF16, BF16),
                    epi=lambda acc: (acc, jnp.square(jnp.maximum(acc, 0.0))))
        yb, xc = mm(act, wl["mlp_w2"], "nn", name="mlp_down" + tag, out_dtypes=(BF16, F32), epi=residual,
                    extras=[x1], rowvecs=[g2])
        sv.update(h2=h2, z=z, act=act, yb=yb)
        saved.append(sv)

    dx, loss_cols = loss_grad(xc, tgt, "loss")
    loss = lax.psum(0.5 / D * jnp.sum(loss_cols), ("x", "y", "c"))

    big_out = {k: [None] * W[k].shape[0] for k in BIG}
    sm = {k: [None] * W[k].shape[0] for k in ("norm1_g", "norm2_g", "sgu_norm_g", "sgu_w", "sgu_b", "conv_b", "conv_ln_g",
                                               "conv_ln_b", "mla_q_head_g", "mla_k_head_g", "mla_q_norm_g",
                                               "mla_kv_norm_g", "conv_w")}
    dmod = [None] * L
    flat2 = {k: W[k].reshape(-1, W[k].shape[2]) for k in BIG}
    flat2m = {k: M[k].reshape(-1, W[k].shape[2]) for k in BIG}
    flat2v = {k: V[k].reshape(-1, W[k].shape[2]) for k in BIG}

    def send_grads(gr, what, tag, after=None):
        names = list(gr)
        blocks = [gr[k] if k in COL_SHARDED else gr[k].reshape(NDEV, gr[k].shape[0] // NDEV, gr[k].shape[1])
                  for k in names]
        handle, tok = scatter_start(blocks, "scatter_start_%s%s" % (what, tag), after=after)
        return (names, handle, what, tag), tok

    rep = ("norm1_g", "norm2_g", "sgu_norm_g", "sgu_w", "sgu_b", "conv_b", "conv_ln_g", "conv_ln_b", "mla_q_head_g",
           "mla_k_head_g")
    part_full = {"mla_q_norm_g": (NO, R), "mla_kv_norm_g": (NO, R), "conv_w": (NE, CONV_W, DB)}
    small = ["ada_b"] + list(rep) + list(part_full)
    shapes = [(L, 6 * D)] + [W[k].shape for k in rep] + list(part_full.values())

    def small_gather():
        parts = [jnp.stack(dmod).reshape(L, 6 * D)] + [jnp.stack(sm[k]).reshape(s_) for k, s_ in zip(small[1:], shapes[1:])]
        return all_gather([(_pack(parts), None)], "gather_smallgrads")[0]

    def scatter_finish(pending, after, l):
        names, handle, what, tag = pending
        landed = scatter_wait(handle, after, "scatter_wait_%s%s" % (what, tag))
        for k, land in zip(names, landed):
            li = widx(k, l)
            big_out[k][li] = adamw(flat2[k], flat2m[k], flat2v[k], land, row0=li * W[k].shape[1],
                                   name="adamw_%s%s" % (k, tag))

    pend_mix, tok_mix = None, None
    for l in reversed(range(L)):
        sh1, sc1, g1, sh2, sc2, g2 = [mod[l, k] for k in range(6)]
        if tok_mix is not None:
            g2 = g2 + tok_mix
        sv = saved[l]
        wl = sv["w"]
        tag = "_l%d" % l
        gr = {}
        dy, dgate2 = gate_bwd(dx, sv["yb"], g2, "gate2_bwd" + tag)
        dz = mm(dy, wl["mlp_w2"], "nt", name="mlp_down_dx" + tag, out_dtypes=(BF16,), extras=[sv["z"]],
                epi=lambda acc, z_: (acc * (2.0 * jnp.maximum(z_.astype(F32), 0.0)),))
        gr["mlp_w2"] = mm(sv["act"], dy, "tn", name="mlp_down_dw" + tag, out_dtypes=(BF16,))
        dh2 = mm(dz, wl["mlp_w1"], "nt", name="mlp_up_dx" + tag, b_blocked=True)
        gr["mlp_w1"] = mm(sv["h2"], dz, "tn", name="mlp_up_dw" + tag, out_dtypes=(BF16,), out_blocked=NDEV)
        dx1, dsc2, dsh2, sm["norm2_g"][l] = norm_bwd(dh2, sv["x1"], sv["rstd2"], dx, norm2_g[l][None], sc2,
                                                     "norm2_bwd" + tag)
        pend_mlp, tok_mlp = send_grads(gr, "mlp", tag)
        gr = {}
        if pend_mix is not None:
            scatter_finish(pend_mix, dx1, l + 1)
        dmix, dgate1 = gate_bwd(dx1, sv["mixb"], g1 + tok_mlp, "gate1_bwd" + tag)
        if l % 2 == 0:
            e = l // 2
            ng = sgu_norm_g[e].reshape(1, DA)
            bcol = sgu_b[e][:, :, None]
            dcat = mm(dmix, wl["ab_w_out"], "nt", name="ab_out_dx" + tag)
            gr["ab_w_out"] = mm(sv["cat"], dmix, "tn", name="ab_out_dw" + tag, out_dtypes=(BF16,))
            dy1, sm["conv_ln_g"][e], sm["conv_ln_b"][e], sm["conv_b"][e] = ln_silu_bwd(
                dcat, sv["y1"], conv_ln_g[e][None], conv_ln_b[e][None], "ln_silu_bwd" + tag)
            da, dg_, dwc = conv_bwd(sv["proj"], dy1, cw_pad[e], DA, DB, "conv_bwd" + tag)
            sm["conv_w"][e] = dwc[:CONV_W]
            duv, dsw, dsb, dsng = sgu_bwd(sv["proj"], dcat, ng, sgu_w[e], bcol, DA, "sgu_bwd" + tag)
            sm["sgu_w"][e], sm["sgu_b"][e], sm["sgu_norm_g"][e] = dsw, dsb, dsng
            dproj = jnp.concatenate([duv, da, dg_], axis=1)
            dh = mm(dproj, wl["ab_w_in"], "nt", name="ab_in_dx" + tag, b_blocked=True)
            gr["ab_w_in"] = mm(sv["h"], dproj, "tn", name="ab_in_dw" + tag, out_dtypes=(BF16,), out_blocked=NDEV)
        else:
            o_ = l // 2
            kgr = mla_k_head_g[o_][None, NOPE:QK]
            dO = mm(dmix, wl["mla_w_out"], "nt", name="mla_out_dx" + tag, out_dtypes=(BF16,))
            gr["mla_w_out"] = mm(sv["att"], dmix, "tn", name="mla_out_dw" + tag, out_dtypes=(BF16,))
            dQ, dK, dV = attn_bwd(sv["Qh"], sv["Kh"], sv["Vh"], dO, sv["att"], sv["lse"], "attn_bwd" + tag)
            dq_pre, dkv_pre, dkr, dqgn, dqgr, dkgn = mla_heads_bwd(
                dQ, dK, dV, sv["q"], sv["kv"], cos2, sinm, mla_q_head_g[o_][None], mla_k_head_g[o_][None], H,
                "mla_heads_bwd" + tag)
            dqn = mm(dq_pre, wl["mla_w_uq"], "nt", name="mla_uq_dx" + tag, b_blocked=True)
            gr["mla_w_uq"] = mm(sv["qn"], dq_pre, "tn", name="mla_uq_dw" + tag, out_dtypes=(BF16,), out_blocked=NDEV)
            dkvn = mm(dkv_pre, wl["mla_w_ukv"], "nt", name="mla_ukv_dx" + tag, b_blocked=True)
            gr["mla_w_ukv"] = mm(sv["kvn"], dkv_pre, "tn", name="mla_ukv_dw" + tag, out_dtypes=(BF16,),
                                 out_blocked=NDEV)
            dproj, sm["mla_q_norm_g"][o_], sm["mla_kv_norm_g"][o_], dkgr = mla_norms_bwd(
                dqn, dkvn, dkr, sv["proj"], qng_full[o_], kvng_full[o_], kgr, R, "mla_norms_bwd" + tag)
            sm["mla_q_head_g"][o_] = jnp.concatenate([dqgn, dqgr], axis=1)
            sm["mla_k_head_g"][o_] = jnp.concatenate([dkgn, dkgr], axis=1)
            dh = mm(dproj, wl["mla_w_in"], "nt", name="mla_in_dx" + tag)
            gr["mla_w_in"] = mm(sv["h"], dproj, "tn", name="mla_in_dw" + tag, out_dtypes=(BF16,))
        dx, dsc1, dsh1, sm["norm1_g"][l] = norm_bwd(dh, sv["x0"], sv["rstd1"], dx1, norm1_g[l][None], sc1,
                                                    "norm1_bwd" + tag)
        dmod[l] = jnp.concatenate([dsh1, dsc1, dgate1, dsh2, dsc2, dgate2], axis=1)

        gp = small_gather() if l == 0 else None
        pend_mix, tok_mix = send_grads(gr, "mix", tag, after=gp)
        scatter_finish(pend_mlp, dx, l)

    rows_p = gp.shape[1]
    per = 6 * D // LANES
    dm = lax.slice_in_dim(gp, 0, L * per, axis=1).reshape(NDEV, L, per, LANES)
    dmod_cols = lax.dynamic_slice_in_dim(dm, me * (AW // LANES), AW // LANES, axis=2).reshape(NDEV, L * AW) + tok_mix

    def sum_fn(t, c_, o):
        acc = t[0][0]
        for s_ in range(1, NDEV):
            acc = acc + t[0][s_]
        o[0][...] = acc
        return []

    gsummed = rowwise(sum_fn, [gp], outs=[((rows_p, LANES), F32)], ts=_tile(rows_p, 256, 8), name="sum_smallgrads")[0]
    gsum = dict(zip(small, _unpack(gsummed, shapes)))
    gsum["mla_q_norm_g"] = _shard_cols(gsum["mla_q_norm_g"], me, R // NDEV, 1)
    gsum["mla_kv_norm_g"] = _shard_cols(gsum["mla_kv_norm_g"], me, R // NDEV, 1)
    gsum["conv_w"] = _shard_cols(gsum["conv_w"], me, CB, 2)
    sm_shapes = [W[k].shape for k in small]
    sres = adamw(_pack([W[k] for k in small]), _pack([M[k] for k in small]), _pack([V[k] for k in small]),
                 (_pack([gsum[k] for k in small]) + tok_mix)[None], row0=0, name="adamw_small")
    small_out = {k: vals for k, vals in zip(small, zip(*[_unpack(r_, sm_shapes) for r_ in sres]))}

    g_ada = mm(c_act, dmod_cols, "tn", name="ada_dw", out_blocked=L, tm=1024, tn=768, tk=NDEV, cast=None,
               precision=lax.Precision.HIGHEST)
    ada_out = adamw(ada_w.reshape(L * D, AW), m_ada_w.reshape(L * D, AW), v_ada_w.reshape(L * D, AW),
                    g_ada.reshape(1, L * D, AW), row0=0, name="adamw_ada_w")
    scatter_finish(pend_mix, ada_out[1], 0)
    ada_out = [a.reshape(L, D, AW) for a in ada_out]

    def result(k, which):
        if k == "ada_w":
            return ada_out[which]
        if k in BIG:
            return jnp.stack([r_[which] for r_ in big_out[k]]).reshape(W[k].shape)
        return small_out[k][which]

    outs = [loss, dx[None]]
    for which in range(4):
        outs += [result(k, which) for k in ORDER]
    return tuple(outs)
```

```python
import functools

import jax
import jax.numpy as jnp
from jax import lax
from jax.experimental import pallas as pl
from jax.experimental.pallas import tpu as pltpu

F32 = jnp.float32
BF16 = jnp.bfloat16
EPS = 1e-6
NDEV = 8
LANES = 128
CHUNK = 128
GROUP = 128
CONV_W = 31
CONV_PAD = 32
NOPE, ROPE, VDIM = 128, 64, 128
QK = NOPE + ROPE
ROPE_THETA = 10000.0
VMEM_LIMIT = 56 * 1024 * 1024
ADAM_LR, ADAM_B1, ADAM_B2, ADAM_EPS, ADAM_WD, ADAM_STEP = 0.001, 0.9, 0.999, 1e-08, 0.01, 10
MESH = pl.DeviceIdType.MESH
NEG = -1e30
ATTN_BLOCK = 1024
ATTN_STRIP = 64


def _pcall(body, **kw):
    return pl.pallas_call(body, **kw)


def _params(sem=None):
    return pltpu.CompilerParams(dimension_semantics=sem, vmem_limit_bytes=VMEM_LIMIT)


def _tile(dim, target, align=LANES):
    if dim <= target:
        return dim
    t = (target // align) * align
    while t >= align:
        if dim % t == 0:
            return t
        t -= align
    return dim


def _rstd(x):
    return lax.rsqrt(jnp.mean(x * x, axis=-1, keepdims=True) + EPS)


def _sigmoid(x):
    return 1.0 / (1.0 + jnp.exp(-x))


_GC = 0.7978845608028654


def _gelu(x):
    return 0.5 * x * (1.0 + jnp.tanh(_GC * (x + 0.044715 * x * x * x)))


def _gelu_grad(x):
    t = jnp.tanh(_GC * (x + 0.044715 * x * x * x))
    return 0.5 * (1.0 + t) + 0.5 * x * (1.0 - t * t) * _GC * (1.0 + 3 * 0.044715 * x * x)


def _colsum(x):
    return jnp.sum(x, axis=0, keepdims=True)


def _rms_bwd(dy, xhat, rstd, g):
    dxh = dy * g
    dx = rstd * (dxh - xhat * jnp.mean(dxh * xhat, axis=-1, keepdims=True))
    return dx, _colsum(dy * xhat)


def _swap_halves(x):
    h = x.shape[-1] // 2
    return jnp.concatenate([x[:, h:], x[:, :h]], axis=1)


def _rope(x, cos2, sinm):
    return x * cos2 + _swap_halves(x) * sinm


def _unrope(dy, cos2, sinm):
    return dy * cos2 + _swap_halves(dy * sinm)


_DIMS = {"nn": (((1,), (0,)), ((), ())), "nt": (((1,), (1,)), ((), ())), "tn": (((0,), (0,)), ((), ()))}


def mm(a, b, mode, *, name, out_dtypes=(F32,), epi=None, extras=(), rowvecs=(), b_blocked=False, out_blocked=0,
       tm=1024, tn=1024, tk=2048, precision=None, cast=BF16):
    if mode == "tn":
        K, M = a.shape
    else:
        M, K = a.shape
    if b_blocked:
        J, Rb, Cb = b.shape
        N = Rb if mode == "nt" else J * Cb
    else:
        N = b.shape[0] if mode == "nt" else b.shape[1]
    tm = _tile(M, tm)
    if mode == "nn" and b_blocked:
        tn = _tile(Cb, tn)
    elif out_blocked:
        tn = _tile(N // out_blocked, tn)
    else:
        tn = _tile(N, tn)
    kb = 1
    if mode == "nt" and b_blocked:
        if Cb >= tk:
            tk = _tile(Cb, tk)
        else:
            kb = max(d for d in range(1, J + 1) if J % d == 0 and d * Cb <= tk)
            tk = kb * Cb
    else:
        tk = _tile(K, tk)
    nk = K // tk
    grid = (M // tm, N // tn, nk)

    if mode == "tn":
        a_spec = pl.BlockSpec((tk, tm), lambda i, j, k: (k, i))
    else:
        a_spec = pl.BlockSpec((tm, tk), lambda i, j, k: (i, k))
    if mode == "nn":
        if b_blocked:
            nper = Cb // tn
            b_spec = pl.BlockSpec((None, tk, tn), lambda i, j, k: (j // nper, k, j % nper))
        else:
            b_spec = pl.BlockSpec((tk, tn), lambda i, j, k: (k, j))
    elif mode == "nt":
        if b_blocked:
            if kb > 1:
                b_spec = pl.BlockSpec((kb, tn, Cb), lambda i, j, k: (k, j, 0))
            else:
                kper = Cb // tk
                b_spec = pl.BlockSpec((None, tn, tk), lambda i, j, k: (k // kper, j, k % kper))
        else:
            b_spec = pl.BlockSpec((tn, tk), lambda i, j, k: (j, k))
    else:
        b_spec = pl.BlockSpec((tk, tn), lambda i, j, k: (k, j))
    if out_blocked:
        oper = (N // out_blocked) // tn
        o_spec = pl.BlockSpec((None, tm, tn), lambda i, j, k: (j // oper, i, j % oper))
        o_shape = (out_blocked, M, N // out_blocked)
    else:
        o_spec = pl.BlockSpec((tm, tn), lambda i, j, k: (i, j))
        o_shape = (M, N)
    e_spec = pl.BlockSpec((tm, tn), lambda i, j, k: (i, j))
    r_spec = pl.BlockSpec((1, tn), lambda i, j, k: (0, j))
    ne, nr, no = len(extras), len(rowvecs), len(out_dtypes)
    dims = _DIMS[mode]

    def body(a_ref, b_ref, *rest):
        ex = rest[:ne]
        rv = rest[ne:ne + nr]
        outs = rest[ne + nr:ne + nr + no]

        def product():
            if kb > 1:
                r = None
                for q in range(kb):
                    av, bv = a_ref[:, q * Cb:(q + 1) * Cb], b_ref[q]
                    if cast is not None:
                        av, bv = av.astype(cast), bv.astype(cast)
                    d = lax.dot_general(av, bv, dims, preferred_element_type=F32, precision=precision)
                    r = d if r is None else r + d
                return r
            av, bv = a_ref[...], b_ref[...]
            if cast is not None:
                av, bv = av.astype(cast), bv.astype(cast)
            return lax.dot_general(av, bv, dims, preferred_element_type=F32, precision=precision)

        def finish(r):
            vals = (r,) if epi is None else epi(r, *[e[...] for e in ex], *[v[...] for v in rv])
            for o, val in zip(outs, vals):
                o[...] = val.astype(o.dtype)

        if nk == 1:
            finish(product())
            return
        acc = rest[ne + nr + no]
        k = pl.program_id(2)

        @pl.when(k == 0)
        def _():
            acc[...] = product()

        @pl.when((k > 0) & (k < nk - 1))
        def _():
            acc[...] += product()

        @pl.when(k == nk - 1)
        def _():
            finish(acc[...] + product())

    res = _pcall(
        body, name=name, grid=grid,
        in_specs=[a_spec, b_spec] + [e_spec] * ne + [r_spec] * nr,
        out_specs=[o_spec] * no,
        out_shape=[jax.ShapeDtypeStruct(o_shape, dt) for dt in out_dtypes],
        scratch_shapes=[] if nk == 1 else [pltpu.VMEM((tm, tn), F32)],
        compiler_params=_params(("parallel", "parallel", "arbitrary")),
    )(a, b, *extras, *rowvecs)
    return res[0] if no == 1 else res


def rowwise(fn, tiled, consts=(), outs=(), reds=(), *, ts, name):
    specs = []
    arrs = []
    rows = None
    for t in tiled:
        a, w, cb = t if isinstance(t, tuple) else (t, None, 0)
        arrs.append(a)
        rows = a.shape[-2] if rows is None else rows
        if a.ndim == 2:
            specs.append(pl.BlockSpec((ts, a.shape[1] if w is None else w), lambda i, cb=cb: (i, cb)))
        else:
            specs.append(pl.BlockSpec((a.shape[0], ts, a.shape[2]), lambda i: (0, i, 0)))
    for a in consts:
        specs.append(pl.BlockSpec(a.shape, lambda i, n=a.ndim: (0,) * n))
    o_specs, o_shapes = [], []
    for shp, dt in outs:
        if len(shp) == 2:
            o_specs.append(pl.BlockSpec((ts, shp[1]), lambda i: (i, 0)))
        else:
            o_specs.append(pl.BlockSpec((shp[0], ts, shp[2]), lambda i: (0, i, 0)))
        o_shapes.append(jax.ShapeDtypeStruct(shp, dt))
    for shp in reds:
        o_specs.append(pl.BlockSpec(shp, lambda i, n=len(shp): (0,) * n))
        o_shapes.append(jax.ShapeDtypeStruct(shp, F32))
    nt, nc, no = len(arrs), len(consts), len(outs)

    def body(*refs):
        i = pl.program_id(0)
        red_refs = refs[nt + nc + no:]
        vals = fn(refs[:nt], refs[nt:nt + nc], refs[nt + nc:nt + nc + no])
        if red_refs:
            @pl.when(i == 0)
            def _():
                for r in red_refs:
                    r[...] = jnp.zeros_like(r)
            for r, v in zip(red_refs, vals):
                r[...] += v

    return _pcall(body, name=name, grid=(rows // ts,), in_specs=specs, out_specs=o_specs, out_shape=o_shapes,
                  compiler_params=_params(("arbitrary",)))(*arrs, *consts)


def cast_bf16(w, l, name):
    _, R, C = w.shape
    tr = _tile(R, 512, 16)

    def body(w_ref, o_ref):
        o_ref[...] = w_ref[...].astype(BF16)

    return _pcall(body, name=name, grid=(R // tr,), in_specs=[pl.BlockSpec((None, tr, C), lambda i: (l, i, 0))],
                  out_specs=pl.BlockSpec((tr, C), lambda i: (i, 0)), out_shape=jax.ShapeDtypeStruct((R, C), BF16),
                  compiler_params=_params(("parallel",)))(w)


def prenorm(x, g, scale, shift, name):
    S, D = x.shape

    def fn(t, c, o):
        xv = t[0][...]
        r = _rstd(xv)
        o[0][...] = ((xv * r * c[0][...]) * (1.0 + c[1][...]) + c[2][...]).astype(BF16)
        o[1][...] = r
        return []

    return rowwise(fn, [x], [g, scale, shift], [((S, D), BF16), ((S, 1), F32)], ts=_tile(S, 128, 16), name=name)


def gate_bwd(dx, y, gate, name):
    S, D = dx.shape

    def fn(t, c, o):
        d = t[0][...]
        o[0][...] = (d * c[0][...]).astype(BF16)
        return [_colsum(d * t[1][...].astype(F32))]

    return rowwise(fn, [dx, y], [gate], [((S, D), BF16)], [(1, D)], ts=_tile(S, 128, 16), name=name)


def norm_bwd(dh, x, rstd, dres, g, scale, name):
    S, D = x.shape

    def fn(t, c, o):
        d = t[0][...]
        r = t[2][...]
        xh = t[1][...] * r
        gv = c[0][...]
        dr = d * (1.0 + c[1][...])
        dx, dg = _rms_bwd(dr, xh, r, gv)
        o[0][...] = t[3][...] + dx
        return [_colsum(d * (xh * gv)), _colsum(d), dg]

    return rowwise(fn, [dh, x, rstd, dres], [g, scale], [((S, D), F32)], [(1, D)] * 3, ts=_tile(S, 128, 8), name=name)


def loss_grad(y, tgt, name):
    S, D = y.shape

    def fn(t, c, o):
        e = t[0][...] - t[1][...]
        o[0][...] = e * (1.0 / D)
        return [_colsum(e * e)]

    return rowwise(fn, [y, tgt], outs=[((S, D), F32)], reds=[(1, D)], ts=_tile(S, 128, 8), name=name)


def _tril_mask():
    r = lax.broadcasted_iota(jnp.int32, (CHUNK, CHUNK), 0)
    c = lax.broadcasted_iota(jnp.int32, (CHUNK, CHUNK), 1)
    return c <= r


def sgu_fwd(proj, ng, w, bcol, DA, name):
    S = proj.shape[0]
    G = DA // GROUP
    tr = _tile(S, 2 * CHUNK)

    def body(u_ref, v_ref, ng_ref, w_ref, b_ref, o_ref):
        mask = _tril_mask()
        for g in range(G):
            cols = slice(g * GROUP, (g + 1) * GROUP)
            wm = jnp.where(mask, w_ref[g], 0.0).astype(BF16)
            for ci in range(tr // CHUNK):
                rows = slice(ci * CHUNK, (ci + 1) * CHUNK)
                gv = _gelu(v_ref[rows, cols])
                vn = gv * _rstd(gv) * ng_ref[:, cols]
                mixed = jnp.dot(wm, vn.astype(BF16), preferred_element_type=F32) + b_ref[g]
                o_ref[rows, cols] = (_gelu(u_ref[rows, cols]) * mixed).astype(o_ref.dtype)

    return _pcall(
        body, name=name, grid=(S // tr,),
        in_specs=[pl.BlockSpec((tr, DA), lambda i: (i, 0)), pl.BlockSpec((tr, DA), lambda i: (i, 1)),
                  pl.BlockSpec((1, DA), lambda i: (0, 0)), pl.BlockSpec((G, CHUNK, CHUNK), lambda i: (0, 0, 0)),
                  pl.BlockSpec((G, CHUNK, 1), lambda i: (0, 0, 0))],
        out_specs=pl.BlockSpec((tr, DA), lambda i: (i, 0)),
        out_shape=jax.ShapeDtypeStruct((S, DA), BF16),
        compiler_params=_params(("parallel",)),
    )(proj, proj, ng, w, bcol)


def sgu_bwd(proj, dcat, ng, w, bcol, DA, name):
    S = proj.shape[0]
    G = DA // GROUP
    tr = _tile(S, 2 * CHUNK)
    nsteps = S // tr

    def body(u_ref, v_ref, d_ref, ng_ref, w_ref, b_ref, duv_ref, dw_ref, db_ref, dng_ref, dbacc):
        i = pl.program_id(0)

        @pl.when(i == 0)
        def _():
            dw_ref[...] = jnp.zeros_like(dw_ref)
            dng_ref[...] = jnp.zeros_like(dng_ref)
            dbacc[...] = jnp.zeros_like(dbacc)

        mask = _tril_mask()
        for g in range(G):
            cols = slice(g * GROUP, (g + 1) * GROUP)
            wm = jnp.where(mask, w_ref[g], 0.0).astype(BF16)
            ngg = ng_ref[:, cols]
            for ci in range(tr // CHUNK):
                rows = slice(ci * CHUNK, (ci + 1) * CHUNK)
                u, v, d = u_ref[rows, cols], v_ref[rows, cols], d_ref[rows, cols]
                gv = _gelu(v)
                rs = _rstd(gv)
                vhat = gv * rs
                vn = (vhat * ngg).astype(BF16)
                mixed = jnp.dot(wm, vn, preferred_element_type=F32) + b_ref[g]
                dmixed = d * _gelu(u)
                dmb = dmixed.astype(BF16)
                duv_ref[rows, cols] = (d * mixed * _gelu_grad(u)).astype(duv_ref.dtype)
                dwg = lax.dot_general(dmb, vn, _DIMS["nt"], preferred_element_type=F32)
                dw_ref[g] += jnp.where(mask, dwg, 0.0)
                dbacc[g] += dmixed
                dvn = lax.dot_general(wm, dmb, _DIMS["tn"], preferred_element_type=F32)
                dgv, dngg = _rms_bwd(dvn, vhat, rs, ngg)
                dng_ref[:, cols] += dngg
                duv_ref[rows, DA + g * GROUP:DA + (g + 1) * GROUP] = (dgv * _gelu_grad(v)).astype(duv_ref.dtype)

        @pl.when(i == nsteps - 1)
        def _():
            for g in range(G):
                db_ref[g] = jnp.sum(dbacc[g], axis=-1, keepdims=True)

    return _pcall(
        body, name=name, grid=(nsteps,),
        in_specs=[pl.BlockSpec((tr, DA), lambda i: (i, 0)), pl.BlockSpec((tr, DA), lambda i: (i, 1)),
                  pl.BlockSpec((tr, DA), lambda i: (i, 0)),
                  pl.BlockSpec((1, DA), lambda i: (0, 0)), pl.BlockSpec((G, CHUNK, CHUNK), lambda i: (0, 0, 0)),
                  pl.BlockSpec((G, CHUNK, 1), lambda i: (0, 0, 0))],
        out_specs=[pl.BlockSpec((tr, 2 * DA), lambda i: (i, 0)), pl.BlockSpec((G, CHUNK, CHUNK), lambda i: (0, 0, 0)),
                   pl.BlockSpec((G, CHUNK, 1), lambda i: (0, 0, 0)), pl.BlockSpec((1, DA), lambda i: (0, 0))],
        out_shape=[jax.ShapeDtypeStruct((S, 2 * DA), BF16), jax.ShapeDtypeStruct((G, CHUNK, CHUNK), F32),
                   jax.ShapeDtypeStruct((G, CHUNK, 1), F32), jax.ShapeDtypeStruct((1, DA), F32)],
        scratch_shapes=[pltpu.VMEM((G, CHUNK, CHUNK), F32)],
        compiler_params=_params(("arbitrary",)),
    )(proj, proj, dcat, ng, w, bcol)


def _conv_tile(S):
    return _tile(S, 256, 8)


def conv_fwd(proj, wk, bias, DA, DB, name):
    S = proj.shape[0]
    nb = DB // LANES
    a0, g0 = 2 * DA // LANES, (2 * DA + DB) // LANES
    T = _conv_tile(S)
    off = CONV_PAD - (CONV_W - 1)

    def body(a_ref, g_ref, w_ref, b_ref, o_ref, ypad):
        ypad[0:CONV_PAD, :] = jnp.zeros((CONV_PAD, LANES), F32)

        def fill(t, cr):
            r = pl.multiple_of(t * T, T)
            ypad[pl.ds(CONV_PAD + r, T), :] = a_ref[pl.ds(r, T), :] * _sigmoid(g_ref[pl.ds(r, T), :])
            return cr

        lax.fori_loop(0, S // T, fill, 0)

        def step(t, cr):
            r = pl.multiple_of(t * T, T)
            acc = jnp.zeros((T, LANES), F32) + b_ref[...]
            for k in range(CONV_W):
                acc = acc + w_ref[k:k + 1, :] * ypad[pl.ds(r + (k + off), T), :]
            o_ref[pl.ds(r, T), :] = acc
            return cr

        lax.fori_loop(0, S // T, step, 0)

    return _pcall(
        body, name=name, grid=(nb,),
        in_specs=[pl.BlockSpec((S, LANES), lambda j: (0, a0 + j)), pl.BlockSpec((S, LANES), lambda j: (0, g0 + j)),
                  pl.BlockSpec((CONV_PAD, LANES), lambda j: (0, j)), pl.BlockSpec((1, LANES), lambda j: (0, j))],
        out_specs=pl.BlockSpec((S, LANES), lambda j: (0, j)),
        out_shape=jax.ShapeDtypeStruct((S, DB), F32),
        scratch_shapes=[pltpu.VMEM((S + CONV_PAD, LANES), F32)],
        compiler_params=_params(("parallel",)),
    )(proj, proj, wk, bias)


def conv_bwd(proj, dy1, wk, DA, DB, name):
    S = proj.shape[0]
    nb = DB // LANES
    a0, g0 = 2 * DA // LANES, (2 * DA + DB) // LANES
    T = _conv_tile(S)
    off = CONV_PAD - (CONV_W - 1)

    def body(a_ref, g_ref, d_ref, w_ref, da_ref, dg_ref, dw_ref, ypad, dpad, wacc):
        ypad[0:CONV_PAD, :] = jnp.zeros((CONV_PAD, LANES), F32)
        dpad[S:S + CONV_PAD, :] = jnp.zeros((CONV_PAD, LANES), F32)
        wacc[...] = jnp.zeros_like(wacc)

        def fill(t, cr):
            r = pl.multiple_of(t * T, T)
            ypad[pl.ds(CONV_PAD + r, T), :] = a_ref[pl.ds(r, T), :] * _sigmoid(g_ref[pl.ds(r, T), :])
            dpad[pl.ds(r, T), :] = d_ref[pl.ds(r, T), :]
            return cr

        lax.fori_loop(0, S // T, fill, 0)

        def step(t, cr):
            r = pl.multiple_of(t * T, T)
            dt = dpad[pl.ds(r, T), :]
            dy0 = jnp.zeros((T, LANES), F32)
            for k in range(CONV_W):
                prod = dt * ypad[pl.ds(r + (k + off), T), :]
                wacc[k] += jnp.sum(prod.reshape(T // 8, 8, LANES), axis=0)
                dy0 = dy0 + w_ref[k:k + 1, :] * dpad[pl.ds(r + (CONV_W - 1 - k), T), :]
            av, gv = a_ref[pl.ds(r, T), :], g_ref[pl.ds(r, T), :]
            sg = _sigmoid(gv)
            da_ref[pl.ds(r, T), :] = (dy0 * sg).astype(da_ref.dtype)
            dg_ref[pl.ds(r, T), :] = (dy0 * av * sg * (1.0 - sg)).astype(dg_ref.dtype)
            return cr

        lax.fori_loop(0, S // T, step, 0)
        for k in range(CONV_W):
            dw_ref[k:k + 1, :] = jnp.sum(wacc[k], axis=0, keepdims=True)
        dw_ref[CONV_W:CONV_PAD, :] = jnp.zeros((CONV_PAD - CONV_W, LANES), F32)

    return _pcall(
        body, name=name, grid=(nb,),
        in_specs=[pl.BlockSpec((S, LANES), lambda j: (0, a0 + j)), pl.BlockSpec((S, LANES), lambda j: (0, g0 + j)),
                  pl.BlockSpec((S, LANES), lambda j: (0, j)), pl.BlockSpec((CONV_PAD, LANES), lambda j: (0, j))],
        out_specs=[pl.BlockSpec((S, LANES), lambda j: (0, j)), pl.BlockSpec((S, LANES), lambda j: (0, j)),
                   pl.BlockSpec((CONV_PAD, LANES), lambda j: (0, j))],
        out_shape=[jax.ShapeDtypeStruct((S, DB), BF16), jax.ShapeDtypeStruct((S, DB), BF16),
                   jax.ShapeDtypeStruct((CONV_PAD, DB), F32)],
        scratch_shapes=[pltpu.VMEM((S + CONV_PAD, LANES), F32), pltpu.VMEM((S + CONV_PAD, LANES), F32),
                        pltpu.VMEM((CONV_PAD, 8, LANES), F32)],
        compiler_params=_params(("parallel",)),
    )(proj, proj, dy1, wk)


def _ln_stats(y):
    mu = jnp.mean(y, axis=-1, keepdims=True)
    yc = y - mu
    rs = lax.rsqrt(jnp.mean(yc * yc, axis=-1, keepdims=True) + EPS)
    return yc * rs, rs


def ln_silu(y1, lg, lb, name):
    S, DB = y1.shape

    def fn(t, c, o):
        yh, _ = _ln_stats(t[0][...])
        ln = yh * c[0][...] + c[1][...]
        o[0][...] = (ln * _sigmoid(ln)).astype(BF16)
        return []

    return rowwise(fn, [y1], [lg, lb], [((S, DB), BF16)], ts=_tile(S, 128, 16), name=name)[0]


def ln_silu_bwd(dcat, y1, lg, lb, name):
    S, DB = y1.shape
    cb = (dcat.shape[1] - DB) // DB

    def fn(t, c, o):
        yh, rs = _ln_stats(t[1][...])
        gv = c[0][...]
        ln = yh * gv + c[1][...]
        sg = _sigmoid(ln)
        dln = t[0][...] * (sg * (1.0 + ln * (1.0 - sg)))
        dyh = dln * gv
        dy = rs * (dyh - jnp.mean(dyh, axis=-1, keepdims=True) - yh * jnp.mean(dyh * yh, axis=-1, keepdims=True))
        o[0][...] = dy
        return [_colsum(dln * yh), _colsum(dln), _colsum(dy)]

    return rowwise(fn, [(dcat, DB, cb), y1], [lg, lb], [((S, DB), F32)], [(1, DB)] * 3, ts=_tile(S, 128, 8), name=name)


def mla_norms(proj, cos2, sinm, qg, kvg, kgr, R, name):
    S = proj.shape[0]

    def fn(t, c, o):
        cq = t[0][:, 0:R]
        ckv = t[0][:, R:2 * R]
        kr = t[0][:, 2 * R:2 * R + ROPE]
        o[0][...] = (cq * _rstd(cq) * c[0][...]).astype(BF16)
        o[1][...] = (ckv * _rstd(ckv) * c[1][...]).astype(BF16)
        o[2][...] = _rope(kr * _rstd(kr) * c[2][...], t[1][...], t[2][...])
        return []

    return rowwise(fn, [proj, cos2, sinm], [qg, kvg, kgr], [((S, R), BF16), ((S, R), BF16), ((S, ROPE), F32)],
                   ts=_tile(S, 128, 16), name=name)


def mla_heads(q, kv, kr, cos2, sinm, qg, kg, H, name):
    S = q.shape[0]

    def fn(t, c, o):
        cs, sn = t[3][...], t[4][...]
        krv = t[2][...]
        qgn, qgr, kgn = c[0][:, 0:NOPE], c[0][:, NOPE:QK], c[1][:, 0:NOPE]
        for h in range(H):
            qn = t[0][:, QK * h:QK * h + NOPE]
            qr = t[0][:, QK * h + NOPE:QK * (h + 1)]
            o[0][h, :, 0:NOPE] = (qn * _rstd(qn) * qgn).astype(BF16)
            o[0][h, :, NOPE:QK] = _rope(qr * _rstd(qr) * qgr, cs, sn).astype(BF16)
            kn = t[1][:, (NOPE + VDIM) * h:(NOPE + VDIM) * h + NOPE]
            o[1][h, :, 0:NOPE] = (kn * _rstd(kn) * kgn).astype(BF16)
            o[1][h, :, NOPE:QK] = krv.astype(BF16)
            o[2][h] = t[1][:, (NOPE + VDIM) * h + NOPE:(NOPE + VDIM) * (h + 1)].astype(BF16)
        return []

    return rowwise(fn, [q, kv, kr, cos2, sinm], [qg, kg],
                   [((H, S, QK), BF16), ((H, S, QK), BF16), ((H, S, VDIM), BF16)], ts=_tile(S, 128, 16), name=name)


def mla_heads_bwd(dQ, dK, dV, q, kv, cos2, sinm, qg, kg, H, name):
    S = q.shape[0]
    KV = NOPE + VDIM

    def fn(t, c, o):
        cs, sn = t[5][...], t[6][...]
        qgn, qgr, kgn = c[0][:, 0:NOPE], c[0][:, NOPE:QK], c[1][:, 0:NOPE]
        a_qn = jnp.zeros((1, NOPE), F32)
        a_qr = jnp.zeros((1, ROPE), F32)
        a_kn = jnp.zeros((1, NOPE), F32)
        dkr = jnp.zeros((t[0].shape[1], ROPE), F32)
        for h in range(H):
            qn = t[3][:, QK * h:QK * h + NOPE]
            rs = _rstd(qn)
            dx, dg = _rms_bwd(t[0][h, :, 0:NOPE], qn * rs, rs, qgn)
            o[0][:, QK * h:QK * h + NOPE] = dx.astype(BF16)
            a_qn = a_qn + dg
            qr = t[3][:, QK * h + NOPE:QK * (h + 1)]
            rs = _rstd(qr)
            dx, dg = _rms_bwd(_unrope(t[0][h, :, NOPE:QK], cs, sn), qr * rs, rs, qgr)
            o[0][:, QK * h + NOPE:QK * (h + 1)] = dx.astype(BF16)
            a_qr = a_qr + dg
            kn = t[4][:, KV * h:KV * h + NOPE]
            rs = _rstd(kn)
            dx, dg = _rms_bwd(t[1][h, :, 0:NOPE], kn * rs, rs, kgn)
            o[1][:, KV * h:KV * h + NOPE] = dx.astype(BF16)
            a_kn = a_kn + dg
            o[1][:, KV * h + NOPE:KV * (h + 1)] = t[2][h].astype(BF16)
            dkr = dkr + t[1][h, :, NOPE:QK]
        o[2][...] = _unrope(dkr, cs, sn)
        return [a_qn, a_qr, a_kn]

    return rowwise(fn, [dQ, dK, dV, q, kv, cos2, sinm], [qg, kg],
                   [((S, H * QK), BF16), ((S, H * KV), BF16), ((S, ROPE), F32)],
                   [(1, NOPE), (1, ROPE), (1, NOPE)], ts=_tile(S, 128, 16), name=name)


def mla_norms_bwd(dqn, dkvn, dkr, proj, qg, kvg, kgr, R, name):
    S = proj.shape[0]

    def fn(t, c, o):
        reds = []
        for idx, (lo, hi) in enumerate(((0, R), (R, 2 * R), (2 * R, 2 * R + ROPE))):
            xv = t[3][:, lo:hi]
            rs = _rstd(xv)
            dx, dg = _rms_bwd(t[idx][...], xv * rs, rs, c[idx][...])
            o[0][:, lo:hi] = dx.astype(BF16)
            reds.append(dg)
        return reds

    return rowwise(fn, [dqn, dkvn, dkr, proj], [qg, kvg, kgr], [((S, 2 * R + ROPE), BF16)],
                   [(1, R), (1, R), (1, ROPE)], ts=_tile(S, 128, 16), name=name)


def _tri_rows(p, n):
    qi = 0
    for j in range(1, n):
        qi = qi + (p >= j * (j + 1) // 2).astype(jnp.int32)
    return qi, p - (qi * (qi + 1)) // 2


def _tri_cols(p, n):
    ki = 0
    for j in range(1, n):
        ki = ki + (p >= j * n - j * (j - 1) // 2).astype(jnp.int32)
    return ki, ki + p - (ki * n - (ki * (ki - 1)) // 2)


def attn_fwd(Q, K, V, name):
    H, S, _ = Q.shape
    t = _tile(S, ATTN_BLOCK)
    n = S // t
    scale = QK ** -0.5

    rs = _tile(t, ATTN_STRIP, 8)

    def body(q_ref, k_ref, v_ref, o_ref, lse_ref, m_s, l_s, acc, s_scr, p_scr):
        qi, ki = _tri_rows(pl.program_id(1), n)

        @pl.when(ki == 0)
        def _():
            m_s[...] = jnp.full_like(m_s, NEG)
            l_s[...] = jnp.zeros_like(l_s)
            acc[...] = jnp.zeros_like(acc)

        def block(diagonal):
            s_scr[...] = lax.dot_general(q_ref[...], k_ref[...], _DIMS["nt"], preferred_element_type=F32)

            def strip(i, cr):
                r = slice(i * rs, (i + 1) * rs)
                s = s_scr[r, :] * scale
                if diagonal:
                    row = i * rs + lax.broadcasted_iota(jnp.int32, (rs, t), 0)
                    s = jnp.where(lax.broadcasted_iota(jnp.int32, (rs, t), 1) <= row, s, NEG)
                m_old = m_s[r, :]
                m_new = jnp.maximum(m_old, jnp.max(s, axis=-1, keepdims=True))
                alpha = jnp.exp(m_old - m_new)
                p = jnp.exp(s - m_new)
                l_s[r, :] = alpha * l_s[r, :] + jnp.sum(p, axis=-1, keepdims=True)
                m_s[r, :] = m_new
                acc[r, :] = alpha * acc[r, :]
                p_scr[r, :] = p.astype(BF16)
                return cr

            for i in range(t // rs):
                strip(i, 0)
            acc[...] += jnp.dot(p_scr[...], v_ref[...], preferred_element_type=F32)

        @pl.when(ki < qi)
        def _():
            block(False)

        @pl.when(ki == qi)
        def _():
            block(True)

        @pl.when(ki == qi)
        def _():
            o_ref[...] = (acc[...] / l_s[...]).astype(o_ref.dtype)
            lse_ref[...] = m_s[...] + jnp.log(l_s[...])

    return _pcall(
        body, name=name, grid=(H, n * (n + 1) // 2),
        in_specs=[pl.BlockSpec((None, t, QK), lambda h, p: (h, _tri_rows(p, n)[0], 0)),
                  pl.BlockSpec((None, t, QK), lambda h, p: (h, _tri_rows(p, n)[1], 0)),
                  pl.BlockSpec((None, t, VDIM), lambda h, p: (h, _tri_rows(p, n)[1], 0))],
        out_specs=[pl.BlockSpec((t, VDIM), lambda h, p: (_tri_rows(p, n)[0], h)),
                   pl.BlockSpec((None, t, 1), lambda h, p: (h, _tri_rows(p, n)[0], 0))],
        out_shape=[jax.ShapeDtypeStruct((S, H * VDIM), BF16), jax.ShapeDtypeStruct((H, S, 1), F32)],
        scratch_shapes=[pltpu.VMEM((t, 1), F32), pltpu.VMEM((t, 1), F32), pltpu.VMEM((t, VDIM), F32),
                        pltpu.VMEM((t, t), F32), pltpu.VMEM((t, t), BF16)],
        compiler_params=_params(("parallel", "arbitrary")),
    )(Q, K, V)


def attn_bwd(Q, K, V, dO, O, lse, name):
    H, S, _ = Q.shape
    t = _tile(S, ATTN_BLOCK)
    n = S // t
    scale = QK ** -0.5

    rs = _tile(t, ATTN_STRIP, 8)

    def body(q_ref, k_ref, v_ref, do_ref, o_ref, lse_ref, dq_ref, dk_ref, dv_ref, s_scr, dp_scr, p_scr, ds_scr):
        ki, qi = _tri_cols(pl.program_id(1), n)

        @pl.when(pl.program_id(1) == 0)
        def _():
            dq_ref[...] = jnp.zeros_like(dq_ref)

        @pl.when(qi == ki)
        def _():
            dk_ref[...] = jnp.zeros_like(dk_ref)
            dv_ref[...] = jnp.zeros_like(dv_ref)

        def block(diagonal):
            s_scr[...] = lax.dot_general(q_ref[...], k_ref[...], _DIMS["nt"], preferred_element_type=F32)
            dp_scr[...] = lax.dot_general(do_ref[...], v_ref[...], _DIMS["nt"], preferred_element_type=F32)

            def strip(i, cr):
                r = slice(i * rs, (i + 1) * rs)
                s = s_scr[r, :] * scale
                if diagonal:
                    row = i * rs + lax.broadcasted_iota(jnp.int32, (rs, t), 0)
                    s = jnp.where(lax.broadcasted_iota(jnp.int32, (rs, t), 1) <= row, s, NEG)
                p = jnp.exp(s - lse_ref[r, :])
                delta = jnp.sum(do_ref[r, :].astype(F32) * o_ref[r, :].astype(F32), axis=-1, keepdims=True)
                p_scr[r, :] = p.astype(BF16)
                ds_scr[r, :] = (p * (dp_scr[r, :] - delta) * scale).astype(BF16)
                return cr

            for i in range(t // rs):
                strip(i, 0)
            ds = ds_scr[...]
            dv_ref[...] += lax.dot_general(p_scr[...], do_ref[...], _DIMS["tn"], preferred_element_type=F32)
            dk_ref[...] += lax.dot_general(ds, q_ref[...], _DIMS["tn"], preferred_element_type=F32)
            rq = pl.multiple_of(qi * t, t)
            dq_ref[pl.ds(rq, t), :] += jnp.dot(ds, k_ref[...], preferred_element_type=F32)

        @pl.when(qi > ki)
        def _():
            block(False)

        @pl.when(qi == ki)
        def _():
            block(True)

    qmap = lambda h, p: (h, _tri_cols(p, n)[1], 0)
    kmap = lambda h, p: (h, _tri_cols(p, n)[0], 0)
    return _pcall(
        body, name=name, grid=(H, n * (n + 1) // 2),
        in_specs=[pl.BlockSpec((None, t, QK), qmap),
                  pl.BlockSpec((None, t, QK), kmap),
                  pl.BlockSpec((None, t, VDIM), kmap),
                  pl.BlockSpec((t, VDIM), lambda h, p: (_tri_cols(p, n)[1], h)),
                  pl.BlockSpec((t, VDIM), lambda h, p: (_tri_cols(p, n)[1], h)),
                  pl.BlockSpec((None, t, 1), qmap)],
        out_specs=[pl.BlockSpec((None, S, QK), lambda h, p: (h, 0, 0)),
                   pl.BlockSpec((None, t, QK), kmap),
                   pl.BlockSpec((None, t, VDIM), kmap)],
        out_shape=[jax.ShapeDtypeStruct((H, S, QK), F32), jax.ShapeDtypeStruct((H, S, QK), F32),
                   jax.ShapeDtypeStruct((H, S, VDIM), F32)],
        scratch_shapes=[pltpu.VMEM((t, t), F32), pltpu.VMEM((t, t), F32), pltpu.VMEM((t, t), BF16),
                        pltpu.VMEM((t, t), BF16)],
        compiler_params=_params(("parallel", "arbitrary")),
    )(Q, K, V, dO, O, lse)


def adamw(w, m, v, g, *, row0, name, into=None):
    P, rows, C = g.shape
    tr = _tile(rows, max(16, 131072 // C), 16)
    off = row0 // tr
    assert row0 % tr == 0
    bc1 = 1.0 - ADAM_B1 ** ADAM_STEP
    bc2 = 1.0 - ADAM_B2 ** ADAM_STEP
    chained = into is not None and into is not True

    def body(w_ref, m_ref, v_ref, g_ref, *rest):
        go_ref, d_ref, mo_ref, vo_ref = rest[-4:]
        gs = g_ref[0].astype(F32)
        for p in range(1, P):
            gs = gs + g_ref[p].astype(F32)
        wv = w_ref[...]
        mn = ADAM_B1 * m_ref[...] + (1.0 - ADAM_B1) * gs
        vn = ADAM_B2 * v_ref[...] + (1.0 - ADAM_B2) * (gs * gs)
        go_ref[...] = gs
        mo_ref[...] = mn
        vo_ref[...] = vn
        d_ref[...] = -ADAM_LR * ((mn / bc1) / (jnp.sqrt(vn / bc2) + ADAM_EPS) + ADAM_WD * wv)

    wspec = pl.BlockSpec((tr, C), lambda i: (i + off, 0))
    ospec = wspec if into is not None else pl.BlockSpec((tr, C), lambda i: (i, 0))
    out_rows = w.shape[0] if into is not None else rows
    return _pcall(
        body, name=name, grid=(rows // tr,),
        in_specs=[wspec, wspec, wspec, pl.BlockSpec((P, tr, C), lambda i: (0, i, 0))] + ([_ANY] * 4 if chained else []),
        out_specs=[ospec] * 4, out_shape=[jax.ShapeDtypeStruct((out_rows, C), F32)] * 4,
        input_output_aliases={4 + q: q for q in range(4)} if chained else {},
        compiler_params=_params(("parallel",)),
    )(w, m, v, g, *(into if chained else ()))


def _coords():
    return lax.axis_index("x"), lax.axis_index("y"), lax.axis_index("c")


def _me():
    x, y, c = _coords()
    return 4 * x + 2 * y + c


_ANY = pl.BlockSpec(memory_space=pl.ANY)


def all_gather(items, name):
    n = len(items)
    blks = [a.shape if idx is None else a.shape[1:] for a, idx in items]

    def body(*refs):
        ins, outs = refs[:n], refs[n:2 * n]
        send, recv, lsem = refs[2 * n:]
        x, y, c = _coords()
        me, sib = (x, y, c), (x, y, 1 - c)
        chips = [(1 - x, y), (x, 1 - y), (1 - x, 1 - y)]

        def src(i):
            return ins[i] if items[i][1] is None else ins[i].at[items[i][1]]

        def slot(i, p):
            return outs[i].at[4 * p[0] + 2 * p[1] + p[2]]

        def cp(i, k, block, to, s=None):
            return pltpu.make_async_remote_copy(
                src_ref=slot(i, block) if s is None else s, dst_ref=slot(i, block),
                send_sem=send.at[7 * i + k], recv_sem=recv.at[7 * i + k], device_id=to, device_id_type=MESH)

        mine = [pltpu.make_async_copy(src(i), slot(i, me), lsem.at[i]) for i in range(n)]
        for m_ in mine:
            m_.start()
        first = []
        for i in range(n):
            first.append(cp(i, 0, me, sib, src(i)))
            first += [cp(i, 1 + j, me, (*chip, c), src(i)) for j, chip in enumerate(chips)]
        for f in first:
            f.start()
        passed = []
        for j, chip in enumerate(chips):
            for i in range(n):
                cp(i, 1 + j, (*chip, c), me).wait_recv()
                p_ = cp(i, 4 + j, (*chip, c), sib)
                p_.start()
                passed.append(p_)
        for i in range(n):
            cp(i, 0, sib, me).wait_recv()
            for j, chip in enumerate(chips):
                cp(i, 4 + j, (*chip, 1 - c), me).wait_recv()
        for f in first + passed:
            f.wait_send()
        for m_ in mine:
            m_.wait()

    res = _pcall(
        body, name=name, in_specs=[_ANY] * n, out_specs=[_ANY] * n,
        out_shape=[jax.ShapeDtypeStruct((NDEV,) + tuple(b), a.dtype) for b, (a, _) in zip(blks, items)],
        scratch_shapes=[pltpu.SemaphoreType.DMA((7 * n,)), pltpu.SemaphoreType.DMA((7 * n,)),
                        pltpu.SemaphoreType.DMA((n,))],
    )(*[a for a, _ in items])
    return list(res)


_HBM = pl.BlockSpec(memory_space=pltpu.HBM)
_SEM = pl.BlockSpec(memory_space=pltpu.SEMAPHORE)
_EFFECT = pltpu.SideEffectType.DATAFLOW_SIDE_EFFECTING


def _xchg_copy(src_ref, land_ref, send, recv, r, scatter, at_peer):
    x, y, c = _coords()
    px = jnp.bitwise_xor(x, (r >> 2) & 1)
    py = jnp.bitwise_xor(y, (r >> 1) & 1)
    pc = jnp.bitwise_xor(c, r & 1)
    p_i = 4 * px + 2 * py + pc
    me_i = 4 * x + 2 * y + c
    return pltpu.make_async_remote_copy(
        src_ref=src_ref.at[p_i] if scatter else src_ref, dst_ref=land_ref.at[p_i if at_peer else me_i],
        send_sem=send.at[r - 1], recv_sem=recv.at[r - 1], device_id=(px, py, pc), device_id_type=MESH)


def _phase(body, name, bufs, sems_in=(), new_sems=(), after=None, token=False):
    nb, ns, nn = len(bufs), len(sems_in), len(new_sems)

    def wrapped(*refs):
        outs = refs[nb + ns + (after is not None):]
        body(refs[:nb], refs[nb:nb + ns], outs[:nn])
        if token:
            outs[nn + nb][...] = jnp.zeros_like(outs[nn + nb])

    res = _pcall(
        wrapped, name=name,
        out_shape=tuple([pltpu.SemaphoreType.DMA((k,)) for k in new_sems] + [pltpu.HBM(a.shape, a.dtype) for a in bufs]
                        + ([jax.ShapeDtypeStruct((8, LANES), F32)] if token else [])),
        in_specs=[_HBM] * nb + [_SEM] * ns + ([] if after is None else [_ANY]),
        out_specs=tuple([_SEM] * nn + [_HBM] * nb + ([pl.BlockSpec(memory_space=pltpu.VMEM)] if token else [])),
        input_output_aliases={i: nn + i for i in range(nb)},
        compiler_params=pltpu.CompilerParams(has_side_effects=_EFFECT),
    )(*[pltpu.with_memory_space_constraint(a, pltpu.HBM) for a in bufs], *sems_in, *([] if after is None else [after]))
    return list(res[nn:nn + nb]), list(res[:nn]), (res[nn + nb][0:1, 0:1] if token else None)


def scatter_start(srcs, name, after=None):
    n = len(srcs)
    lands = [lax.empty(s.shape, s.dtype) for s in srcs]

    def body(b, taken, new):
        me_i = _me()
        for i in range(n):
            pltpu.make_async_copy(b[i].at[me_i], b[n + i].at[me_i], new[3 * i + 2].at[0]).start()
            for r in range(1, NDEV):
                _xchg_copy(b[i], b[n + i], new[3 * i], new[3 * i + 1], r, True, False).start()

    bufs, sems, tok = _phase(body, name, list(srcs) + lands, new_sems=[NDEV - 1, NDEV - 1, 1] * n, after=after, token=True)
    return (bufs, sems), tok


def scatter_wait(handle, after, name):
    bufs, sems = handle
    n = len(bufs) // 2

    def body(b, taken, new):
        me_i = _me()
        for i in range(n):
            pltpu.make_async_copy(b[i].at[me_i], b[n + i].at[me_i], taken[3 * i + 2].at[0]).wait()
            for r in range(1, NDEV):
                cp = _xchg_copy(b[i], b[n + i], taken[3 * i], taken[3 * i + 1], r, True, True)
                cp.wait_send()
                cp.wait_recv()

    return _phase(body, name, bufs, sems_in=sems, after=after)[0][n:]


def _gather_peers():
    x, y, c = _coords()
    return (x, y, c), (x, y, 1 - c), [(1 - x, y), (x, 1 - y), (1 - x, 1 - y)]


def _row(p):
    return 4 * p[0] + 2 * p[1] + p[2]


def _gcopy(src_ref, land_ref, send, recv, k, block, to):
    return pltpu.make_async_remote_copy(
        src_ref=land_ref.at[_row(block)] if src_ref is None else src_ref, dst_ref=land_ref.at[_row(block)],
        send_sem=send.at[k], recv_sem=recv.at[k], device_id=to, device_id_type=MESH)


def gather_start(srcs, name, after=None):
    n = len(srcs)
    lands = [lax.empty((NDEV,) + s.shape, s.dtype) for s in srcs]

    def body(b, taken, new):
        me, sib, chips = _gather_peers()
        for i in range(n):
            send, recv = new[3 * i], new[3 * i + 1]
            pltpu.make_async_copy(b[i], b[n + i].at[_row(me)], new[3 * i + 2].at[0]).start()
            for j, chip in enumerate(chips):
                _gcopy(b[i], b[n + i], send, recv, 1 + j, me, (*chip, me[2])).start()
            _gcopy(b[i], b[n + i], send, recv, 0, me, sib).start()

    bufs, sems, tok = _phase(body, name, list(srcs) + lands, new_sems=[4, 4, 1] * n, after=after, token=True)
    return (bufs, sems), tok


def gather_mid(handle, after, name):
    bufs, sems = handle
    n = len(bufs) // 2

    def body(b, taken, new):
        me, sib, chips = _gather_peers()
        for j, chip in enumerate(chips):
            for i in range(n):
                _gcopy(b[i], b[n + i], taken[3 * i], taken[3 * i + 1], 1 + j, (*chip, me[2]), me).wait_recv()
                _gcopy(None, b[n + i], new[2 * i], new[2 * i + 1], j, (*chip, me[2]), sib).start()
        for i in range(n):
            send, recv = taken[3 * i], taken[3 * i + 1]
            _gcopy(b[i], b[n + i], send, recv, 0, sib, me).wait_recv()
            for k in range(4):
                _gcopy(b[i], b[n + i], send, recv, k, me, sib).wait_send()
            pltpu.make_async_copy(b[i], b[n + i].at[_row(me)], taken[3 * i + 2].at[0]).wait()

    bufs, new, tok = _phase(body, name, bufs, sems_in=sems, new_sems=[3, 3] * n, after=after, token=True)
    return (bufs, new), tok


def gather_wait(handle, after, name):
    bufs, sems = handle
    n = len(bufs) // 2

    def body(b, taken, new):
        me, sib, chips = _gather_peers()
        for i in range(n):
            for j, chip in enumerate(chips):
                _gcopy(None, b[n + i], taken[2 * i], taken[2 * i + 1], j, (*chip, me[2]), sib).wait_send()
                _gcopy(None, b[n + i], taken[2 * i], taken[2 * i + 1], j, (*chip, 1 - me[2]), me).wait_recv()

    return _phase(body, name, bufs, sems_in=sems, after=after)[0][n:]


_PACK_ALIGN = 8 * LANES


def _pack(arrs, aligned=False):
    parts = []
    for a in arrs:
        f = a.reshape(-1).astype(F32)
        pad = (-f.shape[0]) % _PACK_ALIGN if aligned else 0
        parts.append(jnp.pad(f, (0, pad)) if pad else f)
    flat = jnp.concatenate(parts)
    pad = (-flat.shape[0]) % _PACK_ALIGN
    return (jnp.pad(flat, (0, pad)) if pad else flat).reshape(-1, LANES)


def _unpack(p, shapes, lead=(), aligned=False):
    nl = len(lead)
    flat = p.reshape(lead + (-1,))
    out, off = [], 0
    for shp in shapes:
        n = 1
        for d in shp:
            n *= d
        out.append(lax.slice_in_dim(flat, off, off + n, axis=nl).reshape(lead + tuple(shp)))
        off += n + ((-n) % _PACK_ALIGN if aligned else 0)
    return out


def _shard_cols(a, me, width, axis):
    return lax.dynamic_slice_in_dim(a, me * width, width, axis=axis)


def kernel(x, c, norm1_g, norm2_g, ada_w, ada_b, mlp_w1, mlp_w2, ab_w_in, sgu_norm_g, sgu_w, sgu_b, conv_w, conv_b, conv_ln_g, conv_ln_b, ab_w_out, mla_w_in, mla_q_norm_g, mla_kv_norm_g, mla_w_uq, mla_w_ukv, mla_q_head_g, mla_k_head_g, mla_w_out, loss_target, m_norm1_g, m_norm2_g, m_ada_w, m_ada_b, m_mlp_w1, m_mlp_w2, m_ab_w_in, m_sgu_norm_g, m_sgu_w, m_sgu_b, m_conv_w, m_conv_b, m_conv_ln_g, m_conv_ln_b, m_ab_w_out, m_mla_w_in, m_mla_q_norm_g, m_mla_kv_norm_g, m_mla_w_uq, m_mla_w_ukv, m_mla_q_head_g, m_mla_k_head_g, m_mla_w_out, v_norm1_g, v_norm2_g, v_ada_w, v_ada_b, v_mlp_w1, v_mlp_w2, v_ab_w_in, v_sgu_norm_g, v_sgu_w, v_sgu_b, v_conv_w, v_conv_b, v_conv_ln_g, v_conv_ln_b, v_ab_w_out, v_mla_w_in, v_mla_q_norm_g, v_mla_kv_norm_g, v_mla_w_uq, v_mla_w_ukv, v_mla_q_head_g, v_mla_k_head_g, v_mla_w_out):
    W = dict(norm1_g=norm1_g, norm2_g=norm2_g, ada_w=ada_w, ada_b=ada_b, mlp_w1=mlp_w1, mlp_w2=mlp_w2, ab_w_in=ab_w_in,
             sgu_norm_g=sgu_norm_g, sgu_w=sgu_w, sgu_b=sgu_b, conv_w=conv_w, conv_b=conv_b, conv_ln_g=conv_ln_g,
             conv_ln_b=conv_ln_b, ab_w_out=ab_w_out, mla_w_in=mla_w_in, mla_q_norm_g=mla_q_norm_g,
             mla_kv_norm_g=mla_kv_norm_g, mla_w_uq=mla_w_uq, mla_w_ukv=mla_w_ukv, mla_q_head_g=mla_q_head_g,
             mla_k_head_g=mla_k_head_g, mla_w_out=mla_w_out)
    M = dict(norm1_g=m_norm1_g, norm2_g=m_norm2_g, ada_w=m_ada_w, ada_b=m_ada_b, mlp_w1=m_mlp_w1, mlp_w2=m_mlp_w2,
             ab_w_in=m_ab_w_in, sgu_norm_g=m_sgu_norm_g, sgu_w=m_sgu_w, sgu_b=m_sgu_b, conv_w=m_conv_w, conv_b=m_conv_b,
             conv_ln_g=m_conv_ln_g, conv_ln_b=m_conv_ln_b, ab_w_out=m_ab_w_out, mla_w_in=m_mla_w_in,
             mla_q_norm_g=m_mla_q_norm_g, mla_kv_norm_g=m_mla_kv_norm_g, mla_w_uq=m_mla_w_uq, mla_w_ukv=m_mla_w_ukv,
             mla_q_head_g=m_mla_q_head_g, mla_k_head_g=m_mla_k_head_g, mla_w_out=m_mla_w_out)
    V = dict(norm1_g=v_norm1_g, norm2_g=v_norm2_g, ada_w=v_ada_w, ada_b=v_ada_b, mlp_w1=v_mlp_w1, mlp_w2=v_mlp_w2,
             ab_w_in=v_ab_w_in, sgu_norm_g=v_sgu_norm_g, sgu_w=v_sgu_w, sgu_b=v_sgu_b, conv_w=v_conv_w, conv_b=v_conv_b,
             conv_ln_g=v_conv_ln_g, conv_ln_b=v_conv_ln_b, ab_w_out=v_ab_w_out, mla_w_in=v_mla_w_in,
             mla_q_norm_g=v_mla_q_norm_g, mla_kv_norm_g=v_mla_kv_norm_g, mla_w_uq=v_mla_w_uq, mla_w_ukv=v_mla_w_ukv,
             mla_q_head_g=v_mla_q_head_g, mla_k_head_g=v_mla_k_head_g, mla_w_out=v_mla_w_out)
    ORDER = list(W)

    S, D = x.shape[1], x.shape[2]
    L, NE, NO = norm1_g.shape[0], ab_w_in.shape[0], mla_w_in.shape[0]
    DA = D // 2
    DB = D - DA
    G = DA // GROUP
    R = NDEV * mla_q_norm_g.shape[1]
    H = NDEV * mla_w_uq.shape[2] // QK
    AW = ada_w.shape[2]
    CB = conv_w.shape[2]
    me = _me()
    xs, tgt = x[0], loss_target[0]

    BIG_EVEN = ("mlp_w1", "mlp_w2", "ab_w_in", "ab_w_out")
    BIG_ODD = ("mlp_w1", "mlp_w2", "mla_w_in", "mla_w_uq", "mla_w_ukv", "mla_w_out")
    BIG = ("mlp_w1", "mlp_w2", "ab_w_in", "ab_w_out", "mla_w_in", "mla_w_uq", "mla_w_ukv", "mla_w_out")
    COL_SHARDED = ("mlp_w1", "ab_w_in", "mla_w_uq", "mla_w_ukv")
    MLP_W = ("mlp_w1", "mlp_w2")
    mixer_w = lambda l: ("ab_w_in", "ab_w_out") if l % 2 == 0 else ("mla_w_in", "mla_w_uq", "mla_w_ukv", "mla_w_out")
    widx = lambda k, l: l if k in MLP_W else l // 2

    small_in = [c, mla_q_norm_g, mla_kv_norm_g, conv_w]
    sg = all_gather([(_pack(small_in, True), None)], "gather_small")[0]
    c_all, qng_all, kvng_all, cw_all = _unpack(sg, [a.shape for a in small_in], (NDEV,), True)
    c_all = c_all.reshape(NDEV, D)
    qng_full = jnp.transpose(qng_all, (1, 0, 2)).reshape(NO, 1, R)
    kvng_full = jnp.transpose(kvng_all, (1, 0, 2)).reshape(NO, 1, R)
    cw_full = jnp.transpose(cw_all, (1, 2, 0, 3)).reshape(NE, CONV_W, DB)
    cw_pad = jnp.pad(cw_full, ((0, 0), (0, CONV_PAD - CONV_W), (0, 0)))

    def silu_fn(t, c_, o):
        v_ = t[0][...]
        o[0][...] = v_ * _sigmoid(v_)
        return []

    c_act = rowwise(silu_fn, [c_all], outs=[((NDEV, D), F32)], ts=NDEV, name="silu_c")[0]
    bias_cols = _shard_cols(ada_b, me, AW, 1).reshape(1, L * AW)
    mod_cols = mm(c_act, ada_w, "nn", name="ada_fwd", b_blocked=True, rowvecs=[bias_cols],
                  epi=lambda acc, b_: (acc + b_,), tm=NDEV, tn=768)
    mod_all = all_gather([(mod_cols, None)], "gather_mod")[0]
    mod = lax.dynamic_index_in_dim(mod_all, me, axis=1, keepdims=False)
    mod = jnp.transpose(mod.reshape(NDEV, L, AW), (1, 0, 2)).reshape(L, 6, 1, D)

    wnames = lambda l, what: mixer_w(l) if what == "mix" else MLP_W
    g_first, g_second = {}, {}
    tok_sum = jnp.zeros((1, 1), F32)
    for l in range(L):
        for what in ("mix", "mlp"):
            srcs = [cast_bf16(W[k], widx(k, l), "cast_%s_l%d" % (k, l)) for k in wnames(l, what)]
            g_first[l, what], tok = gather_start(srcs, "gather_start_%s_l%d" % (what, l), after=mod_all)
            tok_sum = tok_sum + tok
    mod = mod + tok_sum

    def pass_on(l, what, after):
        g_second[l, what], tok = gather_mid(g_first[l, what], after, "gather_mid_%s_l%d" % (what, l))
        return tok

    def wait_weights(l, what, after):
        lands = gather_wait(g_second[l, what], after, "gather_wait_%s_l%d" % (what, l))
        return {k: (ld if k in COL_SHARDED else ld.reshape(NDEV * ld.shape[1], ld.shape[2]))
                for k, ld in zip(wnames(l, what), lands)}

    mod = mod + pass_on(0, "mix", mod)

    pos = jnp.arange(S, dtype=F32)
    inv = ROPE_THETA ** (-jnp.arange(0, ROPE, 2, dtype=F32) / ROPE)
    ang = pos[:, None] * inv[None, :]
    cos2 = jnp.concatenate([jnp.cos(ang), jnp.cos(ang)], axis=1)
    sinm = jnp.concatenate([-jnp.sin(ang), jnp.sin(ang)], axis=1)

    residual = lambda acc, xr, gt: (acc, xr + gt * acc)

    saved = []
    xc = xs
    for l in range(L):
        sh1, sc1, g1, sh2, sc2, g2 = [mod[l, k] for k in range(6)]
        tag = "_l%d" % l
        sv = dict(x0=xc)
        h, sv["rstd1"] = prenorm(xc, norm1_g[l][None], sc1, sh1, "prenorm1" + tag)
        sv["h"] = h
        wl = wait_weights(l, "mix", h)
        if l % 2 == 0:
            e = l // 2
            ng = sgu_norm_g[e].reshape(1, DA)
            bcol = sgu_b[e][:, :, None]
            proj = mm(h, wl["ab_w_in"], "nn", name="ab_in" + tag, b_blocked=True)
            out_a = sgu_fwd(proj, ng, sgu_w[e], bcol, DA, "sgu_fwd" + tag)
            y1 = conv_fwd(proj, cw_pad[e], conv_b[e][None], DA, DB, "conv_fwd" + tag)
            out_b = ln_silu(y1, conv_ln_g[e][None], conv_ln_b[e][None], "ln_silu" + tag)
            cat = jnp.concatenate([out_a, out_b], axis=1)
            sv.update(proj=proj, y1=y1, cat=cat)
            mixb, x1 = mm(cat, wl["ab_w_out"], "nn", name="ab_out" + tag, out_dtypes=(BF16, F32), epi=residual,
                          extras=[xc], rowvecs=[g1])
        else:
            o_ = l // 2
            proj = mm(h, wl["mla_w_in"], "nn", name="mla_in" + tag)
            kgr = mla_k_head_g[o_][None, NOPE:QK]
            qn, kvn, kr = mla_norms(proj, cos2, sinm, qng_full[o_], kvng_full[o_], kgr, R, "mla_norms" + tag)
            q = mm(qn, wl["mla_w_uq"], "nn", name="mla_uq" + tag, b_blocked=True)
            kv = mm(kvn, wl["mla_w_ukv"], "nn", name="mla_ukv" + tag, b_blocked=True)
            Qh, Kh, Vh = mla_heads(q, kv, kr, cos2, sinm, mla_q_head_g[o_][None], mla_k_head_g[o_][None], H,
                                   "mla_heads" + tag)
            att, lse = attn_fwd(Qh, Kh, Vh, "attn_fwd" + tag)
            sv.update(proj=proj, qn=qn, kvn=kvn, q=q, kv=kv, Qh=Qh, Kh=Kh, Vh=Vh, att=att, lse=lse)
            mixb, x1 = mm(att, wl["mla_w_out"], "nn", name="mla_out" + tag, out_dtypes=(BF16, F32), epi=residual,
                          extras=[xc], rowvecs=[g1])
        sv.update(mixb=mixb, x1=x1)
        h2, sv["rstd2"] = prenorm(x1, norm2_g[l][None], sc2 + pass_on(l, "mlp", x1), sh2, "prenorm2" + tag)
        wl.update(wait_weights(l, "mlp", h2))
        sv["w"] = wl
        z, act = mm(h2, wl["mlp_w1"], "nn", name="mlp_up" + tag, b_blocked=True, out_dtypes=(BF16, BF16),
                    epi=lambda acc: (acc, jnp.square(jnp.maximum(acc, 0.0))))
        g2t = g2 + pass_on(l + 1, "mix", z) if l + 1 < L else g2
        yb, xc = mm(act, wl["mlp_w2"], "nn", name="mlp_down" + tag, out_dtypes=(BF16, F32), epi=residual,
                    extras=[x1], rowvecs=[g2t])
        sv.update(h2=h2, z=z, act=act, yb=yb)
        saved.append(sv)

    dx, loss_cols = loss_grad(xc, tgt, "loss")
    loss = lax.psum(0.5 / D * jnp.sum(loss_cols), ("x", "y", "c"))

    big_out = {}
    sm = {k: [None] * W[k].shape[0] for k in ("norm1_g", "norm2_g", "sgu_norm_g", "sgu_w", "sgu_b", "conv_b", "conv_ln_g",
                                               "conv_ln_b", "mla_q_head_g", "mla_k_head_g", "mla_q_norm_g",
                                               "mla_kv_norm_g", "conv_w")}
    dmod = [None] * L
    flat2 = {k: W[k].reshape(-1, W[k].shape[2]) for k in BIG}
    flat2m = {k: M[k].reshape(-1, W[k].shape[2]) for k in BIG}
    flat2v = {k: V[k].reshape(-1, W[k].shape[2]) for k in BIG}

    def send_grads(gr, what, tag, after=None):
        names = list(gr)
        blocks = [gr[k] if k in COL_SHARDED else gr[k].reshape(NDEV, gr[k].shape[0] // NDEV, gr[k].shape[1])
                  for k in names]
        handle, tok = scatter_start(blocks, "scatter_start_%s%s" % (what, tag), after=after)
        return (names, handle, what, tag), tok

    rep = ("norm1_g", "norm2_g", "sgu_norm_g", "sgu_w", "sgu_b", "conv_b", "conv_ln_g", "conv_ln_b", "mla_q_head_g",
           "mla_k_head_g")
    part_full = {"mla_q_norm_g": (NO, R), "mla_kv_norm_g": (NO, R), "conv_w": (NE, CONV_W, DB)}
    small = ["ada_b"] + list(rep) + list(part_full)
    shapes = [(L, 6 * D)] + [W[k].shape for k in rep] + list(part_full.values())

    def small_gather():
        parts = [jnp.stack(dmod).reshape(L, 6 * D)] + [jnp.stack(sm[k]).reshape(s_) for k, s_ in zip(small[1:], shapes[1:])]
        return all_gather([(_pack(parts), None)], "gather_smallgrads")[0]

    def scatter_finish(pending, after, l):
        names, handle, what, tag = pending
        landed = scatter_wait(handle, after, "scatter_wait_%s%s" % (what, tag))
        for k, land in zip(names, landed):
            li = widx(k, l)
            big_out[k] = adamw(flat2[k], flat2m[k], flat2v[k], land, row0=li * W[k].shape[1],
                               name="adamw_%s%s" % (k, tag), into=big_out.get(k, True))

    pend_mix, tok_mix = None, None
    for l in reversed(range(L)):
        sh1, sc1, g1, sh2, sc2, g2 = [mod[l, k] for k in range(6)]
        if tok_mix is not None:
            g2 = g2 + tok_mix
        sv = saved[l]
        wl = sv["w"]
        tag = "_l%d" % l
        gr = {}
        dy, dgate2 = gate_bwd(dx, sv["yb"], g2, "gate2_bwd" + tag)
        dw2 = mm(sv["act"], dy, "tn", name="mlp_down_dw" + tag, out_dtypes=(BF16,))
        pend_w2, tok_w2 = send_grads({"mlp_w2": dw2}, "w2", tag)
        dz = mm(dy, wl["mlp_w2"], "nt", name="mlp_down_dx" + tag, out_dtypes=(BF16,), extras=[sv["z"]],
                rowvecs=[jnp.zeros((1, sv["z"].shape[1]), F32) + tok_w2],
                epi=lambda acc, z_, t_: (acc * (2.0 * jnp.maximum(z_.astype(F32), 0.0)) + t_,))
        dh2 = mm(dz, wl["mlp_w1"], "nt", name="mlp_up_dx" + tag, b_blocked=True)
        dw1 = mm(sv["h2"], dz, "tn", name="mlp_up_dw" + tag, out_dtypes=(BF16,), out_blocked=NDEV)
        dx1, dsc2, dsh2, sm["norm2_g"][l] = norm_bwd(dh2, sv["x1"], sv["rstd2"], dx, norm2_g[l][None], sc2,
                                                     "norm2_bwd" + tag)
        pend_mlp, tok_mlp = send_grads({"mlp_w1": dw1}, "w1", tag)
        if pend_mix is not None:
            scatter_finish(pend_mix, dx1, l + 1)
        dmix, dgate1 = gate_bwd(dx1, sv["mixb"], g1 + tok_mlp, "gate1_bwd" + tag)
        if l % 2 == 0:
            e = l // 2
            ng = sgu_norm_g[e].reshape(1, DA)
            bcol = sgu_b[e][:, :, None]
            dcat = mm(dmix, wl["ab_w_out"], "nt", name="ab_out_dx" + tag)
            gr["ab_w_out"] = mm(sv["cat"], dmix, "tn", name="ab_out_dw" + tag, out_dtypes=(BF16,))
            dy1, sm["conv_ln_g"][e], sm["conv_ln_b"][e], sm["conv_b"][e] = ln_silu_bwd(
                dcat, sv["y1"], conv_ln_g[e][None], conv_ln_b[e][None], "ln_silu_bwd" + tag)
            da, dg_, dwc = conv_bwd(sv["proj"], dy1, cw_pad[e], DA, DB, "conv_bwd" + tag)
            sm["conv_w"][e] = dwc[:CONV_W]
            duv, dsw, dsb, dsng = sgu_bwd(sv["proj"], dcat, ng, sgu_w[e], bcol, DA, "sgu_bwd" + tag)
            sm["sgu_w"][e], sm["sgu_b"][e], sm["sgu_norm_g"][e] = dsw, dsb, dsng
            dproj = jnp.concatenate([duv, da, dg_], axis=1)
            dh = mm(dproj, wl["ab_w_in"], "nt", name="ab_in_dx" + tag, b_blocked=True)
            gr["ab_w_in"] = mm(sv["h"], dproj, "tn", name="ab_in_dw" + tag, out_dtypes=(BF16,), out_blocked=NDEV)
        else:
            o_ = l // 2
            kgr = mla_k_head_g[o_][None, NOPE:QK]
            dO = mm(dmix, wl["mla_w_out"], "nt", name="mla_out_dx" + tag, out_dtypes=(BF16,))
            gr["mla_w_out"] = mm(sv["att"], dmix, "tn", name="mla_out_dw" + tag, out_dtypes=(BF16,))
            dQ, dK, dV = attn_bwd(sv["Qh"], sv["Kh"], sv["Vh"], dO, sv["att"], sv["lse"], "attn_bwd" + tag)
            dq_pre, dkv_pre, dkr, dqgn, dqgr, dkgn = mla_heads_bwd(
                dQ, dK, dV, sv["q"], sv["kv"], cos2, sinm, mla_q_head_g[o_][None], mla_k_head_g[o_][None], H,
                "mla_heads_bwd" + tag)
            dqn = mm(dq_pre, wl["mla_w_uq"], "nt", name="mla_uq_dx" + tag, b_blocked=True)
            gr["mla_w_uq"] = mm(sv["qn"], dq_pre, "tn", name="mla_uq_dw" + tag, out_dtypes=(BF16,), out_blocked=NDEV)
            dkvn = mm(dkv_pre, wl["mla_w_ukv"], "nt", name="mla_ukv_dx" + tag, b_blocked=True)
            gr["mla_w_ukv"] = mm(sv["kvn"], dkv_pre, "tn", name="mla_ukv_dw" + tag, out_dtypes=(BF16,),
                                 out_blocked=NDEV)
            dproj, sm["mla_q_norm_g"][o_], sm["mla_kv_norm_g"][o_], dkgr = mla_norms_bwd(
                dqn, dkvn, dkr, sv["proj"], qng_full[o_], kvng_full[o_], kgr, R, "mla_norms_bwd" + tag)
            sm["mla_q_head_g"][o_] = jnp.concatenate([dqgn, dqgr], axis=1)
            sm["mla_k_head_g"][o_] = jnp.concatenate([dkgn, dkgr], axis=1)
            dh = mm(dproj, wl["mla_w_in"], "nt", name="mla_in_dx" + tag)
            gr["mla_w_in"] = mm(sv["h"], dproj, "tn", name="mla_in_dw" + tag, out_dtypes=(BF16,))
        dx, dsc1, dsh1, sm["norm1_g"][l] = norm_bwd(dh, sv["x0"], sv["rstd1"], dx1, norm1_g[l][None], sc1,
                                                    "norm1_bwd" + tag)
        dmod[l] = jnp.concatenate([dsh1, dsc1, dgate1, dsh2, dsc2, dgate2], axis=1)

        gp = small_gather() if l == 0 else None
        pend_mix, tok_mix = send_grads(gr, "mix", tag, after=gp)
        scatter_finish(pend_w2, dx, l)
        scatter_finish(pend_mlp, dx, l)

    rows_p = gp.shape[1]
    per = 6 * D // LANES
    dm = lax.slice_in_dim(gp, 0, L * per, axis=1).reshape(NDEV, L, per, LANES)
    dmod_cols = lax.dynamic_slice_in_dim(dm, me * (AW // LANES), AW // LANES, axis=2).reshape(NDEV, L * AW) + tok_mix

    def sum_fn(t, c_, o):
        acc = t[0][0]
        for s_ in range(1, NDEV):
            acc = acc + t[0][s_]
        o[0][...] = acc
        return []

    gsummed = rowwise(sum_fn, [gp], outs=[((rows_p, LANES), F32)], ts=_tile(rows_p, 256, 8), name="sum_smallgrads")[0]
    gsum = dict(zip(small, _unpack(gsummed, shapes)))
    gsum["mla_q_norm_g"] = _shard_cols(gsum["mla_q_norm_g"], me, R // NDEV, 1)
    gsum["mla_kv_norm_g"] = _shard_cols(gsum["mla_kv_norm_g"], me, R // NDEV, 1)
    gsum["conv_w"] = _shard_cols(gsum["conv_w"], me, CB, 2)
    sm_shapes = [W[k].shape for k in small]
    sres = adamw(_pack([W[k] for k in small]), _pack([M[k] for k in small]), _pack([V[k] for k in small]),
                 (_pack([gsum[k] for k in small]) + tok_mix)[None], row0=0, name="adamw_small")
    small_out = {k: vals for k, vals in zip(small, zip(*[_unpack(r_, sm_shapes) for r_ in sres]))}

    g_ada = mm(c_act, dmod_cols, "tn", name="ada_dw", out_blocked=L, tm=1024, tn=768, tk=NDEV, cast=None,
               precision=lax.Precision.HIGHEST)
    ada_out = adamw(ada_w.reshape(L * D, AW), m_ada_w.reshape(L * D, AW), v_ada_w.reshape(L * D, AW),
                    g_ada.reshape(1, L * D, AW), row0=0, name="adamw_ada_w")
    scatter_finish(pend_mix, ada_out[1], 0)
    ada_out = [a.reshape(L, D, AW) for a in ada_out]

    def result(k, which):
        if k == "ada_w":
            return ada_out[which]
        if k in BIG:
            return big_out[k][which].reshape(W[k].shape)
        return small_out[k][which]

    outs = [loss, dx[None]]
    for which in range(4):
        outs += [result(k, which) for k in ORDER]
    return tuple(outs)
```

```python
import functools

import jax
import jax.numpy as jnp
from jax import lax
from jax.experimental import pallas as pl
from jax.experimental.pallas import tpu as pltpu

F32 = jnp.float32
BF16 = jnp.bfloat16
EPS = 1e-6
NDEV = 8
LANES = 128
CHUNK = 128
GROUP = 128
CONV_W = 31
CONV_PAD = 32
NOPE, ROPE, VDIM = 128, 64, 128
QK = NOPE + ROPE
ROPE_THETA = 10000.0
VMEM_LIMIT = 56 * 1024 * 1024
ADAM_LR, ADAM_B1, ADAM_B2, ADAM_EPS, ADAM_WD, ADAM_STEP = 0.001, 0.9, 0.999, 1e-08, 0.01, 10
MESH = pl.DeviceIdType.MESH
NEG = -1e30
ROW_TILE = 256
HEAD_ROWS = 256
ATTN_BLOCK = 1024
ATTN_STRIP = 64


def _pcall(body, **kw):
    return pl.pallas_call(body, **kw)


def _params(sem=None):
    return pltpu.CompilerParams(dimension_semantics=sem, vmem_limit_bytes=VMEM_LIMIT)


def _tile(dim, target, align=LANES):
    if dim <= target:
        return dim
    t = (target // align) * align
    while t >= align:
        if dim % t == 0:
            return t
        t -= align
    return dim


def _rstd(x):
    return lax.rsqrt(jnp.mean(x * x, axis=-1, keepdims=True) + EPS)


def _sigmoid(x):
    return 1.0 / (1.0 + jnp.exp(-x))


_GC = 0.7978845608028654


def _gelu(x):
    return 0.5 * x * (1.0 + jnp.tanh(_GC * (x + 0.044715 * x * x * x)))


def _gelu_grad(x):
    t = jnp.tanh(_GC * (x + 0.044715 * x * x * x))
    return 0.5 * (1.0 + t) + 0.5 * x * (1.0 - t * t) * _GC * (1.0 + 3 * 0.044715 * x * x)


def _colsum(x):
    return jnp.sum(x, axis=0, keepdims=True)


def _rms_bwd(dy, xhat, rstd, g):
    dxh = dy * g
    dx = rstd * (dxh - xhat * jnp.mean(dxh * xhat, axis=-1, keepdims=True))
    return dx, _colsum(dy * xhat)


def _swap_halves(x):
    h = x.shape[-1] // 2
    return jnp.concatenate([x[:, h:], x[:, :h]], axis=1)


def _rope(x, cos2, sinm):
    return x * cos2 + _swap_halves(x) * sinm


def _unrope(dy, cos2, sinm):
    return dy * cos2 + _swap_halves(dy * sinm)


_DIMS = {"nn": (((1,), (0,)), ((), ())), "nt": (((1,), (1,)), ((), ())), "tn": (((0,), (0,)), ((), ()))}


def mm(a, b, mode, *, name, out_dtypes=(F32,), epi=None, extras=(), rowvecs=(), b_blocked=False, out_blocked=0,
       tm=1024, tn=1024, tk=2048, precision=None, cast=BF16):
    if mode == "tn":
        K, M = a.shape
    else:
        M, K = a.shape
    if b_blocked:
        J, Rb, Cb = b.shape
        N = Rb if mode == "nt" else J * Cb
    else:
        N = b.shape[0] if mode == "nt" else b.shape[1]
    tm = _tile(M, tm)
    if mode == "nn" and b_blocked:
        tn = _tile(Cb, tn)
    elif out_blocked:
        tn = _tile(N // out_blocked, tn)
    else:
        tn = _tile(N, tn)
    kb = 1
    if mode == "nt" and b_blocked:
        if Cb >= tk:
            tk = _tile(Cb, tk)
        else:
            kb = max(d for d in range(1, J + 1) if J % d == 0 and d * Cb <= tk)
            tk = kb * Cb
    else:
        tk = _tile(K, tk)
    nk = K // tk
    grid = (M // tm, N // tn, nk)

    if mode == "tn":
        a_spec = pl.BlockSpec((tk, tm), lambda i, j, k: (k, i))
    else:
        a_spec = pl.BlockSpec((tm, tk), lambda i, j, k: (i, k))
    if mode == "nn":
        if b_blocked:
            nper = Cb // tn
            b_spec = pl.BlockSpec((None, tk, tn), lambda i, j, k: (j // nper, k, j % nper))
        else:
            b_spec = pl.BlockSpec((tk, tn), lambda i, j, k: (k, j))
    elif mode == "nt":
        if b_blocked:
            if kb > 1:
                b_spec = pl.BlockSpec((kb, tn, Cb), lambda i, j, k: (k, j, 0))
            else:
                kper = Cb // tk
                b_spec = pl.BlockSpec((None, tn, tk), lambda i, j, k: (k // kper, j, k % kper))
        else:
            b_spec = pl.BlockSpec((tn, tk), lambda i, j, k: (j, k))
    else:
        b_spec = pl.BlockSpec((tk, tn), lambda i, j, k: (k, j))
    if out_blocked:
        oper = (N // out_blocked) // tn
        o_spec = pl.BlockSpec((None, tm, tn), lambda i, j, k: (j // oper, i, j % oper))
        o_shape = (out_blocked, M, N // out_blocked)
    else:
        o_spec = pl.BlockSpec((tm, tn), lambda i, j, k: (i, j))
        o_shape = (M, N)
    e_spec = pl.BlockSpec((tm, tn), lambda i, j, k: (i, j))
    r_spec = pl.BlockSpec((1, tn), lambda i, j, k: (0, j))
    ne, nr, no = len(extras), len(rowvecs), len(out_dtypes)
    dims = _DIMS[mode]

    def body(a_ref, b_ref, *rest):
        ex = rest[:ne]
        rv = rest[ne:ne + nr]
        outs = rest[ne + nr:ne + nr + no]

        def product():
            if kb > 1:
                r = None
                for q in range(kb):
                    av, bv = a_ref[:, q * Cb:(q + 1) * Cb], b_ref[q]
                    if cast is not None:
                        av, bv = av.astype(cast), bv.astype(cast)
                    d = lax.dot_general(av, bv, dims, preferred_element_type=F32, precision=precision)
                    r = d if r is None else r + d
                return r
            av, bv = a_ref[...], b_ref[...]
            if cast is not None:
                av, bv = av.astype(cast), bv.astype(cast)
            return lax.dot_general(av, bv, dims, preferred_element_type=F32, precision=precision)

        def finish(r):
            vals = (r,) if epi is None else epi(r, *[e[...] for e in ex], *[v[...] for v in rv])
            for o, val in zip(outs, vals):
                o[...] = val.astype(o.dtype)

        if nk == 1:
            finish(product())
            return
        acc = rest[ne + nr + no]
        k = pl.program_id(2)

        @pl.when(k == 0)
        def _():
            acc[...] = product()

        @pl.when((k > 0) & (k < nk - 1))
        def _():
            acc[...] += product()

        @pl.when(k == nk - 1)
        def _():
            finish(acc[...] + product())

    res = _pcall(
        body, name=name, grid=grid,
        in_specs=[a_spec, b_spec] + [e_spec] * ne + [r_spec] * nr,
        out_specs=[o_spec] * no,
        out_shape=[jax.ShapeDtypeStruct(o_shape, dt) for dt in out_dtypes],
        scratch_shapes=[] if nk == 1 else [pltpu.VMEM((tm, tn), F32)],
        compiler_params=_params(("parallel", "parallel", "arbitrary")),
    )(a, b, *extras, *rowvecs)
    return res[0] if no == 1 else res


def rowwise(fn, tiled, consts=(), outs=(), reds=(), *, ts, name):
    specs = []
    arrs = []
    rows = None
    for t in tiled:
        a, w, cb = t if isinstance(t, tuple) else (t, None, 0)
        arrs.append(a)
        rows = a.shape[-2] if rows is None else rows
        if a.ndim == 2:
            specs.append(pl.BlockSpec((ts, a.shape[1] if w is None else w), lambda i, cb=cb: (i, cb)))
        else:
            specs.append(pl.BlockSpec((a.shape[0], ts, a.shape[2]), lambda i: (0, i, 0)))
    for a in consts:
        specs.append(pl.BlockSpec(a.shape, lambda i, n=a.ndim: (0,) * n))
    o_specs, o_shapes = [], []
    for shp, dt in outs:
        if len(shp) == 2:
            o_specs.append(pl.BlockSpec((ts, shp[1]), lambda i: (i, 0)))
        else:
            o_specs.append(pl.BlockSpec((shp[0], ts, shp[2]), lambda i: (0, i, 0)))
        o_shapes.append(jax.ShapeDtypeStruct(shp, dt))
    for shp in reds:
        o_specs.append(pl.BlockSpec(shp, lambda i, n=len(shp): (0,) * n))
        o_shapes.append(jax.ShapeDtypeStruct(shp, F32))
    nt, nc, no = len(arrs), len(consts), len(outs)

    def body(*refs):
        i = pl.program_id(0)
        red_refs = refs[nt + nc + no:]
        vals = fn(refs[:nt], refs[nt:nt + nc], refs[nt + nc:nt + nc + no])
        if red_refs:
            @pl.when(i == 0)
            def _():
                for r in red_refs:
                    r[...] = jnp.zeros_like(r)
            for r, v in zip(red_refs, vals):
                r[...] += v

    return _pcall(body, name=name, grid=(rows // ts,), in_specs=specs, out_specs=o_specs, out_shape=o_shapes,
                  compiler_params=_params(("arbitrary",)))(*arrs, *consts)


def cast_bf16(w, l, name):
    _, R, C = w.shape
    tr = _tile(R, 512, 16)

    def body(w_ref, o_ref):
        o_ref[...] = w_ref[...].astype(BF16)

    return _pcall(body, name=name, grid=(R // tr,), in_specs=[pl.BlockSpec((None, tr, C), lambda i: (l, i, 0))],
                  out_specs=pl.BlockSpec((tr, C), lambda i: (i, 0)), out_shape=jax.ShapeDtypeStruct((R, C), BF16),
                  compiler_params=_params(("parallel",)))(w)


def prenorm(x, g, scale, shift, name):
    S, D = x.shape

    def fn(t, c, o):
        xv = t[0][...]
        r = _rstd(xv)
        o[0][...] = ((xv * r * c[0][...]) * (1.0 + c[1][...]) + c[2][...]).astype(BF16)
        o[1][...] = r
        return []

    return rowwise(fn, [x], [g, scale, shift], [((S, D), BF16), ((S, 1), F32)], ts=_tile(S, ROW_TILE, 16), name=name)


def gate_bwd(dx, y, gate, name):
    S, D = dx.shape

    def fn(t, c, o):
        d = t[0][...]
        o[0][...] = (d * c[0][...]).astype(BF16)
        return [_colsum(d * t[1][...].astype(F32))]

    return rowwise(fn, [dx, y], [gate], [((S, D), BF16)], [(1, D)], ts=_tile(S, ROW_TILE, 16), name=name)


def norm_bwd(dh, x, rstd, dres, g, scale, name):
    S, D = x.shape

    def fn(t, c, o):
        d = t[0][...]
        r = t[2][...]
        xh = t[1][...] * r
        gv = c[0][...]
        dr = d * (1.0 + c[1][...])
        dx, dg = _rms_bwd(dr, xh, r, gv)
        o[0][...] = t[3][...] + dx
        return [_colsum(d * (xh * gv)), _colsum(d), dg]

    return rowwise(fn, [dh, x, rstd, dres], [g, scale], [((S, D), F32)], [(1, D)] * 3, ts=_tile(S, ROW_TILE, 8), name=name)


def loss_grad(y, tgt, name):
    S, D = y.shape

    def fn(t, c, o):
        e = t[0][...] - t[1][...]
        o[0][...] = e * (1.0 / D)
        return [_colsum(e * e)]

    return rowwise(fn, [y, tgt], outs=[((S, D), F32)], reds=[(1, D)], ts=_tile(S, ROW_TILE, 8), name=name)


def _tril_mask():
    r = lax.broadcasted_iota(jnp.int32, (CHUNK, CHUNK), 0)
    c = lax.broadcasted_iota(jnp.int32, (CHUNK, CHUNK), 1)
    return c <= r


def sgu_fwd(proj, ng, w, bcol, DA, name):
    S = proj.shape[0]
    G = DA // GROUP
    tr = _tile(S, 2 * CHUNK)

    def body(u_ref, v_ref, ng_ref, w_ref, b_ref, o_ref):
        mask = _tril_mask()
        for g in range(G):
            cols = slice(g * GROUP, (g + 1) * GROUP)
            wm = jnp.where(mask, w_ref[g], 0.0).astype(BF16)
            for ci in range(tr // CHUNK):
                rows = slice(ci * CHUNK, (ci + 1) * CHUNK)
                gv = _gelu(v_ref[rows, cols])
                vn = gv * _rstd(gv) * ng_ref[:, cols]
                mixed = jnp.dot(wm, vn.astype(BF16), preferred_element_type=F32) + b_ref[g]
                o_ref[rows, cols] = (_gelu(u_ref[rows, cols]) * mixed).astype(o_ref.dtype)

    return _pcall(
        body, name=name, grid=(S // tr,),
        in_specs=[pl.BlockSpec((tr, DA), lambda i: (i, 0)), pl.BlockSpec((tr, DA), lambda i: (i, 1)),
                  pl.BlockSpec((1, DA), lambda i: (0, 0)), pl.BlockSpec((G, CHUNK, CHUNK), lambda i: (0, 0, 0)),
                  pl.BlockSpec((G, CHUNK, 1), lambda i: (0, 0, 0))],
        out_specs=pl.BlockSpec((tr, DA), lambda i: (i, 0)),
        out_shape=jax.ShapeDtypeStruct((S, DA), BF16),
        compiler_params=_params(("parallel",)),
    )(proj, proj, ng, w, bcol)


def sgu_bwd(proj, dcat, ng, w, bcol, DA, name):
    S = proj.shape[0]
    G = DA // GROUP
    tr = _tile(S, 2 * CHUNK)
    nsteps = S // tr

    def body(u_ref, v_ref, d_ref, ng_ref, w_ref, b_ref, duv_ref, dw_ref, db_ref, dng_ref, dbacc):
        i = pl.program_id(0)

        @pl.when(i == 0)
        def _():
            dw_ref[...] = jnp.zeros_like(dw_ref)
            dng_ref[...] = jnp.zeros_like(dng_ref)
            dbacc[...] = jnp.zeros_like(dbacc)

        mask = _tril_mask()
        for g in range(G):
            cols = slice(g * GROUP, (g + 1) * GROUP)
            wm = jnp.where(mask, w_ref[g], 0.0).astype(BF16)
            ngg = ng_ref[:, cols]
            for ci in range(tr // CHUNK):
                rows = slice(ci * CHUNK, (ci + 1) * CHUNK)
                u, v, d = u_ref[rows, cols], v_ref[rows, cols], d_ref[rows, cols]
                gv = _gelu(v)
                rs = _rstd(gv)
                vhat = gv * rs
                vn = (vhat * ngg).astype(BF16)
                mixed = jnp.dot(wm, vn, preferred_element_type=F32) + b_ref[g]
                dmixed = d * _gelu(u)
                dmb = dmixed.astype(BF16)
                duv_ref[rows, cols] = (d * mixed * _gelu_grad(u)).astype(duv_ref.dtype)
                dwg = lax.dot_general(dmb, vn, _DIMS["nt"], preferred_element_type=F32)
                dw_ref[g] += jnp.where(mask, dwg, 0.0)
                dbacc[g] += dmixed
                dvn = lax.dot_general(wm, dmb, _DIMS["tn"], preferred_element_type=F32)
                dgv, dngg = _rms_bwd(dvn, vhat, rs, ngg)
                dng_ref[:, cols] += dngg
                duv_ref[rows, DA + g * GROUP:DA + (g + 1) * GROUP] = (dgv * _gelu_grad(v)).astype(duv_ref.dtype)

        @pl.when(i == nsteps - 1)
        def _():
            for g in range(G):
                db_ref[g] = jnp.sum(dbacc[g], axis=-1, keepdims=True)

    return _pcall(
        body, name=name, grid=(nsteps,),
        in_specs=[pl.BlockSpec((tr, DA), lambda i: (i, 0)), pl.BlockSpec((tr, DA), lambda i: (i, 1)),
                  pl.BlockSpec((tr, DA), lambda i: (i, 0)),
                  pl.BlockSpec((1, DA), lambda i: (0, 0)), pl.BlockSpec((G, CHUNK, CHUNK), lambda i: (0, 0, 0)),
                  pl.BlockSpec((G, CHUNK, 1), lambda i: (0, 0, 0))],
        out_specs=[pl.BlockSpec((tr, 2 * DA), lambda i: (i, 0)), pl.BlockSpec((G, CHUNK, CHUNK), lambda i: (0, 0, 0)),
                   pl.BlockSpec((G, CHUNK, 1), lambda i: (0, 0, 0)), pl.BlockSpec((1, DA), lambda i: (0, 0))],
        out_shape=[jax.ShapeDtypeStruct((S, 2 * DA), BF16), jax.ShapeDtypeStruct((G, CHUNK, CHUNK), F32),
                   jax.ShapeDtypeStruct((G, CHUNK, 1), F32), jax.ShapeDtypeStruct((1, DA), F32)],
        scratch_shapes=[pltpu.VMEM((G, CHUNK, CHUNK), F32)],
        compiler_params=_params(("arbitrary",)),
    )(proj, proj, dcat, ng, w, bcol)


def _conv_tile(S):
    return _tile(S, 256, 8)


def conv_fwd(proj, wk, bias, DA, DB, name):
    S = proj.shape[0]
    nb = DB // LANES
    a0, g0 = 2 * DA // LANES, (2 * DA + DB) // LANES
    T = _conv_tile(S)
    off = CONV_PAD - (CONV_W - 1)

    def body(a_ref, g_ref, w_ref, b_ref, o_ref, ypad):
        ypad[0:CONV_PAD, :] = jnp.zeros((CONV_PAD, LANES), F32)

        def fill(t, cr):
            r = pl.multiple_of(t * T, T)
            ypad[pl.ds(CONV_PAD + r, T), :] = a_ref[pl.ds(r, T), :] * _sigmoid(g_ref[pl.ds(r, T), :])
            return cr

        lax.fori_loop(0, S // T, fill, 0)

        def step(t, cr):
            r = pl.multiple_of(t * T, T)
            acc = jnp.zeros((T, LANES), F32) + b_ref[...]
            for k in range(CONV_W):
                acc = acc + w_ref[k:k + 1, :] * ypad[pl.ds(r + (k + off), T), :]
            o_ref[pl.ds(r, T), :] = acc
            return cr

        lax.fori_loop(0, S // T, step, 0)

    return _pcall(
        body, name=name, grid=(nb,),
        in_specs=[pl.BlockSpec((S, LANES), lambda j: (0, a0 + j)), pl.BlockSpec((S, LANES), lambda j: (0, g0 + j)),
                  pl.BlockSpec((CONV_PAD, LANES), lambda j: (0, j)), pl.BlockSpec((1, LANES), lambda j: (0, j))],
        out_specs=pl.BlockSpec((S, LANES), lambda j: (0, j)),
        out_shape=jax.ShapeDtypeStruct((S, DB), F32),
        scratch_shapes=[pltpu.VMEM((S + CONV_PAD, LANES), F32)],
        compiler_params=_params(("parallel",)),
    )(proj, proj, wk, bias)


def conv_bwd(proj, dy1, wk, DA, DB, name):
    S = proj.shape[0]
    nb = DB // LANES
    a0, g0 = 2 * DA // LANES, (2 * DA + DB) // LANES
    T = _conv_tile(S)
    off = CONV_PAD - (CONV_W - 1)

    def body(a_ref, g_ref, d_ref, w_ref, da_ref, dg_ref, dw_ref, ypad, dpad, wacc):
        ypad[0:CONV_PAD, :] = jnp.zeros((CONV_PAD, LANES), F32)
        dpad[S:S + CONV_PAD, :] = jnp.zeros((CONV_PAD, LANES), F32)
        wacc[...] = jnp.zeros_like(wacc)

        def fill(t, cr):
            r = pl.multiple_of(t * T, T)
            ypad[pl.ds(CONV_PAD + r, T), :] = a_ref[pl.ds(r, T), :] * _sigmoid(g_ref[pl.ds(r, T), :])
            dpad[pl.ds(r, T), :] = d_ref[pl.ds(r, T), :]
            return cr

        lax.fori_loop(0, S // T, fill, 0)

        def step(t, cr):
            r = pl.multiple_of(t * T, T)
            dt = dpad[pl.ds(r, T), :]
            dy0 = jnp.zeros((T, LANES), F32)
            for k in range(CONV_W):
                prod = dt * ypad[pl.ds(r + (k + off), T), :]
                wacc[k] += jnp.sum(prod.reshape(T // 8, 8, LANES), axis=0)
                dy0 = dy0 + w_ref[k:k + 1, :] * dpad[pl.ds(r + (CONV_W - 1 - k), T), :]
            av, gv = a_ref[pl.ds(r, T), :], g_ref[pl.ds(r, T), :]
            sg = _sigmoid(gv)
            da_ref[pl.ds(r, T), :] = (dy0 * sg).astype(da_ref.dtype)
            dg_ref[pl.ds(r, T), :] = (dy0 * av * sg * (1.0 - sg)).astype(dg_ref.dtype)
            return cr

        lax.fori_loop(0, S // T, step, 0)
        for k in range(CONV_W):
            dw_ref[k:k + 1, :] = jnp.sum(wacc[k], axis=0, keepdims=True)
        dw_ref[CONV_W:CONV_PAD, :] = jnp.zeros((CONV_PAD - CONV_W, LANES), F32)

    return _pcall(
        body, name=name, grid=(nb,),
        in_specs=[pl.BlockSpec((S, LANES), lambda j: (0, a0 + j)), pl.BlockSpec((S, LANES), lambda j: (0, g0 + j)),
                  pl.BlockSpec((S, LANES), lambda j: (0, j)), pl.BlockSpec((CONV_PAD, LANES), lambda j: (0, j))],
        out_specs=[pl.BlockSpec((S, LANES), lambda j: (0, j)), pl.BlockSpec((S, LANES), lambda j: (0, j)),
                   pl.BlockSpec((CONV_PAD, LANES), lambda j: (0, j))],
        out_shape=[jax.ShapeDtypeStruct((S, DB), BF16), jax.ShapeDtypeStruct((S, DB), BF16),
                   jax.ShapeDtypeStruct((CONV_PAD, DB), F32)],
        scratch_shapes=[pltpu.VMEM((S + CONV_PAD, LANES), F32), pltpu.VMEM((S + CONV_PAD, LANES), F32),
                        pltpu.VMEM((CONV_PAD, 8, LANES), F32)],
        compiler_params=_params(("parallel",)),
    )(proj, proj, dy1, wk)


def _ln_stats(y):
    mu = jnp.mean(y, axis=-1, keepdims=True)
    yc = y - mu
    rs = lax.rsqrt(jnp.mean(yc * yc, axis=-1, keepdims=True) + EPS)
    return yc * rs, rs


def ln_silu(y1, lg, lb, name):
    S, DB = y1.shape

    def fn(t, c, o):
        yh, _ = _ln_stats(t[0][...])
        ln = yh * c[0][...] + c[1][...]
        o[0][...] = (ln * _sigmoid(ln)).astype(BF16)
        return []

    return rowwise(fn, [y1], [lg, lb], [((S, DB), BF16)], ts=_tile(S, ROW_TILE, 16), name=name)[0]


def ln_silu_bwd(dcat, y1, lg, lb, name):
    S, DB = y1.shape
    cb = (dcat.shape[1] - DB) // DB

    def fn(t, c, o):
        yh, rs = _ln_stats(t[1][...])
        gv = c[0][...]
        ln = yh * gv + c[1][...]
        sg = _sigmoid(ln)
        dln = t[0][...] * (sg * (1.0 + ln * (1.0 - sg)))
        dyh = dln * gv
        dy = rs * (dyh - jnp.mean(dyh, axis=-1, keepdims=True) - yh * jnp.mean(dyh * yh, axis=-1, keepdims=True))
        o[0][...] = dy
        return [_colsum(dln * yh), _colsum(dln), _colsum(dy)]

    return rowwise(fn, [(dcat, DB, cb), y1], [lg, lb], [((S, DB), F32)], [(1, DB)] * 3, ts=_tile(S, ROW_TILE, 8), name=name)


def mla_norms(proj, cos2, sinm, qg, kvg, kgr, R, name):
    S = proj.shape[0]

    def fn(t, c, o):
        cq = t[0][:, 0:R]
        ckv = t[0][:, R:2 * R]
        kr = t[0][:, 2 * R:2 * R + ROPE]
        o[0][...] = (cq * _rstd(cq) * c[0][...]).astype(BF16)
        o[1][...] = (ckv * _rstd(ckv) * c[1][...]).astype(BF16)
        o[2][...] = _rope(kr * _rstd(kr) * c[2][...], t[1][...], t[2][...])
        return []

    return rowwise(fn, [proj, cos2, sinm], [qg, kvg, kgr], [((S, R), BF16), ((S, R), BF16), ((S, ROPE), F32)],
                   ts=_tile(S, ROW_TILE, 16), name=name)


def mla_heads(q, kv, kr, cos2, sinm, qg, kg, H, name):
    S = q.shape[0]

    def fn(t, c, o):
        cs, sn = t[3][...], t[4][...]
        krv = t[2][...]
        qgn, qgr, kgn = c[0][:, 0:NOPE], c[0][:, NOPE:QK], c[1][:, 0:NOPE]
        for h in range(H):
            qn = t[0][:, QK * h:QK * h + NOPE]
            qr = t[0][:, QK * h + NOPE:QK * (h + 1)]
            o[0][h, :, 0:NOPE] = (qn * _rstd(qn) * qgn).astype(BF16)
            o[0][h, :, NOPE:QK] = _rope(qr * _rstd(qr) * qgr, cs, sn).astype(BF16)
            kn = t[1][:, (NOPE + VDIM) * h:(NOPE + VDIM) * h + NOPE]
            o[1][h, :, 0:NOPE] = (kn * _rstd(kn) * kgn).astype(BF16)
            o[1][h, :, NOPE:QK] = krv.astype(BF16)
            o[2][h] = t[1][:, (NOPE + VDIM) * h + NOPE:(NOPE + VDIM) * (h + 1)].astype(BF16)
        return []

    return rowwise(fn, [q, kv, kr, cos2, sinm], [qg, kg],
                   [((H, S, QK), BF16), ((H, S, QK), BF16), ((H, S, VDIM), BF16)], ts=_tile(S, HEAD_ROWS, 16), name=name)


def mla_heads_bwd(dQ, dK, dV, q, kv, cos2, sinm, qg, kg, H, name):
    S = q.shape[0]
    KV = NOPE + VDIM

    def fn(t, c, o):
        cs, sn = t[5][...], t[6][...]
        qgn, qgr, kgn = c[0][:, 0:NOPE], c[0][:, NOPE:QK], c[1][:, 0:NOPE]
        a_qn = jnp.zeros((1, NOPE), F32)
        a_qr = jnp.zeros((1, ROPE), F32)
        a_kn = jnp.zeros((1, NOPE), F32)
        dkr = jnp.zeros((t[0].shape[1], ROPE), F32)
        for h in range(H):
            qn = t[3][:, QK * h:QK * h + NOPE]
            rs = _rstd(qn)
            dx, dg = _rms_bwd(t[0][h, :, 0:NOPE], qn * rs, rs, qgn)
            o[0][:, QK * h:QK * h + NOPE] = dx.astype(BF16)
            a_qn = a_qn + dg
            qr = t[3][:, QK * h + NOPE:QK * (h + 1)]
            rs = _rstd(qr)
            dx, dg = _rms_bwd(_unrope(t[0][h, :, NOPE:QK], cs, sn), qr * rs, rs, qgr)
            o[0][:, QK * h + NOPE:QK * (h + 1)] = dx.astype(BF16)
            a_qr = a_qr + dg
            kn = t[4][:, KV * h:KV * h + NOPE]
            rs = _rstd(kn)
            dx, dg = _rms_bwd(t[1][h, :, 0:NOPE], kn * rs, rs, kgn)
            o[1][:, KV * h:KV * h + NOPE] = dx.astype(BF16)
            a_kn = a_kn + dg
            o[1][:, KV * h + NOPE:KV * (h + 1)] = t[2][h].astype(BF16)
            dkr = dkr + t[1][h, :, NOPE:QK]
        o[2][...] = _unrope(dkr, cs, sn)
        return [a_qn, a_qr, a_kn]

    return rowwise(fn, [dQ, dK, dV, q, kv, cos2, sinm], [qg, kg],
                   [((S, H * QK), BF16), ((S, H * KV), BF16), ((S, ROPE), F32)],
                   [(1, NOPE), (1, ROPE), (1, NOPE)], ts=_tile(S, HEAD_ROWS, 16), name=name)


def mla_norms_bwd(dqn, dkvn, dkr, proj, qg, kvg, kgr, R, name):
    S = proj.shape[0]

    def fn(t, c, o):
        reds = []
        for idx, (lo, hi) in enumerate(((0, R), (R, 2 * R), (2 * R, 2 * R + ROPE))):
            xv = t[3][:, lo:hi]
            rs = _rstd(xv)
            dx, dg = _rms_bwd(t[idx][...], xv * rs, rs, c[idx][...])
            o[0][:, lo:hi] = dx.astype(BF16)
            reds.append(dg)
        return reds

    return rowwise(fn, [dqn, dkvn, dkr, proj], [qg, kvg, kgr], [((S, 2 * R + ROPE), BF16)],
                   [(1, R), (1, R), (1, ROPE)], ts=_tile(S, ROW_TILE, 16), name=name)


def _tri_rows(p, n):
    qi = 0
    for j in range(1, n):
        qi = qi + (p >= j * (j + 1) // 2).astype(jnp.int32)
    return qi, p - (qi * (qi + 1)) // 2


def _tri_cols(p, n):
    ki = 0
    for j in range(1, n):
        ki = ki + (p >= j * n - j * (j - 1) // 2).astype(jnp.int32)
    return ki, ki + p - (ki * n - (ki * (ki - 1)) // 2)


def attn_fwd(Q, K, V, name):
    H, S, _ = Q.shape
    t = _tile(S, ATTN_BLOCK)
    n = S // t
    scale = QK ** -0.5

    rs = _tile(t, ATTN_STRIP, 8)

    def body(q_ref, k_ref, v_ref, o_ref, lse_ref, m_s, l_s, acc, s_scr, p_scr):
        qi, ki = _tri_rows(pl.program_id(1), n)

        @pl.when(ki == 0)
        def _():
            m_s[...] = jnp.full_like(m_s, NEG)
            l_s[...] = jnp.zeros_like(l_s)
            acc[...] = jnp.zeros_like(acc)

        def block(diagonal):
            s_scr[...] = lax.dot_general(q_ref[...], k_ref[...], _DIMS["nt"], preferred_element_type=F32)

            def strip(i, cr):
                r = slice(i * rs, (i + 1) * rs)
                s = s_scr[r, :] * scale
                if diagonal:
                    row = i * rs + lax.broadcasted_iota(jnp.int32, (rs, t), 0)
                    s = jnp.where(lax.broadcasted_iota(jnp.int32, (rs, t), 1) <= row, s, NEG)
                m_old = m_s[r, :]
                m_new = jnp.maximum(m_old, jnp.max(s, axis=-1, keepdims=True))
                alpha = jnp.exp(m_old - m_new)
                p = jnp.exp(s - m_new)
                l_s[r, :] = alpha * l_s[r, :] + jnp.sum(p, axis=-1, keepdims=True)
                m_s[r, :] = m_new
                acc[r, :] = alpha * acc[r, :]
                p_scr[r, :] = p.astype(BF16)
                return cr

            for i in range(t // rs):
                strip(i, 0)
            acc[...] += jnp.dot(p_scr[...], v_ref[...], preferred_element_type=F32)

        @pl.when(ki < qi)
        def _():
            block(False)

        @pl.when(ki == qi)
        def _():
            block(True)

        @pl.when(ki == qi)
        def _():
            o_ref[...] = (acc[...] / l_s[...]).astype(o_ref.dtype)
            lse_ref[...] = m_s[...] + jnp.log(l_s[...])

    return _pcall(
        body, name=name, grid=(H, n * (n + 1) // 2),
        in_specs=[pl.BlockSpec((None, t, QK), lambda h, p: (h, _tri_rows(p, n)[0], 0)),
                  pl.BlockSpec((None, t, QK), lambda h, p: (h, _tri_rows(p, n)[1], 0)),
                  pl.BlockSpec((None, t, VDIM), lambda h, p: (h, _tri_rows(p, n)[1], 0))],
        out_specs=[pl.BlockSpec((t, VDIM), lambda h, p: (_tri_rows(p, n)[0], h)),
                   pl.BlockSpec((None, t, 1), lambda h, p: (h, _tri_rows(p, n)[0], 0))],
        out_shape=[jax.ShapeDtypeStruct((S, H * VDIM), BF16), jax.ShapeDtypeStruct((H, S, 1), F32)],
        scratch_shapes=[pltpu.VMEM((t, 1), F32), pltpu.VMEM((t, 1), F32), pltpu.VMEM((t, VDIM), F32),
                        pltpu.VMEM((t, t), F32), pltpu.VMEM((t, t), BF16)],
        compiler_params=_params(("parallel", "arbitrary")),
    )(Q, K, V)


def attn_bwd(Q, K, V, dO, O, lse, name):
    H, S, _ = Q.shape
    t = _tile(S, ATTN_BLOCK)
    n = S // t
    scale = QK ** -0.5

    rs = _tile(t, ATTN_STRIP, 8)

    def body(q_ref, k_ref, v_ref, do_ref, o_ref, lse_ref, dq_ref, dk_ref, dv_ref, s_scr, dp_scr, p_scr, ds_scr):
        ki, qi = _tri_cols(pl.program_id(1), n)

        @pl.when(pl.program_id(1) == 0)
        def _():
            dq_ref[...] = jnp.zeros_like(dq_ref)

        @pl.when(qi == ki)
        def _():
            dk_ref[...] = jnp.zeros_like(dk_ref)
            dv_ref[...] = jnp.zeros_like(dv_ref)

        def block(diagonal):
            s_scr[...] = lax.dot_general(q_ref[...], k_ref[...], _DIMS["nt"], preferred_element_type=F32)
            dp_scr[...] = lax.dot_general(do_ref[...], v_ref[...], _DIMS["nt"], preferred_element_type=F32)

            def strip(i, cr):
                r = slice(i * rs, (i + 1) * rs)
                s = s_scr[r, :] * scale
                if diagonal:
                    row = i * rs + lax.broadcasted_iota(jnp.int32, (rs, t), 0)
                    s = jnp.where(lax.broadcasted_iota(jnp.int32, (rs, t), 1) <= row, s, NEG)
                p = jnp.exp(s - lse_ref[r, :])
                delta = jnp.sum(do_ref[r, :].astype(F32) * o_ref[r, :].astype(F32), axis=-1, keepdims=True)
                p_scr[r, :] = p.astype(BF16)
                ds_scr[r, :] = (p * (dp_scr[r, :] - delta) * scale).astype(BF16)
                return cr

            for i in range(t // rs):
                strip(i, 0)
            ds = ds_scr[...]
            dv_ref[...] += lax.dot_general(p_scr[...], do_ref[...], _DIMS["tn"], preferred_element_type=F32)
            dk_ref[...] += lax.dot_general(ds, q_ref[...], _DIMS["tn"], preferred_element_type=F32)
            rq = pl.multiple_of(qi * t, t)
            dq_ref[pl.ds(rq, t), :] += jnp.dot(ds, k_ref[...], preferred_element_type=F32)

        @pl.when(qi > ki)
        def _():
            block(False)

        @pl.when(qi == ki)
        def _():
            block(True)

    qmap = lambda h, p: (h, _tri_cols(p, n)[1], 0)
    kmap = lambda h, p: (h, _tri_cols(p, n)[0], 0)
    return _pcall(
        body, name=name, grid=(H, n * (n + 1) // 2),
        in_specs=[pl.BlockSpec((None, t, QK), qmap),
                  pl.BlockSpec((None, t, QK), kmap),
                  pl.BlockSpec((None, t, VDIM), kmap),
                  pl.BlockSpec((t, VDIM), lambda h, p: (_tri_cols(p, n)[1], h)),
                  pl.BlockSpec((t, VDIM), lambda h, p: (_tri_cols(p, n)[1], h)),
                  pl.BlockSpec((None, t, 1), qmap)],
        out_specs=[pl.BlockSpec((None, S, QK), lambda h, p: (h, 0, 0)),
                   pl.BlockSpec((None, t, QK), kmap),
                   pl.BlockSpec((None, t, VDIM), kmap)],
        out_shape=[jax.ShapeDtypeStruct((H, S, QK), F32), jax.ShapeDtypeStruct((H, S, QK), F32),
                   jax.ShapeDtypeStruct((H, S, VDIM), F32)],
        scratch_shapes=[pltpu.VMEM((t, t), F32), pltpu.VMEM((t, t), F32), pltpu.VMEM((t, t), BF16),
                        pltpu.VMEM((t, t), BF16)],
        compiler_params=_params(("parallel", "arbitrary")),
    )(Q, K, V, dO, O, lse)


def adamw(w, m, v, g, *, row0, name, into=None):
    P, rows, C = g.shape
    tr = _tile(rows, max(16, 131072 // C), 16)
    off = row0 // tr
    assert row0 % tr == 0
    bc1 = 1.0 - ADAM_B1 ** ADAM_STEP
    bc2 = 1.0 - ADAM_B2 ** ADAM_STEP
    chained = into is not None and into is not True

    def body(w_ref, m_ref, v_ref, g_ref, *rest):
        go_ref, d_ref, mo_ref, vo_ref = rest[-4:]
        gs = g_ref[0].astype(F32)
        for p in range(1, P):
            gs = gs + g_ref[p].astype(F32)
        wv = w_ref[...]
        mn = ADAM_B1 * m_ref[...] + (1.0 - ADAM_B1) * gs
        vn = ADAM_B2 * v_ref[...] + (1.0 - ADAM_B2) * (gs * gs)
        go_ref[...] = gs
        mo_ref[...] = mn
        vo_ref[...] = vn
        d_ref[...] = -ADAM_LR * ((mn / bc1) / (jnp.sqrt(vn / bc2) + ADAM_EPS) + ADAM_WD * wv)

    wspec = pl.BlockSpec((tr, C), lambda i: (i + off, 0))
    ospec = wspec if into is not None else pl.BlockSpec((tr, C), lambda i: (i, 0))
    out_rows = w.shape[0] if into is not None else rows
    return _pcall(
        body, name=name, grid=(rows // tr,),
        in_specs=[wspec, wspec, wspec, pl.BlockSpec((P, tr, C), lambda i: (0, i, 0))] + ([_ANY] * 4 if chained else []),
        out_specs=[ospec] * 4, out_shape=[jax.ShapeDtypeStruct((out_rows, C), F32)] * 4,
        input_output_aliases={4 + q: q for q in range(4)} if chained else {},
        compiler_params=_params(("parallel",)),
    )(w, m, v, g, *(into if chained else ()))


def _coords():
    return lax.axis_index("x"), lax.axis_index("y"), lax.axis_index("c")


def _me():
    x, y, c = _coords()
    return 4 * x + 2 * y + c


_ANY = pl.BlockSpec(memory_space=pl.ANY)


def all_gather(items, name):
    n = len(items)
    blks = [a.shape if idx is None else a.shape[1:] for a, idx in items]

    def body(*refs):
        ins, outs = refs[:n], refs[n:2 * n]
        send, recv, lsem = refs[2 * n:]
        x, y, c = _coords()
        me, sib = (x, y, c), (x, y, 1 - c)
        chips = [(1 - x, y), (x, 1 - y), (1 - x, 1 - y)]

        def src(i):
            return ins[i] if items[i][1] is None else ins[i].at[items[i][1]]

        def slot(i, p):
            return outs[i].at[4 * p[0] + 2 * p[1] + p[2]]

        def cp(i, k, block, to, s=None):
            return pltpu.make_async_remote_copy(
                src_ref=slot(i, block) if s is None else s, dst_ref=slot(i, block),
                send_sem=send.at[7 * i + k], recv_sem=recv.at[7 * i + k], device_id=to, device_id_type=MESH)

        mine = [pltpu.make_async_copy(src(i), slot(i, me), lsem.at[i]) for i in range(n)]
        for m_ in mine:
            m_.start()
        first = []
        for i in range(n):
            first.append(cp(i, 0, me, sib, src(i)))
            first += [cp(i, 1 + j, me, (*chip, c), src(i)) for j, chip in enumerate(chips)]
        for f in first:
            f.start()
        passed = []
        for j, chip in enumerate(chips):
            for i in range(n):
                cp(i, 1 + j, (*chip, c), me).wait_recv()
                p_ = cp(i, 4 + j, (*chip, c), sib)
                p_.start()
                passed.append(p_)
        for i in range(n):
            cp(i, 0, sib, me).wait_recv()
            for j, chip in enumerate(chips):
                cp(i, 4 + j, (*chip, 1 - c), me).wait_recv()
        for f in first + passed:
            f.wait_send()
        for m_ in mine:
            m_.wait()

    res = _pcall(
        body, name=name, in_specs=[_ANY] * n, out_specs=[_ANY] * n,
        out_shape=[jax.ShapeDtypeStruct((NDEV,) + tuple(b), a.dtype) for b, (a, _) in zip(blks, items)],
        scratch_shapes=[pltpu.SemaphoreType.DMA((7 * n,)), pltpu.SemaphoreType.DMA((7 * n,)),
                        pltpu.SemaphoreType.DMA((n,))],
    )(*[a for a, _ in items])
    return list(res)


_HBM = pl.BlockSpec(memory_space=pltpu.HBM)
_SEM = pl.BlockSpec(memory_space=pltpu.SEMAPHORE)
_EFFECT = pltpu.SideEffectType.DATAFLOW_SIDE_EFFECTING


def _xchg_copy(src_ref, land_ref, send, recv, r, scatter, at_peer):
    x, y, c = _coords()
    px = jnp.bitwise_xor(x, (r >> 2) & 1)
    py = jnp.bitwise_xor(y, (r >> 1) & 1)
    pc = jnp.bitwise_xor(c, r & 1)
    p_i = 4 * px + 2 * py + pc
    me_i = 4 * x + 2 * y + c
    return pltpu.make_async_remote_copy(
        src_ref=src_ref.at[p_i] if scatter else src_ref, dst_ref=land_ref.at[p_i if at_peer else me_i],
        send_sem=send.at[r - 1], recv_sem=recv.at[r - 1], device_id=(px, py, pc), device_id_type=MESH)


def _phase(body, name, bufs, sems_in=(), new_sems=(), after=None, token=False):
    nb, ns, nn = len(bufs), len(sems_in), len(new_sems)

    def wrapped(*refs):
        outs = refs[nb + ns + (after is not None):]
        body(refs[:nb], refs[nb:nb + ns], outs[:nn])
        if token:
            outs[nn + nb][...] = jnp.zeros_like(outs[nn + nb])

    res = _pcall(
        wrapped, name=name,
        out_shape=tuple([pltpu.SemaphoreType.DMA((k,)) for k in new_sems] + [pltpu.HBM(a.shape, a.dtype) for a in bufs]
                        + ([jax.ShapeDtypeStruct((8, LANES), F32)] if token else [])),
        in_specs=[_HBM] * nb + [_SEM] * ns + ([] if after is None else [_ANY]),
        out_specs=tuple([_SEM] * nn + [_HBM] * nb + ([pl.BlockSpec(memory_space=pltpu.VMEM)] if token else [])),
        input_output_aliases={i: nn + i for i in range(nb)},
        compiler_params=pltpu.CompilerParams(has_side_effects=_EFFECT),
    )(*[pltpu.with_memory_space_constraint(a, pltpu.HBM) for a in bufs], *sems_in, *([] if after is None else [after]))
    return list(res[nn:nn + nb]), list(res[:nn]), (res[nn + nb][0:1, 0:1] if token else None)


def scatter_start(srcs, name, after=None):
    n = len(srcs)
    lands = [lax.empty(s.shape, s.dtype) for s in srcs]

    def body(b, taken, new):
        me_i = _me()
        for i in range(n):
            pltpu.make_async_copy(b[i].at[me_i], b[n + i].at[me_i], new[3 * i + 2].at[0]).start()
            for r in range(1, NDEV):
                _xchg_copy(b[i], b[n + i], new[3 * i], new[3 * i + 1], r, True, False).start()

    bufs, sems, tok = _phase(body, name, list(srcs) + lands, new_sems=[NDEV - 1, NDEV - 1, 1] * n, after=after, token=True)
    return (bufs, sems), tok


def scatter_wait(handle, after, name):
    bufs, sems = handle
    n = len(bufs) // 2

    def body(b, taken, new):
        me_i = _me()
        for i in range(n):
            pltpu.make_async_copy(b[i].at[me_i], b[n + i].at[me_i], taken[3 * i + 2].at[0]).wait()
            for r in range(1, NDEV):
                cp = _xchg_copy(b[i], b[n + i], taken[3 * i], taken[3 * i + 1], r, True, True)
                cp.wait_send()
                cp.wait_recv()

    return _phase(body, name, bufs, sems_in=sems, after=after)[0][n:]


def _gather_peers():
    x, y, c = _coords()
    return (x, y, c), (x, y, 1 - c), [(1 - x, y), (x, 1 - y), (1 - x, 1 - y)]


def _row(p):
    return 4 * p[0] + 2 * p[1] + p[2]


def _gcopy(src_ref, land_ref, send, recv, k, block, to):
    return pltpu.make_async_remote_copy(
        src_ref=land_ref.at[_row(block)] if src_ref is None else src_ref, dst_ref=land_ref.at[_row(block)],
        send_sem=send.at[k], recv_sem=recv.at[k], device_id=to, device_id_type=MESH)


def gather_start(srcs, name, after=None):
    n = len(srcs)
    lands = [lax.empty((NDEV,) + s.shape, s.dtype) for s in srcs]

    def body(b, taken, new):
        me, sib, chips = _gather_peers()
        for i in range(n):
            send, recv = new[3 * i], new[3 * i + 1]
            pltpu.make_async_copy(b[i], b[n + i].at[_row(me)], new[3 * i + 2].at[0]).start()
            for j, chip in enumerate(chips):
                _gcopy(b[i], b[n + i], send, recv, 1 + j, me, (*chip, me[2])).start()
            _gcopy(b[i], b[n + i], send, recv, 0, me, sib).start()

    bufs, sems, tok = _phase(body, name, list(srcs) + lands, new_sems=[4, 4, 1] * n, after=after, token=True)
    return (bufs, sems), tok


def gather_mid(handle, after, name):
    bufs, sems = handle
    n = len(bufs) // 2

    def body(b, taken, new):
        me, sib, chips = _gather_peers()
        for j, chip in enumerate(chips):
            for i in range(n):
                _gcopy(b[i], b[n + i], taken[3 * i], taken[3 * i + 1], 1 + j, (*chip, me[2]), me).wait_recv()
                _gcopy(None, b[n + i], new[2 * i], new[2 * i + 1], j, (*chip, me[2]), sib).start()
        for i in range(n):
            send, recv = taken[3 * i], taken[3 * i + 1]
            _gcopy(b[i], b[n + i], send, recv, 0, sib, me).wait_recv()
            for k in range(4):
                _gcopy(b[i], b[n + i], send, recv, k, me, sib).wait_send()
            pltpu.make_async_copy(b[i], b[n + i].at[_row(me)], taken[3 * i + 2].at[0]).wait()

    bufs, new, tok = _phase(body, name, bufs, sems_in=sems, new_sems=[3, 3] * n, after=after, token=True)
    return (bufs, new), tok


def gather_wait(handle, after, name):
    bufs, sems = handle
    n = len(bufs) // 2

    def body(b, taken, new):
        me, sib, chips = _gather_peers()
        for i in range(n):
            for j, chip in enumerate(chips):
                _gcopy(None, b[n + i], taken[2 * i], taken[2 * i + 1], j, (*chip, me[2]), sib).wait_send()
                _gcopy(None, b[n + i], taken[2 * i], taken[2 * i + 1], j, (*chip, 1 - me[2]), me).wait_recv()

    return _phase(body, name, bufs, sems_in=sems, after=after)[0][n:]


_PACK_ALIGN = 8 * LANES


def _pack(arrs, aligned=False):
    parts = []
    for a in arrs:
        f = a.reshape(-1).astype(F32)
        pad = (-f.shape[0]) % _PACK_ALIGN if aligned else 0
        parts.append(jnp.pad(f, (0, pad)) if pad else f)
    flat = jnp.concatenate(parts)
    pad = (-flat.shape[0]) % _PACK_ALIGN
    return (jnp.pad(flat, (0, pad)) if pad else flat).reshape(-1, LANES)


def _unpack(p, shapes, lead=(), aligned=False):
    nl = len(lead)
    flat = p.reshape(lead + (-1,))
    out, off = [], 0
    for shp in shapes:
        n = 1
        for d in shp:
            n *= d
        out.append(lax.slice_in_dim(flat, off, off + n, axis=nl).reshape(lead + tuple(shp)))
        off += n + ((-n) % _PACK_ALIGN if aligned else 0)
    return out


def _shard_cols(a, me, width, axis):
    return lax.dynamic_slice_in_dim(a, me * width, width, axis=axis)


def kernel(x, c, norm1_g, norm2_g, ada_w, ada_b, mlp_w1, mlp_w2, ab_w_in, sgu_norm_g, sgu_w, sgu_b, conv_w, conv_b, conv_ln_g, conv_ln_b, ab_w_out, mla_w_in, mla_q_norm_g, mla_kv_norm_g, mla_w_uq, mla_w_ukv, mla_q_head_g, mla_k_head_g, mla_w_out, loss_target, m_norm1_g, m_norm2_g, m_ada_w, m_ada_b, m_mlp_w1, m_mlp_w2, m_ab_w_in, m_sgu_norm_g, m_sgu_w, m_sgu_b, m_conv_w, m_conv_b, m_conv_ln_g, m_conv_ln_b, m_ab_w_out, m_mla_w_in, m_mla_q_norm_g, m_mla_kv_norm_g, m_mla_w_uq, m_mla_w_ukv, m_mla_q_head_g, m_mla_k_head_g, m_mla_w_out, v_norm1_g, v_norm2_g, v_ada_w, v_ada_b, v_mlp_w1, v_mlp_w2, v_ab_w_in, v_sgu_norm_g, v_sgu_w, v_sgu_b, v_conv_w, v_conv_b, v_conv_ln_g, v_conv_ln_b, v_ab_w_out, v_mla_w_in, v_mla_q_norm_g, v_mla_kv_norm_g, v_mla_w_uq, v_mla_w_ukv, v_mla_q_head_g, v_mla_k_head_g, v_mla_w_out):
    W = dict(norm1_g=norm1_g, norm2_g=norm2_g, ada_w=ada_w, ada_b=ada_b, mlp_w1=mlp_w1, mlp_w2=mlp_w2, ab_w_in=ab_w_in,
             sgu_norm_g=sgu_norm_g, sgu_w=sgu_w, sgu_b=sgu_b, conv_w=conv_w, conv_b=conv_b, conv_ln_g=conv_ln_g,
             conv_ln_b=conv_ln_b, ab_w_out=ab_w_out, mla_w_in=mla_w_in, mla_q_norm_g=mla_q_norm_g,
             mla_kv_norm_g=mla_kv_norm_g, mla_w_uq=mla_w_uq, mla_w_ukv=mla_w_ukv, mla_q_head_g=mla_q_head_g,
             mla_k_head_g=mla_k_head_g, mla_w_out=mla_w_out)
    M = dict(norm1_g=m_norm1_g, norm2_g=m_norm2_g, ada_w=m_ada_w, ada_b=m_ada_b, mlp_w1=m_mlp_w1, mlp_w2=m_mlp_w2,
             ab_w_in=m_ab_w_in, sgu_norm_g=m_sgu_norm_g, sgu_w=m_sgu_w, sgu_b=m_sgu_b, conv_w=m_conv_w, conv_b=m_conv_b,
             conv_ln_g=m_conv_ln_g, conv_ln_b=m_conv_ln_b, ab_w_out=m_ab_w_out, mla_w_in=m_mla_w_in,
             mla_q_norm_g=m_mla_q_norm_g, mla_kv_norm_g=m_mla_kv_norm_g, mla_w_uq=m_mla_w_uq, mla_w_ukv=m_mla_w_ukv,
             mla_q_head_g=m_mla_q_head_g, mla_k_head_g=m_mla_k_head_g, mla_w_out=m_mla_w_out)
    V = dict(norm1_g=v_norm1_g, norm2_g=v_norm2_g, ada_w=v_ada_w, ada_b=v_ada_b, mlp_w1=v_mlp_w1, mlp_w2=v_mlp_w2,
             ab_w_in=v_ab_w_in, sgu_norm_g=v_sgu_norm_g, sgu_w=v_sgu_w, sgu_b=v_sgu_b, conv_w=v_conv_w, conv_b=v_conv_b,
             conv_ln_g=v_conv_ln_g, conv_ln_b=v_conv_ln_b, ab_w_out=v_ab_w_out, mla_w_in=v_mla_w_in,
             mla_q_norm_g=v_mla_q_norm_g, mla_kv_norm_g=v_mla_kv_norm_g, mla_w_uq=v_mla_w_uq, mla_w_ukv=v_mla_w_ukv,
             mla_q_head_g=v_mla_q_head_g, mla_k_head_g=v_mla_k_head_g, mla_w_out=v_mla_w_out)
    ORDER = list(W)

    S, D = x.shape[1], x.shape[2]
    L, NE, NO = norm1_g.shape[0], ab_w_in.shape[0], mla_w_in.shape[0]
    DA = D // 2
    DB = D - DA
    G = DA // GROUP
    R = NDEV * mla_q_norm_g.shape[1]
    H = NDEV * mla_w_uq.shape[2] // QK
    AW = ada_w.shape[2]
    CB = conv_w.shape[2]
    me = _me()
    xs, tgt = x[0], loss_target[0]

    BIG_EVEN = ("mlp_w1", "mlp_w2", "ab_w_in", "ab_w_out")
    BIG_ODD = ("mlp_w1", "mlp_w2", "mla_w_in", "mla_w_uq", "mla_w_ukv", "mla_w_out")
    BIG = ("mlp_w1", "mlp_w2", "ab_w_in", "ab_w_out", "mla_w_in", "mla_w_uq", "mla_w_ukv", "mla_w_out")
    COL_SHARDED = ("mlp_w1", "ab_w_in", "mla_w_uq", "mla_w_ukv")
    MLP_W = ("mlp_w1", "mlp_w2")
    mixer_w = lambda l: ("ab_w_in", "ab_w_out") if l % 2 == 0 else ("mla_w_in", "mla_w_uq", "mla_w_ukv", "mla_w_out")
    widx = lambda k, l: l if k in MLP_W else l // 2

    small_in = [c, mla_q_norm_g, mla_kv_norm_g, conv_w]
    sg = all_gather([(_pack(small_in, True), None)], "gather_small")[0]
    c_all, qng_all, kvng_all, cw_all = _unpack(sg, [a.shape for a in small_in], (NDEV,), True)
    c_all = c_all.reshape(NDEV, D)
    qng_full = jnp.transpose(qng_all, (1, 0, 2)).reshape(NO, 1, R)
    kvng_full = jnp.transpose(kvng_all, (1, 0, 2)).reshape(NO, 1, R)
    cw_full = jnp.transpose(cw_all, (1, 2, 0, 3)).reshape(NE, CONV_W, DB)
    cw_pad = jnp.pad(cw_full, ((0, 0), (0, CONV_PAD - CONV_W), (0, 0)))

    def silu_fn(t, c_, o):
        v_ = t[0][...]
        o[0][...] = v_ * _sigmoid(v_)
        return []

    c_act = rowwise(silu_fn, [c_all], outs=[((NDEV, D), F32)], ts=NDEV, name="silu_c")[0]
    bias_cols = _shard_cols(ada_b, me, AW, 1).reshape(1, L * AW)
    mod_cols = mm(c_act, ada_w, "nn", name="ada_fwd", b_blocked=True, rowvecs=[bias_cols],
                  epi=lambda acc, b_: (acc + b_,), tm=NDEV, tn=768)
    mod_all = all_gather([(mod_cols, None)], "gather_mod")[0]
    mod = lax.dynamic_index_in_dim(mod_all, me, axis=1, keepdims=False)
    mod = jnp.transpose(mod.reshape(NDEV, L, AW), (1, 0, 2)).reshape(L, 6, 1, D)

    wnames = lambda l, what: mixer_w(l) if what == "mix" else MLP_W
    g_first, g_second = {}, {}
    tok_sum = jnp.zeros((1, 1), F32)
    for l in range(L):
        for what in ("mix", "mlp"):
            srcs = [cast_bf16(W[k], widx(k, l), "cast_%s_l%d" % (k, l)) for k in wnames(l, what)]
            g_first[l, what], tok = gather_start(srcs, "gather_start_%s_l%d" % (what, l), after=mod_all)
            tok_sum = tok_sum + tok
    mod = mod + tok_sum

    def pass_on(l, what, after):
        g_second[l, what], tok = gather_mid(g_first[l, what], after, "gather_mid_%s_l%d" % (what, l))
        return tok

    def wait_weights(l, what, after):
        lands = gather_wait(g_second[l, what], after, "gather_wait_%s_l%d" % (what, l))
        return {k: (ld if k in COL_SHARDED else ld.reshape(NDEV * ld.shape[1], ld.shape[2]))
                for k, ld in zip(wnames(l, what), lands)}

    mod = mod + pass_on(0, "mix", mod)

    pos = jnp.arange(S, dtype=F32)
    inv = ROPE_THETA ** (-jnp.arange(0, ROPE, 2, dtype=F32) / ROPE)
    ang = pos[:, None] * inv[None, :]
    cos2 = jnp.concatenate([jnp.cos(ang), jnp.cos(ang)], axis=1)
    sinm = jnp.concatenate([-jnp.sin(ang), jnp.sin(ang)], axis=1)

    residual = lambda acc, xr, gt: (acc, xr + gt * acc)

    saved = []
    xc = xs
    for l in range(L):
        sh1, sc1, g1, sh2, sc2, g2 = [mod[l, k] for k in range(6)]
        tag = "_l%d" % l
        sv = dict(x0=xc)
        h, sv["rstd1"] = prenorm(xc, norm1_g[l][None], sc1, sh1, "prenorm1" + tag)
        sv["h"] = h
        wl = wait_weights(l, "mix", h)
        if l % 2 == 0:
            e = l // 2
            ng = sgu_norm_g[e].reshape(1, DA)
            bcol = sgu_b[e][:, :, None]
            proj = mm(h, wl["ab_w_in"], "nn", name="ab_in" + tag, b_blocked=True)
            out_a = sgu_fwd(proj, ng, sgu_w[e], bcol, DA, "sgu_fwd" + tag)
            y1 = conv_fwd(proj, cw_pad[e], conv_b[e][None], DA, DB, "conv_fwd" + tag)
            out_b = ln_silu(y1, conv_ln_g[e][None], conv_ln_b[e][None], "ln_silu" + tag)
            cat = jnp.concatenate([out_a, out_b], axis=1)
            sv.update(proj=proj, y1=y1, cat=cat)
            mixb, x1 = mm(cat, wl["ab_w_out"], "nn", name="ab_out" + tag, out_dtypes=(BF16, F32), epi=residual,
                          extras=[xc], rowvecs=[g1])
        else:
            o_ = l // 2
            proj = mm(h, wl["mla_w_in"], "nn", name="mla_in" + tag)
            kgr = mla_k_head_g[o_][None, NOPE:QK]
            qn, kvn, kr = mla_norms(proj, cos2, sinm, qng_full[o_], kvng_full[o_], kgr, R, "mla_norms" + tag)
            q = mm(qn, wl["mla_w_uq"], "nn", name="mla_uq" + tag, b_blocked=True)
            kv = mm(kvn, wl["mla_w_ukv"], "nn", name="mla_ukv" + tag, b_blocked=True)
            Qh, Kh, Vh = mla_heads(q, kv, kr, cos2, sinm, mla_q_head_g[o_][None], mla_k_head_g[o_][None], H,
                                   "mla_heads" + tag)
            att, lse = attn_fwd(Qh, Kh, Vh, "attn_fwd" + tag)
            sv.update(proj=proj, qn=qn, kvn=kvn, q=q, kv=kv, Qh=Qh, Kh=Kh, Vh=Vh, att=att, lse=lse)
            mixb, x1 = mm(att, wl["mla_w_out"], "nn", name="mla_out" + tag, out_dtypes=(BF16, F32), epi=residual,
                          extras=[xc], rowvecs=[g1])
        sv.update(mixb=mixb, x1=x1)
        h2, sv["rstd2"] = prenorm(x1, norm2_g[l][None], sc2 + pass_on(l, "mlp", x1), sh2, "prenorm2" + tag)
        wl.update(wait_weights(l, "mlp", h2))
        sv["w"] = wl
        z, act = mm(h2, wl["mlp_w1"], "nn", name="mlp_up" + tag, b_blocked=True, out_dtypes=(BF16, BF16),
                    epi=lambda acc: (acc, jnp.square(jnp.maximum(acc, 0.0))))
        g2t = g2 + pass_on(l + 1, "mix", z) if l + 1 < L else g2
        yb, xc = mm(act, wl["mlp_w2"], "nn", name="mlp_down" + tag, out_dtypes=(BF16, F32), epi=residual,
                    extras=[x1], rowvecs=[g2t])
        sv.update(h2=h2, z=z, act=act, yb=yb)
        saved.append(sv)

    dx, loss_cols = loss_grad(xc, tgt, "loss")
    loss = lax.psum(0.5 / D * jnp.sum(loss_cols), ("x", "y", "c"))

    big_out = {}
    sm = {k: [None] * W[k].shape[0] for k in ("norm1_g", "norm2_g", "sgu_norm_g", "sgu_w", "sgu_b", "conv_b", "conv_ln_g",
                                               "conv_ln_b", "mla_q_head_g", "mla_k_head_g", "mla_q_norm_g",
                                               "mla_kv_norm_g", "conv_w")}
    dmod = [None] * L
    flat2 = {k: W[k].reshape(-1, W[k].shape[2]) for k in BIG}
    flat2m = {k: M[k].reshape(-1, W[k].shape[2]) for k in BIG}
    flat2v = {k: V[k].reshape(-1, W[k].shape[2]) for k in BIG}

    def send_grads(gr, what, tag, after=None):
        names = list(gr)
        blocks = [gr[k] if k in COL_SHARDED else gr[k].reshape(NDEV, gr[k].shape[0] // NDEV, gr[k].shape[1])
                  for k in names]
        handle, tok = scatter_start(blocks, "scatter_start_%s%s" % (what, tag), after=after)
        return (names, handle, what, tag), tok

    rep = ("norm1_g", "norm2_g", "sgu_norm_g", "sgu_w", "sgu_b", "conv_b", "conv_ln_g", "conv_ln_b", "mla_q_head_g",
           "mla_k_head_g")
    part_full = {"mla_q_norm_g": (NO, R), "mla_kv_norm_g": (NO, R), "conv_w": (NE, CONV_W, DB)}
    small = ["ada_b"] + list(rep) + list(part_full)
    shapes = [(L, 6 * D)] + [W[k].shape for k in rep] + list(part_full.values())

    def small_gather():
        parts = [jnp.stack(dmod).reshape(L, 6 * D)] + [jnp.stack(sm[k]).reshape(s_) for k, s_ in zip(small[1:], shapes[1:])]
        return all_gather([(_pack(parts), None)], "gather_smallgrads")[0]

    def scatter_finish(pending, after, l):
        names, handle, what, tag = pending
        landed = scatter_wait(handle, after, "scatter_wait_%s%s" % (what, tag))
        for k, land in zip(names, landed):
            li = widx(k, l)
            big_out[k] = adamw(flat2[k], flat2m[k], flat2v[k], land, row0=li * W[k].shape[1],
                               name="adamw_%s%s" % (k, tag), into=big_out.get(k, True))

    pend_mix, tok_mix = None, None
    for l in reversed(range(L)):
        sh1, sc1, g1, sh2, sc2, g2 = [mod[l, k] for k in range(6)]
        if tok_mix is not None:
            g2 = g2 + tok_mix
        sv = saved[l]
        wl = sv["w"]
        tag = "_l%d" % l
        gr = {}
        dy, dgate2 = gate_bwd(dx, sv["yb"], g2, "gate2_bwd" + tag)
        dw2 = mm(sv["act"], dy, "tn", name="mlp_down_dw" + tag, out_dtypes=(BF16,))
        pend_w2, tok_w2 = send_grads({"mlp_w2": dw2}, "w2", tag)
        dz = mm(dy, wl["mlp_w2"], "nt", name="mlp_down_dx" + tag, out_dtypes=(BF16,), extras=[sv["z"]],
                rowvecs=[jnp.zeros((1, sv["z"].shape[1]), F32) + tok_w2],
                epi=lambda acc, z_, t_: (acc * (2.0 * jnp.maximum(z_.astype(F32), 0.0)) + t_,))
        dh2 = mm(dz, wl["mlp_w1"], "nt", name="mlp_up_dx" + tag, b_blocked=True)
        dw1 = mm(sv["h2"], dz, "tn", name="mlp_up_dw" + tag, out_dtypes=(BF16,), out_blocked=NDEV)
        dx1, dsc2, dsh2, sm["norm2_g"][l] = norm_bwd(dh2, sv["x1"], sv["rstd2"], dx, norm2_g[l][None], sc2,
                                                     "norm2_bwd" + tag)
        pend_mlp, tok_mlp = send_grads({"mlp_w1": dw1}, "w1", tag)
        if pend_mix is not None:
            scatter_finish(pend_mix, dx1, l + 1)
        dmix, dgate1 = gate_bwd(dx1, sv["mixb"], g1 + tok_mlp, "gate1_bwd" + tag)
        if l % 2 == 0:
            e = l // 2
            ng = sgu_norm_g[e].reshape(1, DA)
            bcol = sgu_b[e][:, :, None]
            dcat = mm(dmix, wl["ab_w_out"], "nt", name="ab_out_dx" + tag)
            gr["ab_w_out"] = mm(sv["cat"], dmix, "tn", name="ab_out_dw" + tag, out_dtypes=(BF16,))
            dy1, sm["conv_ln_g"][e], sm["conv_ln_b"][e], sm["conv_b"][e] = ln_silu_bwd(
                dcat, sv["y1"], conv_ln_g[e][None], conv_ln_b[e][None], "ln_silu_bwd" + tag)
            da, dg_, dwc = conv_bwd(sv["proj"], dy1, cw_pad[e], DA, DB, "conv_bwd" + tag)
            sm["conv_w"][e] = dwc[:CONV_W]
            duv, dsw, dsb, dsng = sgu_bwd(sv["proj"], dcat, ng, sgu_w[e], bcol, DA, "sgu_bwd" + tag)
            sm["sgu_w"][e], sm["sgu_b"][e], sm["sgu_norm_g"][e] = dsw, dsb, dsng
            dproj = jnp.concatenate([duv, da, dg_], axis=1)
            dh = mm(dproj, wl["ab_w_in"], "nt", name="ab_in_dx" + tag, b_blocked=True)
            gr["ab_w_in"] = mm(sv["h"], dproj, "tn", name="ab_in_dw" + tag, out_dtypes=(BF16,), out_blocked=NDEV)
        else:
            o_ = l // 2
            kgr = mla_k_head_g[o_][None, NOPE:QK]
            dO = mm(dmix, wl["mla_w_out"], "nt", name="mla_out_dx" + tag, out_dtypes=(BF16,))
            gr["mla_w_out"] = mm(sv["att"], dmix, "tn", name="mla_out_dw" + tag, out_dtypes=(BF16,))
            dQ, dK, dV = attn_bwd(sv["Qh"], sv["Kh"], sv["Vh"], dO, sv["att"], sv["lse"], "attn_bwd" + tag)
            dq_pre, dkv_pre, dkr, dqgn, dqgr, dkgn = mla_heads_bwd(
                dQ, dK, dV, sv["q"], sv["kv"], cos2, sinm, mla_q_head_g[o_][None], mla_k_head_g[o_][None], H,
                "mla_heads_bwd" + tag)
            dqn = mm(dq_pre, wl["mla_w_uq"], "nt", name="mla_uq_dx" + tag, b_blocked=True)
            gr["mla_w_uq"] = mm(sv["qn"], dq_pre, "tn", name="mla_uq_dw" + tag, out_dtypes=(BF16,), out_blocked=NDEV)
            dkvn = mm(dkv_pre, wl["mla_w_ukv"], "nt", name="mla_ukv_dx" + tag, b_blocked=True)
            gr["mla_w_ukv"] = mm(sv["kvn"], dkv_pre, "tn", name="mla_ukv_dw" + tag, out_dtypes=(BF16,),
                                 out_blocked=NDEV)
            dproj, sm["mla_q_norm_g"][o_], sm["mla_kv_norm_g"][o_], dkgr = mla_norms_bwd(
                dqn, dkvn, dkr, sv["proj"], qng_full[o_], kvng_full[o_], kgr, R, "mla_norms_bwd" + tag)
            sm["mla_q_head_g"][o_] = jnp.concatenate([dqgn, dqgr], axis=1)
            sm["mla_k_head_g"][o_] = jnp.concatenate([dkgn, dkgr], axis=1)
            dh = mm(dproj, wl["mla_w_in"], "nt", name="mla_in_dx" + tag)
            gr["mla_w_in"] = mm(sv["h"], dproj, "tn", name="mla_in_dw" + tag, out_dtypes=(BF16,))
        dx, dsc1, dsh1, sm["norm1_g"][l] = norm_bwd(dh, sv["x0"], sv["rstd1"], dx1, norm1_g[l][None], sc1,
                                                    "norm1_bwd" + tag)
        dmod[l] = jnp.concatenate([dsh1, dsc1, dgate1, dsh2, dsc2, dgate2], axis=1)

        gp = small_gather() if l == 0 else None
        pend_mix, tok_mix = send_grads(gr, "mix", tag, after=gp)
        scatter_finish(pend_w2, dx, l)
        scatter_finish(pend_mlp, dx, l)

    rows_p = gp.shape[1]
    per = 6 * D // LANES
    dm = lax.slice_in_dim(gp, 0, L * per, axis=1).reshape(NDEV, L, per, LANES)
    dmod_cols = lax.dynamic_slice_in_dim(dm, me * (AW // LANES), AW // LANES, axis=2).reshape(NDEV, L * AW) + tok_mix

    def sum_fn(t, c_, o):
        acc = t[0][0]
        for s_ in range(1, NDEV):
            acc = acc + t[0][s_]
        o[0][...] = acc
        return []

    gsummed = rowwise(sum_fn, [gp], outs=[((rows_p, LANES), F32)], ts=_tile(rows_p, 256, 8), name="sum_smallgrads")[0]
    gsum = dict(zip(small, _unpack(gsummed, shapes)))
    gsum["mla_q_norm_g"] = _shard_cols(gsum["mla_q_norm_g"], me, R // NDEV, 1)
    gsum["mla_kv_norm_g"] = _shard_cols(gsum["mla_kv_norm_g"], me, R // NDEV, 1)
    gsum["conv_w"] = _shard_cols(gsum["conv_w"], me, CB, 2)
    sm_shapes = [W[k].shape for k in small]
    sres = adamw(_pack([W[k] for k in small]), _pack([M[k] for k in small]), _pack([V[k] for k in small]),
                 (_pack([gsum[k] for k in small]) + tok_mix)[None], row0=0, name="adamw_small")
    small_out = {k: vals for k, vals in zip(small, zip(*[_unpack(r_, sm_shapes) for r_ in sres]))}

    g_ada = mm(c_act, dmod_cols, "tn", name="ada_dw", out_blocked=L, tm=1024, tn=768, tk=NDEV, cast=None,
               precision=lax.Precision.HIGHEST)
    ada_out = adamw(ada_w.reshape(L * D, AW), m_ada_w.reshape(L * D, AW), v_ada_w.reshape(L * D, AW),
                    g_ada.reshape(1, L * D, AW), row0=0, name="adamw_ada_w")
    scatter_finish(pend_mix, ada_out[1], 0)
    ada_out = [a.reshape(L, D, AW) for a in ada_out]

    def result(k, which):
        if k == "ada_w":
            return ada_out[which]
        if k in BIG:
            return big_out[k][which].reshape(W[k].shape)
        return small_out[k][which]

    outs = [loss, dx[None]]
    for which in range(4):
        outs += [result(k, which) for k in ORDER]
    return tuple(outs)
```

```python
import functools

import jax
import jax.numpy as jnp
from jax import lax
from jax.experimental import pallas as pl
from jax.experimental.pallas import tpu as pltpu

F32 = jnp.float32
BF16 = jnp.bfloat16
EPS = 1e-6
NDEV = 8
LANES = 128
CHUNK = 128
GROUP = 128
CONV_W = 31
CONV_PAD = 32
NOPE, ROPE, VDIM = 128, 64, 128
QK = NOPE + ROPE
ROPE_THETA = 10000.0
VMEM_LIMIT = 56 * 1024 * 1024
ADAM_LR, ADAM_B1, ADAM_B2, ADAM_EPS, ADAM_WD, ADAM_STEP = 0.001, 0.9, 0.999, 1e-08, 0.01, 10
MESH = pl.DeviceIdType.MESH
NEG = -1e30
ROW_TILE = 256
HEAD_ROWS = 256
ATTN_BLOCK = 1024
ATTN_STRIP = 64


def _pcall(body, **kw):
    return pl.pallas_call(body, **kw)


def _params(sem=None):
    return pltpu.CompilerParams(dimension_semantics=sem, vmem_limit_bytes=VMEM_LIMIT)


def _tile(dim, target, align=LANES):
    if dim <= target:
        return dim
    t = (target // align) * align
    while t >= align:
        if dim % t == 0:
            return t
        t -= align
    return dim


def _rstd(x):
    return lax.rsqrt(jnp.mean(x * x, axis=-1, keepdims=True) + EPS)


def _sigmoid(x):
    return 1.0 / (1.0 + jnp.exp(-x))


_GC = 0.7978845608028654


def _gelu(x):
    return 0.5 * x * (1.0 + jnp.tanh(_GC * (x + 0.044715 * x * x * x)))


def _gelu_grad(x):
    t = jnp.tanh(_GC * (x + 0.044715 * x * x * x))
    return 0.5 * (1.0 + t) + 0.5 * x * (1.0 - t * t) * _GC * (1.0 + 3 * 0.044715 * x * x)


def _colsum(x):
    return jnp.sum(x, axis=0, keepdims=True)


def _rms_bwd(dy, xhat, rstd, g):
    dxh = dy * g
    dx = rstd * (dxh - xhat * jnp.mean(dxh * xhat, axis=-1, keepdims=True))
    return dx, _colsum(dy * xhat)


def _swap_halves(x):
    h = x.shape[-1] // 2
    return jnp.concatenate([x[:, h:], x[:, :h]], axis=1)


def _rope(x, cos2, sinm):
    return x * cos2 + _swap_halves(x) * sinm


def _unrope(dy, cos2, sinm):
    return dy * cos2 + _swap_halves(dy * sinm)


_DIMS = {"nn": (((1,), (0,)), ((), ())), "nt": (((1,), (1,)), ((), ())), "tn": (((0,), (0,)), ((), ()))}


def mm(a, b, mode, *, name, out_dtypes=(F32,), epi=None, extras=(), rowvecs=(), b_blocked=False, out_blocked=0,
       tm=1024, tn=1024, tk=2048, precision=None, cast=BF16):
    if mode == "tn":
        K, M = a.shape
    else:
        M, K = a.shape
    if b_blocked:
        J, Rb, Cb = b.shape
        N = Rb if mode == "nt" else J * Cb
    else:
        N = b.shape[0] if mode == "nt" else b.shape[1]
    tm = _tile(M, tm)
    if mode == "nn" and b_blocked:
        tn = _tile(Cb, tn)
    elif out_blocked:
        tn = _tile(N // out_blocked, tn)
    else:
        tn = _tile(N, tn)
    kb = 1
    if mode == "nt" and b_blocked:
        if Cb >= tk:
            tk = _tile(Cb, tk)
        else:
            kb = max(d for d in range(1, J + 1) if J % d == 0 and d * Cb <= tk)
            tk = kb * Cb
    else:
        tk = _tile(K, tk)
    nk = K // tk
    grid = (M // tm, N // tn, nk)

    if mode == "tn":
        a_spec = pl.BlockSpec((tk, tm), lambda i, j, k: (k, i))
    else:
        a_spec = pl.BlockSpec((tm, tk), lambda i, j, k: (i, k))
    if mode == "nn":
        if b_blocked:
            nper = Cb // tn
            b_spec = pl.BlockSpec((None, tk, tn), lambda i, j, k: (j // nper, k, j % nper))
        else:
            b_spec = pl.BlockSpec((tk, tn), lambda i, j, k: (k, j))
    elif mode == "nt":
        if b_blocked:
            if kb > 1:
                b_spec = pl.BlockSpec((kb, tn, Cb), lambda i, j, k: (k, j, 0))
            else:
                kper = Cb // tk
                b_spec = pl.BlockSpec((None, tn, tk), lambda i, j, k: (k // kper, j, k % kper))
        else:
            b_spec = pl.BlockSpec((tn, tk), lambda i, j, k: (j, k))
    else:
        b_spec = pl.BlockSpec((tk, tn), lambda i, j, k: (k, j))
    if out_blocked:
        oper = (N // out_blocked) // tn
        o_spec = pl.BlockSpec((None, tm, tn), lambda i, j, k: (j // oper, i, j % oper))
        o_shape = (out_blocked, M, N // out_blocked)
    else:
        o_spec = pl.BlockSpec((tm, tn), lambda i, j, k: (i, j))
        o_shape = (M, N)
    e_spec = pl.BlockSpec((tm, tn), lambda i, j, k: (i, j))
    r_spec = pl.BlockSpec((1, tn), lambda i, j, k: (0, j))
    ne, nr, no = len(extras), len(rowvecs), len(out_dtypes)
    dims = _DIMS[mode]

    def body(a_ref, b_ref, *rest):
        ex = rest[:ne]
        rv = rest[ne:ne + nr]
        outs = rest[ne + nr:ne + nr + no]

        def product():
            if kb > 1:
                r = None
                for q in range(kb):
                    av, bv = a_ref[:, q * Cb:(q + 1) * Cb], b_ref[q]
                    if cast is not None:
                        av, bv = av.astype(cast), bv.astype(cast)
                    d = lax.dot_general(av, bv, dims, preferred_element_type=F32, precision=precision)
                    r = d if r is None else r + d
                return r
            av, bv = a_ref[...], b_ref[...]
            if cast is not None:
                av, bv = av.astype(cast), bv.astype(cast)
            return lax.dot_general(av, bv, dims, preferred_element_type=F32, precision=precision)

        def finish(r):
            vals = (r,) if epi is None else epi(r, *[e[...] for e in ex], *[v[...] for v in rv])
            for o, val in zip(outs, vals):
                o[...] = val.astype(o.dtype)

        if nk == 1:
            finish(product())
            return
        acc = rest[ne + nr + no]
        k = pl.program_id(2)

        @pl.when(k == 0)
        def _():
            acc[...] = product()

        @pl.when((k > 0) & (k < nk - 1))
        def _():
            acc[...] += product()

        @pl.when(k == nk - 1)
        def _():
            finish(acc[...] + product())

    res = _pcall(
        body, name=name, grid=grid,
        in_specs=[a_spec, b_spec] + [e_spec] * ne + [r_spec] * nr,
        out_specs=[o_spec] * no,
        out_shape=[jax.ShapeDtypeStruct(o_shape, dt) for dt in out_dtypes],
        scratch_shapes=[] if nk == 1 else [pltpu.VMEM((tm, tn), F32)],
        compiler_params=_params(("parallel", "parallel", "arbitrary")),
    )(a, b, *extras, *rowvecs)
    return res[0] if no == 1 else res


def rowwise(fn, tiled, consts=(), outs=(), reds=(), *, ts, name):
    specs = []
    arrs = []
    rows = None
    for t in tiled:
        a, w, cb = t if isinstance(t, tuple) else (t, None, 0)
        arrs.append(a)
        rows = a.shape[-2] if rows is None else rows
        if a.ndim == 2:
            specs.append(pl.BlockSpec((ts, a.shape[1] if w is None else w), lambda i, cb=cb: (i, cb)))
        else:
            specs.append(pl.BlockSpec((a.shape[0], ts, a.shape[2]), lambda i: (0, i, 0)))
    for a in consts:
        specs.append(pl.BlockSpec(a.shape, lambda i, n=a.ndim: (0,) * n))
    o_specs, o_shapes = [], []
    for shp, dt in outs:
        if len(shp) == 2:
            o_specs.append(pl.BlockSpec((ts, shp[1]), lambda i: (i, 0)))
        else:
            o_specs.append(pl.BlockSpec((shp[0], ts, shp[2]), lambda i: (0, i, 0)))
        o_shapes.append(jax.ShapeDtypeStruct(shp, dt))
    for shp in reds:
        o_specs.append(pl.BlockSpec(shp, lambda i, n=len(shp): (0,) * n))
        o_shapes.append(jax.ShapeDtypeStruct(shp, F32))
    nt, nc, no = len(arrs), len(consts), len(outs)

    def body(*refs):
        i = pl.program_id(0)
        red_refs = refs[nt + nc + no:]
        vals = fn(refs[:nt], refs[nt:nt + nc], refs[nt + nc:nt + nc + no])
        if red_refs:
            @pl.when(i == 0)
            def _():
                for r in red_refs:
                    r[...] = jnp.zeros_like(r)
            for r, v in zip(red_refs, vals):
                r[...] += v

    return _pcall(body, name=name, grid=(rows // ts,), in_specs=specs, out_specs=o_specs, out_shape=o_shapes,
                  compiler_params=_params(("arbitrary",)))(*arrs, *consts)


def cast_bf16(w, l, name):
    _, R, C = w.shape
    tr = _tile(R, 512, 16)

    def body(w_ref, o_ref):
        o_ref[...] = w_ref[...].astype(BF16)

    return _pcall(body, name=name, grid=(R // tr,), in_specs=[pl.BlockSpec((None, tr, C), lambda i: (l, i, 0))],
                  out_specs=pl.BlockSpec((tr, C), lambda i: (i, 0)), out_shape=jax.ShapeDtypeStruct((R, C), BF16),
                  compiler_params=_params(("parallel",)))(w)


def prenorm(x, g, scale, shift, name):
    S, D = x.shape

    def fn(t, c, o):
        xv = t[0][...]
        r = _rstd(xv)
        o[0][...] = ((xv * r * c[0][...]) * (1.0 + c[1][...]) + c[2][...]).astype(BF16)
        o[1][...] = r
        return []

    return rowwise(fn, [x], [g, scale, shift], [((S, D), BF16), ((S, 1), F32)], ts=_tile(S, ROW_TILE, 16), name=name)


def _gate_stage(dx, y_ref, gate_ref, dy_ref):
    dy_ref[...] = (dx * gate_ref[...]).astype(BF16)
    return _colsum(dx * y_ref[...].astype(F32))


def norm_bwd(dh, x, rstd, dres, g, scale, name, gated=None):
    S, D = x.shape

    def fn(t, c, o):
        d = t[0][...]
        r = t[2][...]
        xh = t[1][...] * r
        gv = c[0][...]
        dr = d * (1.0 + c[1][...])
        dx, dg = _rms_bwd(dr, xh, r, gv)
        dx = t[3][...] + dx
        o[0][...] = dx
        reds = [_colsum(d * (xh * gv)), _colsum(d), dg]
        if gated is not None:
            reds.append(_gate_stage(dx, t[4], c[2], o[1]))
        return reds

    ng = gated is not None
    res = rowwise(fn, [dh, x, rstd, dres] + ([gated[0]] if ng else []), [g, scale] + ([gated[1]] if ng else []),
                  [((S, D), F32)] + ([((S, D), BF16)] if ng else []), [(1, D)] * (3 + ng), ts=_tile(S, ROW_TILE, 8),
                  name=name)
    return (res[0], res[2], res[3], res[4], res[1], res[5]) if ng else res


def loss_grad(y, tgt, yb, gate, name):
    S, D = y.shape

    def fn(t, c, o):
        e = t[0][...] - t[1][...]
        dx = e * (1.0 / D)
        o[0][...] = dx
        return [_colsum(e * e), _gate_stage(dx, t[2], c[0], o[1])]

    return rowwise(fn, [y, tgt, yb], [gate], outs=[((S, D), F32), ((S, D), BF16)], reds=[(1, D)] * 2,
                   ts=_tile(S, ROW_TILE, 8), name=name)


def _tril_mask():
    r = lax.broadcasted_iota(jnp.int32, (CHUNK, CHUNK), 0)
    c = lax.broadcasted_iota(jnp.int32, (CHUNK, CHUNK), 1)
    return c <= r


def sgu_fwd(proj, ng, w, bcol, DA, name):
    S = proj.shape[0]
    G = DA // GROUP
    tr = _tile(S, 2 * CHUNK)

    def body(u_ref, v_ref, ng_ref, w_ref, b_ref, o_ref):
        mask = _tril_mask()
        for g in range(G):
            cols = slice(g * GROUP, (g + 1) * GROUP)
            wm = jnp.where(mask, w_ref[g], 0.0).astype(BF16)
            for ci in range(tr // CHUNK):
                rows = slice(ci * CHUNK, (ci + 1) * CHUNK)
                gv = _gelu(v_ref[rows, cols])
                vn = gv * _rstd(gv) * ng_ref[:, cols]
                mixed = jnp.dot(wm, vn.astype(BF16), preferred_element_type=F32) + b_ref[g]
                o_ref[rows, cols] = (_gelu(u_ref[rows, cols]) * mixed).astype(o_ref.dtype)

    return _pcall(
        body, name=name, grid=(S // tr,),
        in_specs=[pl.BlockSpec((tr, DA), lambda i: (i, 0)), pl.BlockSpec((tr, DA), lambda i: (i, 1)),
                  pl.BlockSpec((1, DA), lambda i: (0, 0)), pl.BlockSpec((G, CHUNK, CHUNK), lambda i: (0, 0, 0)),
                  pl.BlockSpec((G, CHUNK, 1), lambda i: (0, 0, 0))],
        out_specs=pl.BlockSpec((tr, DA), lambda i: (i, 0)),
        out_shape=jax.ShapeDtypeStruct((S, DA), BF16),
        compiler_params=_params(("parallel",)),
    )(proj, proj, ng, w, bcol)


def sgu_bwd(proj, dcat, ng, w, bcol, DA, name):
    S = proj.shape[0]
    G = DA // GROUP
    tr = _tile(S, 2 * CHUNK)
    nsteps = S // tr

    def body(u_ref, v_ref, d_ref, ng_ref, w_ref, b_ref, duv_ref, dw_ref, db_ref, dng_ref, dbacc):
        i = pl.program_id(0)

        @pl.when(i == 0)
        def _():
            dw_ref[...] = jnp.zeros_like(dw_ref)
            dng_ref[...] = jnp.zeros_like(dng_ref)
            dbacc[...] = jnp.zeros_like(dbacc)

        mask = _tril_mask()
        for g in range(G):
            cols = slice(g * GROUP, (g + 1) * GROUP)
            wm = jnp.where(mask, w_ref[g], 0.0).astype(BF16)
            ngg = ng_ref[:, cols]
            for ci in range(tr // CHUNK):
                rows = slice(ci * CHUNK, (ci + 1) * CHUNK)
                u, v, d = u_ref[rows, cols], v_ref[rows, cols], d_ref[rows, cols]
                gv = _gelu(v)
                rs = _rstd(gv)
                vhat = gv * rs
                vn = (vhat * ngg).astype(BF16)
                mixed = jnp.dot(wm, vn, preferred_element_type=F32) + b_ref[g]
                dmixed = d * _gelu(u)
                dmb = dmixed.astype(BF16)
                duv_ref[rows, cols] = (d * mixed * _gelu_grad(u)).astype(duv_ref.dtype)
                dwg = lax.dot_general(dmb, vn, _DIMS["nt"], preferred_element_type=F32)
                dw_ref[g] += jnp.where(mask, dwg, 0.0)
                dbacc[g] += dmixed
                dvn = lax.dot_general(wm, dmb, _DIMS["tn"], preferred_element_type=F32)
                dgv, dngg = _rms_bwd(dvn, vhat, rs, ngg)
                dng_ref[:, cols] += dngg
                duv_ref[rows, DA + g * GROUP:DA + (g + 1) * GROUP] = (dgv * _gelu_grad(v)).astype(duv_ref.dtype)

        @pl.when(i == nsteps - 1)
        def _():
            for g in range(G):
                db_ref[g] = jnp.sum(dbacc[g], axis=-1, keepdims=True)

    return _pcall(
        body, name=name, grid=(nsteps,),
        in_specs=[pl.BlockSpec((tr, DA), lambda i: (i, 0)), pl.BlockSpec((tr, DA), lambda i: (i, 1)),
                  pl.BlockSpec((tr, DA), lambda i: (i, 0)),
                  pl.BlockSpec((1, DA), lambda i: (0, 0)), pl.BlockSpec((G, CHUNK, CHUNK), lambda i: (0, 0, 0)),
                  pl.BlockSpec((G, CHUNK, 1), lambda i: (0, 0, 0))],
        out_specs=[pl.BlockSpec((tr, 2 * DA), lambda i: (i, 0)), pl.BlockSpec((G, CHUNK, CHUNK), lambda i: (0, 0, 0)),
                   pl.BlockSpec((G, CHUNK, 1), lambda i: (0, 0, 0)), pl.BlockSpec((1, DA), lambda i: (0, 0))],
        out_shape=[jax.ShapeDtypeStruct((S, 2 * DA), BF16), jax.ShapeDtypeStruct((G, CHUNK, CHUNK), F32),
                   jax.ShapeDtypeStruct((G, CHUNK, 1), F32), jax.ShapeDtypeStruct((1, DA), F32)],
        scratch_shapes=[pltpu.VMEM((G, CHUNK, CHUNK), F32)],
        compiler_params=_params(("arbitrary",)),
    )(proj, proj, dcat, ng, w, bcol)


def _conv_tile(S):
    return _tile(S, 256, 8)


def conv_fwd(proj, wk, bias, DA, DB, name):
    S = proj.shape[0]
    nb = DB // LANES
    a0, g0 = 2 * DA // LANES, (2 * DA + DB) // LANES
    T = _conv_tile(S)
    off = CONV_PAD - (CONV_W - 1)

    def body(a_ref, g_ref, w_ref, b_ref, o_ref, ypad):
        ypad[0:CONV_PAD, :] = jnp.zeros((CONV_PAD, LANES), F32)

        def fill(t, cr):
            r = pl.multiple_of(t * T, T)
            ypad[pl.ds(CONV_PAD + r, T), :] = a_ref[pl.ds(r, T), :] * _sigmoid(g_ref[pl.ds(r, T), :])
            return cr

        lax.fori_loop(0, S // T, fill, 0)

        def step(t, cr):
            r = pl.multiple_of(t * T, T)
            acc = jnp.zeros((T, LANES), F32) + b_ref[...]
            for k in range(CONV_W):
                acc = acc + w_ref[k:k + 1, :] * ypad[pl.ds(r + (k + off), T), :]
            o_ref[pl.ds(r, T), :] = acc
            return cr

        lax.fori_loop(0, S // T, step, 0)

    return _pcall(
        body, name=name, grid=(nb,),
        in_specs=[pl.BlockSpec((S, LANES), lambda j: (0, a0 + j)), pl.BlockSpec((S, LANES), lambda j: (0, g0 + j)),
                  pl.BlockSpec((CONV_PAD, LANES), lambda j: (0, j)), pl.BlockSpec((1, LANES), lambda j: (0, j))],
        out_specs=pl.BlockSpec((S, LANES), lambda j: (0, j)),
        out_shape=jax.ShapeDtypeStruct((S, DB), F32),
        scratch_shapes=[pltpu.VMEM((S + CONV_PAD, LANES), F32)],
        compiler_params=_params(("parallel",)),
    )(proj, proj, wk, bias)


def conv_bwd(proj, dy1, wk, DA, DB, name):
    S = proj.shape[0]
    nb = DB // LANES
    a0, g0 = 2 * DA // LANES, (2 * DA + DB) // LANES
    T = _conv_tile(S)
    off = CONV_PAD - (CONV_W - 1)

    def body(a_ref, g_ref, d_ref, w_ref, da_ref, dg_ref, dw_ref, ypad, dpad, wacc):
        ypad[0:CONV_PAD, :] = jnp.zeros((CONV_PAD, LANES), F32)
        dpad[S:S + CONV_PAD, :] = jnp.zeros((CONV_PAD, LANES), F32)
        wacc[...] = jnp.zeros_like(wacc)

        def fill(t, cr):
            r = pl.multiple_of(t * T, T)
            ypad[pl.ds(CONV_PAD + r, T), :] = a_ref[pl.ds(r, T), :] * _sigmoid(g_ref[pl.ds(r, T), :])
            dpad[pl.ds(r, T), :] = d_ref[pl.ds(r, T), :]
            return cr

        lax.fori_loop(0, S // T, fill, 0)

        def step(t, cr):
            r = pl.multiple_of(t * T, T)
            dt = dpad[pl.ds(r, T), :]
            dy0 = jnp.zeros((T, LANES), F32)
            for k in range(CONV_W):
                prod = dt * ypad[pl.ds(r + (k + off), T), :]
                wacc[k] += jnp.sum(prod.reshape(T // 8, 8, LANES), axis=0)
                dy0 = dy0 + w_ref[k:k + 1, :] * dpad[pl.ds(r + (CONV_W - 1 - k), T), :]
            av, gv = a_ref[pl.ds(r, T), :], g_ref[pl.ds(r, T), :]
            sg = _sigmoid(gv)
            da_ref[pl.ds(r, T), :] = (dy0 * sg).astype(da_ref.dtype)
            dg_ref[pl.ds(r, T), :] = (dy0 * av * sg * (1.0 - sg)).astype(dg_ref.dtype)
            return cr

        lax.fori_loop(0, S // T, step, 0)
        for k in range(CONV_W):
            dw_ref[k:k + 1, :] = jnp.sum(wacc[k], axis=0, keepdims=True)
        dw_ref[CONV_W:CONV_PAD, :] = jnp.zeros((CONV_PAD - CONV_W, LANES), F32)

    return _pcall(
        body, name=name, grid=(nb,),
        in_specs=[pl.BlockSpec((S, LANES), lambda j: (0, a0 + j)), pl.BlockSpec((S, LANES), lambda j: (0, g0 + j)),
                  pl.BlockSpec((S, LANES), lambda j: (0, j)), pl.BlockSpec((CONV_PAD, LANES), lambda j: (0, j))],
        out_specs=[pl.BlockSpec((S, LANES), lambda j: (0, j)), pl.BlockSpec((S, LANES), lambda j: (0, j)),
                   pl.BlockSpec((CONV_PAD, LANES), lambda j: (0, j))],
        out_shape=[jax.ShapeDtypeStruct((S, DB), BF16), jax.ShapeDtypeStruct((S, DB), BF16),
                   jax.ShapeDtypeStruct((CONV_PAD, DB), F32)],
        scratch_shapes=[pltpu.VMEM((S + CONV_PAD, LANES), F32), pltpu.VMEM((S + CONV_PAD, LANES), F32),
                        pltpu.VMEM((CONV_PAD, 8, LANES), F32)],
        compiler_params=_params(("parallel",)),
    )(proj, proj, dy1, wk)


def _ln_stats(y):
    mu = jnp.mean(y, axis=-1, keepdims=True)
    yc = y - mu
    rs = lax.rsqrt(jnp.mean(yc * yc, axis=-1, keepdims=True) + EPS)
    return yc * rs, rs


def ln_silu(y1, lg, lb, name):
    S, DB = y1.shape

    def fn(t, c, o):
        yh, _ = _ln_stats(t[0][...])
        ln = yh * c[0][...] + c[1][...]
        o[0][...] = (ln * _sigmoid(ln)).astype(BF16)
        return []

    return rowwise(fn, [y1], [lg, lb], [((S, DB), BF16)], ts=_tile(S, ROW_TILE, 16), name=name)[0]


def ln_silu_bwd(dcat, y1, lg, lb, name):
    S, DB = y1.shape
    cb = (dcat.shape[1] - DB) // DB

    def fn(t, c, o):
        yh, rs = _ln_stats(t[1][...])
        gv = c[0][...]
        ln = yh * gv + c[1][...]
        sg = _sigmoid(ln)
        dln = t[0][...] * (sg * (1.0 + ln * (1.0 - sg)))
        dyh = dln * gv
        dy = rs * (dyh - jnp.mean(dyh, axis=-1, keepdims=True) - yh * jnp.mean(dyh * yh, axis=-1, keepdims=True))
        o[0][...] = dy
        return [_colsum(dln * yh), _colsum(dln), _colsum(dy)]

    return rowwise(fn, [(dcat, DB, cb), y1], [lg, lb], [((S, DB), F32)], [(1, DB)] * 3, ts=_tile(S, ROW_TILE, 8), name=name)


def mla_norms(proj, cos2, sinm, qg, kvg, kgr, R, name):
    S = proj.shape[0]

    def fn(t, c, o):
        cq = t[0][:, 0:R]
        ckv = t[0][:, R:2 * R]
        kr = t[0][:, 2 * R:2 * R + ROPE]
        o[0][...] = (cq * _rstd(cq) * c[0][...]).astype(BF16)
        o[1][...] = (ckv * _rstd(ckv) * c[1][...]).astype(BF16)
        o[2][...] = _rope(kr * _rstd(kr) * c[2][...], t[1][...], t[2][...])
        return []

    return rowwise(fn, [proj, cos2, sinm], [qg, kvg, kgr], [((S, R), BF16), ((S, R), BF16), ((S, ROPE), F32)],
                   ts=_tile(S, ROW_TILE, 16), name=name)


def mla_heads(q, kv, kr, cos2, sinm, qg, kg, H, name):
    S = q.shape[0]

    def fn(t, c, o):
        cs, sn = t[3][...], t[4][...]
        krv = t[2][...]
        qgn, qgr, kgn = c[0][:, 0:NOPE], c[0][:, NOPE:QK], c[1][:, 0:NOPE]
        for h in range(H):
            qn = t[0][:, QK * h:QK * h + NOPE]
            qr = t[0][:, QK * h + NOPE:QK * (h + 1)]
            o[0][h, :, 0:NOPE] = (qn * _rstd(qn) * qgn).astype(BF16)
            o[0][h, :, NOPE:QK] = _rope(qr * _rstd(qr) * qgr, cs, sn).astype(BF16)
            kn = t[1][:, (NOPE + VDIM) * h:(NOPE + VDIM) * h + NOPE]
            o[1][h, :, 0:NOPE] = (kn * _rstd(kn) * kgn).astype(BF16)
            o[1][h, :, NOPE:QK] = krv.astype(BF16)
            o[2][h] = t[1][:, (NOPE + VDIM) * h + NOPE:(NOPE + VDIM) * (h + 1)].astype(BF16)
        return []

    return rowwise(fn, [q, kv, kr, cos2, sinm], [qg, kg],
                   [((H, S, QK), BF16), ((H, S, QK), BF16), ((H, S, VDIM), BF16)], ts=_tile(S, HEAD_ROWS, 16), name=name)


def mla_heads_bwd(dQ, dK, dV, q, kv, cos2, sinm, qg, kg, H, name):
    S = q.shape[0]
    KV = NOPE + VDIM

    def fn(t, c, o):
        cs, sn = t[5][...], t[6][...]
        qgn, qgr, kgn = c[0][:, 0:NOPE], c[0][:, NOPE:QK], c[1][:, 0:NOPE]
        a_qn = jnp.zeros((1, NOPE), F32)
        a_qr = jnp.zeros((1, ROPE), F32)
        a_kn = jnp.zeros((1, NOPE), F32)
        dkr = jnp.zeros((t[0].shape[1], ROPE), F32)
        for h in range(H):
            qn = t[3][:, QK * h:QK * h + NOPE]
            rs = _rstd(qn)
            dx, dg = _rms_bwd(t[0][h, :, 0:NOPE], qn * rs, rs, qgn)
            o[0][:, QK * h:QK * h + NOPE] = dx.astype(BF16)
            a_qn = a_qn + dg
            qr = t[3][:, QK * h + NOPE:QK * (h + 1)]
            rs = _rstd(qr)
            dx, dg = _rms_bwd(_unrope(t[0][h, :, NOPE:QK], cs, sn), qr * rs, rs, qgr)
            o[0][:, QK * h + NOPE:QK * (h + 1)] = dx.astype(BF16)
            a_qr = a_qr + dg
            kn = t[4][:, KV * h:KV * h + NOPE]
            rs = _rstd(kn)
            dx, dg = _rms_bwd(t[1][h, :, 0:NOPE], kn * rs, rs, kgn)
            o[1][:, KV * h:KV * h + NOPE] = dx.astype(BF16)
            a_kn = a_kn + dg
            o[1][:, KV * h + NOPE:KV * (h + 1)] = t[2][h].astype(BF16)
            dkr = dkr + t[1][h, :, NOPE:QK]
        o[2][...] = _unrope(dkr, cs, sn)
        return [a_qn, a_qr, a_kn]

    return rowwise(fn, [dQ, dK, dV, q, kv, cos2, sinm], [qg, kg],
                   [((S, H * QK), BF16), ((S, H * KV), BF16), ((S, ROPE), F32)],
                   [(1, NOPE), (1, ROPE), (1, NOPE)], ts=_tile(S, HEAD_ROWS, 16), name=name)


def mla_norms_bwd(dqn, dkvn, dkr, proj, qg, kvg, kgr, R, name):
    S = proj.shape[0]

    def fn(t, c, o):
        reds = []
        for idx, (lo, hi) in enumerate(((0, R), (R, 2 * R), (2 * R, 2 * R + ROPE))):
            xv = t[3][:, lo:hi]
            rs = _rstd(xv)
            dx, dg = _rms_bwd(t[idx][...], xv * rs, rs, c[idx][...])
            o[0][:, lo:hi] = dx.astype(BF16)
            reds.append(dg)
        return reds

    return rowwise(fn, [dqn, dkvn, dkr, proj], [qg, kvg, kgr], [((S, 2 * R + ROPE), BF16)],
                   [(1, R), (1, R), (1, ROPE)], ts=_tile(S, ROW_TILE, 16), name=name)


def _tri_rows(p, n):
    qi = 0
    for j in range(1, n):
        qi = qi + (p >= j * (j + 1) // 2).astype(jnp.int32)
    return qi, p - (qi * (qi + 1)) // 2


def _tri_cols(p, n):
    ki = 0
    for j in range(1, n):
        ki = ki + (p >= j * n - j * (j - 1) // 2).astype(jnp.int32)
    return ki, ki + p - (ki * n - (ki * (ki - 1)) // 2)


def attn_fwd(Q, K, V, name):
    H, S, _ = Q.shape
    t = _tile(S, ATTN_BLOCK)
    n = S // t
    scale = QK ** -0.5

    rs = _tile(t, ATTN_STRIP, 8)

    def body(q_ref, k_ref, v_ref, o_ref, lse_ref, m_s, l_s, acc, s_scr, p_scr):
        qi, ki = _tri_rows(pl.program_id(1), n)

        @pl.when(ki == 0)
        def _():
            m_s[...] = jnp.full_like(m_s, NEG)
            l_s[...] = jnp.zeros_like(l_s)
            acc[...] = jnp.zeros_like(acc)

        def block(diagonal):
            s_scr[...] = lax.dot_general(q_ref[...], k_ref[...], _DIMS["nt"], preferred_element_type=F32)

            def strip(i, cr):
                r = slice(i * rs, (i + 1) * rs)
                s = s_scr[r, :] * scale
                if diagonal:
                    row = i * rs + lax.broadcasted_iota(jnp.int32, (rs, t), 0)
                    s = jnp.where(lax.broadcasted_iota(jnp.int32, (rs, t), 1) <= row, s, NEG)
                m_old = m_s[r, :]
                m_new = jnp.maximum(m_old, jnp.max(s, axis=-1, keepdims=True))
                alpha = jnp.exp(m_old - m_new)
                p = jnp.exp(s - m_new)
                l_s[r, :] = alpha * l_s[r, :] + jnp.sum(p, axis=-1, keepdims=True)
                m_s[r, :] = m_new
                acc[r, :] = alpha * acc[r, :]
                p_scr[r, :] = p.astype(BF16)
                return cr

            for i in range(t // rs):
                strip(i, 0)
            acc[...] += jnp.dot(p_scr[...], v_ref[...], preferred_element_type=F32)

        @pl.when(ki < qi)
        def _():
            block(False)

        @pl.when(ki == qi)
        def _():
            block(True)

        @pl.when(ki == qi)
        def _():
            o_ref[...] = (acc[...] / l_s[...]).astype(o_ref.dtype)
            lse_ref[...] = m_s[...] + jnp.log(l_s[...])

    return _pcall(
        body, name=name, grid=(H, n * (n + 1) // 2),
        in_specs=[pl.BlockSpec((None, t, QK), lambda h, p: (h, _tri_rows(p, n)[0], 0)),
                  pl.BlockSpec((None, t, QK), lambda h, p: (h, _tri_rows(p, n)[1], 0)),
                  pl.BlockSpec((None, t, VDIM), lambda h, p: (h, _tri_rows(p, n)[1], 0))],
        out_specs=[pl.BlockSpec((t, VDIM), lambda h, p: (_tri_rows(p, n)[0], h)),
                   pl.BlockSpec((None, t, 1), lambda h, p: (h, _tri_rows(p, n)[0], 0))],
        out_shape=[jax.ShapeDtypeStruct((S, H * VDIM), BF16), jax.ShapeDtypeStruct((H, S, 1), F32)],
        scratch_shapes=[pltpu.VMEM((t, 1), F32), pltpu.VMEM((t, 1), F32), pltpu.VMEM((t, VDIM), F32),
                        pltpu.VMEM((t, t), F32), pltpu.VMEM((t, t), BF16)],
        compiler_params=_params(("parallel", "arbitrary")),
    )(Q, K, V)


def attn_bwd(Q, K, V, dO, O, lse, name):
    H, S, _ = Q.shape
    t = _tile(S, ATTN_BLOCK)
    n = S // t
    scale = QK ** -0.5

    rs = _tile(t, ATTN_STRIP, 8)

    def body(q_ref, k_ref, v_ref, do_ref, o_ref, lse_ref, dq_ref, dk_ref, dv_ref, s_scr, dp_scr, p_scr, ds_scr):
        ki, qi = _tri_cols(pl.program_id(1), n)

        @pl.when(pl.program_id(1) == 0)
        def _():
            dq_ref[...] = jnp.zeros_like(dq_ref)

        @pl.when(qi == ki)
        def _():
            dk_ref[...] = jnp.zeros_like(dk_ref)
            dv_ref[...] = jnp.zeros_like(dv_ref)

        def block(diagonal):
            s_scr[...] = lax.dot_general(q_ref[...], k_ref[...], _DIMS["nt"], preferred_element_type=F32)
            dp_scr[...] = lax.dot_general(do_ref[...], v_ref[...], _DIMS["nt"], preferred_element_type=F32)

            def strip(i, cr):
                r = slice(i * rs, (i + 1) * rs)
                s = s_scr[r, :] * scale
                if diagonal:
                    row = i * rs + lax.broadcasted_iota(jnp.int32, (rs, t), 0)
                    s = jnp.where(lax.broadcasted_iota(jnp.int32, (rs, t), 1) <= row, s, NEG)
                p = jnp.exp(s - lse_ref[r, :])
                delta = jnp.sum(do_ref[r, :].astype(F32) * o_ref[r, :].astype(F32), axis=-1, keepdims=True)
                p_scr[r, :] = p.astype(BF16)
                ds_scr[r, :] = (p * (dp_scr[r, :] - delta) * scale).astype(BF16)
                return cr

            for i in range(t // rs):
                strip(i, 0)
            ds = ds_scr[...]
            dv_ref[...] += lax.dot_general(p_scr[...], do_ref[...], _DIMS["tn"], preferred_element_type=F32)
            dk_ref[...] += lax.dot_general(ds, q_ref[...], _DIMS["tn"], preferred_element_type=F32)
            rq = pl.multiple_of(qi * t, t)
            dq_ref[pl.ds(rq, t), :] += jnp.dot(ds, k_ref[...], preferred_element_type=F32)

        @pl.when(qi > ki)
        def _():
            block(False)

        @pl.when(qi == ki)
        def _():
            block(True)

    qmap = lambda h, p: (h, _tri_cols(p, n)[1], 0)
    kmap = lambda h, p: (h, _tri_cols(p, n)[0], 0)
    return _pcall(
        body, name=name, grid=(H, n * (n + 1) // 2),
        in_specs=[pl.BlockSpec((None, t, QK), qmap),
                  pl.BlockSpec((None, t, QK), kmap),
                  pl.BlockSpec((None, t, VDIM), kmap),
                  pl.BlockSpec((t, VDIM), lambda h, p: (_tri_cols(p, n)[1], h)),
                  pl.BlockSpec((t, VDIM), lambda h, p: (_tri_cols(p, n)[1], h)),
                  pl.BlockSpec((None, t, 1), qmap)],
        out_specs=[pl.BlockSpec((None, S, QK), lambda h, p: (h, 0, 0)),
                   pl.BlockSpec((None, t, QK), kmap),
                   pl.BlockSpec((None, t, VDIM), kmap)],
        out_shape=[jax.ShapeDtypeStruct((H, S, QK), F32), jax.ShapeDtypeStruct((H, S, QK), F32),
                   jax.ShapeDtypeStruct((H, S, VDIM), F32)],
        scratch_shapes=[pltpu.VMEM((t, t), F32), pltpu.VMEM((t, t), F32), pltpu.VMEM((t, t), BF16),
                        pltpu.VMEM((t, t), BF16)],
        compiler_params=_params(("parallel", "arbitrary")),
    )(Q, K, V, dO, O, lse)


def adamw(w, m, v, g, *, row0, name, into=None):
    P, rows, C = g.shape
    tr = _tile(rows, max(16, 131072 // C), 16)
    off = row0 // tr
    assert row0 % tr == 0
    bc1 = 1.0 - ADAM_B1 ** ADAM_STEP
    bc2 = 1.0 - ADAM_B2 ** ADAM_STEP
    chained = into is not None and into is not True

    def body(w_ref, m_ref, v_ref, g_ref, *rest):
        go_ref, d_ref, mo_ref, vo_ref = rest[-4:]
        gs = g_ref[0].astype(F32)
        for p in range(1, P):
            gs = gs + g_ref[p].astype(F32)
        wv = w_ref[...]
        mn = ADAM_B1 * m_ref[...] + (1.0 - ADAM_B1) * gs
        vn = ADAM_B2 * v_ref[...] + (1.0 - ADAM_B2) * (gs * gs)
        go_ref[...] = gs
        mo_ref[...] = mn
        vo_ref[...] = vn
        d_ref[...] = -ADAM_LR * ((mn / bc1) / (jnp.sqrt(vn / bc2) + ADAM_EPS) + ADAM_WD * wv)

    wspec = pl.BlockSpec((tr, C), lambda i: (i + off, 0))
    ospec = wspec if into is not None else pl.BlockSpec((tr, C), lambda i: (i, 0))
    out_rows = w.shape[0] if into is not None else rows
    return _pcall(
        body, name=name, grid=(rows // tr,),
        in_specs=[wspec, wspec, wspec, pl.BlockSpec((P, tr, C), lambda i: (0, i, 0))] + ([_ANY] * 4 if chained else []),
        out_specs=[ospec] * 4, out_shape=[jax.ShapeDtypeStruct((out_rows, C), F32)] * 4,
        input_output_aliases={4 + q: q for q in range(4)} if chained else {},
        compiler_params=_params(("parallel",)),
    )(w, m, v, g, *(into if chained else ()))


def _coords():
    return lax.axis_index("x"), lax.axis_index("y"), lax.axis_index("c")


def _me():
    x, y, c = _coords()
    return 4 * x + 2 * y + c


_ANY = pl.BlockSpec(memory_space=pl.ANY)


def all_gather(items, name):
    n = len(items)
    blks = [a.shape if idx is None else a.shape[1:] for a, idx in items]

    def body(*refs):
        ins, outs = refs[:n], refs[n:2 * n]
        send, recv, lsem = refs[2 * n:]
        x, y, c = _coords()
        me, sib = (x, y, c), (x, y, 1 - c)
        chips = [(1 - x, y), (x, 1 - y), (1 - x, 1 - y)]

        def src(i):
            return ins[i] if items[i][1] is None else ins[i].at[items[i][1]]

        def slot(i, p):
            return outs[i].at[4 * p[0] + 2 * p[1] + p[2]]

        def cp(i, k, block, to, s=None):
            return pltpu.make_async_remote_copy(
                src_ref=slot(i, block) if s is None else s, dst_ref=slot(i, block),
                send_sem=send.at[7 * i + k], recv_sem=recv.at[7 * i + k], device_id=to, device_id_type=MESH)

        mine = [pltpu.make_async_copy(src(i), slot(i, me), lsem.at[i]) for i in range(n)]
        for m_ in mine:
            m_.start()
        first = []
        for i in range(n):
            first.append(cp(i, 0, me, sib, src(i)))
            first += [cp(i, 1 + j, me, (*chip, c), src(i)) for j, chip in enumerate(chips)]
        for f in first:
            f.start()
        passed = []
        for j, chip in enumerate(chips):
            for i in range(n):
                cp(i, 1 + j, (*chip, c), me).wait_recv()
                p_ = cp(i, 4 + j, (*chip, c), sib)
                p_.start()
                passed.append(p_)
        for i in range(n):
            cp(i, 0, sib, me).wait_recv()
            for j, chip in enumerate(chips):
                cp(i, 4 + j, (*chip, 1 - c), me).wait_recv()
        for f in first + passed:
            f.wait_send()
        for m_ in mine:
            m_.wait()

    res = _pcall(
        body, name=name, in_specs=[_ANY] * n, out_specs=[_ANY] * n,
        out_shape=[jax.ShapeDtypeStruct((NDEV,) + tuple(b), a.dtype) for b, (a, _) in zip(blks, items)],
        scratch_shapes=[pltpu.SemaphoreType.DMA((7 * n,)), pltpu.SemaphoreType.DMA((7 * n,)),
                        pltpu.SemaphoreType.DMA((n,))],
    )(*[a for a, _ in items])
    return list(res)


_HBM = pl.BlockSpec(memory_space=pltpu.HBM)
_SEM = pl.BlockSpec(memory_space=pltpu.SEMAPHORE)
_EFFECT = pltpu.SideEffectType.DATAFLOW_SIDE_EFFECTING


def _xchg_copy(src_ref, land_ref, send, recv, r, scatter, at_peer):
    x, y, c = _coords()
    px = jnp.bitwise_xor(x, (r >> 2) & 1)
    py = jnp.bitwise_xor(y, (r >> 1) & 1)
    pc = jnp.bitwise_xor(c, r & 1)
    p_i = 4 * px + 2 * py + pc
    me_i = 4 * x + 2 * y + c
    return pltpu.make_async_remote_copy(
        src_ref=src_ref.at[p_i] if scatter else src_ref, dst_ref=land_ref.at[p_i if at_peer else me_i],
        send_sem=send.at[r - 1], recv_sem=recv.at[r - 1], device_id=(px, py, pc), device_id_type=MESH)


def _phase(body, name, bufs, sems_in=(), new_sems=(), after=None, token=False):
    nb, ns, nn = len(bufs), len(sems_in), len(new_sems)

    def wrapped(*refs):
        outs = refs[nb + ns + (after is not None):]
        body(refs[:nb], refs[nb:nb + ns], outs[:nn])
        if token:
            outs[nn + nb][...] = jnp.zeros_like(outs[nn + nb])

    res = _pcall(
        wrapped, name=name,
        out_shape=tuple([pltpu.SemaphoreType.DMA((k,)) for k in new_sems] + [pltpu.HBM(a.shape, a.dtype) for a in bufs]
                        + ([jax.ShapeDtypeStruct((8, LANES), F32)] if token else [])),
        in_specs=[_HBM] * nb + [_SEM] * ns + ([] if after is None else [_ANY]),
        out_specs=tuple([_SEM] * nn + [_HBM] * nb + ([pl.BlockSpec(memory_space=pltpu.VMEM)] if token else [])),
        input_output_aliases={i: nn + i for i in range(nb)},
        compiler_params=pltpu.CompilerParams(has_side_effects=_EFFECT),
    )(*[pltpu.with_memory_space_constraint(a, pltpu.HBM) for a in bufs], *sems_in, *([] if after is None else [after]))
    return list(res[nn:nn + nb]), list(res[:nn]), (res[nn + nb][0:1, 0:1] if token else None)


def scatter_start(srcs, name, after=None):
    n = len(srcs)
    lands = [lax.empty(s.shape, s.dtype) for s in srcs]

    def body(b, taken, new):
        me_i = _me()
        for i in range(n):
            pltpu.make_async_copy(b[i].at[me_i], b[n + i].at[me_i], new[3 * i + 2].at[0]).start()
            for r in range(1, NDEV):
                _xchg_copy(b[i], b[n + i], new[3 * i], new[3 * i + 1], r, True, False).start()

    bufs, sems, tok = _phase(body, name, list(srcs) + lands, new_sems=[NDEV - 1, NDEV - 1, 1] * n, after=after, token=True)
    return (bufs, sems), tok


def scatter_wait(handle, after, name):
    bufs, sems = handle
    n = len(bufs) // 2

    def body(b, taken, new):
        me_i = _me()
        for i in range(n):
            pltpu.make_async_copy(b[i].at[me_i], b[n + i].at[me_i], taken[3 * i + 2].at[0]).wait()
            for r in range(1, NDEV):
                cp = _xchg_copy(b[i], b[n + i], taken[3 * i], taken[3 * i + 1], r, True, True)
                cp.wait_send()
                cp.wait_recv()

    return _phase(body, name, bufs, sems_in=sems, after=after)[0][n:]


def _gather_peers():
    x, y, c = _coords()
    return (x, y, c), (x, y, 1 - c), [(1 - x, y), (x, 1 - y), (1 - x, 1 - y)]


def _row(p):
    return 4 * p[0] + 2 * p[1] + p[2]


def _gcopy(src_ref, land_ref, send, recv, k, block, to):
    return pltpu.make_async_remote_copy(
        src_ref=land_ref.at[_row(block)] if src_ref is None else src_ref, dst_ref=land_ref.at[_row(block)],
        send_sem=send.at[k], recv_sem=recv.at[k], device_id=to, device_id_type=MESH)


def gather_start(srcs, name, after=None):
    n = len(srcs)
    lands = [lax.empty((NDEV,) + s.shape, s.dtype) for s in srcs]

    def body(b, taken, new):
        me, sib, chips = _gather_peers()
        for i in range(n):
            send, recv = new[3 * i], new[3 * i + 1]
            pltpu.make_async_copy(b[i], b[n + i].at[_row(me)], new[3 * i + 2].at[0]).start()
            for j, chip in enumerate(chips):
                _gcopy(b[i], b[n + i], send, recv, 1 + j, me, (*chip, me[2])).start()
            _gcopy(b[i], b[n + i], send, recv, 0, me, sib).start()

    bufs, sems, tok = _phase(body, name, list(srcs) + lands, new_sems=[4, 4, 1] * n, after=after, token=True)
    return (bufs, sems), tok


def gather_mid(handle, after, name):
    bufs, sems = handle
    n = len(bufs) // 2

    def body(b, taken, new):
        me, sib, chips = _gather_peers()
        for j, chip in enumerate(chips):
            for i in range(n):
                _gcopy(b[i], b[n + i], taken[3 * i], taken[3 * i + 1], 1 + j, (*chip, me[2]), me).wait_recv()
                _gcopy(None, b[n + i], new[2 * i], new[2 * i + 1], j, (*chip, me[2]), sib).start()
        for i in range(n):
            send, recv = taken[3 * i], taken[3 * i + 1]
            _gcopy(b[i], b[n + i], send, recv, 0, sib, me).wait_recv()
            for k in range(4):
                _gcopy(b[i], b[n + i], send, recv, k, me, sib).wait_send()
            pltpu.make_async_copy(b[i], b[n + i].at[_row(me)], taken[3 * i + 2].at[0]).wait()

    bufs, new, tok = _phase(body, name, bufs, sems_in=sems, new_sems=[3, 3] * n, after=after, token=True)
    return (bufs, new), tok


def gather_wait(handle, after, name):
    bufs, sems = handle
    n = len(bufs) // 2

    def body(b, taken, new):
        me, sib, chips = _gather_peers()
        for i in range(n):
            for j, chip in enumerate(chips):
                _gcopy(None, b[n + i], taken[2 * i], taken[2 * i + 1], j, (*chip, me[2]), sib).wait_send()
                _gcopy(None, b[n + i], taken[2 * i], taken[2 * i + 1], j, (*chip, 1 - me[2]), me).wait_recv()

    return _phase(body, name, bufs, sems_in=sems, after=after)[0][n:]


_PACK_ALIGN = 8 * LANES


def _pack(arrs, aligned=False):
    parts = []
    for a in arrs:
        f = a.reshape(-1).astype(F32)
        pad = (-f.shape[0]) % _PACK_ALIGN if aligned else 0
        parts.append(jnp.pad(f, (0, pad)) if pad else f)
    flat = jnp.concatenate(parts)
    pad = (-flat.shape[0]) % _PACK_ALIGN
    return (jnp.pad(flat, (0, pad)) if pad else flat).reshape(-1, LANES)


def _unpack(p, shapes, lead=(), aligned=False):
    nl = len(lead)
    flat = p.reshape(lead + (-1,))
    out, off = [], 0
    for shp in shapes:
        n = 1
        for d in shp:
            n *= d
        out.append(lax.slice_in_dim(flat, off, off + n, axis=nl).reshape(lead + tuple(shp)))
        off += n + ((-n) % _PACK_ALIGN if aligned else 0)
    return out


def _shard_cols(a, me, width, axis):
    return lax.dynamic_slice_in_dim(a, me * width, width, axis=axis)


def kernel(x, c, norm1_g, norm2_g, ada_w, ada_b, mlp_w1, mlp_w2, ab_w_in, sgu_norm_g, sgu_w, sgu_b, conv_w, conv_b, conv_ln_g, conv_ln_b, ab_w_out, mla_w_in, mla_q_norm_g, mla_kv_norm_g, mla_w_uq, mla_w_ukv, mla_q_head_g, mla_k_head_g, mla_w_out, loss_target, m_norm1_g, m_norm2_g, m_ada_w, m_ada_b, m_mlp_w1, m_mlp_w2, m_ab_w_in, m_sgu_norm_g, m_sgu_w, m_sgu_b, m_conv_w, m_conv_b, m_conv_ln_g, m_conv_ln_b, m_ab_w_out, m_mla_w_in, m_mla_q_norm_g, m_mla_kv_norm_g, m_mla_w_uq, m_mla_w_ukv, m_mla_q_head_g, m_mla_k_head_g, m_mla_w_out, v_norm1_g, v_norm2_g, v_ada_w, v_ada_b, v_mlp_w1, v_mlp_w2, v_ab_w_in, v_sgu_norm_g, v_sgu_w, v_sgu_b, v_conv_w, v_conv_b, v_conv_ln_g, v_conv_ln_b, v_ab_w_out, v_mla_w_in, v_mla_q_norm_g, v_mla_kv_norm_g, v_mla_w_uq, v_mla_w_ukv, v_mla_q_head_g, v_mla_k_head_g, v_mla_w_out):
    W = dict(norm1_g=norm1_g, norm2_g=norm2_g, ada_w=ada_w, ada_b=ada_b, mlp_w1=mlp_w1, mlp_w2=mlp_w2, ab_w_in=ab_w_in,
             sgu_norm_g=sgu_norm_g, sgu_w=sgu_w, sgu_b=sgu_b, conv_w=conv_w, conv_b=conv_b, conv_ln_g=conv_ln_g,
             conv_ln_b=conv_ln_b, ab_w_out=ab_w_out, mla_w_in=mla_w_in, mla_q_norm_g=mla_q_norm_g,
             mla_kv_norm_g=mla_kv_norm_g, mla_w_uq=mla_w_uq, mla_w_ukv=mla_w_ukv, mla_q_head_g=mla_q_head_g,
             mla_k_head_g=mla_k_head_g, mla_w_out=mla_w_out)
    M = dict(norm1_g=m_norm1_g, norm2_g=m_norm2_g, ada_w=m_ada_w, ada_b=m_ada_b, mlp_w1=m_mlp_w1, mlp_w2=m_mlp_w2,
             ab_w_in=m_ab_w_in, sgu_norm_g=m_sgu_norm_g, sgu_w=m_sgu_w, sgu_b=m_sgu_b, conv_w=m_conv_w, conv_b=m_conv_b,
             conv_ln_g=m_conv_ln_g, conv_ln_b=m_conv_ln_b, ab_w_out=m_ab_w_out, mla_w_in=m_mla_w_in,
             mla_q_norm_g=m_mla_q_norm_g, mla_kv_norm_g=m_mla_kv_norm_g, mla_w_uq=m_mla_w_uq, mla_w_ukv=m_mla_w_ukv,
             mla_q_head_g=m_mla_q_head_g, mla_k_head_g=m_mla_k_head_g, mla_w_out=m_mla_w_out)
    V = dict(norm1_g=v_norm1_g, norm2_g=v_norm2_g, ada_w=v_ada_w, ada_b=v_ada_b, mlp_w1=v_mlp_w1, mlp_w2=v_mlp_w2,
             ab_w_in=v_ab_w_in, sgu_norm_g=v_sgu_norm_g, sgu_w=v_sgu_w, sgu_b=v_sgu_b, conv_w=v_conv_w, conv_b=v_conv_b,
             conv_ln_g=v_conv_ln_g, conv_ln_b=v_conv_ln_b, ab_w_out=v_ab_w_out, mla_w_in=v_mla_w_in,
             mla_q_norm_g=v_mla_q_norm_g, mla_kv_norm_g=v_mla_kv_norm_g, mla_w_uq=v_mla_w_uq, mla_w_ukv=v_mla_w_ukv,
             mla_q_head_g=v_mla_q_head_g, mla_k_head_g=v_mla_k_head_g, mla_w_out=v_mla_w_out)
    ORDER = list(W)

    S, D = x.shape[1], x.shape[2]
    L, NE, NO = norm1_g.shape[0], ab_w_in.shape[0], mla_w_in.shape[0]
    DA = D // 2
    DB = D - DA
    G = DA // GROUP
    R = NDEV * mla_q_norm_g.shape[1]
    H = NDEV * mla_w_uq.shape[2] // QK
    AW = ada_w.shape[2]
    CB = conv_w.shape[2]
    me = _me()
    xs, tgt = x[0], loss_target[0]

    BIG_EVEN = ("mlp_w1", "mlp_w2", "ab_w_in", "ab_w_out")
    BIG_ODD = ("mlp_w1", "mlp_w2", "mla_w_in", "mla_w_uq", "mla_w_ukv", "mla_w_out")
    BIG = ("mlp_w1", "mlp_w2", "ab_w_in", "ab_w_out", "mla_w_in", "mla_w_uq", "mla_w_ukv", "mla_w_out")
    COL_SHARDED = ("mlp_w1", "ab_w_in", "mla_w_uq", "mla_w_ukv")
    MLP_W = ("mlp_w1", "mlp_w2")
    mixer_w = lambda l: ("ab_w_in", "ab_w_out") if l % 2 == 0 else ("mla_w_in", "mla_w_uq", "mla_w_ukv", "mla_w_out")
    widx = lambda k, l: l if k in MLP_W else l // 2

    small_in = [c, mla_q_norm_g, mla_kv_norm_g, conv_w]
    sg = all_gather([(_pack(small_in, True), None)], "gather_small")[0]
    c_all, qng_all, kvng_all, cw_all = _unpack(sg, [a.shape for a in small_in], (NDEV,), True)
    c_all = c_all.reshape(NDEV, D)
    qng_full = jnp.transpose(qng_all, (1, 0, 2)).reshape(NO, 1, R)
    kvng_full = jnp.transpose(kvng_all, (1, 0, 2)).reshape(NO, 1, R)
    cw_full = jnp.transpose(cw_all, (1, 2, 0, 3)).reshape(NE, CONV_W, DB)
    cw_pad = jnp.pad(cw_full, ((0, 0), (0, CONV_PAD - CONV_W), (0, 0)))

    def silu_fn(t, c_, o):
        v_ = t[0][...]
        o[0][...] = v_ * _sigmoid(v_)
        return []

    c_act = rowwise(silu_fn, [c_all], outs=[((NDEV, D), F32)], ts=NDEV, name="silu_c")[0]
    bias_cols = _shard_cols(ada_b, me, AW, 1).reshape(1, L * AW)
    mod_cols = mm(c_act, ada_w, "nn", name="ada_fwd", b_blocked=True, rowvecs=[bias_cols],
                  epi=lambda acc, b_: (acc + b_,), tm=NDEV, tn=768)
    mod_all = all_gather([(mod_cols, None)], "gather_mod")[0]
    mod = lax.dynamic_index_in_dim(mod_all, me, axis=1, keepdims=False)
    mod = jnp.transpose(mod.reshape(NDEV, L, AW), (1, 0, 2)).reshape(L, 6, 1, D)

    wnames = lambda l, what: mixer_w(l) if what == "mix" else MLP_W
    g_first, g_second = {}, {}
    tok_sum = jnp.zeros((1, 1), F32)
    for l in range(L):
        for what in ("mix", "mlp"):
            srcs = [cast_bf16(W[k], widx(k, l), "cast_%s_l%d" % (k, l)) for k in wnames(l, what)]
            g_first[l, what], tok = gather_start(srcs, "gather_start_%s_l%d" % (what, l), after=mod_all)
            tok_sum = tok_sum + tok
    mod = mod + tok_sum

    def pass_on(l, what, after):
        g_second[l, what], tok = gather_mid(g_first[l, what], after, "gather_mid_%s_l%d" % (what, l))
        return tok

    def wait_weights(l, what, after):
        lands = gather_wait(g_second[l, what], after, "gather_wait_%s_l%d" % (what, l))
        return {k: (ld if k in COL_SHARDED else ld.reshape(NDEV * ld.shape[1], ld.shape[2]))
                for k, ld in zip(wnames(l, what), lands)}

    mod = mod + pass_on(0, "mix", mod)

    pos = jnp.arange(S, dtype=F32)
    inv = ROPE_THETA ** (-jnp.arange(0, ROPE, 2, dtype=F32) / ROPE)
    ang = pos[:, None] * inv[None, :]
    cos2 = jnp.concatenate([jnp.cos(ang), jnp.cos(ang)], axis=1)
    sinm = jnp.concatenate([-jnp.sin(ang), jnp.sin(ang)], axis=1)

    residual = lambda acc, xr, gt: (acc, xr + gt * acc)

    saved = []
    xc = xs
    for l in range(L):
        sh1, sc1, g1, sh2, sc2, g2 = [mod[l, k] for k in range(6)]
        tag = "_l%d" % l
        sv = dict(x0=xc)
        h, sv["rstd1"] = prenorm(xc, norm1_g[l][None], sc1, sh1, "prenorm1" + tag)
        sv["h"] = h
        wl = wait_weights(l, "mix", h)
        if l % 2 == 0:
            e = l // 2
            ng = sgu_norm_g[e].reshape(1, DA)
            bcol = sgu_b[e][:, :, None]
            proj = mm(h, wl["ab_w_in"], "nn", name="ab_in" + tag, b_blocked=True)
            out_a = sgu_fwd(proj, ng, sgu_w[e], bcol, DA, "sgu_fwd" + tag)
            y1 = conv_fwd(proj, cw_pad[e], conv_b[e][None], DA, DB, "conv_fwd" + tag)
            out_b = ln_silu(y1, conv_ln_g[e][None], conv_ln_b[e][None], "ln_silu" + tag)
            cat = jnp.concatenate([out_a, out_b], axis=1)
            sv.update(proj=proj, y1=y1, cat=cat)
            mixb, x1 = mm(cat, wl["ab_w_out"], "nn", name="ab_out" + tag, out_dtypes=(BF16, F32), epi=residual,
                          extras=[xc], rowvecs=[g1])
        else:
            o_ = l // 2
            proj = mm(h, wl["mla_w_in"], "nn", name="mla_in" + tag)
            kgr = mla_k_head_g[o_][None, NOPE:QK]
            qn, kvn, kr = mla_norms(proj, cos2, sinm, qng_full[o_], kvng_full[o_], kgr, R, "mla_norms" + tag)
            q = mm(qn, wl["mla_w_uq"], "nn", name="mla_uq" + tag, b_blocked=True)
            kv = mm(kvn, wl["mla_w_ukv"], "nn", name="mla_ukv" + tag, b_blocked=True)
            Qh, Kh, Vh = mla_heads(q, kv, kr, cos2, sinm, mla_q_head_g[o_][None], mla_k_head_g[o_][None], H,
                                   "mla_heads" + tag)
            att, lse = attn_fwd(Qh, Kh, Vh, "attn_fwd" + tag)
            sv.update(proj=proj, qn=qn, kvn=kvn, q=q, kv=kv, Qh=Qh, Kh=Kh, Vh=Vh, att=att, lse=lse)
            mixb, x1 = mm(att, wl["mla_w_out"], "nn", name="mla_out" + tag, out_dtypes=(BF16, F32), epi=residual,
                          extras=[xc], rowvecs=[g1])
        sv.update(mixb=mixb, x1=x1)
        h2, sv["rstd2"] = prenorm(x1, norm2_g[l][None], sc2 + pass_on(l, "mlp", x1), sh2, "prenorm2" + tag)
        wl.update(wait_weights(l, "mlp", h2))
        sv["w"] = wl
        z, act = mm(h2, wl["mlp_w1"], "nn", name="mlp_up" + tag, b_blocked=True, out_dtypes=(BF16, BF16),
                    epi=lambda acc: (acc, jnp.square(jnp.maximum(acc, 0.0))))
        g2t = g2 + pass_on(l + 1, "mix", z) if l + 1 < L else g2
        yb, xc = mm(act, wl["mlp_w2"], "nn", name="mlp_down" + tag, out_dtypes=(BF16, F32), epi=residual,
                    extras=[x1], rowvecs=[g2t])
        sv.update(h2=h2, z=z, act=act, yb=yb)
        saved.append(sv)

    dx, dy, loss_cols, dgate2 = loss_grad(xc, tgt, saved[L - 1]["yb"], mod[L - 1, 5], "loss")
    loss = lax.psum(0.5 / D * jnp.sum(loss_cols), ("x", "y", "c"))

    big_out = {}
    sm = {k: [None] * W[k].shape[0] for k in ("norm1_g", "norm2_g", "sgu_norm_g", "sgu_w", "sgu_b", "conv_b", "conv_ln_g",
                                               "conv_ln_b", "mla_q_head_g", "mla_k_head_g", "mla_q_norm_g",
                                               "mla_kv_norm_g", "conv_w")}
    dmod = [None] * L
    flat2 = {k: W[k].reshape(-1, W[k].shape[2]) for k in BIG}
    flat2m = {k: M[k].reshape(-1, W[k].shape[2]) for k in BIG}
    flat2v = {k: V[k].reshape(-1, W[k].shape[2]) for k in BIG}

    def send_grads(gr, what, tag, after=None):
        names = list(gr)
        blocks = [gr[k] if k in COL_SHARDED else gr[k].reshape(NDEV, gr[k].shape[0] // NDEV, gr[k].shape[1])
                  for k in names]
        handle, tok = scatter_start(blocks, "scatter_start_%s%s" % (what, tag), after=after)
        return (names, handle, what, tag), tok

    rep = ("norm1_g", "norm2_g", "sgu_norm_g", "sgu_w", "sgu_b", "conv_b", "conv_ln_g", "conv_ln_b", "mla_q_head_g",
           "mla_k_head_g")
    part_full = {"mla_q_norm_g": (NO, R), "mla_kv_norm_g": (NO, R), "conv_w": (NE, CONV_W, DB)}
    small = ["ada_b"] + list(rep) + list(part_full)
    shapes = [(L, 6 * D)] + [W[k].shape for k in rep] + list(part_full.values())

    def small_gather():
        parts = [jnp.stack(dmod).reshape(L, 6 * D)] + [jnp.stack(sm[k]).reshape(s_) for k, s_ in zip(small[1:], shapes[1:])]
        return all_gather([(_pack(parts), None)], "gather_smallgrads")[0]

    def scatter_finish(pending, after, l):
        names, handle, what, tag = pending
        landed = scatter_wait(handle, after, "scatter_wait_%s%s" % (what, tag))
        for k, land in zip(names, landed):
            li = widx(k, l)
            big_out[k] = adamw(flat2[k], flat2m[k], flat2v[k], land, row0=li * W[k].shape[1],
                               name="adamw_%s%s" % (k, tag), into=big_out.get(k, True))

    pend_mix, tok_mix = None, None
    for l in reversed(range(L)):
        sh1, sc1, g1, sh2, sc2, g2 = [mod[l, k] for k in range(6)]
        sv = saved[l]
        wl = sv["w"]
        tag = "_l%d" % l
        gr = {}
        dw2 = mm(sv["act"], dy, "tn", name="mlp_down_dw" + tag, out_dtypes=(BF16,))
        pend_w2, tok_w2 = send_grads({"mlp_w2": dw2}, "w2", tag)
        dz = mm(dy, wl["mlp_w2"], "nt", name="mlp_down_dx" + tag, out_dtypes=(BF16,), extras=[sv["z"]],
                rowvecs=[jnp.zeros((1, sv["z"].shape[1]), F32) + tok_w2],
                epi=lambda acc, z_, t_: (acc * (2.0 * jnp.maximum(z_.astype(F32), 0.0)) + t_,))
        dh2 = mm(dz, wl["mlp_w1"], "nt", name="mlp_up_dx" + tag, b_blocked=True)
        dw1 = mm(sv["h2"], dz, "tn", name="mlp_up_dw" + tag, out_dtypes=(BF16,), out_blocked=NDEV)
        pend_mlp, tok_mlp = send_grads({"mlp_w1": dw1}, "w1", tag)
        dx1, dsc2, dsh2, sm["norm2_g"][l], dmix, dgate1 = norm_bwd(
            dh2, sv["x1"], sv["rstd2"], dx, norm2_g[l][None], sc2 + tok_mlp, "norm2_bwd" + tag, gated=(sv["mixb"], g1))
        if pend_mix is not None:
            scatter_finish(pend_mix, dx1, l + 1)
        if l % 2 == 0:
            e = l // 2
            ng = sgu_norm_g[e].reshape(1, DA)
            bcol = sgu_b[e][:, :, None]
            dcat = mm(dmix, wl["ab_w_out"], "nt", name="ab_out_dx" + tag)
            gr["ab_w_out"] = mm(sv["cat"], dmix, "tn", name="ab_out_dw" + tag, out_dtypes=(BF16,))
            dy1, sm["conv_ln_g"][e], sm["conv_ln_b"][e], sm["conv_b"][e] = ln_silu_bwd(
                dcat, sv["y1"], conv_ln_g[e][None], conv_ln_b[e][None], "ln_silu_bwd" + tag)
            da, dg_, dwc = conv_bwd(sv["proj"], dy1, cw_pad[e], DA, DB, "conv_bwd" + tag)
            sm["conv_w"][e] = dwc[:CONV_W]
            duv, dsw, dsb, dsng = sgu_bwd(sv["proj"], dcat, ng, sgu_w[e], bcol, DA, "sgu_bwd" + tag)
            sm["sgu_w"][e], sm["sgu_b"][e], sm["sgu_norm_g"][e] = dsw, dsb, dsng
            dproj = jnp.concatenate([duv, da, dg_], axis=1)
            dh = mm(dproj, wl["ab_w_in"], "nt", name="ab_in_dx" + tag, b_blocked=True)
            gr["ab_w_in"] = mm(sv["h"], dproj, "tn", name="ab_in_dw" + tag, out_dtypes=(BF16,), out_blocked=NDEV)
        else:
            o_ = l // 2
            kgr = mla_k_head_g[o_][None, NOPE:QK]
            dO = mm(dmix, wl["mla_w_out"], "nt", name="mla_out_dx" + tag, out_dtypes=(BF16,))
            gr["mla_w_out"] = mm(sv["att"], dmix, "tn", name="mla_out_dw" + tag, out_dtypes=(BF16,))
            dQ, dK, dV = attn_bwd(sv["Qh"], sv["Kh"], sv["Vh"], dO, sv["att"], sv["lse"], "attn_bwd" + tag)
            dq_pre, dkv_pre, dkr, dqgn, dqgr, dkgn = mla_heads_bwd(
                dQ, dK, dV, sv["q"], sv["kv"], cos2, sinm, mla_q_head_g[o_][None], mla_k_head_g[o_][None], H,
                "mla_heads_bwd" + tag)
            dqn = mm(dq_pre, wl["mla_w_uq"], "nt", name="mla_uq_dx" + tag, b_blocked=True)
            gr["mla_w_uq"] = mm(sv["qn"], dq_pre, "tn", name="mla_uq_dw" + tag, out_dtypes=(BF16,), out_blocked=NDEV)
            dkvn = mm(dkv_pre, wl["mla_w_ukv"], "nt", name="mla_ukv_dx" + tag, b_blocked=True)
            gr["mla_w_ukv"] = mm(sv["kvn"], dkv_pre, "tn", name="mla_ukv_dw" + tag, out_dtypes=(BF16,),
                                 out_blocked=NDEV)
            dproj, sm["mla_q_norm_g"][o_], sm["mla_kv_norm_g"][o_], dkgr = mla_norms_bwd(
                dqn, dkvn, dkr, sv["proj"], qng_full[o_], kvng_full[o_], kgr, R, "mla_norms_bwd" + tag)
            sm["mla_q_head_g"][o_] = jnp.concatenate([dqgn, dqgr], axis=1)
            sm["mla_k_head_g"][o_] = jnp.concatenate([dkgn, dkgr], axis=1)
            dh = mm(dproj, wl["mla_w_in"], "nt", name="mla_in_dx" + tag)
            gr["mla_w_in"] = mm(sv["h"], dproj, "tn", name="mla_in_dw" + tag, out_dtypes=(BF16,))
        if l > 0:
            pend_mix, tok_mix = send_grads(gr, "mix", tag)
            dx, dsc1, dsh1, sm["norm1_g"][l], dy, dgate2_below = norm_bwd(
                dh, sv["x0"], sv["rstd1"], dx1, norm1_g[l][None], sc1 + tok_mix, "norm1_bwd" + tag,
                gated=(saved[l - 1]["yb"], mod[l - 1, 5]))
        else:
            dx, dsc1, dsh1, sm["norm1_g"][l] = norm_bwd(dh, sv["x0"], sv["rstd1"], dx1, norm1_g[l][None], sc1,
                                                        "norm1_bwd" + tag)
        dmod[l] = jnp.concatenate([dsh1, dsc1, dgate1, dsh2, dsc2, dgate2], axis=1)
        if l > 0:
            dgate2 = dgate2_below
        else:
            gp = small_gather()
            pend_mix, tok_mix = send_grads(gr, "mix", tag, after=gp)
        scatter_finish(pend_w2, dx, l)
        scatter_finish(pend_mlp, dx, l)

    rows_p = gp.shape[1]
    per = 6 * D // LANES
    dm = lax.slice_in_dim(gp, 0, L * per, axis=1).reshape(NDEV, L, per, LANES)
    dmod_cols = lax.dynamic_slice_in_dim(dm, me * (AW // LANES), AW // LANES, axis=2).reshape(NDEV, L * AW) + tok_mix

    def sum_fn(t, c_, o):
        acc = t[0][0]
        for s_ in range(1, NDEV):
            acc = acc + t[0][s_]
        o[0][...] = acc
        return []

    gsummed = rowwise(sum_fn, [gp], outs=[((rows_p, LANES), F32)], ts=_tile(rows_p, 256, 8), name="sum_smallgrads")[0]
    gsum = dict(zip(small, _unpack(gsummed, shapes)))
    gsum["mla_q_norm_g"] = _shard_cols(gsum["mla_q_norm_g"], me, R // NDEV, 1)
    gsum["mla_kv_norm_g"] = _shard_cols(gsum["mla_kv_norm_g"], me, R // NDEV, 1)
    gsum["conv_w"] = _shard_cols(gsum["conv_w"], me, CB, 2)
    sm_shapes = [W[k].shape for k in small]
    sres = adamw(_pack([W[k] for k in small]), _pack([M[k] for k in small]), _pack([V[k] for k in small]),
                 (_pack([gsum[k] for k in small]) + tok_mix)[None], row0=0, name="adamw_small")
    small_out = {k: vals for k, vals in zip(small, zip(*[_unpack(r_, sm_shapes) for r_ in sres]))}

    g_ada = mm(c_act, dmod_cols, "tn", name="ada_dw", out_blocked=L, tm=1024, tn=768, tk=NDEV, cast=None,
               precision=lax.Precision.HIGHEST)
    ada_out = adamw(ada_w.reshape(L * D, AW), m_ada_w.reshape(L * D, AW), v_ada_w.reshape(L * D, AW),
                    g_ada.reshape(1, L * D, AW), row0=0, name="adamw_ada_w")
    scatter_finish(pend_mix, ada_out[1], 0)
    ada_out = [a.reshape(L, D, AW) for a in ada_out]

    def result(k, which):
        if k == "ada_w":
            return ada_out[which]
        if k in BIG:
            return big_out[k][which].reshape(W[k].shape)
        return small_out[k][which]

    outs = [loss, dx[None]]
    for which in range(4):
        outs += [result(k, which) for k in ORDER]
    return tuple(outs)
```

```python
import functools

import jax
import jax.numpy as jnp
from jax import lax
from jax.experimental import pallas as pl
from jax.experimental.pallas import tpu as pltpu

F32 = jnp.float32
BF16 = jnp.bfloat16
EPS = 1e-6
NDEV = 8
LANES = 128
CHUNK = 128
GROUP = 128
CONV_W = 31
CONV_PAD = 32
NOPE, ROPE, VDIM = 128, 64, 128
QK = NOPE + ROPE
ROPE_THETA = 10000.0
VMEM_LIMIT = 56 * 1024 * 1024
ADAM_LR, ADAM_B1, ADAM_B2, ADAM_EPS, ADAM_WD, ADAM_STEP = 0.001, 0.9, 0.999, 1e-08, 0.01, 10
MESH = pl.DeviceIdType.MESH
NEG = -1e30
MLP_ROWS = 2048
ROW_TILE = 256
HEAD_ROWS = 256
ATTN_BLOCK = 1024
ATTN_STRIP = 64


def _pcall(body, **kw):
    return pl.pallas_call(body, **kw)


def _params(sem=None):
    return pltpu.CompilerParams(dimension_semantics=sem, vmem_limit_bytes=VMEM_LIMIT)


def _tile(dim, target, align=LANES):
    if dim <= target:
        return dim
    t = (target // align) * align
    while t >= align:
        if dim % t == 0:
            return t
        t -= align
    return dim


def _rstd(x):
    return lax.rsqrt(jnp.mean(x * x, axis=-1, keepdims=True) + EPS)


def _sigmoid(x):
    return 1.0 / (1.0 + jnp.exp(-x))


_GC = 0.7978845608028654


def _gelu(x):
    return 0.5 * x * (1.0 + jnp.tanh(_GC * (x + 0.044715 * x * x * x)))


def _gelu_grad(x):
    t = jnp.tanh(_GC * (x + 0.044715 * x * x * x))
    return 0.5 * (1.0 + t) + 0.5 * x * (1.0 - t * t) * _GC * (1.0 + 3 * 0.044715 * x * x)


def _colsum(x):
    return jnp.sum(x, axis=0, keepdims=True)


def _rms_bwd(dy, xhat, rstd, g):
    dxh = dy * g
    dx = rstd * (dxh - xhat * jnp.mean(dxh * xhat, axis=-1, keepdims=True))
    return dx, _colsum(dy * xhat)


def _swap_halves(x):
    h = x.shape[-1] // 2
    return jnp.concatenate([x[:, h:], x[:, :h]], axis=1)


def _rope(x, cos2, sinm):
    return x * cos2 + _swap_halves(x) * sinm


def _unrope(dy, cos2, sinm):
    return dy * cos2 + _swap_halves(dy * sinm)


_DIMS = {"nn": (((1,), (0,)), ((), ())), "nt": (((1,), (1,)), ((), ())), "tn": (((0,), (0,)), ((), ()))}


def mm(a, b, mode, *, name, out_dtypes=(F32,), epi=None, extras=(), rowvecs=(), b_blocked=False, out_blocked=0,
       tm=1024, tn=1024, tk=2048, precision=None, cast=BF16):
    if mode == "tn":
        K, M = a.shape
    else:
        M, K = a.shape
    if b_blocked:
        J, Rb, Cb = b.shape
        N = Rb if mode == "nt" else J * Cb
    else:
        N = b.shape[0] if mode == "nt" else b.shape[1]
    tm = _tile(M, tm)
    if mode == "nn" and b_blocked:
        tn = _tile(Cb, tn)
    elif out_blocked:
        tn = _tile(N // out_blocked, tn)
    else:
        tn = _tile(N, tn)
    kb = 1
    if mode == "nt" and b_blocked:
        if Cb >= tk:
            tk = _tile(Cb, tk)
        else:
            kb = max(d for d in range(1, J + 1) if J % d == 0 and d * Cb <= tk)
            tk = kb * Cb
    else:
        tk = _tile(K, tk)
    nk = K // tk
    grid = (M // tm, N // tn, nk)

    if mode == "tn":
        a_spec = pl.BlockSpec((tk, tm), lambda i, j, k: (k, i))
    else:
        a_spec = pl.BlockSpec((tm, tk), lambda i, j, k: (i, k))
    if mode == "nn":
        if b_blocked:
            nper = Cb // tn
            b_spec = pl.BlockSpec((None, tk, tn), lambda i, j, k: (j // nper, k, j % nper))
        else:
            b_spec = pl.BlockSpec((tk, tn), lambda i, j, k: (k, j))
    elif mode == "nt":
        if b_blocked:
            if kb > 1:
                b_spec = pl.BlockSpec((kb, tn, Cb), lambda i, j, k: (k, j, 0))
            else:
                kper = Cb // tk
                b_spec = pl.BlockSpec((None, tn, tk), lambda i, j, k: (k // kper, j, k % kper))
        else:
            b_spec = pl.BlockSpec((tn, tk), lambda i, j, k: (j, k))
    else:
        b_spec = pl.BlockSpec((tk, tn), lambda i, j, k: (k, j))
    if out_blocked:
        oper = (N // out_blocked) // tn
        o_spec = pl.BlockSpec((None, tm, tn), lambda i, j, k: (j // oper, i, j % oper))
        o_shape = (out_blocked, M, N // out_blocked)
    else:
        o_spec = pl.BlockSpec((tm, tn), lambda i, j, k: (i, j))
        o_shape = (M, N)
    e_spec = pl.BlockSpec((tm, tn), lambda i, j, k: (i, j))
    r_spec = pl.BlockSpec((1, tn), lambda i, j, k: (0, j))
    ne, nr, no = len(extras), len(rowvecs), len(out_dtypes)
    dims = _DIMS[mode]

    def body(a_ref, b_ref, *rest):
        ex = rest[:ne]
        rv = rest[ne:ne + nr]
        outs = rest[ne + nr:ne + nr + no]

        def product():
            if kb > 1:
                r = None
                for q in range(kb):
                    av, bv = a_ref[:, q * Cb:(q + 1) * Cb], b_ref[q]
                    if cast is not None:
                        av, bv = av.astype(cast), bv.astype(cast)
                    d = lax.dot_general(av, bv, dims, preferred_element_type=F32, precision=precision)
                    r = d if r is None else r + d
                return r
            av, bv = a_ref[...], b_ref[...]
            if cast is not None:
                av, bv = av.astype(cast), bv.astype(cast)
            return lax.dot_general(av, bv, dims, preferred_element_type=F32, precision=precision)

        def finish(r):
            vals = (r,) if epi is None else epi(r, *[e[...] for e in ex], *[v[...] for v in rv])
            for o, val in zip(outs, vals):
                o[...] = val.astype(o.dtype)

        if nk == 1:
            finish(product())
            return
        acc = rest[ne + nr + no]
        k = pl.program_id(2)

        @pl.when(k == 0)
        def _():
            acc[...] = product()

        @pl.when((k > 0) & (k < nk - 1))
        def _():
            acc[...] += product()

        @pl.when(k == nk - 1)
        def _():
            finish(acc[...] + product())

    res = _pcall(
        body, name=name, grid=grid,
        in_specs=[a_spec, b_spec] + [e_spec] * ne + [r_spec] * nr,
        out_specs=[o_spec] * no,
        out_shape=[jax.ShapeDtypeStruct(o_shape, dt) for dt in out_dtypes],
        scratch_shapes=[] if nk == 1 else [pltpu.VMEM((tm, tn), F32)],
        compiler_params=_params(("parallel", "parallel", "arbitrary")),
    )(a, b, *extras, *rowvecs)
    return res[0] if no == 1 else res


def rowwise(fn, tiled, consts=(), outs=(), reds=(), *, ts, name):
    specs = []
    arrs = []
    rows = None
    for t in tiled:
        a, w, cb = t if isinstance(t, tuple) else (t, None, 0)
        arrs.append(a)
        rows = a.shape[-2] if rows is None else rows
        if a.ndim == 2:
            specs.append(pl.BlockSpec((ts, a.shape[1] if w is None else w), lambda i, cb=cb: (i, cb)))
        else:
            specs.append(pl.BlockSpec((a.shape[0], ts, a.shape[2]), lambda i: (0, i, 0)))
    for a in consts:
        specs.append(pl.BlockSpec(a.shape, lambda i, n=a.ndim: (0,) * n))
    o_specs, o_shapes = [], []
    for shp, dt in outs:
        if len(shp) == 2:
            o_specs.append(pl.BlockSpec((ts, shp[1]), lambda i: (i, 0)))
        else:
            o_specs.append(pl.BlockSpec((shp[0], ts, shp[2]), lambda i: (0, i, 0)))
        o_shapes.append(jax.ShapeDtypeStruct(shp, dt))
    for shp in reds:
        o_specs.append(pl.BlockSpec(shp, lambda i, n=len(shp): (0,) * n))
        o_shapes.append(jax.ShapeDtypeStruct(shp, F32))
    nt, nc, no = len(arrs), len(consts), len(outs)

    def body(*refs):
        i = pl.program_id(0)
        red_refs = refs[nt + nc + no:]
        vals = fn(refs[:nt], refs[nt:nt + nc], refs[nt + nc:nt + nc + no])
        if red_refs:
            @pl.when(i == 0)
            def _():
                for r in red_refs:
                    r[...] = jnp.zeros_like(r)
            for r, v in zip(red_refs, vals):
                r[...] += v

    return _pcall(body, name=name, grid=(rows // ts,), in_specs=specs, out_specs=o_specs, out_shape=o_shapes,
                  compiler_params=_params(("arbitrary",)))(*arrs, *consts)


def cast_bf16(w, l, name):
    _, R, C = w.shape
    tr = _tile(R, 512, 16)

    def body(w_ref, o_ref):
        o_ref[...] = w_ref[...].astype(BF16)

    return _pcall(body, name=name, grid=(R // tr,), in_specs=[pl.BlockSpec((None, tr, C), lambda i: (l, i, 0))],
                  out_specs=pl.BlockSpec((tr, C), lambda i: (i, 0)), out_shape=jax.ShapeDtypeStruct((R, C), BF16),
                  compiler_params=_params(("parallel",)))(w)


def prenorm(x, g, scale, shift, name):
    S, D = x.shape

    def fn(t, c, o):
        xv = t[0][...]
        r = _rstd(xv)
        o[0][...] = ((xv * r * c[0][...]) * (1.0 + c[1][...]) + c[2][...]).astype(BF16)
        o[1][...] = r
        return []

    return rowwise(fn, [x], [g, scale, shift], [((S, D), BF16), ((S, 1), F32)], ts=_tile(S, ROW_TILE, 16), name=name)


def _gate_stage(dx, y_ref, gate_ref, dy_ref):
    dy_ref[...] = (dx * gate_ref[...]).astype(BF16)
    return _colsum(dx * y_ref[...].astype(F32))


def norm_bwd(dh, x, rstd, dres, g, scale, name, gated=None):
    S, D = x.shape

    def fn(t, c, o):
        d = t[0][...]
        r = t[2][...]
        xh = t[1][...] * r
        gv = c[0][...]
        dr = d * (1.0 + c[1][...])
        dx, dg = _rms_bwd(dr, xh, r, gv)
        dx = t[3][...] + dx
        o[0][...] = dx
        reds = [_colsum(d * (xh * gv)), _colsum(d), dg]
        if gated is not None:
            reds.append(_gate_stage(dx, t[4], c[2], o[1]))
        return reds

    ng = gated is not None
    res = rowwise(fn, [dh, x, rstd, dres] + ([gated[0]] if ng else []), [g, scale] + ([gated[1]] if ng else []),
                  [((S, D), F32)] + ([((S, D), BF16)] if ng else []), [(1, D)] * (3 + ng), ts=_tile(S, ROW_TILE, 8),
                  name=name)
    return (res[0], res[2], res[3], res[4], res[1], res[5]) if ng else res


def loss_grad(y, tgt, yb, gate, name):
    S, D = y.shape

    def fn(t, c, o):
        e = t[0][...] - t[1][...]
        dx = e * (1.0 / D)
        o[0][...] = dx
        return [_colsum(e * e), _gate_stage(dx, t[2], c[0], o[1])]

    return rowwise(fn, [y, tgt, yb], [gate], outs=[((S, D), F32), ((S, D), BF16)], reds=[(1, D)] * 2,
                   ts=_tile(S, ROW_TILE, 8), name=name)


def _tril_mask():
    r = lax.broadcasted_iota(jnp.int32, (CHUNK, CHUNK), 0)
    c = lax.broadcasted_iota(jnp.int32, (CHUNK, CHUNK), 1)
    return c <= r


def sgu_fwd(proj, ng, w, bcol, DA, name):
    S = proj.shape[0]
    G = DA // GROUP
    tr = _tile(S, 2 * CHUNK)

    def body(u_ref, v_ref, ng_ref, w_ref, b_ref, o_ref):
        mask = _tril_mask()
        for g in range(G):
            cols = slice(g * GROUP, (g + 1) * GROUP)
            wm = jnp.where(mask, w_ref[g], 0.0).astype(BF16)
            for ci in range(tr // CHUNK):
                rows = slice(ci * CHUNK, (ci + 1) * CHUNK)
                gv = _gelu(v_ref[rows, cols])
                vn = gv * _rstd(gv) * ng_ref[:, cols]
                mixed = jnp.dot(wm, vn.astype(BF16), preferred_element_type=F32) + b_ref[g]
                o_ref[rows, cols] = (_gelu(u_ref[rows, cols]) * mixed).astype(o_ref.dtype)

    return _pcall(
        body, name=name, grid=(S // tr,),
        in_specs=[pl.BlockSpec((tr, DA), lambda i: (i, 0)), pl.BlockSpec((tr, DA), lambda i: (i, 1)),
                  pl.BlockSpec((1, DA), lambda i: (0, 0)), pl.BlockSpec((G, CHUNK, CHUNK), lambda i: (0, 0, 0)),
                  pl.BlockSpec((G, CHUNK, 1), lambda i: (0, 0, 0))],
        out_specs=pl.BlockSpec((tr, DA), lambda i: (i, 0)),
        out_shape=jax.ShapeDtypeStruct((S, DA), BF16),
        compiler_params=_params(("parallel",)),
    )(proj, proj, ng, w, bcol)


def sgu_bwd(proj, dcat, ng, w, bcol, DA, name):
    S = proj.shape[0]
    G = DA // GROUP
    tr = _tile(S, 2 * CHUNK)
    nsteps = S // tr

    def body(u_ref, v_ref, d_ref, ng_ref, w_ref, b_ref, duv_ref, dw_ref, db_ref, dng_ref, dbacc):
        i = pl.program_id(0)

        @pl.when(i == 0)
        def _():
            dw_ref[...] = jnp.zeros_like(dw_ref)
            dng_ref[...] = jnp.zeros_like(dng_ref)
            dbacc[...] = jnp.zeros_like(dbacc)

        mask = _tril_mask()
        for g in range(G):
            cols = slice(g * GROUP, (g + 1) * GROUP)
            wm = jnp.where(mask, w_ref[g], 0.0).astype(BF16)
            ngg = ng_ref[:, cols]
            for ci in range(tr // CHUNK):
                rows = slice(ci * CHUNK, (ci + 1) * CHUNK)
                u, v, d = u_ref[rows, cols], v_ref[rows, cols], d_ref[rows, cols]
                gv = _gelu(v)
                rs = _rstd(gv)
                vhat = gv * rs
                vn = (vhat * ngg).astype(BF16)
                mixed = jnp.dot(wm, vn, preferred_element_type=F32) + b_ref[g]
                dmixed = d * _gelu(u)
                dmb = dmixed.astype(BF16)
                duv_ref[rows, cols] = (d * mixed * _gelu_grad(u)).astype(duv_ref.dtype)
                dwg = lax.dot_general(dmb, vn, _DIMS["nt"], preferred_element_type=F32)
                dw_ref[g] += jnp.where(mask, dwg, 0.0)
                dbacc[g] += dmixed
                dvn = lax.dot_general(wm, dmb, _DIMS["tn"], preferred_element_type=F32)
                dgv, dngg = _rms_bwd(dvn, vhat, rs, ngg)
                dng_ref[:, cols] += dngg
                duv_ref[rows, DA + g * GROUP:DA + (g + 1) * GROUP] = (dgv * _gelu_grad(v)).astype(duv_ref.dtype)

        @pl.when(i == nsteps - 1)
        def _():
            for g in range(G):
                db_ref[g] = jnp.sum(dbacc[g], axis=-1, keepdims=True)

    return _pcall(
        body, name=name, grid=(nsteps,),
        in_specs=[pl.BlockSpec((tr, DA), lambda i: (i, 0)), pl.BlockSpec((tr, DA), lambda i: (i, 1)),
                  pl.BlockSpec((tr, DA), lambda i: (i, 0)),
                  pl.BlockSpec((1, DA), lambda i: (0, 0)), pl.BlockSpec((G, CHUNK, CHUNK), lambda i: (0, 0, 0)),
                  pl.BlockSpec((G, CHUNK, 1), lambda i: (0, 0, 0))],
        out_specs=[pl.BlockSpec((tr, 2 * DA), lambda i: (i, 0)), pl.BlockSpec((G, CHUNK, CHUNK), lambda i: (0, 0, 0)),
                   pl.BlockSpec((G, CHUNK, 1), lambda i: (0, 0, 0)), pl.BlockSpec((1, DA), lambda i: (0, 0))],
        out_shape=[jax.ShapeDtypeStruct((S, 2 * DA), BF16), jax.ShapeDtypeStruct((G, CHUNK, CHUNK), F32),
                   jax.ShapeDtypeStruct((G, CHUNK, 1), F32), jax.ShapeDtypeStruct((1, DA), F32)],
        scratch_shapes=[pltpu.VMEM((G, CHUNK, CHUNK), F32)],
        compiler_params=_params(("arbitrary",)),
    )(proj, proj, dcat, ng, w, bcol)


def _conv_tile(S):
    return _tile(S, 256, 8)


def conv_fwd(proj, wk, bias, DA, DB, name):
    S = proj.shape[0]
    nb = DB // LANES
    a0, g0 = 2 * DA // LANES, (2 * DA + DB) // LANES
    T = _conv_tile(S)
    off = CONV_PAD - (CONV_W - 1)

    def body(a_ref, g_ref, w_ref, b_ref, o_ref, ypad):
        ypad[0:CONV_PAD, :] = jnp.zeros((CONV_PAD, LANES), F32)

        def fill(t, cr):
            r = pl.multiple_of(t * T, T)
            ypad[pl.ds(CONV_PAD + r, T), :] = a_ref[pl.ds(r, T), :] * _sigmoid(g_ref[pl.ds(r, T), :])
            return cr

        lax.fori_loop(0, S // T, fill, 0)

        def step(t, cr):
            r = pl.multiple_of(t * T, T)
            acc = jnp.zeros((T, LANES), F32) + b_ref[...]
            for k in range(CONV_W):
                acc = acc + w_ref[k:k + 1, :] * ypad[pl.ds(r + (k + off), T), :]
            o_ref[pl.ds(r, T), :] = acc
            return cr

        lax.fori_loop(0, S // T, step, 0)

    return _pcall(
        body, name=name, grid=(nb,),
        in_specs=[pl.BlockSpec((S, LANES), lambda j: (0, a0 + j)), pl.BlockSpec((S, LANES), lambda j: (0, g0 + j)),
                  pl.BlockSpec((CONV_PAD, LANES), lambda j: (0, j)), pl.BlockSpec((1, LANES), lambda j: (0, j))],
        out_specs=pl.BlockSpec((S, LANES), lambda j: (0, j)),
        out_shape=jax.ShapeDtypeStruct((S, DB), F32),
        scratch_shapes=[pltpu.VMEM((S + CONV_PAD, LANES), F32)],
        compiler_params=_params(("parallel",)),
    )(proj, proj, wk, bias)


def conv_bwd(proj, dy1, wk, DA, DB, name):
    S = proj.shape[0]
    nb = DB // LANES
    a0, g0 = 2 * DA // LANES, (2 * DA + DB) // LANES
    T = _conv_tile(S)
    off = CONV_PAD - (CONV_W - 1)

    def body(a_ref, g_ref, d_ref, w_ref, da_ref, dg_ref, dw_ref, ypad, dpad, wacc):
        ypad[0:CONV_PAD, :] = jnp.zeros((CONV_PAD, LANES), F32)
        dpad[S:S + CONV_PAD, :] = jnp.zeros((CONV_PAD, LANES), F32)
        wacc[...] = jnp.zeros_like(wacc)

        def fill(t, cr):
            r = pl.multiple_of(t * T, T)
            ypad[pl.ds(CONV_PAD + r, T), :] = a_ref[pl.ds(r, T), :] * _sigmoid(g_ref[pl.ds(r, T), :])
            dpad[pl.ds(r, T), :] = d_ref[pl.ds(r, T), :]
            return cr

        lax.fori_loop(0, S // T, fill, 0)

        def step(t, cr):
            r = pl.multiple_of(t * T, T)
            dt = dpad[pl.ds(r, T), :]
            dy0 = jnp.zeros((T, LANES), F32)
            for k in range(CONV_W):
                prod = dt * ypad[pl.ds(r + (k + off), T), :]
                wacc[k] += jnp.sum(prod.reshape(T // 8, 8, LANES), axis=0)
                dy0 = dy0 + w_ref[k:k + 1, :] * dpad[pl.ds(r + (CONV_W - 1 - k), T), :]
            av, gv = a_ref[pl.ds(r, T), :], g_ref[pl.ds(r, T), :]
            sg = _sigmoid(gv)
            da_ref[pl.ds(r, T), :] = (dy0 * sg).astype(da_ref.dtype)
            dg_ref[pl.ds(r, T), :] = (dy0 * av * sg * (1.0 - sg)).astype(dg_ref.dtype)
            return cr

        lax.fori_loop(0, S // T, step, 0)
        for k in range(CONV_W):
            dw_ref[k:k + 1, :] = jnp.sum(wacc[k], axis=0, keepdims=True)
        dw_ref[CONV_W:CONV_PAD, :] = jnp.zeros((CONV_PAD - CONV_W, LANES), F32)

    return _pcall(
        body, name=name, grid=(nb,),
        in_specs=[pl.BlockSpec((S, LANES), lambda j: (0, a0 + j)), pl.BlockSpec((S, LANES), lambda j: (0, g0 + j)),
                  pl.BlockSpec((S, LANES), lambda j: (0, j)), pl.BlockSpec((CONV_PAD, LANES), lambda j: (0, j))],
        out_specs=[pl.BlockSpec((S, LANES), lambda j: (0, j)), pl.BlockSpec((S, LANES), lambda j: (0, j)),
                   pl.BlockSpec((CONV_PAD, LANES), lambda j: (0, j))],
        out_shape=[jax.ShapeDtypeStruct((S, DB), BF16), jax.ShapeDtypeStruct((S, DB), BF16),
                   jax.ShapeDtypeStruct((CONV_PAD, DB), F32)],
        scratch_shapes=[pltpu.VMEM((S + CONV_PAD, LANES), F32), pltpu.VMEM((S + CONV_PAD, LANES), F32),
                        pltpu.VMEM((CONV_PAD, 8, LANES), F32)],
        compiler_params=_params(("parallel",)),
    )(proj, proj, dy1, wk)


def _ln_stats(y):
    mu = jnp.mean(y, axis=-1, keepdims=True)
    yc = y - mu
    rs = lax.rsqrt(jnp.mean(yc * yc, axis=-1, keepdims=True) + EPS)
    return yc * rs, rs


def ln_silu(y1, lg, lb, name):
    S, DB = y1.shape

    def fn(t, c, o):
        yh, _ = _ln_stats(t[0][...])
        ln = yh * c[0][...] + c[1][...]
        o[0][...] = (ln * _sigmoid(ln)).astype(BF16)
        return []

    return rowwise(fn, [y1], [lg, lb], [((S, DB), BF16)], ts=_tile(S, ROW_TILE, 16), name=name)[0]


def ln_silu_bwd(dcat, y1, lg, lb, name):
    S, DB = y1.shape
    cb = (dcat.shape[1] - DB) // DB

    def fn(t, c, o):
        yh, rs = _ln_stats(t[1][...])
        gv = c[0][...]
        ln = yh * gv + c[1][...]
        sg = _sigmoid(ln)
        dln = t[0][...] * (sg * (1.0 + ln * (1.0 - sg)))
        dyh = dln * gv
        dy = rs * (dyh - jnp.mean(dyh, axis=-1, keepdims=True) - yh * jnp.mean(dyh * yh, axis=-1, keepdims=True))
        o[0][...] = dy
        return [_colsum(dln * yh), _colsum(dln), _colsum(dy)]

    return rowwise(fn, [(dcat, DB, cb), y1], [lg, lb], [((S, DB), F32)], [(1, DB)] * 3, ts=_tile(S, ROW_TILE, 8), name=name)


def mla_norms(proj, cos2, sinm, qg, kvg, kgr, R, name):
    S = proj.shape[0]

    def fn(t, c, o):
        cq = t[0][:, 0:R]
        ckv = t[0][:, R:2 * R]
        kr = t[0][:, 2 * R:2 * R + ROPE]
        o[0][...] = (cq * _rstd(cq) * c[0][...]).astype(BF16)
        o[1][...] = (ckv * _rstd(ckv) * c[1][...]).astype(BF16)
        o[2][...] = _rope(kr * _rstd(kr) * c[2][...], t[1][...], t[2][...])
        return []

    return rowwise(fn, [proj, cos2, sinm], [qg, kvg, kgr], [((S, R), BF16), ((S, R), BF16), ((S, ROPE), F32)],
                   ts=_tile(S, ROW_TILE, 16), name=name)


def mla_heads(q, kv, kr, cos2, sinm, qg, kg, H, name):
    S = q.shape[0]

    def fn(t, c, o):
        cs, sn = t[3][...], t[4][...]
        krv = t[2][...]
        qgn, qgr, kgn = c[0][:, 0:NOPE], c[0][:, NOPE:QK], c[1][:, 0:NOPE]
        for h in range(H):
            qn = t[0][:, QK * h:QK * h + NOPE]
            qr = t[0][:, QK * h + NOPE:QK * (h + 1)]
            o[0][h, :, 0:NOPE] = (qn * _rstd(qn) * qgn).astype(BF16)
            o[0][h, :, NOPE:QK] = _rope(qr * _rstd(qr) * qgr, cs, sn).astype(BF16)
            kn = t[1][:, (NOPE + VDIM) * h:(NOPE + VDIM) * h + NOPE]
            o[1][h, :, 0:NOPE] = (kn * _rstd(kn) * kgn).astype(BF16)
            o[1][h, :, NOPE:QK] = krv.astype(BF16)
            o[2][h] = t[1][:, (NOPE + VDIM) * h + NOPE:(NOPE + VDIM) * (h + 1)].astype(BF16)
        return []

    return rowwise(fn, [q, kv, kr, cos2, sinm], [qg, kg],
                   [((H, S, QK), BF16), ((H, S, QK), BF16), ((H, S, VDIM), BF16)], ts=_tile(S, HEAD_ROWS, 16), name=name)


def mla_heads_bwd(dQ, dK, dV, q, kv, cos2, sinm, qg, kg, H, name):
    S = q.shape[0]
    KV = NOPE + VDIM

    def fn(t, c, o):
        cs, sn = t[5][...], t[6][...]
        qgn, qgr, kgn = c[0][:, 0:NOPE], c[0][:, NOPE:QK], c[1][:, 0:NOPE]
        a_qn = jnp.zeros((1, NOPE), F32)
        a_qr = jnp.zeros((1, ROPE), F32)
        a_kn = jnp.zeros((1, NOPE), F32)
        dkr = jnp.zeros((t[0].shape[1], ROPE), F32)
        for h in range(H):
            qn = t[3][:, QK * h:QK * h + NOPE]
            rs = _rstd(qn)
            dx, dg = _rms_bwd(t[0][h, :, 0:NOPE], qn * rs, rs, qgn)
            o[0][:, QK * h:QK * h + NOPE] = dx.astype(BF16)
            a_qn = a_qn + dg
            qr = t[3][:, QK * h + NOPE:QK * (h + 1)]
            rs = _rstd(qr)
            dx, dg = _rms_bwd(_unrope(t[0][h, :, NOPE:QK], cs, sn), qr * rs, rs, qgr)
            o[0][:, QK * h + NOPE:QK * (h + 1)] = dx.astype(BF16)
            a_qr = a_qr + dg
            kn = t[4][:, KV * h:KV * h + NOPE]
            rs = _rstd(kn)
            dx, dg = _rms_bwd(t[1][h, :, 0:NOPE], kn * rs, rs, kgn)
            o[1][:, KV * h:KV * h + NOPE] = dx.astype(BF16)
            a_kn = a_kn + dg
            o[1][:, KV * h + NOPE:KV * (h + 1)] = t[2][h].astype(BF16)
            dkr = dkr + t[1][h, :, NOPE:QK]
        o[2][...] = _unrope(dkr, cs, sn)
        return [a_qn, a_qr, a_kn]

    return rowwise(fn, [dQ, dK, dV, q, kv, cos2, sinm], [qg, kg],
                   [((S, H * QK), BF16), ((S, H * KV), BF16), ((S, ROPE), F32)],
                   [(1, NOPE), (1, ROPE), (1, NOPE)], ts=_tile(S, HEAD_ROWS, 16), name=name)


def mla_norms_bwd(dqn, dkvn, dkr, proj, qg, kvg, kgr, R, name):
    S = proj.shape[0]

    def fn(t, c, o):
        reds = []
        for idx, (lo, hi) in enumerate(((0, R), (R, 2 * R), (2 * R, 2 * R + ROPE))):
            xv = t[3][:, lo:hi]
            rs = _rstd(xv)
            dx, dg = _rms_bwd(t[idx][...], xv * rs, rs, c[idx][...])
            o[0][:, lo:hi] = dx.astype(BF16)
            reds.append(dg)
        return reds

    return rowwise(fn, [dqn, dkvn, dkr, proj], [qg, kvg, kgr], [((S, 2 * R + ROPE), BF16)],
                   [(1, R), (1, R), (1, ROPE)], ts=_tile(S, ROW_TILE, 16), name=name)


def _tri_rows(p, n):
    qi = 0
    for j in range(1, n):
        qi = qi + (p >= j * (j + 1) // 2).astype(jnp.int32)
    return qi, p - (qi * (qi + 1)) // 2


def _tri_cols(p, n):
    ki = 0
    for j in range(1, n):
        ki = ki + (p >= j * n - j * (j - 1) // 2).astype(jnp.int32)
    return ki, ki + p - (ki * n - (ki * (ki - 1)) // 2)


def attn_fwd(Q, K, V, name):
    H, S, _ = Q.shape
    t = _tile(S, ATTN_BLOCK)
    n = S // t
    scale = QK ** -0.5

    rs = _tile(t, ATTN_STRIP, 8)

    def body(q_ref, k_ref, v_ref, o_ref, lse_ref, m_s, l_s, acc, s_scr, p_scr):
        qi, ki = _tri_rows(pl.program_id(1), n)

        @pl.when(ki == 0)
        def _():
            m_s[...] = jnp.full_like(m_s, NEG)
            l_s[...] = jnp.zeros_like(l_s)
            acc[...] = jnp.zeros_like(acc)

        def block(diagonal):
            s_scr[...] = lax.dot_general(q_ref[...], k_ref[...], _DIMS["nt"], preferred_element_type=F32)

            def strip(i, cr):
                r = slice(i * rs, (i + 1) * rs)
                s = s_scr[r, :] * scale
                if diagonal:
                    row = i * rs + lax.broadcasted_iota(jnp.int32, (rs, t), 0)
                    s = jnp.where(lax.broadcasted_iota(jnp.int32, (rs, t), 1) <= row, s, NEG)
                m_old = m_s[r, :]
                m_new = jnp.maximum(m_old, jnp.max(s, axis=-1, keepdims=True))
                alpha = jnp.exp(m_old - m_new)
                p = jnp.exp(s - m_new)
                l_s[r, :] = alpha * l_s[r, :] + jnp.sum(p, axis=-1, keepdims=True)
                m_s[r, :] = m_new
                acc[r, :] = alpha * acc[r, :]
                p_scr[r, :] = p.astype(BF16)
                return cr

            for i in range(t // rs):
                strip(i, 0)
            acc[...] += jnp.dot(p_scr[...], v_ref[...], preferred_element_type=F32)

        @pl.when(ki < qi)
        def _():
            block(False)

        @pl.when(ki == qi)
        def _():
            block(True)

        @pl.when(ki == qi)
        def _():
            o_ref[...] = (acc[...] / l_s[...]).astype(o_ref.dtype)
            lse_ref[...] = m_s[...] + jnp.log(l_s[...])

    return _pcall(
        body, name=name, grid=(H, n * (n + 1) // 2),
        in_specs=[pl.BlockSpec((None, t, QK), lambda h, p: (h, _tri_rows(p, n)[0], 0)),
                  pl.BlockSpec((None, t, QK), lambda h, p: (h, _tri_rows(p, n)[1], 0)),
                  pl.BlockSpec((None, t, VDIM), lambda h, p: (h, _tri_rows(p, n)[1], 0))],
        out_specs=[pl.BlockSpec((t, VDIM), lambda h, p: (_tri_rows(p, n)[0], h)),
                   pl.BlockSpec((None, t, 1), lambda h, p: (h, _tri_rows(p, n)[0], 0))],
        out_shape=[jax.ShapeDtypeStruct((S, H * VDIM), BF16), jax.ShapeDtypeStruct((H, S, 1), F32)],
        scratch_shapes=[pltpu.VMEM((t, 1), F32), pltpu.VMEM((t, 1), F32), pltpu.VMEM((t, VDIM), F32),
                        pltpu.VMEM((t, t), F32), pltpu.VMEM((t, t), BF16)],
        compiler_params=_params(("parallel", "arbitrary")),
    )(Q, K, V)


def attn_bwd(Q, K, V, dO, O, lse, name):
    H, S, _ = Q.shape
    t = _tile(S, ATTN_BLOCK)
    n = S // t
    scale = QK ** -0.5

    rs = _tile(t, ATTN_STRIP, 8)

    def body(q_ref, k_ref, v_ref, do_ref, o_ref, lse_ref, dq_ref, dk_ref, dv_ref, s_scr, dp_scr, p_scr, ds_scr):
        ki, qi = _tri_cols(pl.program_id(1), n)

        @pl.when(pl.program_id(1) == 0)
        def _():
            dq_ref[...] = jnp.zeros_like(dq_ref)

        @pl.when(qi == ki)
        def _():
            dk_ref[...] = jnp.zeros_like(dk_ref)
            dv_ref[...] = jnp.zeros_like(dv_ref)

        def block(diagonal):
            s_scr[...] = lax.dot_general(q_ref[...], k_ref[...], _DIMS["nt"], preferred_element_type=F32)
            dp_scr[...] = lax.dot_general(do_ref[...], v_ref[...], _DIMS["nt"], preferred_element_type=F32)

            def strip(i, cr):
                r = slice(i * rs, (i + 1) * rs)
                s = s_scr[r, :] * scale
                if diagonal:
                    row = i * rs + lax.broadcasted_iota(jnp.int32, (rs, t), 0)
                    s = jnp.where(lax.broadcasted_iota(jnp.int32, (rs, t), 1) <= row, s, NEG)
                p = jnp.exp(s - lse_ref[r, :])
                delta = jnp.sum(do_ref[r, :].astype(F32) * o_ref[r, :].astype(F32), axis=-1, keepdims=True)
                p_scr[r, :] = p.astype(BF16)
                ds_scr[r, :] = (p * (dp_scr[r, :] - delta) * scale).astype(BF16)
                return cr

            for i in range(t // rs):
                strip(i, 0)
            ds = ds_scr[...]
            dv_ref[...] += lax.dot_general(p_scr[...], do_ref[...], _DIMS["tn"], preferred_element_type=F32)
            dk_ref[...] += lax.dot_general(ds, q_ref[...], _DIMS["tn"], preferred_element_type=F32)
            rq = pl.multiple_of(qi * t, t)
            dq_ref[pl.ds(rq, t), :] += jnp.dot(ds, k_ref[...], preferred_element_type=F32)

        @pl.when(qi > ki)
        def _():
            block(False)

        @pl.when(qi == ki)
        def _():
            block(True)

    qmap = lambda h, p: (h, _tri_cols(p, n)[1], 0)
    kmap = lambda h, p: (h, _tri_cols(p, n)[0], 0)
    return _pcall(
        body, name=name, grid=(H, n * (n + 1) // 2),
        in_specs=[pl.BlockSpec((None, t, QK), qmap),
                  pl.BlockSpec((None, t, QK), kmap),
                  pl.BlockSpec((None, t, VDIM), kmap),
                  pl.BlockSpec((t, VDIM), lambda h, p: (_tri_cols(p, n)[1], h)),
                  pl.BlockSpec((t, VDIM), lambda h, p: (_tri_cols(p, n)[1], h)),
                  pl.BlockSpec((None, t, 1), qmap)],
        out_specs=[pl.BlockSpec((None, S, QK), lambda h, p: (h, 0, 0)),
                   pl.BlockSpec((None, t, QK), kmap),
                   pl.BlockSpec((None, t, VDIM), kmap)],
        out_shape=[jax.ShapeDtypeStruct((H, S, QK), F32), jax.ShapeDtypeStruct((H, S, QK), F32),
                   jax.ShapeDtypeStruct((H, S, VDIM), F32)],
        scratch_shapes=[pltpu.VMEM((t, t), F32), pltpu.VMEM((t, t), F32), pltpu.VMEM((t, t), BF16),
                        pltpu.VMEM((t, t), BF16)],
        compiler_params=_params(("parallel", "arbitrary")),
    )(Q, K, V, dO, O, lse)


def adamw(w, m, v, g, *, row0, name, into=None):
    P, rows, C = g.shape
    tr = _tile(rows, max(16, 131072 // C), 16)
    off = row0 // tr
    assert row0 % tr == 0
    bc1 = 1.0 - ADAM_B1 ** ADAM_STEP
    bc2 = 1.0 - ADAM_B2 ** ADAM_STEP
    chained = into is not None and into is not True

    def body(w_ref, m_ref, v_ref, g_ref, *rest):
        go_ref, d_ref, mo_ref, vo_ref = rest[-4:]
        gs = g_ref[0].astype(F32)
        for p in range(1, P):
            gs = gs + g_ref[p].astype(F32)
        wv = w_ref[...]
        mn = ADAM_B1 * m_ref[...] + (1.0 - ADAM_B1) * gs
        vn = ADAM_B2 * v_ref[...] + (1.0 - ADAM_B2) * (gs * gs)
        go_ref[...] = gs
        mo_ref[...] = mn
        vo_ref[...] = vn
        d_ref[...] = -ADAM_LR * ((mn / bc1) / (jnp.sqrt(vn / bc2) + ADAM_EPS) + ADAM_WD * wv)

    wspec = pl.BlockSpec((tr, C), lambda i: (i + off, 0))
    ospec = wspec if into is not None else pl.BlockSpec((tr, C), lambda i: (i, 0))
    out_rows = w.shape[0] if into is not None else rows
    return _pcall(
        body, name=name, grid=(rows // tr,),
        in_specs=[wspec, wspec, wspec, pl.BlockSpec((P, tr, C), lambda i: (0, i, 0))] + ([_ANY] * 4 if chained else []),
        out_specs=[ospec] * 4, out_shape=[jax.ShapeDtypeStruct((out_rows, C), F32)] * 4,
        input_output_aliases={4 + q: q for q in range(4)} if chained else {},
        compiler_params=_params(("parallel",)),
    )(w, m, v, g, *(into if chained else ()))


def _coords():
    return lax.axis_index("x"), lax.axis_index("y"), lax.axis_index("c")


def _me():
    x, y, c = _coords()
    return 4 * x + 2 * y + c


_ANY = pl.BlockSpec(memory_space=pl.ANY)


def all_gather(items, name):
    n = len(items)
    blks = [a.shape if idx is None else a.shape[1:] for a, idx in items]

    def body(*refs):
        ins, outs = refs[:n], refs[n:2 * n]
        send, recv, lsem = refs[2 * n:]
        x, y, c = _coords()
        me, sib = (x, y, c), (x, y, 1 - c)
        chips = [(1 - x, y), (x, 1 - y), (1 - x, 1 - y)]

        def src(i):
            return ins[i] if items[i][1] is None else ins[i].at[items[i][1]]

        def slot(i, p):
            return outs[i].at[4 * p[0] + 2 * p[1] + p[2]]

        def cp(i, k, block, to, s=None):
            return pltpu.make_async_remote_copy(
                src_ref=slot(i, block) if s is None else s, dst_ref=slot(i, block),
                send_sem=send.at[7 * i + k], recv_sem=recv.at[7 * i + k], device_id=to, device_id_type=MESH)

        mine = [pltpu.make_async_copy(src(i), slot(i, me), lsem.at[i]) for i in range(n)]
        for m_ in mine:
            m_.start()
        first = []
        for i in range(n):
            first.append(cp(i, 0, me, sib, src(i)))
            first += [cp(i, 1 + j, me, (*chip, c), src(i)) for j, chip in enumerate(chips)]
        for f in first:
            f.start()
        passed = []
        for j, chip in enumerate(chips):
            for i in range(n):
                cp(i, 1 + j, (*chip, c), me).wait_recv()
                p_ = cp(i, 4 + j, (*chip, c), sib)
                p_.start()
                passed.append(p_)
        for i in range(n):
            cp(i, 0, sib, me).wait_recv()
            for j, chip in enumerate(chips):
                cp(i, 4 + j, (*chip, 1 - c), me).wait_recv()
        for f in first + passed:
            f.wait_send()
        for m_ in mine:
            m_.wait()

    res = _pcall(
        body, name=name, in_specs=[_ANY] * n, out_specs=[_ANY] * n,
        out_shape=[jax.ShapeDtypeStruct((NDEV,) + tuple(b), a.dtype) for b, (a, _) in zip(blks, items)],
        scratch_shapes=[pltpu.SemaphoreType.DMA((7 * n,)), pltpu.SemaphoreType.DMA((7 * n,)),
                        pltpu.SemaphoreType.DMA((n,))],
    )(*[a for a, _ in items])
    return list(res)


_HBM = pl.BlockSpec(memory_space=pltpu.HBM)
_SEM = pl.BlockSpec(memory_space=pltpu.SEMAPHORE)
_EFFECT = pltpu.SideEffectType.DATAFLOW_SIDE_EFFECTING


def _xchg_copy(src_ref, land_ref, send, recv, r, scatter, at_peer):
    x, y, c = _coords()
    px = jnp.bitwise_xor(x, (r >> 2) & 1)
    py = jnp.bitwise_xor(y, (r >> 1) & 1)
    pc = jnp.bitwise_xor(c, r & 1)
    p_i = 4 * px + 2 * py + pc
    me_i = 4 * x + 2 * y + c
    return pltpu.make_async_remote_copy(
        src_ref=src_ref.at[p_i] if scatter else src_ref, dst_ref=land_ref.at[p_i if at_peer else me_i],
        send_sem=send.at[r - 1], recv_sem=recv.at[r - 1], device_id=(px, py, pc), device_id_type=MESH)


def _phase(body, name, bufs, sems_in=(), new_sems=(), after=None, token=False):
    nb, ns, nn = len(bufs), len(sems_in), len(new_sems)

    def wrapped(*refs):
        outs = refs[nb + ns + (after is not None):]
        body(refs[:nb], refs[nb:nb + ns], outs[:nn])
        if token:
            outs[nn + nb][...] = jnp.zeros_like(outs[nn + nb])

    res = _pcall(
        wrapped, name=name,
        out_shape=tuple([pltpu.SemaphoreType.DMA((k,)) for k in new_sems] + [pltpu.HBM(a.shape, a.dtype) for a in bufs]
                        + ([jax.ShapeDtypeStruct((8, LANES), F32)] if token else [])),
        in_specs=[_HBM] * nb + [_SEM] * ns + ([] if after is None else [_ANY]),
        out_specs=tuple([_SEM] * nn + [_HBM] * nb + ([pl.BlockSpec(memory_space=pltpu.VMEM)] if token else [])),
        input_output_aliases={i: nn + i for i in range(nb)},
        compiler_params=pltpu.CompilerParams(has_side_effects=_EFFECT),
    )(*[pltpu.with_memory_space_constraint(a, pltpu.HBM) for a in bufs], *sems_in, *([] if after is None else [after]))
    return list(res[nn:nn + nb]), list(res[:nn]), (res[nn + nb][0:1, 0:1] if token else None)


def scatter_start(srcs, name, after=None):
    n = len(srcs)
    lands = [lax.empty(s.shape, s.dtype) for s in srcs]

    def body(b, taken, new):
        me_i = _me()
        for i in range(n):
            pltpu.make_async_copy(b[i].at[me_i], b[n + i].at[me_i], new[3 * i + 2].at[0]).start()
            for r in range(1, NDEV):
                _xchg_copy(b[i], b[n + i], new[3 * i], new[3 * i + 1], r, True, False).start()

    bufs, sems, tok = _phase(body, name, list(srcs) + lands, new_sems=[NDEV - 1, NDEV - 1, 1] * n, after=after, token=True)
    return (bufs, sems), tok


def scatter_wait(handle, after, name):
    bufs, sems = handle
    n = len(bufs) // 2

    def body(b, taken, new):
        me_i = _me()
        for i in range(n):
            pltpu.make_async_copy(b[i].at[me_i], b[n + i].at[me_i], taken[3 * i + 2].at[0]).wait()
            for r in range(1, NDEV):
                cp = _xchg_copy(b[i], b[n + i], taken[3 * i], taken[3 * i + 1], r, True, True)
                cp.wait_send()
                cp.wait_recv()

    return _phase(body, name, bufs, sems_in=sems, after=after)[0][n:]


def _gather_peers():
    x, y, c = _coords()
    return (x, y, c), (x, y, 1 - c), [(1 - x, y), (x, 1 - y), (1 - x, 1 - y)]


def _row(p):
    return 4 * p[0] + 2 * p[1] + p[2]


def _gcopy(src_ref, land_ref, send, recv, k, block, to):
    return pltpu.make_async_remote_copy(
        src_ref=land_ref.at[_row(block)] if src_ref is None else src_ref, dst_ref=land_ref.at[_row(block)],
        send_sem=send.at[k], recv_sem=recv.at[k], device_id=to, device_id_type=MESH)


def gather_start(srcs, name, after=None):
    n = len(srcs)
    lands = [lax.empty((NDEV,) + s.shape, s.dtype) for s in srcs]

    def body(b, taken, new):
        me, sib, chips = _gather_peers()
        for i in range(n):
            send, recv = new[3 * i], new[3 * i + 1]
            pltpu.make_async_copy(b[i], b[n + i].at[_row(me)], new[3 * i + 2].at[0]).start()
            for j, chip in enumerate(chips):
                _gcopy(b[i], b[n + i], send, recv, 1 + j, me, (*chip, me[2])).start()
            _gcopy(b[i], b[n + i], send, recv, 0, me, sib).start()

    bufs, sems, tok = _phase(body, name, list(srcs) + lands, new_sems=[4, 4, 1] * n, after=after, token=True)
    return (bufs, sems), tok


def gather_mid(handle, after, name):
    bufs, sems = handle
    n = len(bufs) // 2

    def body(b, taken, new):
        me, sib, chips = _gather_peers()
        for j, chip in enumerate(chips):
            for i in range(n):
                _gcopy(b[i], b[n + i], taken[3 * i], taken[3 * i + 1], 1 + j, (*chip, me[2]), me).wait_recv()
                _gcopy(None, b[n + i], new[2 * i], new[2 * i + 1], j, (*chip, me[2]), sib).start()
        for i in range(n):
            send, recv = taken[3 * i], taken[3 * i + 1]
            _gcopy(b[i], b[n + i], send, recv, 0, sib, me).wait_recv()
            for k in range(4):
                _gcopy(b[i], b[n + i], send, recv, k, me, sib).wait_send()
            pltpu.make_async_copy(b[i], b[n + i].at[_row(me)], taken[3 * i + 2].at[0]).wait()

    bufs, new, tok = _phase(body, name, bufs, sems_in=sems, new_sems=[3, 3] * n, after=after, token=True)
    return (bufs, new), tok


def gather_wait(handle, after, name):
    bufs, sems = handle
    n = len(bufs) // 2

    def body(b, taken, new):
        me, sib, chips = _gather_peers()
        for i in range(n):
            for j, chip in enumerate(chips):
                _gcopy(None, b[n + i], taken[2 * i], taken[2 * i + 1], j, (*chip, me[2]), sib).wait_send()
                _gcopy(None, b[n + i], taken[2 * i], taken[2 * i + 1], j, (*chip, 1 - me[2]), me).wait_recv()

    return _phase(body, name, bufs, sems_in=sems, after=after)[0][n:]


_PACK_ALIGN = 8 * LANES


def _pack(arrs, aligned=False):
    parts = []
    for a in arrs:
        f = a.reshape(-1).astype(F32)
        pad = (-f.shape[0]) % _PACK_ALIGN if aligned else 0
        parts.append(jnp.pad(f, (0, pad)) if pad else f)
    flat = jnp.concatenate(parts)
    pad = (-flat.shape[0]) % _PACK_ALIGN
    return (jnp.pad(flat, (0, pad)) if pad else flat).reshape(-1, LANES)


def _unpack(p, shapes, lead=(), aligned=False):
    nl = len(lead)
    flat = p.reshape(lead + (-1,))
    out, off = [], 0
    for shp in shapes:
        n = 1
        for d in shp:
            n *= d
        out.append(lax.slice_in_dim(flat, off, off + n, axis=nl).reshape(lead + tuple(shp)))
        off += n + ((-n) % _PACK_ALIGN if aligned else 0)
    return out


def _shard_cols(a, me, width, axis):
    return lax.dynamic_slice_in_dim(a, me * width, width, axis=axis)


def kernel(x, c, norm1_g, norm2_g, ada_w, ada_b, mlp_w1, mlp_w2, ab_w_in, sgu_norm_g, sgu_w, sgu_b, conv_w, conv_b, conv_ln_g, conv_ln_b, ab_w_out, mla_w_in, mla_q_norm_g, mla_kv_norm_g, mla_w_uq, mla_w_ukv, mla_q_head_g, mla_k_head_g, mla_w_out, loss_target, m_norm1_g, m_norm2_g, m_ada_w, m_ada_b, m_mlp_w1, m_mlp_w2, m_ab_w_in, m_sgu_norm_g, m_sgu_w, m_sgu_b, m_conv_w, m_conv_b, m_conv_ln_g, m_conv_ln_b, m_ab_w_out, m_mla_w_in, m_mla_q_norm_g, m_mla_kv_norm_g, m_mla_w_uq, m_mla_w_ukv, m_mla_q_head_g, m_mla_k_head_g, m_mla_w_out, v_norm1_g, v_norm2_g, v_ada_w, v_ada_b, v_mlp_w1, v_mlp_w2, v_ab_w_in, v_sgu_norm_g, v_sgu_w, v_sgu_b, v_conv_w, v_conv_b, v_conv_ln_g, v_conv_ln_b, v_ab_w_out, v_mla_w_in, v_mla_q_norm_g, v_mla_kv_norm_g, v_mla_w_uq, v_mla_w_ukv, v_mla_q_head_g, v_mla_k_head_g, v_mla_w_out):
    W = dict(norm1_g=norm1_g, norm2_g=norm2_g, ada_w=ada_w, ada_b=ada_b, mlp_w1=mlp_w1, mlp_w2=mlp_w2, ab_w_in=ab_w_in,
             sgu_norm_g=sgu_norm_g, sgu_w=sgu_w, sgu_b=sgu_b, conv_w=conv_w, conv_b=conv_b, conv_ln_g=conv_ln_g,
             conv_ln_b=conv_ln_b, ab_w_out=ab_w_out, mla_w_in=mla_w_in, mla_q_norm_g=mla_q_norm_g,
             mla_kv_norm_g=mla_kv_norm_g, mla_w_uq=mla_w_uq, mla_w_ukv=mla_w_ukv, mla_q_head_g=mla_q_head_g,
             mla_k_head_g=mla_k_head_g, mla_w_out=mla_w_out)
    M = dict(norm1_g=m_norm1_g, norm2_g=m_norm2_g, ada_w=m_ada_w, ada_b=m_ada_b, mlp_w1=m_mlp_w1, mlp_w2=m_mlp_w2,
             ab_w_in=m_ab_w_in, sgu_norm_g=m_sgu_norm_g, sgu_w=m_sgu_w, sgu_b=m_sgu_b, conv_w=m_conv_w, conv_b=m_conv_b,
             conv_ln_g=m_conv_ln_g, conv_ln_b=m_conv_ln_b, ab_w_out=m_ab_w_out, mla_w_in=m_mla_w_in,
             mla_q_norm_g=m_mla_q_norm_g, mla_kv_norm_g=m_mla_kv_norm_g, mla_w_uq=m_mla_w_uq, mla_w_ukv=m_mla_w_ukv,
             mla_q_head_g=m_mla_q_head_g, mla_k_head_g=m_mla_k_head_g, mla_w_out=m_mla_w_out)
    V = dict(norm1_g=v_norm1_g, norm2_g=v_norm2_g, ada_w=v_ada_w, ada_b=v_ada_b, mlp_w1=v_mlp_w1, mlp_w2=v_mlp_w2,
             ab_w_in=v_ab_w_in, sgu_norm_g=v_sgu_norm_g, sgu_w=v_sgu_w, sgu_b=v_sgu_b, conv_w=v_conv_w, conv_b=v_conv_b,
             conv_ln_g=v_conv_ln_g, conv_ln_b=v_conv_ln_b, ab_w_out=v_ab_w_out, mla_w_in=v_mla_w_in,
             mla_q_norm_g=v_mla_q_norm_g, mla_kv_norm_g=v_mla_kv_norm_g, mla_w_uq=v_mla_w_uq, mla_w_ukv=v_mla_w_ukv,
             mla_q_head_g=v_mla_q_head_g, mla_k_head_g=v_mla_k_head_g, mla_w_out=v_mla_w_out)
    ORDER = list(W)

    S, D = x.shape[1], x.shape[2]
    L, NE, NO = norm1_g.shape[0], ab_w_in.shape[0], mla_w_in.shape[0]
    DA = D // 2
    DB = D - DA
    G = DA // GROUP
    R = NDEV * mla_q_norm_g.shape[1]
    H = NDEV * mla_w_uq.shape[2] // QK
    AW = ada_w.shape[2]
    CB = conv_w.shape[2]
    me = _me()
    xs, tgt = x[0], loss_target[0]

    BIG_EVEN = ("mlp_w1", "mlp_w2", "ab_w_in", "ab_w_out")
    BIG_ODD = ("mlp_w1", "mlp_w2", "mla_w_in", "mla_w_uq", "mla_w_ukv", "mla_w_out")
    BIG = ("mlp_w1", "mlp_w2", "ab_w_in", "ab_w_out", "mla_w_in", "mla_w_uq", "mla_w_ukv", "mla_w_out")
    COL_SHARDED = ("mlp_w1", "ab_w_in", "mla_w_uq", "mla_w_ukv")
    MLP_W = ("mlp_w1", "mlp_w2")
    mixer_w = lambda l: ("ab_w_in", "ab_w_out") if l % 2 == 0 else ("mla_w_in", "mla_w_uq", "mla_w_ukv", "mla_w_out")
    widx = lambda k, l: l if k in MLP_W else l // 2

    small_in = [c, mla_q_norm_g, mla_kv_norm_g, conv_w]
    sg = all_gather([(_pack(small_in, True), None)], "gather_small")[0]
    c_all, qng_all, kvng_all, cw_all = _unpack(sg, [a.shape for a in small_in], (NDEV,), True)
    c_all = c_all.reshape(NDEV, D)
    qng_full = jnp.transpose(qng_all, (1, 0, 2)).reshape(NO, 1, R)
    kvng_full = jnp.transpose(kvng_all, (1, 0, 2)).reshape(NO, 1, R)
    cw_full = jnp.transpose(cw_all, (1, 2, 0, 3)).reshape(NE, CONV_W, DB)
    cw_pad = jnp.pad(cw_full, ((0, 0), (0, CONV_PAD - CONV_W), (0, 0)))

    def silu_fn(t, c_, o):
        v_ = t[0][...]
        o[0][...] = v_ * _sigmoid(v_)
        return []

    c_act = rowwise(silu_fn, [c_all], outs=[((NDEV, D), F32)], ts=NDEV, name="silu_c")[0]
    bias_cols = _shard_cols(ada_b, me, AW, 1).reshape(1, L * AW)
    mod_cols = mm(c_act, ada_w, "nn", name="ada_fwd", b_blocked=True, rowvecs=[bias_cols],
                  epi=lambda acc, b_: (acc + b_,), tm=NDEV, tn=768)
    mod_all = all_gather([(mod_cols, None)], "gather_mod")[0]
    mod = lax.dynamic_index_in_dim(mod_all, me, axis=1, keepdims=False)
    mod = jnp.transpose(mod.reshape(NDEV, L, AW), (1, 0, 2)).reshape(L, 6, 1, D)

    wnames = lambda l, what: mixer_w(l) if what == "mix" else MLP_W
    g_first, g_second = {}, {}
    tok_sum = jnp.zeros((1, 1), F32)
    for l in range(L):
        for what in ("mix", "mlp"):
            srcs = [cast_bf16(W[k], widx(k, l), "cast_%s_l%d" % (k, l)) for k in wnames(l, what)]
            g_first[l, what], tok = gather_start(srcs, "gather_start_%s_l%d" % (what, l), after=mod_all)
            tok_sum = tok_sum + tok
    mod = mod + tok_sum

    def pass_on(l, what, after):
        g_second[l, what], tok = gather_mid(g_first[l, what], after, "gather_mid_%s_l%d" % (what, l))
        return tok

    def wait_weights(l, what, after):
        lands = gather_wait(g_second[l, what], after, "gather_wait_%s_l%d" % (what, l))
        return {k: (ld if k in COL_SHARDED else ld.reshape(NDEV * ld.shape[1], ld.shape[2]))
                for k, ld in zip(wnames(l, what), lands)}

    mod = mod + pass_on(0, "mix", mod)

    pos = jnp.arange(S, dtype=F32)
    inv = ROPE_THETA ** (-jnp.arange(0, ROPE, 2, dtype=F32) / ROPE)
    ang = pos[:, None] * inv[None, :]
    cos2 = jnp.concatenate([jnp.cos(ang), jnp.cos(ang)], axis=1)
    sinm = jnp.concatenate([-jnp.sin(ang), jnp.sin(ang)], axis=1)

    residual = lambda acc, xr, gt: (acc, xr + gt * acc)

    saved = []
    xc = xs
    for l in range(L):
        sh1, sc1, g1, sh2, sc2, g2 = [mod[l, k] for k in range(6)]
        tag = "_l%d" % l
        sv = dict(x0=xc)
        h, sv["rstd1"] = prenorm(xc, norm1_g[l][None], sc1, sh1, "prenorm1" + tag)
        sv["h"] = h
        wl = wait_weights(l, "mix", h)
        if l % 2 == 0:
            e = l // 2
            ng = sgu_norm_g[e].reshape(1, DA)
            bcol = sgu_b[e][:, :, None]
            proj = mm(h, wl["ab_w_in"], "nn", name="ab_in" + tag, b_blocked=True)
            out_a = sgu_fwd(proj, ng, sgu_w[e], bcol, DA, "sgu_fwd" + tag)
            y1 = conv_fwd(proj, cw_pad[e], conv_b[e][None], DA, DB, "conv_fwd" + tag)
            out_b = ln_silu(y1, conv_ln_g[e][None], conv_ln_b[e][None], "ln_silu" + tag)
            cat = jnp.concatenate([out_a, out_b], axis=1)
            sv.update(proj=proj, y1=y1, cat=cat)
            mixb, x1 = mm(cat, wl["ab_w_out"], "nn", name="ab_out" + tag, out_dtypes=(BF16, F32), epi=residual,
                          extras=[xc], rowvecs=[g1])
        else:
            o_ = l // 2
            proj = mm(h, wl["mla_w_in"], "nn", name="mla_in" + tag)
            kgr = mla_k_head_g[o_][None, NOPE:QK]
            qn, kvn, kr = mla_norms(proj, cos2, sinm, qng_full[o_], kvng_full[o_], kgr, R, "mla_norms" + tag)
            q = mm(qn, wl["mla_w_uq"], "nn", name="mla_uq" + tag, b_blocked=True)
            kv = mm(kvn, wl["mla_w_ukv"], "nn", name="mla_ukv" + tag, b_blocked=True)
            Qh, Kh, Vh = mla_heads(q, kv, kr, cos2, sinm, mla_q_head_g[o_][None], mla_k_head_g[o_][None], H,
                                   "mla_heads" + tag)
            att, lse = attn_fwd(Qh, Kh, Vh, "attn_fwd" + tag)
            sv.update(proj=proj, qn=qn, kvn=kvn, q=q, kv=kv, Qh=Qh, Kh=Kh, Vh=Vh, att=att, lse=lse)
            mixb, x1 = mm(att, wl["mla_w_out"], "nn", name="mla_out" + tag, out_dtypes=(BF16, F32), epi=residual,
                          extras=[xc], rowvecs=[g1])
        sv.update(mixb=mixb, x1=x1)
        h2, sv["rstd2"] = prenorm(x1, norm2_g[l][None], sc2 + pass_on(l, "mlp", x1), sh2, "prenorm2" + tag)
        wl.update(wait_weights(l, "mlp", h2))
        sv["w"] = wl
        z, act = mm(h2, wl["mlp_w1"], "nn", name="mlp_up" + tag, b_blocked=True, out_dtypes=(BF16, BF16),
                    epi=lambda acc: (acc, jnp.square(jnp.maximum(acc, 0.0))), tm=MLP_ROWS)
        g2t = g2 + pass_on(l + 1, "mix", z) if l + 1 < L else g2
        yb, xc = mm(act, wl["mlp_w2"], "nn", name="mlp_down" + tag, out_dtypes=(BF16, F32), epi=residual,
                    extras=[x1], rowvecs=[g2t])
        sv.update(h2=h2, z=z, act=act, yb=yb)
        saved.append(sv)

    dx, dy, loss_cols, dgate2 = loss_grad(xc, tgt, saved[L - 1]["yb"], mod[L - 1, 5], "loss")
    loss = lax.psum(0.5 / D * jnp.sum(loss_cols), ("x", "y", "c"))

    big_out = {}
    sm = {k: [None] * W[k].shape[0] for k in ("norm1_g", "norm2_g", "sgu_norm_g", "sgu_w", "sgu_b", "conv_b", "conv_ln_g",
                                               "conv_ln_b", "mla_q_head_g", "mla_k_head_g", "mla_q_norm_g",
                                               "mla_kv_norm_g", "conv_w")}
    dmod = [None] * L
    flat2 = {k: W[k].reshape(-1, W[k].shape[2]) for k in BIG}
    flat2m = {k: M[k].reshape(-1, W[k].shape[2]) for k in BIG}
    flat2v = {k: V[k].reshape(-1, W[k].shape[2]) for k in BIG}

    def send_grads(gr, what, tag, after=None):
        names = list(gr)
        blocks = [gr[k] if k in COL_SHARDED else gr[k].reshape(NDEV, gr[k].shape[0] // NDEV, gr[k].shape[1])
                  for k in names]
        handle, tok = scatter_start(blocks, "scatter_start_%s%s" % (what, tag), after=after)
        return (names, handle, what, tag), tok

    rep = ("norm1_g", "norm2_g", "sgu_norm_g", "sgu_w", "sgu_b", "conv_b", "conv_ln_g", "conv_ln_b", "mla_q_head_g",
           "mla_k_head_g")
    part_full = {"mla_q_norm_g": (NO, R), "mla_kv_norm_g": (NO, R), "conv_w": (NE, CONV_W, DB)}
    small = ["ada_b"] + list(rep) + list(part_full)
    shapes = [(L, 6 * D)] + [W[k].shape for k in rep] + list(part_full.values())

    def small_gather():
        parts = [jnp.stack(dmod).reshape(L, 6 * D)] + [jnp.stack(sm[k]).reshape(s_) for k, s_ in zip(small[1:], shapes[1:])]
        return all_gather([(_pack(parts), None)], "gather_smallgrads")[0]

    def scatter_finish(pending, after, l):
        names, handle, what, tag = pending
        landed = scatter_wait(handle, after, "scatter_wait_%s%s" % (what, tag))
        for k, land in zip(names, landed):
            li = widx(k, l)
            big_out[k] = adamw(flat2[k], flat2m[k], flat2v[k], land, row0=li * W[k].shape[1],
                               name="adamw_%s%s" % (k, tag), into=big_out.get(k, True))

    pend_mix, tok_mix = None, None
    for l in reversed(range(L)):
        sh1, sc1, g1, sh2, sc2, g2 = [mod[l, k] for k in range(6)]
        sv = saved[l]
        wl = sv["w"]
        tag = "_l%d" % l
        gr = {}
        dw2 = mm(sv["act"], dy, "tn", name="mlp_down_dw" + tag, out_dtypes=(BF16,), tm=MLP_ROWS)
        pend_w2, tok_w2 = send_grads({"mlp_w2": dw2}, "w2", tag)
        dz = mm(dy, wl["mlp_w2"], "nt", name="mlp_down_dx" + tag, out_dtypes=(BF16,), extras=[sv["z"]],
                rowvecs=[jnp.zeros((1, sv["z"].shape[1]), F32) + tok_w2],
                epi=lambda acc, z_, t_: (acc * (2.0 * jnp.maximum(z_.astype(F32), 0.0)) + t_,), tm=MLP_ROWS)
        dh2 = mm(dz, wl["mlp_w1"], "nt", name="mlp_up_dx" + tag, b_blocked=True)
        dw1 = mm(sv["h2"], dz, "tn", name="mlp_up_dw" + tag, out_dtypes=(BF16,), out_blocked=NDEV, tm=MLP_ROWS)
        pend_mlp, tok_mlp = send_grads({"mlp_w1": dw1}, "w1", tag)
        dx1, dsc2, dsh2, sm["norm2_g"][l], dmix, dgate1 = norm_bwd(
            dh2, sv["x1"], sv["rstd2"], dx, norm2_g[l][None], sc2 + tok_mlp, "norm2_bwd" + tag, gated=(sv["mixb"], g1))
        if pend_mix is not None:
            scatter_finish(pend_mix, dx1, l + 1)
        if l % 2 == 0:
            e = l // 2
            ng = sgu_norm_g[e].reshape(1, DA)
            bcol = sgu_b[e][:, :, None]
            dcat = mm(dmix, wl["ab_w_out"], "nt", name="ab_out_dx" + tag)
            gr["ab_w_out"] = mm(sv["cat"], dmix, "tn", name="ab_out_dw" + tag, out_dtypes=(BF16,))
            dy1, sm["conv_ln_g"][e], sm["conv_ln_b"][e], sm["conv_b"][e] = ln_silu_bwd(
                dcat, sv["y1"], conv_ln_g[e][None], conv_ln_b[e][None], "ln_silu_bwd" + tag)
            da, dg_, dwc = conv_bwd(sv["proj"], dy1, cw_pad[e], DA, DB, "conv_bwd" + tag)
            sm["conv_w"][e] = dwc[:CONV_W]
            duv, dsw, dsb, dsng = sgu_bwd(sv["proj"], dcat, ng, sgu_w[e], bcol, DA, "sgu_bwd" + tag)
            sm["sgu_w"][e], sm["sgu_b"][e], sm["sgu_norm_g"][e] = dsw, dsb, dsng
            dproj = jnp.concatenate([duv, da, dg_], axis=1)
            dh = mm(dproj, wl["ab_w_in"], "nt", name="ab_in_dx" + tag, b_blocked=True)
            gr["ab_w_in"] = mm(sv["h"], dproj, "tn", name="ab_in_dw" + tag, out_dtypes=(BF16,), out_blocked=NDEV)
        else:
            o_ = l // 2
            kgr = mla_k_head_g[o_][None, NOPE:QK]
            dO = mm(dmix, wl["mla_w_out"], "nt", name="mla_out_dx" + tag, out_dtypes=(BF16,))
            gr["mla_w_out"] = mm(sv["att"], dmix, "tn", name="mla_out_dw" + tag, out_dtypes=(BF16,))
            dQ, dK, dV = attn_bwd(sv["Qh"], sv["Kh"], sv["Vh"], dO, sv["att"], sv["lse"], "attn_bwd" + tag)
            dq_pre, dkv_pre, dkr, dqgn, dqgr, dkgn = mla_heads_bwd(
                dQ, dK, dV, sv["q"], sv["kv"], cos2, sinm, mla_q_head_g[o_][None], mla_k_head_g[o_][None], H,
                "mla_heads_bwd" + tag)
            dqn = mm(dq_pre, wl["mla_w_uq"], "nt", name="mla_uq_dx" + tag, b_blocked=True)
            gr["mla_w_uq"] = mm(sv["qn"], dq_pre, "tn", name="mla_uq_dw" + tag, out_dtypes=(BF16,), out_blocked=NDEV)
            dkvn = mm(dkv_pre, wl["mla_w_ukv"], "nt", name="mla_ukv_dx" + tag, b_blocked=True)
            gr["mla_w_ukv"] = mm(sv["kvn"], dkv_pre, "tn", name="mla_ukv_dw" + tag, out_dtypes=(BF16,),
                                 out_blocked=NDEV)
            dproj, sm["mla_q_norm_g"][o_], sm["mla_kv_norm_g"][o_], dkgr = mla_norms_bwd(
                dqn, dkvn, dkr, sv["proj"], qng_full[o_], kvng_full[o_], kgr, R, "mla_norms_bwd" + tag)
            sm["mla_q_head_g"][o_] = jnp.concatenate([dqgn, dqgr], axis=1)
            sm["mla_k_head_g"][o_] = jnp.concatenate([dkgn, dkgr], axis=1)
            dh = mm(dproj, wl["mla_w_in"], "nt", name="mla_in_dx" + tag)
            gr["mla_w_in"] = mm(sv["h"], dproj, "tn", name="mla_in_dw" + tag, out_dtypes=(BF16,))
        if l > 0:
            pend_mix, tok_mix = send_grads(gr, "mix", tag)
            dx, dsc1, dsh1, sm["norm1_g"][l], dy, dgate2_below = norm_bwd(
                dh, sv["x0"], sv["rstd1"], dx1, norm1_g[l][None], sc1 + tok_mix, "norm1_bwd" + tag,
                gated=(saved[l - 1]["yb"], mod[l - 1, 5]))
        else:
            dx, dsc1, dsh1, sm["norm1_g"][l] = norm_bwd(dh, sv["x0"], sv["rstd1"], dx1, norm1_g[l][None], sc1,
                                                        "norm1_bwd" + tag)
        dmod[l] = jnp.concatenate([dsh1, dsc1, dgate1, dsh2, dsc2, dgate2], axis=1)
        if l > 0:
            dgate2 = dgate2_below
        else:
            gp = small_gather()
            pend_mix, tok_mix = send_grads(gr, "mix", tag, after=gp)
        scatter_finish(pend_w2, dx, l)
        scatter_finish(pend_mlp, dx, l)

    rows_p = gp.shape[1]
    per = 6 * D // LANES
    dm = lax.slice_in_dim(gp, 0, L * per, axis=1).reshape(NDEV, L, per, LANES)
    dmod_cols = lax.dynamic_slice_in_dim(dm, me * (AW // LANES), AW // LANES, axis=2).reshape(NDEV, L * AW) + tok_mix

    def sum_fn(t, c_, o):
        acc = t[0][0]
        for s_ in range(1, NDEV):
            acc = acc + t[0][s_]
        o[0][...] = acc
        return []

    gsummed = rowwise(sum_fn, [gp], outs=[((rows_p, LANES), F32)], ts=_tile(rows_p, 256, 8), name="sum_smallgrads")[0]
    gsum = dict(zip(small, _unpack(gsummed, shapes)))
    gsum["mla_q_norm_g"] = _shard_cols(gsum["mla_q_norm_g"], me, R // NDEV, 1)
    gsum["mla_kv_norm_g"] = _shard_cols(gsum["mla_kv_norm_g"], me, R // NDEV, 1)
    gsum["conv_w"] = _shard_cols(gsum["conv_w"], me, CB, 2)
    sm_shapes = [W[k].shape for k in small]
    sres = adamw(_pack([W[k] for k in small]), _pack([M[k] for k in small]), _pack([V[k] for k in small]),
                 (_pack([gsum[k] for k in small]) + tok_mix)[None], row0=0, name="adamw_small")
    small_out = {k: vals for k, vals in zip(small, zip(*[_unpack(r_, sm_shapes) for r_ in sres]))}

    g_ada = mm(c_act, dmod_cols, "tn", name="ada_dw", out_blocked=L, tm=1024, tn=768, tk=NDEV, cast=None,
               precision=lax.Precision.HIGHEST)
    ada_out = adamw(ada_w.reshape(L * D, AW), m_ada_w.reshape(L * D, AW), v_ada_w.reshape(L * D, AW),
                    g_ada.reshape(1, L * D, AW), row0=0, name="adamw_ada_w")
    scatter_finish(pend_mix, ada_out[1], 0)
    ada_out = [a.reshape(L, D, AW) for a in ada_out]

    def result(k, which):
        if k == "ada_w":
            return ada_out[which]
        if k in BIG:
            return big_out[k][which].reshape(W[k].shape)
        return small_out[k][which]

    outs = [loss, dx[None]]
    for which in range(4):
        outs += [result(k, which) for k in ORDER]
    return tuple(outs)
```

```python
import functools

import jax
import jax.numpy as jnp
from jax import lax
from jax.experimental import pallas as pl
from jax.experimental.pallas import tpu as pltpu

F32 = jnp.float32
BF16 = jnp.bfloat16
EPS = 1e-6
NDEV = 8
LANES = 128
CHUNK = 128
GROUP = 128
CONV_W = 31
CONV_PAD = 32
NOPE, ROPE, VDIM = 128, 64, 128
QK = NOPE + ROPE
ROPE_THETA = 10000.0
VMEM_LIMIT = 56 * 1024 * 1024
ADAM_LR, ADAM_B1, ADAM_B2, ADAM_EPS, ADAM_WD, ADAM_STEP = 0.001, 0.9, 0.999, 1e-08, 0.01, 10
MESH = pl.DeviceIdType.MESH
NEG = -1e30
MLP_ROWS = 2048
ROW_TILE = 256
HEAD_ROWS = 256
HEAD_ROWS_FWD = 512
ATTN_BLOCK = 1024
ATTN_STRIP = 64


def _pcall(body, **kw):
    return pl.pallas_call(body, **kw)


def _params(sem=None):
    return pltpu.CompilerParams(dimension_semantics=sem, vmem_limit_bytes=VMEM_LIMIT)


def _tile(dim, target, align=LANES):
    if dim <= target:
        return dim
    t = (target // align) * align
    while t >= align:
        if dim % t == 0:
            return t
        t -= align
    return dim


def _rstd(x):
    return lax.rsqrt(jnp.mean(x * x, axis=-1, keepdims=True) + EPS)


def _sigmoid(x):
    return 1.0 / (1.0 + jnp.exp(-x))


_GC = 0.7978845608028654


def _gelu(x):
    return 0.5 * x * (1.0 + jnp.tanh(_GC * (x + 0.044715 * x * x * x)))


def _gelu_grad(x):
    t = jnp.tanh(_GC * (x + 0.044715 * x * x * x))
    return 0.5 * (1.0 + t) + 0.5 * x * (1.0 - t * t) * _GC * (1.0 + 3 * 0.044715 * x * x)


def _colsum(x):
    return jnp.sum(x, axis=0, keepdims=True)


def _rms_bwd(dy, xhat, rstd, g):
    dxh = dy * g
    dx = rstd * (dxh - xhat * jnp.mean(dxh * xhat, axis=-1, keepdims=True))
    return dx, _colsum(dy * xhat)


def _swap_halves(x):
    h = x.shape[-1] // 2
    return jnp.concatenate([x[:, h:], x[:, :h]], axis=1)


def _rope(x, cos2, sinm):
    return x * cos2 + _swap_halves(x) * sinm


def _unrope(dy, cos2, sinm):
    return dy * cos2 + _swap_halves(dy * sinm)


_DIMS = {"nn": (((1,), (0,)), ((), ())), "nt": (((1,), (1,)), ((), ())), "tn": (((0,), (0,)), ((), ()))}


def mm(a, b, mode, *, name, out_dtypes=(F32,), epi=None, extras=(), rowvecs=(), b_blocked=False, out_blocked=0,
       tm=1024, tn=1024, tk=2048, precision=None, cast=BF16):
    if mode == "tn":
        K, M = a.shape
    else:
        M, K = a.shape
    if b_blocked:
        J, Rb, Cb = b.shape
        N = Rb if mode == "nt" else J * Cb
    else:
        N = b.shape[0] if mode == "nt" else b.shape[1]
    tm = _tile(M, tm)
    nb = 1
    if mode == "nn" and b_blocked:
        if Cb >= tn:
            tn = _tile(Cb, tn)
        else:
            nb = max(d for d in range(1, J + 1) if J % d == 0 and d * Cb <= tn)
            tn = nb * Cb
    elif out_blocked:
        tn = _tile(N // out_blocked, tn)
    else:
        tn = _tile(N, tn)
    kb = 1
    if mode == "nt" and b_blocked:
        if Cb >= tk:
            tk = _tile(Cb, tk)
        else:
            kb = max(d for d in range(1, J + 1) if J % d == 0 and d * Cb <= tk)
            tk = kb * Cb
    else:
        tk = _tile(K, tk)
    nk = K // tk
    grid = (M // tm, N // tn, nk)

    if mode == "tn":
        a_spec = pl.BlockSpec((tk, tm), lambda i, j, k: (k, i))
    else:
        a_spec = pl.BlockSpec((tm, tk), lambda i, j, k: (i, k))
    if mode == "nn":
        if b_blocked:
            if nb > 1:
                b_spec = pl.BlockSpec((nb, tk, Cb), lambda i, j, k: (j, k, 0))
            else:
                nper = Cb // tn
                b_spec = pl.BlockSpec((None, tk, tn), lambda i, j, k: (j // nper, k, j % nper))
        else:
            b_spec = pl.BlockSpec((tk, tn), lambda i, j, k: (k, j))
    elif mode == "nt":
        if b_blocked:
            if kb > 1:
                b_spec = pl.BlockSpec((kb, tn, Cb), lambda i, j, k: (k, j, 0))
            else:
                kper = Cb // tk
                b_spec = pl.BlockSpec((None, tn, tk), lambda i, j, k: (k // kper, j, k % kper))
        else:
            b_spec = pl.BlockSpec((tn, tk), lambda i, j, k: (j, k))
    else:
        b_spec = pl.BlockSpec((tk, tn), lambda i, j, k: (k, j))
    if out_blocked:
        oper = (N // out_blocked) // tn
        o_spec = pl.BlockSpec((None, tm, tn), lambda i, j, k: (j // oper, i, j % oper))
        o_shape = (out_blocked, M, N // out_blocked)
    else:
        o_spec = pl.BlockSpec((tm, tn), lambda i, j, k: (i, j))
        o_shape = (M, N)
    e_spec = pl.BlockSpec((tm, tn), lambda i, j, k: (i, j))
    r_spec = pl.BlockSpec((1, tn), lambda i, j, k: (0, j))
    ne, nr, no = len(extras), len(rowvecs), len(out_dtypes)
    dims = _DIMS[mode]

    def body(a_ref, b_ref, *rest):
        ex = rest[:ne]
        rv = rest[ne:ne + nr]
        outs = rest[ne + nr:ne + nr + no]

        def product():
            if nb > 1:
                av = a_ref[...] if cast is None else a_ref[...].astype(cast)
                return jnp.concatenate(
                    [lax.dot_general(av, b_ref[q] if cast is None else b_ref[q].astype(cast), dims,
                                     preferred_element_type=F32, precision=precision) for q in range(nb)], axis=1)
            if kb > 1:
                r = None
                for q in range(kb):
                    av, bv = a_ref[:, q * Cb:(q + 1) * Cb], b_ref[q]
                    if cast is not None:
                        av, bv = av.astype(cast), bv.astype(cast)
                    d = lax.dot_general(av, bv, dims, preferred_element_type=F32, precision=precision)
                    r = d if r is None else r + d
                return r
            av, bv = a_ref[...], b_ref[...]
            if cast is not None:
                av, bv = av.astype(cast), bv.astype(cast)
            return lax.dot_general(av, bv, dims, preferred_element_type=F32, precision=precision)

        def finish(r):
            vals = (r,) if epi is None else epi(r, *[e[...] for e in ex], *[v[...] for v in rv])
            for o, val in zip(outs, vals):
                o[...] = val.astype(o.dtype)

        if nk == 1:
            finish(product())
            return
        acc = rest[ne + nr + no]
        k = pl.program_id(2)

        @pl.when(k == 0)
        def _():
            acc[...] = product()

        @pl.when((k > 0) & (k < nk - 1))
        def _():
            acc[...] += product()

        @pl.when(k == nk - 1)
        def _():
            finish(acc[...] + product())

    res = _pcall(
        body, name=name, grid=grid,
        in_specs=[a_spec, b_spec] + [e_spec] * ne + [r_spec] * nr,
        out_specs=[o_spec] * no,
        out_shape=[jax.ShapeDtypeStruct(o_shape, dt) for dt in out_dtypes],
        scratch_shapes=[] if nk == 1 else [pltpu.VMEM((tm, tn), F32)],
        compiler_params=_params(("parallel", "parallel", "arbitrary")),
    )(a, b, *extras, *rowvecs)
    return res[0] if no == 1 else res


def rowwise(fn, tiled, consts=(), outs=(), reds=(), *, ts, name):
    specs = []
    arrs = []
    rows = None
    for t in tiled:
        a, w, cb = t if isinstance(t, tuple) else (t, None, 0)
        arrs.append(a)
        rows = a.shape[-2] if rows is None else rows
        if a.ndim == 2:
            specs.append(pl.BlockSpec((ts, a.shape[1] if w is None else w), lambda i, cb=cb: (i, cb)))
        else:
            specs.append(pl.BlockSpec((a.shape[0], ts, a.shape[2]), lambda i: (0, i, 0)))
    for a in consts:
        specs.append(pl.BlockSpec(a.shape, lambda i, n=a.ndim: (0,) * n))
    o_specs, o_shapes = [], []
    for shp, dt in outs:
        if len(shp) == 2:
            o_specs.append(pl.BlockSpec((ts, shp[1]), lambda i: (i, 0)))
        else:
            o_specs.append(pl.BlockSpec((shp[0], ts, shp[2]), lambda i: (0, i, 0)))
        o_shapes.append(jax.ShapeDtypeStruct(shp, dt))
    for shp in reds:
        o_specs.append(pl.BlockSpec(shp, lambda i, n=len(shp): (0,) * n))
        o_shapes.append(jax.ShapeDtypeStruct(shp, F32))
    nt, nc, no = len(arrs), len(consts), len(outs)

    def body(*refs):
        i = pl.program_id(0)
        red_refs = refs[nt + nc + no:]
        vals = fn(refs[:nt], refs[nt:nt + nc], refs[nt + nc:nt + nc + no])
        if red_refs:
            @pl.when(i == 0)
            def _():
                for r in red_refs:
                    r[...] = jnp.zeros_like(r)
            for r, v in zip(red_refs, vals):
                r[...] += v

    return _pcall(body, name=name, grid=(rows // ts,), in_specs=specs, out_specs=o_specs, out_shape=o_shapes,
                  compiler_params=_params(("arbitrary",)))(*arrs, *consts)


def cast_bf16(w, l, name):
    _, R, C = w.shape
    tr = _tile(R, 512, 16)

    def body(w_ref, o_ref):
        o_ref[...] = w_ref[...].astype(BF16)

    return _pcall(body, name=name, grid=(R // tr,), in_specs=[pl.BlockSpec((None, tr, C), lambda i: (l, i, 0))],
                  out_specs=pl.BlockSpec((tr, C), lambda i: (i, 0)), out_shape=jax.ShapeDtypeStruct((R, C), BF16),
                  compiler_params=_params(("parallel",)))(w)


def prenorm(x, g, scale, shift, name):
    S, D = x.shape

    def fn(t, c, o):
        xv = t[0][...]
        r = _rstd(xv)
        o[0][...] = ((xv * r * c[0][...]) * (1.0 + c[1][...]) + c[2][...]).astype(BF16)
        o[1][...] = r
        return []

    return rowwise(fn, [x], [g, scale, shift], [((S, D), BF16), ((S, 1), F32)], ts=_tile(S, ROW_TILE, 16), name=name)


def _gate_stage(dx, y_ref, gate_ref, dy_ref):
    dy_ref[...] = (dx * gate_ref[...]).astype(BF16)
    return _colsum(dx * y_ref[...].astype(F32))


def norm_bwd(dh, x, rstd, dres, g, scale, name, gated=None):
    S, D = x.shape

    def fn(t, c, o):
        d = t[0][...]
        r = t[2][...]
        xh = t[1][...] * r
        gv = c[0][...]
        dr = d * (1.0 + c[1][...])
        dx, dg = _rms_bwd(dr, xh, r, gv)
        dx = t[3][...] + dx
        o[0][...] = dx
        reds = [_colsum(d * (xh * gv)), _colsum(d), dg]
        if gated is not None:
            reds.append(_gate_stage(dx, t[4], c[2], o[1]))
        return reds

    ng = gated is not None
    res = rowwise(fn, [dh, x, rstd, dres] + ([gated[0]] if ng else []), [g, scale] + ([gated[1]] if ng else []),
                  [((S, D), F32)] + ([((S, D), BF16)] if ng else []), [(1, D)] * (3 + ng), ts=_tile(S, ROW_TILE, 8),
                  name=name)
    return (res[0], res[2], res[3], res[4], res[1], res[5]) if ng else res


def loss_grad(y, tgt, yb, gate, name):
    S, D = y.shape

    def fn(t, c, o):
        e = t[0][...] - t[1][...]
        dx = e * (1.0 / D)
        o[0][...] = dx
        return [_colsum(e * e), _gate_stage(dx, t[2], c[0], o[1])]

    return rowwise(fn, [y, tgt, yb], [gate], outs=[((S, D), F32), ((S, D), BF16)], reds=[(1, D)] * 2,
                   ts=_tile(S, ROW_TILE, 8), name=name)


def _tril_mask():
    r = lax.broadcasted_iota(jnp.int32, (CHUNK, CHUNK), 0)
    c = lax.broadcasted_iota(jnp.int32, (CHUNK, CHUNK), 1)
    return c <= r


def sgu_fwd(proj, ng, w, bcol, DA, name):
    S = proj.shape[0]
    G = DA // GROUP
    tr = _tile(S, 2 * CHUNK)

    def body(u_ref, v_ref, ng_ref, w_ref, b_ref, o_ref):
        mask = _tril_mask()
        for g in range(G):
            cols = slice(g * GROUP, (g + 1) * GROUP)
            wm = jnp.where(mask, w_ref[g], 0.0).astype(BF16)
            for ci in range(tr // CHUNK):
                rows = slice(ci * CHUNK, (ci + 1) * CHUNK)
                gv = _gelu(v_ref[rows, cols])
                vn = gv * _rstd(gv) * ng_ref[:, cols]
                mixed = jnp.dot(wm, vn.astype(BF16), preferred_element_type=F32) + b_ref[g]
                o_ref[rows, cols] = (_gelu(u_ref[rows, cols]) * mixed).astype(o_ref.dtype)

    return _pcall(
        body, name=name, grid=(S // tr,),
        in_specs=[pl.BlockSpec((tr, DA), lambda i: (i, 0)), pl.BlockSpec((tr, DA), lambda i: (i, 1)),
                  pl.BlockSpec((1, DA), lambda i: (0, 0)), pl.BlockSpec((G, CHUNK, CHUNK), lambda i: (0, 0, 0)),
                  pl.BlockSpec((G, CHUNK, 1), lambda i: (0, 0, 0))],
        out_specs=pl.BlockSpec((tr, DA), lambda i: (i, 0)),
        out_shape=jax.ShapeDtypeStruct((S, DA), BF16),
        compiler_params=_params(("parallel",)),
    )(proj, proj, ng, w, bcol)


def sgu_bwd(proj, dcat, ng, w, bcol, DA, name):
    S = proj.shape[0]
    G = DA // GROUP
    tr = _tile(S, 2 * CHUNK)
    nsteps = S // tr

    def body(u_ref, v_ref, d_ref, ng_ref, w_ref, b_ref, duv_ref, dw_ref, db_ref, dng_ref, dbacc):
        i = pl.program_id(0)

        @pl.when(i == 0)
        def _():
            dw_ref[...] = jnp.zeros_like(dw_ref)
            dng_ref[...] = jnp.zeros_like(dng_ref)
            dbacc[...] = jnp.zeros_like(dbacc)

        mask = _tril_mask()
        for g in range(G):
            cols = slice(g * GROUP, (g + 1) * GROUP)
            wm = jnp.where(mask, w_ref[g], 0.0).astype(BF16)
            ngg = ng_ref[:, cols]
            for ci in range(tr // CHUNK):
                rows = slice(ci * CHUNK, (ci + 1) * CHUNK)
                u, v, d = u_ref[rows, cols], v_ref[rows, cols], d_ref[rows, cols]
                gv = _gelu(v)
                rs = _rstd(gv)
                vhat = gv * rs
                vn = (vhat * ngg).astype(BF16)
                mixed = jnp.dot(wm, vn, preferred_element_type=F32) + b_ref[g]
                dmixed = d * _gelu(u)
                dmb = dmixed.astype(BF16)
                duv_ref[rows, cols] = (d * mixed * _gelu_grad(u)).astype(duv_ref.dtype)
                dwg = lax.dot_general(dmb, vn, _DIMS["nt"], preferred_element_type=F32)
                dw_ref[g] += jnp.where(mask, dwg, 0.0)
                dbacc[g] += dmixed
                dvn = lax.dot_general(wm, dmb, _DIMS["tn"], preferred_element_type=F32)
                dgv, dngg = _rms_bwd(dvn, vhat, rs, ngg)
                dng_ref[:, cols] += dngg
                duv_ref[rows, DA + g * GROUP:DA + (g + 1) * GROUP] = (dgv * _gelu_grad(v)).astype(duv_ref.dtype)

        @pl.when(i == nsteps - 1)
        def _():
            for g in range(G):
                db_ref[g] = jnp.sum(dbacc[g], axis=-1, keepdims=True)

    return _pcall(
        body, name=name, grid=(nsteps,),
        in_specs=[pl.BlockSpec((tr, DA), lambda i: (i, 0)), pl.BlockSpec((tr, DA), lambda i: (i, 1)),
                  pl.BlockSpec((tr, DA), lambda i: (i, 0)),
                  pl.BlockSpec((1, DA), lambda i: (0, 0)), pl.BlockSpec((G, CHUNK, CHUNK), lambda i: (0, 0, 0)),
                  pl.BlockSpec((G, CHUNK, 1), lambda i: (0, 0, 0))],
        out_specs=[pl.BlockSpec((tr, 2 * DA), lambda i: (i, 0)), pl.BlockSpec((G, CHUNK, CHUNK), lambda i: (0, 0, 0)),
                   pl.BlockSpec((G, CHUNK, 1), lambda i: (0, 0, 0)), pl.BlockSpec((1, DA), lambda i: (0, 0))],
        out_shape=[jax.ShapeDtypeStruct((S, 2 * DA), BF16), jax.ShapeDtypeStruct((G, CHUNK, CHUNK), F32),
                   jax.ShapeDtypeStruct((G, CHUNK, 1), F32), jax.ShapeDtypeStruct((1, DA), F32)],
        scratch_shapes=[pltpu.VMEM((G, CHUNK, CHUNK), F32)],
        compiler_params=_params(("arbitrary",)),
    )(proj, proj, dcat, ng, w, bcol)


def _conv_tile(S):
    return _tile(S, 256, 8)


def conv_fwd(proj, wk, bias, DA, DB, name):
    S = proj.shape[0]
    nb = DB // LANES
    a0, g0 = 2 * DA // LANES, (2 * DA + DB) // LANES
    T = _conv_tile(S)
    off = CONV_PAD - (CONV_W - 1)

    def body(a_ref, g_ref, w_ref, b_ref, o_ref, ypad):
        ypad[0:CONV_PAD, :] = jnp.zeros((CONV_PAD, LANES), F32)

        def fill(t, cr):
            r = pl.multiple_of(t * T, T)
            ypad[pl.ds(CONV_PAD + r, T), :] = a_ref[pl.ds(r, T), :] * _sigmoid(g_ref[pl.ds(r, T), :])
            return cr

        lax.fori_loop(0, S // T, fill, 0)

        def step(t, cr):
            r = pl.multiple_of(t * T, T)
            acc = jnp.zeros((T, LANES), F32) + b_ref[...]
            for k in range(CONV_W):
                acc = acc + w_ref[k:k + 1, :] * ypad[pl.ds(r + (k + off), T), :]
            o_ref[pl.ds(r, T), :] = acc
            return cr

        lax.fori_loop(0, S // T, step, 0)

    return _pcall(
        body, name=name, grid=(nb,),
        in_specs=[pl.BlockSpec((S, LANES), lambda j: (0, a0 + j)), pl.BlockSpec((S, LANES), lambda j: (0, g0 + j)),
                  pl.BlockSpec((CONV_PAD, LANES), lambda j: (0, j)), pl.BlockSpec((1, LANES), lambda j: (0, j))],
        out_specs=pl.BlockSpec((S, LANES), lambda j: (0, j)),
        out_shape=jax.ShapeDtypeStruct((S, DB), F32),
        scratch_shapes=[pltpu.VMEM((S + CONV_PAD, LANES), F32)],
        compiler_params=_params(("parallel",)),
    )(proj, proj, wk, bias)


def conv_bwd(proj, dy1, wk, DA, DB, name):
    S = proj.shape[0]
    nb = DB // LANES
    a0, g0 = 2 * DA // LANES, (2 * DA + DB) // LANES
    T = _conv_tile(S)
    off = CONV_PAD - (CONV_W - 1)

    def body(a_ref, g_ref, d_ref, w_ref, da_ref, dg_ref, dw_ref, ypad, dpad, wacc):
        ypad[0:CONV_PAD, :] = jnp.zeros((CONV_PAD, LANES), F32)
        dpad[S:S + CONV_PAD, :] = jnp.zeros((CONV_PAD, LANES), F32)
        wacc[...] = jnp.zeros_like(wacc)

        def fill(t, cr):
            r = pl.multiple_of(t * T, T)
            ypad[pl.ds(CONV_PAD + r, T), :] = a_ref[pl.ds(r, T), :] * _sigmoid(g_ref[pl.ds(r, T), :])
            dpad[pl.ds(r, T), :] = d_ref[pl.ds(r, T), :]
            return cr

        lax.fori_loop(0, S // T, fill, 0)

        def step(t, cr):
            r = pl.multiple_of(t * T, T)
            dt = dpad[pl.ds(r, T), :]
            dy0 = jnp.zeros((T, LANES), F32)
            for k in range(CONV_W):
                prod = dt * ypad[pl.ds(r + (k + off), T), :]
                wacc[k] += jnp.sum(prod.reshape(T // 8, 8, LANES), axis=0)
                dy0 = dy0 + w_ref[k:k + 1, :] * dpad[pl.ds(r + (CONV_W - 1 - k), T), :]
            av, gv = a_ref[pl.ds(r, T), :], g_ref[pl.ds(r, T), :]
            sg = _sigmoid(gv)
            da_ref[pl.ds(r, T), :] = (dy0 * sg).astype(da_ref.dtype)
            dg_ref[pl.ds(r, T), :] = (dy0 * av * sg * (1.0 - sg)).astype(dg_ref.dtype)
            return cr

        lax.fori_loop(0, S // T, step, 0)
        for k in range(CONV_W):
            dw_ref[k:k + 1, :] = jnp.sum(wacc[k], axis=0, keepdims=True)
        dw_ref[CONV_W:CONV_PAD, :] = jnp.zeros((CONV_PAD - CONV_W, LANES), F32)

    return _pcall(
        body, name=name, grid=(nb,),
        in_specs=[pl.BlockSpec((S, LANES), lambda j: (0, a0 + j)), pl.BlockSpec((S, LANES), lambda j: (0, g0 + j)),
                  pl.BlockSpec((S, LANES), lambda j: (0, j)), pl.BlockSpec((CONV_PAD, LANES), lambda j: (0, j))],
        out_specs=[pl.BlockSpec((S, LANES), lambda j: (0, j)), pl.BlockSpec((S, LANES), lambda j: (0, j)),
                   pl.BlockSpec((CONV_PAD, LANES), lambda j: (0, j))],
        out_shape=[jax.ShapeDtypeStruct((S, DB), BF16), jax.ShapeDtypeStruct((S, DB), BF16),
                   jax.ShapeDtypeStruct((CONV_PAD, DB), F32)],
        scratch_shapes=[pltpu.VMEM((S + CONV_PAD, LANES), F32), pltpu.VMEM((S + CONV_PAD, LANES), F32),
                        pltpu.VMEM((CONV_PAD, 8, LANES), F32)],
        compiler_params=_params(("parallel",)),
    )(proj, proj, dy1, wk)


def _ln_stats(y):
    mu = jnp.mean(y, axis=-1, keepdims=True)
    yc = y - mu
    rs = lax.rsqrt(jnp.mean(yc * yc, axis=-1, keepdims=True) + EPS)
    return yc * rs, rs


def ln_silu(y1, lg, lb, name):
    S, DB = y1.shape

    def fn(t, c, o):
        yh, _ = _ln_stats(t[0][...])
        ln = yh * c[0][...] + c[1][...]
        o[0][...] = (ln * _sigmoid(ln)).astype(BF16)
        return []

    return rowwise(fn, [y1], [lg, lb], [((S, DB), BF16)], ts=_tile(S, ROW_TILE, 16), name=name)[0]


def ln_silu_bwd(dcat, y1, lg, lb, name):
    S, DB = y1.shape
    cb = (dcat.shape[1] - DB) // DB

    def fn(t, c, o):
        yh, rs = _ln_stats(t[1][...])
        gv = c[0][...]
        ln = yh * gv + c[1][...]
        sg = _sigmoid(ln)
        dln = t[0][...] * (sg * (1.0 + ln * (1.0 - sg)))
        dyh = dln * gv
        dy = rs * (dyh - jnp.mean(dyh, axis=-1, keepdims=True) - yh * jnp.mean(dyh * yh, axis=-1, keepdims=True))
        o[0][...] = dy
        return [_colsum(dln * yh), _colsum(dln), _colsum(dy)]

    return rowwise(fn, [(dcat, DB, cb), y1], [lg, lb], [((S, DB), F32)], [(1, DB)] * 3, ts=_tile(S, ROW_TILE, 8), name=name)


def mla_norms(proj, cos2, sinm, qg, kvg, kgr, R, name):
    S = proj.shape[0]

    def fn(t, c, o):
        cq = t[0][:, 0:R]
        ckv = t[0][:, R:2 * R]
        kr = t[0][:, 2 * R:2 * R + ROPE]
        o[0][...] = (cq * _rstd(cq) * c[0][...]).astype(BF16)
        o[1][...] = (ckv * _rstd(ckv) * c[1][...]).astype(BF16)
        o[2][...] = _rope(kr * _rstd(kr) * c[2][...], t[1][...], t[2][...])
        return []

    return rowwise(fn, [proj, cos2, sinm], [qg, kvg, kgr], [((S, R), BF16), ((S, R), BF16), ((S, ROPE), F32)],
                   ts=_tile(S, ROW_TILE, 16), name=name)


def mla_heads(q, kv, kr, cos2, sinm, qg, kg, H, name):
    S = q.shape[0]

    def fn(t, c, o):
        cs, sn = t[3][...], t[4][...]
        krv = t[2][...]
        qgn, qgr, kgn = c[0][:, 0:NOPE], c[0][:, NOPE:QK], c[1][:, 0:NOPE]
        for h in range(H):
            qn = t[0][:, QK * h:QK * h + NOPE]
            qr = t[0][:, QK * h + NOPE:QK * (h + 1)]
            o[0][h, :, 0:NOPE] = (qn * _rstd(qn) * qgn).astype(BF16)
            o[0][h, :, NOPE:QK] = _rope(qr * _rstd(qr) * qgr, cs, sn).astype(BF16)
            kn = t[1][:, (NOPE + VDIM) * h:(NOPE + VDIM) * h + NOPE]
            o[1][h, :, 0:NOPE] = (kn * _rstd(kn) * kgn).astype(BF16)
            o[1][h, :, NOPE:QK] = krv.astype(BF16)
            o[2][h] = t[1][:, (NOPE + VDIM) * h + NOPE:(NOPE + VDIM) * (h + 1)].astype(BF16)
        return []

    return rowwise(fn, [q, kv, kr, cos2, sinm], [qg, kg],
                   [((H, S, QK), BF16), ((H, S, QK), BF16), ((H, S, VDIM), BF16)], ts=_tile(S, HEAD_ROWS_FWD, 16),
                   name=name)


def mla_heads_bwd(dQ, dK, dV, q, kv, cos2, sinm, qg, kg, H, name):
    S = q.shape[0]
    KV = NOPE + VDIM

    def fn(t, c, o):
        cs, sn = t[5][...], t[6][...]
        qgn, qgr, kgn = c[0][:, 0:NOPE], c[0][:, NOPE:QK], c[1][:, 0:NOPE]
        a_qn = jnp.zeros((1, NOPE), F32)
        a_qr = jnp.zeros((1, ROPE), F32)
        a_kn = jnp.zeros((1, NOPE), F32)
        dkr = jnp.zeros((t[0].shape[1], ROPE), F32)
        for h in range(H):
            qn = t[3][:, QK * h:QK * h + NOPE]
            rs = _rstd(qn)
            dx, dg = _rms_bwd(t[0][h, :, 0:NOPE], qn * rs, rs, qgn)
            o[0][:, QK * h:QK * h + NOPE] = dx.astype(BF16)
            a_qn = a_qn + dg
            qr = t[3][:, QK * h + NOPE:QK * (h + 1)]
            rs = _rstd(qr)
            dx, dg = _rms_bwd(_unrope(t[0][h, :, NOPE:QK], cs, sn), qr * rs, rs, qgr)
            o[0][:, QK * h + NOPE:QK * (h + 1)] = dx.astype(BF16)
            a_qr = a_qr + dg
            kn = t[4][:, KV * h:KV * h + NOPE]
            rs = _rstd(kn)
            dx, dg = _rms_bwd(t[1][h, :, 0:NOPE], kn * rs, rs, kgn)
            o[1][:, KV * h:KV * h + NOPE] = dx.astype(BF16)
            a_kn = a_kn + dg
            o[1][:, KV * h + NOPE:KV * (h + 1)] = t[2][h].astype(BF16)
            dkr = dkr + t[1][h, :, NOPE:QK]
        o[2][...] = _unrope(dkr, cs, sn)
        return [a_qn, a_qr, a_kn]

    return rowwise(fn, [dQ, dK, dV, q, kv, cos2, sinm], [qg, kg],
                   [((S, H * QK), BF16), ((S, H * KV), BF16), ((S, ROPE), F32)],
                   [(1, NOPE), (1, ROPE), (1, NOPE)], ts=_tile(S, HEAD_ROWS, 16), name=name)


def mla_norms_bwd(dqn, dkvn, dkr, proj, qg, kvg, kgr, R, name):
    S = proj.shape[0]

    def fn(t, c, o):
        reds = []
        for idx, (lo, hi) in enumerate(((0, R), (R, 2 * R), (2 * R, 2 * R + ROPE))):
            xv = t[3][:, lo:hi]
            rs = _rstd(xv)
            dx, dg = _rms_bwd(t[idx][...], xv * rs, rs, c[idx][...])
            o[0][:, lo:hi] = dx.astype(BF16)
            reds.append(dg)
        return reds

    return rowwise(fn, [dqn, dkvn, dkr, proj], [qg, kvg, kgr], [((S, 2 * R + ROPE), BF16)],
                   [(1, R), (1, R), (1, ROPE)], ts=_tile(S, ROW_TILE, 16), name=name)


def _tri_rows(p, n):
    qi = 0
    for j in range(1, n):
        qi = qi + (p >= j * (j + 1) // 2).astype(jnp.int32)
    return qi, p - (qi * (qi + 1)) // 2


def _tri_cols(p, n):
    ki = 0
    for j in range(1, n):
        ki = ki + (p >= j * n - j * (j - 1) // 2).astype(jnp.int32)
    return ki, ki + p - (ki * n - (ki * (ki - 1)) // 2)


def attn_fwd(Q, K, V, name):
    H, S, _ = Q.shape
    t = _tile(S, ATTN_BLOCK)
    n = S // t
    scale = QK ** -0.5

    rs = _tile(t, ATTN_STRIP, 8)

    def body(q_ref, k_ref, v_ref, o_ref, lse_ref, m_s, l_s, acc, s_scr, p_scr):
        qi, ki = _tri_rows(pl.program_id(1), n)

        @pl.when(ki == 0)
        def _():
            m_s[...] = jnp.full_like(m_s, NEG)
            l_s[...] = jnp.zeros_like(l_s)
            acc[...] = jnp.zeros_like(acc)

        def block(diagonal):
            s_scr[...] = lax.dot_general(q_ref[...], k_ref[...], _DIMS["nt"], preferred_element_type=F32)

            def strip(i, cr):
                r = slice(i * rs, (i + 1) * rs)
                s = s_scr[r, :] * scale
                if diagonal:
                    row = i * rs + lax.broadcasted_iota(jnp.int32, (rs, t), 0)
                    s = jnp.where(lax.broadcasted_iota(jnp.int32, (rs, t), 1) <= row, s, NEG)
                m_old = m_s[r, :]
                m_new = jnp.maximum(m_old, jnp.max(s, axis=-1, keepdims=True))
                alpha = jnp.exp(m_old - m_new)
                p = jnp.exp(s - m_new)
                l_s[r, :] = alpha * l_s[r, :] + jnp.sum(p, axis=-1, keepdims=True)
                m_s[r, :] = m_new
                acc[r, :] = alpha * acc[r, :]
                p_scr[r, :] = p.astype(BF16)
                return cr

            for i in range(t // rs):
                strip(i, 0)
            acc[...] += jnp.dot(p_scr[...], v_ref[...], preferred_element_type=F32)

        @pl.when(ki < qi)
        def _():
            block(False)

        @pl.when(ki == qi)
        def _():
            block(True)

        @pl.when(ki == qi)
        def _():
            o_ref[...] = (acc[...] / l_s[...]).astype(o_ref.dtype)
            lse_ref[...] = m_s[...] + jnp.log(l_s[...])

    return _pcall(
        body, name=name, grid=(H, n * (n + 1) // 2),
        in_specs=[pl.BlockSpec((None, t, QK), lambda h, p: (h, _tri_rows(p, n)[0], 0)),
                  pl.BlockSpec((None, t, QK), lambda h, p: (h, _tri_rows(p, n)[1], 0)),
                  pl.BlockSpec((None, t, VDIM), lambda h, p: (h, _tri_rows(p, n)[1], 0))],
        out_specs=[pl.BlockSpec((t, VDIM), lambda h, p: (_tri_rows(p, n)[0], h)),
                   pl.BlockSpec((None, t, 1), lambda h, p: (h, _tri_rows(p, n)[0], 0))],
        out_shape=[jax.ShapeDtypeStruct((S, H * VDIM), BF16), jax.ShapeDtypeStruct((H, S, 1), F32)],
        scratch_shapes=[pltpu.VMEM((t, 1), F32), pltpu.VMEM((t, 1), F32), pltpu.VMEM((t, VDIM), F32),
                        pltpu.VMEM((t, t), F32), pltpu.VMEM((t, t), BF16)],
        compiler_params=_params(("parallel", "arbitrary")),
    )(Q, K, V)


def attn_bwd(Q, K, V, dO, O, lse, name):
    H, S, _ = Q.shape
    t = _tile(S, ATTN_BLOCK)
    n = S // t
    scale = QK ** -0.5

    rs = _tile(t, ATTN_STRIP, 8)

    def body(q_ref, k_ref, v_ref, do_ref, o_ref, lse_ref, dq_ref, dk_ref, dv_ref, s_scr, dp_scr, p_scr, ds_scr):
        ki, qi = _tri_cols(pl.program_id(1), n)

        @pl.when(pl.program_id(1) == 0)
        def _():
            dq_ref[...] = jnp.zeros_like(dq_ref)

        @pl.when(qi == ki)
        def _():
            dk_ref[...] = jnp.zeros_like(dk_ref)
            dv_ref[...] = jnp.zeros_like(dv_ref)

        def block(diagonal):
            s_scr[...] = lax.dot_general(q_ref[...], k_ref[...], _DIMS["nt"], preferred_element_type=F32)
            dp_scr[...] = lax.dot_general(do_ref[...], v_ref[...], _DIMS["nt"], preferred_element_type=F32)

            def strip(i, cr):
                r = slice(i * rs, (i + 1) * rs)
                s = s_scr[r, :] * scale
                if diagonal:
                    row = i * rs + lax.broadcasted_iota(jnp.int32, (rs, t), 0)
                    s = jnp.where(lax.broadcasted_iota(jnp.int32, (rs, t), 1) <= row, s, NEG)
                p = jnp.exp(s - lse_ref[r, :])
                delta = jnp.sum(do_ref[r, :].astype(F32) * o_ref[r, :].astype(F32), axis=-1, keepdims=True)
                p_scr[r, :] = p.astype(BF16)
                ds_scr[r, :] = (p * (dp_scr[r, :] - delta) * scale).astype(BF16)
                return cr

            for i in range(t // rs):
                strip(i, 0)
            ds = ds_scr[...]
            dv_ref[...] += lax.dot_general(p_scr[...], do_ref[...], _DIMS["tn"], preferred_element_type=F32)
            dk_ref[...] += lax.dot_general(ds, q_ref[...], _DIMS["tn"], preferred_element_type=F32)
            rq = pl.multiple_of(qi * t, t)
            dq_ref[pl.ds(rq, t), :] += jnp.dot(ds, k_ref[...], preferred_element_type=F32)

        @pl.when(qi > ki)
        def _():
            block(False)

        @pl.when(qi == ki)
        def _():
            block(True)

    qmap = lambda h, p: (h, _tri_cols(p, n)[1], 0)
    kmap = lambda h, p: (h, _tri_cols(p, n)[0], 0)
    return _pcall(
        body, name=name, grid=(H, n * (n + 1) // 2),
        in_specs=[pl.BlockSpec((None, t, QK), qmap),
                  pl.BlockSpec((None, t, QK), kmap),
                  pl.BlockSpec((None, t, VDIM), kmap),
                  pl.BlockSpec((t, VDIM), lambda h, p: (_tri_cols(p, n)[1], h)),
                  pl.BlockSpec((t, VDIM), lambda h, p: (_tri_cols(p, n)[1], h)),
                  pl.BlockSpec((None, t, 1), qmap)],
        out_specs=[pl.BlockSpec((None, S, QK), lambda h, p: (h, 0, 0)),
                   pl.BlockSpec((None, t, QK), kmap),
                   pl.BlockSpec((None, t, VDIM), kmap)],
        out_shape=[jax.ShapeDtypeStruct((H, S, QK), F32), jax.ShapeDtypeStruct((H, S, QK), F32),
                   jax.ShapeDtypeStruct((H, S, VDIM), F32)],
        scratch_shapes=[pltpu.VMEM((t, t), F32), pltpu.VMEM((t, t), F32), pltpu.VMEM((t, t), BF16),
                        pltpu.VMEM((t, t), BF16)],
        compiler_params=_params(("parallel", "arbitrary")),
    )(Q, K, V, dO, O, lse)


def adamw(w, m, v, g, *, row0, name, into=None):
    P, rows, C = g.shape
    tr = _tile(rows, max(16, 131072 // C), 16)
    off = row0 // tr
    assert row0 % tr == 0
    bc1 = 1.0 - ADAM_B1 ** ADAM_STEP
    bc2 = 1.0 - ADAM_B2 ** ADAM_STEP
    chained = into is not None and into is not True

    def body(w_ref, m_ref, v_ref, g_ref, *rest):
        go_ref, d_ref, mo_ref, vo_ref = rest[-4:]
        gs = g_ref[0].astype(F32)
        for p in range(1, P):
            gs = gs + g_ref[p].astype(F32)
        wv = w_ref[...]
        mn = ADAM_B1 * m_ref[...] + (1.0 - ADAM_B1) * gs
        vn = ADAM_B2 * v_ref[...] + (1.0 - ADAM_B2) * (gs * gs)
        go_ref[...] = gs
        mo_ref[...] = mn
        vo_ref[...] = vn
        d_ref[...] = -ADAM_LR * ((mn / bc1) / (jnp.sqrt(vn / bc2) + ADAM_EPS) + ADAM_WD * wv)

    wspec = pl.BlockSpec((tr, C), lambda i: (i + off, 0))
    ospec = wspec if into is not None else pl.BlockSpec((tr, C), lambda i: (i, 0))
    out_rows = w.shape[0] if into is not None else rows
    return _pcall(
        body, name=name, grid=(rows // tr,),
        in_specs=[wspec, wspec, wspec, pl.BlockSpec((P, tr, C), lambda i: (0, i, 0))] + ([_ANY] * 4 if chained else []),
        out_specs=[ospec] * 4, out_shape=[jax.ShapeDtypeStruct((out_rows, C), F32)] * 4,
        input_output_aliases={4 + q: q for q in range(4)} if chained else {},
        compiler_params=_params(("parallel",)),
    )(w, m, v, g, *(into if chained else ()))


def _coords():
    return lax.axis_index("x"), lax.axis_index("y"), lax.axis_index("c")


def _me():
    x, y, c = _coords()
    return 4 * x + 2 * y + c


_ANY = pl.BlockSpec(memory_space=pl.ANY)


def all_gather(items, name):
    n = len(items)
    blks = [a.shape if idx is None else a.shape[1:] for a, idx in items]

    def body(*refs):
        ins, outs = refs[:n], refs[n:2 * n]
        send, recv, lsem = refs[2 * n:]
        x, y, c = _coords()
        me, sib = (x, y, c), (x, y, 1 - c)
        chips = [(1 - x, y), (x, 1 - y), (1 - x, 1 - y)]

        def src(i):
            return ins[i] if items[i][1] is None else ins[i].at[items[i][1]]

        def slot(i, p):
            return outs[i].at[4 * p[0] + 2 * p[1] + p[2]]

        def cp(i, k, block, to, s=None):
            return pltpu.make_async_remote_copy(
                src_ref=slot(i, block) if s is None else s, dst_ref=slot(i, block),
                send_sem=send.at[7 * i + k], recv_sem=recv.at[7 * i + k], device_id=to, device_id_type=MESH)

        mine = [pltpu.make_async_copy(src(i), slot(i, me), lsem.at[i]) for i in range(n)]
        for m_ in mine:
            m_.start()
        first = []
        for i in range(n):
            first.append(cp(i, 0, me, sib, src(i)))
            first += [cp(i, 1 + j, me, (*chip, c), src(i)) for j, chip in enumerate(chips)]
        for f in first:
            f.start()
        passed = []
        for j, chip in enumerate(chips):
            for i in range(n):
                cp(i, 1 + j, (*chip, c), me).wait_recv()
                p_ = cp(i, 4 + j, (*chip, c), sib)
                p_.start()
                passed.append(p_)
        for i in range(n):
            cp(i, 0, sib, me).wait_recv()
            for j, chip in enumerate(chips):
                cp(i, 4 + j, (*chip, 1 - c), me).wait_recv()
        for f in first + passed:
            f.wait_send()
        for m_ in mine:
            m_.wait()

    res = _pcall(
        body, name=name, in_specs=[_ANY] * n, out_specs=[_ANY] * n,
        out_shape=[jax.ShapeDtypeStruct((NDEV,) + tuple(b), a.dtype) for b, (a, _) in zip(blks, items)],
        scratch_shapes=[pltpu.SemaphoreType.DMA((7 * n,)), pltpu.SemaphoreType.DMA((7 * n,)),
                        pltpu.SemaphoreType.DMA((n,))],
    )(*[a for a, _ in items])
    return list(res)


_HBM = pl.BlockSpec(memory_space=pltpu.HBM)
_SEM = pl.BlockSpec(memory_space=pltpu.SEMAPHORE)
_EFFECT = pltpu.SideEffectType.DATAFLOW_SIDE_EFFECTING


def _xchg_copy(src_ref, land_ref, send, recv, r, scatter, at_peer):
    x, y, c = _coords()
    px = jnp.bitwise_xor(x, (r >> 2) & 1)
    py = jnp.bitwise_xor(y, (r >> 1) & 1)
    pc = jnp.bitwise_xor(c, r & 1)
    p_i = 4 * px + 2 * py + pc
    me_i = 4 * x + 2 * y + c
    return pltpu.make_async_remote_copy(
        src_ref=src_ref.at[p_i] if scatter else src_ref, dst_ref=land_ref.at[p_i if at_peer else me_i],
        send_sem=send.at[r - 1], recv_sem=recv.at[r - 1], device_id=(px, py, pc), device_id_type=MESH)


def _phase(body, name, bufs, sems_in=(), new_sems=(), after=None, token=False):
    nb, ns, nn = len(bufs), len(sems_in), len(new_sems)

    def wrapped(*refs):
        outs = refs[nb + ns + (after is not None):]
        body(refs[:nb], refs[nb:nb + ns], outs[:nn])
        if token:
            outs[nn + nb][...] = jnp.zeros_like(outs[nn + nb])

    res = _pcall(
        wrapped, name=name,
        out_shape=tuple([pltpu.SemaphoreType.DMA((k,)) for k in new_sems] + [pltpu.HBM(a.shape, a.dtype) for a in bufs]
                        + ([jax.ShapeDtypeStruct((8, LANES), F32)] if token else [])),
        in_specs=[_HBM] * nb + [_SEM] * ns + ([] if after is None else [_ANY]),
        out_specs=tuple([_SEM] * nn + [_HBM] * nb + ([pl.BlockSpec(memory_space=pltpu.VMEM)] if token else [])),
        input_output_aliases={i: nn + i for i in range(nb)},
        compiler_params=pltpu.CompilerParams(has_side_effects=_EFFECT),
    )(*[pltpu.with_memory_space_constraint(a, pltpu.HBM) for a in bufs], *sems_in, *([] if after is None else [after]))
    return list(res[nn:nn + nb]), list(res[:nn]), (res[nn + nb][0:1, 0:1] if token else None)


def scatter_start(srcs, name, after=None):
    n = len(srcs)
    lands = [lax.empty(s.shape, s.dtype) for s in srcs]

    def body(b, taken, new):
        me_i = _me()
        for i in range(n):
            pltpu.make_async_copy(b[i].at[me_i], b[n + i].at[me_i], new[3 * i + 2].at[0]).start()
            for r in range(1, NDEV):
                _xchg_copy(b[i], b[n + i], new[3 * i], new[3 * i + 1], r, True, False).start()

    bufs, sems, tok = _phase(body, name, list(srcs) + lands, new_sems=[NDEV - 1, NDEV - 1, 1] * n, after=after, token=True)
    return (bufs, sems), tok


def scatter_wait(handle, after, name):
    bufs, sems = handle
    n = len(bufs) // 2

    def body(b, taken, new):
        me_i = _me()
        for i in range(n):
            pltpu.make_async_copy(b[i].at[me_i], b[n + i].at[me_i], taken[3 * i + 2].at[0]).wait()
            for r in range(1, NDEV):
                cp = _xchg_copy(b[i], b[n + i], taken[3 * i], taken[3 * i + 1], r, True, True)
                cp.wait_send()
                cp.wait_recv()

    return _phase(body, name, bufs, sems_in=sems, after=after)[0][n:]


def _gather_peers():
    x, y, c = _coords()
    return (x, y, c), (x, y, 1 - c), [(1 - x, y), (x, 1 - y), (1 - x, 1 - y)]


def _row(p):
    return 4 * p[0] + 2 * p[1] + p[2]


def _gcopy(src_ref, land_ref, send, recv, k, block, to):
    return pltpu.make_async_remote_copy(
        src_ref=land_ref.at[_row(block)] if src_ref is None else src_ref, dst_ref=land_ref.at[_row(block)],
        send_sem=send.at[k], recv_sem=recv.at[k], device_id=to, device_id_type=MESH)


def gather_start(srcs, name, after=None):
    n = len(srcs)
    lands = [lax.empty((NDEV,) + s.shape, s.dtype) for s in srcs]

    def body(b, taken, new):
        me, sib, chips = _gather_peers()
        for i in range(n):
            send, recv = new[3 * i], new[3 * i + 1]
            pltpu.make_async_copy(b[i], b[n + i].at[_row(me)], new[3 * i + 2].at[0]).start()
            for j, chip in enumerate(chips):
                _gcopy(b[i], b[n + i], send, recv, 1 + j, me, (*chip, me[2])).start()
            _gcopy(b[i], b[n + i], send, recv, 0, me, sib).start()

    bufs, sems, tok = _phase(body, name, list(srcs) + lands, new_sems=[4, 4, 1] * n, after=after, token=True)
    return (bufs, sems), tok


def gather_mid(handle, after, name):
    bufs, sems = handle
    n = len(bufs) // 2

    def body(b, taken, new):
        me, sib, chips = _gather_peers()
        for j, chip in enumerate(chips):
            for i in range(n):
                _gcopy(b[i], b[n + i], taken[3 * i], taken[3 * i + 1], 1 + j, (*chip, me[2]), me).wait_recv()
                _gcopy(None, b[n + i], new[2 * i], new[2 * i + 1], j, (*chip, me[2]), sib).start()
        for i in range(n):
            send, recv = taken[3 * i], taken[3 * i + 1]
            _gcopy(b[i], b[n + i], send, recv, 0, sib, me).wait_recv()
            for k in range(4):
                _gcopy(b[i], b[n + i], send, recv, k, me, sib).wait_send()
            pltpu.make_async_copy(b[i], b[n + i].at[_row(me)], taken[3 * i + 2].at[0]).wait()

    bufs, new, tok = _phase(body, name, bufs, sems_in=sems, new_sems=[3, 3] * n, after=after, token=True)
    return (bufs, new), tok


def gather_wait(handle, after, name):
    bufs, sems = handle
    n = len(bufs) // 2

    def body(b, taken, new):
        me, sib, chips = _gather_peers()
        for i in range(n):
            for j, chip in enumerate(chips):
                _gcopy(None, b[n + i], taken[2 * i], taken[2 * i + 1], j, (*chip, me[2]), sib).wait_send()
                _gcopy(None, b[n + i], taken[2 * i], taken[2 * i + 1], j, (*chip, 1 - me[2]), me).wait_recv()

    return _phase(body, name, bufs, sems_in=sems, after=after)[0][n:]


_PACK_ALIGN = 8 * LANES


def _pack(arrs, aligned=False):
    parts = []
    for a in arrs:
        f = a.reshape(-1).astype(F32)
        pad = (-f.shape[0]) % _PACK_ALIGN if aligned else 0
        parts.append(jnp.pad(f, (0, pad)) if pad else f)
    flat = jnp.concatenate(parts)
    pad = (-flat.shape[0]) % _PACK_ALIGN
    return (jnp.pad(flat, (0, pad)) if pad else flat).reshape(-1, LANES)


def _unpack(p, shapes, lead=(), aligned=False):
    nl = len(lead)
    flat = p.reshape(lead + (-1,))
    out, off = [], 0
    for shp in shapes:
        n = 1
        for d in shp:
            n *= d
        out.append(lax.slice_in_dim(flat, off, off + n, axis=nl).reshape(lead + tuple(shp)))
        off += n + ((-n) % _PACK_ALIGN if aligned else 0)
    return out


def _shard_cols(a, me, width, axis):
    return lax.dynamic_slice_in_dim(a, me * width, width, axis=axis)


def kernel(x, c, norm1_g, norm2_g, ada_w, ada_b, mlp_w1, mlp_w2, ab_w_in, sgu_norm_g, sgu_w, sgu_b, conv_w, conv_b, conv_ln_g, conv_ln_b, ab_w_out, mla_w_in, mla_q_norm_g, mla_kv_norm_g, mla_w_uq, mla_w_ukv, mla_q_head_g, mla_k_head_g, mla_w_out, loss_target, m_norm1_g, m_norm2_g, m_ada_w, m_ada_b, m_mlp_w1, m_mlp_w2, m_ab_w_in, m_sgu_norm_g, m_sgu_w, m_sgu_b, m_conv_w, m_conv_b, m_conv_ln_g, m_conv_ln_b, m_ab_w_out, m_mla_w_in, m_mla_q_norm_g, m_mla_kv_norm_g, m_mla_w_uq, m_mla_w_ukv, m_mla_q_head_g, m_mla_k_head_g, m_mla_w_out, v_norm1_g, v_norm2_g, v_ada_w, v_ada_b, v_mlp_w1, v_mlp_w2, v_ab_w_in, v_sgu_norm_g, v_sgu_w, v_sgu_b, v_conv_w, v_conv_b, v_conv_ln_g, v_conv_ln_b, v_ab_w_out, v_mla_w_in, v_mla_q_norm_g, v_mla_kv_norm_g, v_mla_w_uq, v_mla_w_ukv, v_mla_q_head_g, v_mla_k_head_g, v_mla_w_out):
    W = dict(norm1_g=norm1_g, norm2_g=norm2_g, ada_w=ada_w, ada_b=ada_b, mlp_w1=mlp_w1, mlp_w2=mlp_w2, ab_w_in=ab_w_in,
             sgu_norm_g=sgu_norm_g, sgu_w=sgu_w, sgu_b=sgu_b, conv_w=conv_w, conv_b=conv_b, conv_ln_g=conv_ln_g,
             conv_ln_b=conv_ln_b, ab_w_out=ab_w_out, mla_w_in=mla_w_in, mla_q_norm_g=mla_q_norm_g,
             mla_kv_norm_g=mla_kv_norm_g, mla_w_uq=mla_w_uq, mla_w_ukv=mla_w_ukv, mla_q_head_g=mla_q_head_g,
             mla_k_head_g=mla_k_head_g, mla_w_out=mla_w_out)
    M = dict(norm1_g=m_norm1_g, norm2_g=m_norm2_g, ada_w=m_ada_w, ada_b=m_ada_b, mlp_w1=m_mlp_w1, mlp_w2=m_mlp_w2,
             ab_w_in=m_ab_w_in, sgu_norm_g=m_sgu_norm_g, sgu_w=m_sgu_w, sgu_b=m_sgu_b, conv_w=m_conv_w, conv_b=m_conv_b,
             conv_ln_g=m_conv_ln_g, conv_ln_b=m_conv_ln_b, ab_w_out=m_ab_w_out, mla_w_in=m_mla_w_in,
             mla_q_norm_g=m_mla_q_norm_g, mla_kv_norm_g=m_mla_kv_norm_g, mla_w_uq=m_mla_w_uq, mla_w_ukv=m_mla_w_ukv,
             mla_q_head_g=m_mla_q_head_g, mla_k_head_g=m_mla_k_head_g, mla_w_out=m_mla_w_out)
    V = dict(norm1_g=v_norm1_g, norm2_g=v_norm2_g, ada_w=v_ada_w, ada_b=v_ada_b, mlp_w1=v_mlp_w1, mlp_w2=v_mlp_w2,
             ab_w_in=v_ab_w_in, sgu_norm_g=v_sgu_norm_g, sgu_w=v_sgu_w, sgu_b=v_sgu_b, conv_w=v_conv_w, conv_b=v_conv_b,
             conv_ln_g=v_conv_ln_g, conv_ln_b=v_conv_ln_b, ab_w_out=v_ab_w_out, mla_w_in=v_mla_w_in,
             mla_q_norm_g=v_mla_q_norm_g, mla_kv_norm_g=v_mla_kv_norm_g, mla_w_uq=v_mla_w_uq, mla_w_ukv=v_mla_w_ukv,
             mla_q_head_g=v_mla_q_head_g, mla_k_head_g=v_mla_k_head_g, mla_w_out=v_mla_w_out)
    ORDER = list(W)

    S, D = x.shape[1], x.shape[2]
    L, NE, NO = norm1_g.shape[0], ab_w_in.shape[0], mla_w_in.shape[0]
    DA = D // 2
    DB = D - DA
    G = DA // GROUP
    R = NDEV * mla_q_norm_g.shape[1]
    H = NDEV * mla_w_uq.shape[2] // QK
    AW = ada_w.shape[2]
    CB = conv_w.shape[2]
    me = _me()
    xs, tgt = x[0], loss_target[0]

    BIG_EVEN = ("mlp_w1", "mlp_w2", "ab_w_in", "ab_w_out")
    BIG_ODD = ("mlp_w1", "mlp_w2", "mla_w_in", "mla_w_uq", "mla_w_ukv", "mla_w_out")
    BIG = ("mlp_w1", "mlp_w2", "ab_w_in", "ab_w_out", "mla_w_in", "mla_w_uq", "mla_w_ukv", "mla_w_out")
    COL_SHARDED = ("mlp_w1", "ab_w_in", "mla_w_uq", "mla_w_ukv")
    MLP_W = ("mlp_w1", "mlp_w2")
    mixer_w = lambda l: ("ab_w_in", "ab_w_out") if l % 2 == 0 else ("mla_w_in", "mla_w_uq", "mla_w_ukv", "mla_w_out")
    widx = lambda k, l: l if k in MLP_W else l // 2

    small_in = [c, mla_q_norm_g, mla_kv_norm_g, conv_w]
    sg = all_gather([(_pack(small_in, True), None)], "gather_small")[0]
    c_all, qng_all, kvng_all, cw_all = _unpack(sg, [a.shape for a in small_in], (NDEV,), True)
    c_all = c_all.reshape(NDEV, D)
    qng_full = jnp.transpose(qng_all, (1, 0, 2)).reshape(NO, 1, R)
    kvng_full = jnp.transpose(kvng_all, (1, 0, 2)).reshape(NO, 1, R)
    cw_full = jnp.transpose(cw_all, (1, 2, 0, 3)).reshape(NE, CONV_W, DB)
    cw_pad = jnp.pad(cw_full, ((0, 0), (0, CONV_PAD - CONV_W), (0, 0)))

    def silu_fn(t, c_, o):
        v_ = t[0][...]
        o[0][...] = v_ * _sigmoid(v_)
        return []

    c_act = rowwise(silu_fn, [c_all], outs=[((NDEV, D), F32)], ts=NDEV, name="silu_c")[0]
    bias_cols = _shard_cols(ada_b, me, AW, 1).reshape(1, L * AW)
    mod_cols = mm(c_act, ada_w, "nn", name="ada_fwd", b_blocked=True, rowvecs=[bias_cols],
                  epi=lambda acc, b_: (acc + b_,), tm=NDEV, tn=768)
    mod_all = all_gather([(mod_cols, None)], "gather_mod")[0]
    mod = lax.dynamic_index_in_dim(mod_all, me, axis=1, keepdims=False)
    mod = jnp.transpose(mod.reshape(NDEV, L, AW), (1, 0, 2)).reshape(L, 6, 1, D)

    wnames = lambda l, what: mixer_w(l) if what == "mix" else MLP_W
    g_first, g_second = {}, {}
    tok_sum = jnp.zeros((1, 1), F32)
    for l in range(L):
        for what in ("mix", "mlp"):
            srcs = [cast_bf16(W[k], widx(k, l), "cast_%s_l%d" % (k, l)) for k in wnames(l, what)]
            g_first[l, what], tok = gather_start(srcs, "gather_start_%s_l%d" % (what, l), after=mod_all)
            tok_sum = tok_sum + tok
    mod = mod + tok_sum

    def pass_on(l, what, after):
        g_second[l, what], tok = gather_mid(g_first[l, what], after, "gather_mid_%s_l%d" % (what, l))
        return tok

    def wait_weights(l, what, after):
        lands = gather_wait(g_second[l, what], after, "gather_wait_%s_l%d" % (what, l))
        return {k: (ld if k in COL_SHARDED else ld.reshape(NDEV * ld.shape[1], ld.shape[2]))
                for k, ld in zip(wnames(l, what), lands)}

    mod = mod + pass_on(0, "mix", mod)

    pos = jnp.arange(S, dtype=F32)
    inv = ROPE_THETA ** (-jnp.arange(0, ROPE, 2, dtype=F32) / ROPE)
    ang = pos[:, None] * inv[None, :]
    cos2 = jnp.concatenate([jnp.cos(ang), jnp.cos(ang)], axis=1)
    sinm = jnp.concatenate([-jnp.sin(ang), jnp.sin(ang)], axis=1)

    residual = lambda acc, xr, gt: (acc, xr + gt * acc)

    saved = []
    xc = xs
    for l in range(L):
        sh1, sc1, g1, sh2, sc2, g2 = [mod[l, k] for k in range(6)]
        tag = "_l%d" % l
        sv = dict(x0=xc)
        h, sv["rstd1"] = prenorm(xc, norm1_g[l][None], sc1, sh1, "prenorm1" + tag)
        sv["h"] = h
        wl = wait_weights(l, "mix", h)
        if l % 2 == 0:
            e = l // 2
            ng = sgu_norm_g[e].reshape(1, DA)
            bcol = sgu_b[e][:, :, None]
            proj = mm(h, wl["ab_w_in"], "nn", name="ab_in" + tag, b_blocked=True)
            out_a = sgu_fwd(proj, ng, sgu_w[e], bcol, DA, "sgu_fwd" + tag)
            y1 = conv_fwd(proj, cw_pad[e], conv_b[e][None], DA, DB, "conv_fwd" + tag)
            out_b = ln_silu(y1, conv_ln_g[e][None], conv_ln_b[e][None], "ln_silu" + tag)
            cat = jnp.concatenate([out_a, out_b], axis=1)
            sv.update(proj=proj, y1=y1, cat=cat)
            mixb, x1 = mm(cat, wl["ab_w_out"], "nn", name="ab_out" + tag, out_dtypes=(BF16, F32), epi=residual,
                          extras=[xc], rowvecs=[g1])
        else:
            o_ = l // 2
            proj = mm(h, wl["mla_w_in"], "nn", name="mla_in" + tag)
            kgr = mla_k_head_g[o_][None, NOPE:QK]
            qn, kvn, kr = mla_norms(proj, cos2, sinm, qng_full[o_], kvng_full[o_], kgr, R, "mla_norms" + tag)
            q = mm(qn, wl["mla_w_uq"], "nn", name="mla_uq" + tag, b_blocked=True)
            kv = mm(kvn, wl["mla_w_ukv"], "nn", name="mla_ukv" + tag, b_blocked=True)
            Qh, Kh, Vh = mla_heads(q, kv, kr, cos2, sinm, mla_q_head_g[o_][None], mla_k_head_g[o_][None], H,
                                   "mla_heads" + tag)
            att, lse = attn_fwd(Qh, Kh, Vh, "attn_fwd" + tag)
            sv.update(proj=proj, qn=qn, kvn=kvn, q=q, kv=kv, Qh=Qh, Kh=Kh, Vh=Vh, att=att, lse=lse)
            mixb, x1 = mm(att, wl["mla_w_out"], "nn", name="mla_out" + tag, out_dtypes=(BF16, F32), epi=residual,
                          extras=[xc], rowvecs=[g1])
        sv.update(mixb=mixb, x1=x1)
        h2, sv["rstd2"] = prenorm(x1, norm2_g[l][None], sc2 + pass_on(l, "mlp", x1), sh2, "prenorm2" + tag)
        wl.update(wait_weights(l, "mlp", h2))
        sv["w"] = wl
        z, act = mm(h2, wl["mlp_w1"], "nn", name="mlp_up" + tag, b_blocked=True, out_dtypes=(BF16, BF16),
                    epi=lambda acc: (acc, jnp.square(jnp.maximum(acc, 0.0))), tm=MLP_ROWS)
        g2t = g2 + pass_on(l + 1, "mix", z) if l + 1 < L else g2
        yb, xc = mm(act, wl["mlp_w2"], "nn", name="mlp_down" + tag, out_dtypes=(BF16, F32), epi=residual,
                    extras=[x1], rowvecs=[g2t])
        sv.update(h2=h2, z=z, act=act, yb=yb)
        saved.append(sv)

    dx, dy, loss_cols, dgate2 = loss_grad(xc, tgt, saved[L - 1]["yb"], mod[L - 1, 5], "loss")
    loss = lax.psum(0.5 / D * jnp.sum(loss_cols), ("x", "y", "c"))

    big_out = {}
    sm = {k: [None] * W[k].shape[0] for k in ("norm1_g", "norm2_g", "sgu_norm_g", "sgu_w", "sgu_b", "conv_b", "conv_ln_g",
                                               "conv_ln_b", "mla_q_head_g", "mla_k_head_g", "mla_q_norm_g",
                                               "mla_kv_norm_g", "conv_w")}
    dmod = [None] * L
    flat2 = {k: W[k].reshape(-1, W[k].shape[2]) for k in BIG}
    flat2m = {k: M[k].reshape(-1, W[k].shape[2]) for k in BIG}
    flat2v = {k: V[k].reshape(-1, W[k].shape[2]) for k in BIG}

    def send_grads(gr, what, tag, after=None):
        names = list(gr)
        blocks = [gr[k] if k in COL_SHARDED else gr[k].reshape(NDEV, gr[k].shape[0] // NDEV, gr[k].shape[1])
                  for k in names]
        handle, tok = scatter_start(blocks, "scatter_start_%s%s" % (what, tag), after=after)
        return (names, handle, what, tag), tok

    rep = ("norm1_g", "norm2_g", "sgu_norm_g", "sgu_w", "sgu_b", "conv_b", "conv_ln_g", "conv_ln_b", "mla_q_head_g",
           "mla_k_head_g")
    part_full = {"mla_q_norm_g": (NO, R), "mla_kv_norm_g": (NO, R), "conv_w": (NE, CONV_W, DB)}
    small = ["ada_b"] + list(rep) + list(part_full)
    shapes = [(L, 6 * D)] + [W[k].shape for k in rep] + list(part_full.values())

    def small_gather():
        parts = [jnp.stack(dmod).reshape(L, 6 * D)] + [jnp.stack(sm[k]).reshape(s_) for k, s_ in zip(small[1:], shapes[1:])]
        return all_gather([(_pack(parts), None)], "gather_smallgrads")[0]

    def scatter_finish(pending, after, l):
        names, handle, what, tag = pending
        landed = scatter_wait(handle, after, "scatter_wait_%s%s" % (what, tag))
        for k, land in zip(names, landed):
            li = widx(k, l)
            big_out[k] = adamw(flat2[k], flat2m[k], flat2v[k], land, row0=li * W[k].shape[1],
                               name="adamw_%s%s" % (k, tag), into=big_out.get(k, True))

    pend_mix, tok_mix = None, None
    for l in reversed(range(L)):
        sh1, sc1, g1, sh2, sc2, g2 = [mod[l, k] for k in range(6)]
        sv = saved[l]
        wl = sv["w"]
        tag = "_l%d" % l
        gr = {}
        dw2 = mm(sv["act"], dy, "tn", name="mlp_down_dw" + tag, out_dtypes=(BF16,), tm=MLP_ROWS)
        pend_w2, tok_w2 = send_grads({"mlp_w2": dw2}, "w2", tag)
        dz = mm(dy, wl["mlp_w2"], "nt", name="mlp_down_dx" + tag, out_dtypes=(BF16,), extras=[sv["z"]],
                rowvecs=[jnp.zeros((1, sv["z"].shape[1]), F32) + tok_w2],
                epi=lambda acc, z_, t_: (acc * (2.0 * jnp.maximum(z_.astype(F32), 0.0)) + t_,), tm=MLP_ROWS)
        dh2 = mm(dz, wl["mlp_w1"], "nt", name="mlp_up_dx" + tag, b_blocked=True)
        dw1 = mm(sv["h2"], dz, "tn", name="mlp_up_dw" + tag, out_dtypes=(BF16,), out_blocked=NDEV, tm=MLP_ROWS)
        pend_mlp, tok_mlp = send_grads({"mlp_w1": dw1}, "w1", tag)
        dx1, dsc2, dsh2, sm["norm2_g"][l], dmix, dgate1 = norm_bwd(
            dh2, sv["x1"], sv["rstd2"], dx, norm2_g[l][None], sc2 + tok_mlp, "norm2_bwd" + tag, gated=(sv["mixb"], g1))
        if pend_mix is not None:
            scatter_finish(pend_mix, dx1, l + 1)
        if l % 2 == 0:
            e = l // 2
            ng = sgu_norm_g[e].reshape(1, DA)
            bcol = sgu_b[e][:, :, None]
            dcat = mm(dmix, wl["ab_w_out"], "nt", name="ab_out_dx" + tag)
            gr["ab_w_out"] = mm(sv["cat"], dmix, "tn", name="ab_out_dw" + tag, out_dtypes=(BF16,))
            dy1, sm["conv_ln_g"][e], sm["conv_ln_b"][e], sm["conv_b"][e] = ln_silu_bwd(
                dcat, sv["y1"], conv_ln_g[e][None], conv_ln_b[e][None], "ln_silu_bwd" + tag)
            da, dg_, dwc = conv_bwd(sv["proj"], dy1, cw_pad[e], DA, DB, "conv_bwd" + tag)
            sm["conv_w"][e] = dwc[:CONV_W]
            duv, dsw, dsb, dsng = sgu_bwd(sv["proj"], dcat, ng, sgu_w[e], bcol, DA, "sgu_bwd" + tag)
            sm["sgu_w"][e], sm["sgu_b"][e], sm["sgu_norm_g"][e] = dsw, dsb, dsng
            dproj = jnp.concatenate([duv, da, dg_], axis=1)
            dh = mm(dproj, wl["ab_w_in"], "nt", name="ab_in_dx" + tag, b_blocked=True)
            gr["ab_w_in"] = mm(sv["h"], dproj, "tn", name="ab_in_dw" + tag, out_dtypes=(BF16,), out_blocked=NDEV)
        else:
            o_ = l // 2
            kgr = mla_k_head_g[o_][None, NOPE:QK]
            dO = mm(dmix, wl["mla_w_out"], "nt", name="mla_out_dx" + tag, out_dtypes=(BF16,))
            gr["mla_w_out"] = mm(sv["att"], dmix, "tn", name="mla_out_dw" + tag, out_dtypes=(BF16,))
            dQ, dK, dV = attn_bwd(sv["Qh"], sv["Kh"], sv["Vh"], dO, sv["att"], sv["lse"], "attn_bwd" + tag)
            dq_pre, dkv_pre, dkr, dqgn, dqgr, dkgn = mla_heads_bwd(
                dQ, dK, dV, sv["q"], sv["kv"], cos2, sinm, mla_q_head_g[o_][None], mla_k_head_g[o_][None], H,
                "mla_heads_bwd" + tag)
            dqn = mm(dq_pre, wl["mla_w_uq"], "nt", name="mla_uq_dx" + tag, b_blocked=True)
            gr["mla_w_uq"] = mm(sv["qn"], dq_pre, "tn", name="mla_uq_dw" + tag, out_dtypes=(BF16,), out_blocked=NDEV)
            dkvn = mm(dkv_pre, wl["mla_w_ukv"], "nt", name="mla_ukv_dx" + tag, b_blocked=True)
            gr["mla_w_ukv"] = mm(sv["kvn"], dkv_pre, "tn", name="mla_ukv_dw" + tag, out_dtypes=(BF16,),
                                 out_blocked=NDEV)
            dproj, sm["mla_q_norm_g"][o_], sm["mla_kv_norm_g"][o_], dkgr = mla_norms_bwd(
                dqn, dkvn, dkr, sv["proj"], qng_full[o_], kvng_full[o_], kgr, R, "mla_norms_bwd" + tag)
            sm["mla_q_head_g"][o_] = jnp.concatenate([dqgn, dqgr], axis=1)
            sm["mla_k_head_g"][o_] = jnp.concatenate([dkgn, dkgr], axis=1)
            dh = mm(dproj, wl["mla_w_in"], "nt", name="mla_in_dx" + tag)
            gr["mla_w_in"] = mm(sv["h"], dproj, "tn", name="mla_in_dw" + tag, out_dtypes=(BF16,))
        if l > 0:
            pend_mix, tok_mix = send_grads(gr, "mix", tag)
            dx, dsc1, dsh1, sm["norm1_g"][l], dy, dgate2_below = norm_bwd(
                dh, sv["x0"], sv["rstd1"], dx1, norm1_g[l][None], sc1 + tok_mix, "norm1_bwd" + tag,
                gated=(saved[l - 1]["yb"], mod[l - 1, 5]))
        else:
            dx, dsc1, dsh1, sm["norm1_g"][l] = norm_bwd(dh, sv["x0"], sv["rstd1"], dx1, norm1_g[l][None], sc1,
                                                        "norm1_bwd" + tag)
        dmod[l] = jnp.concatenate([dsh1, dsc1, dgate1, dsh2, dsc2, dgate2], axis=1)
        if l > 0:
            dgate2 = dgate2_below
        else:
            gp = small_gather()
            pend_mix, tok_mix = send_grads(gr, "mix", tag, after=gp)
        scatter_finish(pend_w2, dx, l)
        scatter_finish(pend_mlp, dx, l)

    rows_p = gp.shape[1]
    per = 6 * D // LANES
    dm = lax.slice_in_dim(gp, 0, L * per, axis=1).reshape(NDEV, L, per, LANES)
    dmod_cols = lax.dynamic_slice_in_dim(dm, me * (AW // LANES), AW // LANES, axis=2).reshape(NDEV, L * AW) + tok_mix

    def sum_fn(t, c_, o):
        acc = t[0][0]
        for s_ in range(1, NDEV):
            acc = acc + t[0][s_]
        o[0][...] = acc
        return []

    gsummed = rowwise(sum_fn, [gp], outs=[((rows_p, LANES), F32)], ts=_tile(rows_p, 256, 8), name="sum_smallgrads")[0]
    gsum = dict(zip(small, _unpack(gsummed, shapes)))
    gsum["mla_q_norm_g"] = _shard_cols(gsum["mla_q_norm_g"], me, R // NDEV, 1)
    gsum["mla_kv_norm_g"] = _shard_cols(gsum["mla_kv_norm_g"], me, R // NDEV, 1)
    gsum["conv_w"] = _shard_cols(gsum["conv_w"], me, CB, 2)
    sm_shapes = [W[k].shape for k in small]
    sres = adamw(_pack([W[k] for k in small]), _pack([M[k] for k in small]), _pack([V[k] for k in small]),
                 (_pack([gsum[k] for k in small]) + tok_mix)[None], row0=0, name="adamw_small")
    small_out = {k: vals for k, vals in zip(small, zip(*[_unpack(r_, sm_shapes) for r_ in sres]))}

    g_ada = mm(c_act, dmod_cols, "tn", name="ada_dw", out_blocked=L, tm=1024, tn=768, tk=NDEV, cast=None,
               precision=lax.Precision.HIGHEST)
    ada_out = adamw(ada_w.reshape(L * D, AW), m_ada_w.reshape(L * D, AW), v_ada_w.reshape(L * D, AW),
                    g_ada.reshape(1, L * D, AW), row0=0, name="adamw_ada_w")
    scatter_finish(pend_mix, ada_out[1], 0)
    ada_out = [a.reshape(L, D, AW) for a in ada_out]

    def result(k, which):
        if k == "ada_w":
            return ada_out[which]
        if k in BIG:
            return big_out[k][which].reshape(W[k].shape)
        return small_out[k][which]

    outs = [loss, dx[None]]
    for which in range(4):
        outs += [result(k, which) for k in ORDER]
    return tuple(outs)
```

```python
import functools

import jax
import jax.numpy as jnp
from jax import lax
from jax.experimental import pallas as pl
from jax.experimental.pallas import tpu as pltpu

F32 = jnp.float32
BF16 = jnp.bfloat16
EPS = 1e-6
NDEV = 8
LANES = 128
CHUNK = 128
GROUP = 128
CONV_W = 31
CONV_PAD = 32
NOPE, ROPE, VDIM = 128, 64, 128
QK = NOPE + ROPE
ROPE_THETA = 10000.0
VMEM_LIMIT = 56 * 1024 * 1024
ADAM_LR, ADAM_B1, ADAM_B2, ADAM_EPS, ADAM_WD, ADAM_STEP = 0.001, 0.9, 0.999, 1e-08, 0.01, 10
MESH = pl.DeviceIdType.MESH
NEG = -1e30
MLP_ROWS = 2048
ROW_TILE = 256
HEAD_ROWS = 256
HEAD_ROWS_FWD = 512
ATTN_BLOCK = 1024
ATTN_STRIP = 64


def _pcall(body, **kw):
    return pl.pallas_call(body, **kw)


def _params(sem=None):
    return pltpu.CompilerParams(dimension_semantics=sem, vmem_limit_bytes=VMEM_LIMIT)


def _tile(dim, target, align=LANES):
    if dim <= target:
        return dim
    t = (target // align) * align
    while t >= align:
        if dim % t == 0:
            return t
        t -= align
    return dim


def _rstd(x):
    return lax.rsqrt(jnp.mean(x * x, axis=-1, keepdims=True) + EPS)


def _sigmoid(x):
    return 1.0 / (1.0 + jnp.exp(-x))


_GC = 0.7978845608028654


def _gelu(x):
    return 0.5 * x * (1.0 + jnp.tanh(_GC * (x + 0.044715 * x * x * x)))


def _gelu_grad(x):
    t = jnp.tanh(_GC * (x + 0.044715 * x * x * x))
    return 0.5 * (1.0 + t) + 0.5 * x * (1.0 - t * t) * _GC * (1.0 + 3 * 0.044715 * x * x)


def _colsum(x):
    return jnp.sum(x, axis=0, keepdims=True)


def _rms_bwd(dy, xhat, rstd, g):
    dxh = dy * g
    dx = rstd * (dxh - xhat * jnp.mean(dxh * xhat, axis=-1, keepdims=True))
    return dx, _colsum(dy * xhat)


def _swap_halves(x):
    h = x.shape[-1] // 2
    return jnp.concatenate([x[:, h:], x[:, :h]], axis=1)


def _rope(x, cos2, sinm):
    return x * cos2 + _swap_halves(x) * sinm


def _unrope(dy, cos2, sinm):
    return dy * cos2 + _swap_halves(dy * sinm)


_DIMS = {"nn": (((1,), (0,)), ((), ())), "nt": (((1,), (1,)), ((), ())), "tn": (((0,), (0,)), ((), ()))}


def mm(a, b, mode, *, name, out_dtypes=(F32,), epi=None, extras=(), rowvecs=(), b_blocked=False, out_blocked=0,
       tm=1024, tn=1024, tk=2048, precision=None, cast=BF16):
    if mode == "tn":
        K, M = a.shape
    else:
        M, K = a.shape
    if b_blocked:
        J, Rb, Cb = b.shape
        N = Rb if mode == "nt" else J * Cb
    else:
        N = b.shape[0] if mode == "nt" else b.shape[1]
    tm = _tile(M, tm)
    nbo = 1
    nb = 1
    if mode == "nn" and b_blocked:
        if Cb >= tn:
            tn = _tile(Cb, tn)
        else:
            nb = max(d for d in range(1, J + 1) if J % d == 0 and d * Cb <= tn)
            tn = nb * Cb
    elif out_blocked:
        Nb = N // out_blocked
        if Nb >= tn:
            tn = _tile(Nb, tn)
        else:
            nbo = max(d for d in range(1, out_blocked + 1) if out_blocked % d == 0 and d * Nb <= tn)
            tn = nbo * Nb
    else:
        tn = _tile(N, tn)
    kb = 1
    if mode == "nt" and b_blocked:
        if Cb >= tk:
            tk = _tile(Cb, tk)
        else:
            kb = max(d for d in range(1, J + 1) if J % d == 0 and d * Cb <= tk)
            tk = kb * Cb
    else:
        tk = _tile(K, tk)
    nk = K // tk
    grid = (M // tm, N // tn, nk)

    if mode == "tn":
        a_spec = pl.BlockSpec((tk, tm), lambda i, j, k: (k, i))
    else:
        a_spec = pl.BlockSpec((tm, tk), lambda i, j, k: (i, k))
    if mode == "nn":
        if b_blocked:
            if nb > 1:
                b_spec = pl.BlockSpec((nb, tk, Cb), lambda i, j, k: (j, k, 0))
            else:
                nper = Cb // tn
                b_spec = pl.BlockSpec((None, tk, tn), lambda i, j, k: (j // nper, k, j % nper))
        else:
            b_spec = pl.BlockSpec((tk, tn), lambda i, j, k: (k, j))
    elif mode == "nt":
        if b_blocked:
            if kb > 1:
                b_spec = pl.BlockSpec((kb, tn, Cb), lambda i, j, k: (k, j, 0))
            else:
                kper = Cb // tk
                b_spec = pl.BlockSpec((None, tn, tk), lambda i, j, k: (k // kper, j, k % kper))
        else:
            b_spec = pl.BlockSpec((tn, tk), lambda i, j, k: (j, k))
    else:
        b_spec = pl.BlockSpec((tk, tn), lambda i, j, k: (k, j))
    if out_blocked:
        if nbo > 1:
            o_spec = pl.BlockSpec((nbo, tm, N // out_blocked), lambda i, j, k: (j, i, 0))
        else:
            oper = (N // out_blocked) // tn
            o_spec = pl.BlockSpec((None, tm, tn), lambda i, j, k: (j // oper, i, j % oper))
        o_shape = (out_blocked, M, N // out_blocked)
    else:
        o_spec = pl.BlockSpec((tm, tn), lambda i, j, k: (i, j))
        o_shape = (M, N)
    e_spec = pl.BlockSpec((tm, tn), lambda i, j, k: (i, j))
    r_spec = pl.BlockSpec((1, tn), lambda i, j, k: (0, j))
    ne, nr, no = len(extras), len(rowvecs), len(out_dtypes)
    dims = _DIMS[mode]

    def body(a_ref, b_ref, *rest):
        ex = rest[:ne]
        rv = rest[ne:ne + nr]
        outs = rest[ne + nr:ne + nr + no]

        def product():
            if nb > 1:
                av = a_ref[...] if cast is None else a_ref[...].astype(cast)
                return jnp.concatenate(
                    [lax.dot_general(av, b_ref[q] if cast is None else b_ref[q].astype(cast), dims,
                                     preferred_element_type=F32, precision=precision) for q in range(nb)], axis=1)
            if kb > 1:
                r = None
                for q in range(kb):
                    av, bv = a_ref[:, q * Cb:(q + 1) * Cb], b_ref[q]
                    if cast is not None:
                        av, bv = av.astype(cast), bv.astype(cast)
                    d = lax.dot_general(av, bv, dims, preferred_element_type=F32, precision=precision)
                    r = d if r is None else r + d
                return r
            av, bv = a_ref[...], b_ref[...]
            if cast is not None:
                av, bv = av.astype(cast), bv.astype(cast)
            return lax.dot_general(av, bv, dims, preferred_element_type=F32, precision=precision)

        def finish(r):
            vals = (r,) if epi is None else epi(r, *[e[...] for e in ex], *[v[...] for v in rv])
            for o, val in zip(outs, vals):
                if nbo > 1:
                    w_ = N // out_blocked
                    for q in range(nbo):
                        o[q] = val[:, q * w_:(q + 1) * w_].astype(o.dtype)
                else:
                    o[...] = val.astype(o.dtype)

        if nk == 1:
            finish(product())
            return
        acc = rest[ne + nr + no]
        k = pl.program_id(2)

        @pl.when(k == 0)
        def _():
            acc[...] = product()

        @pl.when((k > 0) & (k < nk - 1))
        def _():
            acc[...] += product()

        @pl.when(k == nk - 1)
        def _():
            finish(acc[...] + product())

    res = _pcall(
        body, name=name, grid=grid,
        in_specs=[a_spec, b_spec] + [e_spec] * ne + [r_spec] * nr,
        out_specs=[o_spec] * no,
        out_shape=[jax.ShapeDtypeStruct(o_shape, dt) for dt in out_dtypes],
        scratch_shapes=[] if nk == 1 else [pltpu.VMEM((tm, tn), F32)],
        compiler_params=_params(("parallel", "parallel", "arbitrary")),
    )(a, b, *extras, *rowvecs)
    return res[0] if no == 1 else res


def rowwise(fn, tiled, consts=(), outs=(), reds=(), *, ts, name):
    specs = []
    arrs = []
    rows = None
    for t in tiled:
        a, w, cb = t if isinstance(t, tuple) else (t, None, 0)
        arrs.append(a)
        rows = a.shape[-2] if rows is None else rows
        if a.ndim == 2:
            specs.append(pl.BlockSpec((ts, a.shape[1] if w is None else w), lambda i, cb=cb: (i, cb)))
        else:
            specs.append(pl.BlockSpec((a.shape[0], ts, a.shape[2]), lambda i: (0, i, 0)))
    for a in consts:
        specs.append(pl.BlockSpec(a.shape, lambda i, n=a.ndim: (0,) * n))
    o_specs, o_shapes = [], []
    for shp, dt in outs:
        if len(shp) == 2:
            o_specs.append(pl.BlockSpec((ts, shp[1]), lambda i: (i, 0)))
        else:
            o_specs.append(pl.BlockSpec((shp[0], ts, shp[2]), lambda i: (0, i, 0)))
        o_shapes.append(jax.ShapeDtypeStruct(shp, dt))
    for shp in reds:
        o_specs.append(pl.BlockSpec(shp, lambda i, n=len(shp): (0,) * n))
        o_shapes.append(jax.ShapeDtypeStruct(shp, F32))
    nt, nc, no = len(arrs), len(consts), len(outs)

    def body(*refs):
        i = pl.program_id(0)
        red_refs = refs[nt + nc + no:]
        vals = fn(refs[:nt], refs[nt:nt + nc], refs[nt + nc:nt + nc + no])
        if red_refs:
            @pl.when(i == 0)
            def _():
                for r in red_refs:
                    r[...] = jnp.zeros_like(r)
            for r, v in zip(red_refs, vals):
                r[...] += v

    return _pcall(body, name=name, grid=(rows // ts,), in_specs=specs, out_specs=o_specs, out_shape=o_shapes,
                  compiler_params=_params(("arbitrary",)))(*arrs, *consts)


def cast_bf16(w, l, name):
    _, R, C = w.shape
    tr = _tile(R, 512, 16)

    def body(w_ref, o_ref):
        o_ref[...] = w_ref[...].astype(BF16)

    return _pcall(body, name=name, grid=(R // tr,), in_specs=[pl.BlockSpec((None, tr, C), lambda i: (l, i, 0))],
                  out_specs=pl.BlockSpec((tr, C), lambda i: (i, 0)), out_shape=jax.ShapeDtypeStruct((R, C), BF16),
                  compiler_params=_params(("parallel",)))(w)


def prenorm(x, g, scale, shift, name):
    S, D = x.shape

    def fn(t, c, o):
        xv = t[0][...]
        r = _rstd(xv)
        o[0][...] = ((xv * r * c[0][...]) * (1.0 + c[1][...]) + c[2][...]).astype(BF16)
        o[1][...] = r
        return []

    return rowwise(fn, [x], [g, scale, shift], [((S, D), BF16), ((S, 1), F32)], ts=_tile(S, ROW_TILE, 16), name=name)


def _gate_stage(dx, y_ref, gate_ref, dy_ref):
    dy_ref[...] = (dx * gate_ref[...]).astype(BF16)
    return _colsum(dx * y_ref[...].astype(F32))


def norm_bwd(dh, x, rstd, dres, g, scale, name, gated=None):
    S, D = x.shape

    def fn(t, c, o):
        d = t[0][...]
        r = t[2][...]
        xh = t[1][...] * r
        gv = c[0][...]
        dr = d * (1.0 + c[1][...])
        dx, dg = _rms_bwd(dr, xh, r, gv)
        dx = t[3][...] + dx
        o[0][...] = dx
        reds = [_colsum(d * (xh * gv)), _colsum(d), dg]
        if gated is not None:
            reds.append(_gate_stage(dx, t[4], c[2], o[1]))
        return reds

    ng = gated is not None
    res = rowwise(fn, [dh, x, rstd, dres] + ([gated[0]] if ng else []), [g, scale] + ([gated[1]] if ng else []),
                  [((S, D), F32)] + ([((S, D), BF16)] if ng else []), [(1, D)] * (3 + ng), ts=_tile(S, ROW_TILE, 8),
                  name=name)
    return (res[0], res[2], res[3], res[4], res[1], res[5]) if ng else res


def loss_grad(y, tgt, yb, gate, name):
    S, D = y.shape

    def fn(t, c, o):
        e = t[0][...] - t[1][...]
        dx = e * (1.0 / D)
        o[0][...] = dx
        return [_colsum(e * e), _gate_stage(dx, t[2], c[0], o[1])]

    return rowwise(fn, [y, tgt, yb], [gate], outs=[((S, D), F32), ((S, D), BF16)], reds=[(1, D)] * 2,
                   ts=_tile(S, ROW_TILE, 8), name=name)


def _tril_mask():
    r = lax.broadcasted_iota(jnp.int32, (CHUNK, CHUNK), 0)
    c = lax.broadcasted_iota(jnp.int32, (CHUNK, CHUNK), 1)
    return c <= r


def sgu_fwd(proj, ng, w, bcol, DA, name):
    S = proj.shape[0]
    G = DA // GROUP
    tr = _tile(S, 2 * CHUNK)

    def body(u_ref, v_ref, ng_ref, w_ref, b_ref, o_ref):
        mask = _tril_mask()
        for g in range(G):
            cols = slice(g * GROUP, (g + 1) * GROUP)
            wm = jnp.where(mask, w_ref[g], 0.0).astype(BF16)
            for ci in range(tr // CHUNK):
                rows = slice(ci * CHUNK, (ci + 1) * CHUNK)
                gv = _gelu(v_ref[rows, cols])
                vn = gv * _rstd(gv) * ng_ref[:, cols]
                mixed = jnp.dot(wm, vn.astype(BF16), preferred_element_type=F32) + b_ref[g]
                o_ref[rows, cols] = (_gelu(u_ref[rows, cols]) * mixed).astype(o_ref.dtype)

    return _pcall(
        body, name=name, grid=(S // tr,),
        in_specs=[pl.BlockSpec((tr, DA), lambda i: (i, 0)), pl.BlockSpec((tr, DA), lambda i: (i, 1)),
                  pl.BlockSpec((1, DA), lambda i: (0, 0)), pl.BlockSpec((G, CHUNK, CHUNK), lambda i: (0, 0, 0)),
                  pl.BlockSpec((G, CHUNK, 1), lambda i: (0, 0, 0))],
        out_specs=pl.BlockSpec((tr, DA), lambda i: (i, 0)),
        out_shape=jax.ShapeDtypeStruct((S, DA), BF16),
        compiler_params=_params(("parallel",)),
    )(proj, proj, ng, w, bcol)


def sgu_bwd(proj, dcat, ng, w, bcol, DA, name):
    S = proj.shape[0]
    G = DA // GROUP
    tr = _tile(S, 2 * CHUNK)
    nsteps = S // tr

    def body(u_ref, v_ref, d_ref, ng_ref, w_ref, b_ref, duv_ref, dw_ref, db_ref, dng_ref, dbacc):
        i = pl.program_id(0)

        @pl.when(i == 0)
        def _():
            dw_ref[...] = jnp.zeros_like(dw_ref)
            dng_ref[...] = jnp.zeros_like(dng_ref)
            dbacc[...] = jnp.zeros_like(dbacc)

        mask = _tril_mask()
        for g in range(G):
            cols = slice(g * GROUP, (g + 1) * GROUP)
            wm = jnp.where(mask, w_ref[g], 0.0).astype(BF16)
            ngg = ng_ref[:, cols]
            for ci in range(tr // CHUNK):
                rows = slice(ci * CHUNK, (ci + 1) * CHUNK)
                u, v, d = u_ref[rows, cols], v_ref[rows, cols], d_ref[rows, cols]
                gv = _gelu(v)
                rs = _rstd(gv)
                vhat = gv * rs
                vn = (vhat * ngg).astype(BF16)
                mixed = jnp.dot(wm, vn, preferred_element_type=F32) + b_ref[g]
                dmixed = d * _gelu(u)
                dmb = dmixed.astype(BF16)
                duv_ref[rows, cols] = (d * mixed * _gelu_grad(u)).astype(duv_ref.dtype)
                dwg = lax.dot_general(dmb, vn, _DIMS["nt"], preferred_element_type=F32)
                dw_ref[g] += jnp.where(mask, dwg, 0.0)
                dbacc[g] += dmixed
                dvn = lax.dot_general(wm, dmb, _DIMS["tn"], preferred_element_type=F32)
                dgv, dngg = _rms_bwd(dvn, vhat, rs, ngg)
                dng_ref[:, cols] += dngg
                duv_ref[rows, DA + g * GROUP:DA + (g + 1) * GROUP] = (dgv * _gelu_grad(v)).astype(duv_ref.dtype)

        @pl.when(i == nsteps - 1)
        def _():
            for g in range(G):
                db_ref[g] = jnp.sum(dbacc[g], axis=-1, keepdims=True)

    return _pcall(
        body, name=name, grid=(nsteps,),
        in_specs=[pl.BlockSpec((tr, DA), lambda i: (i, 0)), pl.BlockSpec((tr, DA), lambda i: (i, 1)),
                  pl.BlockSpec((tr, DA), lambda i: (i, 0)),
                  pl.BlockSpec((1, DA), lambda i: (0, 0)), pl.BlockSpec((G, CHUNK, CHUNK), lambda i: (0, 0, 0)),
                  pl.BlockSpec((G, CHUNK, 1), lambda i: (0, 0, 0))],
        out_specs=[pl.BlockSpec((tr, 2 * DA), lambda i: (i, 0)), pl.BlockSpec((G, CHUNK, CHUNK), lambda i: (0, 0, 0)),
                   pl.BlockSpec((G, CHUNK, 1), lambda i: (0, 0, 0)), pl.BlockSpec((1, DA), lambda i: (0, 0))],
        out_shape=[jax.ShapeDtypeStruct((S, 2 * DA), BF16), jax.ShapeDtypeStruct((G, CHUNK, CHUNK), F32),
                   jax.ShapeDtypeStruct((G, CHUNK, 1), F32), jax.ShapeDtypeStruct((1, DA), F32)],
        scratch_shapes=[pltpu.VMEM((G, CHUNK, CHUNK), F32)],
        compiler_params=_params(("arbitrary",)),
    )(proj, proj, dcat, ng, w, bcol)


def _conv_tile(S):
    return _tile(S, 256, 8)


def conv_fwd(proj, wk, bias, DA, DB, name):
    S = proj.shape[0]
    nb = DB // LANES
    a0, g0 = 2 * DA // LANES, (2 * DA + DB) // LANES
    T = _conv_tile(S)
    off = CONV_PAD - (CONV_W - 1)

    def body(a_ref, g_ref, w_ref, b_ref, o_ref, ypad):
        ypad[0:CONV_PAD, :] = jnp.zeros((CONV_PAD, LANES), F32)

        def fill(t, cr):
            r = pl.multiple_of(t * T, T)
            ypad[pl.ds(CONV_PAD + r, T), :] = a_ref[pl.ds(r, T), :] * _sigmoid(g_ref[pl.ds(r, T), :])
            return cr

        lax.fori_loop(0, S // T, fill, 0)

        def step(t, cr):
            r = pl.multiple_of(t * T, T)
            acc = jnp.zeros((T, LANES), F32) + b_ref[...]
            for k in range(CONV_W):
                acc = acc + w_ref[k:k + 1, :] * ypad[pl.ds(r + (k + off), T), :]
            o_ref[pl.ds(r, T), :] = acc
            return cr

        lax.fori_loop(0, S // T, step, 0)

    return _pcall(
        body, name=name, grid=(nb,),
        in_specs=[pl.BlockSpec((S, LANES), lambda j: (0, a0 + j)), pl.BlockSpec((S, LANES), lambda j: (0, g0 + j)),
                  pl.BlockSpec((CONV_PAD, LANES), lambda j: (0, j)), pl.BlockSpec((1, LANES), lambda j: (0, j))],
        out_specs=pl.BlockSpec((S, LANES), lambda j: (0, j)),
        out_shape=jax.ShapeDtypeStruct((S, DB), F32),
        scratch_shapes=[pltpu.VMEM((S + CONV_PAD, LANES), F32)],
        compiler_params=_params(("parallel",)),
    )(proj, proj, wk, bias)


def conv_bwd(proj, dy1, wk, DA, DB, name):
    S = proj.shape[0]
    nb = DB // LANES
    a0, g0 = 2 * DA // LANES, (2 * DA + DB) // LANES
    T = _conv_tile(S)
    off = CONV_PAD - (CONV_W - 1)

    def body(a_ref, g_ref, d_ref, w_ref, da_ref, dg_ref, dw_ref, ypad, dpad, wacc):
        ypad[0:CONV_PAD, :] = jnp.zeros((CONV_PAD, LANES), F32)
        dpad[S:S + CONV_PAD, :] = jnp.zeros((CONV_PAD, LANES), F32)
        wacc[...] = jnp.zeros_like(wacc)

        def fill(t, cr):
            r = pl.multiple_of(t * T, T)
            ypad[pl.ds(CONV_PAD + r, T), :] = a_ref[pl.ds(r, T), :] * _sigmoid(g_ref[pl.ds(r, T), :])
            dpad[pl.ds(r, T), :] = d_ref[pl.ds(r, T), :]
            return cr

        lax.fori_loop(0, S // T, fill, 0)

        def step(t, cr):
            r = pl.multiple_of(t * T, T)
            dt = dpad[pl.ds(r, T), :]
            dy0 = jnp.zeros((T, LANES), F32)
            for k in range(CONV_W):
                prod = dt * ypad[pl.ds(r + (k + off), T), :]
                wacc[k] += jnp.sum(prod.reshape(T // 8, 8, LANES), axis=0)
                dy0 = dy0 + w_ref[k:k + 1, :] * dpad[pl.ds(r + (CONV_W - 1 - k), T), :]
            av, gv = a_ref[pl.ds(r, T), :], g_ref[pl.ds(r, T), :]
            sg = _sigmoid(gv)
            da_ref[pl.ds(r, T), :] = (dy0 * sg).astype(da_ref.dtype)
            dg_ref[pl.ds(r, T), :] = (dy0 * av * sg * (1.0 - sg)).astype(dg_ref.dtype)
            return cr

        lax.fori_loop(0, S // T, step, 0)
        for k in range(CONV_W):
            dw_ref[k:k + 1, :] = jnp.sum(wacc[k], axis=0, keepdims=True)
        dw_ref[CONV_W:CONV_PAD, :] = jnp.zeros((CONV_PAD - CONV_W, LANES), F32)

    return _pcall(
        body, name=name, grid=(nb,),
        in_specs=[pl.BlockSpec((S, LANES), lambda j: (0, a0 + j)), pl.BlockSpec((S, LANES), lambda j: (0, g0 + j)),
                  pl.BlockSpec((S, LANES), lambda j: (0, j)), pl.BlockSpec((CONV_PAD, LANES), lambda j: (0, j))],
        out_specs=[pl.BlockSpec((S, LANES), lambda j: (0, j)), pl.BlockSpec((S, LANES), lambda j: (0, j)),
                   pl.BlockSpec((CONV_PAD, LANES), lambda j: (0, j))],
        out_shape=[jax.ShapeDtypeStruct((S, DB), BF16), jax.ShapeDtypeStruct((S, DB), BF16),
                   jax.ShapeDtypeStruct((CONV_PAD, DB), F32)],
        scratch_shapes=[pltpu.VMEM((S + CONV_PAD, LANES), F32), pltpu.VMEM((S + CONV_PAD, LANES), F32),
                        pltpu.VMEM((CONV_PAD, 8, LANES), F32)],
        compiler_params=_params(("parallel",)),
    )(proj, proj, dy1, wk)


def _ln_stats(y):
    mu = jnp.mean(y, axis=-1, keepdims=True)
    yc = y - mu
    rs = lax.rsqrt(jnp.mean(yc * yc, axis=-1, keepdims=True) + EPS)
    return yc * rs, rs


def ln_silu(y1, lg, lb, name):
    S, DB = y1.shape

    def fn(t, c, o):
        yh, _ = _ln_stats(t[0][...])
        ln = yh * c[0][...] + c[1][...]
        o[0][...] = (ln * _sigmoid(ln)).astype(BF16)
        return []

    return rowwise(fn, [y1], [lg, lb], [((S, DB), BF16)], ts=_tile(S, ROW_TILE, 16), name=name)[0]


def ln_silu_bwd(dcat, y1, lg, lb, name):
    S, DB = y1.shape
    cb = (dcat.shape[1] - DB) // DB

    def fn(t, c, o):
        yh, rs = _ln_stats(t[1][...])
        gv = c[0][...]
        ln = yh * gv + c[1][...]
        sg = _sigmoid(ln)
        dln = t[0][...] * (sg * (1.0 + ln * (1.0 - sg)))
        dyh = dln * gv
        dy = rs * (dyh - jnp.mean(dyh, axis=-1, keepdims=True) - yh * jnp.mean(dyh * yh, axis=-1, keepdims=True))
        o[0][...] = dy
        return [_colsum(dln * yh), _colsum(dln), _colsum(dy)]

    return rowwise(fn, [(dcat, DB, cb), y1], [lg, lb], [((S, DB), F32)], [(1, DB)] * 3, ts=_tile(S, ROW_TILE, 8), name=name)


def mla_norms(proj, cos2, sinm, qg, kvg, kgr, R, name):
    S = proj.shape[0]

    def fn(t, c, o):
        cq = t[0][:, 0:R]
        ckv = t[0][:, R:2 * R]
        kr = t[0][:, 2 * R:2 * R + ROPE]
        o[0][...] = (cq * _rstd(cq) * c[0][...]).astype(BF16)
        o[1][...] = (ckv * _rstd(ckv) * c[1][...]).astype(BF16)
        o[2][...] = _rope(kr * _rstd(kr) * c[2][...], t[1][...], t[2][...])
        return []

    return rowwise(fn, [proj, cos2, sinm], [qg, kvg, kgr], [((S, R), BF16), ((S, R), BF16), ((S, ROPE), F32)],
                   ts=_tile(S, ROW_TILE, 16), name=name)


def mla_heads(q, kv, kr, cos2, sinm, qg, kg, H, name):
    S = q.shape[0]

    def fn(t, c, o):
        cs, sn = t[3][...], t[4][...]
        krv = t[2][...]
        qgn, qgr, kgn = c[0][:, 0:NOPE], c[0][:, NOPE:QK], c[1][:, 0:NOPE]
        for h in range(H):
            qn = t[0][:, QK * h:QK * h + NOPE]
            qr = t[0][:, QK * h + NOPE:QK * (h + 1)]
            o[0][h, :, 0:NOPE] = (qn * _rstd(qn) * qgn).astype(BF16)
            o[0][h, :, NOPE:QK] = _rope(qr * _rstd(qr) * qgr, cs, sn).astype(BF16)
            kn = t[1][:, (NOPE + VDIM) * h:(NOPE + VDIM) * h + NOPE]
            o[1][h, :, 0:NOPE] = (kn * _rstd(kn) * kgn).astype(BF16)
            o[1][h, :, NOPE:QK] = krv.astype(BF16)
            o[2][h] = t[1][:, (NOPE + VDIM) * h + NOPE:(NOPE + VDIM) * (h + 1)].astype(BF16)
        return []

    return rowwise(fn, [q, kv, kr, cos2, sinm], [qg, kg],
                   [((H, S, QK), BF16), ((H, S, QK), BF16), ((H, S, VDIM), BF16)], ts=_tile(S, HEAD_ROWS_FWD, 16),
                   name=name)


def mla_heads_bwd(dQ, dK, dV, q, kv, cos2, sinm, qg, kg, H, name):
    S = q.shape[0]
    KV = NOPE + VDIM

    def fn(t, c, o):
        cs, sn = t[5][...], t[6][...]
        qgn, qgr, kgn = c[0][:, 0:NOPE], c[0][:, NOPE:QK], c[1][:, 0:NOPE]
        a_qn = jnp.zeros((1, NOPE), F32)
        a_qr = jnp.zeros((1, ROPE), F32)
        a_kn = jnp.zeros((1, NOPE), F32)
        dkr = jnp.zeros((t[0].shape[1], ROPE), F32)
        for h in range(H):
            qn = t[3][:, QK * h:QK * h + NOPE]
            rs = _rstd(qn)
            dx, dg = _rms_bwd(t[0][h, :, 0:NOPE], qn * rs, rs, qgn)
            o[0][:, QK * h:QK * h + NOPE] = dx.astype(BF16)
            a_qn = a_qn + dg
            qr = t[3][:, QK * h + NOPE:QK * (h + 1)]
            rs = _rstd(qr)
            dx, dg = _rms_bwd(_unrope(t[0][h, :, NOPE:QK], cs, sn), qr * rs, rs, qgr)
            o[0][:, QK * h + NOPE:QK * (h + 1)] = dx.astype(BF16)
            a_qr = a_qr + dg
            kn = t[4][:, KV * h:KV * h + NOPE]
            rs = _rstd(kn)
            dx, dg = _rms_bwd(t[1][h, :, 0:NOPE], kn * rs, rs, kgn)
            o[1][:, KV * h:KV * h + NOPE] = dx.astype(BF16)
            a_kn = a_kn + dg
            o[1][:, KV * h + NOPE:KV * (h + 1)] = t[2][h].astype(BF16)
            dkr = dkr + t[1][h, :, NOPE:QK]
        o[2][...] = _unrope(dkr, cs, sn)
        return [a_qn, a_qr, a_kn]

    return rowwise(fn, [dQ, dK, dV, q, kv, cos2, sinm], [qg, kg],
                   [((S, H * QK), BF16), ((S, H * KV), BF16), ((S, ROPE), F32)],
                   [(1, NOPE), (1, ROPE), (1, NOPE)], ts=_tile(S, HEAD_ROWS, 16), name=name)


def mla_norms_bwd(dqn, dkvn, dkr, proj, qg, kvg, kgr, R, name):
    S = proj.shape[0]

    def fn(t, c, o):
        reds = []
        for idx, (lo, hi) in enumerate(((0, R), (R, 2 * R), (2 * R, 2 * R + ROPE))):
            xv = t[3][:, lo:hi]
            rs = _rstd(xv)
            dx, dg = _rms_bwd(t[idx][...], xv * rs, rs, c[idx][...])
            o[0][:, lo:hi] = dx.astype(BF16)
            reds.append(dg)
        return reds

    return rowwise(fn, [dqn, dkvn, dkr, proj], [qg, kvg, kgr], [((S, 2 * R + ROPE), BF16)],
                   [(1, R), (1, R), (1, ROPE)], ts=_tile(S, ROW_TILE, 16), name=name)


def _tri_rows(p, n):
    qi = 0
    for j in range(1, n):
        qi = qi + (p >= j * (j + 1) // 2).astype(jnp.int32)
    return qi, p - (qi * (qi + 1)) // 2


def _tri_cols(p, n):
    ki = 0
    for j in range(1, n):
        ki = ki + (p >= j * n - j * (j - 1) // 2).astype(jnp.int32)
    return ki, ki + p - (ki * n - (ki * (ki - 1)) // 2)


def attn_fwd(Q, K, V, name):
    H, S, _ = Q.shape
    t = _tile(S, ATTN_BLOCK)
    n = S // t
    scale = QK ** -0.5

    rs = _tile(t, ATTN_STRIP, 8)

    def body(q_ref, k_ref, v_ref, o_ref, lse_ref, m_s, l_s, acc, s_scr, p_scr):
        qi, ki = _tri_rows(pl.program_id(1), n)

        @pl.when(ki == 0)
        def _():
            m_s[...] = jnp.full_like(m_s, NEG)
            l_s[...] = jnp.zeros_like(l_s)
            acc[...] = jnp.zeros_like(acc)

        def block(diagonal):
            s_scr[...] = lax.dot_general(q_ref[...], k_ref[...], _DIMS["nt"], preferred_element_type=F32)

            def strip(i, cr):
                r = slice(i * rs, (i + 1) * rs)
                s = s_scr[r, :] * scale
                if diagonal:
                    row = i * rs + lax.broadcasted_iota(jnp.int32, (rs, t), 0)
                    s = jnp.where(lax.broadcasted_iota(jnp.int32, (rs, t), 1) <= row, s, NEG)
                m_old = m_s[r, :]
                m_new = jnp.maximum(m_old, jnp.max(s, axis=-1, keepdims=True))
                alpha = jnp.exp(m_old - m_new)
                p = jnp.exp(s - m_new)
                l_s[r, :] = alpha * l_s[r, :] + jnp.sum(p, axis=-1, keepdims=True)
                m_s[r, :] = m_new
                acc[r, :] = alpha * acc[r, :]
                p_scr[r, :] = p.astype(BF16)
                return cr

            for i in range(t // rs):
                strip(i, 0)
            acc[...] += jnp.dot(p_scr[...], v_ref[...], preferred_element_type=F32)

        @pl.when(ki < qi)
        def _():
            block(False)

        @pl.when(ki == qi)
        def _():
            block(True)

        @pl.when(ki == qi)
        def _():
            o_ref[...] = (acc[...] / l_s[...]).astype(o_ref.dtype)
            lse_ref[...] = m_s[...] + jnp.log(l_s[...])

    return _pcall(
        body, name=name, grid=(H, n * (n + 1) // 2),
        in_specs=[pl.BlockSpec((None, t, QK), lambda h, p: (h, _tri_rows(p, n)[0], 0)),
                  pl.BlockSpec((None, t, QK), lambda h, p: (h, _tri_rows(p, n)[1], 0)),
                  pl.BlockSpec((None, t, VDIM), lambda h, p: (h, _tri_rows(p, n)[1], 0))],
        out_specs=[pl.BlockSpec((t, VDIM), lambda h, p: (_tri_rows(p, n)[0], h)),
                   pl.BlockSpec((None, t, 1), lambda h, p: (h, _tri_rows(p, n)[0], 0))],
        out_shape=[jax.ShapeDtypeStruct((S, H * VDIM), BF16), jax.ShapeDtypeStruct((H, S, 1), F32)],
        scratch_shapes=[pltpu.VMEM((t, 1), F32), pltpu.VMEM((t, 1), F32), pltpu.VMEM((t, VDIM), F32),
                        pltpu.VMEM((t, t), F32), pltpu.VMEM((t, t), BF16)],
        compiler_params=_params(("parallel", "arbitrary")),
    )(Q, K, V)


def attn_bwd(Q, K, V, dO, O, lse, name):
    H, S, _ = Q.shape
    t = _tile(S, ATTN_BLOCK)
    n = S // t
    scale = QK ** -0.5

    rs = _tile(t, ATTN_STRIP, 8)

    def body(q_ref, k_ref, v_ref, do_ref, o_ref, lse_ref, dq_ref, dk_ref, dv_ref, s_scr, dp_scr, p_scr, ds_scr):
        ki, qi = _tri_cols(pl.program_id(1), n)

        @pl.when(pl.program_id(1) == 0)
        def _():
            dq_ref[...] = jnp.zeros_like(dq_ref)

        @pl.when(qi == ki)
        def _():
            dk_ref[...] = jnp.zeros_like(dk_ref)
            dv_ref[...] = jnp.zeros_like(dv_ref)

        def block(diagonal):
            s_scr[...] = lax.dot_general(q_ref[...], k_ref[...], _DIMS["nt"], preferred_element_type=F32)
            dp_scr[...] = lax.dot_general(do_ref[...], v_ref[...], _DIMS["nt"], preferred_element_type=F32)

            def strip(i, cr):
                r = slice(i * rs, (i + 1) * rs)
                s = s_scr[r, :] * scale
                if diagonal:
                    row = i * rs + lax.broadcasted_iota(jnp.int32, (rs, t), 0)
                    s = jnp.where(lax.broadcasted_iota(jnp.int32, (rs, t), 1) <= row, s, NEG)
                p = jnp.exp(s - lse_ref[r, :])
                delta = jnp.sum(do_ref[r, :].astype(F32) * o_ref[r, :].astype(F32), axis=-1, keepdims=True)
                p_scr[r, :] = p.astype(BF16)
                ds_scr[r, :] = (p * (dp_scr[r, :] - delta) * scale).astype(BF16)
                return cr

            for i in range(t // rs):
                strip(i, 0)
            ds = ds_scr[...]
            dv_ref[...] += lax.dot_general(p_scr[...], do_ref[...], _DIMS["tn"], preferred_element_type=F32)
            dk_ref[...] += lax.dot_general(ds, q_ref[...], _DIMS["tn"], preferred_element_type=F32)
            rq = pl.multiple_of(qi * t, t)
            dq_ref[pl.ds(rq, t), :] += jnp.dot(ds, k_ref[...], preferred_element_type=F32)

        @pl.when(qi > ki)
        def _():
            block(False)

        @pl.when(qi == ki)
        def _():
            block(True)

    qmap = lambda h, p: (h, _tri_cols(p, n)[1], 0)
    kmap = lambda h, p: (h, _tri_cols(p, n)[0], 0)
    return _pcall(
        body, name=name, grid=(H, n * (n + 1) // 2),
        in_specs=[pl.BlockSpec((None, t, QK), qmap),
                  pl.BlockSpec((None, t, QK), kmap),
                  pl.BlockSpec((None, t, VDIM), kmap),
                  pl.BlockSpec((t, VDIM), lambda h, p: (_tri_cols(p, n)[1], h)),
                  pl.BlockSpec((t, VDIM), lambda h, p: (_tri_cols(p, n)[1], h)),
                  pl.BlockSpec((None, t, 1), qmap)],
        out_specs=[pl.BlockSpec((None, S, QK), lambda h, p: (h, 0, 0)),
                   pl.BlockSpec((None, t, QK), kmap),
                   pl.BlockSpec((None, t, VDIM), kmap)],
        out_shape=[jax.ShapeDtypeStruct((H, S, QK), F32), jax.ShapeDtypeStruct((H, S, QK), F32),
                   jax.ShapeDtypeStruct((H, S, VDIM), F32)],
        scratch_shapes=[pltpu.VMEM((t, t), F32), pltpu.VMEM((t, t), F32), pltpu.VMEM((t, t), BF16),
                        pltpu.VMEM((t, t), BF16)],
        compiler_params=_params(("parallel", "arbitrary")),
    )(Q, K, V, dO, O, lse)


def adamw(w, m, v, g, *, row0, name, into=None):
    P, rows, C = g.shape
    tr = _tile(rows, max(16, 131072 // C), 16)
    off = row0 // tr
    assert row0 % tr == 0
    bc1 = 1.0 - ADAM_B1 ** ADAM_STEP
    bc2 = 1.0 - ADAM_B2 ** ADAM_STEP
    chained = into is not None and into is not True

    def body(w_ref, m_ref, v_ref, g_ref, *rest):
        go_ref, d_ref, mo_ref, vo_ref = rest[-4:]
        gs = g_ref[0].astype(F32)
        for p in range(1, P):
            gs = gs + g_ref[p].astype(F32)
        wv = w_ref[...]
        mn = ADAM_B1 * m_ref[...] + (1.0 - ADAM_B1) * gs
        vn = ADAM_B2 * v_ref[...] + (1.0 - ADAM_B2) * (gs * gs)
        go_ref[...] = gs
        mo_ref[...] = mn
        vo_ref[...] = vn
        d_ref[...] = -ADAM_LR * ((mn / bc1) / (jnp.sqrt(vn / bc2) + ADAM_EPS) + ADAM_WD * wv)

    wspec = pl.BlockSpec((tr, C), lambda i: (i + off, 0))
    ospec = wspec if into is not None else pl.BlockSpec((tr, C), lambda i: (i, 0))
    out_rows = w.shape[0] if into is not None else rows
    return _pcall(
        body, name=name, grid=(rows // tr,),
        in_specs=[wspec, wspec, wspec, pl.BlockSpec((P, tr, C), lambda i: (0, i, 0))] + ([_ANY] * 4 if chained else []),
        out_specs=[ospec] * 4, out_shape=[jax.ShapeDtypeStruct((out_rows, C), F32)] * 4,
        input_output_aliases={4 + q: q for q in range(4)} if chained else {},
        compiler_params=_params(("parallel",)),
    )(w, m, v, g, *(into if chained else ()))


def _coords():
    return lax.axis_index("x"), lax.axis_index("y"), lax.axis_index("c")


def _me():
    x, y, c = _coords()
    return 4 * x + 2 * y + c


_ANY = pl.BlockSpec(memory_space=pl.ANY)


def all_gather(items, name):
    n = len(items)
    blks = [a.shape if idx is None else a.shape[1:] for a, idx in items]

    def body(*refs):
        ins, outs = refs[:n], refs[n:2 * n]
        send, recv, lsem = refs[2 * n:]
        x, y, c = _coords()
        me, sib = (x, y, c), (x, y, 1 - c)
        chips = [(1 - x, y), (x, 1 - y), (1 - x, 1 - y)]

        def src(i):
            return ins[i] if items[i][1] is None else ins[i].at[items[i][1]]

        def slot(i, p):
            return outs[i].at[4 * p[0] + 2 * p[1] + p[2]]

        def cp(i, k, block, to, s=None):
            return pltpu.make_async_remote_copy(
                src_ref=slot(i, block) if s is None else s, dst_ref=slot(i, block),
                send_sem=send.at[7 * i + k], recv_sem=recv.at[7 * i + k], device_id=to, device_id_type=MESH)

        mine = [pltpu.make_async_copy(src(i), slot(i, me), lsem.at[i]) for i in range(n)]
        for m_ in mine:
            m_.start()
        first = []
        for i in range(n):
            first.append(cp(i, 0, me, sib, src(i)))
            first += [cp(i, 1 + j, me, (*chip, c), src(i)) for j, chip in enumerate(chips)]
        for f in first:
            f.start()
        passed = []
        for j, chip in enumerate(chips):
            for i in range(n):
                cp(i, 1 + j, (*chip, c), me).wait_recv()
                p_ = cp(i, 4 + j, (*chip, c), sib)
                p_.start()
                passed.append(p_)
        for i in range(n):
            cp(i, 0, sib, me).wait_recv()
            for j, chip in enumerate(chips):
                cp(i, 4 + j, (*chip, 1 - c), me).wait_recv()
        for f in first + passed:
            f.wait_send()
        for m_ in mine:
            m_.wait()

    res = _pcall(
        body, name=name, in_specs=[_ANY] * n, out_specs=[_ANY] * n,
        out_shape=[jax.ShapeDtypeStruct((NDEV,) + tuple(b), a.dtype) for b, (a, _) in zip(blks, items)],
        scratch_shapes=[pltpu.SemaphoreType.DMA((7 * n,)), pltpu.SemaphoreType.DMA((7 * n,)),
                        pltpu.SemaphoreType.DMA((n,))],
    )(*[a for a, _ in items])
    return list(res)


_HBM = pl.BlockSpec(memory_space=pltpu.HBM)
_SEM = pl.BlockSpec(memory_space=pltpu.SEMAPHORE)
_EFFECT = pltpu.SideEffectType.DATAFLOW_SIDE_EFFECTING


def _xchg_copy(src_ref, land_ref, send, recv, r, scatter, at_peer):
    x, y, c = _coords()
    px = jnp.bitwise_xor(x, (r >> 2) & 1)
    py = jnp.bitwise_xor(y, (r >> 1) & 1)
    pc = jnp.bitwise_xor(c, r & 1)
    p_i = 4 * px + 2 * py + pc
    me_i = 4 * x + 2 * y + c
    return pltpu.make_async_remote_copy(
        src_ref=src_ref.at[p_i] if scatter else src_ref, dst_ref=land_ref.at[p_i if at_peer else me_i],
        send_sem=send.at[r - 1], recv_sem=recv.at[r - 1], device_id=(px, py, pc), device_id_type=MESH)


def _phase(body, name, bufs, sems_in=(), new_sems=(), after=None, token=False):
    nb, ns, nn = len(bufs), len(sems_in), len(new_sems)

    def wrapped(*refs):
        outs = refs[nb + ns + (after is not None):]
        body(refs[:nb], refs[nb:nb + ns], outs[:nn])
        if token:
            outs[nn + nb][...] = jnp.zeros_like(outs[nn + nb])

    res = _pcall(
        wrapped, name=name,
        out_shape=tuple([pltpu.SemaphoreType.DMA((k,)) for k in new_sems] + [pltpu.HBM(a.shape, a.dtype) for a in bufs]
                        + ([jax.ShapeDtypeStruct((8, LANES), F32)] if token else [])),
        in_specs=[_HBM] * nb + [_SEM] * ns + ([] if after is None else [_ANY]),
        out_specs=tuple([_SEM] * nn + [_HBM] * nb + ([pl.BlockSpec(memory_space=pltpu.VMEM)] if token else [])),
        input_output_aliases={i: nn + i for i in range(nb)},
        compiler_params=pltpu.CompilerParams(has_side_effects=_EFFECT),
    )(*[pltpu.with_memory_space_constraint(a, pltpu.HBM) for a in bufs], *sems_in, *([] if after is None else [after]))
    return list(res[nn:nn + nb]), list(res[:nn]), (res[nn + nb][0:1, 0:1] if token else None)


def scatter_start(srcs, name, after=None):
    n = len(srcs)
    lands = [lax.empty(s.shape, s.dtype) for s in srcs]

    def body(b, taken, new):
        me_i = _me()
        for i in range(n):
            pltpu.make_async_copy(b[i].at[me_i], b[n + i].at[me_i], new[3 * i + 2].at[0]).start()
            for r in range(1, NDEV):
                _xchg_copy(b[i], b[n + i], new[3 * i], new[3 * i + 1], r, True, False).start()

    bufs, sems, tok = _phase(body, name, list(srcs) + lands, new_sems=[NDEV - 1, NDEV - 1, 1] * n, after=after, token=True)
    return (bufs, sems), tok


def scatter_wait(handle, after, name):
    bufs, sems = handle
    n = len(bufs) // 2

    def body(b, taken, new):
        me_i = _me()
        for i in range(n):
            pltpu.make_async_copy(b[i].at[me_i], b[n + i].at[me_i], taken[3 * i + 2].at[0]).wait()
            for r in range(1, NDEV):
                cp = _xchg_copy(b[i], b[n + i], taken[3 * i], taken[3 * i + 1], r, True, True)
                cp.wait_send()
                cp.wait_recv()

    return _phase(body, name, bufs, sems_in=sems, after=after)[0][n:]


def _gather_peers():
    x, y, c = _coords()
    return (x, y, c), (x, y, 1 - c), [(1 - x, y), (x, 1 - y), (1 - x, 1 - y)]


def _row(p):
    return 4 * p[0] + 2 * p[1] + p[2]


def _gcopy(src_ref, land_ref, send, recv, k, block, to):
    return pltpu.make_async_remote_copy(
        src_ref=land_ref.at[_row(block)] if src_ref is None else src_ref, dst_ref=land_ref.at[_row(block)],
        send_sem=send.at[k], recv_sem=recv.at[k], device_id=to, device_id_type=MESH)


def gather_start(srcs, name, after=None):
    n = len(srcs)
    lands = [lax.empty((NDEV,) + s.shape, s.dtype) for s in srcs]

    def body(b, taken, new):
        me, sib, chips = _gather_peers()
        for i in range(n):
            send, recv = new[3 * i], new[3 * i + 1]
            pltpu.make_async_copy(b[i], b[n + i].at[_row(me)], new[3 * i + 2].at[0]).start()
            for j, chip in enumerate(chips):
                _gcopy(b[i], b[n + i], send, recv, 1 + j, me, (*chip, me[2])).start()
            _gcopy(b[i], b[n + i], send, recv, 0, me, sib).start()

    bufs, sems, tok = _phase(body, name, list(srcs) + lands, new_sems=[4, 4, 1] * n, after=after, token=True)
    return (bufs, sems), tok


def gather_mid(handle, after, name):
    bufs, sems = handle
    n = len(bufs) // 2

    def body(b, taken, new):
        me, sib, chips = _gather_peers()
        for j, chip in enumerate(chips):
            for i in range(n):
                _gcopy(b[i], b[n + i], taken[3 * i], taken[3 * i + 1], 1 + j, (*chip, me[2]), me).wait_recv()
                _gcopy(None, b[n + i], new[2 * i], new[2 * i + 1], j, (*chip, me[2]), sib).start()
        for i in range(n):
            send, recv = taken[3 * i], taken[3 * i + 1]
            _gcopy(b[i], b[n + i], send, recv, 0, sib, me).wait_recv()
            for k in range(4):
                _gcopy(b[i], b[n + i], send, recv, k, me, sib).wait_send()
            pltpu.make_async_copy(b[i], b[n + i].at[_row(me)], taken[3 * i + 2].at[0]).wait()

    bufs, new, tok = _phase(body, name, bufs, sems_in=sems, new_sems=[3, 3] * n, after=after, token=True)
    return (bufs, new), tok


def gather_wait(handle, after, name):
    bufs, sems = handle
    n = len(bufs) // 2

    def body(b, taken, new):
        me, sib, chips = _gather_peers()
        for i in range(n):
            for j, chip in enumerate(chips):
                _gcopy(None, b[n + i], taken[2 * i], taken[2 * i + 1], j, (*chip, me[2]), sib).wait_send()
                _gcopy(None, b[n + i], taken[2 * i], taken[2 * i + 1], j, (*chip, 1 - me[2]), me).wait_recv()

    return _phase(body, name, bufs, sems_in=sems, after=after)[0][n:]


_PACK_ALIGN = 8 * LANES


def _pack(arrs, aligned=False):
    parts = []
    for a in arrs:
        f = a.reshape(-1).astype(F32)
        pad = (-f.shape[0]) % _PACK_ALIGN if aligned else 0
        parts.append(jnp.pad(f, (0, pad)) if pad else f)
    flat = jnp.concatenate(parts)
    pad = (-flat.shape[0]) % _PACK_ALIGN
    return (jnp.pad(flat, (0, pad)) if pad else flat).reshape(-1, LANES)


def _unpack(p, shapes, lead=(), aligned=False):
    nl = len(lead)
    flat = p.reshape(lead + (-1,))
    out, off = [], 0
    for shp in shapes:
        n = 1
        for d in shp:
            n *= d
        out.append(lax.slice_in_dim(flat, off, off + n, axis=nl).reshape(lead + tuple(shp)))
        off += n + ((-n) % _PACK_ALIGN if aligned else 0)
    return out


def _shard_cols(a, me, width, axis):
    return lax.dynamic_slice_in_dim(a, me * width, width, axis=axis)


def kernel(x, c, norm1_g, norm2_g, ada_w, ada_b, mlp_w1, mlp_w2, ab_w_in, sgu_norm_g, sgu_w, sgu_b, conv_w, conv_b, conv_ln_g, conv_ln_b, ab_w_out, mla_w_in, mla_q_norm_g, mla_kv_norm_g, mla_w_uq, mla_w_ukv, mla_q_head_g, mla_k_head_g, mla_w_out, loss_target, m_norm1_g, m_norm2_g, m_ada_w, m_ada_b, m_mlp_w1, m_mlp_w2, m_ab_w_in, m_sgu_norm_g, m_sgu_w, m_sgu_b, m_conv_w, m_conv_b, m_conv_ln_g, m_conv_ln_b, m_ab_w_out, m_mla_w_in, m_mla_q_norm_g, m_mla_kv_norm_g, m_mla_w_uq, m_mla_w_ukv, m_mla_q_head_g, m_mla_k_head_g, m_mla_w_out, v_norm1_g, v_norm2_g, v_ada_w, v_ada_b, v_mlp_w1, v_mlp_w2, v_ab_w_in, v_sgu_norm_g, v_sgu_w, v_sgu_b, v_conv_w, v_conv_b, v_conv_ln_g, v_conv_ln_b, v_ab_w_out, v_mla_w_in, v_mla_q_norm_g, v_mla_kv_norm_g, v_mla_w_uq, v_mla_w_ukv, v_mla_q_head_g, v_mla_k_head_g, v_mla_w_out):
    W = dict(norm1_g=norm1_g, norm2_g=norm2_g, ada_w=ada_w, ada_b=ada_b, mlp_w1=mlp_w1, mlp_w2=mlp_w2, ab_w_in=ab_w_in,
             sgu_norm_g=sgu_norm_g, sgu_w=sgu_w, sgu_b=sgu_b, conv_w=conv_w, conv_b=conv_b, conv_ln_g=conv_ln_g,
             conv_ln_b=conv_ln_b, ab_w_out=ab_w_out, mla_w_in=mla_w_in, mla_q_norm_g=mla_q_norm_g,
             mla_kv_norm_g=mla_kv_norm_g, mla_w_uq=mla_w_uq, mla_w_ukv=mla_w_ukv, mla_q_head_g=mla_q_head_g,
             mla_k_head_g=mla_k_head_g, mla_w_out=mla_w_out)
    M = dict(norm1_g=m_norm1_g, norm2_g=m_norm2_g, ada_w=m_ada_w, ada_b=m_ada_b, mlp_w1=m_mlp_w1, mlp_w2=m_mlp_w2,
             ab_w_in=m_ab_w_in, sgu_norm_g=m_sgu_norm_g, sgu_w=m_sgu_w, sgu_b=m_sgu_b, conv_w=m_conv_w, conv_b=m_conv_b,
             conv_ln_g=m_conv_ln_g, conv_ln_b=m_conv_ln_b, ab_w_out=m_ab_w_out, mla_w_in=m_mla_w_in,
             mla_q_norm_g=m_mla_q_norm_g, mla_kv_norm_g=m_mla_kv_norm_g, mla_w_uq=m_mla_w_uq, mla_w_ukv=m_mla_w_ukv,
             mla_q_head_g=m_mla_q_head_g, mla_k_head_g=m_mla_k_head_g, mla_w_out=m_mla_w_out)
    V = dict(norm1_g=v_norm1_g, norm2_g=v_norm2_g, ada_w=v_ada_w, ada_b=v_ada_b, mlp_w1=v_mlp_w1, mlp_w2=v_mlp_w2,
             ab_w_in=v_ab_w_in, sgu_norm_g=v_sgu_norm_g, sgu_w=v_sgu_w, sgu_b=v_sgu_b, conv_w=v_conv_w, conv_b=v_conv_b,
             conv_ln_g=v_conv_ln_g, conv_ln_b=v_conv_ln_b, ab_w_out=v_ab_w_out, mla_w_in=v_mla_w_in,
             mla_q_norm_g=v_mla_q_norm_g, mla_kv_norm_g=v_mla_kv_norm_g, mla_w_uq=v_mla_w_uq, mla_w_ukv=v_mla_w_ukv,
             mla_q_head_g=v_mla_q_head_g, mla_k_head_g=v_mla_k_head_g, mla_w_out=v_mla_w_out)
    ORDER = list(W)

    S, D = x.shape[1], x.shape[2]
    L, NE, NO = norm1_g.shape[0], ab_w_in.shape[0], mla_w_in.shape[0]
    DA = D // 2
    DB = D - DA
    G = DA // GROUP
    R = NDEV * mla_q_norm_g.shape[1]
    H = NDEV * mla_w_uq.shape[2] // QK
    AW = ada_w.shape[2]
    CB = conv_w.shape[2]
    me = _me()
    xs, tgt = x[0], loss_target[0]

    BIG_EVEN = ("mlp_w1", "mlp_w2", "ab_w_in", "ab_w_out")
    BIG_ODD = ("mlp_w1", "mlp_w2", "mla_w_in", "mla_w_uq", "mla_w_ukv", "mla_w_out")
    BIG = ("mlp_w1", "mlp_w2", "ab_w_in", "ab_w_out", "mla_w_in", "mla_w_uq", "mla_w_ukv", "mla_w_out")
    COL_SHARDED = ("mlp_w1", "ab_w_in", "mla_w_uq", "mla_w_ukv")
    MLP_W = ("mlp_w1", "mlp_w2")
    mixer_w = lambda l: ("ab_w_in", "ab_w_out") if l % 2 == 0 else ("mla_w_in", "mla_w_uq", "mla_w_ukv", "mla_w_out")
    widx = lambda k, l: l if k in MLP_W else l // 2

    small_in = [c, mla_q_norm_g, mla_kv_norm_g, conv_w]
    sg = all_gather([(_pack(small_in, True), None)], "gather_small")[0]
    c_all, qng_all, kvng_all, cw_all = _unpack(sg, [a.shape for a in small_in], (NDEV,), True)
    c_all = c_all.reshape(NDEV, D)
    qng_full = jnp.transpose(qng_all, (1, 0, 2)).reshape(NO, 1, R)
    kvng_full = jnp.transpose(kvng_all, (1, 0, 2)).reshape(NO, 1, R)
    cw_full = jnp.transpose(cw_all, (1, 2, 0, 3)).reshape(NE, CONV_W, DB)
    cw_pad = jnp.pad(cw_full, ((0, 0), (0, CONV_PAD - CONV_W), (0, 0)))

    def silu_fn(t, c_, o):
        v_ = t[0][...]
        o[0][...] = v_ * _sigmoid(v_)
        return []

    c_act = rowwise(silu_fn, [c_all], outs=[((NDEV, D), F32)], ts=NDEV, name="silu_c")[0]
    bias_cols = _shard_cols(ada_b, me, AW, 1).reshape(1, L * AW)
    mod_cols = mm(c_act, ada_w, "nn", name="ada_fwd", b_blocked=True, rowvecs=[bias_cols],
                  epi=lambda acc, b_: (acc + b_,), tm=NDEV, tn=768)
    mod_all = all_gather([(mod_cols, None)], "gather_mod")[0]
    mod = lax.dynamic_index_in_dim(mod_all, me, axis=1, keepdims=False)
    mod = jnp.transpose(mod.reshape(NDEV, L, AW), (1, 0, 2)).reshape(L, 6, 1, D)

    wnames = lambda l, what: mixer_w(l) if what == "mix" else MLP_W
    g_first, g_second = {}, {}
    tok_sum = jnp.zeros((1, 1), F32)
    for l in range(L):
        for what in ("mix", "mlp"):
            srcs = [cast_bf16(W[k], widx(k, l), "cast_%s_l%d" % (k, l)) for k in wnames(l, what)]
            g_first[l, what], tok = gather_start(srcs, "gather_start_%s_l%d" % (what, l), after=mod_all)
            tok_sum = tok_sum + tok
    mod = mod + tok_sum

    def pass_on(l, what, after):
        g_second[l, what], tok = gather_mid(g_first[l, what], after, "gather_mid_%s_l%d" % (what, l))
        return tok

    def wait_weights(l, what, after):
        lands = gather_wait(g_second[l, what], after, "gather_wait_%s_l%d" % (what, l))
        return {k: (ld if k in COL_SHARDED else ld.reshape(NDEV * ld.shape[1], ld.shape[2]))
                for k, ld in zip(wnames(l, what), lands)}

    mod = mod + pass_on(0, "mix", mod)

    pos = jnp.arange(S, dtype=F32)
    inv = ROPE_THETA ** (-jnp.arange(0, ROPE, 2, dtype=F32) / ROPE)
    ang = pos[:, None] * inv[None, :]
    cos2 = jnp.concatenate([jnp.cos(ang), jnp.cos(ang)], axis=1)
    sinm = jnp.concatenate([-jnp.sin(ang), jnp.sin(ang)], axis=1)

    residual = lambda acc, xr, gt: (acc, xr + gt * acc)

    saved = []
    xc = xs
    for l in range(L):
        sh1, sc1, g1, sh2, sc2, g2 = [mod[l, k] for k in range(6)]
        tag = "_l%d" % l
        sv = dict(x0=xc)
        h, sv["rstd1"] = prenorm(xc, norm1_g[l][None], sc1, sh1, "prenorm1" + tag)
        sv["h"] = h
        wl = wait_weights(l, "mix", h)
        if l % 2 == 0:
            e = l // 2
            ng = sgu_norm_g[e].reshape(1, DA)
            bcol = sgu_b[e][:, :, None]
            proj = mm(h, wl["ab_w_in"], "nn", name="ab_in" + tag, b_blocked=True)
            out_a = sgu_fwd(proj, ng, sgu_w[e], bcol, DA, "sgu_fwd" + tag)
            y1 = conv_fwd(proj, cw_pad[e], conv_b[e][None], DA, DB, "conv_fwd" + tag)
            out_b = ln_silu(y1, conv_ln_g[e][None], conv_ln_b[e][None], "ln_silu" + tag)
            cat = jnp.concatenate([out_a, out_b], axis=1)
            sv.update(proj=proj, y1=y1, cat=cat)
            mixb, x1 = mm(cat, wl["ab_w_out"], "nn", name="ab_out" + tag, out_dtypes=(BF16, F32), epi=residual,
                          extras=[xc], rowvecs=[g1])
        else:
            o_ = l // 2
            proj = mm(h, wl["mla_w_in"], "nn", name="mla_in" + tag)
            kgr = mla_k_head_g[o_][None, NOPE:QK]
            qn, kvn, kr = mla_norms(proj, cos2, sinm, qng_full[o_], kvng_full[o_], kgr, R, "mla_norms" + tag)
            q = mm(qn, wl["mla_w_uq"], "nn", name="mla_uq" + tag, b_blocked=True)
            kv = mm(kvn, wl["mla_w_ukv"], "nn", name="mla_ukv" + tag, b_blocked=True)
            Qh, Kh, Vh = mla_heads(q, kv, kr, cos2, sinm, mla_q_head_g[o_][None], mla_k_head_g[o_][None], H,
                                   "mla_heads" + tag)
            att, lse = attn_fwd(Qh, Kh, Vh, "attn_fwd" + tag)
            sv.update(proj=proj, qn=qn, kvn=kvn, q=q, kv=kv, Qh=Qh, Kh=Kh, Vh=Vh, att=att, lse=lse)
            mixb, x1 = mm(att, wl["mla_w_out"], "nn", name="mla_out" + tag, out_dtypes=(BF16, F32), epi=residual,
                          extras=[xc], rowvecs=[g1])
        sv.update(mixb=mixb, x1=x1)
        h2, sv["rstd2"] = prenorm(x1, norm2_g[l][None], sc2 + pass_on(l, "mlp", x1), sh2, "prenorm2" + tag)
        wl.update(wait_weights(l, "mlp", h2))
        sv["w"] = wl
        z, act = mm(h2, wl["mlp_w1"], "nn", name="mlp_up" + tag, b_blocked=True, out_dtypes=(BF16, BF16),
                    epi=lambda acc: (acc, jnp.square(jnp.maximum(acc, 0.0))), tm=MLP_ROWS)
        g2t = g2 + pass_on(l + 1, "mix", z) if l + 1 < L else g2
        yb, xc = mm(act, wl["mlp_w2"], "nn", name="mlp_down" + tag, out_dtypes=(BF16, F32), epi=residual,
                    extras=[x1], rowvecs=[g2t])
        sv.update(h2=h2, z=z, act=act, yb=yb)
        saved.append(sv)

    dx, dy, loss_cols, dgate2 = loss_grad(xc, tgt, saved[L - 1]["yb"], mod[L - 1, 5], "loss")
    loss = lax.psum(0.5 / D * jnp.sum(loss_cols), ("x", "y", "c"))

    big_out = {}
    sm = {k: [None] * W[k].shape[0] for k in ("norm1_g", "norm2_g", "sgu_norm_g", "sgu_w", "sgu_b", "conv_b", "conv_ln_g",
                                               "conv_ln_b", "mla_q_head_g", "mla_k_head_g", "mla_q_norm_g",
                                               "mla_kv_norm_g", "conv_w")}
    dmod = [None] * L
    flat2 = {k: W[k].reshape(-1, W[k].shape[2]) for k in BIG}
    flat2m = {k: M[k].reshape(-1, W[k].shape[2]) for k in BIG}
    flat2v = {k: V[k].reshape(-1, W[k].shape[2]) for k in BIG}

    def send_grads(gr, what, tag, after=None):
        names = list(gr)
        blocks = [gr[k] if k in COL_SHARDED else gr[k].reshape(NDEV, gr[k].shape[0] // NDEV, gr[k].shape[1])
                  for k in names]
        handle, tok = scatter_start(blocks, "scatter_start_%s%s" % (what, tag), after=after)
        return (names, handle, what, tag), tok

    rep = ("norm1_g", "norm2_g", "sgu_norm_g", "sgu_w", "sgu_b", "conv_b", "conv_ln_g", "conv_ln_b", "mla_q_head_g",
           "mla_k_head_g")
    part_full = {"mla_q_norm_g": (NO, R), "mla_kv_norm_g": (NO, R), "conv_w": (NE, CONV_W, DB)}
    small = ["ada_b"] + list(rep) + list(part_full)
    shapes = [(L, 6 * D)] + [W[k].shape for k in rep] + list(part_full.values())

    def small_gather():
        parts = [jnp.stack(dmod).reshape(L, 6 * D)] + [jnp.stack(sm[k]).reshape(s_) for k, s_ in zip(small[1:], shapes[1:])]
        return all_gather([(_pack(parts), None)], "gather_smallgrads")[0]

    def scatter_finish(pending, after, l):
        names, handle, what, tag = pending
        landed = scatter_wait(handle, after, "scatter_wait_%s%s" % (what, tag))
        for k, land in zip(names, landed):
            li = widx(k, l)
            big_out[k] = adamw(flat2[k], flat2m[k], flat2v[k], land, row0=li * W[k].shape[1],
                               name="adamw_%s%s" % (k, tag), into=big_out.get(k, True))

    pend_mix, tok_mix = None, None
    for l in reversed(range(L)):
        sh1, sc1, g1, sh2, sc2, g2 = [mod[l, k] for k in range(6)]
        sv = saved[l]
        wl = sv["w"]
        tag = "_l%d" % l
        gr = {}
        dw2 = mm(sv["act"], dy, "tn", name="mlp_down_dw" + tag, out_dtypes=(BF16,), tm=MLP_ROWS)
        pend_w2, tok_w2 = send_grads({"mlp_w2": dw2}, "w2", tag)
        dz = mm(dy, wl["mlp_w2"], "nt", name="mlp_down_dx" + tag, out_dtypes=(BF16,), extras=[sv["z"]],
                rowvecs=[jnp.zeros((1, sv["z"].shape[1]), F32) + tok_w2],
                epi=lambda acc, z_, t_: (acc * (2.0 * jnp.maximum(z_.astype(F32), 0.0)) + t_,), tm=MLP_ROWS)
        dh2 = mm(dz, wl["mlp_w1"], "nt", name="mlp_up_dx" + tag, b_blocked=True)
        dw1 = mm(sv["h2"], dz, "tn", name="mlp_up_dw" + tag, out_dtypes=(BF16,), out_blocked=NDEV, tm=MLP_ROWS)
        pend_mlp, tok_mlp = send_grads({"mlp_w1": dw1}, "w1", tag)
        dx1, dsc2, dsh2, sm["norm2_g"][l], dmix, dgate1 = norm_bwd(
            dh2, sv["x1"], sv["rstd2"], dx, norm2_g[l][None], sc2 + tok_mlp, "norm2_bwd" + tag, gated=(sv["mixb"], g1))
        if pend_mix is not None:
            scatter_finish(pend_mix, dx1, l + 1)
        if l % 2 == 0:
            e = l // 2
            ng = sgu_norm_g[e].reshape(1, DA)
            bcol = sgu_b[e][:, :, None]
            dcat = mm(dmix, wl["ab_w_out"], "nt", name="ab_out_dx" + tag)
            gr["ab_w_out"] = mm(sv["cat"], dmix, "tn", name="ab_out_dw" + tag, out_dtypes=(BF16,))
            dy1, sm["conv_ln_g"][e], sm["conv_ln_b"][e], sm["conv_b"][e] = ln_silu_bwd(
                dcat, sv["y1"], conv_ln_g[e][None], conv_ln_b[e][None], "ln_silu_bwd" + tag)
            da, dg_, dwc = conv_bwd(sv["proj"], dy1, cw_pad[e], DA, DB, "conv_bwd" + tag)
            sm["conv_w"][e] = dwc[:CONV_W]
            duv, dsw, dsb, dsng = sgu_bwd(sv["proj"], dcat, ng, sgu_w[e], bcol, DA, "sgu_bwd" + tag)
            sm["sgu_w"][e], sm["sgu_b"][e], sm["sgu_norm_g"][e] = dsw, dsb, dsng
            dproj = jnp.concatenate([duv, da, dg_], axis=1)
            dh = mm(dproj, wl["ab_w_in"], "nt", name="ab_in_dx" + tag, b_blocked=True)
            gr["ab_w_in"] = mm(sv["h"], dproj, "tn", name="ab_in_dw" + tag, out_dtypes=(BF16,), out_blocked=NDEV)
        else:
            o_ = l // 2
            kgr = mla_k_head_g[o_][None, NOPE:QK]
            dO = mm(dmix, wl["mla_w_out"], "nt", name="mla_out_dx" + tag, out_dtypes=(BF16,))
            gr["mla_w_out"] = mm(sv["att"], dmix, "tn", name="mla_out_dw" + tag, out_dtypes=(BF16,))
            dQ, dK, dV = attn_bwd(sv["Qh"], sv["Kh"], sv["Vh"], dO, sv["att"], sv["lse"], "attn_bwd" + tag)
            dq_pre, dkv_pre, dkr, dqgn, dqgr, dkgn = mla_heads_bwd(
                dQ, dK, dV, sv["q"], sv["kv"], cos2, sinm, mla_q_head_g[o_][None], mla_k_head_g[o_][None], H,
                "mla_heads_bwd" + tag)
            dqn = mm(dq_pre, wl["mla_w_uq"], "nt", name="mla_uq_dx" + tag, b_blocked=True)
            gr["mla_w_uq"] = mm(sv["qn"], dq_pre, "tn", name="mla_uq_dw" + tag, out_dtypes=(BF16,), out_blocked=NDEV)
            dkvn = mm(dkv_pre, wl["mla_w_ukv"], "nt", name="mla_ukv_dx" + tag, b_blocked=True)
            gr["mla_w_ukv"] = mm(sv["kvn"], dkv_pre, "tn", name="mla_ukv_dw" + tag, out_dtypes=(BF16,),
                                 out_blocked=NDEV)
            dproj, sm["mla_q_norm_g"][o_], sm["mla_kv_norm_g"][o_], dkgr = mla_norms_bwd(
                dqn, dkvn, dkr, sv["proj"], qng_full[o_], kvng_full[o_], kgr, R, "mla_norms_bwd" + tag)
            sm["mla_q_head_g"][o_] = jnp.concatenate([dqgn, dqgr], axis=1)
            sm["mla_k_head_g"][o_] = jnp.concatenate([dkgn, dkgr], axis=1)
            dh = mm(dproj, wl["mla_w_in"], "nt", name="mla_in_dx" + tag)
            gr["mla_w_in"] = mm(sv["h"], dproj, "tn", name="mla_in_dw" + tag, out_dtypes=(BF16,))
        if l > 0:
            pend_mix, tok_mix = send_grads(gr, "mix", tag)
            dx, dsc1, dsh1, sm["norm1_g"][l], dy, dgate2_below = norm_bwd(
                dh, sv["x0"], sv["rstd1"], dx1, norm1_g[l][None], sc1 + tok_mix, "norm1_bwd" + tag,
                gated=(saved[l - 1]["yb"], mod[l - 1, 5]))
        else:
            dx, dsc1, dsh1, sm["norm1_g"][l] = norm_bwd(dh, sv["x0"], sv["rstd1"], dx1, norm1_g[l][None], sc1,
                                                        "norm1_bwd" + tag)
        dmod[l] = jnp.concatenate([dsh1, dsc1, dgate1, dsh2, dsc2, dgate2], axis=1)
        if l > 0:
            dgate2 = dgate2_below
        else:
            gp = small_gather()
            pend_mix, tok_mix = send_grads(gr, "mix", tag, after=gp)
        scatter_finish(pend_w2, dx, l)
        scatter_finish(pend_mlp, dx, l)

    rows_p = gp.shape[1]
    per = 6 * D // LANES
    dm = lax.slice_in_dim(gp, 0, L * per, axis=1).reshape(NDEV, L, per, LANES)
    dmod_cols = lax.dynamic_slice_in_dim(dm, me * (AW // LANES), AW // LANES, axis=2).reshape(NDEV, L * AW) + tok_mix

    def sum_fn(t, c_, o):
        acc = t[0][0]
        for s_ in range(1, NDEV):
            acc = acc + t[0][s_]
        o[0][...] = acc
        return []

    gsummed = rowwise(sum_fn, [gp], outs=[((rows_p, LANES), F32)], ts=_tile(rows_p, 256, 8), name="sum_smallgrads")[0]
    gsum = dict(zip(small, _unpack(gsummed, shapes)))
    gsum["mla_q_norm_g"] = _shard_cols(gsum["mla_q_norm_g"], me, R // NDEV, 1)
    gsum["mla_kv_norm_g"] = _shard_cols(gsum["mla_kv_norm_g"], me, R // NDEV, 1)
    gsum["conv_w"] = _shard_cols(gsum["conv_w"], me, CB, 2)
    sm_shapes = [W[k].shape for k in small]
    sres = adamw(_pack([W[k] for k in small]), _pack([M[k] for k in small]), _pack([V[k] for k in small]),
                 (_pack([gsum[k] for k in small]) + tok_mix)[None], row0=0, name="adamw_small")
    small_out = {k: vals for k, vals in zip(small, zip(*[_unpack(r_, sm_shapes) for r_ in sres]))}

    g_ada = mm(c_act, dmod_cols, "tn", name="ada_dw", out_blocked=L, tm=1024, tn=768, tk=NDEV, cast=None,
               precision=lax.Precision.HIGHEST)
    ada_out = adamw(ada_w.reshape(L * D, AW), m_ada_w.reshape(L * D, AW), v_ada_w.reshape(L * D, AW),
                    g_ada.reshape(1, L * D, AW), row0=0, name="adamw_ada_w")
    scatter_finish(pend_mix, ada_out[1], 0)
    ada_out = [a.reshape(L, D, AW) for a in ada_out]

    def result(k, which):
        if k == "ada_w":
            return ada_out[which]
        if k in BIG:
            return big_out[k][which].reshape(W[k].shape)
        return small_out[k][which]

    outs = [loss, dx[None]]
    for which in range(4):
        outs += [result(k, which) for k in ORDER]
    return tuple(outs)
```

```python
import functools

import jax
import jax.numpy as jnp
from jax import lax
from jax.experimental import pallas as pl
from jax.experimental.pallas import tpu as pltpu

F32 = jnp.float32
BF16 = jnp.bfloat16
EPS = 1e-6
NDEV = 8
LANES = 128
CHUNK = 128
GROUP = 128
CONV_W = 31
CONV_PAD = 32
NOPE, ROPE, VDIM = 128, 64, 128
QK = NOPE + ROPE
ROPE_THETA = 10000.0
VMEM_LIMIT = 56 * 1024 * 1024
ADAM_LR, ADAM_B1, ADAM_B2, ADAM_EPS, ADAM_WD, ADAM_STEP = 0.001, 0.9, 0.999, 1e-08, 0.01, 10
MESH = pl.DeviceIdType.MESH
NEG = -1e30
MLP_ROWS = 2048
ROW_TILE = 256
HEAD_ROWS = 256
HEAD_ROWS_FWD = 512
ATTN_BLOCK = 1024
ATTN_STRIP = 64


def _pcall(body, **kw):
    return pl.pallas_call(body, **kw)


def _params(sem=None):
    return pltpu.CompilerParams(dimension_semantics=sem, vmem_limit_bytes=VMEM_LIMIT)


def _tile(dim, target, align=LANES):
    if dim <= target:
        return dim
    t = (target // align) * align
    while t >= align:
        if dim % t == 0:
            return t
        t -= align
    return dim


def _rstd(x):
    return lax.rsqrt(jnp.mean(x * x, axis=-1, keepdims=True) + EPS)


def _sigmoid(x):
    return 1.0 / (1.0 + jnp.exp(-x))


_GC = 0.7978845608028654


def _gelu(x):
    return 0.5 * x * (1.0 + jnp.tanh(_GC * (x + 0.044715 * x * x * x)))


def _gelu_grad(x):
    t = jnp.tanh(_GC * (x + 0.044715 * x * x * x))
    return 0.5 * (1.0 + t) + 0.5 * x * (1.0 - t * t) * _GC * (1.0 + 3 * 0.044715 * x * x)


def _colsum(x):
    return jnp.sum(x, axis=0, keepdims=True)


def _rms_bwd(dy, xhat, rstd, g):
    dxh = dy * g
    dx = rstd * (dxh - xhat * jnp.mean(dxh * xhat, axis=-1, keepdims=True))
    return dx, _colsum(dy * xhat)


def _swap_halves(x):
    h = x.shape[-1] // 2
    return jnp.concatenate([x[:, h:], x[:, :h]], axis=1)


def _rope(x, cos2, sinm):
    return x * cos2 + _swap_halves(x) * sinm


def _unrope(dy, cos2, sinm):
    return dy * cos2 + _swap_halves(dy * sinm)


_DIMS = {"nn": (((1,), (0,)), ((), ())), "nt": (((1,), (1,)), ((), ())), "tn": (((0,), (0,)), ((), ()))}


def mm(a, b, mode, *, name, out_dtypes=(F32,), epi=None, extras=(), rowvecs=(), b_blocked=False, out_blocked=0,
       tm=1024, tn=1024, tk=2048, precision=None, cast=BF16):
    if mode == "tn":
        K, M = a.shape
    else:
        M, K = a.shape
    if b_blocked:
        J, Rb, Cb = b.shape
        N = Rb if mode == "nt" else J * Cb
    else:
        N = b.shape[0] if mode == "nt" else b.shape[1]
    tm = _tile(M, tm)
    nbo = 1
    nb = 1
    if mode == "nn" and b_blocked:
        if Cb >= tn:
            tn = _tile(Cb, tn)
        else:
            nb = max(d for d in range(1, J + 1) if J % d == 0 and d * Cb <= tn)
            tn = nb * Cb
    elif out_blocked:
        Nb = N // out_blocked
        if Nb >= tn:
            tn = _tile(Nb, tn)
        else:
            nbo = max(d for d in range(1, out_blocked + 1) if out_blocked % d == 0 and d * Nb <= tn)
            tn = nbo * Nb
    else:
        tn = _tile(N, tn)
    kb = 1
    if mode == "nt" and b_blocked:
        if Cb >= tk:
            tk = _tile(Cb, tk)
        else:
            kb = max(d for d in range(1, J + 1) if J % d == 0 and d * Cb <= tk)
            tk = kb * Cb
    else:
        tk = _tile(K, tk)
    nk = K // tk
    grid = (M // tm, N // tn, nk)

    if mode == "tn":
        a_spec = pl.BlockSpec((tk, tm), lambda i, j, k: (k, i))
    else:
        a_spec = pl.BlockSpec((tm, tk), lambda i, j, k: (i, k))
    if mode == "nn":
        if b_blocked:
            if nb > 1:
                b_spec = pl.BlockSpec((nb, tk, Cb), lambda i, j, k: (j, k, 0))
            else:
                nper = Cb // tn
                b_spec = pl.BlockSpec((None, tk, tn), lambda i, j, k: (j // nper, k, j % nper))
        else:
            b_spec = pl.BlockSpec((tk, tn), lambda i, j, k: (k, j))
    elif mode == "nt":
        if b_blocked:
            if kb > 1:
                b_spec = pl.BlockSpec((kb, tn, Cb), lambda i, j, k: (k, j, 0))
            else:
                kper = Cb // tk
                b_spec = pl.BlockSpec((None, tn, tk), lambda i, j, k: (k // kper, j, k % kper))
        else:
            b_spec = pl.BlockSpec((tn, tk), lambda i, j, k: (j, k))
    else:
        b_spec = pl.BlockSpec((tk, tn), lambda i, j, k: (k, j))
    if out_blocked:
        if nbo > 1:
            o_spec = pl.BlockSpec((nbo, tm, N // out_blocked), lambda i, j, k: (j, i, 0))
        else:
            oper = (N // out_blocked) // tn
            o_spec = pl.BlockSpec((None, tm, tn), lambda i, j, k: (j // oper, i, j % oper))
        o_shape = (out_blocked, M, N // out_blocked)
    else:
        o_spec = pl.BlockSpec((tm, tn), lambda i, j, k: (i, j))
        o_shape = (M, N)
    e_spec = pl.BlockSpec((tm, tn), lambda i, j, k: (i, j))
    r_spec = pl.BlockSpec((1, tn), lambda i, j, k: (0, j))
    ne, nr, no = len(extras), len(rowvecs), len(out_dtypes)
    dims = _DIMS[mode]

    def body(a_ref, b_ref, *rest):
        ex = rest[:ne]
        rv = rest[ne:ne + nr]
        outs = rest[ne + nr:ne + nr + no]

        def product():
            if nb > 1:
                av = a_ref[...] if cast is None else a_ref[...].astype(cast)
                return jnp.concatenate(
                    [lax.dot_general(av, b_ref[q] if cast is None else b_ref[q].astype(cast), dims,
                                     preferred_element_type=F32, precision=precision) for q in range(nb)], axis=1)
            if kb > 1:
                r = None
                for q in range(kb):
                    av, bv = a_ref[:, q * Cb:(q + 1) * Cb], b_ref[q]
                    if cast is not None:
                        av, bv = av.astype(cast), bv.astype(cast)
                    d = lax.dot_general(av, bv, dims, preferred_element_type=F32, precision=precision)
                    r = d if r is None else r + d
                return r
            av, bv = a_ref[...], b_ref[...]
            if cast is not None:
                av, bv = av.astype(cast), bv.astype(cast)
            return lax.dot_general(av, bv, dims, preferred_element_type=F32, precision=precision)

        def finish(r):
            vals = (r,) if epi is None else epi(r, *[e[...] for e in ex], *[v[...] for v in rv])
            for o, val in zip(outs, vals):
                if nbo > 1:
                    w_ = N // out_blocked
                    for q in range(nbo):
                        o[q] = val[:, q * w_:(q + 1) * w_].astype(o.dtype)
                else:
                    o[...] = val.astype(o.dtype)

        if nk == 1:
            finish(product())
            return
        acc = rest[ne + nr + no]
        k = pl.program_id(2)

        @pl.when(k == 0)
        def _():
            acc[...] = product()

        @pl.when((k > 0) & (k < nk - 1))
        def _():
            acc[...] += product()

        @pl.when(k == nk - 1)
        def _():
            finish(acc[...] + product())

    res = _pcall(
        body, name=name, grid=grid,
        in_specs=[a_spec, b_spec] + [e_spec] * ne + [r_spec] * nr,
        out_specs=[o_spec] * no,
        out_shape=[jax.ShapeDtypeStruct(o_shape, dt) for dt in out_dtypes],
        scratch_shapes=[] if nk == 1 else [pltpu.VMEM((tm, tn), F32)],
        compiler_params=_params(("parallel", "parallel", "arbitrary")),
    )(a, b, *extras, *rowvecs)
    return res[0] if no == 1 else res


def rowwise(fn, tiled, consts=(), outs=(), reds=(), *, ts, name):
    specs = []
    arrs = []
    rows = None
    for t in tiled:
        a, w, cb = t if isinstance(t, tuple) else (t, None, 0)
        arrs.append(a)
        rows = a.shape[-2] if rows is None else rows
        if a.ndim == 2:
            specs.append(pl.BlockSpec((ts, a.shape[1] if w is None else w), lambda i, cb=cb: (i, cb)))
        else:
            specs.append(pl.BlockSpec((a.shape[0], ts, a.shape[2]), lambda i: (0, i, 0)))
    for a in consts:
        specs.append(pl.BlockSpec(a.shape, lambda i, n=a.ndim: (0,) * n))
    o_specs, o_shapes = [], []
    for shp, dt in outs:
        if len(shp) == 2:
            o_specs.append(pl.BlockSpec((ts, shp[1]), lambda i: (i, 0)))
        else:
            o_specs.append(pl.BlockSpec((shp[0], ts, shp[2]), lambda i: (0, i, 0)))
        o_shapes.append(jax.ShapeDtypeStruct(shp, dt))
    for shp in reds:
        o_specs.append(pl.BlockSpec(shp, lambda i, n=len(shp): (0,) * n))
        o_shapes.append(jax.ShapeDtypeStruct(shp, F32))
    nt, nc, no = len(arrs), len(consts), len(outs)

    def body(*refs):
        i = pl.program_id(0)
        red_refs = refs[nt + nc + no:]
        vals = fn(refs[:nt], refs[nt:nt + nc], refs[nt + nc:nt + nc + no])
        if red_refs:
            @pl.when(i == 0)
            def _():
                for r in red_refs:
                    r[...] = jnp.zeros_like(r)
            for r, v in zip(red_refs, vals):
                r[...] += v

    return _pcall(body, name=name, grid=(rows // ts,), in_specs=specs, out_specs=o_specs, out_shape=o_shapes,
                  compiler_params=_params(("arbitrary",)))(*arrs, *consts)


def cast_bf16(w, l, name):
    _, R, C = w.shape
    tr = _tile(R, 512, 16)

    def body(w_ref, o_ref):
        o_ref[...] = w_ref[...].astype(BF16)

    return _pcall(body, name=name, grid=(R // tr,), in_specs=[pl.BlockSpec((None, tr, C), lambda i: (l, i, 0))],
                  out_specs=pl.BlockSpec((tr, C), lambda i: (i, 0)), out_shape=jax.ShapeDtypeStruct((R, C), BF16),
                  compiler_params=_params(("parallel",)))(w)


def prenorm(x, g, scale, shift, name):
    S, D = x.shape

    def fn(t, c, o):
        xv = t[0][...]
        r = _rstd(xv)
        o[0][...] = ((xv * r * c[0][...]) * (1.0 + c[1][...]) + c[2][...]).astype(BF16)
        o[1][...] = r
        return []

    return rowwise(fn, [x], [g, scale, shift], [((S, D), BF16), ((S, 1), F32)], ts=_tile(S, ROW_TILE, 16), name=name)


def _gate_stage(dx, y_ref, gate_ref, dy_ref):
    dy_ref[...] = (dx * gate_ref[...]).astype(BF16)
    return _colsum(dx * y_ref[...].astype(F32))


def norm_bwd(dh, x, rstd, dres, g, scale, name, gated=None):
    S, D = x.shape

    def fn(t, c, o):
        d = t[0][...]
        r = t[2][...]
        xh = t[1][...] * r
        gv = c[0][...]
        dr = d * (1.0 + c[1][...])
        dx, dg = _rms_bwd(dr, xh, r, gv)
        dx = t[3][...] + dx
        o[0][...] = dx
        reds = [_colsum(d * (xh * gv)), _colsum(d), dg]
        if gated is not None:
            reds.append(_gate_stage(dx, t[4], c[2], o[1]))
        return reds

    ng = gated is not None
    res = rowwise(fn, [dh, x, rstd, dres] + ([gated[0]] if ng else []), [g, scale] + ([gated[1]] if ng else []),
                  [((S, D), F32)] + ([((S, D), BF16)] if ng else []), [(1, D)] * (3 + ng), ts=_tile(S, ROW_TILE, 8),
                  name=name)
    return (res[0], res[2], res[3], res[4], res[1], res[5]) if ng else res


def loss_grad(y, tgt, yb, gate, name):
    S, D = y.shape

    def fn(t, c, o):
        e = t[0][...] - t[1][...]
        dx = e * (1.0 / D)
        o[0][...] = dx
        return [_colsum(e * e), _gate_stage(dx, t[2], c[0], o[1])]

    return rowwise(fn, [y, tgt, yb], [gate], outs=[((S, D), F32), ((S, D), BF16)], reds=[(1, D)] * 2,
                   ts=_tile(S, ROW_TILE, 8), name=name)


def _tril_mask():
    r = lax.broadcasted_iota(jnp.int32, (CHUNK, CHUNK), 0)
    c = lax.broadcasted_iota(jnp.int32, (CHUNK, CHUNK), 1)
    return c <= r


def sgu_fwd(proj, ng, w, bcol, DA, name):
    S = proj.shape[0]
    G = DA // GROUP
    tr = _tile(S, 2 * CHUNK)

    def body(u_ref, v_ref, ng_ref, w_ref, b_ref, o_ref):
        mask = _tril_mask()
        for g in range(G):
            cols = slice(g * GROUP, (g + 1) * GROUP)
            wm = jnp.where(mask, w_ref[g], 0.0).astype(BF16)
            for ci in range(tr // CHUNK):
                rows = slice(ci * CHUNK, (ci + 1) * CHUNK)
                gv = _gelu(v_ref[rows, cols])
                vn = gv * _rstd(gv) * ng_ref[:, cols]
                mixed = jnp.dot(wm, vn.astype(BF16), preferred_element_type=F32) + b_ref[g]
                o_ref[rows, cols] = (_gelu(u_ref[rows, cols]) * mixed).astype(o_ref.dtype)

    return _pcall(
        body, name=name, grid=(S // tr,),
        in_specs=[pl.BlockSpec((tr, DA), lambda i: (i, 0)), pl.BlockSpec((tr, DA), lambda i: (i, 1)),
                  pl.BlockSpec((1, DA), lambda i: (0, 0)), pl.BlockSpec((G, CHUNK, CHUNK), lambda i: (0, 0, 0)),
                  pl.BlockSpec((G, CHUNK, 1), lambda i: (0, 0, 0))],
        out_specs=pl.BlockSpec((tr, DA), lambda i: (i, 0)),
        out_shape=jax.ShapeDtypeStruct((S, DA), BF16),
        compiler_params=_params(("parallel",)),
    )(proj, proj, ng, w, bcol)


def sgu_bwd(proj, dcat, ng, w, bcol, DA, name):
    S = proj.shape[0]
    G = DA // GROUP
    tr = _tile(S, 2 * CHUNK)
    nsteps = S // tr

    def body(u_ref, v_ref, d_ref, ng_ref, w_ref, b_ref, duv_ref, dw_ref, db_ref, dng_ref, dbacc):
        i = pl.program_id(0)

        @pl.when(i == 0)
        def _():
            dw_ref[...] = jnp.zeros_like(dw_ref)
            dng_ref[...] = jnp.zeros_like(dng_ref)
            dbacc[...] = jnp.zeros_like(dbacc)

        mask = _tril_mask()
        for g in range(G):
            cols = slice(g * GROUP, (g + 1) * GROUP)
            wm = jnp.where(mask, w_ref[g], 0.0).astype(BF16)
            ngg = ng_ref[:, cols]
            for ci in range(tr // CHUNK):
                rows = slice(ci * CHUNK, (ci + 1) * CHUNK)
                u, v, d = u_ref[rows, cols], v_ref[rows, cols], d_ref[rows, cols]
                gv = _gelu(v)
                rs = _rstd(gv)
                vhat = gv * rs
                vn = (vhat * ngg).astype(BF16)
                mixed = jnp.dot(wm, vn, preferred_element_type=F32) + b_ref[g]
                dmixed = d * _gelu(u)
                dmb = dmixed.astype(BF16)
                duv_ref[rows, cols] = (d * mixed * _gelu_grad(u)).astype(duv_ref.dtype)
                dwg = lax.dot_general(dmb, vn, _DIMS["nt"], preferred_element_type=F32)
                dw_ref[g] += jnp.where(mask, dwg, 0.0)
                dbacc[g] += dmixed
                dvn = lax.dot_general(wm, dmb, _DIMS["tn"], preferred_element_type=F32)
                dgv, dngg = _rms_bwd(dvn, vhat, rs, ngg)
                dng_ref[:, cols] += dngg
                duv_ref[rows, DA + g * GROUP:DA + (g + 1) * GROUP] = (dgv * _gelu_grad(v)).astype(duv_ref.dtype)

        @pl.when(i == nsteps - 1)
        def _():
            for g in range(G):
                db_ref[g] = jnp.sum(dbacc[g], axis=-1, keepdims=True)

    return _pcall(
        body, name=name, grid=(nsteps,),
        in_specs=[pl.BlockSpec((tr, DA), lambda i: (i, 0)), pl.BlockSpec((tr, DA), lambda i: (i, 1)),
                  pl.BlockSpec((tr, DA), lambda i: (i, 0)),
                  pl.BlockSpec((1, DA), lambda i: (0, 0)), pl.BlockSpec((G, CHUNK, CHUNK), lambda i: (0, 0, 0)),
                  pl.BlockSpec((G, CHUNK, 1), lambda i: (0, 0, 0))],
        out_specs=[pl.BlockSpec((tr, 2 * DA), lambda i: (i, 0)), pl.BlockSpec((G, CHUNK, CHUNK), lambda i: (0, 0, 0)),
                   pl.BlockSpec((G, CHUNK, 1), lambda i: (0, 0, 0)), pl.BlockSpec((1, DA), lambda i: (0, 0))],
        out_shape=[jax.ShapeDtypeStruct((S, 2 * DA), BF16), jax.ShapeDtypeStruct((G, CHUNK, CHUNK), F32),
                   jax.ShapeDtypeStruct((G, CHUNK, 1), F32), jax.ShapeDtypeStruct((1, DA), F32)],
        scratch_shapes=[pltpu.VMEM((G, CHUNK, CHUNK), F32)],
        compiler_params=_params(("arbitrary",)),
    )(proj, proj, dcat, ng, w, bcol)


def _conv_tile(S):
    return _tile(S, 256, 8)


def conv_fwd(proj, wk, bias, DA, DB, name):
    S = proj.shape[0]
    nb = DB // LANES
    a0, g0 = 2 * DA // LANES, (2 * DA + DB) // LANES
    T = _conv_tile(S)
    off = CONV_PAD - (CONV_W - 1)

    def body(a_ref, g_ref, w_ref, b_ref, o_ref, ypad):
        ypad[0:CONV_PAD, :] = jnp.zeros((CONV_PAD, LANES), F32)

        def fill(t, cr):
            r = pl.multiple_of(t * T, T)
            ypad[pl.ds(CONV_PAD + r, T), :] = a_ref[pl.ds(r, T), :] * _sigmoid(g_ref[pl.ds(r, T), :])
            return cr

        lax.fori_loop(0, S // T, fill, 0)

        def step(t, cr):
            r = pl.multiple_of(t * T, T)
            acc = jnp.zeros((T, LANES), F32) + b_ref[...]
            for k in range(CONV_W):
                acc = acc + w_ref[k:k + 1, :] * ypad[pl.ds(r + (k + off), T), :]
            o_ref[pl.ds(r, T), :] = acc
            return cr

        lax.fori_loop(0, S // T, step, 0)

    return _pcall(
        body, name=name, grid=(nb,),
        in_specs=[pl.BlockSpec((S, LANES), lambda j: (0, a0 + j)), pl.BlockSpec((S, LANES), lambda j: (0, g0 + j)),
                  pl.BlockSpec((CONV_PAD, LANES), lambda j: (0, j)), pl.BlockSpec((1, LANES), lambda j: (0, j))],
        out_specs=pl.BlockSpec((S, LANES), lambda j: (0, j)),
        out_shape=jax.ShapeDtypeStruct((S, DB), F32),
        scratch_shapes=[pltpu.VMEM((S + CONV_PAD, LANES), F32)],
        compiler_params=_params(("parallel",)),
    )(proj, proj, wk, bias)


def conv_bwd(proj, dy1, wk, DA, DB, name):
    S = proj.shape[0]
    nb = DB // LANES
    a0, g0 = 2 * DA // LANES, (2 * DA + DB) // LANES
    T = _conv_tile(S)
    off = CONV_PAD - (CONV_W - 1)

    def body(a_ref, g_ref, d_ref, w_ref, da_ref, dg_ref, dw_ref, ypad, dpad, wacc):
        ypad[0:CONV_PAD, :] = jnp.zeros((CONV_PAD, LANES), F32)
        dpad[S:S + CONV_PAD, :] = jnp.zeros((CONV_PAD, LANES), F32)
        wacc[...] = jnp.zeros_like(wacc)

        def fill(t, cr):
            r = pl.multiple_of(t * T, T)
            ypad[pl.ds(CONV_PAD + r, T), :] = a_ref[pl.ds(r, T), :] * _sigmoid(g_ref[pl.ds(r, T), :])
            dpad[pl.ds(r, T), :] = d_ref[pl.ds(r, T), :]
            return cr

        lax.fori_loop(0, S // T, fill, 0)

        def step(t, cr):
            r = pl.multiple_of(t * T, T)
            dt = dpad[pl.ds(r, T), :]
            dy0 = jnp.zeros((T, LANES), F32)
            for k in range(CONV_W):
                prod = dt * ypad[pl.ds(r + (k + off), T), :]
                wacc[k] += jnp.sum(prod.reshape(T // 8, 8, LANES), axis=0)
                dy0 = dy0 + w_ref[k:k + 1, :] * dpad[pl.ds(r + (CONV_W - 1 - k), T), :]
            av, gv = a_ref[pl.ds(r, T), :], g_ref[pl.ds(r, T), :]
            sg = _sigmoid(gv)
            da_ref[pl.ds(r, T), :] = (dy0 * sg).astype(da_ref.dtype)
            dg_ref[pl.ds(r, T), :] = (dy0 * av * sg * (1.0 - sg)).astype(dg_ref.dtype)
            return cr

        lax.fori_loop(0, S // T, step, 0)
        for k in range(CONV_W):
            dw_ref[k:k + 1, :] = jnp.sum(wacc[k], axis=0, keepdims=True)
        dw_ref[CONV_W:CONV_PAD, :] = jnp.zeros((CONV_PAD - CONV_W, LANES), F32)

    return _pcall(
        body, name=name, grid=(nb,),
        in_specs=[pl.BlockSpec((S, LANES), lambda j: (0, a0 + j)), pl.BlockSpec((S, LANES), lambda j: (0, g0 + j)),
                  pl.BlockSpec((S, LANES), lambda j: (0, j)), pl.BlockSpec((CONV_PAD, LANES), lambda j: (0, j))],
        out_specs=[pl.BlockSpec((S, LANES), lambda j: (0, j)), pl.BlockSpec((S, LANES), lambda j: (0, j)),
                   pl.BlockSpec((CONV_PAD, LANES), lambda j: (0, j))],
        out_shape=[jax.ShapeDtypeStruct((S, DB), BF16), jax.ShapeDtypeStruct((S, DB), BF16),
                   jax.ShapeDtypeStruct((CONV_PAD, DB), F32)],
        scratch_shapes=[pltpu.VMEM((S + CONV_PAD, LANES), F32), pltpu.VMEM((S + CONV_PAD, LANES), F32),
                        pltpu.VMEM((CONV_PAD, 8, LANES), F32)],
        compiler_params=_params(("parallel",)),
    )(proj, proj, dy1, wk)


def _ln_stats(y):
    mu = jnp.mean(y, axis=-1, keepdims=True)
    yc = y - mu
    rs = lax.rsqrt(jnp.mean(yc * yc, axis=-1, keepdims=True) + EPS)
    return yc * rs, rs


def ln_silu(y1, lg, lb, name):
    S, DB = y1.shape

    def fn(t, c, o):
        yh, _ = _ln_stats(t[0][...])
        ln = yh * c[0][...] + c[1][...]
        o[0][...] = (ln * _sigmoid(ln)).astype(BF16)
        return []

    return rowwise(fn, [y1], [lg, lb], [((S, DB), BF16)], ts=_tile(S, ROW_TILE, 16), name=name)[0]


def ln_silu_bwd(dcat, y1, lg, lb, name):
    S, DB = y1.shape
    cb = (dcat.shape[1] - DB) // DB

    def fn(t, c, o):
        yh, rs = _ln_stats(t[1][...])
        gv = c[0][...]
        ln = yh * gv + c[1][...]
        sg = _sigmoid(ln)
        dln = t[0][...] * (sg * (1.0 + ln * (1.0 - sg)))
        dyh = dln * gv
        dy = rs * (dyh - jnp.mean(dyh, axis=-1, keepdims=True) - yh * jnp.mean(dyh * yh, axis=-1, keepdims=True))
        o[0][...] = dy
        return [_colsum(dln * yh), _colsum(dln), _colsum(dy)]

    return rowwise(fn, [(dcat, DB, cb), y1], [lg, lb], [((S, DB), F32)], [(1, DB)] * 3, ts=_tile(S, ROW_TILE, 8), name=name)


def mla_norms(proj, cos2, sinm, qg, kvg, kgr, R, name):
    S = proj.shape[0]

    def fn(t, c, o):
        cq = t[0][:, 0:R]
        ckv = t[0][:, R:2 * R]
        kr = t[0][:, 2 * R:2 * R + ROPE]
        o[0][...] = (cq * _rstd(cq) * c[0][...]).astype(BF16)
        o[1][...] = (ckv * _rstd(ckv) * c[1][...]).astype(BF16)
        o[2][...] = _rope(kr * _rstd(kr) * c[2][...], t[1][...], t[2][...])
        return []

    return rowwise(fn, [proj, cos2, sinm], [qg, kvg, kgr], [((S, R), BF16), ((S, R), BF16), ((S, ROPE), F32)],
                   ts=_tile(S, ROW_TILE, 16), name=name)


def mla_heads(q, kv, kr, cos2, sinm, qg, kg, H, name):
    S = q.shape[0]

    def fn(t, c, o):
        cs, sn = t[3][...], t[4][...]
        krv = t[2][...]
        qgn, qgr, kgn = c[0][:, 0:NOPE], c[0][:, NOPE:QK], c[1][:, 0:NOPE]
        for h in range(H):
            qn = t[0][:, QK * h:QK * h + NOPE]
            qr = t[0][:, QK * h + NOPE:QK * (h + 1)]
            o[0][h, :, 0:NOPE] = (qn * _rstd(qn) * qgn).astype(BF16)
            o[0][h, :, NOPE:QK] = _rope(qr * _rstd(qr) * qgr, cs, sn).astype(BF16)
            kn = t[1][:, (NOPE + VDIM) * h:(NOPE + VDIM) * h + NOPE]
            o[1][h, :, 0:NOPE] = (kn * _rstd(kn) * kgn).astype(BF16)
            o[1][h, :, NOPE:QK] = krv.astype(BF16)
            o[2][h] = t[1][:, (NOPE + VDIM) * h + NOPE:(NOPE + VDIM) * (h + 1)].astype(BF16)
        return []

    return rowwise(fn, [q, kv, kr, cos2, sinm], [qg, kg],
                   [((H, S, QK), BF16), ((H, S, QK), BF16), ((H, S, VDIM), BF16)], ts=_tile(S, HEAD_ROWS_FWD, 16),
                   name=name)


def mla_heads_bwd(dQ, dK, dV, q, kv, cos2, sinm, qg, kg, H, name):
    S = q.shape[0]
    KV = NOPE + VDIM

    def fn(t, c, o):
        cs, sn = t[5][...], t[6][...]
        qgn, qgr, kgn = c[0][:, 0:NOPE], c[0][:, NOPE:QK], c[1][:, 0:NOPE]
        a_qn = jnp.zeros((1, NOPE), F32)
        a_qr = jnp.zeros((1, ROPE), F32)
        a_kn = jnp.zeros((1, NOPE), F32)
        dkr = jnp.zeros((t[0].shape[1], ROPE), F32)
        for h in range(H):
            qn = t[3][:, QK * h:QK * h + NOPE]
            rs = _rstd(qn)
            dx, dg = _rms_bwd(t[0][h, :, 0:NOPE], qn * rs, rs, qgn)
            o[0][:, QK * h:QK * h + NOPE] = dx.astype(BF16)
            a_qn = a_qn + dg
            qr = t[3][:, QK * h + NOPE:QK * (h + 1)]
            rs = _rstd(qr)
            dx, dg = _rms_bwd(_unrope(t[0][h, :, NOPE:QK], cs, sn), qr * rs, rs, qgr)
            o[0][:, QK * h + NOPE:QK * (h + 1)] = dx.astype(BF16)
            a_qr = a_qr + dg
            kn = t[4][:, KV * h:KV * h + NOPE]
            rs = _rstd(kn)
            dx, dg = _rms_bwd(t[1][h, :, 0:NOPE], kn * rs, rs, kgn)
            o[1][:, KV * h:KV * h + NOPE] = dx.astype(BF16)
            a_kn = a_kn + dg
            o[1][:, KV * h + NOPE:KV * (h + 1)] = t[2][h].astype(BF16)
            dkr = dkr + t[1][h, :, NOPE:QK]
        o[2][...] = _unrope(dkr, cs, sn)
        return [a_qn, a_qr, a_kn]

    return rowwise(fn, [dQ, dK, dV, q, kv, cos2, sinm], [qg, kg],
                   [((S, H * QK), BF16), ((S, H * KV), BF16), ((S, ROPE), F32)],
                   [(1, NOPE), (1, ROPE), (1, NOPE)], ts=_tile(S, HEAD_ROWS, 16), name=name)


def mla_norms_bwd(dqn, dkvn, dkr, proj, qg, kvg, kgr, R, name):
    S = proj.shape[0]

    def fn(t, c, o):
        reds = []
        for idx, (lo, hi) in enumerate(((0, R), (R, 2 * R), (2 * R, 2 * R + ROPE))):
            xv = t[3][:, lo:hi]
            rs = _rstd(xv)
            dx, dg = _rms_bwd(t[idx][...], xv * rs, rs, c[idx][...])
            o[0][:, lo:hi] = dx.astype(BF16)
            reds.append(dg)
        return reds

    return rowwise(fn, [dqn, dkvn, dkr, proj], [qg, kvg, kgr], [((S, 2 * R + ROPE), BF16)],
                   [(1, R), (1, R), (1, ROPE)], ts=_tile(S, ROW_TILE, 16), name=name)


def _tri_rows(p, n):
    qi = 0
    for j in range(1, n):
        qi = qi + (p >= j * (j + 1) // 2).astype(jnp.int32)
    return qi, p - (qi * (qi + 1)) // 2


def _tri_cols(p, n):
    ki = 0
    for j in range(1, n):
        ki = ki + (p >= j * n - j * (j - 1) // 2).astype(jnp.int32)
    return ki, ki + p - (ki * n - (ki * (ki - 1)) // 2)


def attn_fwd(Q, K, V, name):
    H, S, _ = Q.shape
    t = _tile(S, ATTN_BLOCK)
    n = S // t
    scale = QK ** -0.5

    rs = _tile(t, ATTN_STRIP, 8)

    def body(q_ref, k_ref, v_ref, o_ref, lse_ref, m_s, l_s, acc, s_scr, p_scr):
        qi, ki = _tri_rows(pl.program_id(1), n)

        @pl.when(ki == 0)
        def _():
            m_s[...] = jnp.full_like(m_s, NEG)
            l_s[...] = jnp.zeros_like(l_s)
            acc[...] = jnp.zeros_like(acc)

        def block(diagonal):
            hw = t // 2
            split = diagonal and hw % rs == 0 and hw % LANES == 0
            if split:
                s_scr[0:hw, 0:hw] = lax.dot_general(q_ref[0:hw, :], k_ref[0:hw, :], _DIMS["nt"],
                                                    preferred_element_type=F32)
                s_scr[hw:t, :] = lax.dot_general(q_ref[hw:t, :], k_ref[...], _DIMS["nt"], preferred_element_type=F32)
            else:
                s_scr[...] = lax.dot_general(q_ref[...], k_ref[...], _DIMS["nt"], preferred_element_type=F32)

            def strip(i, cr):
                r = slice(i * rs, (i + 1) * rs)
                w = hw if (split and (i + 1) * rs <= hw) else t
                s = s_scr[r, 0:w] * scale
                if diagonal:
                    row = i * rs + lax.broadcasted_iota(jnp.int32, (rs, w), 0)
                    s = jnp.where(lax.broadcasted_iota(jnp.int32, (rs, w), 1) <= row, s, NEG)
                m_old = m_s[r, :]
                m_new = jnp.maximum(m_old, jnp.max(s, axis=-1, keepdims=True))
                alpha = jnp.exp(m_old - m_new)
                p = jnp.exp(s - m_new)
                l_s[r, :] = alpha * l_s[r, :] + jnp.sum(p, axis=-1, keepdims=True)
                m_s[r, :] = m_new
                acc[r, :] = alpha * acc[r, :]
                p_scr[r, 0:w] = p.astype(BF16)
                return cr

            for i in range(t // rs):
                strip(i, 0)
            if split:
                acc[0:hw, :] += jnp.dot(p_scr[0:hw, 0:hw], v_ref[0:hw, :], preferred_element_type=F32)
                acc[hw:t, :] += jnp.dot(p_scr[hw:t, :], v_ref[...], preferred_element_type=F32)
            else:
                acc[...] += jnp.dot(p_scr[...], v_ref[...], preferred_element_type=F32)

        @pl.when(ki < qi)
        def _():
            block(False)

        @pl.when(ki == qi)
        def _():
            block(True)

        @pl.when(ki == qi)
        def _():
            o_ref[...] = (acc[...] / l_s[...]).astype(o_ref.dtype)
            lse_ref[...] = m_s[...] + jnp.log(l_s[...])

    return _pcall(
        body, name=name, grid=(H, n * (n + 1) // 2),
        in_specs=[pl.BlockSpec((None, t, QK), lambda h, p: (h, _tri_rows(p, n)[0], 0)),
                  pl.BlockSpec((None, t, QK), lambda h, p: (h, _tri_rows(p, n)[1], 0)),
                  pl.BlockSpec((None, t, VDIM), lambda h, p: (h, _tri_rows(p, n)[1], 0))],
        out_specs=[pl.BlockSpec((t, VDIM), lambda h, p: (_tri_rows(p, n)[0], h)),
                   pl.BlockSpec((None, t, 1), lambda h, p: (h, _tri_rows(p, n)[0], 0))],
        out_shape=[jax.ShapeDtypeStruct((S, H * VDIM), BF16), jax.ShapeDtypeStruct((H, S, 1), F32)],
        scratch_shapes=[pltpu.VMEM((t, 1), F32), pltpu.VMEM((t, 1), F32), pltpu.VMEM((t, VDIM), F32),
                        pltpu.VMEM((t, t), F32), pltpu.VMEM((t, t), BF16)],
        compiler_params=_params(("parallel", "arbitrary")),
    )(Q, K, V)


def attn_bwd(Q, K, V, dO, O, lse, name):
    H, S, _ = Q.shape
    t = _tile(S, ATTN_BLOCK)
    n = S // t
    scale = QK ** -0.5

    rs = _tile(t, ATTN_STRIP, 8)

    def body(q_ref, k_ref, v_ref, do_ref, o_ref, lse_ref, dq_ref, dk_ref, dv_ref, s_scr, dp_scr, p_scr, ds_scr):
        ki, qi = _tri_cols(pl.program_id(1), n)

        @pl.when(pl.program_id(1) == 0)
        def _():
            dq_ref[...] = jnp.zeros_like(dq_ref)

        @pl.when(qi == ki)
        def _():
            dk_ref[...] = jnp.zeros_like(dk_ref)
            dv_ref[...] = jnp.zeros_like(dv_ref)

        def block(diagonal):
            s_scr[...] = lax.dot_general(q_ref[...], k_ref[...], _DIMS["nt"], preferred_element_type=F32)
            dp_scr[...] = lax.dot_general(do_ref[...], v_ref[...], _DIMS["nt"], preferred_element_type=F32)

            def strip(i, cr):
                r = slice(i * rs, (i + 1) * rs)
                s = s_scr[r, :] * scale
                if diagonal:
                    row = i * rs + lax.broadcasted_iota(jnp.int32, (rs, t), 0)
                    s = jnp.where(lax.broadcasted_iota(jnp.int32, (rs, t), 1) <= row, s, NEG)
                p = jnp.exp(s - lse_ref[r, :])
                delta = jnp.sum(do_ref[r, :].astype(F32) * o_ref[r, :].astype(F32), axis=-1, keepdims=True)
                p_scr[r, :] = p.astype(BF16)
                ds_scr[r, :] = (p * (dp_scr[r, :] - delta) * scale).astype(BF16)
                return cr

            for i in range(t // rs):
                strip(i, 0)
            ds = ds_scr[...]
            dv_ref[...] += lax.dot_general(p_scr[...], do_ref[...], _DIMS["tn"], preferred_element_type=F32)
            dk_ref[...] += lax.dot_general(ds, q_ref[...], _DIMS["tn"], preferred_element_type=F32)
            rq = pl.multiple_of(qi * t, t)
            dq_ref[pl.ds(rq, t), :] += jnp.dot(ds, k_ref[...], preferred_element_type=F32)

        @pl.when(qi > ki)
        def _():
            block(False)

        @pl.when(qi == ki)
        def _():
            block(True)

    qmap = lambda h, p: (h, _tri_cols(p, n)[1], 0)
    kmap = lambda h, p: (h, _tri_cols(p, n)[0], 0)
    return _pcall(
        body, name=name, grid=(H, n * (n + 1) // 2),
        in_specs=[pl.BlockSpec((None, t, QK), qmap),
                  pl.BlockSpec((None, t, QK), kmap),
                  pl.BlockSpec((None, t, VDIM), kmap),
                  pl.BlockSpec((t, VDIM), lambda h, p: (_tri_cols(p, n)[1], h)),
                  pl.BlockSpec((t, VDIM), lambda h, p: (_tri_cols(p, n)[1], h)),
                  pl.BlockSpec((None, t, 1), qmap)],
        out_specs=[pl.BlockSpec((None, S, QK), lambda h, p: (h, 0, 0)),
                   pl.BlockSpec((None, t, QK), kmap),
                   pl.BlockSpec((None, t, VDIM), kmap)],
        out_shape=[jax.ShapeDtypeStruct((H, S, QK), F32), jax.ShapeDtypeStruct((H, S, QK), F32),
                   jax.ShapeDtypeStruct((H, S, VDIM), F32)],
        scratch_shapes=[pltpu.VMEM((t, t), F32), pltpu.VMEM((t, t), F32), pltpu.VMEM((t, t), BF16),
                        pltpu.VMEM((t, t), BF16)],
        compiler_params=_params(("parallel", "arbitrary")),
    )(Q, K, V, dO, O, lse)


def adamw(w, m, v, g, *, row0, name, into=None):
    P, rows, C = g.shape
    tr = _tile(rows, max(16, 131072 // C), 16)
    off = row0 // tr
    assert row0 % tr == 0
    bc1 = 1.0 - ADAM_B1 ** ADAM_STEP
    bc2 = 1.0 - ADAM_B2 ** ADAM_STEP
    chained = into is not None and into is not True

    def body(w_ref, m_ref, v_ref, g_ref, *rest):
        go_ref, d_ref, mo_ref, vo_ref = rest[-4:]
        gs = g_ref[0].astype(F32)
        for p in range(1, P):
            gs = gs + g_ref[p].astype(F32)
        wv = w_ref[...]
        mn = ADAM_B1 * m_ref[...] + (1.0 - ADAM_B1) * gs
        vn = ADAM_B2 * v_ref[...] + (1.0 - ADAM_B2) * (gs * gs)
        go_ref[...] = gs
        mo_ref[...] = mn
        vo_ref[...] = vn
        d_ref[...] = -ADAM_LR * ((mn / bc1) / (jnp.sqrt(vn / bc2) + ADAM_EPS) + ADAM_WD * wv)

    wspec = pl.BlockSpec((tr, C), lambda i: (i + off, 0))
    ospec = wspec if into is not None else pl.BlockSpec((tr, C), lambda i: (i, 0))
    out_rows = w.shape[0] if into is not None else rows
    return _pcall(
        body, name=name, grid=(rows // tr,),
        in_specs=[wspec, wspec, wspec, pl.BlockSpec((P, tr, C), lambda i: (0, i, 0))] + ([_ANY] * 4 if chained else []),
        out_specs=[ospec] * 4, out_shape=[jax.ShapeDtypeStruct((out_rows, C), F32)] * 4,
        input_output_aliases={4 + q: q for q in range(4)} if chained else {},
        compiler_params=_params(("parallel",)),
    )(w, m, v, g, *(into if chained else ()))


def _coords():
    return lax.axis_index("x"), lax.axis_index("y"), lax.axis_index("c")


def _me():
    x, y, c = _coords()
    return 4 * x + 2 * y + c


_ANY = pl.BlockSpec(memory_space=pl.ANY)


def all_gather(items, name):
    n = len(items)
    blks = [a.shape if idx is None else a.shape[1:] for a, idx in items]

    def body(*refs):
        ins, outs = refs[:n], refs[n:2 * n]
        send, recv, lsem = refs[2 * n:]
        x, y, c = _coords()
        me, sib = (x, y, c), (x, y, 1 - c)
        chips = [(1 - x, y), (x, 1 - y), (1 - x, 1 - y)]

        def src(i):
            return ins[i] if items[i][1] is None else ins[i].at[items[i][1]]

        def slot(i, p):
            return outs[i].at[4 * p[0] + 2 * p[1] + p[2]]

        def cp(i, k, block, to, s=None):
            return pltpu.make_async_remote_copy(
                src_ref=slot(i, block) if s is None else s, dst_ref=slot(i, block),
                send_sem=send.at[7 * i + k], recv_sem=recv.at[7 * i + k], device_id=to, device_id_type=MESH)

        mine = [pltpu.make_async_copy(src(i), slot(i, me), lsem.at[i]) for i in range(n)]
        for m_ in mine:
            m_.start()
        first = []
        for i in range(n):
            first.append(cp(i, 0, me, sib, src(i)))
            first += [cp(i, 1 + j, me, (*chip, c), src(i)) for j, chip in enumerate(chips)]
        for f in first:
            f.start()
        passed = []
        for j, chip in enumerate(chips):
            for i in range(n):
                cp(i, 1 + j, (*chip, c), me).wait_recv()
                p_ = cp(i, 4 + j, (*chip, c), sib)
                p_.start()
                passed.append(p_)
        for i in range(n):
            cp(i, 0, sib, me).wait_recv()
            for j, chip in enumerate(chips):
                cp(i, 4 + j, (*chip, 1 - c), me).wait_recv()
        for f in first + passed:
            f.wait_send()
        for m_ in mine:
            m_.wait()

    res = _pcall(
        body, name=name, in_specs=[_ANY] * n, out_specs=[_ANY] * n,
        out_shape=[jax.ShapeDtypeStruct((NDEV,) + tuple(b), a.dtype) for b, (a, _) in zip(blks, items)],
        scratch_shapes=[pltpu.SemaphoreType.DMA((7 * n,)), pltpu.SemaphoreType.DMA((7 * n,)),
                        pltpu.SemaphoreType.DMA((n,))],
    )(*[a for a, _ in items])
    return list(res)


_HBM = pl.BlockSpec(memory_space=pltpu.HBM)
_SEM = pl.BlockSpec(memory_space=pltpu.SEMAPHORE)
_EFFECT = pltpu.SideEffectType.DATAFLOW_SIDE_EFFECTING


def _xchg_copy(src_ref, land_ref, send, recv, r, scatter, at_peer):
    x, y, c = _coords()
    px = jnp.bitwise_xor(x, (r >> 2) & 1)
    py = jnp.bitwise_xor(y, (r >> 1) & 1)
    pc = jnp.bitwise_xor(c, r & 1)
    p_i = 4 * px + 2 * py + pc
    me_i = 4 * x + 2 * y + c
    return pltpu.make_async_remote_copy(
        src_ref=src_ref.at[p_i] if scatter else src_ref, dst_ref=land_ref.at[p_i if at_peer else me_i],
        send_sem=send.at[r - 1], recv_sem=recv.at[r - 1], device_id=(px, py, pc), device_id_type=MESH)


def _phase(body, name, bufs, sems_in=(), new_sems=(), after=None, token=False):
    nb, ns, nn = len(bufs), len(sems_in), len(new_sems)

    def wrapped(*refs):
        outs = refs[nb + ns + (after is not None):]
        body(refs[:nb], refs[nb:nb + ns], outs[:nn])
        if token:
            outs[nn + nb][...] = jnp.zeros_like(outs[nn + nb])

    res = _pcall(
        wrapped, name=name,
        out_shape=tuple([pltpu.SemaphoreType.DMA((k,)) for k in new_sems] + [pltpu.HBM(a.shape, a.dtype) for a in bufs]
                        + ([jax.ShapeDtypeStruct((8, LANES), F32)] if token else [])),
        in_specs=[_HBM] * nb + [_SEM] * ns + ([] if after is None else [_ANY]),
        out_specs=tuple([_SEM] * nn + [_HBM] * nb + ([pl.BlockSpec(memory_space=pltpu.VMEM)] if token else [])),
        input_output_aliases={i: nn + i for i in range(nb)},
        compiler_params=pltpu.CompilerParams(has_side_effects=_EFFECT),
    )(*[pltpu.with_memory_space_constraint(a, pltpu.HBM) for a in bufs], *sems_in, *([] if after is None else [after]))
    return list(res[nn:nn + nb]), list(res[:nn]), (res[nn + nb][0:1, 0:1] if token else None)


def scatter_start(srcs, name, after=None):
    n = len(srcs)
    lands = [lax.empty(s.shape, s.dtype) for s in srcs]

    def body(b, taken, new):
        me_i = _me()
        for i in range(n):
            pltpu.make_async_copy(b[i].at[me_i], b[n + i].at[me_i], new[3 * i + 2].at[0]).start()
            for r in range(1, NDEV):
                _xchg_copy(b[i], b[n + i], new[3 * i], new[3 * i + 1], r, True, False).start()

    bufs, sems, tok = _phase(body, name, list(srcs) + lands, new_sems=[NDEV - 1, NDEV - 1, 1] * n, after=after, token=True)
    return (bufs, sems), tok


def scatter_wait(handle, after, name):
    bufs, sems = handle
    n = len(bufs) // 2

    def body(b, taken, new):
        me_i = _me()
        for i in range(n):
            pltpu.make_async_copy(b[i].at[me_i], b[n + i].at[me_i], taken[3 * i + 2].at[0]).wait()
            for r in range(1, NDEV):
                cp = _xchg_copy(b[i], b[n + i], taken[3 * i], taken[3 * i + 1], r, True, True)
                cp.wait_send()
                cp.wait_recv()

    return _phase(body, name, bufs, sems_in=sems, after=after)[0][n:]


def _gather_peers():
    x, y, c = _coords()
    return (x, y, c), (x, y, 1 - c), [(1 - x, y), (x, 1 - y), (1 - x, 1 - y)]


def _row(p):
    return 4 * p[0] + 2 * p[1] + p[2]


def _gcopy(src_ref, land_ref, send, recv, k, block, to):
    return pltpu.make_async_remote_copy(
        src_ref=land_ref.at[_row(block)] if src_ref is None else src_ref, dst_ref=land_ref.at[_row(block)],
        send_sem=send.at[k], recv_sem=recv.at[k], device_id=to, device_id_type=MESH)


def gather_start(srcs, name, after=None):
    n = len(srcs)
    lands = [lax.empty((NDEV,) + s.shape, s.dtype) for s in srcs]

    def body(b, taken, new):
        me, sib, chips = _gather_peers()
        for i in range(n):
            send, recv = new[3 * i], new[3 * i + 1]
            pltpu.make_async_copy(b[i], b[n + i].at[_row(me)], new[3 * i + 2].at[0]).start()
            for j, chip in enumerate(chips):
                _gcopy(b[i], b[n + i], send, recv, 1 + j, me, (*chip, me[2])).start()
            _gcopy(b[i], b[n + i], send, recv, 0, me, sib).start()

    bufs, sems, tok = _phase(body, name, list(srcs) + lands, new_sems=[4, 4, 1] * n, after=after, token=True)
    return (bufs, sems), tok


def gather_mid(handle, after, name):
    bufs, sems = handle
    n = len(bufs) // 2

    def body(b, taken, new):
        me, sib, chips = _gather_peers()
        for j, chip in enumerate(chips):
            for i in range(n):
                _gcopy(b[i], b[n + i], taken[3 * i], taken[3 * i + 1], 1 + j, (*chip, me[2]), me).wait_recv()
                _gcopy(None, b[n + i], new[2 * i], new[2 * i + 1], j, (*chip, me[2]), sib).start()
        for i in range(n):
            send, recv = taken[3 * i], taken[3 * i + 1]
            _gcopy(b[i], b[n + i], send, recv, 0, sib, me).wait_recv()
            for k in range(4):
                _gcopy(b[i], b[n + i], send, recv, k, me, sib).wait_send()
            pltpu.make_async_copy(b[i], b[n + i].at[_row(me)], taken[3 * i + 2].at[0]).wait()

    bufs, new, tok = _phase(body, name, bufs, sems_in=sems, new_sems=[3, 3] * n, after=after, token=True)
    return (bufs, new), tok


def gather_wait(handle, after, name):
    bufs, sems = handle
    n = len(bufs) // 2

    def body(b, taken, new):
        me, sib, chips = _gather_peers()
        for i in range(n):
            for j, chip in enumerate(chips):
                _gcopy(None, b[n + i], taken[2 * i], taken[2 * i + 1], j, (*chip, me[2]), sib).wait_send()
                _gcopy(None, b[n + i], taken[2 * i], taken[2 * i + 1], j, (*chip, 1 - me[2]), me).wait_recv()

    return _phase(body, name, bufs, sems_in=sems, after=after)[0][n:]


_PACK_ALIGN = 8 * LANES


def _pack(arrs, aligned=False):
    parts = []
    for a in arrs:
        f = a.reshape(-1).astype(F32)
        pad = (-f.shape[0]) % _PACK_ALIGN if aligned else 0
        parts.append(jnp.pad(f, (0, pad)) if pad else f)
    flat = jnp.concatenate(parts)
    pad = (-flat.shape[0]) % _PACK_ALIGN
    return (jnp.pad(flat, (0, pad)) if pad else flat).reshape(-1, LANES)


def _unpack(p, shapes, lead=(), aligned=False):
    nl = len(lead)
    flat = p.reshape(lead + (-1,))
    out, off = [], 0
    for shp in shapes:
        n = 1
        for d in shp:
            n *= d
        out.append(lax.slice_in_dim(flat, off, off + n, axis=nl).reshape(lead + tuple(shp)))
        off += n + ((-n) % _PACK_ALIGN if aligned else 0)
    return out


def _shard_cols(a, me, width, axis):
    return lax.dynamic_slice_in_dim(a, me * width, width, axis=axis)


def kernel(x, c, norm1_g, norm2_g, ada_w, ada_b, mlp_w1, mlp_w2, ab_w_in, sgu_norm_g, sgu_w, sgu_b, conv_w, conv_b, conv_ln_g, conv_ln_b, ab_w_out, mla_w_in, mla_q_norm_g, mla_kv_norm_g, mla_w_uq, mla_w_ukv, mla_q_head_g, mla_k_head_g, mla_w_out, loss_target, m_norm1_g, m_norm2_g, m_ada_w, m_ada_b, m_mlp_w1, m_mlp_w2, m_ab_w_in, m_sgu_norm_g, m_sgu_w, m_sgu_b, m_conv_w, m_conv_b, m_conv_ln_g, m_conv_ln_b, m_ab_w_out, m_mla_w_in, m_mla_q_norm_g, m_mla_kv_norm_g, m_mla_w_uq, m_mla_w_ukv, m_mla_q_head_g, m_mla_k_head_g, m_mla_w_out, v_norm1_g, v_norm2_g, v_ada_w, v_ada_b, v_mlp_w1, v_mlp_w2, v_ab_w_in, v_sgu_norm_g, v_sgu_w, v_sgu_b, v_conv_w, v_conv_b, v_conv_ln_g, v_conv_ln_b, v_ab_w_out, v_mla_w_in, v_mla_q_norm_g, v_mla_kv_norm_g, v_mla_w_uq, v_mla_w_ukv, v_mla_q_head_g, v_mla_k_head_g, v_mla_w_out):
    W = dict(norm1_g=norm1_g, norm2_g=norm2_g, ada_w=ada_w, ada_b=ada_b, mlp_w1=mlp_w1, mlp_w2=mlp_w2, ab_w_in=ab_w_in,
             sgu_norm_g=sgu_norm_g, sgu_w=sgu_w, sgu_b=sgu_b, conv_w=conv_w, conv_b=conv_b, conv_ln_g=conv_ln_g,
             conv_ln_b=conv_ln_b, ab_w_out=ab_w_out, mla_w_in=mla_w_in, mla_q_norm_g=mla_q_norm_g,
             mla_kv_norm_g=mla_kv_norm_g, mla_w_uq=mla_w_uq, mla_w_ukv=mla_w_ukv, mla_q_head_g=mla_q_head_g,
             mla_k_head_g=mla_k_head_g, mla_w_out=mla_w_out)
    M = dict(norm1_g=m_norm1_g, norm2_g=m_norm2_g, ada_w=m_ada_w, ada_b=m_ada_b, mlp_w1=m_mlp_w1, mlp_w2=m_mlp_w2,
             ab_w_in=m_ab_w_in, sgu_norm_g=m_sgu_norm_g, sgu_w=m_sgu_w, sgu_b=m_sgu_b, conv_w=m_conv_w, conv_b=m_conv_b,
             conv_ln_g=m_conv_ln_g, conv_ln_b=m_conv_ln_b, ab_w_out=m_ab_w_out, mla_w_in=m_mla_w_in,
             mla_q_norm_g=m_mla_q_norm_g, mla_kv_norm_g=m_mla_kv_norm_g, mla_w_uq=m_mla_w_uq, mla_w_ukv=m_mla_w_ukv,
             mla_q_head_g=m_mla_q_head_g, mla_k_head_g=m_mla_k_head_g, mla_w_out=m_mla_w_out)
    V = dict(norm1_g=v_norm1_g, norm2_g=v_norm2_g, ada_w=v_ada_w, ada_b=v_ada_b, mlp_w1=v_mlp_w1, mlp_w2=v_mlp_w2,
             ab_w_in=v_ab_w_in, sgu_norm_g=v_sgu_norm_g, sgu_w=v_sgu_w, sgu_b=v_sgu_b, conv_w=v_conv_w, conv_b=v_conv_b,
             conv_ln_g=v_conv_ln_g, conv_ln_b=v_conv_ln_b, ab_w_out=v_ab_w_out, mla_w_in=v_mla_w_in,
             mla_q_norm_g=v_mla_q_norm_g, mla_kv_norm_g=v_mla_kv_norm_g, mla_w_uq=v_mla_w_uq, mla_w_ukv=v_mla_w_ukv,
             mla_q_head_g=v_mla_q_head_g, mla_k_head_g=v_mla_k_head_g, mla_w_out=v_mla_w_out)
    ORDER = list(W)

    S, D = x.shape[1], x.shape[2]
    L, NE, NO = norm1_g.shape[0], ab_w_in.shape[0], mla_w_in.shape[0]
    DA = D // 2
    DB = D - DA
    G = DA // GROUP
    R = NDEV * mla_q_norm_g.shape[1]
    H = NDEV * mla_w_uq.shape[2] // QK
    AW = ada_w.shape[2]
    CB = conv_w.shape[2]
    me = _me()
    xs, tgt = x[0], loss_target[0]

    BIG_EVEN = ("mlp_w1", "mlp_w2", "ab_w_in", "ab_w_out")
    BIG_ODD = ("mlp_w1", "mlp_w2", "mla_w_in", "mla_w_uq", "mla_w_ukv", "mla_w_out")
    BIG = ("mlp_w1", "mlp_w2", "ab_w_in", "ab_w_out", "mla_w_in", "mla_w_uq", "mla_w_ukv", "mla_w_out")
    COL_SHARDED = ("mlp_w1", "ab_w_in", "mla_w_uq", "mla_w_ukv")
    MLP_W = ("mlp_w1", "mlp_w2")
    mixer_w = lambda l: ("ab_w_in", "ab_w_out") if l % 2 == 0 else ("mla_w_in", "mla_w_uq", "mla_w_ukv", "mla_w_out")
    widx = lambda k, l: l if k in MLP_W else l // 2

    small_in = [c, mla_q_norm_g, mla_kv_norm_g, conv_w]
    sg = all_gather([(_pack(small_in, True), None)], "gather_small")[0]
    c_all, qng_all, kvng_all, cw_all = _unpack(sg, [a.shape for a in small_in], (NDEV,), True)
    c_all = c_all.reshape(NDEV, D)
    qng_full = jnp.transpose(qng_all, (1, 0, 2)).reshape(NO, 1, R)
    kvng_full = jnp.transpose(kvng_all, (1, 0, 2)).reshape(NO, 1, R)
    cw_full = jnp.transpose(cw_all, (1, 2, 0, 3)).reshape(NE, CONV_W, DB)
    cw_pad = jnp.pad(cw_full, ((0, 0), (0, CONV_PAD - CONV_W), (0, 0)))

    def silu_fn(t, c_, o):
        v_ = t[0][...]
        o[0][...] = v_ * _sigmoid(v_)
        return []

    c_act = rowwise(silu_fn, [c_all], outs=[((NDEV, D), F32)], ts=NDEV, name="silu_c")[0]
    bias_cols = _shard_cols(ada_b, me, AW, 1).reshape(1, L * AW)
    mod_cols = mm(c_act, ada_w, "nn", name="ada_fwd", b_blocked=True, rowvecs=[bias_cols],
                  epi=lambda acc, b_: (acc + b_,), tm=NDEV, tn=768)
    mod_all = all_gather([(mod_cols, None)], "gather_mod")[0]
    mod = lax.dynamic_index_in_dim(mod_all, me, axis=1, keepdims=False)
    mod = jnp.transpose(mod.reshape(NDEV, L, AW), (1, 0, 2)).reshape(L, 6, 1, D)

    wnames = lambda l, what: mixer_w(l) if what == "mix" else MLP_W
    g_first, g_second = {}, {}
    tok_sum = jnp.zeros((1, 1), F32)
    for l in range(L):
        for what in ("mix", "mlp"):
            srcs = [cast_bf16(W[k], widx(k, l), "cast_%s_l%d" % (k, l)) for k in wnames(l, what)]
            g_first[l, what], tok = gather_start(srcs, "gather_start_%s_l%d" % (what, l), after=mod_all)
            tok_sum = tok_sum + tok
    mod = mod + tok_sum

    def pass_on(l, what, after):
        g_second[l, what], tok = gather_mid(g_first[l, what], after, "gather_mid_%s_l%d" % (what, l))
        return tok

    def wait_weights(l, what, after):
        lands = gather_wait(g_second[l, what], after, "gather_wait_%s_l%d" % (what, l))
        return {k: (ld if k in COL_SHARDED else ld.reshape(NDEV * ld.shape[1], ld.shape[2]))
                for k, ld in zip(wnames(l, what), lands)}

    mod = mod + pass_on(0, "mix", mod)

    pos = jnp.arange(S, dtype=F32)
    inv = ROPE_THETA ** (-jnp.arange(0, ROPE, 2, dtype=F32) / ROPE)
    ang = pos[:, None] * inv[None, :]
    cos2 = jnp.concatenate([jnp.cos(ang), jnp.cos(ang)], axis=1)
    sinm = jnp.concatenate([-jnp.sin(ang), jnp.sin(ang)], axis=1)

    residual = lambda acc, xr, gt: (acc, xr + gt * acc)

    saved = []
    xc = xs
    for l in range(L):
        sh1, sc1, g1, sh2, sc2, g2 = [mod[l, k] for k in range(6)]
        tag = "_l%d" % l
        sv = dict(x0=xc)
        h, sv["rstd1"] = prenorm(xc, norm1_g[l][None], sc1, sh1, "prenorm1" + tag)
        sv["h"] = h
        wl = wait_weights(l, "mix", h)
        if l % 2 == 0:
            e = l // 2
            ng = sgu_norm_g[e].reshape(1, DA)
            bcol = sgu_b[e][:, :, None]
            proj = mm(h, wl["ab_w_in"], "nn", name="ab_in" + tag, b_blocked=True)
            out_a = sgu_fwd(proj, ng, sgu_w[e], bcol, DA, "sgu_fwd" + tag)
            y1 = conv_fwd(proj, cw_pad[e], conv_b[e][None], DA, DB, "conv_fwd" + tag)
            out_b = ln_silu(y1, conv_ln_g[e][None], conv_ln_b[e][None], "ln_silu" + tag)
            cat = jnp.concatenate([out_a, out_b], axis=1)
            sv.update(proj=proj, y1=y1, cat=cat)
            mixb, x1 = mm(cat, wl["ab_w_out"], "nn", name="ab_out" + tag, out_dtypes=(BF16, F32), epi=residual,
                          extras=[xc], rowvecs=[g1])
        else:
            o_ = l // 2
            proj = mm(h, wl["mla_w_in"], "nn", name="mla_in" + tag)
            kgr = mla_k_head_g[o_][None, NOPE:QK]
            qn, kvn, kr = mla_norms(proj, cos2, sinm, qng_full[o_], kvng_full[o_], kgr, R, "mla_norms" + tag)
            q = mm(qn, wl["mla_w_uq"], "nn", name="mla_uq" + tag, b_blocked=True)
            kv = mm(kvn, wl["mla_w_ukv"], "nn", name="mla_ukv" + tag, b_blocked=True)
            Qh, Kh, Vh = mla_heads(q, kv, kr, cos2, sinm, mla_q_head_g[o_][None], mla_k_head_g[o_][None], H,
                                   "mla_heads" + tag)
            att, lse = attn_fwd(Qh, Kh, Vh, "attn_fwd" + tag)
            sv.update(proj=proj, qn=qn, kvn=kvn, q=q, kv=kv, Qh=Qh, Kh=Kh, Vh=Vh, att=att, lse=lse)
            mixb, x1 = mm(att, wl["mla_w_out"], "nn", name="mla_out" + tag, out_dtypes=(BF16, F32), epi=residual,
                          extras=[xc], rowvecs=[g1])
        sv.update(mixb=mixb, x1=x1)
        h2, sv["rstd2"] = prenorm(x1, norm2_g[l][None], sc2 + pass_on(l, "mlp", x1), sh2, "prenorm2" + tag)
        wl.update(wait_weights(l, "mlp", h2))
        sv["w"] = wl
        z, act = mm(h2, wl["mlp_w1"], "nn", name="mlp_up" + tag, b_blocked=True, out_dtypes=(BF16, BF16),
                    epi=lambda acc: (acc, jnp.square(jnp.maximum(acc, 0.0))), tm=MLP_ROWS)
        g2t = g2 + pass_on(l + 1, "mix", z) if l + 1 < L else g2
        yb, xc = mm(act, wl["mlp_w2"], "nn", name="mlp_down" + tag, out_dtypes=(BF16, F32), epi=residual,
                    extras=[x1], rowvecs=[g2t])
        sv.update(h2=h2, z=z, act=act, yb=yb)
        saved.append(sv)

    dx, dy, loss_cols, dgate2 = loss_grad(xc, tgt, saved[L - 1]["yb"], mod[L - 1, 5], "loss")
    loss = lax.psum(0.5 / D * jnp.sum(loss_cols), ("x", "y", "c"))

    big_out = {}
    sm = {k: [None] * W[k].shape[0] for k in ("norm1_g", "norm2_g", "sgu_norm_g", "sgu_w", "sgu_b", "conv_b", "conv_ln_g",
                                               "conv_ln_b", "mla_q_head_g", "mla_k_head_g", "mla_q_norm_g",
                                               "mla_kv_norm_g", "conv_w")}
    dmod = [None] * L
    flat2 = {k: W[k].reshape(-1, W[k].shape[2]) for k in BIG}
    flat2m = {k: M[k].reshape(-1, W[k].shape[2]) for k in BIG}
    flat2v = {k: V[k].reshape(-1, W[k].shape[2]) for k in BIG}

    def send_grads(gr, what, tag, after=None):
        names = list(gr)
        blocks = [gr[k] if k in COL_SHARDED else gr[k].reshape(NDEV, gr[k].shape[0] // NDEV, gr[k].shape[1])
                  for k in names]
        handle, tok = scatter_start(blocks, "scatter_start_%s%s" % (what, tag), after=after)
        return (names, handle, what, tag), tok

    rep = ("norm1_g", "norm2_g", "sgu_norm_g", "sgu_w", "sgu_b", "conv_b", "conv_ln_g", "conv_ln_b", "mla_q_head_g",
           "mla_k_head_g")
    part_full = {"mla_q_norm_g": (NO, R), "mla_kv_norm_g": (NO, R), "conv_w": (NE, CONV_W, DB)}
    small = ["ada_b"] + list(rep) + list(part_full)
    shapes = [(L, 6 * D)] + [W[k].shape for k in rep] + list(part_full.values())

    def small_gather():
        parts = [jnp.stack(dmod).reshape(L, 6 * D)] + [jnp.stack(sm[k]).reshape(s_) for k, s_ in zip(small[1:], shapes[1:])]
        return all_gather([(_pack(parts), None)], "gather_smallgrads")[0]

    def scatter_finish(pending, after, l):
        names, handle, what, tag = pending
        landed = scatter_wait(handle, after, "scatter_wait_%s%s" % (what, tag))
        for k, land in zip(names, landed):
            li = widx(k, l)
            big_out[k] = adamw(flat2[k], flat2m[k], flat2v[k], land, row0=li * W[k].shape[1],
                               name="adamw_%s%s" % (k, tag), into=big_out.get(k, True))

    pend_mix, tok_mix = None, None
    for l in reversed(range(L)):
        sh1, sc1, g1, sh2, sc2, g2 = [mod[l, k] for k in range(6)]
        sv = saved[l]
        wl = sv["w"]
        tag = "_l%d" % l
        gr = {}
        dw2 = mm(sv["act"], dy, "tn", name="mlp_down_dw" + tag, out_dtypes=(BF16,), tm=MLP_ROWS)
        pend_w2, tok_w2 = send_grads({"mlp_w2": dw2}, "w2", tag)
        dz = mm(dy, wl["mlp_w2"], "nt", name="mlp_down_dx" + tag, out_dtypes=(BF16,), extras=[sv["z"]],
                rowvecs=[jnp.zeros((1, sv["z"].shape[1]), F32) + tok_w2],
                epi=lambda acc, z_, t_: (acc * (2.0 * jnp.maximum(z_.astype(F32), 0.0)) + t_,), tm=MLP_ROWS)
        dh2 = mm(dz, wl["mlp_w1"], "nt", name="mlp_up_dx" + tag, b_blocked=True)
        dw1 = mm(sv["h2"], dz, "tn", name="mlp_up_dw" + tag, out_dtypes=(BF16,), out_blocked=NDEV, tm=MLP_ROWS)
        pend_mlp, tok_mlp = send_grads({"mlp_w1": dw1}, "w1", tag)
        dx1, dsc2, dsh2, sm["norm2_g"][l], dmix, dgate1 = norm_bwd(
            dh2, sv["x1"], sv["rstd2"], dx, norm2_g[l][None], sc2 + tok_mlp, "norm2_bwd" + tag, gated=(sv["mixb"], g1))
        if pend_mix is not None:
            scatter_finish(pend_mix, dx1, l + 1)
        if l % 2 == 0:
            e = l // 2
            ng = sgu_norm_g[e].reshape(1, DA)
            bcol = sgu_b[e][:, :, None]
            dcat = mm(dmix, wl["ab_w_out"], "nt", name="ab_out_dx" + tag)
            gr["ab_w_out"] = mm(sv["cat"], dmix, "tn", name="ab_out_dw" + tag, out_dtypes=(BF16,))
            dy1, sm["conv_ln_g"][e], sm["conv_ln_b"][e], sm["conv_b"][e] = ln_silu_bwd(
                dcat, sv["y1"], conv_ln_g[e][None], conv_ln_b[e][None], "ln_silu_bwd" + tag)
            da, dg_, dwc = conv_bwd(sv["proj"], dy1, cw_pad[e], DA, DB, "conv_bwd" + tag)
            sm["conv_w"][e] = dwc[:CONV_W]
            duv, dsw, dsb, dsng = sgu_bwd(sv["proj"], dcat, ng, sgu_w[e], bcol, DA, "sgu_bwd" + tag)
            sm["sgu_w"][e], sm["sgu_b"][e], sm["sgu_norm_g"][e] = dsw, dsb, dsng
            dproj = jnp.concatenate([duv, da, dg_], axis=1)
            dh = mm(dproj, wl["ab_w_in"], "nt", name="ab_in_dx" + tag, b_blocked=True)
            gr["ab_w_in"] = mm(sv["h"], dproj, "tn", name="ab_in_dw" + tag, out_dtypes=(BF16,), out_blocked=NDEV)
        else:
            o_ = l // 2
            kgr = mla_k_head_g[o_][None, NOPE:QK]
            dO = mm(dmix, wl["mla_w_out"], "nt", name="mla_out_dx" + tag, out_dtypes=(BF16,))
            gr["mla_w_out"] = mm(sv["att"], dmix, "tn", name="mla_out_dw" + tag, out_dtypes=(BF16,))
            dQ, dK, dV = attn_bwd(sv["Qh"], sv["Kh"], sv["Vh"], dO, sv["att"], sv["lse"], "attn_bwd" + tag)
            dq_pre, dkv_pre, dkr, dqgn, dqgr, dkgn = mla_heads_bwd(
                dQ, dK, dV, sv["q"], sv["kv"], cos2, sinm, mla_q_head_g[o_][None], mla_k_head_g[o_][None], H,
                "mla_heads_bwd" + tag)
            dqn = mm(dq_pre, wl["mla_w_uq"], "nt", name="mla_uq_dx" + tag, b_blocked=True)
            gr["mla_w_uq"] = mm(sv["qn"], dq_pre, "tn", name="mla_uq_dw" + tag, out_dtypes=(BF16,), out_blocked=NDEV)
            dkvn = mm(dkv_pre, wl["mla_w_ukv"], "nt", name="mla_ukv_dx" + tag, b_blocked=True)
            gr["mla_w_ukv"] = mm(sv["kvn"], dkv_pre, "tn", name="mla_ukv_dw" + tag, out_dtypes=(BF16,),
                                 out_blocked=NDEV)
            dproj, sm["mla_q_norm_g"][o_], sm["mla_kv_norm_g"][o_], dkgr = mla_norms_bwd(
                dqn, dkvn, dkr, sv["proj"], qng_full[o_], kvng_full[o_], kgr, R, "mla_norms_bwd" + tag)
            sm["mla_q_head_g"][o_] = jnp.concatenate([dqgn, dqgr], axis=1)
            sm["mla_k_head_g"][o_] = jnp.concatenate([dkgn, dkgr], axis=1)
            dh = mm(dproj, wl["mla_w_in"], "nt", name="mla_in_dx" + tag)
            gr["mla_w_in"] = mm(sv["h"], dproj, "tn", name="mla_in_dw" + tag, out_dtypes=(BF16,))
        if l > 0:
            pend_mix, tok_mix = send_grads(gr, "mix", tag)
            dx, dsc1, dsh1, sm["norm1_g"][l], dy, dgate2_below = norm_bwd(
                dh, sv["x0"], sv["rstd1"], dx1, norm1_g[l][None], sc1 + tok_mix, "norm1_bwd" + tag,
                gated=(saved[l - 1]["yb"], mod[l - 1, 5]))
        else:
            dx, dsc1, dsh1, sm["norm1_g"][l] = norm_bwd(dh, sv["x0"], sv["rstd1"], dx1, norm1_g[l][None], sc1,
                                                        "norm1_bwd" + tag)
        dmod[l] = jnp.concatenate([dsh1, dsc1, dgate1, dsh2, dsc2, dgate2], axis=1)
        if l > 0:
            dgate2 = dgate2_below
        else:
            gp = small_gather()
            pend_mix, tok_mix = send_grads(gr, "mix", tag, after=gp)
        scatter_finish(pend_w2, dx, l)
        scatter_finish(pend_mlp, dx, l)

    rows_p = gp.shape[1]
    per = 6 * D // LANES
    dm = lax.slice_in_dim(gp, 0, L * per, axis=1).reshape(NDEV, L, per, LANES)
    dmod_cols = lax.dynamic_slice_in_dim(dm, me * (AW // LANES), AW // LANES, axis=2).reshape(NDEV, L * AW) + tok_mix

    def sum_fn(t, c_, o):
        acc = t[0][0]
        for s_ in range(1, NDEV):
            acc = acc + t[0][s_]
        o[0][...] = acc
        return []

    gsummed = rowwise(sum_fn, [gp], outs=[((rows_p, LANES), F32)], ts=_tile(rows_p, 256, 8), name="sum_smallgrads")[0]
    gsum = dict(zip(small, _unpack(gsummed, shapes)))
    gsum["mla_q_norm_g"] = _shard_cols(gsum["mla_q_norm_g"], me, R // NDEV, 1)
    gsum["mla_kv_norm_g"] = _shard_cols(gsum["mla_kv_norm_g"], me, R // NDEV, 1)
    gsum["conv_w"] = _shard_cols(gsum["conv_w"], me, CB, 2)
    sm_shapes = [W[k].shape for k in small]
    sres = adamw(_pack([W[k] for k in small]), _pack([M[k] for k in small]), _pack([V[k] for k in small]),
                 (_pack([gsum[k] for k in small]) + tok_mix)[None], row0=0, name="adamw_small")
    small_out = {k: vals for k, vals in zip(small, zip(*[_unpack(r_, sm_shapes) for r_ in sres]))}

    g_ada = mm(c_act, dmod_cols, "tn", name="ada_dw", out_blocked=L, tm=1024, tn=768, tk=NDEV, cast=None,
               precision=lax.Precision.HIGHEST)
    ada_out = adamw(ada_w.reshape(L * D, AW), m_ada_w.reshape(L * D, AW), v_ada_w.reshape(L * D, AW),
                    g_ada.reshape(1, L * D, AW), row0=0, name="adamw_ada_w")
    scatter_finish(pend_mix, ada_out[1], 0)
    ada_out = [a.reshape(L, D, AW) for a in ada_out]

    def result(k, which):
        if k == "ada_w":
            return ada_out[which]
        if k in BIG:
            return big_out[k][which].reshape(W[k].shape)
        return small_out[k][which]

    outs = [loss, dx[None]]
    for which in range(4):
        outs += [result(k, which) for k in ORDER]
    return tuple(outs)
```

```python
import functools

import jax
import jax.numpy as jnp
from jax import lax
from jax.experimental import pallas as pl
from jax.experimental.pallas import tpu as pltpu

F32 = jnp.float32
BF16 = jnp.bfloat16
EPS = 1e-6
NDEV = 8
LANES = 128
CHUNK = 128
GROUP = 128
CONV_W = 31
CONV_PAD = 32
NOPE, ROPE, VDIM = 128, 64, 128
QK = NOPE + ROPE
ROPE_THETA = 10000.0
VMEM_LIMIT = 56 * 1024 * 1024
ADAM_LR, ADAM_B1, ADAM_B2, ADAM_EPS, ADAM_WD, ADAM_STEP = 0.001, 0.9, 0.999, 1e-08, 0.01, 10
MESH = pl.DeviceIdType.MESH
NEG = -1e30
LOG2E = 1.4426950408889634
MLP_ROWS = 2048
ROW_TILE = 256
HEAD_ROWS = 256
HEAD_ROWS_FWD = 512
ATTN_BLOCK = 1024
ATTN_STRIP = 64


def _pcall(body, **kw):
    return pl.pallas_call(body, **kw)


def _params(sem=None):
    return pltpu.CompilerParams(dimension_semantics=sem, vmem_limit_bytes=VMEM_LIMIT)


def _tile(dim, target, align=LANES):
    if dim <= target:
        return dim
    t = (target // align) * align
    while t >= align:
        if dim % t == 0:
            return t
        t -= align
    return dim


def _rstd(x):
    return lax.rsqrt(jnp.mean(x * x, axis=-1, keepdims=True) + EPS)


def _sigmoid(x):
    return 1.0 / (1.0 + jnp.exp(-x))


_GC = 0.7978845608028654


def _gelu(x):
    return 0.5 * x * (1.0 + jnp.tanh(_GC * (x + 0.044715 * x * x * x)))


def _gelu_grad(x):
    t = jnp.tanh(_GC * (x + 0.044715 * x * x * x))
    return 0.5 * (1.0 + t) + 0.5 * x * (1.0 - t * t) * _GC * (1.0 + 3 * 0.044715 * x * x)


def _colsum(x):
    return jnp.sum(x, axis=0, keepdims=True)


def _rms_bwd(dy, xhat, rstd, g):
    dxh = dy * g
    dx = rstd * (dxh - xhat * jnp.mean(dxh * xhat, axis=-1, keepdims=True))
    return dx, _colsum(dy * xhat)


def _swap_halves(x):
    h = x.shape[-1] // 2
    return jnp.concatenate([x[:, h:], x[:, :h]], axis=1)


def _rope(x, cos2, sinm):
    return x * cos2 + _swap_halves(x) * sinm


def _unrope(dy, cos2, sinm):
    return dy * cos2 + _swap_halves(dy * sinm)


_DIMS = {"nn": (((1,), (0,)), ((), ())), "nt": (((1,), (1,)), ((), ())), "tn": (((0,), (0,)), ((), ()))}


def mm(a, b, mode, *, name, out_dtypes=(F32,), epi=None, extras=(), rowvecs=(), b_blocked=False, out_blocked=0,
       tm=1024, tn=1024, tk=2048, precision=None, cast=BF16):
    if mode == "tn":
        K, M = a.shape
    else:
        M, K = a.shape
    if b_blocked:
        J, Rb, Cb = b.shape
        N = Rb if mode == "nt" else J * Cb
    else:
        N = b.shape[0] if mode == "nt" else b.shape[1]
    tm = _tile(M, tm)
    nbo = 1
    nb = 1
    if mode == "nn" and b_blocked:
        if Cb >= tn:
            tn = _tile(Cb, tn)
        else:
            nb = max(d for d in range(1, J + 1) if J % d == 0 and d * Cb <= tn)
            tn = nb * Cb
    elif out_blocked:
        Nb = N // out_blocked
        if Nb >= tn:
            tn = _tile(Nb, tn)
        else:
            nbo = max(d for d in range(1, out_blocked + 1) if out_blocked % d == 0 and d * Nb <= tn)
            tn = nbo * Nb
    else:
        tn = _tile(N, tn)
    kb = 1
    if mode == "nt" and b_blocked:
        if Cb >= tk:
            tk = _tile(Cb, tk)
        else:
            kb = max(d for d in range(1, J + 1) if J % d == 0 and d * Cb <= tk)
            tk = kb * Cb
    else:
        tk = _tile(K, tk)
    nk = K // tk
    grid = (M // tm, N // tn, nk)

    if mode == "tn":
        a_spec = pl.BlockSpec((tk, tm), lambda i, j, k: (k, i))
    else:
        a_spec = pl.BlockSpec((tm, tk), lambda i, j, k: (i, k))
    if mode == "nn":
        if b_blocked:
            if nb > 1:
                b_spec = pl.BlockSpec((nb, tk, Cb), lambda i, j, k: (j, k, 0))
            else:
                nper = Cb // tn
                b_spec = pl.BlockSpec((None, tk, tn), lambda i, j, k: (j // nper, k, j % nper))
        else:
            b_spec = pl.BlockSpec((tk, tn), lambda i, j, k: (k, j))
    elif mode == "nt":
        if b_blocked:
            if kb > 1:
                b_spec = pl.BlockSpec((kb, tn, Cb), lambda i, j, k: (k, j, 0))
            else:
                kper = Cb // tk
                b_spec = pl.BlockSpec((None, tn, tk), lambda i, j, k: (k // kper, j, k % kper))
        else:
            b_spec = pl.BlockSpec((tn, tk), lambda i, j, k: (j, k))
    else:
        b_spec = pl.BlockSpec((tk, tn), lambda i, j, k: (k, j))
    if out_blocked:
        if nbo > 1:
            o_spec = pl.BlockSpec((nbo, tm, N // out_blocked), lambda i, j, k: (j, i, 0))
        else:
            oper = (N // out_blocked) // tn
            o_spec = pl.BlockSpec((None, tm, tn), lambda i, j, k: (j // oper, i, j % oper))
        o_shape = (out_blocked, M, N // out_blocked)
    else:
        o_spec = pl.BlockSpec((tm, tn), lambda i, j, k: (i, j))
        o_shape = (M, N)
    e_spec = pl.BlockSpec((tm, tn), lambda i, j, k: (i, j))
    r_spec = pl.BlockSpec((1, tn), lambda i, j, k: (0, j))
    ne, nr, no = len(extras), len(rowvecs), len(out_dtypes)
    dims = _DIMS[mode]

    def body(a_ref, b_ref, *rest):
        ex = rest[:ne]
        rv = rest[ne:ne + nr]
        outs = rest[ne + nr:ne + nr + no]

        def product():
            if nb > 1:
                av = a_ref[...] if cast is None else a_ref[...].astype(cast)
                return jnp.concatenate(
                    [lax.dot_general(av, b_ref[q] if cast is None else b_ref[q].astype(cast), dims,
                                     preferred_element_type=F32, precision=precision) for q in range(nb)], axis=1)
            if kb > 1:
                r = None
                for q in range(kb):
                    av, bv = a_ref[:, q * Cb:(q + 1) * Cb], b_ref[q]
                    if cast is not None:
                        av, bv = av.astype(cast), bv.astype(cast)
                    d = lax.dot_general(av, bv, dims, preferred_element_type=F32, precision=precision)
                    r = d if r is None else r + d
                return r
            av, bv = a_ref[...], b_ref[...]
            if cast is not None:
                av, bv = av.astype(cast), bv.astype(cast)
            return lax.dot_general(av, bv, dims, preferred_element_type=F32, precision=precision)

        def finish(r):
            vals = (r,) if epi is None else epi(r, *[e[...] for e in ex], *[v[...] for v in rv])
            for o, val in zip(outs, vals):
                if nbo > 1:
                    w_ = N // out_blocked
                    for q in range(nbo):
                        o[q] = val[:, q * w_:(q + 1) * w_].astype(o.dtype)
                else:
                    o[...] = val.astype(o.dtype)

        if nk == 1:
            finish(product())
            return
        acc = rest[ne + nr + no]
        k = pl.program_id(2)

        @pl.when(k == 0)
        def _():
            acc[...] = product()

        @pl.when((k > 0) & (k < nk - 1))
        def _():
            acc[...] += product()

        @pl.when(k == nk - 1)
        def _():
            finish(acc[...] + product())

    res = _pcall(
        body, name=name, grid=grid,
        in_specs=[a_spec, b_spec] + [e_spec] * ne + [r_spec] * nr,
        out_specs=[o_spec] * no,
        out_shape=[jax.ShapeDtypeStruct(o_shape, dt) for dt in out_dtypes],
        scratch_shapes=[] if nk == 1 else [pltpu.VMEM((tm, tn), F32)],
        compiler_params=_params(("parallel", "parallel", "arbitrary")),
    )(a, b, *extras, *rowvecs)
    return res[0] if no == 1 else res


def rowwise(fn, tiled, consts=(), outs=(), reds=(), *, ts, name):
    specs = []
    arrs = []
    rows = None
    for t in tiled:
        a, w, cb = t if isinstance(t, tuple) else (t, None, 0)
        arrs.append(a)
        rows = a.shape[-2] if rows is None else rows
        if a.ndim == 2:
            specs.append(pl.BlockSpec((ts, a.shape[1] if w is None else w), lambda i, cb=cb: (i, cb)))
        else:
            specs.append(pl.BlockSpec((a.shape[0], ts, a.shape[2]), lambda i: (0, i, 0)))
    for a in consts:
        specs.append(pl.BlockSpec(a.shape, lambda i, n=a.ndim: (0,) * n))
    o_specs, o_shapes = [], []
    for shp, dt in outs:
        if len(shp) == 2:
            o_specs.append(pl.BlockSpec((ts, shp[1]), lambda i: (i, 0)))
        else:
            o_specs.append(pl.BlockSpec((shp[0], ts, shp[2]), lambda i: (0, i, 0)))
        o_shapes.append(jax.ShapeDtypeStruct(shp, dt))
    for shp in reds:
        o_specs.append(pl.BlockSpec(shp, lambda i, n=len(shp): (0,) * n))
        o_shapes.append(jax.ShapeDtypeStruct(shp, F32))
    nt, nc, no = len(arrs), len(consts), len(outs)

    def body(*refs):
        i = pl.program_id(0)
        red_refs = refs[nt + nc + no:]
        vals = fn(refs[:nt], refs[nt:nt + nc], refs[nt + nc:nt + nc + no])
        if red_refs:
            @pl.when(i == 0)
            def _():
                for r in red_refs:
                    r[...] = jnp.zeros_like(r)
            for r, v in zip(red_refs, vals):
                r[...] += v

    return _pcall(body, name=name, grid=(rows // ts,), in_specs=specs, out_specs=o_specs, out_shape=o_shapes,
                  compiler_params=_params(("arbitrary",)))(*arrs, *consts)


def cast_bf16(w, l, name):
    _, R, C = w.shape
    tr = _tile(R, 512, 16)

    def body(w_ref, o_ref):
        o_ref[...] = w_ref[...].astype(BF16)

    return _pcall(body, name=name, grid=(R // tr,), in_specs=[pl.BlockSpec((None, tr, C), lambda i: (l, i, 0))],
                  out_specs=pl.BlockSpec((tr, C), lambda i: (i, 0)), out_shape=jax.ShapeDtypeStruct((R, C), BF16),
                  compiler_params=_params(("parallel",)))(w)


def prenorm(x, g, scale, shift, name):
    S, D = x.shape

    def fn(t, c, o):
        xv = t[0][...]
        r = _rstd(xv)
        o[0][...] = ((xv * r * c[0][...]) * (1.0 + c[1][...]) + c[2][...]).astype(BF16)
        o[1][...] = r
        return []

    return rowwise(fn, [x], [g, scale, shift], [((S, D), BF16), ((S, 1), F32)], ts=_tile(S, ROW_TILE, 16), name=name)


def _gate_stage(dx, y_ref, gate_ref, dy_ref):
    dy_ref[...] = (dx * gate_ref[...]).astype(BF16)
    return _colsum(dx * y_ref[...].astype(F32))


def norm_bwd(dh, x, rstd, dres, g, scale, name, gated=None):
    S, D = x.shape

    def fn(t, c, o):
        d = t[0][...]
        r = t[2][...]
        xh = t[1][...] * r
        gv = c[0][...]
        dr = d * (1.0 + c[1][...])
        dx, dg = _rms_bwd(dr, xh, r, gv)
        dx = t[3][...] + dx
        o[0][...] = dx
        reds = [_colsum(d * (xh * gv)), _colsum(d), dg]
        if gated is not None:
            reds.append(_gate_stage(dx, t[4], c[2], o[1]))
        return reds

    ng = gated is not None
    res = rowwise(fn, [dh, x, rstd, dres] + ([gated[0]] if ng else []), [g, scale] + ([gated[1]] if ng else []),
                  [((S, D), F32)] + ([((S, D), BF16)] if ng else []), [(1, D)] * (3 + ng), ts=_tile(S, ROW_TILE, 8),
                  name=name)
    return (res[0], res[2], res[3], res[4], res[1], res[5]) if ng else res


def loss_grad(y, tgt, yb, gate, name):
    S, D = y.shape

    def fn(t, c, o):
        e = t[0][...] - t[1][...]
        dx = e * (1.0 / D)
        o[0][...] = dx
        return [_colsum(e * e), _gate_stage(dx, t[2], c[0], o[1])]

    return rowwise(fn, [y, tgt, yb], [gate], outs=[((S, D), F32), ((S, D), BF16)], reds=[(1, D)] * 2,
                   ts=_tile(S, ROW_TILE, 8), name=name)


def _tril_mask():
    r = lax.broadcasted_iota(jnp.int32, (CHUNK, CHUNK), 0)
    c = lax.broadcasted_iota(jnp.int32, (CHUNK, CHUNK), 1)
    return c <= r


def sgu_fwd(proj, ng, w, bcol, DA, name):
    S = proj.shape[0]
    G = DA // GROUP
    tr = _tile(S, 2 * CHUNK)

    def body(u_ref, v_ref, ng_ref, w_ref, b_ref, o_ref):
        mask = _tril_mask()
        for g in range(G):
            cols = slice(g * GROUP, (g + 1) * GROUP)
            wm = jnp.where(mask, w_ref[g], 0.0).astype(BF16)
            for ci in range(tr // CHUNK):
                rows = slice(ci * CHUNK, (ci + 1) * CHUNK)
                gv = _gelu(v_ref[rows, cols])
                vn = gv * _rstd(gv) * ng_ref[:, cols]
                mixed = jnp.dot(wm, vn.astype(BF16), preferred_element_type=F32) + b_ref[g]
                o_ref[rows, cols] = (_gelu(u_ref[rows, cols]) * mixed).astype(o_ref.dtype)

    return _pcall(
        body, name=name, grid=(S // tr,),
        in_specs=[pl.BlockSpec((tr, DA), lambda i: (i, 0)), pl.BlockSpec((tr, DA), lambda i: (i, 1)),
                  pl.BlockSpec((1, DA), lambda i: (0, 0)), pl.BlockSpec((G, CHUNK, CHUNK), lambda i: (0, 0, 0)),
                  pl.BlockSpec((G, CHUNK, 1), lambda i: (0, 0, 0))],
        out_specs=pl.BlockSpec((tr, DA), lambda i: (i, 0)),
        out_shape=jax.ShapeDtypeStruct((S, DA), BF16),
        compiler_params=_params(("parallel",)),
    )(proj, proj, ng, w, bcol)


def sgu_bwd(proj, dcat, ng, w, bcol, DA, name):
    S = proj.shape[0]
    G = DA // GROUP
    tr = _tile(S, 2 * CHUNK)
    nsteps = S // tr

    def body(u_ref, v_ref, d_ref, ng_ref, w_ref, b_ref, duv_ref, dw_ref, db_ref, dng_ref, dbacc):
        i = pl.program_id(0)

        @pl.when(i == 0)
        def _():
            dw_ref[...] = jnp.zeros_like(dw_ref)
            dng_ref[...] = jnp.zeros_like(dng_ref)
            dbacc[...] = jnp.zeros_like(dbacc)

        mask = _tril_mask()
        for g in range(G):
            cols = slice(g * GROUP, (g + 1) * GROUP)
            wm = jnp.where(mask, w_ref[g], 0.0).astype(BF16)
            ngg = ng_ref[:, cols]
            for ci in range(tr // CHUNK):
                rows = slice(ci * CHUNK, (ci + 1) * CHUNK)
                u, v, d = u_ref[rows, cols], v_ref[rows, cols], d_ref[rows, cols]
                gv = _gelu(v)
                rs = _rstd(gv)
                vhat = gv * rs
                vn = (vhat * ngg).astype(BF16)
                mixed = jnp.dot(wm, vn, preferred_element_type=F32) + b_ref[g]
                dmixed = d * _gelu(u)
                dmb = dmixed.astype(BF16)
                duv_ref[rows, cols] = (d * mixed * _gelu_grad(u)).astype(duv_ref.dtype)
                dwg = lax.dot_general(dmb, vn, _DIMS["nt"], preferred_element_type=F32)
                dw_ref[g] += jnp.where(mask, dwg, 0.0)
                dbacc[g] += dmixed
                dvn = lax.dot_general(wm, dmb, _DIMS["tn"], preferred_element_type=F32)
                dgv, dngg = _rms_bwd(dvn, vhat, rs, ngg)
                dng_ref[:, cols] += dngg
                duv_ref[rows, DA + g * GROUP:DA + (g + 1) * GROUP] = (dgv * _gelu_grad(v)).astype(duv_ref.dtype)

        @pl.when(i == nsteps - 1)
        def _():
            for g in range(G):
                db_ref[g] = jnp.sum(dbacc[g], axis=-1, keepdims=True)

    return _pcall(
        body, name=name, grid=(nsteps,),
        in_specs=[pl.BlockSpec((tr, DA), lambda i: (i, 0)), pl.BlockSpec((tr, DA), lambda i: (i, 1)),
                  pl.BlockSpec((tr, DA), lambda i: (i, 0)),
                  pl.BlockSpec((1, DA), lambda i: (0, 0)), pl.BlockSpec((G, CHUNK, CHUNK), lambda i: (0, 0, 0)),
                  pl.BlockSpec((G, CHUNK, 1), lambda i: (0, 0, 0))],
        out_specs=[pl.BlockSpec((tr, 2 * DA), lambda i: (i, 0)), pl.BlockSpec((G, CHUNK, CHUNK), lambda i: (0, 0, 0)),
                   pl.BlockSpec((G, CHUNK, 1), lambda i: (0, 0, 0)), pl.BlockSpec((1, DA), lambda i: (0, 0))],
        out_shape=[jax.ShapeDtypeStruct((S, 2 * DA), BF16), jax.ShapeDtypeStruct((G, CHUNK, CHUNK), F32),
                   jax.ShapeDtypeStruct((G, CHUNK, 1), F32), jax.ShapeDtypeStruct((1, DA), F32)],
        scratch_shapes=[pltpu.VMEM((G, CHUNK, CHUNK), F32)],
        compiler_params=_params(("arbitrary",)),
    )(proj, proj, dcat, ng, w, bcol)


def _conv_tile(S):
    return _tile(S, 256, 8)


def conv_fwd(proj, wk, bias, DA, DB, name):
    S = proj.shape[0]
    nb = DB // LANES
    a0, g0 = 2 * DA // LANES, (2 * DA + DB) // LANES
    T = _conv_tile(S)
    off = CONV_PAD - (CONV_W - 1)

    def body(a_ref, g_ref, w_ref, b_ref, o_ref, ypad):
        ypad[0:CONV_PAD, :] = jnp.zeros((CONV_PAD, LANES), F32)

        def fill(t, cr):
            r = pl.multiple_of(t * T, T)
            ypad[pl.ds(CONV_PAD + r, T), :] = a_ref[pl.ds(r, T), :] * _sigmoid(g_ref[pl.ds(r, T), :])
            return cr

        lax.fori_loop(0, S // T, fill, 0)

        def step(t, cr):
            r = pl.multiple_of(t * T, T)
            acc = jnp.zeros((T, LANES), F32) + b_ref[...]
            for k in range(CONV_W):
                acc = acc + w_ref[k:k + 1, :] * ypad[pl.ds(r + (k + off), T), :]
            o_ref[pl.ds(r, T), :] = acc
            return cr

        lax.fori_loop(0, S // T, step, 0)

    return _pcall(
        body, name=name, grid=(nb,),
        in_specs=[pl.BlockSpec((S, LANES), lambda j: (0, a0 + j)), pl.BlockSpec((S, LANES), lambda j: (0, g0 + j)),
                  pl.BlockSpec((CONV_PAD, LANES), lambda j: (0, j)), pl.BlockSpec((1, LANES), lambda j: (0, j))],
        out_specs=pl.BlockSpec((S, LANES), lambda j: (0, j)),
        out_shape=jax.ShapeDtypeStruct((S, DB), F32),
        scratch_shapes=[pltpu.VMEM((S + CONV_PAD, LANES), F32)],
        compiler_params=_params(("parallel",)),
    )(proj, proj, wk, bias)


def conv_bwd(proj, dy1, wk, DA, DB, name):
    S = proj.shape[0]
    nb = DB // LANES
    a0, g0 = 2 * DA // LANES, (2 * DA + DB) // LANES
    T = _conv_tile(S)
    off = CONV_PAD - (CONV_W - 1)

    def body(a_ref, g_ref, d_ref, w_ref, da_ref, dg_ref, dw_ref, ypad, dpad, wacc):
        ypad[0:CONV_PAD, :] = jnp.zeros((CONV_PAD, LANES), F32)
        dpad[S:S + CONV_PAD, :] = jnp.zeros((CONV_PAD, LANES), F32)
        wacc[...] = jnp.zeros_like(wacc)

        def fill(t, cr):
            r = pl.multiple_of(t * T, T)
            ypad[pl.ds(CONV_PAD + r, T), :] = a_ref[pl.ds(r, T), :] * _sigmoid(g_ref[pl.ds(r, T), :])
            dpad[pl.ds(r, T), :] = d_ref[pl.ds(r, T), :]
            return cr

        lax.fori_loop(0, S // T, fill, 0)

        def step(t, cr):
            r = pl.multiple_of(t * T, T)
            dt = dpad[pl.ds(r, T), :]
            dy0 = jnp.zeros((T, LANES), F32)
            for k in range(CONV_W):
                prod = dt * ypad[pl.ds(r + (k + off), T), :]
                wacc[k] += jnp.sum(prod.reshape(T // 8, 8, LANES), axis=0)
                dy0 = dy0 + w_ref[k:k + 1, :] * dpad[pl.ds(r + (CONV_W - 1 - k), T), :]
            av, gv = a_ref[pl.ds(r, T), :], g_ref[pl.ds(r, T), :]
            sg = _sigmoid(gv)
            da_ref[pl.ds(r, T), :] = (dy0 * sg).astype(da_ref.dtype)
            dg_ref[pl.ds(r, T), :] = (dy0 * av * sg * (1.0 - sg)).astype(dg_ref.dtype)
            return cr

        lax.fori_loop(0, S // T, step, 0)
        for k in range(CONV_W):
            dw_ref[k:k + 1, :] = jnp.sum(wacc[k], axis=0, keepdims=True)
        dw_ref[CONV_W:CONV_PAD, :] = jnp.zeros((CONV_PAD - CONV_W, LANES), F32)

    return _pcall(
        body, name=name, grid=(nb,),
        in_specs=[pl.BlockSpec((S, LANES), lambda j: (0, a0 + j)), pl.BlockSpec((S, LANES), lambda j: (0, g0 + j)),
                  pl.BlockSpec((S, LANES), lambda j: (0, j)), pl.BlockSpec((CONV_PAD, LANES), lambda j: (0, j))],
        out_specs=[pl.BlockSpec((S, LANES), lambda j: (0, j)), pl.BlockSpec((S, LANES), lambda j: (0, j)),
                   pl.BlockSpec((CONV_PAD, LANES), lambda j: (0, j))],
        out_shape=[jax.ShapeDtypeStruct((S, DB), BF16), jax.ShapeDtypeStruct((S, DB), BF16),
                   jax.ShapeDtypeStruct((CONV_PAD, DB), F32)],
        scratch_shapes=[pltpu.VMEM((S + CONV_PAD, LANES), F32), pltpu.VMEM((S + CONV_PAD, LANES), F32),
                        pltpu.VMEM((CONV_PAD, 8, LANES), F32)],
        compiler_params=_params(("parallel",)),
    )(proj, proj, dy1, wk)


def _ln_stats(y):
    mu = jnp.mean(y, axis=-1, keepdims=True)
    yc = y - mu
    rs = lax.rsqrt(jnp.mean(yc * yc, axis=-1, keepdims=True) + EPS)
    return yc * rs, rs


def ln_silu(y1, lg, lb, name):
    S, DB = y1.shape

    def fn(t, c, o):
        yh, _ = _ln_stats(t[0][...])
        ln = yh * c[0][...] + c[1][...]
        o[0][...] = (ln * _sigmoid(ln)).astype(BF16)
        return []

    return rowwise(fn, [y1], [lg, lb], [((S, DB), BF16)], ts=_tile(S, ROW_TILE, 16), name=name)[0]


def ln_silu_bwd(dcat, y1, lg, lb, name):
    S, DB = y1.shape
    cb = (dcat.shape[1] - DB) // DB

    def fn(t, c, o):
        yh, rs = _ln_stats(t[1][...])
        gv = c[0][...]
        ln = yh * gv + c[1][...]
        sg = _sigmoid(ln)
        dln = t[0][...] * (sg * (1.0 + ln * (1.0 - sg)))
        dyh = dln * gv
        dy = rs * (dyh - jnp.mean(dyh, axis=-1, keepdims=True) - yh * jnp.mean(dyh * yh, axis=-1, keepdims=True))
        o[0][...] = dy
        return [_colsum(dln * yh), _colsum(dln), _colsum(dy)]

    return rowwise(fn, [(dcat, DB, cb), y1], [lg, lb], [((S, DB), F32)], [(1, DB)] * 3, ts=_tile(S, ROW_TILE, 8), name=name)


def mla_norms(proj, cos2, sinm, qg, kvg, kgr, R, name):
    S = proj.shape[0]

    def fn(t, c, o):
        cq = t[0][:, 0:R]
        ckv = t[0][:, R:2 * R]
        kr = t[0][:, 2 * R:2 * R + ROPE]
        o[0][...] = (cq * _rstd(cq) * c[0][...]).astype(BF16)
        o[1][...] = (ckv * _rstd(ckv) * c[1][...]).astype(BF16)
        o[2][...] = _rope(kr * _rstd(kr) * c[2][...], t[1][...], t[2][...])
        return []

    return rowwise(fn, [proj, cos2, sinm], [qg, kvg, kgr], [((S, R), BF16), ((S, R), BF16), ((S, ROPE), F32)],
                   ts=_tile(S, ROW_TILE, 16), name=name)


def mla_heads(q, kv, kr, cos2, sinm, qg, kg, H, name):
    S = q.shape[0]

    def fn(t, c, o):
        cs, sn = t[3][...], t[4][...]
        krv = t[2][...]
        qgn, qgr, kgn = c[0][:, 0:NOPE], c[0][:, NOPE:QK], c[1][:, 0:NOPE]
        for h in range(H):
            qn = t[0][:, QK * h:QK * h + NOPE]
            qr = t[0][:, QK * h + NOPE:QK * (h + 1)]
            o[0][h, :, 0:NOPE] = (qn * _rstd(qn) * qgn).astype(BF16)
            o[0][h, :, NOPE:QK] = _rope(qr * _rstd(qr) * qgr, cs, sn).astype(BF16)
            kn = t[1][:, (NOPE + VDIM) * h:(NOPE + VDIM) * h + NOPE]
            o[1][h, :, 0:NOPE] = (kn * _rstd(kn) * kgn).astype(BF16)
            o[1][h, :, NOPE:QK] = krv.astype(BF16)
            o[2][h] = t[1][:, (NOPE + VDIM) * h + NOPE:(NOPE + VDIM) * (h + 1)].astype(BF16)
        return []

    return rowwise(fn, [q, kv, kr, cos2, sinm], [qg, kg],
                   [((H, S, QK), BF16), ((H, S, QK), BF16), ((H, S, VDIM), BF16)], ts=_tile(S, HEAD_ROWS_FWD, 16),
                   name=name)


def mla_heads_bwd(dQ, dK, dV, q, kv, cos2, sinm, qg, kg, H, name):
    S = q.shape[0]
    KV = NOPE + VDIM

    def fn(t, c, o):
        cs, sn = t[5][...], t[6][...]
        qgn, qgr, kgn = c[0][:, 0:NOPE], c[0][:, NOPE:QK], c[1][:, 0:NOPE]
        a_qn = jnp.zeros((1, NOPE), F32)
        a_qr = jnp.zeros((1, ROPE), F32)
        a_kn = jnp.zeros((1, NOPE), F32)
        dkr = jnp.zeros((t[0].shape[1], ROPE), F32)
        for h in range(H):
            qn = t[3][:, QK * h:QK * h + NOPE]
            rs = _rstd(qn)
            dx, dg = _rms_bwd(t[0][h, :, 0:NOPE], qn * rs, rs, qgn)
            o[0][:, QK * h:QK * h + NOPE] = dx.astype(BF16)
            a_qn = a_qn + dg
            qr = t[3][:, QK * h + NOPE:QK * (h + 1)]
            rs = _rstd(qr)
            dx, dg = _rms_bwd(_unrope(t[0][h, :, NOPE:QK], cs, sn), qr * rs, rs, qgr)
            o[0][:, QK * h + NOPE:QK * (h + 1)] = dx.astype(BF16)
            a_qr = a_qr + dg
            kn = t[4][:, KV * h:KV * h + NOPE]
            rs = _rstd(kn)
            dx, dg = _rms_bwd(t[1][h, :, 0:NOPE], kn * rs, rs, kgn)
            o[1][:, KV * h:KV * h + NOPE] = dx.astype(BF16)
            a_kn = a_kn + dg
            o[1][:, KV * h + NOPE:KV * (h + 1)] = t[2][h].astype(BF16)
            dkr = dkr + t[1][h, :, NOPE:QK]
        o[2][...] = _unrope(dkr, cs, sn)
        return [a_qn, a_qr, a_kn]

    return rowwise(fn, [dQ, dK, dV, q, kv, cos2, sinm], [qg, kg],
                   [((S, H * QK), BF16), ((S, H * KV), BF16), ((S, ROPE), F32)],
                   [(1, NOPE), (1, ROPE), (1, NOPE)], ts=_tile(S, HEAD_ROWS, 16), name=name)


def mla_norms_bwd(dqn, dkvn, dkr, proj, qg, kvg, kgr, R, name):
    S = proj.shape[0]

    def fn(t, c, o):
        reds = []
        for idx, (lo, hi) in enumerate(((0, R), (R, 2 * R), (2 * R, 2 * R + ROPE))):
            xv = t[3][:, lo:hi]
            rs = _rstd(xv)
            dx, dg = _rms_bwd(t[idx][...], xv * rs, rs, c[idx][...])
            o[0][:, lo:hi] = dx.astype(BF16)
            reds.append(dg)
        return reds

    return rowwise(fn, [dqn, dkvn, dkr, proj], [qg, kvg, kgr], [((S, 2 * R + ROPE), BF16)],
                   [(1, R), (1, R), (1, ROPE)], ts=_tile(S, ROW_TILE, 16), name=name)


def _tri_rows(p, n):
    qi = 0
    for j in range(1, n):
        qi = qi + (p >= j * (j + 1) // 2).astype(jnp.int32)
    return qi, p - (qi * (qi + 1)) // 2


def _tri_cols(p, n):
    ki = 0
    for j in range(1, n):
        ki = ki + (p >= j * n - j * (j - 1) // 2).astype(jnp.int32)
    return ki, ki + p - (ki * n - (ki * (ki - 1)) // 2)


def attn_fwd(Q, K, V, name):
    H, S, _ = Q.shape
    t = _tile(S, ATTN_BLOCK)
    n = S // t
    scale = QK ** -0.5
    scale2 = scale * LOG2E

    rs = _tile(t, ATTN_STRIP, 8)

    def body(q_ref, k_ref, v_ref, o_ref, lse_ref, m_s, l_s, acc, s_scr, p_scr):
        qi, ki = _tri_rows(pl.program_id(1), n)

        @pl.when(ki == 0)
        def _():
            m_s[...] = jnp.full_like(m_s, NEG)
            l_s[...] = jnp.zeros_like(l_s)
            acc[...] = jnp.zeros_like(acc)

        def block(diagonal):
            s_scr[...] = lax.dot_general(q_ref[...], k_ref[...], _DIMS["nt"], preferred_element_type=F32)

            def strip(i, cr):
                r = slice(i * rs, (i + 1) * rs)
                s = s_scr[r, :] * scale2
                if diagonal:
                    row = i * rs + lax.broadcasted_iota(jnp.int32, (rs, t), 0)
                    s = jnp.where(lax.broadcasted_iota(jnp.int32, (rs, t), 1) <= row, s, NEG)
                m_old = m_s[r, :]
                m_new = jnp.maximum(m_old, jnp.max(s, axis=-1, keepdims=True))
                alpha = jnp.exp2(m_old - m_new)
                p = jnp.exp2(s - m_new)
                l_s[r, :] = alpha * l_s[r, :] + jnp.sum(p, axis=-1, keepdims=True)
                m_s[r, :] = m_new
                acc[r, :] = alpha * acc[r, :]
                p_scr[r, :] = p.astype(BF16)
                return cr

            for i in range(t // rs):
                strip(i, 0)
            acc[...] += jnp.dot(p_scr[...], v_ref[...], preferred_element_type=F32)

        @pl.when(ki < qi)
        def _():
            block(False)

        @pl.when(ki == qi)
        def _():
            block(True)

        @pl.when(ki == qi)
        def _():
            o_ref[...] = (acc[...] / l_s[...]).astype(o_ref.dtype)
            lse_ref[...] = m_s[...] * (1.0 / LOG2E) + jnp.log(l_s[...])

    return _pcall(
        body, name=name, grid=(H, n * (n + 1) // 2),
        in_specs=[pl.BlockSpec((None, t, QK), lambda h, p: (h, _tri_rows(p, n)[0], 0)),
                  pl.BlockSpec((None, t, QK), lambda h, p: (h, _tri_rows(p, n)[1], 0)),
                  pl.BlockSpec((None, t, VDIM), lambda h, p: (h, _tri_rows(p, n)[1], 0))],
        out_specs=[pl.BlockSpec((t, VDIM), lambda h, p: (_tri_rows(p, n)[0], h)),
                   pl.BlockSpec((None, t, 1), lambda h, p: (h, _tri_rows(p, n)[0], 0))],
        out_shape=[jax.ShapeDtypeStruct((S, H * VDIM), BF16), jax.ShapeDtypeStruct((H, S, 1), F32)],
        scratch_shapes=[pltpu.VMEM((t, 1), F32), pltpu.VMEM((t, 1), F32), pltpu.VMEM((t, VDIM), F32),
                        pltpu.VMEM((t, t), F32), pltpu.VMEM((t, t), BF16)],
        compiler_params=_params(("parallel", "arbitrary")),
    )(Q, K, V)


def attn_bwd(Q, K, V, dO, O, lse, name):
    H, S, _ = Q.shape
    t = _tile(S, ATTN_BLOCK)
    n = S // t
    scale = QK ** -0.5
    scale2 = scale * LOG2E

    rs = _tile(t, ATTN_STRIP, 8)

    def body(q_ref, k_ref, v_ref, do_ref, o_ref, lse_ref, dq_ref, dk_ref, dv_ref, s_scr, dp_scr, p_scr, ds_scr):
        ki, qi = _tri_cols(pl.program_id(1), n)

        @pl.when(pl.program_id(1) == 0)
        def _():
            dq_ref[...] = jnp.zeros_like(dq_ref)

        @pl.when(qi == ki)
        def _():
            dk_ref[...] = jnp.zeros_like(dk_ref)
            dv_ref[...] = jnp.zeros_like(dv_ref)

        def block(diagonal):
            s_scr[...] = lax.dot_general(q_ref[...], k_ref[...], _DIMS["nt"], preferred_element_type=F32)
            dp_scr[...] = lax.dot_general(do_ref[...], v_ref[...], _DIMS["nt"], preferred_element_type=F32)

            def strip(i, cr):
                r = slice(i * rs, (i + 1) * rs)
                s = s_scr[r, :] * scale2
                if diagonal:
                    row = i * rs + lax.broadcasted_iota(jnp.int32, (rs, t), 0)
                    s = jnp.where(lax.broadcasted_iota(jnp.int32, (rs, t), 1) <= row, s, NEG)
                p = jnp.exp2(s - lse_ref[r, :] * LOG2E)
                delta = jnp.sum(do_ref[r, :].astype(F32) * o_ref[r, :].astype(F32), axis=-1, keepdims=True)
                p_scr[r, :] = p.astype(BF16)
                ds_scr[r, :] = (p * (dp_scr[r, :] - delta) * scale).astype(BF16)
                return cr

            for i in range(t // rs):
                strip(i, 0)
            ds = ds_scr[...]
            dv_ref[...] += lax.dot_general(p_scr[...], do_ref[...], _DIMS["tn"], preferred_element_type=F32)
            dk_ref[...] += lax.dot_general(ds, q_ref[...], _DIMS["tn"], preferred_element_type=F32)
            rq = pl.multiple_of(qi * t, t)
            dq_ref[pl.ds(rq, t), :] += jnp.dot(ds, k_ref[...], preferred_element_type=F32)

        @pl.when(qi > ki)
        def _():
            block(False)

        @pl.when(qi == ki)
        def _():
            block(True)

    qmap = lambda h, p: (h, _tri_cols(p, n)[1], 0)
    kmap = lambda h, p: (h, _tri_cols(p, n)[0], 0)
    return _pcall(
        body, name=name, grid=(H, n * (n + 1) // 2),
        in_specs=[pl.BlockSpec((None, t, QK), qmap),
                  pl.BlockSpec((None, t, QK), kmap),
                  pl.BlockSpec((None, t, VDIM), kmap),
                  pl.BlockSpec((t, VDIM), lambda h, p: (_tri_cols(p, n)[1], h)),
                  pl.BlockSpec((t, VDIM), lambda h, p: (_tri_cols(p, n)[1], h)),
                  pl.BlockSpec((None, t, 1), qmap)],
        out_specs=[pl.BlockSpec((None, S, QK), lambda h, p: (h, 0, 0)),
                   pl.BlockSpec((None, t, QK), kmap),
                   pl.BlockSpec((None, t, VDIM), kmap)],
        out_shape=[jax.ShapeDtypeStruct((H, S, QK), F32), jax.ShapeDtypeStruct((H, S, QK), F32),
                   jax.ShapeDtypeStruct((H, S, VDIM), F32)],
        scratch_shapes=[pltpu.VMEM((t, t), F32), pltpu.VMEM((t, t), F32), pltpu.VMEM((t, t), BF16),
                        pltpu.VMEM((t, t), BF16)],
        compiler_params=_params(("parallel", "arbitrary")),
    )(Q, K, V, dO, O, lse)


def adamw(w, m, v, g, *, row0, name, into=None):
    P, rows, C = g.shape
    tr = _tile(rows, max(16, 131072 // C), 16)
    off = row0 // tr
    assert row0 % tr == 0
    bc1 = 1.0 - ADAM_B1 ** ADAM_STEP
    bc2 = 1.0 - ADAM_B2 ** ADAM_STEP
    chained = into is not None and into is not True

    def body(w_ref, m_ref, v_ref, g_ref, *rest):
        go_ref, d_ref, mo_ref, vo_ref = rest[-4:]
        gs = g_ref[0].astype(F32)
        for p in range(1, P):
            gs = gs + g_ref[p].astype(F32)
        wv = w_ref[...]
        mn = ADAM_B1 * m_ref[...] + (1.0 - ADAM_B1) * gs
        vn = ADAM_B2 * v_ref[...] + (1.0 - ADAM_B2) * (gs * gs)
        go_ref[...] = gs
        mo_ref[...] = mn
        vo_ref[...] = vn
        d_ref[...] = -ADAM_LR * ((mn / bc1) / (jnp.sqrt(vn / bc2) + ADAM_EPS) + ADAM_WD * wv)

    wspec = pl.BlockSpec((tr, C), lambda i: (i + off, 0))
    ospec = wspec if into is not None else pl.BlockSpec((tr, C), lambda i: (i, 0))
    out_rows = w.shape[0] if into is not None else rows
    return _pcall(
        body, name=name, grid=(rows // tr,),
        in_specs=[wspec, wspec, wspec, pl.BlockSpec((P, tr, C), lambda i: (0, i, 0))] + ([_ANY] * 4 if chained else []),
        out_specs=[ospec] * 4, out_shape=[jax.ShapeDtypeStruct((out_rows, C), F32)] * 4,
        input_output_aliases={4 + q: q for q in range(4)} if chained else {},
        compiler_params=_params(("parallel",)),
    )(w, m, v, g, *(into if chained else ()))


def _coords():
    return lax.axis_index("x"), lax.axis_index("y"), lax.axis_index("c")


def _me():
    x, y, c = _coords()
    return 4 * x + 2 * y + c


_ANY = pl.BlockSpec(memory_space=pl.ANY)


def all_gather(items, name):
    n = len(items)
    blks = [a.shape if idx is None else a.shape[1:] for a, idx in items]

    def body(*refs):
        ins, outs = refs[:n], refs[n:2 * n]
        send, recv, lsem = refs[2 * n:]
        x, y, c = _coords()
        me, sib = (x, y, c), (x, y, 1 - c)
        chips = [(1 - x, y), (x, 1 - y), (1 - x, 1 - y)]

        def src(i):
            return ins[i] if items[i][1] is None else ins[i].at[items[i][1]]

        def slot(i, p):
            return outs[i].at[4 * p[0] + 2 * p[1] + p[2]]

        def cp(i, k, block, to, s=None):
            return pltpu.make_async_remote_copy(
                src_ref=slot(i, block) if s is None else s, dst_ref=slot(i, block),
                send_sem=send.at[7 * i + k], recv_sem=recv.at[7 * i + k], device_id=to, device_id_type=MESH)

        mine = [pltpu.make_async_copy(src(i), slot(i, me), lsem.at[i]) for i in range(n)]
        for m_ in mine:
            m_.start()
        first = []
        for i in range(n):
            first.append(cp(i, 0, me, sib, src(i)))
            first += [cp(i, 1 + j, me, (*chip, c), src(i)) for j, chip in enumerate(chips)]
        for f in first:
            f.start()
        passed = []
        for j, chip in enumerate(chips):
            for i in range(n):
                cp(i, 1 + j, (*chip, c), me).wait_recv()
                p_ = cp(i, 4 + j, (*chip, c), sib)
                p_.start()
                passed.append(p_)
        for i in range(n):
            cp(i, 0, sib, me).wait_recv()
            for j, chip in enumerate(chips):
                cp(i, 4 + j, (*chip, 1 - c), me).wait_recv()
        for f in first + passed:
            f.wait_send()
        for m_ in mine:
            m_.wait()

    res = _pcall(
        body, name=name, in_specs=[_ANY] * n, out_specs=[_ANY] * n,
        out_shape=[jax.ShapeDtypeStruct((NDEV,) + tuple(b), a.dtype) for b, (a, _) in zip(blks, items)],
        scratch_shapes=[pltpu.SemaphoreType.DMA((7 * n,)), pltpu.SemaphoreType.DMA((7 * n,)),
                        pltpu.SemaphoreType.DMA((n,))],
    )(*[a for a, _ in items])
    return list(res)


_HBM = pl.BlockSpec(memory_space=pltpu.HBM)
_SEM = pl.BlockSpec(memory_space=pltpu.SEMAPHORE)
_EFFECT = pltpu.SideEffectType.DATAFLOW_SIDE_EFFECTING


def _xchg_copy(src_ref, land_ref, send, recv, r, scatter, at_peer):
    x, y, c = _coords()
    px = jnp.bitwise_xor(x, (r >> 2) & 1)
    py = jnp.bitwise_xor(y, (r >> 1) & 1)
    pc = jnp.bitwise_xor(c, r & 1)
    p_i = 4 * px + 2 * py + pc
    me_i = 4 * x + 2 * y + c
    return pltpu.make_async_remote_copy(
        src_ref=src_ref.at[p_i] if scatter else src_ref, dst_ref=land_ref.at[p_i if at_peer else me_i],
        send_sem=send.at[r - 1], recv_sem=recv.at[r - 1], device_id=(px, py, pc), device_id_type=MESH)


def _phase(body, name, bufs, sems_in=(), new_sems=(), after=None, token=False):
    nb, ns, nn = len(bufs), len(sems_in), len(new_sems)

    def wrapped(*refs):
        outs = refs[nb + ns + (after is not None):]
        body(refs[:nb], refs[nb:nb + ns], outs[:nn])
        if token:
            outs[nn + nb][...] = jnp.zeros_like(outs[nn + nb])

    res = _pcall(
        wrapped, name=name,
        out_shape=tuple([pltpu.SemaphoreType.DMA((k,)) for k in new_sems] + [pltpu.HBM(a.shape, a.dtype) for a in bufs]
                        + ([jax.ShapeDtypeStruct((8, LANES), F32)] if token else [])),
        in_specs=[_HBM] * nb + [_SEM] * ns + ([] if after is None else [_ANY]),
        out_specs=tuple([_SEM] * nn + [_HBM] * nb + ([pl.BlockSpec(memory_space=pltpu.VMEM)] if token else [])),
        input_output_aliases={i: nn + i for i in range(nb)},
        compiler_params=pltpu.CompilerParams(has_side_effects=_EFFECT),
    )(*[pltpu.with_memory_space_constraint(a, pltpu.HBM) for a in bufs], *sems_in, *([] if after is None else [after]))
    return list(res[nn:nn + nb]), list(res[:nn]), (res[nn + nb][0:1, 0:1] if token else None)


def scatter_start(srcs, name, after=None):
    n = len(srcs)
    lands = [lax.empty(s.shape, s.dtype) for s in srcs]

    def body(b, taken, new):
        me_i = _me()
        for i in range(n):
            pltpu.make_async_copy(b[i].at[me_i], b[n + i].at[me_i], new[3 * i + 2].at[0]).start()
            for r in range(1, NDEV):
                _xchg_copy(b[i], b[n + i], new[3 * i], new[3 * i + 1], r, True, False).start()

    bufs, sems, tok = _phase(body, name, list(srcs) + lands, new_sems=[NDEV - 1, NDEV - 1, 1] * n, after=after, token=True)
    return (bufs, sems), tok


def scatter_wait(handle, after, name):
    bufs, sems = handle
    n = len(bufs) // 2

    def body(b, taken, new):
        me_i = _me()
        for i in range(n):
            pltpu.make_async_copy(b[i].at[me_i], b[n + i].at[me_i], taken[3 * i + 2].at[0]).wait()
            for r in range(1, NDEV):
                cp = _xchg_copy(b[i], b[n + i], taken[3 * i], taken[3 * i + 1], r, True, True)
                cp.wait_send()
                cp.wait_recv()

    return _phase(body, name, bufs, sems_in=sems, after=after)[0][n:]


def _gather_peers():
    x, y, c = _coords()
    return (x, y, c), (x, y, 1 - c), [(1 - x, y), (x, 1 - y), (1 - x, 1 - y)]


def _row(p):
    return 4 * p[0] + 2 * p[1] + p[2]


def _gcopy(src_ref, land_ref, send, recv, k, block, to):
    return pltpu.make_async_remote_copy(
        src_ref=land_ref.at[_row(block)] if src_ref is None else src_ref, dst_ref=land_ref.at[_row(block)],
        send_sem=send.at[k], recv_sem=recv.at[k], device_id=to, device_id_type=MESH)


def gather_start(srcs, name, after=None):
    n = len(srcs)
    lands = [lax.empty((NDEV,) + s.shape, s.dtype) for s in srcs]

    def body(b, taken, new):
        me, sib, chips = _gather_peers()
        for i in range(n):
            send, recv = new[3 * i], new[3 * i + 1]
            pltpu.make_async_copy(b[i], b[n + i].at[_row(me)], new[3 * i + 2].at[0]).start()
            for j, chip in enumerate(chips):
                _gcopy(b[i], b[n + i], send, recv, 1 + j, me, (*chip, me[2])).start()
            _gcopy(b[i], b[n + i], send, recv, 0, me, sib).start()

    bufs, sems, tok = _phase(body, name, list(srcs) + lands, new_sems=[4, 4, 1] * n, after=after, token=True)
    return (bufs, sems), tok


def gather_mid(handle, after, name):
    bufs, sems = handle
    n = len(bufs) // 2

    def body(b, taken, new):
        me, sib, chips = _gather_peers()
        for j, chip in enumerate(chips):
            for i in range(n):
                _gcopy(b[i], b[n + i], taken[3 * i], taken[3 * i + 1], 1 + j, (*chip, me[2]), me).wait_recv()
                _gcopy(None, b[n + i], new[2 * i], new[2 * i + 1], j, (*chip, me[2]), sib).start()
        for i in range(n):
            send, recv = taken[3 * i], taken[3 * i + 1]
            _gcopy(b[i], b[n + i], send, recv, 0, sib, me).wait_recv()
            for k in range(4):
                _gcopy(b[i], b[n + i], send, recv, k, me, sib).wait_send()
            pltpu.make_async_copy(b[i], b[n + i].at[_row(me)], taken[3 * i + 2].at[0]).wait()

    bufs, new, tok = _phase(body, name, bufs, sems_in=sems, new_sems=[3, 3] * n, after=after, token=True)
    return (bufs, new), tok


def gather_wait(handle, after, name):
    bufs, sems = handle
    n = len(bufs) // 2

    def body(b, taken, new):
        me, sib, chips = _gather_peers()
        for i in range(n):
            for j, chip in enumerate(chips):
                _gcopy(None, b[n + i], taken[2 * i], taken[2 * i + 1], j, (*chip, me[2]), sib).wait_send()
                _gcopy(None, b[n + i], taken[2 * i], taken[2 * i + 1], j, (*chip, 1 - me[2]), me).wait_recv()

    return _phase(body, name, bufs, sems_in=sems, after=after)[0][n:]


_PACK_ALIGN = 8 * LANES


def _pack(arrs, aligned=False):
    parts = []
    for a in arrs:
        f = a.reshape(-1).astype(F32)
        pad = (-f.shape[0]) % _PACK_ALIGN if aligned else 0
        parts.append(jnp.pad(f, (0, pad)) if pad else f)
    flat = jnp.concatenate(parts)
    pad = (-flat.shape[0]) % _PACK_ALIGN
    return (jnp.pad(flat, (0, pad)) if pad else flat).reshape(-1, LANES)


def _unpack(p, shapes, lead=(), aligned=False):
    nl = len(lead)
    flat = p.reshape(lead + (-1,))
    out, off = [], 0
    for shp in shapes:
        n = 1
        for d in shp:
            n *= d
        out.append(lax.slice_in_dim(flat, off, off + n, axis=nl).reshape(lead + tuple(shp)))
        off += n + ((-n) % _PACK_ALIGN if aligned else 0)
    return out


def _shard_cols(a, me, width, axis):
    return lax.dynamic_slice_in_dim(a, me * width, width, axis=axis)


def kernel(x, c, norm1_g, norm2_g, ada_w, ada_b, mlp_w1, mlp_w2, ab_w_in, sgu_norm_g, sgu_w, sgu_b, conv_w, conv_b, conv_ln_g, conv_ln_b, ab_w_out, mla_w_in, mla_q_norm_g, mla_kv_norm_g, mla_w_uq, mla_w_ukv, mla_q_head_g, mla_k_head_g, mla_w_out, loss_target, m_norm1_g, m_norm2_g, m_ada_w, m_ada_b, m_mlp_w1, m_mlp_w2, m_ab_w_in, m_sgu_norm_g, m_sgu_w, m_sgu_b, m_conv_w, m_conv_b, m_conv_ln_g, m_conv_ln_b, m_ab_w_out, m_mla_w_in, m_mla_q_norm_g, m_mla_kv_norm_g, m_mla_w_uq, m_mla_w_ukv, m_mla_q_head_g, m_mla_k_head_g, m_mla_w_out, v_norm1_g, v_norm2_g, v_ada_w, v_ada_b, v_mlp_w1, v_mlp_w2, v_ab_w_in, v_sgu_norm_g, v_sgu_w, v_sgu_b, v_conv_w, v_conv_b, v_conv_ln_g, v_conv_ln_b, v_ab_w_out, v_mla_w_in, v_mla_q_norm_g, v_mla_kv_norm_g, v_mla_w_uq, v_mla_w_ukv, v_mla_q_head_g, v_mla_k_head_g, v_mla_w_out):
    W = dict(norm1_g=norm1_g, norm2_g=norm2_g, ada_w=ada_w, ada_b=ada_b, mlp_w1=mlp_w1, mlp_w2=mlp_w2, ab_w_in=ab_w_in,
             sgu_norm_g=sgu_norm_g, sgu_w=sgu_w, sgu_b=sgu_b, conv_w=conv_w, conv_b=conv_b, conv_ln_g=conv_ln_g,
             conv_ln_b=conv_ln_b, ab_w_out=ab_w_out, mla_w_in=mla_w_in, mla_q_norm_g=mla_q_norm_g,
             mla_kv_norm_g=mla_kv_norm_g, mla_w_uq=mla_w_uq, mla_w_ukv=mla_w_ukv, mla_q_head_g=mla_q_head_g,
             mla_k_head_g=mla_k_head_g, mla_w_out=mla_w_out)
    M = dict(norm1_g=m_norm1_g, norm2_g=m_norm2_g, ada_w=m_ada_w, ada_b=m_ada_b, mlp_w1=m_mlp_w1, mlp_w2=m_mlp_w2,
             ab_w_in=m_ab_w_in, sgu_norm_g=m_sgu_norm_g, sgu_w=m_sgu_w, sgu_b=m_sgu_b, conv_w=m_conv_w, conv_b=m_conv_b,
             conv_ln_g=m_conv_ln_g, conv_ln_b=m_conv_ln_b, ab_w_out=m_ab_w_out, mla_w_in=m_mla_w_in,
             mla_q_norm_g=m_mla_q_norm_g, mla_kv_norm_g=m_mla_kv_norm_g, mla_w_uq=m_mla_w_uq, mla_w_ukv=m_mla_w_ukv,
             mla_q_head_g=m_mla_q_head_g, mla_k_head_g=m_mla_k_head_g, mla_w_out=m_mla_w_out)
    V = dict(norm1_g=v_norm1_g, norm2_g=v_norm2_g, ada_w=v_ada_w, ada_b=v_ada_b, mlp_w1=v_mlp_w1, mlp_w2=v_mlp_w2,
             ab_w_in=v_ab_w_in, sgu_norm_g=v_sgu_norm_g, sgu_w=v_sgu_w, sgu_b=v_sgu_b, conv_w=v_conv_w, conv_b=v_conv_b,
             conv_ln_g=v_conv_ln_g, conv_ln_b=v_conv_ln_b, ab_w_out=v_ab_w_out, mla_w_in=v_mla_w_in,
             mla_q_norm_g=v_mla_q_norm_g, mla_kv_norm_g=v_mla_kv_norm_g, mla_w_uq=v_mla_w_uq, mla_w_ukv=v_mla_w_ukv,
             mla_q_head_g=v_mla_q_head_g, mla_k_head_g=v_mla_k_head_g, mla_w_out=v_mla_w_out)
    ORDER = list(W)

    S, D = x.shape[1], x.shape[2]
    L, NE, NO = norm1_g.shape[0], ab_w_in.shape[0], mla_w_in.shape[0]
    DA = D // 2
    DB = D - DA
    G = DA // GROUP
    R = NDEV * mla_q_norm_g.shape[1]
    H = NDEV * mla_w_uq.shape[2] // QK
    AW = ada_w.shape[2]
    CB = conv_w.shape[2]
    me = _me()
    xs, tgt = x[0], loss_target[0]

    BIG_EVEN = ("mlp_w1", "mlp_w2", "ab_w_in", "ab_w_out")
    BIG_ODD = ("mlp_w1", "mlp_w2", "mla_w_in", "mla_w_uq", "mla_w_ukv", "mla_w_out")
    BIG = ("mlp_w1", "mlp_w2", "ab_w_in", "ab_w_out", "mla_w_in", "mla_w_uq", "mla_w_ukv", "mla_w_out")
    COL_SHARDED = ("mlp_w1", "ab_w_in", "mla_w_uq", "mla_w_ukv")
    MLP_W = ("mlp_w1", "mlp_w2")
    mixer_w = lambda l: ("ab_w_in", "ab_w_out") if l % 2 == 0 else ("mla_w_in", "mla_w_uq", "mla_w_ukv", "mla_w_out")
    widx = lambda k, l: l if k in MLP_W else l // 2

    small_in = [c, mla_q_norm_g, mla_kv_norm_g, conv_w]
    sg = all_gather([(_pack(small_in, True), None)], "gather_small")[0]
    c_all, qng_all, kvng_all, cw_all = _unpack(sg, [a.shape for a in small_in], (NDEV,), True)
    c_all = c_all.reshape(NDEV, D)
    qng_full = jnp.transpose(qng_all, (1, 0, 2)).reshape(NO, 1, R)
    kvng_full = jnp.transpose(kvng_all, (1, 0, 2)).reshape(NO, 1, R)
    cw_full = jnp.transpose(cw_all, (1, 2, 0, 3)).reshape(NE, CONV_W, DB)
    cw_pad = jnp.pad(cw_full, ((0, 0), (0, CONV_PAD - CONV_W), (0, 0)))

    def silu_fn(t, c_, o):
        v_ = t[0][...]
        o[0][...] = v_ * _sigmoid(v_)
        return []

    c_act = rowwise(silu_fn, [c_all], outs=[((NDEV, D), F32)], ts=NDEV, name="silu_c")[0]
    bias_cols = _shard_cols(ada_b, me, AW, 1).reshape(1, L * AW)
    mod_cols = mm(c_act, ada_w, "nn", name="ada_fwd", b_blocked=True, rowvecs=[bias_cols],
                  epi=lambda acc, b_: (acc + b_,), tm=NDEV, tn=768)
    mod_all = all_gather([(mod_cols, None)], "gather_mod")[0]
    mod = lax.dynamic_index_in_dim(mod_all, me, axis=1, keepdims=False)
    mod = jnp.transpose(mod.reshape(NDEV, L, AW), (1, 0, 2)).reshape(L, 6, 1, D)

    wnames = lambda l, what: mixer_w(l) if what == "mix" else MLP_W
    g_first, g_second = {}, {}
    tok_sum = jnp.zeros((1, 1), F32)
    for l in range(L):
        for what in ("mix", "mlp"):
            srcs = [cast_bf16(W[k], widx(k, l), "cast_%s_l%d" % (k, l)) for k in wnames(l, what)]
            g_first[l, what], tok = gather_start(srcs, "gather_start_%s_l%d" % (what, l), after=mod_all)
            tok_sum = tok_sum + tok
    mod = mod + tok_sum

    def pass_on(l, what, after):
        g_second[l, what], tok = gather_mid(g_first[l, what], after, "gather_mid_%s_l%d" % (what, l))
        return tok

    def wait_weights(l, what, after):
        lands = gather_wait(g_second[l, what], after, "gather_wait_%s_l%d" % (what, l))
        return {k: (ld if k in COL_SHARDED else ld.reshape(NDEV * ld.shape[1], ld.shape[2]))
                for k, ld in zip(wnames(l, what), lands)}

    mod = mod + pass_on(0, "mix", mod)

    pos = jnp.arange(S, dtype=F32)
    inv = ROPE_THETA ** (-jnp.arange(0, ROPE, 2, dtype=F32) / ROPE)
    ang = pos[:, None] * inv[None, :]
    cos2 = jnp.concatenate([jnp.cos(ang), jnp.cos(ang)], axis=1)
    sinm = jnp.concatenate([-jnp.sin(ang), jnp.sin(ang)], axis=1)

    residual = lambda acc, xr, gt: (acc, xr + gt * acc)

    saved = []
    xc = xs
    for l in range(L):
        sh1, sc1, g1, sh2, sc2, g2 = [mod[l, k] for k in range(6)]
        tag = "_l%d" % l
        sv = dict(x0=xc)
        h, sv["rstd1"] = prenorm(xc, norm1_g[l][None], sc1, sh1, "prenorm1" + tag)
        sv["h"] = h
        wl = wait_weights(l, "mix", h)
        if l % 2 == 0:
            e = l // 2
            ng = sgu_norm_g[e].reshape(1, DA)
            bcol = sgu_b[e][:, :, None]
            proj = mm(h, wl["ab_w_in"], "nn", name="ab_in" + tag, b_blocked=True)
            out_a = sgu_fwd(proj, ng, sgu_w[e], bcol, DA, "sgu_fwd" + tag)
            y1 = conv_fwd(proj, cw_pad[e], conv_b[e][None], DA, DB, "conv_fwd" + tag)
            out_b = ln_silu(y1, conv_ln_g[e][None], conv_ln_b[e][None], "ln_silu" + tag)
            cat = jnp.concatenate([out_a, out_b], axis=1)
            sv.update(proj=proj, y1=y1, cat=cat)
            mixb, x1 = mm(cat, wl["ab_w_out"], "nn", name="ab_out" + tag, out_dtypes=(BF16, F32), epi=residual,
                          extras=[xc], rowvecs=[g1])
        else:
            o_ = l // 2
            proj = mm(h, wl["mla_w_in"], "nn", name="mla_in" + tag)
            kgr = mla_k_head_g[o_][None, NOPE:QK]
            qn, kvn, kr = mla_norms(proj, cos2, sinm, qng_full[o_], kvng_full[o_], kgr, R, "mla_norms" + tag)
            q = mm(qn, wl["mla_w_uq"], "nn", name="mla_uq" + tag, b_blocked=True)
            kv = mm(kvn, wl["mla_w_ukv"], "nn", name="mla_ukv" + tag, b_blocked=True)
            Qh, Kh, Vh = mla_heads(q, kv, kr, cos2, sinm, mla_q_head_g[o_][None], mla_k_head_g[o_][None], H,
                                   "mla_heads" + tag)
            att, lse = attn_fwd(Qh, Kh, Vh, "attn_fwd" + tag)
            sv.update(proj=proj, qn=qn, kvn=kvn, q=q, kv=kv, Qh=Qh, Kh=Kh, Vh=Vh, att=att, lse=lse)
            mixb, x1 = mm(att, wl["mla_w_out"], "nn", name="mla_out" + tag, out_dtypes=(BF16, F32), epi=residual,
                          extras=[xc], rowvecs=[g1])
        sv.update(mixb=mixb, x1=x1)
        h2, sv["rstd2"] = prenorm(x1, norm2_g[l][None], sc2 + pass_on(l, "mlp", x1), sh2, "prenorm2" + tag)
        wl.update(wait_weights(l, "mlp", h2))
        sv["w"] = wl
        z, act = mm(h2, wl["mlp_w1"], "nn", name="mlp_up" + tag, b_blocked=True, out_dtypes=(BF16, BF16),
                    epi=lambda acc: (acc, jnp.square(jnp.maximum(acc, 0.0))), tm=MLP_ROWS)
        g2t = g2 + pass_on(l + 1, "mix", z) if l + 1 < L else g2
        yb, xc = mm(act, wl["mlp_w2"], "nn", name="mlp_down" + tag, out_dtypes=(BF16, F32), epi=residual,
                    extras=[x1], rowvecs=[g2t])
        sv.update(h2=h2, z=z, act=act, yb=yb)
        saved.append(sv)

    dx, dy, loss_cols, dgate2 = loss_grad(xc, tgt, saved[L - 1]["yb"], mod[L - 1, 5], "loss")
    loss = lax.psum(0.5 / D * jnp.sum(loss_cols), ("x", "y", "c"))

    big_out = {}
    sm = {k: [None] * W[k].shape[0] for k in ("norm1_g", "norm2_g", "sgu_norm_g", "sgu_w", "sgu_b", "conv_b", "conv_ln_g",
                                               "conv_ln_b", "mla_q_head_g", "mla_k_head_g", "mla_q_norm_g",
                                               "mla_kv_norm_g", "conv_w")}
    dmod = [None] * L
    flat2 = {k: W[k].reshape(-1, W[k].shape[2]) for k in BIG}
    flat2m = {k: M[k].reshape(-1, W[k].shape[2]) for k in BIG}
    flat2v = {k: V[k].reshape(-1, W[k].shape[2]) for k in BIG}

    def send_grads(gr, what, tag, after=None):
        names = list(gr)
        blocks = [gr[k] if k in COL_SHARDED else gr[k].reshape(NDEV, gr[k].shape[0] // NDEV, gr[k].shape[1])
                  for k in names]
        handle, tok = scatter_start(blocks, "scatter_start_%s%s" % (what, tag), after=after)
        return (names, handle, what, tag), tok

    rep = ("norm1_g", "norm2_g", "sgu_norm_g", "sgu_w", "sgu_b", "conv_b", "conv_ln_g", "conv_ln_b", "mla_q_head_g",
           "mla_k_head_g")
    part_full = {"mla_q_norm_g": (NO, R), "mla_kv_norm_g": (NO, R), "conv_w": (NE, CONV_W, DB)}
    small = ["ada_b"] + list(rep) + list(part_full)
    shapes = [(L, 6 * D)] + [W[k].shape for k in rep] + list(part_full.values())

    def small_gather():
        parts = [jnp.stack(dmod).reshape(L, 6 * D)] + [jnp.stack(sm[k]).reshape(s_) for k, s_ in zip(small[1:], shapes[1:])]
        return all_gather([(_pack(parts), None)], "gather_smallgrads")[0]

    def scatter_finish(pending, after, l):
        names, handle, what, tag = pending
        landed = scatter_wait(handle, after, "scatter_wait_%s%s" % (what, tag))
        for k, land in zip(names, landed):
            li = widx(k, l)
            big_out[k] = adamw(flat2[k], flat2m[k], flat2v[k], land, row0=li * W[k].shape[1],
                               name="adamw_%s%s" % (k, tag), into=big_out.get(k, True))

    pend_mix, tok_mix = None, None
    for l in reversed(range(L)):
        sh1, sc1, g1, sh2, sc2, g2 = [mod[l, k] for k in range(6)]
        sv = saved[l]
        wl = sv["w"]
        tag = "_l%d" % l
        gr = {}
        dw2 = mm(sv["act"], dy, "tn", name="mlp_down_dw" + tag, out_dtypes=(BF16,), tm=MLP_ROWS)
        pend_w2, tok_w2 = send_grads({"mlp_w2": dw2}, "w2", tag)
        dz = mm(dy, wl["mlp_w2"], "nt", name="mlp_down_dx" + tag, out_dtypes=(BF16,), extras=[sv["z"]],
                rowvecs=[jnp.zeros((1, sv["z"].shape[1]), F32) + tok_w2],
                epi=lambda acc, z_, t_: (acc * (2.0 * jnp.maximum(z_.astype(F32), 0.0)) + t_,), tm=MLP_ROWS)
        dh2 = mm(dz, wl["mlp_w1"], "nt", name="mlp_up_dx" + tag, b_blocked=True)
        dw1 = mm(sv["h2"], dz, "tn", name="mlp_up_dw" + tag, out_dtypes=(BF16,), out_blocked=NDEV, tm=MLP_ROWS)
        pend_mlp, tok_mlp = send_grads({"mlp_w1": dw1}, "w1", tag)
        dx1, dsc2, dsh2, sm["norm2_g"][l], dmix, dgate1 = norm_bwd(
            dh2, sv["x1"], sv["rstd2"], dx, norm2_g[l][None], sc2 + tok_mlp, "norm2_bwd" + tag, gated=(sv["mixb"], g1))
        if pend_mix is not None:
            scatter_finish(pend_mix, dx1, l + 1)
        if l % 2 == 0:
            e = l // 2
            ng = sgu_norm_g[e].reshape(1, DA)
            bcol = sgu_b[e][:, :, None]
            dcat = mm(dmix, wl["ab_w_out"], "nt", name="ab_out_dx" + tag)
            gr["ab_w_out"] = mm(sv["cat"], dmix, "tn", name="ab_out_dw" + tag, out_dtypes=(BF16,))
            dy1, sm["conv_ln_g"][e], sm["conv_ln_b"][e], sm["conv_b"][e] = ln_silu_bwd(
                dcat, sv["y1"], conv_ln_g[e][None], conv_ln_b[e][None], "ln_silu_bwd" + tag)
            da, dg_, dwc = conv_bwd(sv["proj"], dy1, cw_pad[e], DA, DB, "conv_bwd" + tag)
            sm["conv_w"][e] = dwc[:CONV_W]
            duv, dsw, dsb, dsng = sgu_bwd(sv["proj"], dcat, ng, sgu_w[e], bcol, DA, "sgu_bwd" + tag)
            sm["sgu_w"][e], sm["sgu_b"][e], sm["sgu_norm_g"][e] = dsw, dsb, dsng
            dproj = jnp.concatenate([duv, da, dg_], axis=1)
            dh = mm(dproj, wl["ab_w_in"], "nt", name="ab_in_dx" + tag, b_blocked=True)
            gr["ab_w_in"] = mm(sv["h"], dproj, "tn", name="ab_in_dw" + tag, out_dtypes=(BF16,), out_blocked=NDEV)
        else:
            o_ = l // 2
            kgr = mla_k_head_g[o_][None, NOPE:QK]
            dO = mm(dmix, wl["mla_w_out"], "nt", name="mla_out_dx" + tag, out_dtypes=(BF16,))
            gr["mla_w_out"] = mm(sv["att"], dmix, "tn", name="mla_out_dw" + tag, out_dtypes=(BF16,))
            dQ, dK, dV = attn_bwd(sv["Qh"], sv["Kh"], sv["Vh"], dO, sv["att"], sv["lse"], "attn_bwd" + tag)
            dq_pre, dkv_pre, dkr, dqgn, dqgr, dkgn = mla_heads_bwd(
                dQ, dK, dV, sv["q"], sv["kv"], cos2, sinm, mla_q_head_g[o_][None], mla_k_head_g[o_][None], H,
                "mla_heads_bwd" + tag)
            dqn = mm(dq_pre, wl["mla_w_uq"], "nt", name="mla_uq_dx" + tag, b_blocked=True)
            gr["mla_w_uq"] = mm(sv["qn"], dq_pre, "tn", name="mla_uq_dw" + tag, out_dtypes=(BF16,), out_blocked=NDEV)
            dkvn = mm(dkv_pre, wl["mla_w_ukv"], "nt", name="mla_ukv_dx" + tag, b_blocked=True)
            gr["mla_w_ukv"] = mm(sv["kvn"], dkv_pre, "tn", name="mla_ukv_dw" + tag, out_dtypes=(BF16,),
                                 out_blocked=NDEV)
            dproj, sm["mla_q_norm_g"][o_], sm["mla_kv_norm_g"][o_], dkgr = mla_norms_bwd(
                dqn, dkvn, dkr, sv["proj"], qng_full[o_], kvng_full[o_], kgr, R, "mla_norms_bwd" + tag)
            sm["mla_q_head_g"][o_] = jnp.concatenate([dqgn, dqgr], axis=1)
            sm["mla_k_head_g"][o_] = jnp.concatenate([dkgn, dkgr], axis=1)
            dh = mm(dproj, wl["mla_w_in"], "nt", name="mla_in_dx" + tag)
            gr["mla_w_in"] = mm(sv["h"], dproj, "tn", name="mla_in_dw" + tag, out_dtypes=(BF16,))
        if l > 0:
            pend_mix, tok_mix = send_grads(gr, "mix", tag)
            dx, dsc1, dsh1, sm["norm1_g"][l], dy, dgate2_below = norm_bwd(
                dh, sv["x0"], sv["rstd1"], dx1, norm1_g[l][None], sc1 + tok_mix, "norm1_bwd" + tag,
                gated=(saved[l - 1]["yb"], mod[l - 1, 5]))
        else:
            dx, dsc1, dsh1, sm["norm1_g"][l] = norm_bwd(dh, sv["x0"], sv["rstd1"], dx1, norm1_g[l][None], sc1,
                                                        "norm1_bwd" + tag)
        dmod[l] = jnp.concatenate([dsh1, dsc1, dgate1, dsh2, dsc2, dgate2], axis=1)
        if l > 0:
            dgate2 = dgate2_below
        else:
            gp = small_gather()
            pend_mix, tok_mix = send_grads(gr, "mix", tag, after=gp)
        scatter_finish(pend_w2, dx, l)
        scatter_finish(pend_mlp, dx, l)

    rows_p = gp.shape[1]
    per = 6 * D // LANES
    dm = lax.slice_in_dim(gp, 0, L * per, axis=1).reshape(NDEV, L, per, LANES)
    dmod_cols = lax.dynamic_slice_in_dim(dm, me * (AW // LANES), AW // LANES, axis=2).reshape(NDEV, L * AW) + tok_mix

    def sum_fn(t, c_, o):
        acc = t[0][0]
        for s_ in range(1, NDEV):
            acc = acc + t[0][s_]
        o[0][...] = acc
        return []

    gsummed = rowwise(sum_fn, [gp], outs=[((rows_p, LANES), F32)], ts=_tile(rows_p, 256, 8), name="sum_smallgrads")[0]
    gsum = dict(zip(small, _unpack(gsummed, shapes)))
    gsum["mla_q_norm_g"] = _shard_cols(gsum["mla_q_norm_g"], me, R // NDEV, 1)
    gsum["mla_kv_norm_g"] = _shard_cols(gsum["mla_kv_norm_g"], me, R // NDEV, 1)
    gsum["conv_w"] = _shard_cols(gsum["conv_w"], me, CB, 2)
    sm_shapes = [W[k].shape for k in small]
    sres = adamw(_pack([W[k] for k in small]), _pack([M[k] for k in small]), _pack([V[k] for k in small]),
                 (_pack([gsum[k] for k in small]) + tok_mix)[None], row0=0, name="adamw_small")
    small_out = {k: vals for k, vals in zip(small, zip(*[_unpack(r_, sm_shapes) for r_ in sres]))}

    g_ada = mm(c_act, dmod_cols, "tn", name="ada_dw", out_blocked=L, tm=1024, tn=768, tk=NDEV, cast=None,
               precision=lax.Precision.HIGHEST)
    ada_out = adamw(ada_w.reshape(L * D, AW), m_ada_w.reshape(L * D, AW), v_ada_w.reshape(L * D, AW),
                    g_ada.reshape(1, L * D, AW), row0=0, name="adamw_ada_w")
    scatter_finish(pend_mix, ada_out[1], 0)
    ada_out = [a.reshape(L, D, AW) for a in ada_out]

    def result(k, which):
        if k == "ada_w":
            return ada_out[which]
        if k in BIG:
            return big_out[k][which].reshape(W[k].shape)
        return small_out[k][which]

    outs = [loss, dx[None]]
    for which in range(4):
        outs += [result(k, which) for k in ORDER]
    return tuple(outs)
```
